```python
import jax, jax.numpy as jnp
from jax import lax
import numpy as np

D_MODEL = 1024
BATCH = 8
SEQ = 2048
DEPTH = 2

D_MIX = D_MODEL
ATTN_DIM = D_MIX // 2
CONV_DIM = D_MIX - ATTN_DIM
HEAD_DIM = 64
N_HEADS = ATTN_DIM // HEAD_DIM
CONV_WIDTH = 31
Q_BLOCK = 128
PLE_DIM = 256
D_IN = 4 * ATTN_DIM + 3 * CONV_DIM
EPS = 1e-6

kernel_name = "hymba_conformer_stickbreaking_ple"


def rms_norm(x, g):
    xf = x.astype(jnp.float32)
    y = xf * lax.rsqrt(jnp.mean(xf * xf, axis=-1, keepdims=True) + EPS)
    return (y * g.astype(jnp.float32)).astype(x.dtype)


def layer_norm(x, g, b):
    xf = x.astype(jnp.float32)
    mu = jnp.mean(xf, axis=-1, keepdims=True)
    xc = xf - mu
    y = xc * lax.rsqrt(jnp.mean(xc * xc, axis=-1, keepdims=True) + EPS)
    return (y * g.astype(jnp.float32) + b.astype(jnp.float32)).astype(x.dtype)


def stick_breaking_attention(q, k, v):
    S = q.shape[1]
    scale = HEAD_DIM ** -0.5
    outs = []
    for blk in range(S // Q_BLOCK):
        q0 = blk * Q_BLOCK
        kend = q0 + Q_BLOCK
        qb = q[:, q0:kend]
        kb = k[:, :kend]
        vb = v[:, :kend]
        z = jnp.einsum('bqhd,bkhd->bhqk', qb, kb).astype(jnp.float32) * scale
        qpos = q0 + jnp.arange(Q_BLOCK)[:, None]
        kpos = jnp.arange(kend)[None, :]
        causal = kpos < qpos
        log_1m_beta = jnp.where(causal, -jax.nn.softplus(z), 0.0)
        suffix = lax.cumsum(log_1m_beta, axis=3, reverse=True) - log_1m_beta
        log_a = jax.nn.log_sigmoid(z) + suffix
        a = jnp.where(causal, jnp.exp(log_a), 0.0)
        outs.append(jnp.einsum('bhqk,bkhd->bqhd', a.astype(v.dtype), vb))
    return jnp.concatenate(outs, axis=1)


def causal_depthwise_conv(x, w, b):
    rhs = w[:, None, :].astype(x.dtype)
    y = lax.conv_general_dilated(
        x, rhs, window_strides=(1,), padding=((CONV_WIDTH - 1, 0),),
        dimension_numbers=('NWC', 'WIO', 'NWC'), feature_group_count=x.shape[-1])
    return y + b.astype(x.dtype)


def _fwd_setup_inputs(seed: int = 0) -> dict:
    key = jax.random.key(seed)
    ks = jax.random.split(key, 16)
    f32 = jnp.float32
    nrm = lambda k, shape, s: jax.random.normal(k, shape, f32) * s
    return {
        "x": nrm(ks[0], (BATCH, SEQ, D_MODEL), 1.0),
        "p": nrm(ks[1], (DEPTH, BATCH, SEQ, PLE_DIM), 1.0),
        "norm_g": 1.0 + nrm(ks[2], (DEPTH, D_MODEL), 0.02),
        "w_in": nrm(ks[3], (DEPTH, D_MODEL, D_IN), D_MODEL ** -0.5),
        "attn_out_g": 1.0 + nrm(ks[4], (DEPTH, HEAD_DIM), 0.02),
        "dw_w": nrm(ks[5], (DEPTH, CONV_WIDTH, CONV_DIM), CONV_WIDTH ** -0.5),
        "dw_b": nrm(ks[6], (DEPTH, CONV_DIM), 0.02),
        "conv_ln_g": 1.0 + nrm(ks[7], (DEPTH, CONV_DIM), 0.02),
        "conv_ln_b": nrm(ks[8], (DEPTH, CONV_DIM), 0.02),
        "w_pw": nrm(ks[9], (DEPTH, CONV_DIM, CONV_DIM), CONV_DIM ** -0.5),
        "conv_out_g": 1.0 + nrm(ks[10], (DEPTH, CONV_DIM), 0.02),
        "w_out": nrm(ks[11], (DEPTH, D_MIX, D_MODEL), D_MIX ** -0.5),
        "ple_norm_g": 1.0 + nrm(ks[12], (DEPTH, D_MODEL), 0.02),
        "w_ple_gate": nrm(ks[13], (DEPTH, D_MODEL, D_MODEL), D_MODEL ** -0.5),
        "w_ple": nrm(ks[14], (DEPTH, PLE_DIM, D_MODEL), PLE_DIM ** -0.5),
        "final_g": 1.0 + nrm(ks[15], (D_MODEL,), 0.02),
    }


def _fwd_reference(x, p, norm_g, w_in, attn_out_g, dw_w, dw_b, conv_ln_g, conv_ln_b,
              w_pw, conv_out_g, w_out, ple_norm_g, w_ple_gate, w_ple, final_g):
    B, S, _ = x.shape
    split_at = np.cumsum([ATTN_DIM, ATTN_DIM, ATTN_DIM, ATTN_DIM,
                          CONV_DIM, CONV_DIM])
    h = x
    for i in range(DEPTH):
        hn = rms_norm(h, norm_g[i])
        u = hn @ w_in[i]
        q, k, v, g_attn, c_val, c_gate, g_conv = jnp.split(u, split_at, axis=-1)

        heads = lambda t: t.reshape(B, S, N_HEADS, HEAD_DIM)
        o = stick_breaking_attention(heads(q), heads(k), heads(v))
        o = rms_norm(o, attn_out_g[i]).reshape(B, S, ATTN_DIM)
        y_attn = o * jax.nn.silu(g_attn)

        c = c_val * jax.nn.sigmoid(c_gate)
        c = causal_depthwise_conv(c, dw_w[i], dw_b[i])
        c = jax.nn.silu(layer_norm(c, conv_ln_g[i], conv_ln_b[i]))
        c = c @ w_pw[i]
        y_conv = rms_norm(c, conv_out_g[i]) * jax.nn.silu(g_conv)

        y = jnp.concatenate([y_attn, y_conv], axis=-1) @ w_out[i]
        h = h + y

        gate = jax.nn.sigmoid(rms_norm(h, ple_norm_g[i]) @ w_ple_gate[i])
        h = h + (p[i].astype(h.dtype) @ w_ple[i]) * gate
    return rms_norm(h, final_g)


import jax as _jax
import jax.numpy as _jnp

TWIN_FORMAT = 'train_step'
FWD_PARAMS = ['x', 'p', 'norm_g', 'w_in', 'attn_out_g', 'dw_w', 'dw_b', 'conv_ln_g', 'conv_ln_b', 'w_pw', 'conv_out_g', 'w_out', 'ple_norm_g', 'w_ple_gate', 'w_ple', 'final_g']
TWIN_WEIGHTS = ['norm_g', 'w_in', 'attn_out_g', 'dw_w', 'dw_b', 'conv_ln_g', 'conv_ln_b', 'w_pw', 'conv_out_g', 'w_out', 'ple_norm_g', 'w_ple_gate', 'w_ple', 'final_g']
TWIN_DIFF_INPUT = 'x'
TWIN_INPUTS = ['x', 'p', 'norm_g', 'w_in', 'attn_out_g', 'dw_w', 'dw_b', 'conv_ln_g', 'conv_ln_b', 'w_pw', 'conv_out_g', 'w_out', 'ple_norm_g', 'w_ple_gate', 'w_ple', 'final_g', 'loss_target', 'm_norm_g', 'm_w_in', 'm_attn_out_g', 'm_dw_w', 'm_dw_b', 'm_conv_ln_g', 'm_conv_ln_b', 'm_w_pw', 'm_conv_out_g', 'm_w_out', 'm_ple_norm_g', 'm_w_ple_gate', 'm_w_ple', 'm_final_g', 'v_norm_g', 'v_w_in', 'v_attn_out_g', 'v_dw_w', 'v_dw_b', 'v_conv_ln_g', 'v_conv_ln_b', 'v_w_pw', 'v_conv_out_g', 'v_w_out', 'v_ple_norm_g', 'v_w_ple_gate', 'v_w_ple', 'v_final_g']
TWIN_OUTPUTS = ['loss', 'grad_x', 'grad_norm_g', 'grad_w_in', 'grad_attn_out_g', 'grad_dw_w', 'grad_dw_b', 'grad_conv_ln_g', 'grad_conv_ln_b', 'grad_w_pw', 'grad_conv_out_g', 'grad_w_out', 'grad_ple_norm_g', 'grad_w_ple_gate', 'grad_w_ple', 'grad_final_g', 'delta_norm_g', 'delta_w_in', 'delta_attn_out_g', 'delta_dw_w', 'delta_dw_b', 'delta_conv_ln_g', 'delta_conv_ln_b', 'delta_w_pw', 'delta_conv_out_g', 'delta_w_out', 'delta_ple_norm_g', 'delta_w_ple_gate', 'delta_w_ple', 'delta_final_g', 'new_m_norm_g', 'new_m_w_in', 'new_m_attn_out_g', 'new_m_dw_w', 'new_m_dw_b', 'new_m_conv_ln_g', 'new_m_conv_ln_b', 'new_m_w_pw', 'new_m_conv_out_g', 'new_m_w_out', 'new_m_ple_norm_g', 'new_m_w_ple_gate', 'new_m_w_ple', 'new_m_final_g', 'new_v_norm_g', 'new_v_w_in', 'new_v_attn_out_g', 'new_v_dw_w', 'new_v_dw_b', 'new_v_conv_ln_g', 'new_v_conv_ln_b', 'new_v_w_pw', 'new_v_conv_out_g', 'new_v_w_out', 'new_v_ple_norm_g', 'new_v_w_ple_gate', 'new_v_w_ple', 'new_v_final_g']
TWIN_LEAF_KINDS = {'loss': 'loss', 'grad_x': 'grad_x', 'grad_norm_g': 'grad_w', 'grad_w_in': 'grad_w', 'grad_attn_out_g': 'grad_w', 'grad_dw_w': 'grad_w', 'grad_dw_b': 'grad_w', 'grad_conv_ln_g': 'grad_w', 'grad_conv_ln_b': 'grad_w', 'grad_w_pw': 'grad_w', 'grad_conv_out_g': 'grad_w', 'grad_w_out': 'grad_w', 'grad_ple_norm_g': 'grad_w', 'grad_w_ple_gate': 'grad_w', 'grad_w_ple': 'grad_w', 'grad_final_g': 'grad_w', 'delta_norm_g': 'delta_w', 'delta_w_in': 'delta_w', 'delta_attn_out_g': 'delta_w', 'delta_dw_w': 'delta_w', 'delta_dw_b': 'delta_w', 'delta_conv_ln_g': 'delta_w', 'delta_conv_ln_b': 'delta_w', 'delta_w_pw': 'delta_w', 'delta_conv_out_g': 'delta_w', 'delta_w_out': 'delta_w', 'delta_ple_norm_g': 'delta_w', 'delta_w_ple_gate': 'delta_w', 'delta_w_ple': 'delta_w', 'delta_final_g': 'delta_w', 'new_m_norm_g': 'new_m', 'new_m_w_in': 'new_m', 'new_m_attn_out_g': 'new_m', 'new_m_dw_w': 'new_m', 'new_m_dw_b': 'new_m', 'new_m_conv_ln_g': 'new_m', 'new_m_conv_ln_b': 'new_m', 'new_m_w_pw': 'new_m', 'new_m_conv_out_g': 'new_m', 'new_m_w_out': 'new_m', 'new_m_ple_norm_g': 'new_m', 'new_m_w_ple_gate': 'new_m', 'new_m_w_ple': 'new_m', 'new_m_final_g': 'new_m', 'new_v_norm_g': 'new_v', 'new_v_w_in': 'new_v', 'new_v_attn_out_g': 'new_v', 'new_v_dw_w': 'new_v', 'new_v_dw_b': 'new_v', 'new_v_conv_ln_g': 'new_v', 'new_v_conv_ln_b': 'new_v', 'new_v_w_pw': 'new_v', 'new_v_conv_out_g': 'new_v', 'new_v_w_out': 'new_v', 'new_v_ple_norm_g': 'new_v', 'new_v_w_ple_gate': 'new_v', 'new_v_w_ple': 'new_v', 'new_v_final_g': 'new_v'}


def _forward(args):
    return _fwd_reference(*[args[k] for k in FWD_PARAMS])


def _output_shape():
    out = _jax.eval_shape(lambda: _forward(_fwd_setup_inputs(0)))
    return out.shape, out.dtype

N_MICROBATCH = 1
ADAM_LR = 0.001
ADAM_B1 = 0.9
ADAM_B2 = 0.999
ADAM_EPS = 1e-08
ADAM_WD = 0.01
ADAM_STEP = 10
PER_EXAMPLE_BATCH_AXIS = {'x': 0, 'p': 1, 'loss_target': 0}
SHARED_INPUTS = []
_WEIGHT_DTYPES = {'norm_g': _jnp.float32, 'w_in': _jnp.float32, 'attn_out_g': _jnp.float32, 'dw_w': _jnp.float32, 'dw_b': _jnp.float32, 'conv_ln_g': _jnp.float32, 'conv_ln_b': _jnp.float32, 'w_pw': _jnp.float32, 'conv_out_g': _jnp.float32, 'w_out': _jnp.float32, 'ple_norm_g': _jnp.float32, 'w_ple_gate': _jnp.float32, 'w_ple': _jnp.float32, 'final_g': _jnp.float32}
MOMENT_SCALE = {'norm_g': 8.886435e-02, 'w_in': 4.703762e-02, 'attn_out_g': 1.519977e-01, 'dw_w': 5.935206e-02, 'dw_b': 1.297985e-01, 'conv_ln_g': 6.917068e-02, 'conv_ln_b': 6.091440e-02, 'w_pw': 5.719834e-02, 'conv_out_g': 5.951889e-02, 'w_out': 5.643359e-02, 'ple_norm_g': 1.983002e-02, 'w_ple_gate': 1.981999e-02, 'w_ple': 5.068044e-02, 'final_g': 1.597371e+01}


def _to_microbatches(a, axis):
    t = _jnp.moveaxis(a, axis, 0)
    t = t.reshape((N_MICROBATCH, t.shape[0] // N_MICROBATCH) + t.shape[1:])
    return _jnp.moveaxis(t, 1, axis + 1)


def setup_inputs(seed: int = 0) -> dict:
    inp = _fwd_setup_inputs(seed)
    key = _jax.random.fold_in(_jax.random.key(seed), 7919)
    shape, _ = _output_shape()
    out = dict(inp)
    out["loss_target"] = _jax.random.normal(_jax.random.fold_in(key, 0), shape, _jnp.float32)
    for i, name in enumerate(TWIN_WEIGHTS):
        w = inp[name].astype(_jnp.float32)
        if MOMENT_SCALE is None:
            s = _jnp.sqrt(_jnp.mean(_jnp.square(w)) + 1e-30)
        else:
            s = MOMENT_SCALE[name]
        km, kv = _jax.random.split(_jax.random.fold_in(key, i + 1))
        out[name] = w
        out["m_" + name] = s * _jax.random.normal(km, w.shape, _jnp.float32)
        out["v_" + name] = (s * s) * _jax.random.uniform(kv, w.shape, _jnp.float32, 0.5, 1.5)
    if N_MICROBATCH > 1:
        for name, axis in PER_EXAMPLE_BATCH_AXIS.items():
            out[name] = _to_microbatches(out[name], axis)
    return {'x': out['x'], 'p': out['p'], 'norm_g': out['norm_g'], 'w_in': out['w_in'], 'attn_out_g': out['attn_out_g'], 'dw_w': out['dw_w'], 'dw_b': out['dw_b'], 'conv_ln_g': out['conv_ln_g'], 'conv_ln_b': out['conv_ln_b'], 'w_pw': out['w_pw'], 'conv_out_g': out['conv_out_g'], 'w_out': out['w_out'], 'ple_norm_g': out['ple_norm_g'], 'w_ple_gate': out['w_ple_gate'], 'w_ple': out['w_ple'], 'final_g': out['final_g'], 'loss_target': out['loss_target'], 'm_norm_g': out['m_norm_g'], 'm_w_in': out['m_w_in'], 'm_attn_out_g': out['m_attn_out_g'], 'm_dw_w': out['m_dw_w'], 'm_dw_b': out['m_dw_b'], 'm_conv_ln_g': out['m_conv_ln_g'], 'm_conv_ln_b': out['m_conv_ln_b'], 'm_w_pw': out['m_w_pw'], 'm_conv_out_g': out['m_conv_out_g'], 'm_w_out': out['m_w_out'], 'm_ple_norm_g': out['m_ple_norm_g'], 'm_w_ple_gate': out['m_w_ple_gate'], 'm_w_ple': out['m_w_ple'], 'm_final_g': out['m_final_g'], 'v_norm_g': out['v_norm_g'], 'v_w_in': out['v_w_in'], 'v_attn_out_g': out['v_attn_out_g'], 'v_dw_w': out['v_dw_w'], 'v_dw_b': out['v_dw_b'], 'v_conv_ln_g': out['v_conv_ln_g'], 'v_conv_ln_b': out['v_conv_ln_b'], 'v_w_pw': out['v_w_pw'], 'v_conv_out_g': out['v_conv_out_g'], 'v_w_out': out['v_w_out'], 'v_ple_norm_g': out['v_ple_norm_g'], 'v_w_ple_gate': out['v_w_ple_gate'], 'v_w_ple': out['v_w_ple'], 'v_final_g': out['v_final_g']}


def _loss(weights, diff, rest, loss_target):
    with _jax.named_scope("forward"):
        args = {**rest, TWIN_DIFF_INPUT: diff, **{k: w.astype(_WEIGHT_DTYPES[k]) for k, w in weights.items()}}
        y = _forward(args)
    with _jax.named_scope("loss_head"):
        err = _jnp.square(y.astype(_jnp.float32) - loss_target)
        return 0.5 * _jnp.sum(_jnp.mean(err, axis=-1)) if err.ndim else 0.5 * err


def _adamw(w, g, m, v):
    m = ADAM_B1 * m + (1.0 - ADAM_B1) * g
    v = ADAM_B2 * v + (1.0 - ADAM_B2) * _jnp.square(g)
    m_hat = m / (1.0 - ADAM_B1 ** ADAM_STEP)
    v_hat = v / (1.0 - ADAM_B2 ** ADAM_STEP)
    delta = -ADAM_LR * (m_hat / (_jnp.sqrt(v_hat) + ADAM_EPS) + ADAM_WD * w)
    return delta, m, v


def reference(x, p, norm_g, w_in, attn_out_g, dw_w, dw_b, conv_ln_g, conv_ln_b, w_pw, conv_out_g, w_out, ple_norm_g, w_ple_gate, w_ple, final_g, loss_target, m_norm_g, m_w_in, m_attn_out_g, m_dw_w, m_dw_b, m_conv_ln_g, m_conv_ln_b, m_w_pw, m_conv_out_g, m_w_out, m_ple_norm_g, m_w_ple_gate, m_w_ple, m_final_g, v_norm_g, v_w_in, v_attn_out_g, v_dw_w, v_dw_b, v_conv_ln_g, v_conv_ln_b, v_w_pw, v_conv_out_g, v_w_out, v_ple_norm_g, v_w_ple_gate, v_w_ple, v_final_g):
    given = dict(x=x, p=p, norm_g=norm_g, w_in=w_in, attn_out_g=attn_out_g, dw_w=dw_w, dw_b=dw_b, conv_ln_g=conv_ln_g, conv_ln_b=conv_ln_b, w_pw=w_pw, conv_out_g=conv_out_g, w_out=w_out, ple_norm_g=ple_norm_g, w_ple_gate=w_ple_gate, w_ple=w_ple, final_g=final_g, loss_target=loss_target, m_norm_g=m_norm_g, m_w_in=m_w_in, m_attn_out_g=m_attn_out_g, m_dw_w=m_dw_w, m_dw_b=m_dw_b, m_conv_ln_g=m_conv_ln_g, m_conv_ln_b=m_conv_ln_b, m_w_pw=m_w_pw, m_conv_out_g=m_conv_out_g, m_w_out=m_w_out, m_ple_norm_g=m_ple_norm_g, m_w_ple_gate=m_w_ple_gate, m_w_ple=m_w_ple, m_final_g=m_final_g, v_norm_g=v_norm_g, v_w_in=v_w_in, v_attn_out_g=v_attn_out_g, v_dw_w=v_dw_w, v_dw_b=v_dw_b, v_conv_ln_g=v_conv_ln_g, v_conv_ln_b=v_conv_ln_b, v_w_pw=v_w_pw, v_conv_out_g=v_conv_out_g, v_w_out=v_w_out, v_ple_norm_g=v_ple_norm_g, v_w_ple_gate=v_w_ple_gate, v_w_ple=v_w_ple, v_final_g=v_final_g)
    weights = {n: given[n] for n in TWIN_WEIGHTS}
    shared = {n: given[n] for n in SHARED_INPUTS}
    per_example = {n: given[n] for n in ['x', 'p']}
    grad_fn = _jax.value_and_grad(_loss, argnums=(0, 1))

    def one_microbatch(ex, loss_target):
        ex = dict(ex)
        diff = ex.pop(TWIN_DIFF_INPUT)
        return grad_fn(weights, diff, {**shared, **ex}, loss_target)

    if N_MICROBATCH == 1:
        loss, (grad_w, grad_x) = one_microbatch(per_example, given["loss_target"])
    else:
        def body(carry, xs):
            loss_sum, grad_sum = carry
            l_k, (gw_k, gx_k) = one_microbatch(xs[0], xs[1])
            with _jax.named_scope("update"):
                return (loss_sum + l_k, _jax.tree.map(_jnp.add, grad_sum, gw_k)), gx_k

        init = (_jnp.zeros((), _jnp.float32), _jax.tree.map(_jnp.zeros_like, weights))
        (loss, grad_w), grad_x = _jax.lax.scan(body, init, (per_example, given["loss_target"]))
    with _jax.named_scope("update"):
        delta_w, new_m, new_v = {}, {}, {}
        for n in TWIN_WEIGHTS:
            delta_w[n], new_m[n], new_v[n] = _adamw(weights[n], grad_w[n], given["m_" + n], given["v_" + n])
    return (loss, grad_x, *[grad_w[n] for n in TWIN_WEIGHTS], *[delta_w[n] for n in TWIN_WEIGHTS],
            *[new_m[n] for n in TWIN_WEIGHTS], *[new_v[n] for n in TWIN_WEIGHTS])
```

```python
import functools

import jax
import jax.numpy as jnp
from jax import lax
from jax.experimental import pallas as pl
from jax.experimental.pallas import tpu as pltpu

F32 = jnp.float32
BF16 = jnp.bfloat16

T = 2048
D = 1024
DIN = 3584
NCHIP = 4
SHW = DIN // NCHIP
AD = 512
CD = 512
DH = 64
CW = 31
CWP = 32
PLE = 256
DEPTH = 2
EPS = 1e-6
QB = 128
LANE = 128

ADAM_LR = 0.001
ADAM_B1 = 0.9
ADAM_B2 = 0.999
ADAM_EPS = 1e-08
ADAM_WD = 0.01
ADAM_STEP = 10

PK_ROWS = (896, 256, 256, 64, 64, 4)
PR = 1568
SMALL_ROWS = 80

VMEM_BIG = 56 * 1024 * 1024
MESH = pl.DeviceIdType.MESH


def _cp(sem=None, vmem=None):
    kw = {}
    if sem is not None:
        kw["dimension_semantics"] = sem
    if vmem is not None:
        kw["vmem_limit_bytes"] = vmem
    return pltpu.CompilerParams(**kw)


def _dot(a, b):
    return jnp.dot(a, b, preferred_element_type=F32)


def _dot_nt(a, b):
    return lax.dot_general(a, b, (((1,), (1,)), ((), ())), preferred_element_type=F32)


def _dot_tn(a, b):
    return lax.dot_general(a, b, (((0,), (0,)), ((), ())), preferred_element_type=F32)


def _dot2(x, m):
    hi = x.astype(BF16)
    lo = (x - hi.astype(F32)).astype(BF16)
    return _dot(hi, m) + _dot(lo, m)


def _sig(x):
    return 1.0 / (1.0 + jnp.exp(-x))


def _softplus(z):
    return jnp.maximum(z, 0.0) + jnp.log(1.0 + jnp.exp(-jnp.abs(z)))


def _rstd(x):
    return lax.rsqrt(jnp.mean(x * x, axis=-1, keepdims=True) + EPS)


def _rms_bwd(dy, x, r, g):
    dn = dy * g
    return r * dn - x * (r * r * r) * jnp.mean(dn * x, axis=-1, keepdims=True)


def _rms_inproj(h, g, w):
    tm = 512

    def body(h_ref, g_ref, w_ref, u_ref, hn_ref, hn_s):
        @pl.when(pl.program_id(1) == 0)
        def _():
            hh = h_ref[...]
            hn = (hh * _rstd(hh) * g_ref[...]).astype(BF16)
            hn_s[...] = hn
            hn_ref[...] = hn
        u_ref[...] = _dot(hn_s[...], w_ref[0])

    return pl.pallas_call(
        body, name="rms_inproj", grid=(T // tm, NCHIP),
        in_specs=[pl.BlockSpec((tm, D), lambda i, k: (i, 0)),
                  pl.BlockSpec((1, D), lambda i, k: (0, 0)),
                  pl.BlockSpec((1, D, SHW), lambda i, k: (k, 0, 0))],
        out_specs=[pl.BlockSpec((tm, SHW), lambda i, k: (i, k)),
                   pl.BlockSpec((tm, D), lambda i, k: (i, 0))],
        out_shape=[jax.ShapeDtypeStruct((T, DIN), F32), jax.ShapeDtypeStruct((T, D), BF16)],
        scratch_shapes=[pltpu.VMEM((tm, D), BF16)],
        compiler_params=_cp(("arbitrary", "arbitrary")),
    )(h, g, w)


def _tri_masks():
    row = lax.broadcasted_iota(jnp.int32, (QB, QB), 0)
    col = lax.broadcasted_iota(jnp.int32, (QB, QB), 1)
    return row, col


def _attn_fwd(u, ag):
    def body(q_ref, k_ref, v_ref, g_ref, ag_ref, o_ref, y_ref, tot_ref, kb_s, vb_s):
        qi = pl.program_id(1)

        @pl.when(qi == 0)
        def _():
            kb_s[...] = k_ref[...].astype(BF16)
            vb_s[...] = v_ref[...].astype(BF16)

        row, col = _tri_masks()
        upper = (row > col).astype(BF16)
        q = q_ref[...]
        g = g_ref[...]
        ag_v = ag_ref[...]
        o_parts, y_parts, t_parts = [], [], []
        for a in range(2):
            sl = slice(a * DH, (a + 1) * DH)
            qa = (q[:, sl] * 0.125).astype(BF16)

            def step(i, carry, sl=sl, qa=qa):
                run, acc = carry
                kb = qi - i
                k0 = pl.multiple_of(kb * QB, QB)
                kk = kb_s[pl.ds(k0, QB), sl]
                vv = vb_s[pl.ds(k0, QB), sl]
                z = _dot_nt(qa, kk)
                causal = (col + k0) < (row + qi * QB)
                sp = _softplus(z)
                lm = jnp.where(causal, -sp, 0.0)
                logit = (z - sp) + _dot2(lm, upper) + run
                att = jnp.where(causal, jnp.exp(logit), 0.0)
                acc = acc + _dot(att.astype(BF16), vv)
                run = run + jnp.sum(lm, axis=1, keepdims=True)
                return run, acc

            run, acc = lax.fori_loop(
                0, qi + 1, step, (jnp.zeros((QB, 1), F32), jnp.zeros((QB, DH), F32)))
            ga = g[:, sl]
            ya = acc * _rstd(acc) * ag_v * (ga * _sig(ga))
            o_parts.append(acc)
            y_parts.append(ya)
            t_parts.append(jnp.broadcast_to(run, (QB, DH)))
        o_ref[...] = jnp.concatenate(o_parts, axis=1)
        y_ref[...] = jnp.concatenate(y_parts, axis=1).astype(BF16)
        tot_ref[0] = jnp.concatenate(t_parts, axis=1)

    return pl.pallas_call(
        body, name="attn_fwd", grid=(AD // LANE, T // QB),
        in_specs=[pl.BlockSpec((QB, LANE), lambda hp, qi: (qi, hp)),
                  pl.BlockSpec((T, LANE), lambda hp, qi: (0, 4 + hp)),
                  pl.BlockSpec((T, LANE), lambda hp, qi: (0, 8 + hp)),
                  pl.BlockSpec((QB, LANE), lambda hp, qi: (qi, 12 + hp)),
                  pl.BlockSpec((1, DH), lambda hp, qi: (0, 0))],
        out_specs=[pl.BlockSpec((QB, LANE), lambda hp, qi: (qi, hp)),
                   pl.BlockSpec((QB, LANE), lambda hp, qi: (qi, hp)),
                   pl.BlockSpec((1, QB, LANE), lambda hp, qi: (hp, qi, 0))],
        out_shape=[jax.ShapeDtypeStruct((T, AD), F32), jax.ShapeDtypeStruct((T, AD), BF16),
                   jax.ShapeDtypeStruct((AD // LANE, T, LANE), F32)],
        scratch_shapes=[pltpu.VMEM((T, LANE), BF16), pltpu.VMEM((T, LANE), BF16)],
        compiler_params=_cp(("arbitrary", "arbitrary")),
    )(u, u, u, u, ag)


def _glu_conv(u, dw, db):
    tr = 256

    def body(cv_ref, cg_ref, w_ref, b_ref, c1_ref, pad_s):
        pad_s[pl.ds(0, CWP), :] = jnp.zeros((CWP, LANE), F32)
        pad_s[pl.ds(CWP, T), :] = cv_ref[...] * _sig(cg_ref[...])
        wv = w_ref[0]
        bias = b_ref[...]

        def tile(i, carry):
            r0 = pl.multiple_of(i * tr, tr)
            acc = jnp.zeros((tr, LANE), F32) + bias
            for w in range(CW):
                acc = acc + pad_s[pl.ds(r0 + (CWP - CW + 1) + w, tr), :] * wv[w:w + 1, :]
            c1_ref[pl.ds(r0, tr), :] = acc
            return carry

        lax.fori_loop(0, T // tr, tile, 0)

    return pl.pallas_call(
        body, name="glu_conv", grid=(CD // LANE,),
        in_specs=[pl.BlockSpec((T, LANE), lambda cb: (0, 16 + cb)),
                  pl.BlockSpec((T, LANE), lambda cb: (0, 20 + cb)),
                  pl.BlockSpec((1, CWP, LANE), lambda cb: (cb, 0, 0)),
                  pl.BlockSpec((1, LANE), lambda cb: (0, cb))],
        out_specs=pl.BlockSpec((T, LANE), lambda cb: (0, cb)),
        out_shape=jax.ShapeDtypeStruct((T, CD), F32),
        scratch_shapes=[pltpu.VMEM((T + CWP, LANE), F32)],
        compiler_params=_cp(("arbitrary",)),
    )(u, u, dw, db)


def _ln_silu(c1, lg, lb):
    mu = jnp.mean(c1, axis=-1, keepdims=True)
    xc = c1 - mu
    rs = lax.rsqrt(jnp.mean(xc * xc, axis=-1, keepdims=True) + EPS)
    xh = xc * rs
    ln = xh * lg + lb
    s = _sig(ln)
    return xh, rs, ln, s


def _layer_tail(c1, u, ya, h, p, lg, lb, wpw, cg, wout, pg, wgate, wple):
    tm = 256

    def body(c1_ref, gc_ref, ya_ref, h_ref, p_ref, lg_ref, lb_ref, wpw_ref, cg_ref, wout_ref,
             pg_ref, wgate_ref, wple_ref, c3_ref, yc_ref, h1_ref, gate_ref, pe_ref, h2_ref):
        _, _, ln, s = _ln_silu(c1_ref[...], lg_ref[...], lb_ref[...])
        c2 = (ln * s).astype(BF16)
        c3 = _dot(c2, wpw_ref[...])
        gc = gc_ref[...]
        yc = (c3 * _rstd(c3) * cg_ref[...] * (gc * _sig(gc))).astype(BF16)
        c3_ref[...] = c3
        yc_ref[...] = yc
        y = _dot(ya_ref[...], wout_ref[pl.ds(0, AD), :]) + _dot(yc, wout_ref[pl.ds(AD, CD), :])
        h1 = h_ref[...] + y
        hn2 = (h1 * _rstd(h1) * pg_ref[...]).astype(BF16)
        gate = _sig(_dot(hn2, wgate_ref[...]))
        pb = p_ref[...].astype(BF16)
        pe = jnp.concatenate([_dot(pb, wple_ref[k]) for k in range(NCHIP)], axis=1)
        h1_ref[...] = h1
        gate_ref[...] = gate
        pe_ref[...] = pe
        h2_ref[...] = h1 + pe * gate

    row = lambda w: pl.BlockSpec((tm, w), lambda i: (i, 0))
    full = lambda *s: pl.BlockSpec(s, lambda i: (0,) * len(s))
    return pl.pallas_call(
        body, name="layer_tail", grid=(T // tm,),
        in_specs=[row(CD), pl.BlockSpec((tm, CD), lambda i: (i, 6)), row(AD), row(D), row(PLE),
                  full(1, CD), full(1, CD), full(CD, CD), full(1, CD), full(D, D),
                  full(1, D), full(D, D), full(NCHIP, PLE, PLE)],
        out_specs=[row(CD), row(CD), row(D), row(D), row(D), row(D)],
        out_shape=[jax.ShapeDtypeStruct((T, CD), F32), jax.ShapeDtypeStruct((T, CD), BF16)]
        + [jax.ShapeDtypeStruct((T, D), F32)] * 4,
        compiler_params=_cp(("arbitrary",), VMEM_BIG),
    )(c1, u, ya, h, p, lg, lb, wpw, cg, wout, pg, wgate, wple)


def _loss_head(h, tgt, fg):
    tm = 256

    def body(h_ref, t_ref, g_ref, loss_ref, dh_ref, dg_ref):
        @pl.when(pl.program_id(0) == 0)
        def _():
            loss_ref[...] = jnp.zeros_like(loss_ref)
            dg_ref[...] = jnp.zeros_like(dg_ref)
        hh = h_ref[...]
        g = g_ref[...]
        r = _rstd(hh)
        e = hh * r * g - t_ref[...]
        loss_ref[...] += 0.5 * jnp.sum(jnp.mean(e * e, axis=-1, keepdims=True))
        dy = e * (1.0 / D)
        dg_ref[...] += jnp.sum(dy * hh * r, axis=0, keepdims=True)
        dh_ref[...] = _rms_bwd(dy, hh, r, g)

    return pl.pallas_call(
        body, name="loss_head", grid=(T // tm,),
        in_specs=[pl.BlockSpec((tm, D), lambda i: (i, 0)), pl.BlockSpec((tm, D), lambda i: (i, 0)),
                  pl.BlockSpec((1, D), lambda i: (0, 0))],
        out_specs=[pl.BlockSpec((8, LANE), lambda i: (0, 0)), pl.BlockSpec((tm, D), lambda i: (i, 0)),
                   pl.BlockSpec((1, D), lambda i: (0, 0))],
        out_shape=[jax.ShapeDtypeStruct((8, LANE), F32), jax.ShapeDtypeStruct((T, D), F32),
                   jax.ShapeDtypeStruct((1, D), F32)],
        compiler_params=_cp(("arbitrary",)),
    )(h, tgt, fg)


def _ple_out_bwd(dh2, h1, gate, pe, p, ya, yc, pg, wgate, wout):
    tm = 256

    def body(dh2_ref, h1_ref, gate_ref, pe_ref, p_ref, ya_ref, yc_ref, pg_ref, wgate_ref, wout_ref,
             dh1_ref, dy_ref, dwg_ref, dwp_ref, dwo_ref, dpg_ref):
        @pl.when(pl.program_id(0) == 0)
        def _():
            dwg_ref[...] = jnp.zeros_like(dwg_ref)
            dwp_ref[...] = jnp.zeros_like(dwp_ref)
            dwo_ref[...] = jnp.zeros_like(dwo_ref)
            dpg_ref[...] = jnp.zeros_like(dpg_ref)
        dh2 = dh2_ref[...]
        h1 = h1_ref[...]
        gate = gate_ref[...]
        pg = pg_ref[...]
        dpe = (dh2 * gate).astype(BF16)
        dgp = (dh2 * pe_ref[...] * gate * (1.0 - gate)).astype(BF16)
        r = _rstd(h1)
        hn = h1 * r
        dwg_ref[...] += _dot_tn((hn * pg).astype(BF16), dgp)
        dhn2 = _dot_nt(dgp, wgate_ref[...])
        dpg_ref[...] += jnp.sum(dhn2 * hn, axis=0, keepdims=True)
        dh1 = dh2 + _rms_bwd(dhn2, h1, r, pg)
        pb = p_ref[...].astype(BF16)
        for k in range(NCHIP):
            dwp_ref[k] += _dot_tn(pb, dpe[:, k * PLE:(k + 1) * PLE])
        dh1b = dh1.astype(BF16)
        dy_ref[...] = _dot_nt(dh1b, wout_ref[...])
        dwo_ref[pl.ds(0, AD), :] += _dot_tn(ya_ref[...], dh1b)
        dwo_ref[pl.ds(AD, CD), :] += _dot_tn(yc_ref[...], dh1b)
        dh1_ref[...] = dh1

    row = lambda w: pl.BlockSpec((tm, w), lambda i: (i, 0))
    full = lambda *s: pl.BlockSpec(s, lambda i: (0,) * len(s))
    return pl.pallas_call(
        body, name="ple_out_bwd", grid=(T // tm,),
        in_specs=[row(D), row(D), row(D), row(D), row(PLE), row(AD), row(CD),
                  full(1, D), full(D, D), full(D, D)],
        out_specs=[row(D), row(D), full(D, D), full(NCHIP, PLE, PLE), full(D, D), full(1, D)],
        out_shape=[jax.ShapeDtypeStruct((T, D), F32), jax.ShapeDtypeStruct((T, D), F32),
                   jax.ShapeDtypeStruct((D, D), F32), jax.ShapeDtypeStruct((NCHIP, PLE, PLE), F32),
                   jax.ShapeDtypeStruct((D, D), F32), jax.ShapeDtypeStruct((1, D), F32)],
        compiler_params=_cp(("arbitrary",), VMEM_BIG),
    )(dh2, h1, gate, pe, p, ya, yc, pg, wgate, wout)


def _branch_bwd(dy, o, u, c1, c3, ag, lg, lb, wpw, cg, seg):
    tm = 256

    def body(dya_ref, dyc_ref, o_ref, ga_ref, gc_ref, c1_ref, c3_ref, ag_ref, lg_ref, lb_ref, wpw_ref,
             cg_ref, seg_ref, do_ref, dga_ref, dgc_ref, dc1_ref, dwpw_ref, dag_ref, dcg_ref, dlg_ref, dlb_ref):
        @pl.when(pl.program_id(0) == 0)
        def _():
            for r_ in (dwpw_ref, dag_ref, dcg_ref, dlg_ref, dlb_ref):
                r_[...] = jnp.zeros_like(r_)
        dya = dya_ref[...]
        o = o_ref[...]
        ga = ga_ref[...]
        ag_v = ag_ref[...]
        seg_m = seg_ref[...]
        r = lax.rsqrt(_dot2(o * o, seg_m) * (1.0 / DH) + EPS)
        onr = o * r
        sg = _sig(ga)
        dga_ref[...] = dya * (onr * ag_v) * (sg * (1.0 + ga * (1.0 - sg)))
        don = dya * (ga * sg)
        dag_ref[...] += jnp.sum(don * onr, axis=0, keepdims=True)
        dn = don * ag_v
        do_ref[...] = r * dn - o * (r * r * r) * (_dot2(dn * o, seg_m) * (1.0 / DH))
        dyc = dyc_ref[...]
        c3 = c3_ref[...]
        gc = gc_ref[...]
        cg_v = cg_ref[...]
        r3 = _rstd(c3)
        cn = c3 * r3
        sc = _sig(gc)
        dgc_ref[...] = dyc * (cn * cg_v) * (sc * (1.0 + gc * (1.0 - sc)))
        dcn = dyc * (gc * sc)
        dcg_ref[...] += jnp.sum(dcn * cn, axis=0, keepdims=True)
        dc3 = _rms_bwd(dcn, c3, r3, cg_v).astype(BF16)
        lg_v = lg_ref[...]
        xh, rs, ln, s = _ln_silu(c1_ref[...], lg_v, lb_ref[...])
        c2 = (ln * s).astype(BF16)
        dwpw_ref[...] += _dot_tn(c2, dc3)
        dc2 = _dot_nt(dc3, wpw_ref[...])
        dln = dc2 * (s * (1.0 + ln * (1.0 - s)))
        dlb_ref[...] += jnp.sum(dln, axis=0, keepdims=True)
        dlg_ref[...] += jnp.sum(dln * xh, axis=0, keepdims=True)
        dxh = dln * lg_v
        dc1_ref[...] = rs * (dxh - jnp.mean(dxh, axis=-1, keepdims=True)
                             - xh * jnp.mean(dxh * xh, axis=-1, keepdims=True))

    half = lambda j: pl.BlockSpec((tm, 512), lambda i: (i, j))
    full = lambda *s: pl.BlockSpec(s, lambda i: (0,) * len(s))
    vec = jax.ShapeDtypeStruct((1, 512), F32)
    act = jax.ShapeDtypeStruct((T, 512), F32)
    return pl.pallas_call(
        body, name="branch_bwd", grid=(T // tm,),
        in_specs=[half(0), half(1), half(0), half(3), half(6), half(0), half(0),
                  full(1, AD), full(1, CD), full(1, CD), full(CD, CD), full(1, CD), full(AD, AD)],
        out_specs=[half(0), half(0), half(0), half(0), full(CD, CD), full(1, 512), full(1, 512),
                   full(1, 512), full(1, 512)],
        out_shape=[act, act, act, act, jax.ShapeDtypeStruct((CD, CD), F32), vec, vec, vec, vec],
        compiler_params=_cp(("arbitrary",), VMEM_BIG),
    )(dy, dy, o, u, u, c1, c3, ag, lg, lb, wpw, cg, seg)


def _conv_bwd(dc1, u, dw):
    tr = 256
    off = CWP - CW + 1

    def body(d_ref, cv_ref, cg_ref, w_ref, dcv_ref, dcg_ref, ddw_ref, ddb_ref, padc_s, padd_s, acc_s):
        cv = cv_ref[...]
        sg = _sig(cg_ref[...])
        padc_s[pl.ds(0, CWP), :] = jnp.zeros((CWP, LANE), F32)
        padc_s[pl.ds(CWP, T), :] = cv * sg
        padd_s[pl.ds(0, T), :] = d_ref[...]
        padd_s[pl.ds(T, CWP), :] = jnp.zeros((CWP, LANE), F32)
        acc_s[...] = jnp.zeros_like(acc_s)
        wv = w_ref[0]

        def tile(i, carry):
            r0 = pl.multiple_of(i * tr, tr)
            dt = padd_s[pl.ds(r0, tr), :]
            dc0 = jnp.zeros((tr, LANE), F32)
            for w in range(CW):
                dc0 = dc0 + padd_s[pl.ds(r0 + (CW - 1) - w, tr), :] * wv[w:w + 1, :]
                prod = dt * padc_s[pl.ds(r0 + off + w, tr), :]
                acc_s[w] += jnp.sum(prod.reshape(tr // 8, 8, LANE), axis=0)
            cvt = cv_ref[pl.ds(r0, tr), :]
            sgt = _sig(cg_ref[pl.ds(r0, tr), :])
            dcv_ref[pl.ds(r0, tr), :] = dc0 * sgt
            dcg_ref[pl.ds(r0, tr), :] = dc0 * cvt * sgt * (1.0 - sgt)
            return carry

        lax.fori_loop(0, T // tr, tile, 0)
        ddw_ref[0] = jnp.sum(acc_s[...], axis=1)
        ddb_ref[...] = jnp.sum(d_ref[...], axis=0, keepdims=True)

    col = lambda j: pl.BlockSpec((T, LANE), lambda cb: (0, j + cb))
    return pl.pallas_call(
        body, name="conv_bwd", grid=(CD // LANE,),
        in_specs=[col(0), col(16), col(20), pl.BlockSpec((1, CWP, LANE), lambda cb: (cb, 0, 0))],
        out_specs=[col(0), col(0), pl.BlockSpec((1, CWP, LANE), lambda cb: (cb, 0, 0)),
                   pl.BlockSpec((1, LANE), lambda cb: (0, cb))],
        out_shape=[jax.ShapeDtypeStruct((T, CD), F32), jax.ShapeDtypeStruct((T, CD), F32),
                   jax.ShapeDtypeStruct((NCHIP, CWP, LANE), F32), jax.ShapeDtypeStruct((1, CD), F32)],
        scratch_shapes=[pltpu.VMEM((T + CWP, LANE), F32), pltpu.VMEM((T + CWP, LANE), F32),
                        pltpu.VMEM((CWP, 8, LANE), F32)],
        compiler_params=_cp(("arbitrary",)),
    )(dc1, u, u, dw)


def _attn_bwd(u, do, tot):
    def body(q_ref, k_ref, v_ref, do_ref, tot_ref, dq_ref, dk_ref, dv_ref, kb_s, vb_s):
        qi = pl.program_id(1)

        @pl.when(qi == 0)
        def _():
            kb_s[...] = k_ref[...].astype(BF16)
            vb_s[...] = v_ref[...].astype(BF16)
            dk_ref[...] = jnp.zeros_like(dk_ref)
            dv_ref[...] = jnp.zeros_like(dv_ref)

        row, col = _tri_masks()
        upper = (row > col).astype(BF16)
        lower = (row < col).astype(BF16)
        q = q_ref[...]
        dov = do_ref[...]
        totv = tot_ref[0]
        dq_parts = []
        for a in range(2):
            sl = slice(a * DH, (a + 1) * DH)
            qu = q[:, sl].astype(BF16)
            qa = (q[:, sl] * 0.125).astype(BF16)
            doa = dov[:, sl].astype(BF16)
            total = totv[:, a * DH:a * DH + 1]

            def step(kb, carry, sl=sl, qu=qu, qa=qa, doa=doa, total=total):
                lm_left, dl_left, dq = carry
                k0 = pl.multiple_of(kb * QB, QB)
                kk = kb_s[pl.ds(k0, QB), sl]
                vv = vb_s[pl.ds(k0, QB), sl]
                z = _dot_nt(qa, kk)
                causal = (col + k0) < (row + qi * QB)
                sp = _softplus(z)
                lm = jnp.where(causal, -sp, 0.0)
                lm_incl = lm_left + jnp.sum(lm, axis=1, keepdims=True)
                logit = (z - sp) + _dot2(lm, upper) + (total - lm_incl)
                att = jnp.where(causal, jnp.exp(logit), 0.0)
                dl = att * _dot_nt(doa, vv)
                dv_ref[pl.ds(k0, QB), sl] += _dot_tn(att.astype(BF16), doa)
                prefix = dl_left + _dot2(dl, lower)
                beta = jnp.exp(z - sp)
                dz = jnp.where(causal, (1.0 - beta) * dl - beta * prefix, 0.0)
                dzs = (dz * 0.125).astype(BF16)
                dq = dq + _dot(dzs, kk)
                dk_ref[pl.ds(k0, QB), sl] += _dot_tn(dzs, qu)
                return lm_incl, dl_left + jnp.sum(dl, axis=1, keepdims=True), dq

            zero = jnp.zeros((QB, 1), F32)
            _, _, dq = lax.fori_loop(0, qi + 1, step, (zero, zero, jnp.zeros((QB, DH), F32)))
            dq_parts.append(dq)
        dq_ref[...] = jnp.concatenate(dq_parts, axis=1)

    return pl.pallas_call(
        body, name="attn_bwd", grid=(AD // LANE, T // QB),
        in_specs=[pl.BlockSpec((QB, LANE), lambda hp, qi: (qi, hp)),
                  pl.BlockSpec((T, LANE), lambda hp, qi: (0, 4 + hp)),
                  pl.BlockSpec((T, LANE), lambda hp, qi: (0, 8 + hp)),
                  pl.BlockSpec((QB, LANE), lambda hp, qi: (qi, hp)),
                  pl.BlockSpec((1, QB, LANE), lambda hp, qi: (hp, qi, 0))],
        out_specs=[pl.BlockSpec((QB, LANE), lambda hp, qi: (qi, hp)),
                   pl.BlockSpec((T, LANE), lambda hp, qi: (0, hp)),
                   pl.BlockSpec((T, LANE), lambda hp, qi: (0, hp))],
        out_shape=[jax.ShapeDtypeStruct((T, AD), F32)] * 3,
        scratch_shapes=[pltpu.VMEM((T, LANE), BF16), pltpu.VMEM((T, LANE), BF16)],
        compiler_params=_cp(("arbitrary", "arbitrary")),
    )(u, u, u, do, tot)


def _inproj_dw(hn, du):
    tm = 512

    def body(hn_ref, du_ref, dw_ref):
        @pl.when(pl.program_id(1) == 0)
        def _():
            dw_ref[...] = jnp.zeros_like(dw_ref)
        dw_ref[0] += _dot_tn(hn_ref[...], du_ref[...])

    return pl.pallas_call(
        body, name="inproj_dw", grid=(NCHIP, T // tm),
        in_specs=[pl.BlockSpec((tm, D), lambda k, i: (i, 0)), pl.BlockSpec((tm, SHW), lambda k, i: (i, k))],
        out_specs=pl.BlockSpec((1, D, SHW), lambda k, i: (k, 0, 0)),
        out_shape=jax.ShapeDtypeStruct((NCHIP, D, SHW), F32),
        compiler_params=_cp(("arbitrary", "arbitrary")),
    )(hn, du)


def _inproj_dx(du, w, h, g, dres):
    tm = 512

    def body(du_ref, w_ref, h_ref, g_ref, dres_ref, dh_ref, dg_ref, acc_s):
        i, k = pl.program_id(0), pl.program_id(1)

        @pl.when(jnp.logical_and(i == 0, k == 0))
        def _():
            dg_ref[...] = jnp.zeros_like(dg_ref)

        part = _dot_nt(du_ref[...], w_ref[0])

        @pl.when(k == 0)
        def _():
            acc_s[...] = part

        @pl.when(k > 0)
        def _():
            acc_s[...] += part

        @pl.when(k == NCHIP - 1)
        def _():
            hh = h_ref[...]
            r = _rstd(hh)
            dhn = acc_s[...]
            dg_ref[...] += jnp.sum(dhn * hh * r, axis=0, keepdims=True)
            dh_ref[...] = dres_ref[...] + _rms_bwd(dhn, hh, r, g_ref[...])

    return pl.pallas_call(
        body, name="inproj_dx", grid=(T // tm, NCHIP),
        in_specs=[pl.BlockSpec((tm, SHW), lambda i, k: (i, k)),
                  pl.BlockSpec((1, D, SHW), lambda i, k: (k, 0, 0)),
                  pl.BlockSpec((tm, D), lambda i, k: (i, 0)),
                  pl.BlockSpec((1, D), lambda i, k: (0, 0)),
                  pl.BlockSpec((tm, D), lambda i, k: (i, 0))],
        out_specs=[pl.BlockSpec((tm, D), lambda i, k: (i, 0)), pl.BlockSpec((1, D), lambda i, k: (0, 0))],
        out_shape=[jax.ShapeDtypeStruct((T, D), F32), jax.ShapeDtypeStruct((1, D), F32)],
        scratch_shapes=[pltpu.VMEM((tm, D), F32)],
        compiler_params=_cp(("arbitrary", "arbitrary")),
    )(du, w, h, g, dres)


def _sum_pair(a, b):
    tr = PR // 2

    def body(a_ref, b_ref, o_ref):
        o_ref[...] = (a_ref[...] + b_ref[...].astype(F32)).astype(BF16)

    spec = pl.BlockSpec((1, tr, 1024), lambda k, i: (k, i, 0))
    return pl.pallas_call(
        body, name="sum_pair", grid=(NCHIP, 2), in_specs=[spec, spec], out_specs=spec,
        out_shape=jax.ShapeDtypeStruct((NCHIP, PR, 1024), BF16),
        compiler_params=_cp(("arbitrary", "arbitrary"), VMEM_BIG),
    )(a, b)


def _sum_chips(r):
    tr = PR // 2

    def body(r_ref, o_ref):
        acc = r_ref[0].astype(F32)
        for j in range(1, NCHIP):
            acc = acc + r_ref[j].astype(F32)
        o_ref[...] = acc

    return pl.pallas_call(
        body, name="sum_chips", grid=(2,),
        in_specs=[pl.BlockSpec((NCHIP, tr, 1024), lambda i: (0, i, 0))],
        out_specs=pl.BlockSpec((tr, 1024), lambda i: (i, 0)),
        out_shape=jax.ShapeDtypeStruct((PR, 1024), F32),
        compiler_params=_cp(("arbitrary",), VMEM_BIG),
    )(r)


def _adamw(w, g, m, v, rows):
    R, C = w.shape
    c1 = 1.0 - ADAM_B1 ** ADAM_STEP
    c2 = 1.0 - ADAM_B2 ** ADAM_STEP

    def body(w_ref, g_ref, m_ref, v_ref, d_ref, nm_ref, nv_ref):
        gg = g_ref[...]
        nm = ADAM_B1 * m_ref[...] + (1.0 - ADAM_B1) * gg
        nv = ADAM_B2 * v_ref[...] + (1.0 - ADAM_B2) * (gg * gg)
        d_ref[...] = -ADAM_LR * ((nm / c1) / (jnp.sqrt(nv / c2) + ADAM_EPS) + ADAM_WD * w_ref[...])
        nm_ref[...] = nm
        nv_ref[...] = nv

    spec = pl.BlockSpec((rows, C), lambda i: (i, 0))
    sh = jax.ShapeDtypeStruct((R, C), F32)
    return pl.pallas_call(
        body, name="adamw", grid=(R // rows,), in_specs=[spec] * 4, out_specs=[spec] * 3,
        out_shape=[sh, sh, sh], compiler_params=_cp(("arbitrary",)),
    )(w, g, m, v)


HBM_SPEC = pl.BlockSpec(memory_space=pltpu.HBM)


def _place():
    return lax.axis_index("x"), lax.axis_index("y"), lax.axis_index("c")


def _all_gather_weights(shards):
    n = len(shards)

    def body(*refs):
        ins, outs = refs[:n], refs[n:2 * n]
        send_sems, recv_sems, local_sems = refs[2 * n:]
        x, y, c = _place()
        me = 2 * x + y
        peers = [(1 - x, y, c), (x, 1 - y, c), (1 - x, 1 - y, c)]
        copies = []
        for a in range(n):
            loc = pltpu.make_async_copy(ins[a], outs[a].at[me], local_sems.at[a])
            loc.start()
            copies.append(loc)
            for r, peer in enumerate(peers):
                cp = pltpu.make_async_remote_copy(
                    src_ref=ins[a], dst_ref=outs[a].at[me],
                    send_sem=send_sems.at[a * 3 + r], recv_sem=recv_sems.at[a * 3 + r],
                    device_id=peer, device_id_type=MESH)
                cp.start()
                copies.append(cp)
        for cp in copies:
            cp.wait()

    return pl.pallas_call(
        body, name="all_gather_weights",
        in_specs=[HBM_SPEC] * n, out_specs=[HBM_SPEC] * n,
        out_shape=[jax.ShapeDtypeStruct((NCHIP,) + s.shape, s.dtype) for s in shards],
        scratch_shapes=[pltpu.SemaphoreType.DMA((3 * n,)), pltpu.SemaphoreType.DMA((3 * n,)),
                        pltpu.SemaphoreType.DMA((n,))],
    )(*shards)


def _pair_exchange(gf, gb):
    def body(gf_ref, gb_ref, own_ref, got_ref, send_sem, recv_sem, local_sem):
        x, y, c = _place()
        loc = pltpu.make_async_copy(gf_ref.at[c], own_ref, local_sem)
        loc.start()
        cp = pltpu.make_async_remote_copy(
            src_ref=gb_ref.at[1 - c], dst_ref=got_ref, send_sem=send_sem, recv_sem=recv_sem,
            device_id=(x, y, 1 - c), device_id_type=MESH)
        cp.start()
        cp.wait()
        loc.wait()

    return pl.pallas_call(
        body, name="pair_exchange", in_specs=[HBM_SPEC] * 2, out_specs=[HBM_SPEC] * 2,
        out_shape=[jax.ShapeDtypeStruct((NCHIP, PR, 1024), F32), jax.ShapeDtypeStruct((NCHIP, PR, 1024), BF16)],
        scratch_shapes=[pltpu.SemaphoreType.DMA, pltpu.SemaphoreType.DMA, pltpu.SemaphoreType.DMA],
    )(gf, gb)


def _chip_scatter(pb):
    def body(pb_ref, got_ref, send_sems, recv_sems, local_sem):
        x, y, c = _place()
        me = 2 * x + y
        loc = pltpu.make_async_copy(pb_ref.at[me], got_ref.at[me], local_sem)
        loc.start()
        copies = [loc]
        for r, (px, py) in enumerate([(1 - x, y), (x, 1 - y), (1 - x, 1 - y)]):
            cp = pltpu.make_async_remote_copy(
                src_ref=pb_ref.at[2 * px + py], dst_ref=got_ref.at[me],
                send_sem=send_sems.at[r], recv_sem=recv_sems.at[r],
                device_id=(px, py, c), device_id_type=MESH)
            cp.start()
            copies.append(cp)
        for cp in copies:
            cp.wait()

    return pl.pallas_call(
        body, name="chip_scatter", in_specs=[HBM_SPEC], out_specs=HBM_SPEC,
        out_shape=jax.ShapeDtypeStruct((NCHIP, PR, 1024), BF16),
        scratch_shapes=[pltpu.SemaphoreType.DMA((3,)), pltpu.SemaphoreType.DMA((3,)), pltpu.SemaphoreType.DMA],
    )(pb)


def _pair_share(tot):
    def body(t_ref, out_ref, send_sem, recv_sem, local_sem):
        x, y, c = _place()
        loc = pltpu.make_async_copy(t_ref, out_ref.at[c], local_sem)
        loc.start()
        cp = pltpu.make_async_remote_copy(
            src_ref=t_ref, dst_ref=out_ref.at[c], send_sem=send_sem, recv_sem=recv_sem,
            device_id=(x, y, 1 - c), device_id_type=MESH)
        cp.start()
        cp.wait()
        loc.wait()

    return pl.pallas_call(
        body, name="pair_share", in_specs=[HBM_SPEC], out_specs=HBM_SPEC,
        out_shape=jax.ShapeDtypeStruct((DEPTH, PR, 1024), F32),
        scratch_shapes=[pltpu.SemaphoreType.DMA, pltpu.SemaphoreType.DMA, pltpu.SemaphoreType.DMA],
    )(tot)


def _small_allreduce(mine):
    def body(m_ref, o_ref, slots, send_sems, recv_sems):
        x, y, c = _place()
        me = 4 * x + 2 * y + c
        slots[me] = m_ref[...]
        copies = []
        for r in range(1, 8):
            rx, ry, rc = (r >> 2) & 1, (r >> 1) & 1, r & 1
            peer = (x + rx - 2 * x * rx, y + ry - 2 * y * ry, c + rc - 2 * c * rc)
            cp = pltpu.make_async_remote_copy(
                src_ref=m_ref, dst_ref=slots.at[me], send_sem=send_sems.at[r - 1], recv_sem=recv_sems.at[r - 1],
                device_id=peer, device_id_type=MESH)
            cp.start()
            copies.append(cp)
        for cp in copies:
            cp.wait()
        acc = slots[0]
        for j in range(1, 8):
            acc = acc + slots[j]
        o_ref[...] = acc

    return pl.pallas_call(
        body, name="small_allreduce",
        in_specs=[pl.BlockSpec(memory_space=pltpu.VMEM)], out_specs=pl.BlockSpec(memory_space=pltpu.VMEM),
        out_shape=jax.ShapeDtypeStruct((SMALL_ROWS, LANE), F32),
        scratch_shapes=[pltpu.VMEM((8, SMALL_ROWS, LANE), F32), pltpu.SemaphoreType.DMA((7,)),
                        pltpu.SemaphoreType.DMA((7,))],
    )(mine)


def _seg_matrix():
    i = lax.broadcasted_iota(jnp.int32, (AD, AD), 0) // DH
    j = lax.broadcasted_iota(jnp.int32, (AD, AD), 1) // DH
    return (i == j).astype(BF16)


def _local_step(x, p, tgt, sm, gw):
    seg = _seg_matrix()
    h = x
    saved = []
    for l in range(DEPTH):
        w_in = gw["w_in"][:, l]
        w_out = gw["w_out"][:, l].reshape(D, D)
        w_gate = gw["w_ple_gate"][:, l].reshape(D, D)
        w_ple = gw["w_ple"][:, l]
        w_pw = gw["w_pw"][:, l].reshape(CD, CD)
        dw = gw["dw_w"][:, l]
        row = lambda name: sm[name][l:l + 1]
        u, hn = _rms_inproj(h, row("norm_g"), w_in)
        o, ya, tot = _attn_fwd(u, row("attn_out_g"))
        c1 = _glu_conv(u, dw, row("dw_b"))
        c3, yc, h1, gate, pe, h2 = _layer_tail(
            c1, u, ya, h, p[l], row("conv_ln_g"), row("conv_ln_b"), w_pw, row("conv_out_g"), w_out,
            row("ple_norm_g"), w_gate, w_ple)
        saved.append(dict(h=h, u=u, hn=hn, o=o, ya=ya, tot=tot, c1=c1, c3=c3, yc=yc, h1=h1, gate=gate, pe=pe,
                          w_in=w_in, w_out=w_out, w_gate=w_gate, w_pw=w_pw, dw=dw))
        h = h2
    loss_blk, dh, dfg = _loss_head(h, tgt, sm["final_g"])
    big = [None] * DEPTH
    small = [None] * DEPTH
    for l in reversed(range(DEPTH)):
        s = saved[l]
        row = lambda name: sm[name][l:l + 1]
        dh1, dy, dwg, dwp, dwo, dpg = _ple_out_bwd(
            dh, s["h1"], s["gate"], s["pe"], p[l], s["ya"], s["yc"], row("ple_norm_g"), s["w_gate"], s["w_out"])
        ag_t = jnp.tile(row("attn_out_g"), (1, AD // DH))
        do, dga, dgc, dc1, dwpw, dag, dcg, dlg, dlb = _branch_bwd(
            dy, s["o"], s["u"], s["c1"], s["c3"], ag_t, row("conv_ln_g"), row("conv_ln_b"), s["w_pw"],
            row("conv_out_g"), seg)
        dcv, dcgate, ddw, ddb = _conv_bwd(dc1, s["u"], s["dw"])
        dq, dk, dv = _attn_bwd(s["u"], do, s["tot"])
        du = jnp.concatenate([dq, dk, dv, dga, dcv, dcgate, dgc], axis=1).astype(BF16)
        dwin = _inproj_dw(s["hn"], du)
        dh, dng = _inproj_dx(du, s["w_in"], s["h"], row("norm_g"), dh1)
        big[l] = dict(w_in=dwin, w_out=dwo, w_ple_gate=dwg, w_ple=dwp, w_pw=dwpw, dw_w=ddw)
        small[l] = dict(norm_g=dng, attn_out_g=dag.reshape(AD // DH, DH).sum(axis=0, keepdims=True), dw_b=ddb,
                        conv_ln_g=dlg, conv_ln_b=dlb, conv_out_g=dcg, ple_norm_g=dpg)
    return loss_blk[0, 0], dh, big, small, dfg


BIG = ("w_in", "w_out", "w_ple_gate", "w_ple", "w_pw", "dw_w")
SMALL2 = ("norm_g", "ple_norm_g", "dw_b", "conv_ln_g", "conv_ln_b", "conv_out_g", "attn_out_g")


def _pack_big_layer(g):
    parts = [g["w_in"].reshape(NCHIP, 896, 1024), g["w_out"].reshape(NCHIP, 256, 1024),
             g["w_ple_gate"].reshape(NCHIP, 256, 1024), g["w_ple"].reshape(NCHIP, 64, 1024),
             g["w_pw"].reshape(NCHIP, 64, 1024), g["dw_w"].reshape(NCHIP, 4, 1024),
             jnp.zeros((NCHIP, PR - sum(PK_ROWS), 1024), F32)]
    return jnp.concatenate(parts, axis=1)


def _unpack_big(gfull):
    out, r0 = {}, 0
    shapes = dict(w_in=(D, SHW), w_out=(256, D), w_ple_gate=(256, D), w_ple=(PLE, PLE), w_pw=(128, CD),
                  dw_w=(CWP, LANE))
    for name, rows in zip(BIG, PK_ROWS):
        out[name] = gfull[:, r0:r0 + rows].reshape((DEPTH,) + shapes[name])
        r0 += rows
    return out


def _pack_small(two, final):
    flat = jnp.concatenate([two[k].reshape(-1) for k in SMALL2] + [final.reshape(-1)])
    flat = jnp.concatenate([flat, jnp.zeros((SMALL_ROWS * LANE - flat.shape[0],), F32)])
    return flat.reshape(SMALL_ROWS, LANE)


def _unpack_small(packed, like_two, like_final):
    flat = packed.reshape(-1)
    out, off = {}, 0
    for k in SMALL2:
        n = like_two[k].size
        out[k] = flat[off:off + n].reshape(like_two[k].shape)
        off += n
    return out, flat[off:off + like_final.size].reshape(like_final.shape)


def kernel(x, p, norm_g, w_in, attn_out_g, dw_w, dw_b, conv_ln_g, conv_ln_b, w_pw, conv_out_g, w_out, ple_norm_g, w_ple_gate, w_ple, final_g, loss_target, m_norm_g, m_w_in, m_attn_out_g, m_dw_w, m_dw_b, m_conv_ln_g, m_conv_ln_b, m_w_pw, m_conv_out_g, m_w_out, m_ple_norm_g, m_w_ple_gate, m_w_ple, m_final_g, v_norm_g, v_w_in, v_attn_out_g, v_dw_w, v_dw_b, v_conv_ln_g, v_conv_ln_b, v_w_pw, v_conv_out_g, v_w_out, v_ple_norm_g, v_w_ple_gate, v_w_ple, v_final_g):
    W = dict(norm_g=norm_g, w_in=w_in, attn_out_g=attn_out_g, dw_w=dw_w, dw_b=dw_b, conv_ln_g=conv_ln_g,
             conv_ln_b=conv_ln_b, w_pw=w_pw, conv_out_g=conv_out_g, w_out=w_out, ple_norm_g=ple_norm_g,
             w_ple_gate=w_ple_gate, w_ple=w_ple, final_g=final_g)
    M = dict(norm_g=m_norm_g, w_in=m_w_in, attn_out_g=m_attn_out_g, dw_w=m_dw_w, dw_b=m_dw_b,
             conv_ln_g=m_conv_ln_g, conv_ln_b=m_conv_ln_b, w_pw=m_w_pw, conv_out_g=m_conv_out_g, w_out=m_w_out,
             ple_norm_g=m_ple_norm_g, w_ple_gate=m_w_ple_gate, w_ple=m_w_ple, final_g=m_final_g)
    V = dict(norm_g=v_norm_g, w_in=v_w_in, attn_out_g=v_attn_out_g, dw_w=v_dw_w, dw_b=v_dw_b,
             conv_ln_g=v_conv_ln_g, conv_ln_b=v_conv_ln_b, w_pw=v_w_pw, conv_out_g=v_conv_out_g, w_out=v_w_out,
             ple_norm_g=v_ple_norm_g, w_ple_gate=v_w_ple_gate, w_ple=v_w_ple, final_g=v_final_g)
    order = ("norm_g", "w_in", "attn_out_g", "dw_w", "dw_b", "conv_ln_g", "conv_ln_b", "w_pw", "conv_out_g",
             "w_out", "ple_norm_g", "w_ple_gate", "w_ple", "final_g")

    pad_taps = lambda a: jnp.pad(a, ((0, 0), (0, CWP - CW), (0, 0)))
    shards = [w_in.astype(BF16), w_out.astype(BF16), w_ple_gate.astype(BF16), w_ple.astype(BF16),
              w_pw.astype(BF16), pad_taps(dw_w)]
    gathered = _all_gather_weights(shards)
    gw = dict(zip(BIG, gathered))

    sm = {k: W[k] for k in SMALL2}
    sm["final_g"] = final_g.reshape(1, D)
    loss_part, grad_x, big, small, dfg = _local_step(x[0], p[:, 0], loss_target[0], sm, gw)
    loss = lax.psum(loss_part, ("x", "y", "c"))

    gf = jnp.stack([_pack_big_layer(big[l]) for l in range(DEPTH)])
    own, got = _pair_exchange(gf, gf.astype(BF16))
    contrib = _chip_scatter(_sum_pair(own, got))
    gfull = _pair_share(_sum_chips(contrib))
    g_big = _unpack_big(gfull)

    small_two = {k: jnp.concatenate([small[l][k] for l in range(DEPTH)], axis=0) for k in SMALL2}
    g_small_packed = _small_allreduce(_pack_small(small_two, dfg))
    g_small, g_final = _unpack_small(g_small_packed, {k: W[k] for k in SMALL2}, final_g)

    grads, deltas, new_m, new_v = {}, {}, {}, {}
    for name in BIG:
        wv = pad_taps(W[name]) if name == "dw_w" else W[name]
        mv = pad_taps(M[name]) if name == "dw_w" else M[name]
        vv = pad_taps(V[name]) if name == "dw_w" else V[name]
        gg = g_big[name]
        cols = wv.shape[-1]
        rows_total = wv.size // cols
        tile_rows = min(rows_total, 256)
        d2, m2, v2 = _adamw(wv.reshape(rows_total, cols), gg.reshape(rows_total, cols),
                            mv.reshape(rows_total, cols), vv.reshape(rows_total, cols), tile_rows)
        if name == "dw_w":
            cut = lambda a: a.reshape(DEPTH, CWP, LANE)[:, :CW]
            grads[name], deltas[name], new_m[name], new_v[name] = cut(gg), cut(d2), cut(m2), cut(v2)
        else:
            grads[name] = gg
            deltas[name], new_m[name], new_v[name] = (t.reshape(wv.shape) for t in (d2, m2, v2))
    ws = _pack_small({k: W[k] for k in SMALL2}, final_g)
    ms = _pack_small({k: M[k] for k in SMALL2}, m_final_g)
    vs = _pack_small({k: V[k] for k in SMALL2}, v_final_g)
    ds, nms, nvs = _adamw(ws, g_small_packed, ms, vs, SMALL_ROWS)
    for packed, dst in ((ds, deltas), (nms, new_m), (nvs, new_v)):
        two, fin = _unpack_small(packed, {k: W[k] for k in SMALL2}, final_g)
        dst.update(two)
        dst["final_g"] = fin
    grads.update(g_small)
    grads["final_g"] = g_final

    return (loss, grad_x[None], *[grads[n] for n in order], *[deltas[n] for n in order],
            *[new_m[n] for n in order], *[new_v[n] for n in order])
```

```python
import functools

import jax
import jax.numpy as jnp
from jax import lax
from jax.experimental import pallas as pl
from jax.experimental.pallas import tpu as pltpu

F32 = jnp.float32
BF16 = jnp.bfloat16

T = 2048
D = 1024
DIN = 3584
NCHIP = 4
SHW = DIN // NCHIP
AD = 512
CD = 512
DH = 64
CW = 31
CWP = 32
PLE = 256
DEPTH = 2
EPS = 1e-6
AQ = 256
LANE = 128

ADAM_LR = 0.001
ADAM_B1 = 0.9
ADAM_B2 = 0.999
ADAM_EPS = 1e-08
ADAM_WD = 0.01
ADAM_STEP = 10

PK_ROWS = (896, 256, 256, 64, 64, 4)
PR = 1568
SMALL_ROWS = 80

VMEM_BIG = 56 * 1024 * 1024
MESH = pl.DeviceIdType.MESH


def _cp(sem=None, vmem=None):
    kw = {}
    if sem is not None:
        kw["dimension_semantics"] = sem
    if vmem is not None:
        kw["vmem_limit_bytes"] = vmem
    return pltpu.CompilerParams(**kw)


def _dot(a, b):
    return jnp.dot(a, b, preferred_element_type=F32)


def _dot_nt(a, b):
    return lax.dot_general(a, b, (((1,), (1,)), ((), ())), preferred_element_type=F32)


def _dot_tn(a, b):
    return lax.dot_general(a, b, (((0,), (0,)), ((), ())), preferred_element_type=F32)


def _dot2(x, m):
    hi = x.astype(BF16)
    lo = (x - hi.astype(F32)).astype(BF16)
    return _dot(hi, m) + _dot(lo, m)


def _sig(x):
    return 1.0 / (1.0 + jnp.exp(-x))


def _softplus(z):
    return jnp.maximum(z, 0.0) + jnp.log(1.0 + jnp.exp(-jnp.abs(z)))


def _rstd(x):
    return lax.rsqrt(jnp.mean(x * x, axis=-1, keepdims=True) + EPS)


def _rms_bwd(dy, x, r, g):
    dn = dy * g
    return r * dn - x * (r * r * r) * jnp.mean(dn * x, axis=-1, keepdims=True)


def _rms_inproj(h, g, w):
    tm = 512

    def body(h_ref, g_ref, w_ref, u_ref, hn_ref, hn_s):
        @pl.when(pl.program_id(1) == 0)
        def _():
            hh = h_ref[...]
            hn = (hh * _rstd(hh) * g_ref[...]).astype(BF16)
            hn_s[...] = hn
            hn_ref[...] = hn
        u_ref[...] = _dot(hn_s[...], w_ref[0])

    return pl.pallas_call(
        body, name="rms_inproj", grid=(T // tm, NCHIP),
        in_specs=[pl.BlockSpec((tm, D), lambda i, k: (i, 0)),
                  pl.BlockSpec((1, D), lambda i, k: (0, 0)),
                  pl.BlockSpec((1, D, SHW), lambda i, k: (k, 0, 0))],
        out_specs=[pl.BlockSpec((tm, SHW), lambda i, k: (i, k)),
                   pl.BlockSpec((tm, D), lambda i, k: (i, 0))],
        out_shape=[jax.ShapeDtypeStruct((T, DIN), F32), jax.ShapeDtypeStruct((T, D), BF16)],
        scratch_shapes=[pltpu.VMEM((tm, D), BF16)],
        compiler_params=_cp(("arbitrary", "arbitrary")),
    )(h, g, w)


def _attn_tiles():
    row = lax.broadcasted_iota(jnp.int32, (2 * AQ, AQ), 0) & (AQ - 1)
    col = lax.broadcasted_iota(jnp.int32, (2 * AQ, AQ), 1)
    tr = lax.broadcasted_iota(jnp.int32, (AQ, AQ), 0)
    tc = lax.broadcasted_iota(jnp.int32, (AQ, AQ), 1)
    first = lax.broadcasted_iota(jnp.int32, (1, LANE), 1) < DH
    return col < row, tr, tc, first


def _stack_heads(t, first):
    return jnp.concatenate([jnp.where(first, t, 0.0), jnp.where(first, 0.0, t)], axis=0)


def _unstack_heads(t, first):
    return jnp.where(first, t[:AQ], t[AQ:])


def _tri_sum(x, tri):
    hi = x.astype(BF16)
    lo = (x - hi.astype(F32)).astype(BF16)
    both = _dot(jnp.concatenate([hi, lo], axis=0), tri)
    return both[:2 * AQ] + both[2 * AQ:]


def _attn_fwd(u, ag2):
    def body(q_ref, k_ref, v_ref, g_ref, ag_ref, o_ref, y_ref, tot_ref, kb_s, vb_s):
        qi = pl.program_id(1)

        @pl.when(qi == 0)
        def _():
            kb_s[...] = k_ref[...].astype(BF16)
            vb_s[...] = v_ref[...].astype(BF16)

        causal, tr, tc, first = _attn_tiles()
        upper = (tr > tc).astype(BF16)
        qs = _stack_heads(q_ref[...] * 0.125, first).astype(BF16)

        def block(kb, carry, masked):
            run, acc = carry
            k0 = pl.multiple_of(kb * AQ, AQ)
            kk = kb_s[pl.ds(k0, AQ), :]
            vv = vb_s[pl.ds(k0, AQ), :]
            z = _dot_nt(qs, kk)
            sp = _softplus(z)
            lm = jnp.where(causal, -sp, 0.0) if masked else -sp
            att = jnp.exp((z - sp) + _tri_sum(lm, upper) + run)
            if masked:
                att = jnp.where(causal, att, 0.0)
            acc = acc + _dot(att.astype(BF16), vv)
            return run + jnp.sum(lm, axis=1, keepdims=True), acc

        carry = block(qi, (jnp.zeros((2 * AQ, 1), F32), jnp.zeros((2 * AQ, LANE), F32)), True)
        run, acc = lax.fori_loop(0, qi, lambda i, c: block(qi - 1 - i, c, False), carry)
        o = _unstack_heads(acc, first)
        osq = o * o
        ms_a = jnp.sum(jnp.where(first, osq, 0.0), axis=1, keepdims=True)
        ms_b = jnp.sum(jnp.where(first, 0.0, osq), axis=1, keepdims=True)
        r = lax.rsqrt(jnp.where(first, ms_a, ms_b) * (1.0 / DH) + EPS)
        g = g_ref[...]
        o_ref[...] = o
        y_ref[...] = (o * r * ag_ref[...] * (g * _sig(g))).astype(BF16)
        tot_ref[0] = jnp.where(first, run[:AQ], run[AQ:])

    return pl.pallas_call(
        body, name="attn_fwd", grid=(AD // LANE, T // AQ),
        in_specs=[pl.BlockSpec((AQ, LANE), lambda hp, qi: (qi, hp)),
                  pl.BlockSpec((T, LANE), lambda hp, qi: (0, 4 + hp)),
                  pl.BlockSpec((T, LANE), lambda hp, qi: (0, 8 + hp)),
                  pl.BlockSpec((AQ, LANE), lambda hp, qi: (qi, 12 + hp)),
                  pl.BlockSpec((1, LANE), lambda hp, qi: (0, 0))],
        out_specs=[pl.BlockSpec((AQ, LANE), lambda hp, qi: (qi, hp)),
                   pl.BlockSpec((AQ, LANE), lambda hp, qi: (qi, hp)),
                   pl.BlockSpec((1, AQ, LANE), lambda hp, qi: (hp, qi, 0))],
        out_shape=[jax.ShapeDtypeStruct((T, AD), F32), jax.ShapeDtypeStruct((T, AD), BF16),
                   jax.ShapeDtypeStruct((AD // LANE, T, LANE), F32)],
        scratch_shapes=[pltpu.VMEM((T, LANE), BF16), pltpu.VMEM((T, LANE), BF16)],
        compiler_params=_cp(("arbitrary", "arbitrary"), VMEM_BIG),
    )(u, u, u, u, ag2)


def _glu_conv(u, dw, db):
    tr = 256

    def body(cv_ref, cg_ref, w_ref, b_ref, c1_ref, pad_s):
        pad_s[pl.ds(0, CWP), :] = jnp.zeros((CWP, LANE), F32)
        pad_s[pl.ds(CWP, T), :] = cv_ref[...] * _sig(cg_ref[...])
        wv = w_ref[0]
        bias = b_ref[...]

        def tile(i, carry):
            r0 = pl.multiple_of(i * tr, tr)
            acc = jnp.zeros((tr, LANE), F32) + bias
            for w in range(CW):
                acc = acc + pad_s[pl.ds(r0 + (CWP - CW + 1) + w, tr), :] * wv[w:w + 1, :]
            c1_ref[pl.ds(r0, tr), :] = acc
            return carry

        lax.fori_loop(0, T // tr, tile, 0)

    return pl.pallas_call(
        body, name="glu_conv", grid=(CD // LANE,),
        in_specs=[pl.BlockSpec((T, LANE), lambda cb: (0, 16 + cb)),
                  pl.BlockSpec((T, LANE), lambda cb: (0, 20 + cb)),
                  pl.BlockSpec((1, CWP, LANE), lambda cb: (cb, 0, 0)),
                  pl.BlockSpec((1, LANE), lambda cb: (0, cb))],
        out_specs=pl.BlockSpec((T, LANE), lambda cb: (0, cb)),
        out_shape=jax.ShapeDtypeStruct((T, CD), F32),
        scratch_shapes=[pltpu.VMEM((T + CWP, LANE), F32)],
        compiler_params=_cp(("arbitrary",)),
    )(u, u, dw, db)


def _ln_silu(c1, lg, lb):
    mu = jnp.mean(c1, axis=-1, keepdims=True)
    xc = c1 - mu
    rs = lax.rsqrt(jnp.mean(xc * xc, axis=-1, keepdims=True) + EPS)
    xh = xc * rs
    ln = xh * lg + lb
    s = _sig(ln)
    return xh, rs, ln, s


def _layer_tail(c1, u, ya, h, p, lg, lb, wpw, cg, wout, pg, wgate, wple):
    tm = 256

    def body(c1_ref, gc_ref, ya_ref, h_ref, p_ref, lg_ref, lb_ref, wpw_ref, cg_ref, wout_ref,
             pg_ref, wgate_ref, wple_ref, c3_ref, yc_ref, h1_ref, gate_ref, pe_ref, h2_ref):
        _, _, ln, s = _ln_silu(c1_ref[...], lg_ref[...], lb_ref[...])
        c2 = (ln * s).astype(BF16)
        c3 = _dot(c2, wpw_ref[...])
        gc = gc_ref[...]
        yc = (c3 * _rstd(c3) * cg_ref[...] * (gc * _sig(gc))).astype(BF16)
        c3_ref[...] = c3
        yc_ref[...] = yc
        y = _dot(ya_ref[...], wout_ref[pl.ds(0, AD), :]) + _dot(yc, wout_ref[pl.ds(AD, CD), :])
        h1 = h_ref[...] + y
        hn2 = (h1 * _rstd(h1) * pg_ref[...]).astype(BF16)
        gate = _sig(_dot(hn2, wgate_ref[...]))
        pb = p_ref[...].astype(BF16)
        pe = jnp.concatenate([_dot(pb, wple_ref[k]) for k in range(NCHIP)], axis=1)
        h1_ref[...] = h1
        gate_ref[...] = gate
        pe_ref[...] = pe
        h2_ref[...] = h1 + pe * gate

    row = lambda w: pl.BlockSpec((tm, w), lambda i: (i, 0))
    full = lambda *s: pl.BlockSpec(s, lambda i: (0,) * len(s))
    return pl.pallas_call(
        body, name="layer_tail", grid=(T // tm,),
        in_specs=[row(CD), pl.BlockSpec((tm, CD), lambda i: (i, 6)), row(AD), row(D), row(PLE),
                  full(1, CD), full(1, CD), full(CD, CD), full(1, CD), full(D, D),
                  full(1, D), full(D, D), full(NCHIP, PLE, PLE)],
        out_specs=[row(CD), row(CD), row(D), row(D), row(D), row(D)],
        out_shape=[jax.ShapeDtypeStruct((T, CD), F32), jax.ShapeDtypeStruct((T, CD), BF16)]
        + [jax.ShapeDtypeStruct((T, D), F32)] * 4,
        compiler_params=_cp(("arbitrary",), VMEM_BIG),
    )(c1, u, ya, h, p, lg, lb, wpw, cg, wout, pg, wgate, wple)


def _loss_head(h, tgt, fg):
    tm = 256

    def body(h_ref, t_ref, g_ref, loss_ref, dh_ref, dg_ref):
        @pl.when(pl.program_id(0) == 0)
        def _():
            loss_ref[...] = jnp.zeros_like(loss_ref)
            dg_ref[...] = jnp.zeros_like(dg_ref)
        hh = h_ref[...]
        g = g_ref[...]
        r = _rstd(hh)
        e = hh * r * g - t_ref[...]
        loss_ref[...] += 0.5 * jnp.sum(jnp.mean(e * e, axis=-1, keepdims=True))
        dy = e * (1.0 / D)
        dg_ref[...] += jnp.sum(dy * hh * r, axis=0, keepdims=True)
        dh_ref[...] = _rms_bwd(dy, hh, r, g)

    return pl.pallas_call(
        body, name="loss_head", grid=(T // tm,),
        in_specs=[pl.BlockSpec((tm, D), lambda i: (i, 0)), pl.BlockSpec((tm, D), lambda i: (i, 0)),
                  pl.BlockSpec((1, D), lambda i: (0, 0))],
        out_specs=[pl.BlockSpec((8, LANE), lambda i: (0, 0)), pl.BlockSpec((tm, D), lambda i: (i, 0)),
                   pl.BlockSpec((1, D), lambda i: (0, 0))],
        out_shape=[jax.ShapeDtypeStruct((8, LANE), F32), jax.ShapeDtypeStruct((T, D), F32),
                   jax.ShapeDtypeStruct((1, D), F32)],
        compiler_params=_cp(("arbitrary",)),
    )(h, tgt, fg)


def _ple_out_bwd(dh2, h1, gate, pe, p, ya, yc, pg, wgate, wout):
    tm = 256

    def body(dh2_ref, h1_ref, gate_ref, pe_ref, p_ref, ya_ref, yc_ref, pg_ref, wgate_ref, wout_ref,
             dh1_ref, dy_ref, dwg_ref, dwp_ref, dwo_ref, dpg_ref):
        @pl.when(pl.program_id(0) == 0)
        def _():
            dwg_ref[...] = jnp.zeros_like(dwg_ref)
            dwp_ref[...] = jnp.zeros_like(dwp_ref)
            dwo_ref[...] = jnp.zeros_like(dwo_ref)
            dpg_ref[...] = jnp.zeros_like(dpg_ref)
        dh2 = dh2_ref[...]
        h1 = h1_ref[...]
        gate = gate_ref[...]
        pg = pg_ref[...]
        dpe = (dh2 * gate).astype(BF16)
        dgp = (dh2 * pe_ref[...] * gate * (1.0 - gate)).astype(BF16)
        r = _rstd(h1)
        hn = h1 * r
        dwg_ref[...] += _dot_tn((hn * pg).astype(BF16), dgp)
        dhn2 = _dot_nt(dgp, wgate_ref[...])
        dpg_ref[...] += jnp.sum(dhn2 * hn, axis=0, keepdims=True)
        dh1 = dh2 + _rms_bwd(dhn2, h1, r, pg)
        pb = p_ref[...].astype(BF16)
        for k in range(NCHIP):
            dwp_ref[k] += _dot_tn(pb, dpe[:, k * PLE:(k + 1) * PLE])
        dh1b = dh1.astype(BF16)
        dy_ref[...] = _dot_nt(dh1b, wout_ref[...])
        dwo_ref[pl.ds(0, AD), :] += _dot_tn(ya_ref[...], dh1b)
        dwo_ref[pl.ds(AD, CD), :] += _dot_tn(yc_ref[...], dh1b)
        dh1_ref[...] = dh1

    row = lambda w: pl.BlockSpec((tm, w), lambda i: (i, 0))
    full = lambda *s: pl.BlockSpec(s, lambda i: (0,) * len(s))
    return pl.pallas_call(
        body, name="ple_out_bwd", grid=(T // tm,),
        in_specs=[row(D), row(D), row(D), row(D), row(PLE), row(AD), row(CD),
                  full(1, D), full(D, D), full(D, D)],
        out_specs=[row(D), row(D), full(D, D), full(NCHIP, PLE, PLE), full(D, D), full(1, D)],
        out_shape=[jax.ShapeDtypeStruct((T, D), F32), jax.ShapeDtypeStruct((T, D), F32),
                   jax.ShapeDtypeStruct((D, D), F32), jax.ShapeDtypeStruct((NCHIP, PLE, PLE), F32),
                   jax.ShapeDtypeStruct((D, D), F32), jax.ShapeDtypeStruct((1, D), F32)],
        compiler_params=_cp(("arbitrary",), VMEM_BIG),
    )(dh2, h1, gate, pe, p, ya, yc, pg, wgate, wout)


def _branch_bwd(dy, o, u, c1, c3, ag, lg, lb, wpw, cg, seg):
    tm = 256

    def body(dya_ref, dyc_ref, o_ref, ga_ref, gc_ref, c1_ref, c3_ref, ag_ref, lg_ref, lb_ref, wpw_ref,
             cg_ref, seg_ref, do_ref, dga_ref, dgc_ref, dc1_ref, dwpw_ref, dag_ref, dcg_ref, dlg_ref, dlb_ref):
        @pl.when(pl.program_id(0) == 0)
        def _():
            for r_ in (dwpw_ref, dag_ref, dcg_ref, dlg_ref, dlb_ref):
                r_[...] = jnp.zeros_like(r_)
        dya = dya_ref[...]
        o = o_ref[...]
        ga = ga_ref[...]
        ag_v = ag_ref[...]
        seg_m = seg_ref[...]
        r = lax.rsqrt(_dot2(o * o, seg_m) * (1.0 / DH) + EPS)
        onr = o * r
        sg = _sig(ga)
        dga_ref[...] = dya * (onr * ag_v) * (sg * (1.0 + ga * (1.0 - sg)))
        don = dya * (ga * sg)
        dag_ref[...] += jnp.sum(don * onr, axis=0, keepdims=True)
        dn = don * ag_v
        do_ref[...] = r * dn - o * (r * r * r) * (_dot2(dn * o, seg_m) * (1.0 / DH))
        dyc = dyc_ref[...]
        c3 = c3_ref[...]
        gc = gc_ref[...]
        cg_v = cg_ref[...]
        r3 = _rstd(c3)
        cn = c3 * r3
        sc = _sig(gc)
        dgc_ref[...] = dyc * (cn * cg_v) * (sc * (1.0 + gc * (1.0 - sc)))
        dcn = dyc * (gc * sc)
        dcg_ref[...] += jnp.sum(dcn * cn, axis=0, keepdims=True)
        dc3 = _rms_bwd(dcn, c3, r3, cg_v).astype(BF16)
        lg_v = lg_ref[...]
        xh, rs, ln, s = _ln_silu(c1_ref[...], lg_v, lb_ref[...])
        c2 = (ln * s).astype(BF16)
        dwpw_ref[...] += _dot_tn(c2, dc3)
        dc2 = _dot_nt(dc3, wpw_ref[...])
        dln = dc2 * (s * (1.0 + ln * (1.0 - s)))
        dlb_ref[...] += jnp.sum(dln, axis=0, keepdims=True)
        dlg_ref[...] += jnp.sum(dln * xh, axis=0, keepdims=True)
        dxh = dln * lg_v
        dc1_ref[...] = rs * (dxh - jnp.mean(dxh, axis=-1, keepdims=True)
                             - xh * jnp.mean(dxh * xh, axis=-1, keepdims=True))

    half = lambda j: pl.BlockSpec((tm, 512), lambda i: (i, j))
    full = lambda *s: pl.BlockSpec(s, lambda i: (0,) * len(s))
    vec = jax.ShapeDtypeStruct((1, 512), F32)
    act = jax.ShapeDtypeStruct((T, 512), F32)
    return pl.pallas_call(
        body, name="branch_bwd", grid=(T // tm,),
        in_specs=[half(0), half(1), half(0), half(3), half(6), half(0), half(0),
                  full(1, AD), full(1, CD), full(1, CD), full(CD, CD), full(1, CD), full(AD, AD)],
        out_specs=[half(0), half(0), half(0), half(0), full(CD, CD), full(1, 512), full(1, 512),
                   full(1, 512), full(1, 512)],
        out_shape=[act, act, act, act, jax.ShapeDtypeStruct((CD, CD), F32), vec, vec, vec, vec],
        compiler_params=_cp(("arbitrary",), VMEM_BIG),
    )(dy, dy, o, u, u, c1, c3, ag, lg, lb, wpw, cg, seg)


def _conv_bwd(dc1, u, dw):
    tr = 256
    off = CWP - CW + 1

    def body(d_ref, cv_ref, cg_ref, w_ref, dcv_ref, dcg_ref, ddw_ref, ddb_ref, padc_s, padd_s, acc_s):
        cv = cv_ref[...]
        sg = _sig(cg_ref[...])
        padc_s[pl.ds(0, CWP), :] = jnp.zeros((CWP, LANE), F32)
        padc_s[pl.ds(CWP, T), :] = cv * sg
        padd_s[pl.ds(0, T), :] = d_ref[...]
        padd_s[pl.ds(T, CWP), :] = jnp.zeros((CWP, LANE), F32)
        acc_s[...] = jnp.zeros_like(acc_s)
        wv = w_ref[0]

        def tile(i, carry):
            r0 = pl.multiple_of(i * tr, tr)
            dt = padd_s[pl.ds(r0, tr), :]
            dc0 = jnp.zeros((tr, LANE), F32)
            for w in range(CW):
                dc0 = dc0 + padd_s[pl.ds(r0 + (CW - 1) - w, tr), :] * wv[w:w + 1, :]
                prod = dt * padc_s[pl.ds(r0 + off + w, tr), :]
                acc_s[w] += jnp.sum(prod.reshape(tr // 8, 8, LANE), axis=0)
            cvt = cv_ref[pl.ds(r0, tr), :]
            sgt = _sig(cg_ref[pl.ds(r0, tr), :])
            dcv_ref[pl.ds(r0, tr), :] = dc0 * sgt
            dcg_ref[pl.ds(r0, tr), :] = dc0 * cvt * sgt * (1.0 - sgt)
            return carry

        lax.fori_loop(0, T // tr, tile, 0)
        ddw_ref[0] = jnp.sum(acc_s[...], axis=1)
        ddb_ref[...] = jnp.sum(d_ref[...], axis=0, keepdims=True)

    col = lambda j: pl.BlockSpec((T, LANE), lambda cb: (0, j + cb))
    return pl.pallas_call(
        body, name="conv_bwd", grid=(CD // LANE,),
        in_specs=[col(0), col(16), col(20), pl.BlockSpec((1, CWP, LANE), lambda cb: (cb, 0, 0))],
        out_specs=[col(0), col(0), pl.BlockSpec((1, CWP, LANE), lambda cb: (cb, 0, 0)),
                   pl.BlockSpec((1, LANE), lambda cb: (0, cb))],
        out_shape=[jax.ShapeDtypeStruct((T, CD), F32), jax.ShapeDtypeStruct((T, CD), F32),
                   jax.ShapeDtypeStruct((NCHIP, CWP, LANE), F32), jax.ShapeDtypeStruct((1, CD), F32)],
        scratch_shapes=[pltpu.VMEM((T + CWP, LANE), F32), pltpu.VMEM((T + CWP, LANE), F32),
                        pltpu.VMEM((CWP, 8, LANE), F32)],
        compiler_params=_cp(("arbitrary",)),
    )(dc1, u, u, dw)


def _attn_bwd(u, do, tot):
    def body(q_ref, k_ref, v_ref, do_ref, tot_ref, dq_ref, dk_ref, dv_ref, kb_s, vb_s):
        qi = pl.program_id(1)

        @pl.when(qi == 0)
        def _():
            kb_s[...] = k_ref[...].astype(BF16)
            vb_s[...] = v_ref[...].astype(BF16)
            dk_ref[...] = jnp.zeros_like(dk_ref)
            dv_ref[...] = jnp.zeros_like(dv_ref)

        causal, tr, tc, first = _attn_tiles()
        upper = (tr > tc).astype(BF16)
        lower = (tr < tc).astype(BF16)
        q = q_ref[...]
        qs = _stack_heads(q * 0.125, first).astype(BF16)
        qus = _stack_heads(q, first).astype(BF16)
        dos = _stack_heads(do_ref[...], first).astype(BF16)
        totv = tot_ref[0]
        tots = jnp.concatenate([totv[:, 0:1], totv[:, DH:DH + 1]], axis=0)

        def block(kb, carry, masked):
            lm_left, dl_left, dq = carry
            k0 = pl.multiple_of(kb * AQ, AQ)
            kk = kb_s[pl.ds(k0, AQ), :]
            vv = vb_s[pl.ds(k0, AQ), :]
            z = _dot_nt(qs, kk)
            sp = _softplus(z)
            lm = jnp.where(causal, -sp, 0.0) if masked else -sp
            lm_incl = lm_left + jnp.sum(lm, axis=1, keepdims=True)
            att = jnp.exp((z - sp) + _tri_sum(lm, upper) + (tots - lm_incl))
            if masked:
                att = jnp.where(causal, att, 0.0)
            dl = att * _dot_nt(dos, vv)
            dv_ref[pl.ds(k0, AQ), :] += _dot_tn(att.astype(BF16), dos)
            prefix = dl_left + _tri_sum(dl, lower)
            beta = jnp.exp(z - sp)
            dz = (1.0 - beta) * dl - beta * prefix
            if masked:
                dz = jnp.where(causal, dz, 0.0)
            dzs = (dz * 0.125).astype(BF16)
            dk_ref[pl.ds(k0, AQ), :] += _dot_tn(dzs, qus)
            return lm_incl, dl_left + jnp.sum(dl, axis=1, keepdims=True), dq + _dot(dzs, kk)

        zero = jnp.zeros((2 * AQ, 1), F32)
        carry = lax.fori_loop(0, qi, lambda kb, c: block(kb, c, False),
                              (zero, zero, jnp.zeros((2 * AQ, LANE), F32)))
        _, _, dq = block(qi, carry, True)
        dq_ref[...] = _unstack_heads(dq, first)

    return pl.pallas_call(
        body, name="attn_bwd", grid=(AD // LANE, T // AQ),
        in_specs=[pl.BlockSpec((AQ, LANE), lambda hp, qi: (qi, hp)),
                  pl.BlockSpec((T, LANE), lambda hp, qi: (0, 4 + hp)),
                  pl.BlockSpec((T, LANE), lambda hp, qi: (0, 8 + hp)),
                  pl.BlockSpec((AQ, LANE), lambda hp, qi: (qi, hp)),
                  pl.BlockSpec((1, AQ, LANE), lambda hp, qi: (hp, qi, 0))],
        out_specs=[pl.BlockSpec((AQ, LANE), lambda hp, qi: (qi, hp)),
                   pl.BlockSpec((T, LANE), lambda hp, qi: (0, hp)),
                   pl.BlockSpec((T, LANE), lambda hp, qi: (0, hp))],
        out_shape=[jax.ShapeDtypeStruct((T, AD), F32)] * 3,
        scratch_shapes=[pltpu.VMEM((T, LANE), BF16), pltpu.VMEM((T, LANE), BF16)],
        compiler_params=_cp(("arbitrary", "arbitrary"), VMEM_BIG),
    )(u, u, u, do, tot)


def _inproj_dw(hn, du):
    tm = 512

    def body(hn_ref, du_ref, dw_ref):
        @pl.when(pl.program_id(1) == 0)
        def _():
            dw_ref[...] = jnp.zeros_like(dw_ref)
        dw_ref[0] += _dot_tn(hn_ref[...], du_ref[...])

    return pl.pallas_call(
        body, name="inproj_dw", grid=(NCHIP, T // tm),
        in_specs=[pl.BlockSpec((tm, D), lambda k, i: (i, 0)), pl.BlockSpec((tm, SHW), lambda k, i: (i, k))],
        out_specs=pl.BlockSpec((1, D, SHW), lambda k, i: (k, 0, 0)),
        out_shape=jax.ShapeDtypeStruct((NCHIP, D, SHW), F32),
        compiler_params=_cp(("arbitrary", "arbitrary")),
    )(hn, du)


def _inproj_dx(du, w, h, g, dres):
    tm = 512

    def body(du_ref, w_ref, h_ref, g_ref, dres_ref, dh_ref, dg_ref, acc_s):
        i, k = pl.program_id(0), pl.program_id(1)

        @pl.when(jnp.logical_and(i == 0, k == 0))
        def _():
            dg_ref[...] = jnp.zeros_like(dg_ref)

        part = _dot_nt(du_ref[...], w_ref[0])

        @pl.when(k == 0)
        def _():
            acc_s[...] = part

        @pl.when(k > 0)
        def _():
            acc_s[...] += part

        @pl.when(k == NCHIP - 1)
        def _():
            hh = h_ref[...]
            r = _rstd(hh)
            dhn = acc_s[...]
            dg_ref[...] += jnp.sum(dhn * hh * r, axis=0, keepdims=True)
            dh_ref[...] = dres_ref[...] + _rms_bwd(dhn, hh, r, g_ref[...])

    return pl.pallas_call(
        body, name="inproj_dx", grid=(T // tm, NCHIP),
        in_specs=[pl.BlockSpec((tm, SHW), lambda i, k: (i, k)),
                  pl.BlockSpec((1, D, SHW), lambda i, k: (k, 0, 0)),
                  pl.BlockSpec((tm, D), lambda i, k: (i, 0)),
                  pl.BlockSpec((1, D), lambda i, k: (0, 0)),
                  pl.BlockSpec((tm, D), lambda i, k: (i, 0))],
        out_specs=[pl.BlockSpec((tm, D), lambda i, k: (i, 0)), pl.BlockSpec((1, D), lambda i, k: (0, 0))],
        out_shape=[jax.ShapeDtypeStruct((T, D), F32), jax.ShapeDtypeStruct((1, D), F32)],
        scratch_shapes=[pltpu.VMEM((tm, D), F32)],
        compiler_params=_cp(("arbitrary", "arbitrary")),
    )(du, w, h, g, dres)


def _sum_pair(layer, gf, got):
    tr = PR // 2

    def body(l_ref, a_ref, b_ref, o_ref):
        o_ref[...] = (a_ref[0] + b_ref[...].astype(F32)).astype(BF16)

    spec = pl.BlockSpec((1, tr, 1024), lambda k, i, l: (k, i, 0))
    return pl.pallas_call(
        body, name="sum_pair",
        grid_spec=pltpu.PrefetchScalarGridSpec(
            num_scalar_prefetch=1, grid=(NCHIP, 2),
            in_specs=[pl.BlockSpec((1, 1, tr, 1024), lambda k, i, l: (l[0], k, i, 0)), spec],
            out_specs=spec),
        out_shape=jax.ShapeDtypeStruct((NCHIP, PR, 1024), BF16),
        compiler_params=_cp(("arbitrary", "arbitrary"), VMEM_BIG),
    )(layer, gf, got)


def _sum_chips(chip, pb, got):
    tr = PR // 2

    def body(c_ref, own_ref, r_ref, o_ref):
        acc = own_ref[0].astype(F32)
        for j in range(NCHIP - 1):
            acc = acc + r_ref[j].astype(F32)
        o_ref[...] = acc

    return pl.pallas_call(
        body, name="sum_chips",
        grid_spec=pltpu.PrefetchScalarGridSpec(
            num_scalar_prefetch=1, grid=(2,),
            in_specs=[pl.BlockSpec((1, tr, 1024), lambda i, c: (c[0], i, 0)),
                      pl.BlockSpec((NCHIP - 1, tr, 1024), lambda i, c: (0, i, 0))],
            out_specs=pl.BlockSpec((tr, 1024), lambda i, c: (i, 0))),
        out_shape=jax.ShapeDtypeStruct((PR, 1024), F32),
        compiler_params=_cp(("arbitrary",), VMEM_BIG),
    )(chip, pb, got)


def _adamw(w, g, m, v, rows):
    R, C = w.shape
    c1 = 1.0 - ADAM_B1 ** ADAM_STEP
    c2 = 1.0 - ADAM_B2 ** ADAM_STEP

    def body(w_ref, g_ref, m_ref, v_ref, d_ref, nm_ref, nv_ref):
        gg = g_ref[...]
        nm = ADAM_B1 * m_ref[...] + (1.0 - ADAM_B1) * gg
        nv = ADAM_B2 * v_ref[...] + (1.0 - ADAM_B2) * (gg * gg)
        d_ref[...] = -ADAM_LR * ((nm / c1) / (jnp.sqrt(nv / c2) + ADAM_EPS) + ADAM_WD * w_ref[...])
        nm_ref[...] = nm
        nv_ref[...] = nv

    spec = pl.BlockSpec((rows, C), lambda i: (i, 0))
    sh = jax.ShapeDtypeStruct((R, C), F32)
    return pl.pallas_call(
        body, name="adamw", grid=(R // rows,), in_specs=[spec] * 4, out_specs=[spec] * 3,
        out_shape=[sh, sh, sh], compiler_params=_cp(("arbitrary",)),
    )(w, g, m, v)


HBM_SPEC = pl.BlockSpec(memory_space=pltpu.HBM)


def _place():
    return lax.axis_index("x"), lax.axis_index("y"), lax.axis_index("c")


def _all_gather_weights(shards):
    n = len(shards)

    def body(*refs):
        ins, outs = refs[:n], refs[n:2 * n]
        send_sems, recv_sems = refs[2 * n:]
        x, y, c = _place()
        me = 2 * x + y
        peers = [(1 - x, y, c), (x, 1 - y, c), (1 - x, 1 - y, c)]
        copies = []
        for a in range(n):
            for r, peer in enumerate(peers):
                cp = pltpu.make_async_remote_copy(
                    src_ref=ins[a], dst_ref=outs[a].at[me],
                    send_sem=send_sems.at[a * 3 + r], recv_sem=recv_sems.at[a * 3 + r],
                    device_id=peer, device_id_type=MESH)
                cp.start()
                copies.append(cp)
        for cp in copies:
            cp.wait()

    return pl.pallas_call(
        body, name="all_gather_weights",
        in_specs=[HBM_SPEC] * n, out_specs=[HBM_SPEC] * n,
        out_shape=[jax.ShapeDtypeStruct((NCHIP,) + s.shape, s.dtype) for s in shards],
        scratch_shapes=[pltpu.SemaphoreType.DMA((3 * n,)), pltpu.SemaphoreType.DMA((3 * n,))],
    )(*shards)


def _pair_exchange(gb):
    def body(gb_ref, got_ref, send_sem, recv_sem):
        x, y, c = _place()
        cp = pltpu.make_async_remote_copy(
            src_ref=gb_ref.at[1 - c], dst_ref=got_ref, send_sem=send_sem, recv_sem=recv_sem,
            device_id=(x, y, 1 - c), device_id_type=MESH)
        cp.start()
        cp.wait()

    return pl.pallas_call(
        body, name="pair_exchange", in_specs=[HBM_SPEC], out_specs=HBM_SPEC,
        out_shape=jax.ShapeDtypeStruct((NCHIP, PR, 1024), BF16),
        scratch_shapes=[pltpu.SemaphoreType.DMA, pltpu.SemaphoreType.DMA],
    )(gb)


def _chip_scatter(pb):
    def body(pb_ref, got_ref, send_sems, recv_sems):
        x, y, c = _place()
        copies = []
        for r, (px, py) in enumerate([(1 - x, y), (x, 1 - y), (1 - x, 1 - y)]):
            cp = pltpu.make_async_remote_copy(
                src_ref=pb_ref.at[2 * px + py], dst_ref=got_ref.at[r],
                send_sem=send_sems.at[r], recv_sem=recv_sems.at[r],
                device_id=(px, py, c), device_id_type=MESH)
            cp.start()
            copies.append(cp)
        for cp in copies:
            cp.wait()

    return pl.pallas_call(
        body, name="chip_scatter", in_specs=[HBM_SPEC], out_specs=HBM_SPEC,
        out_shape=jax.ShapeDtypeStruct((NCHIP - 1, PR, 1024), BF16),
        scratch_shapes=[pltpu.SemaphoreType.DMA((3,)), pltpu.SemaphoreType.DMA((3,))],
    )(pb)


def _pair_share(tot):
    def body(t_ref, got_ref, send_sem, recv_sem):
        x, y, c = _place()
        cp = pltpu.make_async_remote_copy(
            src_ref=t_ref, dst_ref=got_ref, send_sem=send_sem, recv_sem=recv_sem,
            device_id=(x, y, 1 - c), device_id_type=MESH)
        cp.start()
        cp.wait()

    return pl.pallas_call(
        body, name="pair_share", in_specs=[HBM_SPEC], out_specs=HBM_SPEC,
        out_shape=jax.ShapeDtypeStruct((PR, 1024), F32),
        scratch_shapes=[pltpu.SemaphoreType.DMA, pltpu.SemaphoreType.DMA],
    )(tot)


def _small_allreduce(mine):
    def body(m_ref, o_ref, slots, send_sems, recv_sems):
        x, y, c = _place()
        me = 4 * x + 2 * y + c
        slots[me] = m_ref[...]
        copies = []
        for r in range(1, 8):
            rx, ry, rc = (r >> 2) & 1, (r >> 1) & 1, r & 1
            peer = (x + rx - 2 * x * rx, y + ry - 2 * y * ry, c + rc - 2 * c * rc)
            cp = pltpu.make_async_remote_copy(
                src_ref=m_ref, dst_ref=slots.at[me], send_sem=send_sems.at[r - 1], recv_sem=recv_sems.at[r - 1],
                device_id=peer, device_id_type=MESH)
            cp.start()
            copies.append(cp)
        for cp in copies:
            cp.wait()
        acc = slots[0]
        for j in range(1, 8):
            acc = acc + slots[j]
        o_ref[...] = acc

    return pl.pallas_call(
        body, name="small_allreduce",
        in_specs=[pl.BlockSpec(memory_space=pltpu.VMEM)], out_specs=pl.BlockSpec(memory_space=pltpu.VMEM),
        out_shape=jax.ShapeDtypeStruct((SMALL_ROWS, LANE), F32),
        scratch_shapes=[pltpu.VMEM((8, SMALL_ROWS, LANE), F32), pltpu.SemaphoreType.DMA((7,)),
                        pltpu.SemaphoreType.DMA((7,))],
    )(mine)


def _seg_matrix():
    i = lax.broadcasted_iota(jnp.int32, (AD, AD), 0) // DH
    j = lax.broadcasted_iota(jnp.int32, (AD, AD), 1) // DH
    return (i == j).astype(BF16)


def _local_step(x, p, tgt, sm, gw):
    seg = _seg_matrix()
    h = x
    saved = []
    for l in range(DEPTH):
        w_in = gw["w_in"][:, l]
        w_out = gw["w_out"][:, l].reshape(D, D)
        w_gate = gw["w_ple_gate"][:, l].reshape(D, D)
        w_ple = gw["w_ple"][:, l]
        w_pw = gw["w_pw"][:, l].reshape(CD, CD)
        dw = gw["dw_w"][:, l]
        row = lambda name: sm[name][l:l + 1]
        u, hn = _rms_inproj(h, row("norm_g"), w_in)
        o, ya, tot = _attn_fwd(u, jnp.tile(row("attn_out_g"), (1, 2)))
        c1 = _glu_conv(u, dw, row("dw_b"))
        c3, yc, h1, gate, pe, h2 = _layer_tail(
            c1, u, ya, h, p[l], row("conv_ln_g"), row("conv_ln_b"), w_pw, row("conv_out_g"), w_out,
            row("ple_norm_g"), w_gate, w_ple)
        saved.append(dict(h=h, u=u, hn=hn, o=o, ya=ya, tot=tot, c1=c1, c3=c3, yc=yc, h1=h1, gate=gate, pe=pe,
                          w_in=w_in, w_out=w_out, w_gate=w_gate, w_pw=w_pw, dw=dw))
        h = h2
    loss_blk, dh, dfg = _loss_head(h, tgt, sm["final_g"])
    big = [None] * DEPTH
    small = [None] * DEPTH
    for l in reversed(range(DEPTH)):
        s = saved[l]
        row = lambda name: sm[name][l:l + 1]
        dh1, dy, dwg, dwp, dwo, dpg = _ple_out_bwd(
            dh, s["h1"], s["gate"], s["pe"], p[l], s["ya"], s["yc"], row("ple_norm_g"), s["w_gate"], s["w_out"])
        ag_t = jnp.tile(row("attn_out_g"), (1, AD // DH))
        do, dga, dgc, dc1, dwpw, dag, dcg, dlg, dlb = _branch_bwd(
            dy, s["o"], s["u"], s["c1"], s["c3"], ag_t, row("conv_ln_g"), row("conv_ln_b"), s["w_pw"],
            row("conv_out_g"), seg)
        dcv, dcgate, ddw, ddb = _conv_bwd(dc1, s["u"], s["dw"])
        dq, dk, dv = _attn_bwd(s["u"], do, s["tot"])
        du = jnp.concatenate([dq, dk, dv, dga, dcv, dcgate, dgc], axis=1).astype(BF16)
        dwin = _inproj_dw(s["hn"], du)
        dh, dng = _inproj_dx(du, s["w_in"], s["h"], row("norm_g"), dh1)
        big[l] = dict(w_in=dwin, w_out=dwo, w_ple_gate=dwg, w_ple=dwp, w_pw=dwpw, dw_w=ddw)
        small[l] = dict(norm_g=dng, attn_out_g=dag.reshape(AD // DH, DH).sum(axis=0, keepdims=True), dw_b=ddb,
                        conv_ln_g=dlg, conv_ln_b=dlb, conv_out_g=dcg, ple_norm_g=dpg)
    return loss_blk[0, 0], dh, big, small, dfg


BIG = ("w_in", "w_out", "w_ple_gate", "w_ple", "w_pw", "dw_w")
SMALL2 = ("norm_g", "ple_norm_g", "dw_b", "conv_ln_g", "conv_ln_b", "conv_out_g", "attn_out_g")


def _pack_big_layer(g):
    parts = [g["w_in"].reshape(NCHIP, 896, 1024), g["w_out"].reshape(NCHIP, 256, 1024),
             g["w_ple_gate"].reshape(NCHIP, 256, 1024), g["w_ple"].reshape(NCHIP, 64, 1024),
             g["w_pw"].reshape(NCHIP, 64, 1024),
             jnp.pad(g["dw_w"].reshape(NCHIP, 4, 1024), ((0, 0), (0, PR - sum(PK_ROWS)), (0, 0)))]
    return jnp.concatenate(parts, axis=1)


def _unpack_big(gfull):
    out, r0 = {}, 0
    shapes = dict(w_in=(D, SHW), w_out=(256, D), w_ple_gate=(256, D), w_ple=(PLE, PLE), w_pw=(128, CD),
                  dw_w=(CWP, LANE))
    for name, rows in zip(BIG, PK_ROWS):
        out[name] = gfull[:, r0:r0 + rows].reshape((DEPTH,) + shapes[name])
        r0 += rows
    return out


def _pack_small(two, final):
    flat = jnp.concatenate([two[k].reshape(-1) for k in SMALL2] + [final.reshape(-1)])
    flat = jnp.concatenate([flat, jnp.zeros((SMALL_ROWS * LANE - flat.shape[0],), F32)])
    return flat.reshape(SMALL_ROWS, LANE)


def _unpack_small(packed, like_two, like_final):
    flat = packed.reshape(-1)
    out, off = {}, 0
    for k in SMALL2:
        n = like_two[k].size
        out[k] = flat[off:off + n].reshape(like_two[k].shape)
        off += n
    return out, flat[off:off + like_final.size].reshape(like_final.shape)


def kernel(x, p, norm_g, w_in, attn_out_g, dw_w, dw_b, conv_ln_g, conv_ln_b, w_pw, conv_out_g, w_out, ple_norm_g, w_ple_gate, w_ple, final_g, loss_target, m_norm_g, m_w_in, m_attn_out_g, m_dw_w, m_dw_b, m_conv_ln_g, m_conv_ln_b, m_w_pw, m_conv_out_g, m_w_out, m_ple_norm_g, m_w_ple_gate, m_w_ple, m_final_g, v_norm_g, v_w_in, v_attn_out_g, v_dw_w, v_dw_b, v_conv_ln_g, v_conv_ln_b, v_w_pw, v_conv_out_g, v_w_out, v_ple_norm_g, v_w_ple_gate, v_w_ple, v_final_g):
    W = dict(norm_g=norm_g, w_in=w_in, attn_out_g=attn_out_g, dw_w=dw_w, dw_b=dw_b, conv_ln_g=conv_ln_g,
             conv_ln_b=conv_ln_b, w_pw=w_pw, conv_out_g=conv_out_g, w_out=w_out, ple_norm_g=ple_norm_g,
             w_ple_gate=w_ple_gate, w_ple=w_ple, final_g=final_g)
    M = dict(norm_g=m_norm_g, w_in=m_w_in, attn_out_g=m_attn_out_g, dw_w=m_dw_w, dw_b=m_dw_b,
             conv_ln_g=m_conv_ln_g, conv_ln_b=m_conv_ln_b, w_pw=m_w_pw, conv_out_g=m_conv_out_g, w_out=m_w_out,
             ple_norm_g=m_ple_norm_g, w_ple_gate=m_w_ple_gate, w_ple=m_w_ple, final_g=m_final_g)
    V = dict(norm_g=v_norm_g, w_in=v_w_in, attn_out_g=v_attn_out_g, dw_w=v_dw_w, dw_b=v_dw_b,
             conv_ln_g=v_conv_ln_g, conv_ln_b=v_conv_ln_b, w_pw=v_w_pw, conv_out_g=v_conv_out_g, w_out=v_w_out,
             ple_norm_g=v_ple_norm_g, w_ple_gate=v_w_ple_gate, w_ple=v_w_ple, final_g=v_final_g)
    order = ("norm_g", "w_in", "attn_out_g", "dw_w", "dw_b", "conv_ln_g", "conv_ln_b", "w_pw", "conv_out_g",
             "w_out", "ple_norm_g", "w_ple_gate", "w_ple", "final_g")

    pad_taps = lambda a: jnp.pad(a, ((0, 0), (0, CWP - CW), (0, 0)))
    shards = [w_in.astype(BF16), w_out.astype(BF16), w_ple_gate.astype(BF16), w_ple.astype(BF16),
              w_pw.astype(BF16), pad_taps(dw_w)]
    xi, yi, ci = lax.axis_index("x"), lax.axis_index("y"), lax.axis_index("c")
    chip = 2 * xi + yi
    gathered = _all_gather_weights(shards)
    gw = {name: lax.dynamic_update_index_in_dim(g, s, chip, 0) for name, g, s in zip(BIG, gathered, shards)}

    sm = {k: W[k] for k in SMALL2}
    sm["final_g"] = final_g.reshape(1, D)
    loss_part, grad_x, big, small, dfg = _local_step(x[0], p[:, 0], loss_target[0], sm, gw)
    loss = lax.psum(loss_part, ("x", "y", "c"))

    gf = jnp.stack([_pack_big_layer(big[l]) for l in range(DEPTH)])
    layer_idx = jnp.reshape(ci, (1,)).astype(jnp.int32)
    chip_idx = jnp.reshape(chip, (1,)).astype(jnp.int32)
    pb = _sum_pair(layer_idx, gf, _pair_exchange(gf.astype(BF16)))
    mine = _sum_chips(chip_idx, pb, _chip_scatter(pb))
    theirs = _pair_share(mine)
    gfull = jnp.where(ci == 0, jnp.stack([mine, theirs]), jnp.stack([theirs, mine]))
    g_big = _unpack_big(gfull)

    small_two = {k: jnp.concatenate([small[l][k] for l in range(DEPTH)], axis=0) for k in SMALL2}
    g_small_packed = _small_allreduce(_pack_small(small_two, dfg))
    g_small, g_final = _unpack_small(g_small_packed, {k: W[k] for k in SMALL2}, final_g)

    grads, deltas, new_m, new_v = {}, {}, {}, {}
    for name in BIG:
        wv = pad_taps(W[name]) if name == "dw_w" else W[name]
        mv = pad_taps(M[name]) if name == "dw_w" else M[name]
        vv = pad_taps(V[name]) if name == "dw_w" else V[name]
        gg = g_big[name]
        cols = wv.shape[-1]
        rows_total = wv.size // cols
        tile_rows = min(rows_total, 256)
        d2, m2, v2 = _adamw(wv.reshape(rows_total, cols), gg.reshape(rows_total, cols),
                            mv.reshape(rows_total, cols), vv.reshape(rows_total, cols), tile_rows)
        if name == "dw_w":
            cut = lambda a: a.reshape(DEPTH, CWP, LANE)[:, :CW]
            grads[name], deltas[name], new_m[name], new_v[name] = cut(gg), cut(d2), cut(m2), cut(v2)
        else:
            grads[name] = gg
            deltas[name], new_m[name], new_v[name] = (t.reshape(wv.shape) for t in (d2, m2, v2))
    ws = _pack_small({k: W[k] for k in SMALL2}, final_g)
    ms = _pack_small({k: M[k] for k in SMALL2}, m_final_g)
    vs = _pack_small({k: V[k] for k in SMALL2}, v_final_g)
    ds, nms, nvs = _adamw(ws, g_small_packed, ms, vs, SMALL_ROWS)
    for packed, dst in ((ds, deltas), (nms, new_m), (nvs, new_v)):
        two, fin = _unpack_small(packed, {k: W[k] for k in SMALL2}, final_g)
        dst.update(two)
        dst["final_g"] = fin
    grads.update(g_small)
    grads["final_g"] = g_final

    return (loss, grad_x[None], *[grads[n] for n in order], *[deltas[n] for n in order],
            *[new_m[n] for n in order], *[new_v[n] for n in order])
```

```python
import functools

import jax
import jax.numpy as jnp
from jax import lax
from jax.experimental import pallas as pl
from jax.experimental.pallas import tpu as pltpu

F32 = jnp.float32
BF16 = jnp.bfloat16

T = 2048
D = 1024
DIN = 3584
NCHIP = 4
SHW = DIN // NCHIP
AD = 512
CD = 512
DH = 64
CW = 31
CWP = 32
PLE = 256
DEPTH = 2
EPS = 1e-6
AQ = 256
LANE = 128

ADAM_LR = 0.001
ADAM_B1 = 0.9
ADAM_B2 = 0.999
ADAM_EPS = 1e-08
ADAM_WD = 0.01
ADAM_STEP = 10

PK_ROWS = (896, 256, 256, 64, 64, 4)
PR = 1568
SMALL_ROWS = 80

VMEM_BIG = 56 * 1024 * 1024
MESH = pl.DeviceIdType.MESH


def _cp(sem=None, vmem=None):
    kw = {}
    if sem is not None:
        kw["dimension_semantics"] = sem
    if vmem is not None:
        kw["vmem_limit_bytes"] = vmem
    return pltpu.CompilerParams(**kw)


def _dot(a, b):
    return jnp.dot(a, b, preferred_element_type=F32)


def _dot_nt(a, b):
    return lax.dot_general(a, b, (((1,), (1,)), ((), ())), preferred_element_type=F32)


def _dot_tn(a, b):
    return lax.dot_general(a, b, (((0,), (0,)), ((), ())), preferred_element_type=F32)


def _dot2(x, m):
    hi = x.astype(BF16)
    lo = (x - hi.astype(F32)).astype(BF16)
    return _dot(hi, m) + _dot(lo, m)


def _sig(x):
    return 1.0 / (1.0 + jnp.exp(-x))


def _softplus(z):
    return jnp.maximum(z, 0.0) + jnp.log(1.0 + jnp.exp(-jnp.abs(z)))


def _rstd(x):
    return lax.rsqrt(jnp.mean(x * x, axis=-1, keepdims=True) + EPS)


def _rms_bwd(dy, x, r, g):
    dn = dy * g
    return r * dn - x * (r * r * r) * jnp.mean(dn * x, axis=-1, keepdims=True)


def _rms_inproj(h, g, w):
    tm = 512

    def body(h_ref, g_ref, w_ref, u_ref, hn_ref, hn_s):
        @pl.when(pl.program_id(1) == 0)
        def _():
            hh = h_ref[...]
            hn = (hh * _rstd(hh) * g_ref[...]).astype(BF16)
            hn_s[...] = hn
            hn_ref[...] = hn
        u_ref[...] = _dot(hn_s[...], w_ref[0])

    return pl.pallas_call(
        body, name="rms_inproj", grid=(T // tm, NCHIP),
        in_specs=[pl.BlockSpec((tm, D), lambda i, k: (i, 0)),
                  pl.BlockSpec((1, D), lambda i, k: (0, 0)),
                  pl.BlockSpec((1, D, SHW), lambda i, k: (k, 0, 0))],
        out_specs=[pl.BlockSpec((tm, SHW), lambda i, k: (i, k)),
                   pl.BlockSpec((tm, D), lambda i, k: (i, 0))],
        out_shape=[jax.ShapeDtypeStruct((T, DIN), F32), jax.ShapeDtypeStruct((T, D), BF16)],
        scratch_shapes=[pltpu.VMEM((tm, D), BF16)],
        compiler_params=_cp(("arbitrary", "arbitrary")),
    )(h, g, w)


def _attn_tiles():
    row = lax.broadcasted_iota(jnp.int32, (2 * AQ, AQ), 0) & (AQ - 1)
    col = lax.broadcasted_iota(jnp.int32, (2 * AQ, AQ), 1)
    tr = lax.broadcasted_iota(jnp.int32, (AQ, AQ), 0)
    tc = lax.broadcasted_iota(jnp.int32, (AQ, AQ), 1)
    first = lax.broadcasted_iota(jnp.int32, (1, LANE), 1) < DH
    return col < row, tr, tc, first


def _stack_heads(t, first):
    return jnp.concatenate([jnp.where(first, t, 0.0), jnp.where(first, 0.0, t)], axis=0)


def _unstack_heads(t, first):
    return jnp.where(first, t[:AQ], t[AQ:])


def _tri_sum(x, tri):
    hi = x.astype(BF16)
    lo = (x - hi.astype(F32)).astype(BF16)
    both = _dot(jnp.concatenate([hi, lo], axis=0), tri)
    return both[:2 * AQ] + both[2 * AQ:]


def _gather_copies(ins, outs, send_sems, recv_sems):
    x, y, c = _place()
    me = 2 * x + y
    peers = [(1 - x, y, c), (x, 1 - y, c), (1 - x, 1 - y, c)]
    return [pltpu.make_async_remote_copy(
        src_ref=ins[a], dst_ref=outs[a].at[me], send_sem=send_sems.at[3 * a + r], recv_sem=recv_sems.at[3 * a + r],
        device_id=peer, device_id_type=MESH) for a in range(len(ins)) for r, peer in enumerate(peers)]


def _attn_fwd(u, ag2, shards=()):
    n = len(shards)
    nhp, nq = AD // LANE, T // AQ

    def body(*refs):
        q_ref, k_ref, v_ref, g_ref, ag_ref = refs[:5]
        sh_refs = refs[5:5 + n]
        o_ref, y_ref, tot_ref = refs[5 + n:8 + n]
        ga_refs = refs[8 + n:8 + 2 * n]
        kb_s, vb_s = refs[8 + 2 * n:10 + 2 * n]
        qi = pl.program_id(1)
        if n:
            copies = _gather_copies(sh_refs, ga_refs, *refs[10 + 2 * n:])

            @pl.when(jnp.logical_and(pl.program_id(0) == 0, qi == 0))
            def _():
                for cp in copies:
                    cp.start()

        @pl.when(qi == 0)
        def _():
            kb_s[...] = k_ref[...].astype(BF16)
            vb_s[...] = v_ref[...].astype(BF16)

        causal, tr, tc, first = _attn_tiles()
        upper = (tr > tc).astype(BF16)
        qs = _stack_heads(q_ref[...] * 0.125, first).astype(BF16)

        def block(kb, carry, masked):
            run, acc = carry
            k0 = pl.multiple_of(kb * AQ, AQ)
            kk = kb_s[pl.ds(k0, AQ), :]
            vv = vb_s[pl.ds(k0, AQ), :]
            z = _dot_nt(qs, kk)
            sp = _softplus(z)
            lm = jnp.where(causal, -sp, 0.0) if masked else -sp
            att = jnp.exp((z - sp) + _tri_sum(lm, upper) + run)
            if masked:
                att = jnp.where(causal, att, 0.0)
            acc = acc + _dot(att.astype(BF16), vv)
            return run + jnp.sum(lm, axis=1, keepdims=True), acc

        carry = block(qi, (jnp.zeros((2 * AQ, 1), F32), jnp.zeros((2 * AQ, LANE), F32)), True)
        run, acc = lax.fori_loop(0, qi, lambda i, c: block(qi - 1 - i, c, False), carry)
        o = _unstack_heads(acc, first)
        osq = o * o
        ms_a = jnp.sum(jnp.where(first, osq, 0.0), axis=1, keepdims=True)
        ms_b = jnp.sum(jnp.where(first, 0.0, osq), axis=1, keepdims=True)
        r = lax.rsqrt(jnp.where(first, ms_a, ms_b) * (1.0 / DH) + EPS)
        g = g_ref[...]
        o_ref[...] = o
        y_ref[...] = (o * r * ag_ref[...] * (g * _sig(g))).astype(BF16)
        tot_ref[0] = jnp.where(first, run[:AQ], run[AQ:])
        if n:
            @pl.when(jnp.logical_and(pl.program_id(0) == nhp - 1, qi == nq - 1))
            def _():
                for cp in copies:
                    cp.wait()

    sems = [pltpu.SemaphoreType.DMA((3 * n,)), pltpu.SemaphoreType.DMA((3 * n,))] if n else []
    res = pl.pallas_call(
        body, name="attn_fwd", grid=(nhp, nq),
        in_specs=[pl.BlockSpec((AQ, LANE), lambda hp, qi: (qi, hp)),
                  pl.BlockSpec((T, LANE), lambda hp, qi: (0, 4 + hp)),
                  pl.BlockSpec((T, LANE), lambda hp, qi: (0, 8 + hp)),
                  pl.BlockSpec((AQ, LANE), lambda hp, qi: (qi, 12 + hp)),
                  pl.BlockSpec((1, LANE), lambda hp, qi: (0, 0))] + [HBM_SPEC] * n,
        out_specs=[pl.BlockSpec((AQ, LANE), lambda hp, qi: (qi, hp)),
                   pl.BlockSpec((AQ, LANE), lambda hp, qi: (qi, hp)),
                   pl.BlockSpec((1, AQ, LANE), lambda hp, qi: (hp, qi, 0))] + [HBM_SPEC] * n,
        out_shape=[jax.ShapeDtypeStruct((T, AD), F32), jax.ShapeDtypeStruct((T, AD), BF16),
                   jax.ShapeDtypeStruct((nhp, T, LANE), F32)]
        + [jax.ShapeDtypeStruct((NCHIP,) + s.shape, s.dtype) for s in shards],
        scratch_shapes=[pltpu.VMEM((T, LANE), BF16), pltpu.VMEM((T, LANE), BF16)] + sems,
        compiler_params=_cp(("arbitrary", "arbitrary"), VMEM_BIG),
    )(u, u, u, u, ag2, *shards)
    return res[0], res[1], res[2], list(res[3:])


def _glu_conv(u, dw, db):
    tr = 256

    def body(cv_ref, cg_ref, w_ref, b_ref, c1_ref, pad_s):
        pad_s[pl.ds(0, CWP), :] = jnp.zeros((CWP, LANE), F32)
        pad_s[pl.ds(CWP, T), :] = cv_ref[...] * _sig(cg_ref[...])
        wv = w_ref[0]
        bias = b_ref[...]

        def tile(i, carry):
            r0 = pl.multiple_of(i * tr, tr)
            acc = jnp.zeros((tr, LANE), F32) + bias
            for w in range(CW):
                acc = acc + pad_s[pl.ds(r0 + (CWP - CW + 1) + w, tr), :] * wv[w:w + 1, :]
            c1_ref[pl.ds(r0, tr), :] = acc
            return carry

        lax.fori_loop(0, T // tr, tile, 0)

    return pl.pallas_call(
        body, name="glu_conv", grid=(CD // LANE,),
        in_specs=[pl.BlockSpec((T, LANE), lambda cb: (0, 16 + cb)),
                  pl.BlockSpec((T, LANE), lambda cb: (0, 20 + cb)),
                  pl.BlockSpec((1, CWP, LANE), lambda cb: (cb, 0, 0)),
                  pl.BlockSpec((1, LANE), lambda cb: (0, cb))],
        out_specs=pl.BlockSpec((T, LANE), lambda cb: (0, cb)),
        out_shape=jax.ShapeDtypeStruct((T, CD), F32),
        scratch_shapes=[pltpu.VMEM((T + CWP, LANE), F32)],
        compiler_params=_cp(("arbitrary",)),
    )(u, u, dw, db)


def _ln_silu(c1, lg, lb):
    mu = jnp.mean(c1, axis=-1, keepdims=True)
    xc = c1 - mu
    rs = lax.rsqrt(jnp.mean(xc * xc, axis=-1, keepdims=True) + EPS)
    xh = xc * rs
    ln = xh * lg + lb
    s = _sig(ln)
    return xh, rs, ln, s


def _layer_tail(c1, u, ya, h, p, lg, lb, wpw, cg, wout, pg, wgate, wple):
    tm = 256

    def body(c1_ref, gc_ref, ya_ref, h_ref, p_ref, lg_ref, lb_ref, wpw_ref, cg_ref, wout_ref,
             pg_ref, wgate_ref, wple_ref, c3_ref, yc_ref, h1_ref, gate_ref, pe_ref, h2_ref):
        _, _, ln, s = _ln_silu(c1_ref[...], lg_ref[...], lb_ref[...])
        c2 = (ln * s).astype(BF16)
        c3 = _dot(c2, wpw_ref[...])
        gc = gc_ref[...]
        yc = (c3 * _rstd(c3) * cg_ref[...] * (gc * _sig(gc))).astype(BF16)
        c3_ref[...] = c3
        yc_ref[...] = yc
        y = _dot(ya_ref[...], wout_ref[pl.ds(0, AD), :]) + _dot(yc, wout_ref[pl.ds(AD, CD), :])
        h1 = h_ref[...] + y
        hn2 = (h1 * _rstd(h1) * pg_ref[...]).astype(BF16)
        gate = _sig(_dot(hn2, wgate_ref[...]))
        pb = p_ref[...].astype(BF16)
        pe = jnp.concatenate([_dot(pb, wple_ref[k]) for k in range(NCHIP)], axis=1)
        h1_ref[...] = h1
        gate_ref[...] = gate
        pe_ref[...] = pe
        h2_ref[...] = h1 + pe * gate

    row = lambda w: pl.BlockSpec((tm, w), lambda i: (i, 0))
    full = lambda *s: pl.BlockSpec(s, lambda i: (0,) * len(s))
    return pl.pallas_call(
        body, name="layer_tail", grid=(T // tm,),
        in_specs=[row(CD), pl.BlockSpec((tm, CD), lambda i: (i, 6)), row(AD), row(D), row(PLE),
                  full(1, CD), full(1, CD), full(CD, CD), full(1, CD), full(D, D),
                  full(1, D), full(D, D), full(NCHIP, PLE, PLE)],
        out_specs=[row(CD), row(CD), row(D), row(D), row(D), row(D)],
        out_shape=[jax.ShapeDtypeStruct((T, CD), F32), jax.ShapeDtypeStruct((T, CD), BF16)]
        + [jax.ShapeDtypeStruct((T, D), F32)] * 4,
        compiler_params=_cp(("arbitrary",), VMEM_BIG),
    )(c1, u, ya, h, p, lg, lb, wpw, cg, wout, pg, wgate, wple)


def _loss_head(h, tgt, fg):
    tm = 256

    def body(h_ref, t_ref, g_ref, loss_ref, dh_ref, dg_ref):
        @pl.when(pl.program_id(0) == 0)
        def _():
            loss_ref[...] = jnp.zeros_like(loss_ref)
            dg_ref[...] = jnp.zeros_like(dg_ref)
        hh = h_ref[...]
        g = g_ref[...]
        r = _rstd(hh)
        e = hh * r * g - t_ref[...]
        loss_ref[...] += 0.5 * jnp.sum(jnp.mean(e * e, axis=-1, keepdims=True))
        dy = e * (1.0 / D)
        dg_ref[...] += jnp.sum(dy * hh * r, axis=0, keepdims=True)
        dh_ref[...] = _rms_bwd(dy, hh, r, g)

    return pl.pallas_call(
        body, name="loss_head", grid=(T // tm,),
        in_specs=[pl.BlockSpec((tm, D), lambda i: (i, 0)), pl.BlockSpec((tm, D), lambda i: (i, 0)),
                  pl.BlockSpec((1, D), lambda i: (0, 0))],
        out_specs=[pl.BlockSpec((8, LANE), lambda i: (0, 0)), pl.BlockSpec((tm, D), lambda i: (i, 0)),
                   pl.BlockSpec((1, D), lambda i: (0, 0))],
        out_shape=[jax.ShapeDtypeStruct((8, LANE), F32), jax.ShapeDtypeStruct((T, D), F32),
                   jax.ShapeDtypeStruct((1, D), F32)],
        compiler_params=_cp(("arbitrary",)),
    )(h, tgt, fg)


def _ple_out_bwd(dh2, h1, gate, pe, p, ya, yc, pg, wgate, wout):
    tm = 256

    def body(dh2_ref, h1_ref, gate_ref, pe_ref, p_ref, ya_ref, yc_ref, pg_ref, wgate_ref, wout_ref,
             dh1_ref, dy_ref, dwg_ref, dwp_ref, dwo_ref, dpg_ref):
        @pl.when(pl.program_id(0) == 0)
        def _():
            dwg_ref[...] = jnp.zeros_like(dwg_ref)
            dwp_ref[...] = jnp.zeros_like(dwp_ref)
            dwo_ref[...] = jnp.zeros_like(dwo_ref)
            dpg_ref[...] = jnp.zeros_like(dpg_ref)
        dh2 = dh2_ref[...]
        h1 = h1_ref[...]
        gate = gate_ref[...]
        pg = pg_ref[...]
        dpe = (dh2 * gate).astype(BF16)
        dgp = (dh2 * pe_ref[...] * gate * (1.0 - gate)).astype(BF16)
        r = _rstd(h1)
        hn = h1 * r
        dwg_ref[...] += _dot_tn((hn * pg).astype(BF16), dgp)
        dhn2 = _dot_nt(dgp, wgate_ref[...])
        dpg_ref[...] += jnp.sum(dhn2 * hn, axis=0, keepdims=True)
        dh1 = dh2 + _rms_bwd(dhn2, h1, r, pg)
        pb = p_ref[...].astype(BF16)
        for k in range(NCHIP):
            dwp_ref[k] += _dot_tn(pb, dpe[:, k * PLE:(k + 1) * PLE])
        dh1b = dh1.astype(BF16)
        dy_ref[...] = _dot_nt(dh1b, wout_ref[...])
        dwo_ref[pl.ds(0, AD), :] += _dot_tn(ya_ref[...], dh1b)
        dwo_ref[pl.ds(AD, CD), :] += _dot_tn(yc_ref[...], dh1b)
        dh1_ref[...] = dh1

    row = lambda w: pl.BlockSpec((tm, w), lambda i: (i, 0))
    full = lambda *s: pl.BlockSpec(s, lambda i: (0,) * len(s))
    return pl.pallas_call(
        body, name="ple_out_bwd", grid=(T // tm,),
        in_specs=[row(D), row(D), row(D), row(D), row(PLE), row(AD), row(CD),
                  full(1, D), full(D, D), full(D, D)],
        out_specs=[row(D), row(D), full(D, D), full(NCHIP, PLE, PLE), full(D, D), full(1, D)],
        out_shape=[jax.ShapeDtypeStruct((T, D), F32), jax.ShapeDtypeStruct((T, D), F32),
                   jax.ShapeDtypeStruct((D, D), F32), jax.ShapeDtypeStruct((NCHIP, PLE, PLE), F32),
                   jax.ShapeDtypeStruct((D, D), F32), jax.ShapeDtypeStruct((1, D), F32)],
        compiler_params=_cp(("arbitrary",), VMEM_BIG),
    )(dh2, h1, gate, pe, p, ya, yc, pg, wgate, wout)


def _branch_bwd(dy, o, u, c1, c3, ag, lg, lb, wpw, cg, seg):
    tm = 256

    def body(dya_ref, dyc_ref, o_ref, ga_ref, gc_ref, c1_ref, c3_ref, ag_ref, lg_ref, lb_ref, wpw_ref,
             cg_ref, seg_ref, do_ref, dga_ref, dgc_ref, dc1_ref, dwpw_ref, dag_ref, dcg_ref, dlg_ref, dlb_ref):
        @pl.when(pl.program_id(0) == 0)
        def _():
            for r_ in (dwpw_ref, dag_ref, dcg_ref, dlg_ref, dlb_ref):
                r_[...] = jnp.zeros_like(r_)
        dya = dya_ref[...]
        o = o_ref[...]
        ga = ga_ref[...]
        ag_v = ag_ref[...]
        seg_m = seg_ref[...]
        r = lax.rsqrt(_dot2(o * o, seg_m) * (1.0 / DH) + EPS)
        onr = o * r
        sg = _sig(ga)
        dga_ref[...] = dya * (onr * ag_v) * (sg * (1.0 + ga * (1.0 - sg)))
        don = dya * (ga * sg)
        dag_ref[...] += jnp.sum(don * onr, axis=0, keepdims=True)
        dn = don * ag_v
        do_ref[...] = r * dn - o * (r * r * r) * (_dot2(dn * o, seg_m) * (1.0 / DH))
        dyc = dyc_ref[...]
        c3 = c3_ref[...]
        gc = gc_ref[...]
        cg_v = cg_ref[...]
        r3 = _rstd(c3)
        cn = c3 * r3
        sc = _sig(gc)
        dgc_ref[...] = dyc * (cn * cg_v) * (sc * (1.0 + gc * (1.0 - sc)))
        dcn = dyc * (gc * sc)
        dcg_ref[...] += jnp.sum(dcn * cn, axis=0, keepdims=True)
        dc3 = _rms_bwd(dcn, c3, r3, cg_v).astype(BF16)
        lg_v = lg_ref[...]
        xh, rs, ln, s = _ln_silu(c1_ref[...], lg_v, lb_ref[...])
        c2 = (ln * s).astype(BF16)
        dwpw_ref[...] += _dot_tn(c2, dc3)
        dc2 = _dot_nt(dc3, wpw_ref[...])
        dln = dc2 * (s * (1.0 + ln * (1.0 - s)))
        dlb_ref[...] += jnp.sum(dln, axis=0, keepdims=True)
        dlg_ref[...] += jnp.sum(dln * xh, axis=0, keepdims=True)
        dxh = dln * lg_v
        dc1_ref[...] = rs * (dxh - jnp.mean(dxh, axis=-1, keepdims=True)
                             - xh * jnp.mean(dxh * xh, axis=-1, keepdims=True))

    half = lambda j: pl.BlockSpec((tm, 512), lambda i: (i, j))
    full = lambda *s: pl.BlockSpec(s, lambda i: (0,) * len(s))
    vec = jax.ShapeDtypeStruct((1, 512), F32)
    act = jax.ShapeDtypeStruct((T, 512), F32)
    return pl.pallas_call(
        body, name="branch_bwd", grid=(T // tm,),
        in_specs=[half(0), half(1), half(0), half(3), half(6), half(0), half(0),
                  full(1, AD), full(1, CD), full(1, CD), full(CD, CD), full(1, CD), full(AD, AD)],
        out_specs=[half(0), half(0), half(0), half(0), full(CD, CD), full(1, 512), full(1, 512),
                   full(1, 512), full(1, 512)],
        out_shape=[act, act, act, act, jax.ShapeDtypeStruct((CD, CD), F32), vec, vec, vec, vec],
        compiler_params=_cp(("arbitrary",), VMEM_BIG),
    )(dy, dy, o, u, u, c1, c3, ag, lg, lb, wpw, cg, seg)


def _conv_bwd(dc1, u, dw):
    tr = 256
    off = CWP - CW + 1

    def body(d_ref, cv_ref, cg_ref, w_ref, dcv_ref, dcg_ref, ddw_ref, ddb_ref, padc_s, padd_s, acc_s):
        cv = cv_ref[...]
        sg = _sig(cg_ref[...])
        padc_s[pl.ds(0, CWP), :] = jnp.zeros((CWP, LANE), F32)
        padc_s[pl.ds(CWP, T), :] = cv * sg
        padd_s[pl.ds(0, T), :] = d_ref[...]
        padd_s[pl.ds(T, CWP), :] = jnp.zeros((CWP, LANE), F32)
        acc_s[...] = jnp.zeros_like(acc_s)
        wv = w_ref[0]

        def tile(i, carry):
            r0 = pl.multiple_of(i * tr, tr)
            dt = padd_s[pl.ds(r0, tr), :]
            dc0 = jnp.zeros((tr, LANE), F32)
            for w in range(CW):
                dc0 = dc0 + padd_s[pl.ds(r0 + (CW - 1) - w, tr), :] * wv[w:w + 1, :]
                prod = dt * padc_s[pl.ds(r0 + off + w, tr), :]
                acc_s[w] += jnp.sum(prod.reshape(tr // 8, 8, LANE), axis=0)
            cvt = cv_ref[pl.ds(r0, tr), :]
            sgt = _sig(cg_ref[pl.ds(r0, tr), :])
            dcv_ref[pl.ds(r0, tr), :] = dc0 * sgt
            dcg_ref[pl.ds(r0, tr), :] = dc0 * cvt * sgt * (1.0 - sgt)
            return carry

        lax.fori_loop(0, T // tr, tile, 0)
        ddw_ref[0] = jnp.sum(acc_s[...], axis=1)
        ddb_ref[...] = jnp.sum(d_ref[...], axis=0, keepdims=True)

    col = lambda j: pl.BlockSpec((T, LANE), lambda cb: (0, j + cb))
    return pl.pallas_call(
        body, name="conv_bwd", grid=(CD // LANE,),
        in_specs=[col(0), col(16), col(20), pl.BlockSpec((1, CWP, LANE), lambda cb: (cb, 0, 0))],
        out_specs=[col(0), col(0), pl.BlockSpec((1, CWP, LANE), lambda cb: (cb, 0, 0)),
                   pl.BlockSpec((1, LANE), lambda cb: (0, cb))],
        out_shape=[jax.ShapeDtypeStruct((T, CD), F32), jax.ShapeDtypeStruct((T, CD), F32),
                   jax.ShapeDtypeStruct((NCHIP, CWP, LANE), F32), jax.ShapeDtypeStruct((1, CD), F32)],
        scratch_shapes=[pltpu.VMEM((T + CWP, LANE), F32), pltpu.VMEM((T + CWP, LANE), F32),
                        pltpu.VMEM((CWP, 8, LANE), F32)],
        compiler_params=_cp(("arbitrary",)),
    )(dc1, u, u, dw)


def _attn_bwd(u, do, tot):
    def body(q_ref, k_ref, v_ref, do_ref, tot_ref, dq_ref, dk_ref, dv_ref, kb_s, vb_s):
        qi = pl.program_id(1)

        @pl.when(qi == 0)
        def _():
            kb_s[...] = k_ref[...].astype(BF16)
            vb_s[...] = v_ref[...].astype(BF16)
            dk_ref[...] = jnp.zeros_like(dk_ref)
            dv_ref[...] = jnp.zeros_like(dv_ref)

        causal, tr, tc, first = _attn_tiles()
        upper = (tr > tc).astype(BF16)
        lower = (tr < tc).astype(BF16)
        q = q_ref[...]
        qs = _stack_heads(q * 0.125, first).astype(BF16)
        qus = _stack_heads(q, first).astype(BF16)
        dos = _stack_heads(do_ref[...], first).astype(BF16)
        totv = tot_ref[0]
        tots = jnp.concatenate([totv[:, 0:1], totv[:, DH:DH + 1]], axis=0)

        def block(kb, carry, masked):
            lm_left, dl_left, dq = carry
            k0 = pl.multiple_of(kb * AQ, AQ)
            kk = kb_s[pl.ds(k0, AQ), :]
            vv = vb_s[pl.ds(k0, AQ), :]
            z = _dot_nt(qs, kk)
            sp = _softplus(z)
            lm = jnp.where(causal, -sp, 0.0) if masked else -sp
            lm_incl = lm_left + jnp.sum(lm, axis=1, keepdims=True)
            att = jnp.exp((z - sp) + _tri_sum(lm, upper) + (tots - lm_incl))
            if masked:
                att = jnp.where(causal, att, 0.0)
            dl = att * _dot_nt(dos, vv)
            dv_ref[pl.ds(k0, AQ), :] += _dot_tn(att.astype(BF16), dos)
            prefix = dl_left + _tri_sum(dl, lower)
            beta = jnp.exp(z - sp)
            dz = (1.0 - beta) * dl - beta * prefix
            if masked:
                dz = jnp.where(causal, dz, 0.0)
            dzs = (dz * 0.125).astype(BF16)
            dk_ref[pl.ds(k0, AQ), :] += _dot_tn(dzs, qus)
            return lm_incl, dl_left + jnp.sum(dl, axis=1, keepdims=True), dq + _dot(dzs, kk)

        zero = jnp.zeros((2 * AQ, 1), F32)
        carry = lax.fori_loop(0, qi, lambda kb, c: block(kb, c, False),
                              (zero, zero, jnp.zeros((2 * AQ, LANE), F32)))
        _, _, dq = block(qi, carry, True)
        dq_ref[...] = _unstack_heads(dq, first)

    return pl.pallas_call(
        body, name="attn_bwd", grid=(AD // LANE, T // AQ),
        in_specs=[pl.BlockSpec((AQ, LANE), lambda hp, qi: (qi, hp)),
                  pl.BlockSpec((T, LANE), lambda hp, qi: (0, 4 + hp)),
                  pl.BlockSpec((T, LANE), lambda hp, qi: (0, 8 + hp)),
                  pl.BlockSpec((AQ, LANE), lambda hp, qi: (qi, hp)),
                  pl.BlockSpec((1, AQ, LANE), lambda hp, qi: (hp, qi, 0))],
        out_specs=[pl.BlockSpec((AQ, LANE), lambda hp, qi: (qi, hp)),
                   pl.BlockSpec((T, LANE), lambda hp, qi: (0, hp)),
                   pl.BlockSpec((T, LANE), lambda hp, qi: (0, hp))],
        out_shape=[jax.ShapeDtypeStruct((T, AD), F32)] * 3,
        scratch_shapes=[pltpu.VMEM((T, LANE), BF16), pltpu.VMEM((T, LANE), BF16)],
        compiler_params=_cp(("arbitrary", "arbitrary"), VMEM_BIG),
    )(u, u, u, do, tot)


def _inproj_dw(hn, du):
    tm = 512

    def body(hn_ref, du_ref, dw_ref):
        @pl.when(pl.program_id(1) == 0)
        def _():
            dw_ref[...] = jnp.zeros_like(dw_ref)
        dw_ref[0] += _dot_tn(hn_ref[...], du_ref[...])

    return pl.pallas_call(
        body, name="inproj_dw", grid=(NCHIP, T // tm),
        in_specs=[pl.BlockSpec((tm, D), lambda k, i: (i, 0)), pl.BlockSpec((tm, SHW), lambda k, i: (i, k))],
        out_specs=pl.BlockSpec((1, D, SHW), lambda k, i: (k, 0, 0)),
        out_shape=jax.ShapeDtypeStruct((NCHIP, D, SHW), F32),
        compiler_params=_cp(("arbitrary", "arbitrary")),
    )(hn, du)


def _inproj_dx(du, w, h, g, dres):
    tm = 512

    def body(du_ref, w_ref, h_ref, g_ref, dres_ref, dh_ref, dg_ref, acc_s):
        i, k = pl.program_id(0), pl.program_id(1)

        @pl.when(jnp.logical_and(i == 0, k == 0))
        def _():
            dg_ref[...] = jnp.zeros_like(dg_ref)

        part = _dot_nt(du_ref[...], w_ref[0])

        @pl.when(k == 0)
        def _():
            acc_s[...] = part

        @pl.when(k > 0)
        def _():
            acc_s[...] += part

        @pl.when(k == NCHIP - 1)
        def _():
            hh = h_ref[...]
            r = _rstd(hh)
            dhn = acc_s[...]
            dg_ref[...] += jnp.sum(dhn * hh * r, axis=0, keepdims=True)
            dh_ref[...] = dres_ref[...] + _rms_bwd(dhn, hh, r, g_ref[...])

    return pl.pallas_call(
        body, name="inproj_dx", grid=(T // tm, NCHIP),
        in_specs=[pl.BlockSpec((tm, SHW), lambda i, k: (i, k)),
                  pl.BlockSpec((1, D, SHW), lambda i, k: (k, 0, 0)),
                  pl.BlockSpec((tm, D), lambda i, k: (i, 0)),
                  pl.BlockSpec((1, D), lambda i, k: (0, 0)),
                  pl.BlockSpec((tm, D), lambda i, k: (i, 0))],
        out_specs=[pl.BlockSpec((tm, D), lambda i, k: (i, 0)), pl.BlockSpec((1, D), lambda i, k: (0, 0))],
        out_shape=[jax.ShapeDtypeStruct((T, D), F32), jax.ShapeDtypeStruct((1, D), F32)],
        scratch_shapes=[pltpu.VMEM((tm, D), F32)],
        compiler_params=_cp(("arbitrary", "arbitrary")),
    )(du, w, h, g, dres)


def _sum_pair(layer, gf, got):
    tr = PR // 2

    def body(l_ref, a_ref, b_ref, o_ref):
        o_ref[...] = (a_ref[0] + b_ref[...].astype(F32)).astype(BF16)

    spec = pl.BlockSpec((1, tr, 1024), lambda k, i, l: (k, i, 0))
    return pl.pallas_call(
        body, name="sum_pair",
        grid_spec=pltpu.PrefetchScalarGridSpec(
            num_scalar_prefetch=1, grid=(NCHIP, 2),
            in_specs=[pl.BlockSpec((1, 1, tr, 1024), lambda k, i, l: (l[0], k, i, 0)), spec],
            out_specs=spec),
        out_shape=jax.ShapeDtypeStruct((NCHIP, PR, 1024), BF16),
        compiler_params=_cp(("arbitrary", "arbitrary"), VMEM_BIG),
    )(layer, gf, got)


def _sum_chips(chip, pb, got):
    tr = PR // 2

    def body(c_ref, own_ref, r_ref, o_ref):
        acc = own_ref[0].astype(F32)
        for j in range(NCHIP - 1):
            acc = acc + r_ref[j].astype(F32)
        o_ref[...] = acc

    return pl.pallas_call(
        body, name="sum_chips",
        grid_spec=pltpu.PrefetchScalarGridSpec(
            num_scalar_prefetch=1, grid=(2,),
            in_specs=[pl.BlockSpec((1, tr, 1024), lambda i, c: (c[0], i, 0)),
                      pl.BlockSpec((NCHIP - 1, tr, 1024), lambda i, c: (0, i, 0))],
            out_specs=pl.BlockSpec((tr, 1024), lambda i, c: (i, 0))),
        out_shape=jax.ShapeDtypeStruct((PR, 1024), F32),
        compiler_params=_cp(("arbitrary",), VMEM_BIG),
    )(chip, pb, got)


def _adamw(w, g, m, v, rows):
    R, C = w.shape
    c1 = 1.0 - ADAM_B1 ** ADAM_STEP
    c2 = 1.0 - ADAM_B2 ** ADAM_STEP

    def body(w_ref, g_ref, m_ref, v_ref, d_ref, nm_ref, nv_ref):
        gg = g_ref[...]
        nm = ADAM_B1 * m_ref[...] + (1.0 - ADAM_B1) * gg
        nv = ADAM_B2 * v_ref[...] + (1.0 - ADAM_B2) * (gg * gg)
        d_ref[...] = -ADAM_LR * ((nm / c1) / (jnp.sqrt(nv / c2) + ADAM_EPS) + ADAM_WD * w_ref[...])
        nm_ref[...] = nm
        nv_ref[...] = nv

    spec = pl.BlockSpec((rows, C), lambda i: (i, 0))
    sh = jax.ShapeDtypeStruct((R, C), F32)
    return pl.pallas_call(
        body, name="adamw", grid=(R // rows,), in_specs=[spec] * 4, out_specs=[spec] * 3,
        out_shape=[sh, sh, sh], compiler_params=_cp(("arbitrary",)),
    )(w, g, m, v)


HBM_SPEC = pl.BlockSpec(memory_space=pltpu.HBM)


def _place():
    return lax.axis_index("x"), lax.axis_index("y"), lax.axis_index("c")


def _all_gather_weights(shards):
    n = len(shards)

    def body(*refs):
        copies = _gather_copies(refs[:n], refs[n:2 * n], *refs[2 * n:])
        for cp in copies:
            cp.start()
        for cp in copies:
            cp.wait()

    return pl.pallas_call(
        body, name="all_gather_weights",
        in_specs=[HBM_SPEC] * n, out_specs=[HBM_SPEC] * n,
        out_shape=[jax.ShapeDtypeStruct((NCHIP,) + s.shape, s.dtype) for s in shards],
        scratch_shapes=[pltpu.SemaphoreType.DMA((3 * n,)), pltpu.SemaphoreType.DMA((3 * n,))],
    )(*shards)


def _pair_exchange(gb):
    def body(gb_ref, got_ref, send_sem, recv_sem):
        x, y, c = _place()
        cp = pltpu.make_async_remote_copy(
            src_ref=gb_ref.at[1 - c], dst_ref=got_ref, send_sem=send_sem, recv_sem=recv_sem,
            device_id=(x, y, 1 - c), device_id_type=MESH)
        cp.start()
        cp.wait()

    return pl.pallas_call(
        body, name="pair_exchange", in_specs=[HBM_SPEC], out_specs=HBM_SPEC,
        out_shape=jax.ShapeDtypeStruct((NCHIP, PR, 1024), BF16),
        scratch_shapes=[pltpu.SemaphoreType.DMA, pltpu.SemaphoreType.DMA],
    )(gb)


def _chip_scatter(pb):
    def body(pb_ref, got_ref, send_sems, recv_sems):
        x, y, c = _place()
        copies = []
        for r, (px, py) in enumerate([(1 - x, y), (x, 1 - y), (1 - x, 1 - y)]):
            cp = pltpu.make_async_remote_copy(
                src_ref=pb_ref.at[2 * px + py], dst_ref=got_ref.at[r],
                send_sem=send_sems.at[r], recv_sem=recv_sems.at[r],
                device_id=(px, py, c), device_id_type=MESH)
            cp.start()
            copies.append(cp)
        for cp in copies:
            cp.wait()

    return pl.pallas_call(
        body, name="chip_scatter", in_specs=[HBM_SPEC], out_specs=HBM_SPEC,
        out_shape=jax.ShapeDtypeStruct((NCHIP - 1, PR, 1024), BF16),
        scratch_shapes=[pltpu.SemaphoreType.DMA((3,)), pltpu.SemaphoreType.DMA((3,))],
    )(pb)


def _pair_share(tot):
    def body(t_ref, got_ref, send_sem, recv_sem):
        x, y, c = _place()
        cp = pltpu.make_async_remote_copy(
            src_ref=t_ref, dst_ref=got_ref, send_sem=send_sem, recv_sem=recv_sem,
            device_id=(x, y, 1 - c), device_id_type=MESH)
        cp.start()
        cp.wait()

    return pl.pallas_call(
        body, name="pair_share", in_specs=[HBM_SPEC], out_specs=HBM_SPEC,
        out_shape=jax.ShapeDtypeStruct((PR, 1024), F32),
        scratch_shapes=[pltpu.SemaphoreType.DMA, pltpu.SemaphoreType.DMA],
    )(tot)


def _small_allreduce(mine):
    def body(m_ref, o_ref, slots, send_sems, recv_sems):
        x, y, c = _place()
        me = 4 * x + 2 * y + c
        slots[me] = m_ref[...]
        copies = []
        for r in range(1, 8):
            rx, ry, rc = (r >> 2) & 1, (r >> 1) & 1, r & 1
            peer = (x + rx - 2 * x * rx, y + ry - 2 * y * ry, c + rc - 2 * c * rc)
            cp = pltpu.make_async_remote_copy(
                src_ref=m_ref, dst_ref=slots.at[me], send_sem=send_sems.at[r - 1], recv_sem=recv_sems.at[r - 1],
                device_id=peer, device_id_type=MESH)
            cp.start()
            copies.append(cp)
        for cp in copies:
            cp.wait()
        acc = slots[0]
        for j in range(1, 8):
            acc = acc + slots[j]
        o_ref[...] = acc

    return pl.pallas_call(
        body, name="small_allreduce",
        in_specs=[pl.BlockSpec(memory_space=pltpu.VMEM)], out_specs=pl.BlockSpec(memory_space=pltpu.VMEM),
        out_shape=jax.ShapeDtypeStruct((SMALL_ROWS, LANE), F32),
        scratch_shapes=[pltpu.VMEM((8, SMALL_ROWS, LANE), F32), pltpu.SemaphoreType.DMA((7,)),
                        pltpu.SemaphoreType.DMA((7,))],
    )(mine)


def _seg_matrix():
    i = lax.broadcasted_iota(jnp.int32, (AD, AD), 0) // DH
    j = lax.broadcasted_iota(jnp.int32, (AD, AD), 1) // DH
    return (i == j).astype(BF16)


TAIL = ("w_out", "w_ple_gate", "w_ple", "w_pw", "dw_w")


def _local_step(x, p, tgt, sm, shards, chip):
    seg = _seg_matrix()
    own = lambda g, s: lax.dynamic_update_index_in_dim(g, s, chip, 0)
    w_in_next = own(_all_gather_weights([shards[0]["w_in"]])[0], shards[0]["w_in"])
    h = x
    saved = []
    for l in range(DEPTH):
        w_in = w_in_next
        row = lambda name: sm[name][l:l + 1]
        u, hn = _rms_inproj(h, row("norm_g"), w_in)
        todo = [shards[l][k] for k in TAIL] + ([shards[l + 1]["w_in"]] if l + 1 < DEPTH else [])
        o, ya, tot, got = _attn_fwd(u, jnp.tile(row("attn_out_g"), (1, 2)), todo)
        got = [own(g, s) for g, s in zip(got, todo)]
        w_out = got[0].reshape(D, D)
        w_gate = got[1].reshape(D, D)
        w_ple = got[2]
        w_pw = got[3].reshape(CD, CD)
        dw = got[4]
        if l + 1 < DEPTH:
            w_in_next = got[5]
        c1 = _glu_conv(u, dw, row("dw_b"))
        c3, yc, h1, gate, pe, h2 = _layer_tail(
            c1, u, ya, h, p[l], row("conv_ln_g"), row("conv_ln_b"), w_pw, row("conv_out_g"), w_out,
            row("ple_norm_g"), w_gate, w_ple)
        saved.append(dict(h=h, u=u, hn=hn, o=o, ya=ya, tot=tot, c1=c1, c3=c3, yc=yc, h1=h1, gate=gate, pe=pe,
                          w_in=w_in, w_out=w_out, w_gate=w_gate, w_pw=w_pw, dw=dw))
        h = h2
    loss_blk, dh, dfg = _loss_head(h, tgt, sm["final_g"])
    big = [None] * DEPTH
    small = [None] * DEPTH
    for l in reversed(range(DEPTH)):
        s = saved[l]
        row = lambda name: sm[name][l:l + 1]
        dh1, dy, dwg, dwp, dwo, dpg = _ple_out_bwd(
            dh, s["h1"], s["gate"], s["pe"], p[l], s["ya"], s["yc"], row("ple_norm_g"), s["w_gate"], s["w_out"])
        ag_t = jnp.tile(row("attn_out_g"), (1, AD // DH))
        do, dga, dgc, dc1, dwpw, dag, dcg, dlg, dlb = _branch_bwd(
            dy, s["o"], s["u"], s["c1"], s["c3"], ag_t, row("conv_ln_g"), row("conv_ln_b"), s["w_pw"],
            row("conv_out_g"), seg)
        dcv, dcgate, ddw, ddb = _conv_bwd(dc1, s["u"], s["dw"])
        dq, dk, dv = _attn_bwd(s["u"], do, s["tot"])
        du = jnp.concatenate([dq, dk, dv, dga, dcv, dcgate, dgc], axis=1).astype(BF16)
        dwin = _inproj_dw(s["hn"], du)
        dh, dng = _inproj_dx(du, s["w_in"], s["h"], row("norm_g"), dh1)
        big[l] = dict(w_in=dwin, w_out=dwo, w_ple_gate=dwg, w_ple=dwp, w_pw=dwpw, dw_w=ddw)
        small[l] = dict(norm_g=dng, attn_out_g=dag.reshape(AD // DH, DH).sum(axis=0, keepdims=True), dw_b=ddb,
                        conv_ln_g=dlg, conv_ln_b=dlb, conv_out_g=dcg, ple_norm_g=dpg)
    return loss_blk[0, 0], dh, big, small, dfg


BIG = ("w_in", "w_out", "w_ple_gate", "w_ple", "w_pw", "dw_w")
SMALL2 = ("norm_g", "ple_norm_g", "dw_b", "conv_ln_g", "conv_ln_b", "conv_out_g", "attn_out_g")


def _pack_big_layer(g):
    parts = [g["w_in"].reshape(NCHIP, 896, 1024), g["w_out"].reshape(NCHIP, 256, 1024),
             g["w_ple_gate"].reshape(NCHIP, 256, 1024), g["w_ple"].reshape(NCHIP, 64, 1024),
             g["w_pw"].reshape(NCHIP, 64, 1024),
             jnp.pad(g["dw_w"].reshape(NCHIP, 4, 1024), ((0, 0), (0, PR - sum(PK_ROWS)), (0, 0)))]
    return jnp.concatenate(parts, axis=1)


def _unpack_big(gfull):
    out, r0 = {}, 0
    shapes = dict(w_in=(D, SHW), w_out=(256, D), w_ple_gate=(256, D), w_ple=(PLE, PLE), w_pw=(128, CD),
                  dw_w=(CWP, LANE))
    for name, rows in zip(BIG, PK_ROWS):
        out[name] = gfull[:, r0:r0 + rows].reshape((DEPTH,) + shapes[name])
        r0 += rows
    return out


def _pack_small(two, final):
    flat = jnp.concatenate([two[k].reshape(-1) for k in SMALL2] + [final.reshape(-1)])
    flat = jnp.concatenate([flat, jnp.zeros((SMALL_ROWS * LANE - flat.shape[0],), F32)])
    return flat.reshape(SMALL_ROWS, LANE)


def _unpack_small(packed, like_two, like_final):
    flat = packed.reshape(-1)
    out, off = {}, 0
    for k in SMALL2:
        n = like_two[k].size
        out[k] = flat[off:off + n].reshape(like_two[k].shape)
        off += n
    return out, flat[off:off + like_final.size].reshape(like_final.shape)


def kernel(x, p, norm_g, w_in, attn_out_g, dw_w, dw_b, conv_ln_g, conv_ln_b, w_pw, conv_out_g, w_out, ple_norm_g, w_ple_gate, w_ple, final_g, loss_target, m_norm_g, m_w_in, m_attn_out_g, m_dw_w, m_dw_b, m_conv_ln_g, m_conv_ln_b, m_w_pw, m_conv_out_g, m_w_out, m_ple_norm_g, m_w_ple_gate, m_w_ple, m_final_g, v_norm_g, v_w_in, v_attn_out_g, v_dw_w, v_dw_b, v_conv_ln_g, v_conv_ln_b, v_w_pw, v_conv_out_g, v_w_out, v_ple_norm_g, v_w_ple_gate, v_w_ple, v_final_g):
    W = dict(norm_g=norm_g, w_in=w_in, attn_out_g=attn_out_g, dw_w=dw_w, dw_b=dw_b, conv_ln_g=conv_ln_g,
             conv_ln_b=conv_ln_b, w_pw=w_pw, conv_out_g=conv_out_g, w_out=w_out, ple_norm_g=ple_norm_g,
             w_ple_gate=w_ple_gate, w_ple=w_ple, final_g=final_g)
    M = dict(norm_g=m_norm_g, w_in=m_w_in, attn_out_g=m_attn_out_g, dw_w=m_dw_w, dw_b=m_dw_b,
             conv_ln_g=m_conv_ln_g, conv_ln_b=m_conv_ln_b, w_pw=m_w_pw, conv_out_g=m_conv_out_g, w_out=m_w_out,
             ple_norm_g=m_ple_norm_g, w_ple_gate=m_w_ple_gate, w_ple=m_w_ple, final_g=m_final_g)
    V = dict(norm_g=v_norm_g, w_in=v_w_in, attn_out_g=v_attn_out_g, dw_w=v_dw_w, dw_b=v_dw_b,
             conv_ln_g=v_conv_ln_g, conv_ln_b=v_conv_ln_b, w_pw=v_w_pw, conv_out_g=v_conv_out_g, w_out=v_w_out,
             ple_norm_g=v_ple_norm_g, w_ple_gate=v_w_ple_gate, w_ple=v_w_ple, final_g=v_final_g)
    order = ("norm_g", "w_in", "attn_out_g", "dw_w", "dw_b", "conv_ln_g", "conv_ln_b", "w_pw", "conv_out_g",
             "w_out", "ple_norm_g", "w_ple_gate", "w_ple", "final_g")

    pad_taps = lambda a: jnp.pad(a, ((0, 0), (0, CWP - CW), (0, 0)))
    cast = dict(w_in=w_in.astype(BF16), w_out=w_out.astype(BF16), w_ple_gate=w_ple_gate.astype(BF16),
                w_ple=w_ple.astype(BF16), w_pw=w_pw.astype(BF16), dw_w=pad_taps(dw_w))
    shards = [{k: v[l] for k, v in cast.items()} for l in range(DEPTH)]
    xi, yi, ci = lax.axis_index("x"), lax.axis_index("y"), lax.axis_index("c")
    chip = 2 * xi + yi

    sm = {k: W[k] for k in SMALL2}
    sm["final_g"] = final_g.reshape(1, D)
    loss_part, grad_x, big, small, dfg = _local_step(x[0], p[:, 0], loss_target[0], sm, shards, chip)
    loss = lax.psum(loss_part, ("x", "y", "c"))

    gf = jnp.stack([_pack_big_layer(big[l]) for l in range(DEPTH)])
    layer_idx = jnp.reshape(ci, (1,)).astype(jnp.int32)
    chip_idx = jnp.reshape(chip, (1,)).astype(jnp.int32)
    pb = _sum_pair(layer_idx, gf, _pair_exchange(gf.astype(BF16)))
    mine = _sum_chips(chip_idx, pb, _chip_scatter(pb))
    theirs = _pair_share(mine)
    gfull = jnp.where(ci == 0, jnp.stack([mine, theirs]), jnp.stack([theirs, mine]))
    g_big = _unpack_big(gfull)

    small_two = {k: jnp.concatenate([small[l][k] for l in range(DEPTH)], axis=0) for k in SMALL2}
    g_small_packed = _small_allreduce(_pack_small(small_two, dfg))
    g_small, g_final = _unpack_small(g_small_packed, {k: W[k] for k in SMALL2}, final_g)

    grads, deltas, new_m, new_v = {}, {}, {}, {}
    for name in BIG:
        wv = pad_taps(W[name]) if name == "dw_w" else W[name]
        mv = pad_taps(M[name]) if name == "dw_w" else M[name]
        vv = pad_taps(V[name]) if name == "dw_w" else V[name]
        gg = g_big[name]
        cols = wv.shape[-1]
        rows_total = wv.size // cols
        tile_rows = min(rows_total, 256)
        d2, m2, v2 = _adamw(wv.reshape(rows_total, cols), gg.reshape(rows_total, cols),
                            mv.reshape(rows_total, cols), vv.reshape(rows_total, cols), tile_rows)
        if name == "dw_w":
            cut = lambda a: a.reshape(DEPTH, CWP, LANE)[:, :CW]
            grads[name], deltas[name], new_m[name], new_v[name] = cut(gg), cut(d2), cut(m2), cut(v2)
        else:
            grads[name] = gg
            deltas[name], new_m[name], new_v[name] = (t.reshape(wv.shape) for t in (d2, m2, v2))
    ws = _pack_small({k: W[k] for k in SMALL2}, final_g)
    ms = _pack_small({k: M[k] for k in SMALL2}, m_final_g)
    vs = _pack_small({k: V[k] for k in SMALL2}, v_final_g)
    ds, nms, nvs = _adamw(ws, g_small_packed, ms, vs, SMALL_ROWS)
    for packed, dst in ((ds, deltas), (nms, new_m), (nvs, new_v)):
        two, fin = _unpack_small(packed, {k: W[k] for k in SMALL2}, final_g)
        dst.update(two)
        dst["final_g"] = fin
    grads.update(g_small)
    grads["final_g"] = g_final

    return (loss, grad_x[None], *[grads[n] for n in order], *[deltas[n] for n in order],
            *[new_m[n] for n in order], *[new_v[n] for n in order])
```

```python
import functools

import jax
import jax.numpy as jnp
from jax import lax
from jax.experimental import pallas as pl
from jax.experimental.pallas import tpu as pltpu

F32 = jnp.float32
BF16 = jnp.bfloat16

T = 2048
D = 1024
DIN = 3584
NCHIP = 4
SHW = DIN // NCHIP
AD = 512
CD = 512
DH = 64
CW = 31
CWP = 32
PLE = 256
DEPTH = 2
EPS = 1e-6
AQ = 256
LANE = 128

ADAM_LR = 0.001
ADAM_B1 = 0.9
ADAM_B2 = 0.999
ADAM_EPS = 1e-08
ADAM_WD = 0.01
ADAM_STEP = 10

SMALL_ROWS = 80

VMEM_BIG = 56 * 1024 * 1024
MESH = pl.DeviceIdType.MESH


def _cp(sem=None, vmem=None):
    kw = {}
    if sem is not None:
        kw["dimension_semantics"] = sem
    if vmem is not None:
        kw["vmem_limit_bytes"] = vmem
    return pltpu.CompilerParams(**kw)


def _dot(a, b):
    return jnp.dot(a, b, preferred_element_type=F32)


def _dot_nt(a, b):
    return lax.dot_general(a, b, (((1,), (1,)), ((), ())), preferred_element_type=F32)


def _dot_tn(a, b):
    return lax.dot_general(a, b, (((0,), (0,)), ((), ())), preferred_element_type=F32)


def _dot2(x, m):
    hi = x.astype(BF16)
    lo = (x - hi.astype(F32)).astype(BF16)
    return _dot(hi, m) + _dot(lo, m)


def _sig(x):
    return 1.0 / (1.0 + jnp.exp(-x))


def _softplus(z):
    return jnp.maximum(z, 0.0) + jnp.log(1.0 + jnp.exp(-jnp.abs(z)))


def _rstd(x):
    return lax.rsqrt(jnp.mean(x * x, axis=-1, keepdims=True) + EPS)


def _rms_bwd(dy, x, r, g):
    dn = dy * g
    return r * dn - x * (r * r * r) * jnp.mean(dn * x, axis=-1, keepdims=True)


def _rms_inproj(h, g, w):
    tm = 512

    def body(h_ref, g_ref, w_ref, u_ref, hn_ref, hn_s):
        @pl.when(pl.program_id(1) == 0)
        def _():
            hh = h_ref[...]
            hn = (hh * _rstd(hh) * g_ref[...]).astype(BF16)
            hn_s[...] = hn
            hn_ref[...] = hn
        u_ref[...] = _dot(hn_s[...], w_ref[0])

    return pl.pallas_call(
        body, name="rms_inproj", grid=(T // tm, NCHIP),
        in_specs=[pl.BlockSpec((tm, D), lambda i, k: (i, 0)),
                  pl.BlockSpec((1, D), lambda i, k: (0, 0)),
                  pl.BlockSpec((1, D, SHW), lambda i, k: (k, 0, 0))],
        out_specs=[pl.BlockSpec((tm, SHW), lambda i, k: (i, k)),
                   pl.BlockSpec((tm, D), lambda i, k: (i, 0))],
        out_shape=[jax.ShapeDtypeStruct((T, DIN), F32), jax.ShapeDtypeStruct((T, D), BF16)],
        scratch_shapes=[pltpu.VMEM((tm, D), BF16)],
        compiler_params=_cp(("arbitrary", "arbitrary")),
    )(h, g, w)


def _attn_tiles():
    row = lax.broadcasted_iota(jnp.int32, (2 * AQ, AQ), 0) & (AQ - 1)
    col = lax.broadcasted_iota(jnp.int32, (2 * AQ, AQ), 1)
    tr = lax.broadcasted_iota(jnp.int32, (AQ, AQ), 0)
    tc = lax.broadcasted_iota(jnp.int32, (AQ, AQ), 1)
    first = lax.broadcasted_iota(jnp.int32, (1, LANE), 1) < DH
    return col < row, tr, tc, first


def _stack_heads(t, first):
    return jnp.concatenate([jnp.where(first, t, 0.0), jnp.where(first, 0.0, t)], axis=0)


def _unstack_heads(t, first):
    return jnp.where(first, t[:AQ], t[AQ:])


def _tri_sum(x, tri):
    hi = x.astype(BF16)
    lo = (x - hi.astype(F32)).astype(BF16)
    both = _dot(jnp.concatenate([hi, lo], axis=0), tri)
    return both[:2 * AQ] + both[2 * AQ:]


def _gather_copies(ins, outs, send_sems, recv_sems):
    x, y, c = _place()
    me = 2 * x + y
    peers = [(1 - x, y, c), (x, 1 - y, c), (1 - x, 1 - y, c)]
    return [pltpu.make_async_remote_copy(
        src_ref=ins[a], dst_ref=outs[a].at[me], send_sem=send_sems.at[3 * a + r], recv_sem=recv_sems.at[3 * a + r],
        device_id=peer, device_id_type=MESH) for a in range(len(ins)) for r, peer in enumerate(peers)]


def _scatter_copies(ps, gots, send_sems, recv_sems):
    x, y, c = _place()
    peers = [(1 - x, y), (x, 1 - y), (1 - x, 1 - y)]
    return [pltpu.make_async_remote_copy(
        src_ref=ps[a].at[2 * px + py], dst_ref=gots[a].at[r], send_sem=send_sems.at[3 * a + r],
        recv_sem=recv_sems.at[3 * a + r], device_id=(px, py, c), device_id_type=MESH)
        for a in range(len(ps)) for r, (px, py) in enumerate(peers)]


def _host(body, grid, n_in, n_out, n_x, make_copies):
    if not n_x:
        return body

    def hosting(*refs):
        a, b = n_in + n_x, n_in + 2 * n_x + n_out
        copies = make_copies(refs[n_in:a], refs[a + n_out:b], refs[-2], refs[-1])
        ids = [pl.program_id(d) for d in range(len(grid))]
        first = functools.reduce(jnp.logical_and, [i == 0 for i in ids])
        last = functools.reduce(jnp.logical_and, [i == g - 1 for i, g in zip(ids, grid)])

        @pl.when(first)
        def _():
            for cp in copies:
                cp.start()

        body(*refs[:n_in], *refs[a:a + n_out], *refs[b:-2])

        @pl.when(last)
        def _():
            for cp in copies:
                cp.wait()

    return hosting


def _hosted_sems(n_x):
    return [pltpu.SemaphoreType.DMA((3 * n_x,)), pltpu.SemaphoreType.DMA((3 * n_x,))] if n_x else []


def _attn_fwd(u, ag2, shards=()):
    n = len(shards)
    nhp, nq = AD // LANE, T // AQ

    def body(*refs):
        q_ref, k_ref, v_ref, g_ref, ag_ref = refs[:5]
        sh_refs = refs[5:5 + n]
        o_ref, y_ref, tot_ref = refs[5 + n:8 + n]
        ga_refs = refs[8 + n:8 + 2 * n]
        kb_s, vb_s = refs[8 + 2 * n:10 + 2 * n]
        qi = pl.program_id(1)
        if n:
            copies = _gather_copies(sh_refs, ga_refs, *refs[10 + 2 * n:])

            @pl.when(jnp.logical_and(pl.program_id(0) == 0, qi == 0))
            def _():
                for cp in copies:
                    cp.start()

        @pl.when(qi == 0)
        def _():
            kb_s[...] = k_ref[...].astype(BF16)
            vb_s[...] = v_ref[...].astype(BF16)

        causal, tr, tc, first = _attn_tiles()
        upper = (tr > tc).astype(BF16)
        qs = _stack_heads(q_ref[...] * 0.125, first).astype(BF16)

        def block(kb, carry, masked):
            run, acc = carry
            k0 = pl.multiple_of(kb * AQ, AQ)
            kk = kb_s[pl.ds(k0, AQ), :]
            vv = vb_s[pl.ds(k0, AQ), :]
            z = _dot_nt(qs, kk)
            sp = _softplus(z)
            lm = jnp.where(causal, -sp, 0.0) if masked else -sp
            att = jnp.exp((z - sp) + _tri_sum(lm, upper) + run)
            if masked:
                att = jnp.where(causal, att, 0.0)
            acc = acc + _dot(att.astype(BF16), vv)
            return run + jnp.sum(lm, axis=1, keepdims=True), acc

        carry = block(qi, (jnp.zeros((2 * AQ, 1), F32), jnp.zeros((2 * AQ, LANE), F32)), True)
        run, acc = lax.fori_loop(0, qi, lambda i, c: block(qi - 1 - i, c, False), carry)
        o = _unstack_heads(acc, first)
        osq = o * o
        ms_a = jnp.sum(jnp.where(first, osq, 0.0), axis=1, keepdims=True)
        ms_b = jnp.sum(jnp.where(first, 0.0, osq), axis=1, keepdims=True)
        r = lax.rsqrt(jnp.where(first, ms_a, ms_b) * (1.0 / DH) + EPS)
        g = g_ref[...]
        o_ref[...] = o
        y_ref[...] = (o * r * ag_ref[...] * (g * _sig(g))).astype(BF16)
        tot_ref[0] = jnp.where(first, run[:AQ], run[AQ:])
        if n:
            @pl.when(jnp.logical_and(pl.program_id(0) == nhp - 1, qi == nq - 1))
            def _():
                for cp in copies:
                    cp.wait()

    sems = [pltpu.SemaphoreType.DMA((3 * n,)), pltpu.SemaphoreType.DMA((3 * n,))] if n else []
    res = pl.pallas_call(
        body, name="attn_fwd", grid=(nhp, nq),
        in_specs=[pl.BlockSpec((AQ, LANE), lambda hp, qi: (qi, hp)),
                  pl.BlockSpec((T, LANE), lambda hp, qi: (0, 4 + hp)),
                  pl.BlockSpec((T, LANE), lambda hp, qi: (0, 8 + hp)),
                  pl.BlockSpec((AQ, LANE), lambda hp, qi: (qi, 12 + hp)),
                  pl.BlockSpec((1, LANE), lambda hp, qi: (0, 0))] + [HBM_SPEC] * n,
        out_specs=[pl.BlockSpec((AQ, LANE), lambda hp, qi: (qi, hp)),
                   pl.BlockSpec((AQ, LANE), lambda hp, qi: (qi, hp)),
                   pl.BlockSpec((1, AQ, LANE), lambda hp, qi: (hp, qi, 0))] + [HBM_SPEC] * n,
        out_shape=[jax.ShapeDtypeStruct((T, AD), F32), jax.ShapeDtypeStruct((T, AD), BF16),
                   jax.ShapeDtypeStruct((nhp, T, LANE), F32)]
        + [jax.ShapeDtypeStruct((NCHIP,) + s.shape, s.dtype) for s in shards],
        scratch_shapes=[pltpu.VMEM((T, LANE), BF16), pltpu.VMEM((T, LANE), BF16)] + sems,
        compiler_params=_cp(("arbitrary", "arbitrary"), VMEM_BIG),
    )(u, u, u, u, ag2, *shards)
    return res[0], res[1], res[2], list(res[3:])


def _glu_conv(u, dw, db):
    tr = 256

    def body(cv_ref, cg_ref, w_ref, b_ref, c1_ref, pad_s):
        pad_s[pl.ds(0, CWP), :] = jnp.zeros((CWP, LANE), F32)
        pad_s[pl.ds(CWP, T), :] = cv_ref[...] * _sig(cg_ref[...])
        wv = w_ref[0]
        bias = b_ref[...]

        def tile(i, carry):
            r0 = pl.multiple_of(i * tr, tr)
            acc = jnp.zeros((tr, LANE), F32) + bias
            for w in range(CW):
                acc = acc + pad_s[pl.ds(r0 + (CWP - CW + 1) + w, tr), :] * wv[w:w + 1, :]
            c1_ref[pl.ds(r0, tr), :] = acc
            return carry

        lax.fori_loop(0, T // tr, tile, 0)

    return pl.pallas_call(
        body, name="glu_conv", grid=(CD // LANE,),
        in_specs=[pl.BlockSpec((T, LANE), lambda cb: (0, 16 + cb)),
                  pl.BlockSpec((T, LANE), lambda cb: (0, 20 + cb)),
                  pl.BlockSpec((1, CWP, LANE), lambda cb: (cb, 0, 0)),
                  pl.BlockSpec((1, LANE), lambda cb: (0, cb))],
        out_specs=pl.BlockSpec((T, LANE), lambda cb: (0, cb)),
        out_shape=jax.ShapeDtypeStruct((T, CD), F32),
        scratch_shapes=[pltpu.VMEM((T + CWP, LANE), F32)],
        compiler_params=_cp(("arbitrary",)),
    )(u, u, dw, db)


def _ln_silu(c1, lg, lb):
    mu = jnp.mean(c1, axis=-1, keepdims=True)
    xc = c1 - mu
    rs = lax.rsqrt(jnp.mean(xc * xc, axis=-1, keepdims=True) + EPS)
    xh = xc * rs
    ln = xh * lg + lb
    s = _sig(ln)
    return xh, rs, ln, s


def _layer_tail(c1, u, ya, h, p, lg, lb, wpw, cg, wout, pg, wgate, wple):
    tm = 256

    def body(c1_ref, gc_ref, ya_ref, h_ref, p_ref, lg_ref, lb_ref, wpw_ref, cg_ref, wout_ref,
             pg_ref, wgate_ref, wple_ref, c3_ref, yc_ref, h1_ref, gate_ref, pe_ref, h2_ref):
        _, _, ln, s = _ln_silu(c1_ref[...], lg_ref[...], lb_ref[...])
        c2 = (ln * s).astype(BF16)
        c3 = _dot(c2, wpw_ref[...])
        gc = gc_ref[...]
        yc = (c3 * _rstd(c3) * cg_ref[...] * (gc * _sig(gc))).astype(BF16)
        c3_ref[...] = c3
        yc_ref[...] = yc
        y = _dot(ya_ref[...], wout_ref[pl.ds(0, AD), :]) + _dot(yc, wout_ref[pl.ds(AD, CD), :])
        h1 = h_ref[...] + y
        hn2 = (h1 * _rstd(h1) * pg_ref[...]).astype(BF16)
        gate = _sig(_dot(hn2, wgate_ref[...]))
        pb = p_ref[...].astype(BF16)
        pe = jnp.concatenate([_dot(pb, wple_ref[k]) for k in range(NCHIP)], axis=1)
        h1_ref[...] = h1
        gate_ref[...] = gate
        pe_ref[...] = pe
        h2_ref[...] = h1 + pe * gate

    row = lambda w: pl.BlockSpec((tm, w), lambda i: (i, 0))
    full = lambda *s: pl.BlockSpec(s, lambda i: (0,) * len(s))
    return pl.pallas_call(
        body, name="layer_tail", grid=(T // tm,),
        in_specs=[row(CD), pl.BlockSpec((tm, CD), lambda i: (i, 6)), row(AD), row(D), row(PLE),
                  full(1, CD), full(1, CD), full(CD, CD), full(1, CD), full(D, D),
                  full(1, D), full(D, D), full(NCHIP, PLE, PLE)],
        out_specs=[row(CD), row(CD), row(D), row(D), row(D), row(D)],
        out_shape=[jax.ShapeDtypeStruct((T, CD), F32), jax.ShapeDtypeStruct((T, CD), BF16)]
        + [jax.ShapeDtypeStruct((T, D), F32)] * 4,
        compiler_params=_cp(("arbitrary",), VMEM_BIG),
    )(c1, u, ya, h, p, lg, lb, wpw, cg, wout, pg, wgate, wple)


def _loss_head(h, tgt, fg):
    tm = 256

    def body(h_ref, t_ref, g_ref, loss_ref, dh_ref, dg_ref):
        @pl.when(pl.program_id(0) == 0)
        def _():
            loss_ref[...] = jnp.zeros_like(loss_ref)
            dg_ref[...] = jnp.zeros_like(dg_ref)
        hh = h_ref[...]
        g = g_ref[...]
        r = _rstd(hh)
        e = hh * r * g - t_ref[...]
        loss_ref[...] += 0.5 * jnp.sum(jnp.mean(e * e, axis=-1, keepdims=True))
        dy = e * (1.0 / D)
        dg_ref[...] += jnp.sum(dy * hh * r, axis=0, keepdims=True)
        dh_ref[...] = _rms_bwd(dy, hh, r, g)

    return pl.pallas_call(
        body, name="loss_head", grid=(T // tm,),
        in_specs=[pl.BlockSpec((tm, D), lambda i: (i, 0)), pl.BlockSpec((tm, D), lambda i: (i, 0)),
                  pl.BlockSpec((1, D), lambda i: (0, 0))],
        out_specs=[pl.BlockSpec((8, LANE), lambda i: (0, 0)), pl.BlockSpec((tm, D), lambda i: (i, 0)),
                   pl.BlockSpec((1, D), lambda i: (0, 0))],
        out_shape=[jax.ShapeDtypeStruct((8, LANE), F32), jax.ShapeDtypeStruct((T, D), F32),
                   jax.ShapeDtypeStruct((1, D), F32)],
        compiler_params=_cp(("arbitrary",)),
    )(h, tgt, fg)


def _ple_out_bwd(dh2, h1, gate, pe, p, ya, yc, pg, wgate, wout):
    tm = 256

    def body(dh2_ref, h1_ref, gate_ref, pe_ref, p_ref, ya_ref, yc_ref, pg_ref, wgate_ref, wout_ref,
             dh1_ref, dy_ref, dwg_ref, dwp_ref, dwo_ref, dpg_ref):
        @pl.when(pl.program_id(0) == 0)
        def _():
            dwg_ref[...] = jnp.zeros_like(dwg_ref)
            dwp_ref[...] = jnp.zeros_like(dwp_ref)
            dwo_ref[...] = jnp.zeros_like(dwo_ref)
            dpg_ref[...] = jnp.zeros_like(dpg_ref)
        dh2 = dh2_ref[...]
        h1 = h1_ref[...]
        gate = gate_ref[...]
        pg = pg_ref[...]
        dpe = (dh2 * gate).astype(BF16)
        dgp = (dh2 * pe_ref[...] * gate * (1.0 - gate)).astype(BF16)
        r = _rstd(h1)
        hn = h1 * r
        dwg_ref[...] += _dot_tn((hn * pg).astype(BF16), dgp)
        dhn2 = _dot_nt(dgp, wgate_ref[...])
        dpg_ref[...] += jnp.sum(dhn2 * hn, axis=0, keepdims=True)
        dh1 = dh2 + _rms_bwd(dhn2, h1, r, pg)
        pb = p_ref[...].astype(BF16)
        for k in range(NCHIP):
            dwp_ref[k] += _dot_tn(pb, dpe[:, k * PLE:(k + 1) * PLE])
        dh1b = dh1.astype(BF16)
        dy_ref[...] = _dot_nt(dh1b, wout_ref[...])
        dwo_ref[pl.ds(0, AD), :] += _dot_tn(ya_ref[...], dh1b)
        dwo_ref[pl.ds(AD, CD), :] += _dot_tn(yc_ref[...], dh1b)
        dh1_ref[...] = dh1

    row = lambda w: pl.BlockSpec((tm, w), lambda i: (i, 0))
    full = lambda *s: pl.BlockSpec(s, lambda i: (0,) * len(s))
    return pl.pallas_call(
        body, name="ple_out_bwd", grid=(T // tm,),
        in_specs=[row(D), row(D), row(D), row(D), row(PLE), row(AD), row(CD),
                  full(1, D), full(D, D), full(D, D)],
        out_specs=[row(D), row(D), full(D, D), full(NCHIP, PLE, PLE), full(D, D), full(1, D)],
        out_shape=[jax.ShapeDtypeStruct((T, D), F32), jax.ShapeDtypeStruct((T, D), F32),
                   jax.ShapeDtypeStruct((D, D), F32), jax.ShapeDtypeStruct((NCHIP, PLE, PLE), F32),
                   jax.ShapeDtypeStruct((D, D), F32), jax.ShapeDtypeStruct((1, D), F32)],
        compiler_params=_cp(("arbitrary",), VMEM_BIG),
    )(dh2, h1, gate, pe, p, ya, yc, pg, wgate, wout)


def _branch_bwd(dy, o, u, c1, c3, ag, lg, lb, wpw, cg, seg):
    tm = 256

    def body(dya_ref, dyc_ref, o_ref, ga_ref, gc_ref, c1_ref, c3_ref, ag_ref, lg_ref, lb_ref, wpw_ref,
             cg_ref, seg_ref, do_ref, dga_ref, dgc_ref, dc1_ref, dwpw_ref, dag_ref, dcg_ref, dlg_ref, dlb_ref):
        @pl.when(pl.program_id(0) == 0)
        def _():
            for r_ in (dwpw_ref, dag_ref, dcg_ref, dlg_ref, dlb_ref):
                r_[...] = jnp.zeros_like(r_)
        dya = dya_ref[...]
        o = o_ref[...]
        ga = ga_ref[...]
        ag_v = ag_ref[...]
        seg_m = seg_ref[...]
        r = lax.rsqrt(_dot2(o * o, seg_m) * (1.0 / DH) + EPS)
        onr = o * r
        sg = _sig(ga)
        dga_ref[...] = dya * (onr * ag_v) * (sg * (1.0 + ga * (1.0 - sg)))
        don = dya * (ga * sg)
        dag_ref[...] += jnp.sum(don * onr, axis=0, keepdims=True)
        dn = don * ag_v
        do_ref[...] = r * dn - o * (r * r * r) * (_dot2(dn * o, seg_m) * (1.0 / DH))
        dyc = dyc_ref[...]
        c3 = c3_ref[...]
        gc = gc_ref[...]
        cg_v = cg_ref[...]
        r3 = _rstd(c3)
        cn = c3 * r3
        sc = _sig(gc)
        dgc_ref[...] = dyc * (cn * cg_v) * (sc * (1.0 + gc * (1.0 - sc)))
        dcn = dyc * (gc * sc)
        dcg_ref[...] += jnp.sum(dcn * cn, axis=0, keepdims=True)
        dc3 = _rms_bwd(dcn, c3, r3, cg_v).astype(BF16)
        lg_v = lg_ref[...]
        xh, rs, ln, s = _ln_silu(c1_ref[...], lg_v, lb_ref[...])
        c2 = (ln * s).astype(BF16)
        dwpw_ref[...] += _dot_tn(c2, dc3)
        dc2 = _dot_nt(dc3, wpw_ref[...])
        dln = dc2 * (s * (1.0 + ln * (1.0 - s)))
        dlb_ref[...] += jnp.sum(dln, axis=0, keepdims=True)
        dlg_ref[...] += jnp.sum(dln * xh, axis=0, keepdims=True)
        dxh = dln * lg_v
        dc1_ref[...] = rs * (dxh - jnp.mean(dxh, axis=-1, keepdims=True)
                             - xh * jnp.mean(dxh * xh, axis=-1, keepdims=True))

    half = lambda j: pl.BlockSpec((tm, 512), lambda i: (i, j))
    full = lambda *s: pl.BlockSpec(s, lambda i: (0,) * len(s))
    vec = jax.ShapeDtypeStruct((1, 512), F32)
    act = jax.ShapeDtypeStruct((T, 512), F32)
    return pl.pallas_call(
        body, name="branch_bwd", grid=(T // tm,),
        in_specs=[half(0), half(1), half(0), half(3), half(6), half(0), half(0),
                  full(1, AD), full(1, CD), full(1, CD), full(CD, CD), full(1, CD), full(AD, AD)],
        out_specs=[half(0), half(0), half(0), half(0), full(CD, CD), full(1, 512), full(1, 512),
                   full(1, 512), full(1, 512)],
        out_shape=[act, act, act, act, jax.ShapeDtypeStruct((CD, CD), F32), vec, vec, vec, vec],
        compiler_params=_cp(("arbitrary",), VMEM_BIG),
    )(dy, dy, o, u, u, c1, c3, ag, lg, lb, wpw, cg, seg)


def _conv_bwd(dc1, u, dw):
    tr = 256
    off = CWP - CW + 1

    def body(d_ref, cv_ref, cg_ref, w_ref, dcv_ref, dcg_ref, ddw_ref, ddb_ref, padc_s, padd_s, acc_s):
        cv = cv_ref[...]
        sg = _sig(cg_ref[...])
        padc_s[pl.ds(0, CWP), :] = jnp.zeros((CWP, LANE), F32)
        padc_s[pl.ds(CWP, T), :] = cv * sg
        padd_s[pl.ds(0, T), :] = d_ref[...]
        padd_s[pl.ds(T, CWP), :] = jnp.zeros((CWP, LANE), F32)
        acc_s[...] = jnp.zeros_like(acc_s)
        wv = w_ref[0]

        def tile(i, carry):
            r0 = pl.multiple_of(i * tr, tr)
            dt = padd_s[pl.ds(r0, tr), :]
            dc0 = jnp.zeros((tr, LANE), F32)
            for w in range(CW):
                dc0 = dc0 + padd_s[pl.ds(r0 + (CW - 1) - w, tr), :] * wv[w:w + 1, :]
                prod = dt * padc_s[pl.ds(r0 + off + w, tr), :]
                acc_s[w] += jnp.sum(prod.reshape(tr // 8, 8, LANE), axis=0)
            cvt = cv_ref[pl.ds(r0, tr), :]
            sgt = _sig(cg_ref[pl.ds(r0, tr), :])
            dcv_ref[pl.ds(r0, tr), :] = dc0 * sgt
            dcg_ref[pl.ds(r0, tr), :] = dc0 * cvt * sgt * (1.0 - sgt)
            return carry

        lax.fori_loop(0, T // tr, tile, 0)
        ddw_ref[0] = jnp.sum(acc_s[...], axis=1)
        ddb_ref[...] = jnp.sum(d_ref[...], axis=0, keepdims=True)

    col = lambda j: pl.BlockSpec((T, LANE), lambda cb: (0, j + cb))
    return pl.pallas_call(
        body, name="conv_bwd", grid=(CD // LANE,),
        in_specs=[col(0), col(16), col(20), pl.BlockSpec((1, CWP, LANE), lambda cb: (cb, 0, 0))],
        out_specs=[col(0), col(0), pl.BlockSpec((1, CWP, LANE), lambda cb: (cb, 0, 0)),
                   pl.BlockSpec((1, LANE), lambda cb: (0, cb))],
        out_shape=[jax.ShapeDtypeStruct((T, CD), F32), jax.ShapeDtypeStruct((T, CD), F32),
                   jax.ShapeDtypeStruct((NCHIP, CWP, LANE), F32), jax.ShapeDtypeStruct((1, CD), F32)],
        scratch_shapes=[pltpu.VMEM((T + CWP, LANE), F32), pltpu.VMEM((T + CWP, LANE), F32),
                        pltpu.VMEM((CWP, 8, LANE), F32)],
        compiler_params=_cp(("arbitrary",)),
    )(dc1, u, u, dw)


def _attn_bwd(u, do, tot, partials=()):
    n_x = len(partials)
    grid = (AD // LANE, T // AQ)

    def body(q_ref, k_ref, v_ref, do_ref, tot_ref, dq_ref, dk_ref, dv_ref, kb_s, vb_s):
        qi = pl.program_id(1)

        @pl.when(qi == 0)
        def _():
            kb_s[...] = k_ref[...].astype(BF16)
            vb_s[...] = v_ref[...].astype(BF16)
            dk_ref[...] = jnp.zeros_like(dk_ref)
            dv_ref[...] = jnp.zeros_like(dv_ref)

        causal, tr, tc, first = _attn_tiles()
        upper = (tr > tc).astype(BF16)
        lower = (tr < tc).astype(BF16)
        q = q_ref[...]
        qs = _stack_heads(q * 0.125, first).astype(BF16)
        qus = _stack_heads(q, first).astype(BF16)
        dos = _stack_heads(do_ref[...], first).astype(BF16)
        totv = tot_ref[0]
        tots = jnp.concatenate([totv[:, 0:1], totv[:, DH:DH + 1]], axis=0)

        def block(kb, carry, masked):
            lm_left, dl_left, dq = carry
            k0 = pl.multiple_of(kb * AQ, AQ)
            kk = kb_s[pl.ds(k0, AQ), :]
            vv = vb_s[pl.ds(k0, AQ), :]
            z = _dot_nt(qs, kk)
            sp = _softplus(z)
            lm = jnp.where(causal, -sp, 0.0) if masked else -sp
            lm_incl = lm_left + jnp.sum(lm, axis=1, keepdims=True)
            att = jnp.exp((z - sp) + _tri_sum(lm, upper) + (tots - lm_incl))
            if masked:
                att = jnp.where(causal, att, 0.0)
            dl = att * _dot_nt(dos, vv)
            dv_ref[pl.ds(k0, AQ), :] += _dot_tn(att.astype(BF16), dos)
            prefix = dl_left + _tri_sum(dl, lower)
            beta = jnp.exp(z - sp)
            dz = (1.0 - beta) * dl - beta * prefix
            if masked:
                dz = jnp.where(causal, dz, 0.0)
            dzs = (dz * 0.125).astype(BF16)
            dk_ref[pl.ds(k0, AQ), :] += _dot_tn(dzs, qus)
            return lm_incl, dl_left + jnp.sum(dl, axis=1, keepdims=True), dq + _dot(dzs, kk)

        zero = jnp.zeros((2 * AQ, 1), F32)
        carry = lax.fori_loop(0, qi, lambda kb, c: block(kb, c, False),
                              (zero, zero, jnp.zeros((2 * AQ, LANE), F32)))
        _, _, dq = block(qi, carry, True)
        dq_ref[...] = _unstack_heads(dq, first)

    res = pl.pallas_call(
        _host(body, grid, 5, 3, n_x, _scatter_copies), name="attn_bwd", grid=grid,
        in_specs=[pl.BlockSpec((AQ, LANE), lambda hp, qi: (qi, hp)),
                  pl.BlockSpec((T, LANE), lambda hp, qi: (0, 4 + hp)),
                  pl.BlockSpec((T, LANE), lambda hp, qi: (0, 8 + hp)),
                  pl.BlockSpec((AQ, LANE), lambda hp, qi: (qi, hp)),
                  pl.BlockSpec((1, AQ, LANE), lambda hp, qi: (hp, qi, 0))] + [HBM_SPEC] * n_x,
        out_specs=[pl.BlockSpec((AQ, LANE), lambda hp, qi: (qi, hp)),
                   pl.BlockSpec((T, LANE), lambda hp, qi: (0, hp)),
                   pl.BlockSpec((T, LANE), lambda hp, qi: (0, hp))] + [HBM_SPEC] * n_x,
        out_shape=[jax.ShapeDtypeStruct((T, AD), F32)] * 3
        + [jax.ShapeDtypeStruct((NCHIP - 1,) + a.shape[1:], a.dtype) for a in partials],
        scratch_shapes=[pltpu.VMEM((T, LANE), BF16), pltpu.VMEM((T, LANE), BF16)] + _hosted_sems(n_x),
        compiler_params=_cp(("arbitrary", "arbitrary"), VMEM_BIG),
    )(u, u, u, do, tot, *partials)
    return res[0], res[1], res[2], list(res[3:])


def _inproj_dw(hn, du):
    tm = 512

    def body(hn_ref, du_ref, dw_ref):
        @pl.when(pl.program_id(1) == 0)
        def _():
            dw_ref[...] = jnp.zeros_like(dw_ref)
        dw_ref[0] += _dot_tn(hn_ref[...], du_ref[...])

    return pl.pallas_call(
        body, name="inproj_dw", grid=(NCHIP, T // tm),
        in_specs=[pl.BlockSpec((tm, D), lambda k, i: (i, 0)), pl.BlockSpec((tm, SHW), lambda k, i: (i, k))],
        out_specs=pl.BlockSpec((1, D, SHW), lambda k, i: (k, 0, 0)),
        out_shape=jax.ShapeDtypeStruct((NCHIP, D, SHW), F32),
        compiler_params=_cp(("arbitrary", "arbitrary")),
    )(hn, du)


def _inproj_dx(du, w, h, g, dres, partials=()):
    tm = 512
    n_x = len(partials)
    grid = (T // tm, NCHIP)

    def body(du_ref, w_ref, h_ref, g_ref, dres_ref, dh_ref, dg_ref, acc_s):
        i, k = pl.program_id(0), pl.program_id(1)

        @pl.when(jnp.logical_and(i == 0, k == 0))
        def _():
            dg_ref[...] = jnp.zeros_like(dg_ref)

        part = _dot_nt(du_ref[...], w_ref[0])

        @pl.when(k == 0)
        def _():
            acc_s[...] = part

        @pl.when(k > 0)
        def _():
            acc_s[...] += part

        @pl.when(k == NCHIP - 1)
        def _():
            hh = h_ref[...]
            r = _rstd(hh)
            dhn = acc_s[...]
            dg_ref[...] += jnp.sum(dhn * hh * r, axis=0, keepdims=True)
            dh_ref[...] = dres_ref[...] + _rms_bwd(dhn, hh, r, g_ref[...])

    res = pl.pallas_call(
        _host(body, grid, 5, 2, n_x, _scatter_copies), name="inproj_dx", grid=grid,
        in_specs=[pl.BlockSpec((tm, SHW), lambda i, k: (i, k)),
                  pl.BlockSpec((1, D, SHW), lambda i, k: (k, 0, 0)),
                  pl.BlockSpec((tm, D), lambda i, k: (i, 0)),
                  pl.BlockSpec((1, D), lambda i, k: (0, 0)),
                  pl.BlockSpec((tm, D), lambda i, k: (i, 0))] + [HBM_SPEC] * n_x,
        out_specs=[pl.BlockSpec((tm, D), lambda i, k: (i, 0)), pl.BlockSpec((1, D), lambda i, k: (0, 0))]
        + [HBM_SPEC] * n_x,
        out_shape=[jax.ShapeDtypeStruct((T, D), F32), jax.ShapeDtypeStruct((1, D), F32)]
        + [jax.ShapeDtypeStruct((NCHIP - 1,) + a.shape[1:], a.dtype) for a in partials],
        scratch_shapes=[pltpu.VMEM((tm, D), F32)] + _hosted_sems(n_x),
        compiler_params=_cp(("arbitrary", "arbitrary")),
    )(du, w, h, g, dres, *partials)
    return res[0], res[1], list(res[2:])


def _sum_pair(core, grads, gots):
    n = len(grads)

    def body(c_ref, *refs):
        for a in range(n):
            refs[2 * n + a][...] = (refs[a][...] + refs[n + a][...]).astype(BF16)

    mine = [pl.BlockSpec((1,) + s.shape[1:], lambda k, c: (k, c[0], 0)) for s in gots]
    same = [pl.BlockSpec((1,) + s.shape[1:], lambda k, c: (k, 0, 0)) for s in gots]
    return pl.pallas_call(
        body, name="sum_pair",
        grid_spec=pltpu.PrefetchScalarGridSpec(
            num_scalar_prefetch=1, grid=(NCHIP,), in_specs=mine + same, out_specs=same),
        out_shape=[jax.ShapeDtypeStruct(s.shape, BF16) for s in gots],
        compiler_params=_cp(("arbitrary",), VMEM_BIG),
    )(core, *grads, *gots)


def _sum_chips(chip, partials, gots):
    n = len(partials)

    def body(c_ref, *refs):
        for a in range(n):
            acc = refs[a][0].astype(F32)
            for j in range(NCHIP - 1):
                acc = acc + refs[n + a][j].astype(F32)
            refs[2 * n + a][...] = acc

    return pl.pallas_call(
        body, name="sum_chips",
        grid_spec=pltpu.PrefetchScalarGridSpec(
            num_scalar_prefetch=1, grid=(1,),
            in_specs=[pl.BlockSpec((1,) + s.shape[1:], lambda i, c: (c[0], 0, 0)) for s in partials]
            + [pl.BlockSpec(s.shape, lambda i, c: (0, 0, 0)) for s in gots],
            out_specs=[pl.BlockSpec(s.shape[1:], lambda i, c: (0, 0)) for s in partials]),
        out_shape=[jax.ShapeDtypeStruct(s.shape[1:], F32) for s in partials],
        compiler_params=_cp(("arbitrary",), VMEM_BIG),
    )(chip, *partials, *gots)


def _adamw(w, g, m, v, rows):
    R, C = w.shape
    c1 = 1.0 - ADAM_B1 ** ADAM_STEP
    c2 = 1.0 - ADAM_B2 ** ADAM_STEP

    def body(w_ref, g_ref, m_ref, v_ref, d_ref, nm_ref, nv_ref):
        gg = g_ref[...]
        nm = ADAM_B1 * m_ref[...] + (1.0 - ADAM_B1) * gg
        nv = ADAM_B2 * v_ref[...] + (1.0 - ADAM_B2) * (gg * gg)
        d_ref[...] = -ADAM_LR * ((nm / c1) / (jnp.sqrt(nv / c2) + ADAM_EPS) + ADAM_WD * w_ref[...])
        nm_ref[...] = nm
        nv_ref[...] = nv

    spec = pl.BlockSpec((rows, C), lambda i: (i, 0))
    sh = jax.ShapeDtypeStruct((R, C), F32)
    return pl.pallas_call(
        body, name="adamw", grid=(R // rows,), in_specs=[spec] * 4, out_specs=[spec] * 3,
        out_shape=[sh, sh, sh], compiler_params=_cp(("arbitrary",)),
    )(w, g, m, v)


HBM_SPEC = pl.BlockSpec(memory_space=pltpu.HBM)


def _place():
    return lax.axis_index("x"), lax.axis_index("y"), lax.axis_index("c")


def _all_gather_weights(shards):
    n = len(shards)

    def body(*refs):
        copies = _gather_copies(refs[:n], refs[n:2 * n], *refs[2 * n:])
        for cp in copies:
            cp.start()
        for cp in copies:
            cp.wait()

    return pl.pallas_call(
        body, name="all_gather_weights",
        in_specs=[HBM_SPEC] * n, out_specs=[HBM_SPEC] * n,
        out_shape=[jax.ShapeDtypeStruct((NCHIP,) + s.shape, s.dtype) for s in shards],
        scratch_shapes=[pltpu.SemaphoreType.DMA((3 * n,)), pltpu.SemaphoreType.DMA((3 * n,))],
    )(*shards)


def _pair_exchange(grads):
    n = len(grads)

    def body(*refs):
        x, y, c = _place()
        copies = []
        for a in range(n):
            half = grads[a].shape[1] // 2
            copies.append(pltpu.make_async_remote_copy(
                src_ref=refs[a].at[:, pl.ds((1 - c) * half, half), :], dst_ref=refs[n + a],
                send_sem=refs[2 * n].at[a], recv_sem=refs[2 * n + 1].at[a],
                device_id=(x, y, 1 - c), device_id_type=MESH))
        for cp in copies:
            cp.start()
        for cp in copies:
            cp.wait()

    return pl.pallas_call(
        body, name="pair_exchange", in_specs=[HBM_SPEC] * n, out_specs=[HBM_SPEC] * n,
        out_shape=[jax.ShapeDtypeStruct((NCHIP, g.shape[1] // 2, g.shape[2]), F32) for g in grads],
        scratch_shapes=[pltpu.SemaphoreType.DMA((n,)), pltpu.SemaphoreType.DMA((n,))],
    )(*grads)


def _pair_share(mine):
    n = len(mine)

    def body(*refs):
        x, y, c = _place()
        copies = [pltpu.make_async_remote_copy(
            src_ref=refs[a], dst_ref=refs[n + a], send_sem=refs[2 * n].at[a], recv_sem=refs[2 * n + 1].at[a],
            device_id=(x, y, 1 - c), device_id_type=MESH) for a in range(n)]
        for cp in copies:
            cp.start()
        for cp in copies:
            cp.wait()

    return pl.pallas_call(
        body, name="pair_share", in_specs=[HBM_SPEC] * n, out_specs=[HBM_SPEC] * n,
        out_shape=[jax.ShapeDtypeStruct(t.shape, F32) for t in mine],
        scratch_shapes=[pltpu.SemaphoreType.DMA((n,)), pltpu.SemaphoreType.DMA((n,))],
    )(*mine)


def _small_allreduce(mine):
    def body(m_ref, o_ref, slots, send_sems, recv_sems):
        x, y, c = _place()
        me = 4 * x + 2 * y + c
        slots[me] = m_ref[...]
        copies = []
        for r in range(1, 8):
            rx, ry, rc = (r >> 2) & 1, (r >> 1) & 1, r & 1
            peer = (x + rx - 2 * x * rx, y + ry - 2 * y * ry, c + rc - 2 * c * rc)
            cp = pltpu.make_async_remote_copy(
                src_ref=m_ref, dst_ref=slots.at[me], send_sem=send_sems.at[r - 1], recv_sem=recv_sems.at[r - 1],
                device_id=peer, device_id_type=MESH)
            cp.start()
            copies.append(cp)
        for cp in copies:
            cp.wait()
        acc = slots[0]
        for j in range(1, 8):
            acc = acc + slots[j]
        o_ref[...] = acc

    return pl.pallas_call(
        body, name="small_allreduce",
        in_specs=[pl.BlockSpec(memory_space=pltpu.VMEM)], out_specs=pl.BlockSpec(memory_space=pltpu.VMEM),
        out_shape=jax.ShapeDtypeStruct((SMALL_ROWS, LANE), F32),
        scratch_shapes=[pltpu.VMEM((8, SMALL_ROWS, LANE), F32), pltpu.SemaphoreType.DMA((7,)),
                        pltpu.SemaphoreType.DMA((7,))],
    )(mine)


def _seg_matrix():
    i = lax.broadcasted_iota(jnp.int32, (AD, AD), 0) // DH
    j = lax.broadcasted_iota(jnp.int32, (AD, AD), 1) // DH
    return (i == j).astype(BF16)


TAIL = ("w_out", "w_ple_gate", "w_ple", "w_pw", "dw_w")


def _local_step(x, p, tgt, sm, shards, chip, ci):
    seg = _seg_matrix()
    core = jnp.reshape(ci, (1,)).astype(jnp.int32)
    chip_idx = jnp.reshape(chip, (1,)).astype(jnp.int32)
    own = lambda g, s: lax.dynamic_update_index_in_dim(g, s, chip, 0)
    w_in_next = own(_all_gather_weights([shards[0]["w_in"]])[0], shards[0]["w_in"])
    h = x
    saved = []
    for l in range(DEPTH):
        w_in = w_in_next
        row = lambda name: sm[name][l:l + 1]
        u, hn = _rms_inproj(h, row("norm_g"), w_in)
        todo = [shards[l][k] for k in TAIL] + ([shards[l + 1]["w_in"]] if l + 1 < DEPTH else [])
        o, ya, tot, got = _attn_fwd(u, jnp.tile(row("attn_out_g"), (1, 2)), todo)
        got = [own(g, s) for g, s in zip(got, todo)]
        w_out = got[0].reshape(D, D)
        w_gate = got[1].reshape(D, D)
        w_ple = got[2]
        w_pw = got[3].reshape(CD, CD)
        dw = got[4]
        if l + 1 < DEPTH:
            w_in_next = got[5]
        c1 = _glu_conv(u, dw, row("dw_b"))
        c3, yc, h1, gate, pe, h2 = _layer_tail(
            c1, u, ya, h, p[l], row("conv_ln_g"), row("conv_ln_b"), w_pw, row("conv_out_g"), w_out,
            row("ple_norm_g"), w_gate, w_ple)
        saved.append(dict(h=h, u=u, hn=hn, o=o, ya=ya, tot=tot, c1=c1, c3=c3, yc=yc, h1=h1, gate=gate, pe=pe,
                          w_in=w_in, w_out=w_out, w_gate=w_gate, w_pw=w_pw, dw=dw))
        h = h2
    loss_blk, dh, dfg = _loss_head(h, tgt, sm["final_g"])
    small = [None] * DEPTH
    pending, partials, arrived = [], {}, {}
    pair_sum = lambda grads: _sum_pair(core, grads, _pair_exchange(grads))
    for l in reversed(range(DEPTH)):
        s = saved[l]
        row = lambda name: sm[name][l:l + 1]
        dh1, dy, dwg, dwp, dwo, dpg = _ple_out_bwd(
            dh, s["h1"], s["gate"], s["pe"], p[l], s["ya"], s["yc"], row("ple_norm_g"), s["w_gate"], s["w_out"])
        ag_t = jnp.tile(row("attn_out_g"), (1, AD // DH))
        do, dga, dgc, dc1, dwpw, dag, dcg, dlg, dlb = _branch_bwd(
            dy, s["o"], s["u"], s["c1"], s["c3"], ag_t, row("conv_ln_g"), row("conv_ln_b"), s["w_pw"],
            row("conv_out_g"), seg)
        dcv, dcgate, ddw, ddb = _conv_bwd(dc1, s["u"], s["dw"])
        tail = [dwo.reshape(NCHIP, 256, D), dwg.reshape(NCHIP, 256, D), dwp, dwpw.reshape(NCHIP, 128, CD), ddw]
        if l == 0:
            partials[(l, "tail")] = pair_sum(tail)
            pending.append((l, "tail"))
        send = [t for key in pending for t in partials[key]]
        dq, dk, dv, got = _attn_bwd(s["u"], do, s["tot"], send)
        for key in pending:
            arrived[key], got = got[:len(partials[key])], got[len(partials[key]):]
        pending = []
        du = jnp.concatenate([dq, dk, dv, dga, dcv, dcgate, dgc], axis=1).astype(BF16)
        dwin = _inproj_dw(s["hn"], du)
        if l == 0:
            partials[(l, "w_in")] = pair_sum([dwin])
            dh, dng, arrived[(l, "w_in")] = _inproj_dx(du, s["w_in"], s["h"], row("norm_g"), dh1,
                                                      partials[(l, "w_in")])
        else:
            dh, dng, _ = _inproj_dx(du, s["w_in"], s["h"], row("norm_g"), dh1)
            tail_p = pair_sum(tail + [dwin])
            partials[(l, "tail")], partials[(l, "w_in")] = tail_p[:-1], tail_p[-1:]
            pending = [(l, "tail"), (l, "w_in")]
        small[l] = dict(norm_g=dng, attn_out_g=dag.reshape(AD // DH, DH).sum(axis=0, keepdims=True), dw_b=ddb,
                        conv_ln_g=dlg, conv_ln_b=dlb, conv_out_g=dcg, ple_norm_g=dpg)
    big = []
    for l in range(DEPTH):
        ps = partials[(l, "w_in")] + partials[(l, "tail")]
        mine = _sum_chips(chip_idx, ps, arrived[(l, "w_in")] + arrived[(l, "tail")])
        theirs = _pair_share(mine)
        big.append(dict(zip(BIG, [jnp.where(core[0] == 0, jnp.concatenate([m, t]), jnp.concatenate([t, m]))
                                  for m, t in zip(mine, theirs)])))
    return loss_blk[0, 0], dh, big, small, dfg


BIG = ("w_in", "w_out", "w_ple_gate", "w_ple", "w_pw", "dw_w")
SMALL2 = ("norm_g", "ple_norm_g", "dw_b", "conv_ln_g", "conv_ln_b", "conv_out_g", "attn_out_g")


def _pack_small(two, final):
    flat = jnp.concatenate([two[k].reshape(-1) for k in SMALL2] + [final.reshape(-1)])
    flat = jnp.concatenate([flat, jnp.zeros((SMALL_ROWS * LANE - flat.shape[0],), F32)])
    return flat.reshape(SMALL_ROWS, LANE)


def _unpack_small(packed, like_two, like_final):
    flat = packed.reshape(-1)
    out, off = {}, 0
    for k in SMALL2:
        n = like_two[k].size
        out[k] = flat[off:off + n].reshape(like_two[k].shape)
        off += n
    return out, flat[off:off + like_final.size].reshape(like_final.shape)


def kernel(x, p, norm_g, w_in, attn_out_g, dw_w, dw_b, conv_ln_g, conv_ln_b, w_pw, conv_out_g, w_out, ple_norm_g, w_ple_gate, w_ple, final_g, loss_target, m_norm_g, m_w_in, m_attn_out_g, m_dw_w, m_dw_b, m_conv_ln_g, m_conv_ln_b, m_w_pw, m_conv_out_g, m_w_out, m_ple_norm_g, m_w_ple_gate, m_w_ple, m_final_g, v_norm_g, v_w_in, v_attn_out_g, v_dw_w, v_dw_b, v_conv_ln_g, v_conv_ln_b, v_w_pw, v_conv_out_g, v_w_out, v_ple_norm_g, v_w_ple_gate, v_w_ple, v_final_g):
    W = dict(norm_g=norm_g, w_in=w_in, attn_out_g=attn_out_g, dw_w=dw_w, dw_b=dw_b, conv_ln_g=conv_ln_g,
             conv_ln_b=conv_ln_b, w_pw=w_pw, conv_out_g=conv_out_g, w_out=w_out, ple_norm_g=ple_norm_g,
             w_ple_gate=w_ple_gate, w_ple=w_ple, final_g=final_g)
    M = dict(norm_g=m_norm_g, w_in=m_w_in, attn_out_g=m_attn_out_g, dw_w=m_dw_w, dw_b=m_dw_b,
             conv_ln_g=m_conv_ln_g, conv_ln_b=m_conv_ln_b, w_pw=m_w_pw, conv_out_g=m_conv_out_g, w_out=m_w_out,
             ple_norm_g=m_ple_norm_g, w_ple_gate=m_w_ple_gate, w_ple=m_w_ple, final_g=m_final_g)
    V = dict(norm_g=v_norm_g, w_in=v_w_in, attn_out_g=v_attn_out_g, dw_w=v_dw_w, dw_b=v_dw_b,
             conv_ln_g=v_conv_ln_g, conv_ln_b=v_conv_ln_b, w_pw=v_w_pw, conv_out_g=v_conv_out_g, w_out=v_w_out,
             ple_norm_g=v_ple_norm_g, w_ple_gate=v_w_ple_gate, w_ple=v_w_ple, final_g=v_final_g)
    order = ("norm_g", "w_in", "attn_out_g", "dw_w", "dw_b", "conv_ln_g", "conv_ln_b", "w_pw", "conv_out_g",
             "w_out", "ple_norm_g", "w_ple_gate", "w_ple", "final_g")

    pad_taps = lambda a: jnp.pad(a, ((0, 0), (0, CWP - CW), (0, 0)))
    cast = dict(w_in=w_in.astype(BF16), w_out=w_out.astype(BF16), w_ple_gate=w_ple_gate.astype(BF16),
                w_ple=w_ple.astype(BF16), w_pw=w_pw.astype(BF16), dw_w=pad_taps(dw_w))
    shards = [{k: v[l] for k, v in cast.items()} for l in range(DEPTH)]
    xi, yi, ci = lax.axis_index("x"), lax.axis_index("y"), lax.axis_index("c")
    chip = 2 * xi + yi

    sm = {k: W[k] for k in SMALL2}
    sm["final_g"] = final_g.reshape(1, D)
    loss_part, grad_x, big, small, dfg = _local_step(x[0], p[:, 0], loss_target[0], sm, shards, chip, ci)
    loss = lax.psum(loss_part, ("x", "y", "c"))
    g_big = {name: jnp.stack([big[l][name] for l in range(DEPTH)]).reshape(cast[name].shape) for name in BIG}

    small_two = {k: jnp.concatenate([small[l][k] for l in range(DEPTH)], axis=0) for k in SMALL2}
    g_small_packed = _small_allreduce(_pack_small(small_two, dfg))
    g_small, g_final = _unpack_small(g_small_packed, {k: W[k] for k in SMALL2}, final_g)

    grads, deltas, new_m, new_v = {}, {}, {}, {}
    for name in BIG:
        wv = pad_taps(W[name]) if name == "dw_w" else W[name]
        mv = pad_taps(M[name]) if name == "dw_w" else M[name]
        vv = pad_taps(V[name]) if name == "dw_w" else V[name]
        gg = g_big[name]
        cols = wv.shape[-1]
        rows_total = wv.size // cols
        tile_rows = min(rows_total, 256)
        d2, m2, v2 = _adamw(wv.reshape(rows_total, cols), gg.reshape(rows_total, cols),
                            mv.reshape(rows_total, cols), vv.reshape(rows_total, cols), tile_rows)
        if name == "dw_w":
            cut = lambda a: a.reshape(DEPTH, CWP, LANE)[:, :CW]
            grads[name], deltas[name], new_m[name], new_v[name] = cut(gg), cut(d2), cut(m2), cut(v2)
        else:
            grads[name] = gg
            deltas[name], new_m[name], new_v[name] = (t.reshape(wv.shape) for t in (d2, m2, v2))
    ws = _pack_small({k: W[k] for k in SMALL2}, final_g)
    ms = _pack_small({k: M[k] for k in SMALL2}, m_final_g)
    vs = _pack_small({k: V[k] for k in SMALL2}, v_final_g)
    ds, nms, nvs = _adamw(ws, g_small_packed, ms, vs, SMALL_ROWS)
    for packed, dst in ((ds, deltas), (nms, new_m), (nvs, new_v)):
        two, fin = _unpack_small(packed, {k: W[k] for k in SMALL2}, final_g)
        dst.update(two)
        dst["final_g"] = fin
    grads.update(g_small)
    grads["final_g"] = g_final

    return (loss, grad_x[None], *[grads[n] for n in order], *[deltas[n] for n in order],
            *[new_m[n] for n in order], *[new_v[n] for n in order])
```

```python
import functools

import jax
import jax.numpy as jnp
from jax import lax
from jax.experimental import pallas as pl
from jax.experimental.pallas import tpu as pltpu

F32 = jnp.float32
BF16 = jnp.bfloat16

T = 2048
D = 1024
DIN = 3584
NCHIP = 4
SHW = DIN // NCHIP
AD = 512
CD = 512
DH = 64
CW = 31
CWP = 32
PLE = 256
DEPTH = 2
EPS = 1e-6
AQ = 256
HG = 4
GW = HG * DH
SR = HG * AQ
NG = AD // GW
LANE = 128

ADAM_LR = 0.001
ADAM_B1 = 0.9
ADAM_B2 = 0.999
ADAM_EPS = 1e-08
ADAM_WD = 0.01
ADAM_STEP = 10

SMALL_ROWS = 80

VMEM_BIG = 56 * 1024 * 1024
MESH = pl.DeviceIdType.MESH


def _cp(sem=None, vmem=None):
    kw = {}
    if sem is not None:
        kw["dimension_semantics"] = sem
    if vmem is not None:
        kw["vmem_limit_bytes"] = vmem
    return pltpu.CompilerParams(**kw)


def _dot(a, b):
    return jnp.dot(a, b, preferred_element_type=F32)


def _dot_nt(a, b):
    return lax.dot_general(a, b, (((1,), (1,)), ((), ())), preferred_element_type=F32)


def _dot_tn(a, b):
    return lax.dot_general(a, b, (((0,), (0,)), ((), ())), preferred_element_type=F32)


def _dot2(x, m):
    hi = x.astype(BF16)
    lo = (x - hi.astype(F32)).astype(BF16)
    return _dot(hi, m) + _dot(lo, m)


def _sig(x):
    return 1.0 / (1.0 + jnp.exp(-x))


def _softplus(z):
    return jnp.maximum(z, 0.0) + jnp.log(1.0 + jnp.exp(-jnp.abs(z)))


def _rstd(x):
    return lax.rsqrt(jnp.mean(x * x, axis=-1, keepdims=True) + EPS)


def _rms_bwd(dy, x, r, g):
    dn = dy * g
    return r * dn - x * (r * r * r) * jnp.mean(dn * x, axis=-1, keepdims=True)


def _rms_inproj(h, g, w):
    tm = 512

    def body(h_ref, g_ref, w_ref, u_ref, hn_ref, hn_s):
        @pl.when(pl.program_id(1) == 0)
        def _():
            hh = h_ref[...]
            hn = (hh * _rstd(hh) * g_ref[...]).astype(BF16)
            hn_s[...] = hn
            hn_ref[...] = hn
        u_ref[...] = _dot(hn_s[...], w_ref[0])

    return pl.pallas_call(
        body, name="rms_inproj", grid=(T // tm, NCHIP),
        in_specs=[pl.BlockSpec((tm, D), lambda i, k: (i, 0)),
                  pl.BlockSpec((1, D), lambda i, k: (0, 0)),
                  pl.BlockSpec((1, D, SHW), lambda i, k: (k, 0, 0))],
        out_specs=[pl.BlockSpec((tm, SHW), lambda i, k: (i, k)),
                   pl.BlockSpec((tm, D), lambda i, k: (i, 0))],
        out_shape=[jax.ShapeDtypeStruct((T, DIN), F32), jax.ShapeDtypeStruct((T, D), BF16)],
        scratch_shapes=[pltpu.VMEM((tm, D), BF16)],
        compiler_params=_cp(("arbitrary", "arbitrary")),
    )(h, g, w)


def _attn_tiles():
    row = lax.broadcasted_iota(jnp.int32, (SR, AQ), 0) & (AQ - 1)
    col = lax.broadcasted_iota(jnp.int32, (SR, AQ), 1)
    tr = lax.broadcasted_iota(jnp.int32, (AQ, AQ), 0)
    tc = lax.broadcasted_iota(jnp.int32, (AQ, AQ), 1)
    lane_head = lax.broadcasted_iota(jnp.int32, (1, GW), 1) // DH
    return col < row, tr, tc, [lane_head == h for h in range(HG)]


def _stack_heads(t, heads):
    return jnp.concatenate([jnp.where(m, t, 0.0) for m in heads], axis=0)


def _unstack_heads(t, heads):
    out = t[:AQ]
    for h in range(1, HG):
        out = jnp.where(heads[h], t[h * AQ:(h + 1) * AQ], out)
    return out


def _tri_sum(x, tri):
    hi = x.astype(BF16)
    lo = (x - hi.astype(F32)).astype(BF16)
    both = _dot(jnp.concatenate([hi, lo], axis=0), tri)
    return both[:SR] + both[SR:]


def _gather_copies(ins, outs, send_sems, recv_sems):
    x, y, c = _place()
    me = 2 * x + y
    peers = [(1 - x, y, c), (x, 1 - y, c), (1 - x, 1 - y, c)]
    return [pltpu.make_async_remote_copy(
        src_ref=ins[a], dst_ref=outs[a].at[me], send_sem=send_sems.at[3 * a + r], recv_sem=recv_sems.at[3 * a + r],
        device_id=peer, device_id_type=MESH) for a in range(len(ins)) for r, peer in enumerate(peers)]


def _scatter_copies(ps, gots, send_sems, recv_sems):
    x, y, c = _place()
    peers = [(1 - x, y), (x, 1 - y), (1 - x, 1 - y)]
    return [pltpu.make_async_remote_copy(
        src_ref=ps[a].at[2 * px + py], dst_ref=gots[a].at[r], send_sem=send_sems.at[3 * a + r],
        recv_sem=recv_sems.at[3 * a + r], device_id=(px, py, c), device_id_type=MESH)
        for a in range(len(ps)) for r, (px, py) in enumerate(peers)]


def _host(body, grid, n_in, n_out, n_x, make_copies):
    if not n_x:
        return body

    def hosting(*refs):
        a, b = n_in + n_x, n_in + 2 * n_x + n_out
        copies = make_copies(refs[n_in:a], refs[a + n_out:b], refs[-2], refs[-1])
        ids = [pl.program_id(d) for d in range(len(grid))]
        first = functools.reduce(jnp.logical_and, [i == 0 for i in ids])
        last = functools.reduce(jnp.logical_and, [i == g - 1 for i, g in zip(ids, grid)])

        @pl.when(first)
        def _():
            for cp in copies:
                cp.start()

        body(*refs[:n_in], *refs[a:a + n_out], *refs[b:-2])

        @pl.when(last)
        def _():
            for cp in copies:
                cp.wait()

    return hosting


def _hosted_sems(n_x):
    return [pltpu.SemaphoreType.DMA((3 * n_x,)), pltpu.SemaphoreType.DMA((3 * n_x,))] if n_x else []


def _attn_fwd(u, agw, shards=()):
    n = len(shards)
    grid = (NG, T // AQ)

    def body(q_ref, k_ref, v_ref, g_ref, ag_ref, o_ref, y_ref, tot_ref, kb_s, vb_s):
        qi = pl.program_id(1)

        @pl.when(qi == 0)
        def _():
            kb_s[...] = k_ref[...].astype(BF16)
            vb_s[...] = v_ref[...].astype(BF16)

        causal, tr, tc, heads = _attn_tiles()
        upper = (tr > tc).astype(BF16)
        qs = _stack_heads(q_ref[...] * 0.125, heads).astype(BF16)

        def block(kb, carry, masked):
            run, acc = carry
            k0 = pl.multiple_of(kb * AQ, AQ)
            kk = kb_s[pl.ds(k0, AQ), :]
            vv = vb_s[pl.ds(k0, AQ), :]
            z = _dot_nt(qs, kk)
            sp = _softplus(z)
            lm = jnp.where(causal, -sp, 0.0) if masked else -sp
            att = jnp.exp((z - sp) + _tri_sum(lm, upper) + run)
            if masked:
                att = jnp.where(causal, att, 0.0)
            acc = acc + _dot(att.astype(BF16), vv)
            return run + jnp.sum(lm, axis=1, keepdims=True), acc

        carry = block(qi, (jnp.zeros((SR, 1), F32), jnp.zeros((SR, GW), F32)), True)
        run, acc = lax.fori_loop(0, qi, lambda i, c: block(qi - 1 - i, c, False), carry)
        o = _unstack_heads(acc, heads)
        osq = o * o
        ms = jnp.zeros((AQ, GW), F32)
        for m in heads:
            ms = jnp.where(m, jnp.sum(jnp.where(m, osq, 0.0), axis=1, keepdims=True), ms)
        g = g_ref[...]
        o_ref[...] = o
        y_ref[...] = (o * lax.rsqrt(ms * (1.0 / DH) + EPS) * ag_ref[...] * (g * _sig(g))).astype(BF16)
        tot_ref[0] = _unstack_heads(jnp.broadcast_to(run, (SR, GW)), heads)

    res = pl.pallas_call(
        _host(body, grid, 5, 3, n, _gather_copies), name="attn_fwd", grid=grid,
        in_specs=[pl.BlockSpec((AQ, GW), lambda hg, qi: (qi, hg)),
                  pl.BlockSpec((T, GW), lambda hg, qi: (0, NG + hg)),
                  pl.BlockSpec((T, GW), lambda hg, qi: (0, 2 * NG + hg)),
                  pl.BlockSpec((AQ, GW), lambda hg, qi: (qi, 3 * NG + hg)),
                  pl.BlockSpec((1, GW), lambda hg, qi: (0, 0))] + [HBM_SPEC] * n,
        out_specs=[pl.BlockSpec((AQ, GW), lambda hg, qi: (qi, hg)),
                   pl.BlockSpec((AQ, GW), lambda hg, qi: (qi, hg)),
                   pl.BlockSpec((1, AQ, GW), lambda hg, qi: (hg, qi, 0))] + [HBM_SPEC] * n,
        out_shape=[jax.ShapeDtypeStruct((T, AD), F32), jax.ShapeDtypeStruct((T, AD), BF16),
                   jax.ShapeDtypeStruct((NG, T, GW), F32)]
        + [jax.ShapeDtypeStruct((NCHIP,) + s.shape, s.dtype) for s in shards],
        scratch_shapes=[pltpu.VMEM((T, GW), BF16), pltpu.VMEM((T, GW), BF16)] + _hosted_sems(n),
        compiler_params=_cp(("arbitrary", "arbitrary"), VMEM_BIG),
    )(u, u, u, u, agw, *shards)
    return res[0], res[1], res[2], list(res[3:])


def _glu_conv(u, dw, db):
    tr = 256

    def body(cv_ref, cg_ref, w_ref, b_ref, c1_ref, pad_s):
        pad_s[pl.ds(0, CWP), :] = jnp.zeros((CWP, LANE), F32)
        pad_s[pl.ds(CWP, T), :] = cv_ref[...] * _sig(cg_ref[...])
        wv = w_ref[0]
        bias = b_ref[...]

        def tile(i, carry):
            r0 = pl.multiple_of(i * tr, tr)
            acc = jnp.zeros((tr, LANE), F32) + bias
            for w in range(CW):
                acc = acc + pad_s[pl.ds(r0 + (CWP - CW + 1) + w, tr), :] * wv[w:w + 1, :]
            c1_ref[pl.ds(r0, tr), :] = acc
            return carry

        lax.fori_loop(0, T // tr, tile, 0)

    return pl.pallas_call(
        body, name="glu_conv", grid=(CD // LANE,),
        in_specs=[pl.BlockSpec((T, LANE), lambda cb: (0, 16 + cb)),
                  pl.BlockSpec((T, LANE), lambda cb: (0, 20 + cb)),
                  pl.BlockSpec((1, CWP, LANE), lambda cb: (cb, 0, 0)),
                  pl.BlockSpec((1, LANE), lambda cb: (0, cb))],
        out_specs=pl.BlockSpec((T, LANE), lambda cb: (0, cb)),
        out_shape=jax.ShapeDtypeStruct((T, CD), F32),
        scratch_shapes=[pltpu.VMEM((T + CWP, LANE), F32)],
        compiler_params=_cp(("arbitrary",)),
    )(u, u, dw, db)


def _ln_silu(c1, lg, lb):
    mu = jnp.mean(c1, axis=-1, keepdims=True)
    xc = c1 - mu
    rs = lax.rsqrt(jnp.mean(xc * xc, axis=-1, keepdims=True) + EPS)
    xh = xc * rs
    ln = xh * lg + lb
    s = _sig(ln)
    return xh, rs, ln, s


def _layer_tail(c1, u, ya, h, p, lg, lb, wpw, cg, wout, pg, wgate, wple):
    tm = 256

    def body(c1_ref, gc_ref, ya_ref, h_ref, p_ref, lg_ref, lb_ref, wpw_ref, cg_ref, wout_ref,
             pg_ref, wgate_ref, wple_ref, c3_ref, yc_ref, h1_ref, gate_ref, pe_ref, h2_ref):
        _, _, ln, s = _ln_silu(c1_ref[...], lg_ref[...], lb_ref[...])
        c2 = (ln * s).astype(BF16)
        c3 = _dot(c2, wpw_ref[...])
        gc = gc_ref[...]
        yc = (c3 * _rstd(c3) * cg_ref[...] * (gc * _sig(gc))).astype(BF16)
        c3_ref[...] = c3
        yc_ref[...] = yc
        y = _dot(ya_ref[...], wout_ref[pl.ds(0, AD), :]) + _dot(yc, wout_ref[pl.ds(AD, CD), :])
        h1 = h_ref[...] + y
        hn2 = (h1 * _rstd(h1) * pg_ref[...]).astype(BF16)
        gate = _sig(_dot(hn2, wgate_ref[...]))
        pb = p_ref[...].astype(BF16)
        pe = jnp.concatenate([_dot(pb, wple_ref[k]) for k in range(NCHIP)], axis=1)
        h1_ref[...] = h1
        gate_ref[...] = gate
        pe_ref[...] = pe
        h2_ref[...] = h1 + pe * gate

    row = lambda w: pl.BlockSpec((tm, w), lambda i: (i, 0))
    full = lambda *s: pl.BlockSpec(s, lambda i: (0,) * len(s))
    return pl.pallas_call(
        body, name="layer_tail", grid=(T // tm,),
        in_specs=[row(CD), pl.BlockSpec((tm, CD), lambda i: (i, 6)), row(AD), row(D), row(PLE),
                  full(1, CD), full(1, CD), full(CD, CD), full(1, CD), full(D, D),
                  full(1, D), full(D, D), full(NCHIP, PLE, PLE)],
        out_specs=[row(CD), row(CD), row(D), row(D), row(D), row(D)],
        out_shape=[jax.ShapeDtypeStruct((T, CD), F32), jax.ShapeDtypeStruct((T, CD), BF16)]
        + [jax.ShapeDtypeStruct((T, D), F32)] * 4,
        compiler_params=_cp(("arbitrary",), VMEM_BIG),
    )(c1, u, ya, h, p, lg, lb, wpw, cg, wout, pg, wgate, wple)


def _loss_head(h, tgt, fg):
    tm = 256

    def body(h_ref, t_ref, g_ref, loss_ref, dh_ref, dg_ref):
        @pl.when(pl.program_id(0) == 0)
        def _():
            loss_ref[...] = jnp.zeros_like(loss_ref)
            dg_ref[...] = jnp.zeros_like(dg_ref)
        hh = h_ref[...]
        g = g_ref[...]
        r = _rstd(hh)
        e = hh * r * g - t_ref[...]
        loss_ref[...] += 0.5 * jnp.sum(jnp.mean(e * e, axis=-1, keepdims=True))
        dy = e * (1.0 / D)
        dg_ref[...] += jnp.sum(dy * hh * r, axis=0, keepdims=True)
        dh_ref[...] = _rms_bwd(dy, hh, r, g)

    return pl.pallas_call(
        body, name="loss_head", grid=(T // tm,),
        in_specs=[pl.BlockSpec((tm, D), lambda i: (i, 0)), pl.BlockSpec((tm, D), lambda i: (i, 0)),
                  pl.BlockSpec((1, D), lambda i: (0, 0))],
        out_specs=[pl.BlockSpec((8, LANE), lambda i: (0, 0)), pl.BlockSpec((tm, D), lambda i: (i, 0)),
                   pl.BlockSpec((1, D), lambda i: (0, 0))],
        out_shape=[jax.ShapeDtypeStruct((8, LANE), F32), jax.ShapeDtypeStruct((T, D), F32),
                   jax.ShapeDtypeStruct((1, D), F32)],
        compiler_params=_cp(("arbitrary",)),
    )(h, tgt, fg)


def _ple_out_bwd(dh2, h1, gate, pe, p, ya, yc, pg, wgate, wout):
    tm = 256

    def body(dh2_ref, h1_ref, gate_ref, pe_ref, p_ref, ya_ref, yc_ref, pg_ref, wgate_ref, wout_ref,
             dh1_ref, dy_ref, dwg_ref, dwp_ref, dwo_ref, dpg_ref):
        @pl.when(pl.program_id(0) == 0)
        def _():
            dwg_ref[...] = jnp.zeros_like(dwg_ref)
            dwp_ref[...] = jnp.zeros_like(dwp_ref)
            dwo_ref[...] = jnp.zeros_like(dwo_ref)
            dpg_ref[...] = jnp.zeros_like(dpg_ref)
        dh2 = dh2_ref[...]
        h1 = h1_ref[...]
        gate = gate_ref[...]
        pg = pg_ref[...]
        dpe = (dh2 * gate).astype(BF16)
        dgp = (dh2 * pe_ref[...] * gate * (1.0 - gate)).astype(BF16)
        r = _rstd(h1)
        hn = h1 * r
        dwg_ref[...] += _dot_tn((hn * pg).astype(BF16), dgp)
        dhn2 = _dot_nt(dgp, wgate_ref[...])
        dpg_ref[...] += jnp.sum(dhn2 * hn, axis=0, keepdims=True)
        dh1 = dh2 + _rms_bwd(dhn2, h1, r, pg)
        pb = p_ref[...].astype(BF16)
        for k in range(NCHIP):
            dwp_ref[k] += _dot_tn(pb, dpe[:, k * PLE:(k + 1) * PLE])
        dh1b = dh1.astype(BF16)
        dy_ref[...] = _dot_nt(dh1b, wout_ref[...])
        dwo_ref[pl.ds(0, AD), :] += _dot_tn(ya_ref[...], dh1b)
        dwo_ref[pl.ds(AD, CD), :] += _dot_tn(yc_ref[...], dh1b)
        dh1_ref[...] = dh1

    row = lambda w: pl.BlockSpec((tm, w), lambda i: (i, 0))
    full = lambda *s: pl.BlockSpec(s, lambda i: (0,) * len(s))
    return pl.pallas_call(
        body, name="ple_out_bwd", grid=(T // tm,),
        in_specs=[row(D), row(D), row(D), row(D), row(PLE), row(AD), row(CD),
                  full(1, D), full(D, D), full(D, D)],
        out_specs=[row(D), row(D), full(D, D), full(NCHIP, PLE, PLE), full(D, D), full(1, D)],
        out_shape=[jax.ShapeDtypeStruct((T, D), F32), jax.ShapeDtypeStruct((T, D), F32),
                   jax.ShapeDtypeStruct((D, D), F32), jax.ShapeDtypeStruct((NCHIP, PLE, PLE), F32),
                   jax.ShapeDtypeStruct((D, D), F32), jax.ShapeDtypeStruct((1, D), F32)],
        compiler_params=_cp(("arbitrary",), VMEM_BIG),
    )(dh2, h1, gate, pe, p, ya, yc, pg, wgate, wout)


def _branch_bwd(dy, o, u, c1, c3, ag, lg, lb, wpw, cg, seg):
    tm = 256

    def body(dya_ref, dyc_ref, o_ref, ga_ref, gc_ref, c1_ref, c3_ref, ag_ref, lg_ref, lb_ref, wpw_ref,
             cg_ref, seg_ref, do_ref, dga_ref, dgc_ref, dc1_ref, dwpw_ref, dag_ref, dcg_ref, dlg_ref, dlb_ref):
        @pl.when(pl.program_id(0) == 0)
        def _():
            for r_ in (dwpw_ref, dag_ref, dcg_ref, dlg_ref, dlb_ref):
                r_[...] = jnp.zeros_like(r_)
        dya = dya_ref[...]
        o = o_ref[...]
        ga = ga_ref[...]
        ag_v = ag_ref[...]
        seg_m = seg_ref[...]
        r = lax.rsqrt(_dot2(o * o, seg_m) * (1.0 / DH) + EPS)
        onr = o * r
        sg = _sig(ga)
        dga_ref[...] = dya * (onr * ag_v) * (sg * (1.0 + ga * (1.0 - sg)))
        don = dya * (ga * sg)
        dag_ref[...] += jnp.sum(don * onr, axis=0, keepdims=True)
        dn = don * ag_v
        do_ref[...] = r * dn - o * (r * r * r) * (_dot2(dn * o, seg_m) * (1.0 / DH))
        dyc = dyc_ref[...]
        c3 = c3_ref[...]
        gc = gc_ref[...]
        cg_v = cg_ref[...]
        r3 = _rstd(c3)
        cn = c3 * r3
        sc = _sig(gc)
        dgc_ref[...] = dyc * (cn * cg_v) * (sc * (1.0 + gc * (1.0 - sc)))
        dcn = dyc * (gc * sc)
        dcg_ref[...] += jnp.sum(dcn * cn, axis=0, keepdims=True)
        dc3 = _rms_bwd(dcn, c3, r3, cg_v).astype(BF16)
        lg_v = lg_ref[...]
        xh, rs, ln, s = _ln_silu(c1_ref[...], lg_v, lb_ref[...])
        c2 = (ln * s).astype(BF16)
        dwpw_ref[...] += _dot_tn(c2, dc3)
        dc2 = _dot_nt(dc3, wpw_ref[...])
        dln = dc2 * (s * (1.0 + ln * (1.0 - s)))
        dlb_ref[...] += jnp.sum(dln, axis=0, keepdims=True)
        dlg_ref[...] += jnp.sum(dln * xh, axis=0, keepdims=True)
        dxh = dln * lg_v
        dc1_ref[...] = rs * (dxh - jnp.mean(dxh, axis=-1, keepdims=True)
                             - xh * jnp.mean(dxh * xh, axis=-1, keepdims=True))

    half = lambda j: pl.BlockSpec((tm, 512), lambda i: (i, j))
    full = lambda *s: pl.BlockSpec(s, lambda i: (0,) * len(s))
    vec = jax.ShapeDtypeStruct((1, 512), F32)
    act = jax.ShapeDtypeStruct((T, 512), F32)
    return pl.pallas_call(
        body, name="branch_bwd", grid=(T // tm,),
        in_specs=[half(0), half(1), half(0), half(3), half(6), half(0), half(0),
                  full(1, AD), full(1, CD), full(1, CD), full(CD, CD), full(1, CD), full(AD, AD)],
        out_specs=[half(0), half(0), half(0), half(0), full(CD, CD), full(1, 512), full(1, 512),
                   full(1, 512), full(1, 512)],
        out_shape=[act, act, act, act, jax.ShapeDtypeStruct((CD, CD), F32), vec, vec, vec, vec],
        compiler_params=_cp(("arbitrary",), VMEM_BIG),
    )(dy, dy, o, u, u, c1, c3, ag, lg, lb, wpw, cg, seg)


def _conv_bwd(dc1, u, dw):
    tr = 256
    off = CWP - CW + 1

    def body(d_ref, cv_ref, cg_ref, w_ref, dcv_ref, dcg_ref, ddw_ref, ddb_ref, padc_s, padd_s, acc_s):
        cv = cv_ref[...]
        sg = _sig(cg_ref[...])
        padc_s[pl.ds(0, CWP), :] = jnp.zeros((CWP, LANE), F32)
        padc_s[pl.ds(CWP, T), :] = cv * sg
        padd_s[pl.ds(0, T), :] = d_ref[...]
        padd_s[pl.ds(T, CWP), :] = jnp.zeros((CWP, LANE), F32)
        acc_s[...] = jnp.zeros_like(acc_s)
        wv = w_ref[0]

        def tile(i, carry):
            r0 = pl.multiple_of(i * tr, tr)
            dt = padd_s[pl.ds(r0, tr), :]
            dc0 = jnp.zeros((tr, LANE), F32)
            for w in range(CW):
                dc0 = dc0 + padd_s[pl.ds(r0 + (CW - 1) - w, tr), :] * wv[w:w + 1, :]
                prod = dt * padc_s[pl.ds(r0 + off + w, tr), :]
                acc_s[w] += jnp.sum(prod.reshape(tr // 8, 8, LANE), axis=0)
            cvt = cv_ref[pl.ds(r0, tr), :]
            sgt = _sig(cg_ref[pl.ds(r0, tr), :])
            dcv_ref[pl.ds(r0, tr), :] = dc0 * sgt
            dcg_ref[pl.ds(r0, tr), :] = dc0 * cvt * sgt * (1.0 - sgt)
            return carry

        lax.fori_loop(0, T // tr, tile, 0)
        ddw_ref[0] = jnp.sum(acc_s[...], axis=1)
        ddb_ref[...] = jnp.sum(d_ref[...], axis=0, keepdims=True)

    col = lambda j: pl.BlockSpec((T, LANE), lambda cb: (0, j + cb))
    return pl.pallas_call(
        body, name="conv_bwd", grid=(CD // LANE,),
        in_specs=[col(0), col(16), col(20), pl.BlockSpec((1, CWP, LANE), lambda cb: (cb, 0, 0))],
        out_specs=[col(0), col(0), pl.BlockSpec((1, CWP, LANE), lambda cb: (cb, 0, 0)),
                   pl.BlockSpec((1, LANE), lambda cb: (0, cb))],
        out_shape=[jax.ShapeDtypeStruct((T, CD), F32), jax.ShapeDtypeStruct((T, CD), F32),
                   jax.ShapeDtypeStruct((NCHIP, CWP, LANE), F32), jax.ShapeDtypeStruct((1, CD), F32)],
        scratch_shapes=[pltpu.VMEM((T + CWP, LANE), F32), pltpu.VMEM((T + CWP, LANE), F32),
                        pltpu.VMEM((CWP, 8, LANE), F32)],
        compiler_params=_cp(("arbitrary",)),
    )(dc1, u, u, dw)


def _attn_bwd(u, do, tot, partials=()):
    n_x = len(partials)
    grid = (NG, T // AQ)

    def body(q_ref, k_ref, v_ref, do_ref, tot_ref, dq_ref, dk_ref, dv_ref, kb_s, vb_s):
        qi = pl.program_id(1)

        @pl.when(qi == 0)
        def _():
            kb_s[...] = k_ref[...].astype(BF16)
            vb_s[...] = v_ref[...].astype(BF16)
            dk_ref[...] = jnp.zeros_like(dk_ref)
            dv_ref[...] = jnp.zeros_like(dv_ref)

        causal, tr, tc, heads = _attn_tiles()
        upper = (tr > tc).astype(BF16)
        lower = (tr < tc).astype(BF16)
        q = q_ref[...]
        qs = _stack_heads(q * 0.125, heads).astype(BF16)
        qus = _stack_heads(q, heads).astype(BF16)
        dos = _stack_heads(do_ref[...], heads).astype(BF16)
        totv = tot_ref[0]
        tots = jnp.concatenate([totv[:, h * DH:h * DH + 1] for h in range(HG)], axis=0)

        def block(kb, carry, masked):
            lm_left, dl_left, dq = carry
            k0 = pl.multiple_of(kb * AQ, AQ)
            kk = kb_s[pl.ds(k0, AQ), :]
            vv = vb_s[pl.ds(k0, AQ), :]
            z = _dot_nt(qs, kk)
            sp = _softplus(z)
            lm = jnp.where(causal, -sp, 0.0) if masked else -sp
            lm_incl = lm_left + jnp.sum(lm, axis=1, keepdims=True)
            att = jnp.exp((z - sp) + _tri_sum(lm, upper) + (tots - lm_incl))
            if masked:
                att = jnp.where(causal, att, 0.0)
            dl = att * _dot_nt(dos, vv)
            dv_ref[pl.ds(k0, AQ), :] += _dot_tn(att.astype(BF16), dos)
            prefix = dl_left + _tri_sum(dl, lower)
            beta = jnp.exp(z - sp)
            dz = (1.0 - beta) * dl - beta * prefix
            if masked:
                dz = jnp.where(causal, dz, 0.0)
            dzs = (dz * 0.125).astype(BF16)
            dk_ref[pl.ds(k0, AQ), :] += _dot_tn(dzs, qus)
            return lm_incl, dl_left + jnp.sum(dl, axis=1, keepdims=True), dq + _dot(dzs, kk)

        zero = jnp.zeros((SR, 1), F32)
        carry = lax.fori_loop(0, qi, lambda kb, c: block(kb, c, False),
                              (zero, zero, jnp.zeros((SR, GW), F32)))
        _, _, dq = block(qi, carry, True)
        dq_ref[...] = _unstack_heads(dq, heads)

    res = pl.pallas_call(
        _host(body, grid, 5, 3, n_x, _scatter_copies), name="attn_bwd", grid=grid,
        in_specs=[pl.BlockSpec((AQ, GW), lambda hg, qi: (qi, hg)),
                  pl.BlockSpec((T, GW), lambda hg, qi: (0, NG + hg)),
                  pl.BlockSpec((T, GW), lambda hg, qi: (0, 2 * NG + hg)),
                  pl.BlockSpec((AQ, GW), lambda hg, qi: (qi, hg)),
                  pl.BlockSpec((1, AQ, GW), lambda hg, qi: (hg, qi, 0))] + [HBM_SPEC] * n_x,
        out_specs=[pl.BlockSpec((AQ, GW), lambda hg, qi: (qi, hg)),
                   pl.BlockSpec((T, GW), lambda hg, qi: (0, hg)),
                   pl.BlockSpec((T, GW), lambda hg, qi: (0, hg))] + [HBM_SPEC] * n_x,
        out_shape=[jax.ShapeDtypeStruct((T, AD), F32)] * 3
        + [jax.ShapeDtypeStruct((NCHIP - 1,) + a.shape[1:], a.dtype) for a in partials],
        scratch_shapes=[pltpu.VMEM((T, GW), BF16), pltpu.VMEM((T, GW), BF16)] + _hosted_sems(n_x),
        compiler_params=_cp(("arbitrary", "arbitrary"), VMEM_BIG),
    )(u, u, u, do, tot, *partials)
    return res[0], res[1], res[2], list(res[3:])


def _inproj_dw(hn, du):
    tm = 512

    def body(hn_ref, du_ref, dw_ref):
        @pl.when(pl.program_id(1) == 0)
        def _():
            dw_ref[...] = jnp.zeros_like(dw_ref)
        dw_ref[0] += _dot_tn(hn_ref[...], du_ref[...])

    return pl.pallas_call(
        body, name="inproj_dw", grid=(NCHIP, T // tm),
        in_specs=[pl.BlockSpec((tm, D), lambda k, i: (i, 0)), pl.BlockSpec((tm, SHW), lambda k, i: (i, k))],
        out_specs=pl.BlockSpec((1, D, SHW), lambda k, i: (k, 0, 0)),
        out_shape=jax.ShapeDtypeStruct((NCHIP, D, SHW), F32),
        compiler_params=_cp(("arbitrary", "arbitrary")),
    )(hn, du)


def _inproj_dx(du, w, h, g, dres, partials=()):
    tm = 512
    n_x = len(partials)
    grid = (T // tm, NCHIP)

    def body(du_ref, w_ref, h_ref, g_ref, dres_ref, dh_ref, dg_ref, acc_s):
        i, k = pl.program_id(0), pl.program_id(1)

        @pl.when(jnp.logical_and(i == 0, k == 0))
        def _():
            dg_ref[...] = jnp.zeros_like(dg_ref)

        part = _dot_nt(du_ref[...], w_ref[0])

        @pl.when(k == 0)
        def _():
            acc_s[...] = part

        @pl.when(k > 0)
        def _():
            acc_s[...] += part

        @pl.when(k == NCHIP - 1)
        def _():
            hh = h_ref[...]
            r = _rstd(hh)
            dhn = acc_s[...]
            dg_ref[...] += jnp.sum(dhn * hh * r, axis=0, keepdims=True)
            dh_ref[...] = dres_ref[...] + _rms_bwd(dhn, hh, r, g_ref[...])

    res = pl.pallas_call(
        _host(body, grid, 5, 2, n_x, _scatter_copies), name="inproj_dx", grid=grid,
        in_specs=[pl.BlockSpec((tm, SHW), lambda i, k: (i, k)),
                  pl.BlockSpec((1, D, SHW), lambda i, k: (k, 0, 0)),
                  pl.BlockSpec((tm, D), lambda i, k: (i, 0)),
                  pl.BlockSpec((1, D), lambda i, k: (0, 0)),
                  pl.BlockSpec((tm, D), lambda i, k: (i, 0))] + [HBM_SPEC] * n_x,
        out_specs=[pl.BlockSpec((tm, D), lambda i, k: (i, 0)), pl.BlockSpec((1, D), lambda i, k: (0, 0))]
        + [HBM_SPEC] * n_x,
        out_shape=[jax.ShapeDtypeStruct((T, D), F32), jax.ShapeDtypeStruct((1, D), F32)]
        + [jax.ShapeDtypeStruct((NCHIP - 1,) + a.shape[1:], a.dtype) for a in partials],
        scratch_shapes=[pltpu.VMEM((tm, D), F32)] + _hosted_sems(n_x),
        compiler_params=_cp(("arbitrary", "arbitrary")),
    )(du, w, h, g, dres, *partials)
    return res[0], res[1], list(res[2:])


def _sum_pair(core, grads, gots):
    n = len(grads)

    def body(c_ref, *refs):
        for a in range(n):
            refs[2 * n + a][...] = (refs[a][...] + refs[n + a][...]).astype(BF16)

    mine = [pl.BlockSpec((1,) + s.shape[1:], lambda k, c: (k, c[0], 0)) for s in gots]
    same = [pl.BlockSpec((1,) + s.shape[1:], lambda k, c: (k, 0, 0)) for s in gots]
    return pl.pallas_call(
        body, name="sum_pair",
        grid_spec=pltpu.PrefetchScalarGridSpec(
            num_scalar_prefetch=1, grid=(NCHIP,), in_specs=mine + same, out_specs=same),
        out_shape=[jax.ShapeDtypeStruct(s.shape, BF16) for s in gots],
        compiler_params=_cp(("arbitrary",), VMEM_BIG),
    )(core, *grads, *gots)


def _sum_chips(chip, partials, gots):
    n = len(partials)

    def body(c_ref, *refs):
        for a in range(n):
            acc = refs[a][0].astype(F32)
            for j in range(NCHIP - 1):
                acc = acc + refs[n + a][j].astype(F32)
            refs[2 * n + a][...] = acc

    return pl.pallas_call(
        body, name="sum_chips",
        grid_spec=pltpu.PrefetchScalarGridSpec(
            num_scalar_prefetch=1, grid=(1,),
            in_specs=[pl.BlockSpec((1,) + s.shape[1:], lambda i, c: (c[0], 0, 0)) for s in partials]
            + [pl.BlockSpec(s.shape, lambda i, c: (0, 0, 0)) for s in gots],
            out_specs=[pl.BlockSpec(s.shape[1:], lambda i, c: (0, 0)) for s in partials]),
        out_shape=[jax.ShapeDtypeStruct(s.shape[1:], F32) for s in partials],
        compiler_params=_cp(("arbitrary",), VMEM_BIG),
    )(chip, *partials, *gots)


def _adamw(w, g, m, v, rows):
    R, C = w.shape
    c1 = 1.0 - ADAM_B1 ** ADAM_STEP
    c2 = 1.0 - ADAM_B2 ** ADAM_STEP

    def body(w_ref, g_ref, m_ref, v_ref, d_ref, nm_ref, nv_ref):
        gg = g_ref[...]
        nm = ADAM_B1 * m_ref[...] + (1.0 - ADAM_B1) * gg
        nv = ADAM_B2 * v_ref[...] + (1.0 - ADAM_B2) * (gg * gg)
        d_ref[...] = -ADAM_LR * ((nm / c1) / (jnp.sqrt(nv / c2) + ADAM_EPS) + ADAM_WD * w_ref[...])
        nm_ref[...] = nm
        nv_ref[...] = nv

    spec = pl.BlockSpec((rows, C), lambda i: (i, 0))
    sh = jax.ShapeDtypeStruct((R, C), F32)
    return pl.pallas_call(
        body, name="adamw", grid=(R // rows,), in_specs=[spec] * 4, out_specs=[spec] * 3,
        out_shape=[sh, sh, sh], compiler_params=_cp(("arbitrary",)),
    )(w, g, m, v)


HBM_SPEC = pl.BlockSpec(memory_space=pltpu.HBM)


def _place():
    return lax.axis_index("x"), lax.axis_index("y"), lax.axis_index("c")


def _all_gather_split(shard):
    half = shard.shape[0] // 2

    def body(in_ref, out_ref, send_sems, recv_sems):
        x, y, c = _place()
        me = 2 * x + y
        mine = pl.ds(c * half, half)
        chips = [(1 - x, y), (x, 1 - y), (1 - x, 1 - y)]
        fetch = [pltpu.make_async_remote_copy(
            src_ref=in_ref.at[mine], dst_ref=out_ref.at[me, mine], send_sem=send_sems.at[r], recv_sem=recv_sems.at[r],
            device_id=(px, py, c), device_id_type=MESH) for r, (px, py) in enumerate(chips)]
        for cp in fetch:
            cp.start()
        passed = []
        for r, (px, py) in enumerate(chips):
            landed = out_ref.at[2 * px + py, mine]
            fetch[r].wait_recv()
            cp = pltpu.make_async_remote_copy(
                src_ref=landed, dst_ref=landed, send_sem=send_sems.at[3 + r], recv_sem=recv_sems.at[3 + r],
                device_id=(x, y, 1 - c), device_id_type=MESH)
            cp.start()
            passed.append(cp)
        for cp in passed:
            cp.wait_recv()
        for cp in fetch + passed:
            cp.wait_send()

    return pl.pallas_call(
        body, name="all_gather_split", in_specs=[HBM_SPEC], out_specs=HBM_SPEC,
        out_shape=jax.ShapeDtypeStruct((NCHIP,) + shard.shape, shard.dtype),
        scratch_shapes=[pltpu.SemaphoreType.DMA((6,)), pltpu.SemaphoreType.DMA((6,))],
    )(shard)


def _pair_exchange(grads):
    n = len(grads)

    def body(*refs):
        x, y, c = _place()
        copies = []
        for a in range(n):
            half = grads[a].shape[1] // 2
            copies.append(pltpu.make_async_remote_copy(
                src_ref=refs[a].at[:, pl.ds((1 - c) * half, half), :], dst_ref=refs[n + a],
                send_sem=refs[2 * n].at[a], recv_sem=refs[2 * n + 1].at[a],
                device_id=(x, y, 1 - c), device_id_type=MESH))
        for cp in copies:
            cp.start()
        for cp in copies:
            cp.wait()

    return pl.pallas_call(
        body, name="pair_exchange", in_specs=[HBM_SPEC] * n, out_specs=[HBM_SPEC] * n,
        out_shape=[jax.ShapeDtypeStruct((NCHIP, g.shape[1] // 2, g.shape[2]), F32) for g in grads],
        scratch_shapes=[pltpu.SemaphoreType.DMA((n,)), pltpu.SemaphoreType.DMA((n,))],
    )(*grads)


def _pair_share(mine):
    n = len(mine)

    def body(*refs):
        x, y, c = _place()
        copies = [pltpu.make_async_remote_copy(
            src_ref=refs[a], dst_ref=refs[n + a], send_sem=refs[2 * n].at[a], recv_sem=refs[2 * n + 1].at[a],
            device_id=(x, y, 1 - c), device_id_type=MESH) for a in range(n)]
        for cp in copies:
            cp.start()
        for cp in copies:
            cp.wait()

    return pl.pallas_call(
        body, name="pair_share", in_specs=[HBM_SPEC] * n, out_specs=[HBM_SPEC] * n,
        out_shape=[jax.ShapeDtypeStruct(t.shape, F32) for t in mine],
        scratch_shapes=[pltpu.SemaphoreType.DMA((n,)), pltpu.SemaphoreType.DMA((n,))],
    )(*mine)


def _small_allreduce(mine):
    def body(m_ref, o_ref, slots, send_sems, recv_sems):
        x, y, c = _place()
        me = 4 * x + 2 * y + c
        slots[me] = m_ref[...]
        copies = []
        for r in range(1, 8):
            rx, ry, rc = (r >> 2) & 1, (r >> 1) & 1, r & 1
            peer = (x + rx - 2 * x * rx, y + ry - 2 * y * ry, c + rc - 2 * c * rc)
            cp = pltpu.make_async_remote_copy(
                src_ref=m_ref, dst_ref=slots.at[me], send_sem=send_sems.at[r - 1], recv_sem=recv_sems.at[r - 1],
                device_id=peer, device_id_type=MESH)
            cp.start()
            copies.append(cp)
        for cp in copies:
            cp.wait()
        acc = slots[0]
        for j in range(1, 8):
            acc = acc + slots[j]
        o_ref[...] = acc

    return pl.pallas_call(
        body, name="small_allreduce",
        in_specs=[pl.BlockSpec(memory_space=pltpu.VMEM)], out_specs=pl.BlockSpec(memory_space=pltpu.VMEM),
        out_shape=jax.ShapeDtypeStruct((SMALL_ROWS, LANE), F32),
        scratch_shapes=[pltpu.VMEM((8, SMALL_ROWS, LANE), F32), pltpu.SemaphoreType.DMA((7,)),
                        pltpu.SemaphoreType.DMA((7,))],
    )(mine)


def _seg_matrix():
    i = lax.broadcasted_iota(jnp.int32, (AD, AD), 0) // DH
    j = lax.broadcasted_iota(jnp.int32, (AD, AD), 1) // DH
    return (i == j).astype(BF16)


TAIL = ("w_out", "w_ple_gate", "w_ple", "w_pw", "dw_w")


def _local_step(x, p, tgt, sm, shards, chip, ci):
    seg = _seg_matrix()
    core = jnp.reshape(ci, (1,)).astype(jnp.int32)
    chip_idx = jnp.reshape(chip, (1,)).astype(jnp.int32)
    own = lambda g, s: lax.dynamic_update_index_in_dim(g, s, chip, 0)
    w_in_next = own(_all_gather_split(shards[0]["w_in"]), shards[0]["w_in"])
    h = x
    saved = []
    for l in range(DEPTH):
        w_in = w_in_next
        row = lambda name: sm[name][l:l + 1]
        u, hn = _rms_inproj(h, row("norm_g"), w_in)
        todo = [shards[l][k] for k in TAIL] + ([shards[l + 1]["w_in"]] if l + 1 < DEPTH else [])
        o, ya, tot, got = _attn_fwd(u, jnp.tile(row("attn_out_g"), (1, HG)), todo)
        got = [own(g, s) for g, s in zip(got, todo)]
        w_out = got[0].reshape(D, D)
        w_gate = got[1].reshape(D, D)
        w_ple = got[2]
        w_pw = got[3].reshape(CD, CD)
        dw = got[4]
        if l + 1 < DEPTH:
            w_in_next = got[5]
        c1 = _glu_conv(u, dw, row("dw_b"))
        c3, yc, h1, gate, pe, h2 = _layer_tail(
            c1, u, ya, h, p[l], row("conv_ln_g"), row("conv_ln_b"), w_pw, row("conv_out_g"), w_out,
            row("ple_norm_g"), w_gate, w_ple)
        saved.append(dict(h=h, u=u, hn=hn, o=o, ya=ya, tot=tot, c1=c1, c3=c3, yc=yc, h1=h1, gate=gate, pe=pe,
                          w_in=w_in, w_out=w_out, w_gate=w_gate, w_pw=w_pw, dw=dw))
        h = h2
    loss_blk, dh, dfg = _loss_head(h, tgt, sm["final_g"])
    small = [None] * DEPTH
    pending, partials, arrived = [], {}, {}
    pair_sum = lambda grads: _sum_pair(core, grads, _pair_exchange(grads))
    for l in reversed(range(DEPTH)):
        s = saved[l]
        row = lambda name: sm[name][l:l + 1]
        dh1, dy, dwg, dwp, dwo, dpg = _ple_out_bwd(
            dh, s["h1"], s["gate"], s["pe"], p[l], s["ya"], s["yc"], row("ple_norm_g"), s["w_gate"], s["w_out"])
        ag_t = jnp.tile(row("attn_out_g"), (1, AD // DH))
        do, dga, dgc, dc1, dwpw, dag, dcg, dlg, dlb = _branch_bwd(
            dy, s["o"], s["u"], s["c1"], s["c3"], ag_t, row("conv_ln_g"), row("conv_ln_b"), s["w_pw"],
            row("conv_out_g"), seg)
        dcv, dcgate, ddw, ddb = _conv_bwd(dc1, s["u"], s["dw"])
        tail = [dwo.reshape(NCHIP, 256, D), dwg.reshape(NCHIP, 256, D), dwp, dwpw.reshape(NCHIP, 128, CD), ddw]
        if l == 0:
            partials[(l, "tail")] = pair_sum(tail)
            pending.append((l, "tail"))
        send = [t for key in pending for t in partials[key]]
        dq, dk, dv, got = _attn_bwd(s["u"], do, s["tot"], send)
        for key in pending:
            arrived[key], got = got[:len(partials[key])], got[len(partials[key]):]
        pending = []
        du = jnp.concatenate([dq, dk, dv, dga, dcv, dcgate, dgc], axis=1).astype(BF16)
        dwin = _inproj_dw(s["hn"], du)
        if l == 0:
            partials[(l, "w_in")] = pair_sum([dwin])
            dh, dng, arrived[(l, "w_in")] = _inproj_dx(du, s["w_in"], s["h"], row("norm_g"), dh1,
                                                      partials[(l, "w_in")])
        else:
            dh, dng, _ = _inproj_dx(du, s["w_in"], s["h"], row("norm_g"), dh1)
            tail_p = pair_sum(tail + [dwin])
            partials[(l, "tail")], partials[(l, "w_in")] = tail_p[:-1], tail_p[-1:]
            pending = [(l, "tail"), (l, "w_in")]
        small[l] = dict(norm_g=dng, attn_out_g=dag.reshape(AD // DH, DH).sum(axis=0, keepdims=True), dw_b=ddb,
                        conv_ln_g=dlg, conv_ln_b=dlb, conv_out_g=dcg, ple_norm_g=dpg)
    big = []
    for l in range(DEPTH):
        ps = partials[(l, "w_in")] + partials[(l, "tail")]
        mine = _sum_chips(chip_idx, ps, arrived[(l, "w_in")] + arrived[(l, "tail")])
        theirs = _pair_share(mine)
        big.append(dict(zip(BIG, [jnp.where(core[0] == 0, jnp.concatenate([m, t]), jnp.concatenate([t, m]))
                                  for m, t in zip(mine, theirs)])))
    return loss_blk[0, 0], dh, big, small, dfg


BIG = ("w_in", "w_out", "w_ple_gate", "w_ple", "w_pw", "dw_w")
SMALL2 = ("norm_g", "ple_norm_g", "dw_b", "conv_ln_g", "conv_ln_b", "conv_out_g", "attn_out_g")


def _pack_small(two, final):
    flat = jnp.concatenate([two[k].reshape(-1) for k in SMALL2] + [final.reshape(-1)])
    flat = jnp.concatenate([flat, jnp.zeros((SMALL_ROWS * LANE - flat.shape[0],), F32)])
    return flat.reshape(SMALL_ROWS, LANE)


def _unpack_small(packed, like_two, like_final):
    flat = packed.reshape(-1)
    out, off = {}, 0
    for k in SMALL2:
        n = like_two[k].size
        out[k] = flat[off:off + n].reshape(like_two[k].shape)
        off += n
    return out, flat[off:off + like_final.size].reshape(like_final.shape)


def kernel(x, p, norm_g, w_in, attn_out_g, dw_w, dw_b, conv_ln_g, conv_ln_b, w_pw, conv_out_g, w_out, ple_norm_g, w_ple_gate, w_ple, final_g, loss_target, m_norm_g, m_w_in, m_attn_out_g, m_dw_w, m_dw_b, m_conv_ln_g, m_conv_ln_b, m_w_pw, m_conv_out_g, m_w_out, m_ple_norm_g, m_w_ple_gate, m_w_ple, m_final_g, v_norm_g, v_w_in, v_attn_out_g, v_dw_w, v_dw_b, v_conv_ln_g, v_conv_ln_b, v_w_pw, v_conv_out_g, v_w_out, v_ple_norm_g, v_w_ple_gate, v_w_ple, v_final_g):
    W = dict(norm_g=norm_g, w_in=w_in, attn_out_g=attn_out_g, dw_w=dw_w, dw_b=dw_b, conv_ln_g=conv_ln_g,
             conv_ln_b=conv_ln_b, w_pw=w_pw, conv_out_g=conv_out_g, w_out=w_out, ple_norm_g=ple_norm_g,
             w_ple_gate=w_ple_gate, w_ple=w_ple, final_g=final_g)
    M = dict(norm_g=m_norm_g, w_in=m_w_in, attn_out_g=m_attn_out_g, dw_w=m_dw_w, dw_b=m_dw_b,
             conv_ln_g=m_conv_ln_g, conv_ln_b=m_conv_ln_b, w_pw=m_w_pw, conv_out_g=m_conv_out_g, w_out=m_w_out,
             ple_norm_g=m_ple_norm_g, w_ple_gate=m_w_ple_gate, w_ple=m_w_ple, final_g=m_final_g)
    V = dict(norm_g=v_norm_g, w_in=v_w_in, attn_out_g=v_attn_out_g, dw_w=v_dw_w, dw_b=v_dw_b,
             conv_ln_g=v_conv_ln_g, conv_ln_b=v_conv_ln_b, w_pw=v_w_pw, conv_out_g=v_conv_out_g, w_out=v_w_out,
             ple_norm_g=v_ple_norm_g, w_ple_gate=v_w_ple_gate, w_ple=v_w_ple, final_g=v_final_g)
    order = ("norm_g", "w_in", "attn_out_g", "dw_w", "dw_b", "conv_ln_g", "conv_ln_b", "w_pw", "conv_out_g",
             "w_out", "ple_norm_g", "w_ple_gate", "w_ple", "final_g")

    pad_taps = lambda a: jnp.pad(a, ((0, 0), (0, CWP - CW), (0, 0)))
    cast = dict(w_in=w_in.astype(BF16), w_out=w_out.astype(BF16), w_ple_gate=w_ple_gate.astype(BF16),
                w_ple=w_ple.astype(BF16), w_pw=w_pw.astype(BF16), dw_w=pad_taps(dw_w))
    shards = [{k: v[l] for k, v in cast.items()} for l in range(DEPTH)]
    xi, yi, ci = lax.axis_index("x"), lax.axis_index("y"), lax.axis_index("c")
    chip = 2 * xi + yi

    sm = {k: W[k] for k in SMALL2}
    sm["final_g"] = final_g.reshape(1, D)
    loss_part, grad_x, big, small, dfg = _local_step(x[0], p[:, 0], loss_target[0], sm, shards, chip, ci)
    g_big = {name: jnp.stack([big[l][name] for l in range(DEPTH)]).reshape(cast[name].shape) for name in BIG}

    small_two = {k: jnp.concatenate([small[l][k] for l in range(DEPTH)], axis=0) for k in SMALL2}
    g_small_packed = _small_allreduce(_pack_small(small_two, dfg).at[SMALL_ROWS - 1, LANE - 1].set(loss_part))
    loss = g_small_packed[SMALL_ROWS - 1, LANE - 1]
    g_small, g_final = _unpack_small(g_small_packed, {k: W[k] for k in SMALL2}, final_g)

    grads, deltas, new_m, new_v = {}, {}, {}, {}
    for name in BIG:
        wv = pad_taps(W[name]) if name == "dw_w" else W[name]
        mv = pad_taps(M[name]) if name == "dw_w" else M[name]
        vv = pad_taps(V[name]) if name == "dw_w" else V[name]
        gg = g_big[name]
        cols = wv.shape[-1]
        rows_total = wv.size // cols
        tile_rows = min(rows_total, 256)
        d2, m2, v2 = _adamw(wv.reshape(rows_total, cols), gg.reshape(rows_total, cols),
                            mv.reshape(rows_total, cols), vv.reshape(rows_total, cols), tile_rows)
        if name == "dw_w":
            cut = lambda a: a.reshape(DEPTH, CWP, LANE)[:, :CW]
            grads[name], deltas[name], new_m[name], new_v[name] = cut(gg), cut(d2), cut(m2), cut(v2)
        else:
            grads[name] = gg
            deltas[name], new_m[name], new_v[name] = (t.reshape(wv.shape) for t in (d2, m2, v2))
    ws = _pack_small({k: W[k] for k in SMALL2}, final_g)
    ms = _pack_small({k: M[k] for k in SMALL2}, m_final_g)
    vs = _pack_small({k: V[k] for k in SMALL2}, v_final_g)
    ds, nms, nvs = _adamw(ws, g_small_packed, ms, vs, SMALL_ROWS)
    for packed, dst in ((ds, deltas), (nms, new_m), (nvs, new_v)):
        two, fin = _unpack_small(packed, {k: W[k] for k in SMALL2}, final_g)
        dst.update(two)
        dst["final_g"] = fin
    grads.update(g_small)
    grads["final_g"] = g_final

    return (loss, grad_x[None], *[grads[n] for n in order], *[deltas[n] for n in order],
            *[new_m[n] for n in order], *[new_v[n] for n in order])
```

```python
import functools

import jax
import jax.numpy as jnp
from jax import lax
from jax.experimental import pallas as pl
from jax.experimental.pallas import tpu as pltpu

F32 = jnp.float32
BF16 = jnp.bfloat16

T = 2048
D = 1024
DIN = 3584
NCHIP = 4
SHW = DIN // NCHIP
AD = 512
CD = 512
DH = 64
CW = 31
CWP = 32
PLE = 256
DEPTH = 2
EPS = 1e-6
AQ = 256
HG = 4
GW = HG * DH
SR = HG * AQ
NG = AD // GW
LANE = 128

ADAM_LR = 0.001
ADAM_B1 = 0.9
ADAM_B2 = 0.999
ADAM_EPS = 1e-08
ADAM_WD = 0.01
ADAM_STEP = 10

SMALL_ROWS = 80

VMEM_BIG = 56 * 1024 * 1024
MESH = pl.DeviceIdType.MESH


def _cp(sem=None, vmem=None):
    kw = {}
    if sem is not None:
        kw["dimension_semantics"] = sem
    if vmem is not None:
        kw["vmem_limit_bytes"] = vmem
    return pltpu.CompilerParams(**kw)


def _dot(a, b):
    return jnp.dot(a, b, preferred_element_type=F32)


def _dot_nt(a, b):
    return lax.dot_general(a, b, (((1,), (1,)), ((), ())), preferred_element_type=F32)


def _dot_tn(a, b):
    return lax.dot_general(a, b, (((0,), (0,)), ((), ())), preferred_element_type=F32)


def _dot2(x, m):
    hi = x.astype(BF16)
    lo = (x - hi.astype(F32)).astype(BF16)
    return _dot(hi, m) + _dot(lo, m)


def _sig(x):
    return 1.0 / (1.0 + jnp.exp(-x))


def _softplus(z):
    return jnp.maximum(z, 0.0) + jnp.log(1.0 + jnp.exp(-jnp.abs(z)))


def _rstd(x):
    return lax.rsqrt(jnp.mean(x * x, axis=-1, keepdims=True) + EPS)


def _rms_bwd(dy, x, r, g):
    dn = dy * g
    return r * dn - x * (r * r * r) * jnp.mean(dn * x, axis=-1, keepdims=True)


def _rms_inproj(h, g, w):
    tm = 512

    def body(h_ref, g_ref, w_ref, u_ref, hn_ref, hn_s):
        @pl.when(pl.program_id(1) == 0)
        def _():
            hh = h_ref[...]
            hn = (hh * _rstd(hh) * g_ref[...]).astype(BF16)
            hn_s[...] = hn
            hn_ref[...] = hn
        u_ref[...] = _dot(hn_s[...], w_ref[0])

    return pl.pallas_call(
        body, name="rms_inproj", grid=(T // tm, NCHIP),
        in_specs=[pl.BlockSpec((tm, D), lambda i, k: (i, 0)),
                  pl.BlockSpec((1, D), lambda i, k: (0, 0)),
                  pl.BlockSpec((1, D, SHW), lambda i, k: (k, 0, 0))],
        out_specs=[pl.BlockSpec((tm, SHW), lambda i, k: (i, k)),
                   pl.BlockSpec((tm, D), lambda i, k: (i, 0))],
        out_shape=[jax.ShapeDtypeStruct((T, DIN), F32), jax.ShapeDtypeStruct((T, D), BF16)],
        scratch_shapes=[pltpu.VMEM((tm, D), BF16)],
        compiler_params=_cp(("arbitrary", "arbitrary")),
    )(h, g, w)


def _attn_tiles():
    row = lax.broadcasted_iota(jnp.int32, (SR, AQ), 0) & (AQ - 1)
    col = lax.broadcasted_iota(jnp.int32, (SR, AQ), 1)
    tr = lax.broadcasted_iota(jnp.int32, (AQ, AQ), 0)
    tc = lax.broadcasted_iota(jnp.int32, (AQ, AQ), 1)
    lane_head = lax.broadcasted_iota(jnp.int32, (1, GW), 1) // DH
    return col < row, tr, tc, [lane_head == h for h in range(HG)]


def _stack_heads(t, heads):
    return jnp.concatenate([jnp.where(m, t, 0.0) for m in heads], axis=0)


def _unstack_heads(t, heads):
    out = t[:AQ]
    for h in range(1, HG):
        out = jnp.where(heads[h], t[h * AQ:(h + 1) * AQ], out)
    return out


def _tri_sum(x, tri):
    hi = x.astype(BF16)
    lo = (x - hi.astype(F32)).astype(BF16)
    both = _dot(jnp.concatenate([hi, lo], axis=0), tri)
    return both[:SR] + both[SR:]


def _scatter_copies(ps, gots, send_sems, recv_sems):
    x, y, c = _place()
    peers = [(1 - x, y), (x, 1 - y), (1 - x, 1 - y)]
    return [pltpu.make_async_remote_copy(
        src_ref=ps[a].at[2 * px + py], dst_ref=gots[a].at[r], send_sem=send_sems.at[3 * a + r],
        recv_sem=recv_sems.at[3 * a + r], device_id=(px, py, c), device_id_type=MESH)
        for a in range(len(ps)) for r, (px, py) in enumerate(peers)]


def _gather_split_copies(ins, outs, send_sems, recv_sems):
    x, y, c = _place()
    me = 2 * x + y
    chips = [(1 - x, y), (x, 1 - y), (1 - x, 1 - y)]
    fetch, passed = [], []
    for a in range(len(ins)):
        half = ins[a].shape[0] // 2
        mine = pl.ds(c * half, half)
        for r, (px, py) in enumerate(chips):
            fetch.append(pltpu.make_async_remote_copy(
                src_ref=ins[a].at[mine], dst_ref=outs[a].at[me, mine], send_sem=send_sems.at[6 * a + r],
                recv_sem=recv_sems.at[6 * a + r], device_id=(px, py, c), device_id_type=MESH))
            landed = outs[a].at[2 * px + py, mine]
            passed.append(pltpu.make_async_remote_copy(
                src_ref=landed, dst_ref=landed, send_sem=send_sems.at[6 * a + 3 + r],
                recv_sem=recv_sems.at[6 * a + 3 + r], device_id=(x, y, 1 - c), device_id_type=MESH))
    return fetch, passed


def _pair_copies(ins, outs, send_sems, recv_sems):
    x, y, c = _place()
    copies = []
    for a in range(len(ins)):
        half = ins[a].shape[1] // 2
        copies.append(pltpu.make_async_remote_copy(
            src_ref=ins[a].at[:, pl.ds((1 - c) * half, half), :], dst_ref=outs[a], send_sem=send_sems.at[a],
            recv_sem=recv_sems.at[a], device_id=(x, y, 1 - c), device_id_type=MESH))
    return copies


def _host(body, grid, n_in, n_out, n_x, make_copies):
    if not n_x:
        return body

    def hosting(*refs):
        a, b = n_in + n_x, n_in + 2 * n_x + n_out
        copies = make_copies(refs[n_in:a], refs[a + n_out:b], refs[-2], refs[-1])
        stages = copies if isinstance(copies, tuple) else (copies,)
        ids = [pl.program_id(d) for d in range(len(grid))]
        at = lambda step: functools.reduce(jnp.logical_and, [i == s for i, s in zip(ids, step)])

        @pl.when(at([0] * len(grid)))
        def _():
            for cp in stages[0]:
                cp.start()

        if len(stages) == 2:
            @pl.when(at([grid[0] // 2] + [0] * (len(grid) - 1)))
            def _():
                for cp in stages[0]:
                    cp.wait_recv()
                for cp in stages[1]:
                    cp.start()

        body(*refs[:n_in], *refs[a:a + n_out], *refs[b:-2])

        @pl.when(at([g - 1 for g in grid]))
        def _():
            if len(stages) == 2:
                for cp in stages[1]:
                    cp.wait_recv()
                for cp in stages[0] + stages[1]:
                    cp.wait_send()
            else:
                for cp in stages[0]:
                    cp.wait()

    return hosting


def _hosted_sems(n_x, per_array=3):
    n = per_array * n_x
    return [pltpu.SemaphoreType.DMA((n,)), pltpu.SemaphoreType.DMA((n,))] if n_x else []


def _attn_fwd(u, agw, shards=()):
    n = len(shards)
    grid = (NG, T // AQ)

    def body(q_ref, k_ref, v_ref, g_ref, ag_ref, o_ref, y_ref, tot_ref, kb_s, vb_s):
        qi = pl.program_id(1)

        @pl.when(qi == 0)
        def _():
            kb_s[...] = k_ref[...].astype(BF16)
            vb_s[...] = v_ref[...].astype(BF16)

        causal, tr, tc, heads = _attn_tiles()
        upper = (tr > tc).astype(BF16)
        qs = _stack_heads(q_ref[...] * 0.125, heads).astype(BF16)

        def block(kb, carry, masked):
            run, acc = carry
            k0 = pl.multiple_of(kb * AQ, AQ)
            kk = kb_s[pl.ds(k0, AQ), :]
            vv = vb_s[pl.ds(k0, AQ), :]
            z = _dot_nt(qs, kk)
            sp = _softplus(z)
            lm = jnp.where(causal, -sp, 0.0) if masked else -sp
            att = jnp.exp((z - sp) + _tri_sum(lm, upper) + run)
            if masked:
                att = jnp.where(causal, att, 0.0)
            acc = acc + _dot(att.astype(BF16), vv)
            return run + jnp.sum(lm, axis=1, keepdims=True), acc

        carry = block(qi, (jnp.zeros((SR, 1), F32), jnp.zeros((SR, GW), F32)), True)
        run, acc = lax.fori_loop(0, qi, lambda i, c: block(qi - 1 - i, c, False), carry)
        o = _unstack_heads(acc, heads)
        osq = o * o
        ms = jnp.zeros((AQ, GW), F32)
        for m in heads:
            ms = jnp.where(m, jnp.sum(jnp.where(m, osq, 0.0), axis=1, keepdims=True), ms)
        g = g_ref[...]
        o_ref[...] = o
        y_ref[...] = (o * lax.rsqrt(ms * (1.0 / DH) + EPS) * ag_ref[...] * (g * _sig(g))).astype(BF16)
        tot_ref[0] = _unstack_heads(jnp.broadcast_to(run, (SR, GW)), heads)

    res = pl.pallas_call(
        _host(body, grid, 5, 3, n, _gather_split_copies), name="attn_fwd", grid=grid,
        in_specs=[pl.BlockSpec((AQ, GW), lambda hg, qi: (qi, hg)),
                  pl.BlockSpec((T, GW), lambda hg, qi: (0, NG + hg)),
                  pl.BlockSpec((T, GW), lambda hg, qi: (0, 2 * NG + hg)),
                  pl.BlockSpec((AQ, GW), lambda hg, qi: (qi, 3 * NG + hg)),
                  pl.BlockSpec((1, GW), lambda hg, qi: (0, 0))] + [HBM_SPEC] * n,
        out_specs=[pl.BlockSpec((AQ, GW), lambda hg, qi: (qi, hg)),
                   pl.BlockSpec((AQ, GW), lambda hg, qi: (qi, hg)),
                   pl.BlockSpec((1, AQ, GW), lambda hg, qi: (hg, qi, 0))] + [HBM_SPEC] * n,
        out_shape=[jax.ShapeDtypeStruct((T, AD), F32), jax.ShapeDtypeStruct((T, AD), BF16),
                   jax.ShapeDtypeStruct((NG, T, GW), F32)]
        + [jax.ShapeDtypeStruct((NCHIP,) + s.shape, s.dtype) for s in shards],
        scratch_shapes=[pltpu.VMEM((T, GW), BF16), pltpu.VMEM((T, GW), BF16)] + _hosted_sems(n, 6),
        compiler_params=_cp(("arbitrary", "arbitrary"), VMEM_BIG),
    )(u, u, u, u, agw, *shards)
    return res[0], res[1], res[2], list(res[3:])


def _glu_conv(u, dw, db):
    tr = 256

    def body(cv_ref, cg_ref, w_ref, b_ref, c1_ref, pad_s):
        pad_s[pl.ds(0, CWP), :] = jnp.zeros((CWP, LANE), F32)
        pad_s[pl.ds(CWP, T), :] = cv_ref[...] * _sig(cg_ref[...])
        wv = w_ref[0]
        bias = b_ref[...]

        def tile(i, carry):
            r0 = pl.multiple_of(i * tr, tr)
            acc = jnp.zeros((tr, LANE), F32) + bias
            for w in range(CW):
                acc = acc + pad_s[pl.ds(r0 + (CWP - CW + 1) + w, tr), :] * wv[w:w + 1, :]
            c1_ref[pl.ds(r0, tr), :] = acc
            return carry

        lax.fori_loop(0, T // tr, tile, 0)

    return pl.pallas_call(
        body, name="glu_conv", grid=(CD // LANE,),
        in_specs=[pl.BlockSpec((T, LANE), lambda cb: (0, 16 + cb)),
                  pl.BlockSpec((T, LANE), lambda cb: (0, 20 + cb)),
                  pl.BlockSpec((1, CWP, LANE), lambda cb: (cb, 0, 0)),
                  pl.BlockSpec((1, LANE), lambda cb: (0, cb))],
        out_specs=pl.BlockSpec((T, LANE), lambda cb: (0, cb)),
        out_shape=jax.ShapeDtypeStruct((T, CD), F32),
        scratch_shapes=[pltpu.VMEM((T + CWP, LANE), F32)],
        compiler_params=_cp(("arbitrary",)),
    )(u, u, dw, db)


def _ln_silu(c1, lg, lb):
    mu = jnp.mean(c1, axis=-1, keepdims=True)
    xc = c1 - mu
    rs = lax.rsqrt(jnp.mean(xc * xc, axis=-1, keepdims=True) + EPS)
    xh = xc * rs
    ln = xh * lg + lb
    s = _sig(ln)
    return xh, rs, ln, s


def _layer_tail(c1, u, ya, h, p, lg, lb, wpw, cg, wout, pg, wgate, wple):
    tm = 256

    def body(c1_ref, gc_ref, ya_ref, h_ref, p_ref, lg_ref, lb_ref, wpw_ref, cg_ref, wout_ref,
             pg_ref, wgate_ref, wple_ref, c3_ref, yc_ref, h1_ref, gate_ref, pe_ref, h2_ref):
        _, _, ln, s = _ln_silu(c1_ref[...], lg_ref[...], lb_ref[...])
        c2 = (ln * s).astype(BF16)
        c3 = _dot(c2, wpw_ref[...])
        gc = gc_ref[...]
        yc = (c3 * _rstd(c3) * cg_ref[...] * (gc * _sig(gc))).astype(BF16)
        c3_ref[...] = c3
        yc_ref[...] = yc
        y = _dot(ya_ref[...], wout_ref[pl.ds(0, AD), :]) + _dot(yc, wout_ref[pl.ds(AD, CD), :])
        h1 = h_ref[...] + y
        hn2 = (h1 * _rstd(h1) * pg_ref[...]).astype(BF16)
        gate = _sig(_dot(hn2, wgate_ref[...]))
        pb = p_ref[...].astype(BF16)
        pe = jnp.concatenate([_dot(pb, wple_ref[k]) for k in range(NCHIP)], axis=1)
        h1_ref[...] = h1
        gate_ref[...] = gate
        pe_ref[...] = pe
        h2_ref[...] = h1 + pe * gate

    row = lambda w: pl.BlockSpec((tm, w), lambda i: (i, 0))
    full = lambda *s: pl.BlockSpec(s, lambda i: (0,) * len(s))
    return pl.pallas_call(
        body, name="layer_tail", grid=(T // tm,),
        in_specs=[row(CD), pl.BlockSpec((tm, CD), lambda i: (i, 6)), row(AD), row(D), row(PLE),
                  full(1, CD), full(1, CD), full(CD, CD), full(1, CD), full(D, D),
                  full(1, D), full(D, D), full(NCHIP, PLE, PLE)],
        out_specs=[row(CD), row(CD), row(D), row(D), row(D), row(D)],
        out_shape=[jax.ShapeDtypeStruct((T, CD), F32), jax.ShapeDtypeStruct((T, CD), BF16)]
        + [jax.ShapeDtypeStruct((T, D), F32)] * 4,
        compiler_params=_cp(("arbitrary",), VMEM_BIG),
    )(c1, u, ya, h, p, lg, lb, wpw, cg, wout, pg, wgate, wple)


def _loss_head(h, tgt, fg):
    tm = 256

    def body(h_ref, t_ref, g_ref, loss_ref, dh_ref, dg_ref):
        @pl.when(pl.program_id(0) == 0)
        def _():
            loss_ref[...] = jnp.zeros_like(loss_ref)
            dg_ref[...] = jnp.zeros_like(dg_ref)
        hh = h_ref[...]
        g = g_ref[...]
        r = _rstd(hh)
        e = hh * r * g - t_ref[...]
        loss_ref[...] += 0.5 * jnp.sum(jnp.mean(e * e, axis=-1, keepdims=True))
        dy = e * (1.0 / D)
        dg_ref[...] += jnp.sum(dy * hh * r, axis=0, keepdims=True)
        dh_ref[...] = _rms_bwd(dy, hh, r, g)

    return pl.pallas_call(
        body, name="loss_head", grid=(T // tm,),
        in_specs=[pl.BlockSpec((tm, D), lambda i: (i, 0)), pl.BlockSpec((tm, D), lambda i: (i, 0)),
                  pl.BlockSpec((1, D), lambda i: (0, 0))],
        out_specs=[pl.BlockSpec((8, LANE), lambda i: (0, 0)), pl.BlockSpec((tm, D), lambda i: (i, 0)),
                   pl.BlockSpec((1, D), lambda i: (0, 0))],
        out_shape=[jax.ShapeDtypeStruct((8, LANE), F32), jax.ShapeDtypeStruct((T, D), F32),
                   jax.ShapeDtypeStruct((1, D), F32)],
        compiler_params=_cp(("arbitrary",)),
    )(h, tgt, fg)


def _ple_out_bwd(dh2, h1, gate, pe, p, ya, yc, pg, wgate, wout):
    tm = 256

    def body(dh2_ref, h1_ref, gate_ref, pe_ref, p_ref, ya_ref, yc_ref, pg_ref, wgate_ref, wout_ref,
             dh1_ref, dy_ref, dwg_ref, dwp_ref, dwo_ref, dpg_ref):
        @pl.when(pl.program_id(0) == 0)
        def _():
            dwg_ref[...] = jnp.zeros_like(dwg_ref)
            dwp_ref[...] = jnp.zeros_like(dwp_ref)
            dwo_ref[...] = jnp.zeros_like(dwo_ref)
            dpg_ref[...] = jnp.zeros_like(dpg_ref)
        dh2 = dh2_ref[...]
        h1 = h1_ref[...]
        gate = gate_ref[...]
        pg = pg_ref[...]
        dpe = (dh2 * gate).astype(BF16)
        dgp = (dh2 * pe_ref[...] * gate * (1.0 - gate)).astype(BF16)
        r = _rstd(h1)
        hn = h1 * r
        dwg_ref[...] += _dot_tn((hn * pg).astype(BF16), dgp)
        dhn2 = _dot_nt(dgp, wgate_ref[...])
        dpg_ref[...] += jnp.sum(dhn2 * hn, axis=0, keepdims=True)
        dh1 = dh2 + _rms_bwd(dhn2, h1, r, pg)
        pb = p_ref[...].astype(BF16)
        for k in range(NCHIP):
            dwp_ref[k] += _dot_tn(pb, dpe[:, k * PLE:(k + 1) * PLE])
        dh1b = dh1.astype(BF16)
        dy_ref[...] = _dot_nt(dh1b, wout_ref[...])
        dwo_ref[pl.ds(0, AD), :] += _dot_tn(ya_ref[...], dh1b)
        dwo_ref[pl.ds(AD, CD), :] += _dot_tn(yc_ref[...], dh1b)
        dh1_ref[...] = dh1

    row = lambda w: pl.BlockSpec((tm, w), lambda i: (i, 0))
    full = lambda *s: pl.BlockSpec(s, lambda i: (0,) * len(s))
    return pl.pallas_call(
        body, name="ple_out_bwd", grid=(T // tm,),
        in_specs=[row(D), row(D), row(D), row(D), row(PLE), row(AD), row(CD),
                  full(1, D), full(D, D), full(D, D)],
        out_specs=[row(D), row(D), full(D, D), full(NCHIP, PLE, PLE), full(D, D), full(1, D)],
        out_shape=[jax.ShapeDtypeStruct((T, D), F32), jax.ShapeDtypeStruct((T, D), F32),
                   jax.ShapeDtypeStruct((D, D), F32), jax.ShapeDtypeStruct((NCHIP, PLE, PLE), F32),
                   jax.ShapeDtypeStruct((D, D), F32), jax.ShapeDtypeStruct((1, D), F32)],
        compiler_params=_cp(("arbitrary",), VMEM_BIG),
    )(dh2, h1, gate, pe, p, ya, yc, pg, wgate, wout)


def _branch_bwd(dy, o, u, c1, c3, ag, lg, lb, wpw, cg, seg):
    tm = 256

    def body(dya_ref, dyc_ref, o_ref, ga_ref, gc_ref, c1_ref, c3_ref, ag_ref, lg_ref, lb_ref, wpw_ref,
             cg_ref, seg_ref, do_ref, dga_ref, dgc_ref, dc1_ref, dwpw_ref, dag_ref, dcg_ref, dlg_ref, dlb_ref):
        @pl.when(pl.program_id(0) == 0)
        def _():
            for r_ in (dwpw_ref, dag_ref, dcg_ref, dlg_ref, dlb_ref):
                r_[...] = jnp.zeros_like(r_)
        dya = dya_ref[...]
        o = o_ref[...]
        ga = ga_ref[...]
        ag_v = ag_ref[...]
        seg_m = seg_ref[...]
        r = lax.rsqrt(_dot2(o * o, seg_m) * (1.0 / DH) + EPS)
        onr = o * r
        sg = _sig(ga)
        dga_ref[...] = dya * (onr * ag_v) * (sg * (1.0 + ga * (1.0 - sg)))
        don = dya * (ga * sg)
        dag_ref[...] += jnp.sum(don * onr, axis=0, keepdims=True)
        dn = don * ag_v
        do_ref[...] = r * dn - o * (r * r * r) * (_dot2(dn * o, seg_m) * (1.0 / DH))
        dyc = dyc_ref[...]
        c3 = c3_ref[...]
        gc = gc_ref[...]
        cg_v = cg_ref[...]
        r3 = _rstd(c3)
        cn = c3 * r3
        sc = _sig(gc)
        dgc_ref[...] = dyc * (cn * cg_v) * (sc * (1.0 + gc * (1.0 - sc)))
        dcn = dyc * (gc * sc)
        dcg_ref[...] += jnp.sum(dcn * cn, axis=0, keepdims=True)
        dc3 = _rms_bwd(dcn, c3, r3, cg_v).astype(BF16)
        lg_v = lg_ref[...]
        xh, rs, ln, s = _ln_silu(c1_ref[...], lg_v, lb_ref[...])
        c2 = (ln * s).astype(BF16)
        dwpw_ref[...] += _dot_tn(c2, dc3)
        dc2 = _dot_nt(dc3, wpw_ref[...])
        dln = dc2 * (s * (1.0 + ln * (1.0 - s)))
        dlb_ref[...] += jnp.sum(dln, axis=0, keepdims=True)
        dlg_ref[...] += jnp.sum(dln * xh, axis=0, keepdims=True)
        dxh = dln * lg_v
        dc1_ref[...] = rs * (dxh - jnp.mean(dxh, axis=-1, keepdims=True)
                             - xh * jnp.mean(dxh * xh, axis=-1, keepdims=True))

    half = lambda j: pl.BlockSpec((tm, 512), lambda i: (i, j))
    full = lambda *s: pl.BlockSpec(s, lambda i: (0,) * len(s))
    vec = jax.ShapeDtypeStruct((1, 512), F32)
    act = jax.ShapeDtypeStruct((T, 512), F32)
    return pl.pallas_call(
        body, name="branch_bwd", grid=(T // tm,),
        in_specs=[half(0), half(1), half(0), half(3), half(6), half(0), half(0),
                  full(1, AD), full(1, CD), full(1, CD), full(CD, CD), full(1, CD), full(AD, AD)],
        out_specs=[half(0), half(0), half(0), half(0), full(CD, CD), full(1, 512), full(1, 512),
                   full(1, 512), full(1, 512)],
        out_shape=[act, act, act, act, jax.ShapeDtypeStruct((CD, CD), F32), vec, vec, vec, vec],
        compiler_params=_cp(("arbitrary",), VMEM_BIG),
    )(dy, dy, o, u, u, c1, c3, ag, lg, lb, wpw, cg, seg)


def _conv_bwd(dc1, u, dw):
    tr = 256
    off = CWP - CW + 1

    def body(d_ref, cv_ref, cg_ref, w_ref, dcv_ref, dcg_ref, ddw_ref, ddb_ref, padc_s, padd_s, acc_s):
        cv = cv_ref[...]
        sg = _sig(cg_ref[...])
        padc_s[pl.ds(0, CWP), :] = jnp.zeros((CWP, LANE), F32)
        padc_s[pl.ds(CWP, T), :] = cv * sg
        padd_s[pl.ds(0, T), :] = d_ref[...]
        padd_s[pl.ds(T, CWP), :] = jnp.zeros((CWP, LANE), F32)
        acc_s[...] = jnp.zeros_like(acc_s)
        wv = w_ref[0]

        def tile(i, carry):
            r0 = pl.multiple_of(i * tr, tr)
            dt = padd_s[pl.ds(r0, tr), :]
            dc0 = jnp.zeros((tr, LANE), F32)
            for w in range(CW):
                dc0 = dc0 + padd_s[pl.ds(r0 + (CW - 1) - w, tr), :] * wv[w:w + 1, :]
                prod = dt * padc_s[pl.ds(r0 + off + w, tr), :]
                acc_s[w] += jnp.sum(prod.reshape(tr // 8, 8, LANE), axis=0)
            cvt = cv_ref[pl.ds(r0, tr), :]
            sgt = _sig(cg_ref[pl.ds(r0, tr), :])
            dcv_ref[pl.ds(r0, tr), :] = dc0 * sgt
            dcg_ref[pl.ds(r0, tr), :] = dc0 * cvt * sgt * (1.0 - sgt)
            return carry

        lax.fori_loop(0, T // tr, tile, 0)
        ddw_ref[0] = jnp.sum(acc_s[...], axis=1)
        ddb_ref[...] = jnp.sum(d_ref[...], axis=0, keepdims=True)

    col = lambda j: pl.BlockSpec((T, LANE), lambda cb: (0, j + cb))
    return pl.pallas_call(
        body, name="conv_bwd", grid=(CD // LANE,),
        in_specs=[col(0), col(16), col(20), pl.BlockSpec((1, CWP, LANE), lambda cb: (cb, 0, 0))],
        out_specs=[col(0), col(0), pl.BlockSpec((1, CWP, LANE), lambda cb: (cb, 0, 0)),
                   pl.BlockSpec((1, LANE), lambda cb: (0, cb))],
        out_shape=[jax.ShapeDtypeStruct((T, CD), F32), jax.ShapeDtypeStruct((T, CD), F32),
                   jax.ShapeDtypeStruct((NCHIP, CWP, LANE), F32), jax.ShapeDtypeStruct((1, CD), F32)],
        scratch_shapes=[pltpu.VMEM((T + CWP, LANE), F32), pltpu.VMEM((T + CWP, LANE), F32),
                        pltpu.VMEM((CWP, 8, LANE), F32)],
        compiler_params=_cp(("arbitrary",)),
    )(dc1, u, u, dw)


def _attn_bwd(u, do, tot, partials=()):
    n_x = len(partials)
    grid = (NG, T // AQ)

    def body(q_ref, k_ref, v_ref, do_ref, tot_ref, dq_ref, dk_ref, dv_ref, kb_s, vb_s):
        qi = pl.program_id(1)

        @pl.when(qi == 0)
        def _():
            kb_s[...] = k_ref[...].astype(BF16)
            vb_s[...] = v_ref[...].astype(BF16)
            dk_ref[...] = jnp.zeros_like(dk_ref)
            dv_ref[...] = jnp.zeros_like(dv_ref)

        causal, tr, tc, heads = _attn_tiles()
        upper = (tr > tc).astype(BF16)
        lower = (tr < tc).astype(BF16)
        q = q_ref[...]
        qs = _stack_heads(q * 0.125, heads).astype(BF16)
        qus = _stack_heads(q, heads).astype(BF16)
        dos = _stack_heads(do_ref[...], heads).astype(BF16)
        totv = tot_ref[0]
        tots = jnp.concatenate([totv[:, h * DH:h * DH + 1] for h in range(HG)], axis=0)

        def block(kb, carry, masked):
            lm_left, dl_left, dq = carry
            k0 = pl.multiple_of(kb * AQ, AQ)
            kk = kb_s[pl.ds(k0, AQ), :]
            vv = vb_s[pl.ds(k0, AQ), :]
            z = _dot_nt(qs, kk)
            sp = _softplus(z)
            lm = jnp.where(causal, -sp, 0.0) if masked else -sp
            lm_incl = lm_left + jnp.sum(lm, axis=1, keepdims=True)
            att = jnp.exp((z - sp) + _tri_sum(lm, upper) + (tots - lm_incl))
            if masked:
                att = jnp.where(causal, att, 0.0)
            dl = att * _dot_nt(dos, vv)
            dv_ref[pl.ds(k0, AQ), :] += _dot_tn(att.astype(BF16), dos)
            prefix = dl_left + _tri_sum(dl, lower)
            beta = jnp.exp(z - sp)
            dz = (1.0 - beta) * dl - beta * prefix
            if masked:
                dz = jnp.where(causal, dz, 0.0)
            dzs = (dz * 0.125).astype(BF16)
            dk_ref[pl.ds(k0, AQ), :] += _dot_tn(dzs, qus)
            return lm_incl, dl_left + jnp.sum(dl, axis=1, keepdims=True), dq + _dot(dzs, kk)

        zero = jnp.zeros((SR, 1), F32)
        carry = lax.fori_loop(0, qi, lambda kb, c: block(kb, c, False),
                              (zero, zero, jnp.zeros((SR, GW), F32)))
        _, _, dq = block(qi, carry, True)
        dq_ref[...] = _unstack_heads(dq, heads)

    res = pl.pallas_call(
        _host(body, grid, 5, 3, n_x, _scatter_copies), name="attn_bwd", grid=grid,
        in_specs=[pl.BlockSpec((AQ, GW), lambda hg, qi: (qi, hg)),
                  pl.BlockSpec((T, GW), lambda hg, qi: (0, NG + hg)),
                  pl.BlockSpec((T, GW), lambda hg, qi: (0, 2 * NG + hg)),
                  pl.BlockSpec((AQ, GW), lambda hg, qi: (qi, hg)),
                  pl.BlockSpec((1, AQ, GW), lambda hg, qi: (hg, qi, 0))] + [HBM_SPEC] * n_x,
        out_specs=[pl.BlockSpec((AQ, GW), lambda hg, qi: (qi, hg)),
                   pl.BlockSpec((T, GW), lambda hg, qi: (0, hg)),
                   pl.BlockSpec((T, GW), lambda hg, qi: (0, hg))] + [HBM_SPEC] * n_x,
        out_shape=[jax.ShapeDtypeStruct((T, AD), F32)] * 3
        + [jax.ShapeDtypeStruct((NCHIP - 1,) + a.shape[1:], a.dtype) for a in partials],
        scratch_shapes=[pltpu.VMEM((T, GW), BF16), pltpu.VMEM((T, GW), BF16)] + _hosted_sems(n_x),
        compiler_params=_cp(("arbitrary", "arbitrary"), VMEM_BIG),
    )(u, u, u, do, tot, *partials)
    return res[0], res[1], res[2], list(res[3:])


def _inproj_dw(hn, du):
    tm = 512

    def body(hn_ref, du_ref, dw_ref):
        @pl.when(pl.program_id(1) == 0)
        def _():
            dw_ref[...] = jnp.zeros_like(dw_ref)
        dw_ref[0] += _dot_tn(hn_ref[...], du_ref[...])

    return pl.pallas_call(
        body, name="inproj_dw", grid=(NCHIP, T // tm),
        in_specs=[pl.BlockSpec((tm, D), lambda k, i: (i, 0)), pl.BlockSpec((tm, SHW), lambda k, i: (i, k))],
        out_specs=pl.BlockSpec((1, D, SHW), lambda k, i: (k, 0, 0)),
        out_shape=jax.ShapeDtypeStruct((NCHIP, D, SHW), F32),
        compiler_params=_cp(("arbitrary", "arbitrary")),
    )(hn, du)


def _inproj_dx(du, w, h, g, dres, partials=(), grads=()):
    tm = 512
    sent = list(partials) + list(grads)
    n_x = len(sent)
    grid = (T // tm, NCHIP)
    if grads:
        landing = [jax.ShapeDtypeStruct((NCHIP, a.shape[1] // 2, a.shape[2]), F32) for a in grads]
    else:
        landing = [jax.ShapeDtypeStruct((NCHIP - 1,) + a.shape[1:], a.dtype) for a in partials]

    def body(du_ref, w_ref, h_ref, g_ref, dres_ref, dh_ref, dg_ref, acc_s):
        i, k = pl.program_id(0), pl.program_id(1)

        @pl.when(jnp.logical_and(i == 0, k == 0))
        def _():
            dg_ref[...] = jnp.zeros_like(dg_ref)

        @pl.when(k == 0)
        def _():
            acc_s[...] = _dot_nt(du_ref[...], w_ref[0])

        @pl.when(k > 0)
        def _():
            acc_s[...] += _dot_nt(du_ref[...], w_ref[0])

        @pl.when(k == NCHIP - 1)
        def _():
            hh = h_ref[...]
            r = _rstd(hh)
            dhn = acc_s[...]
            dg_ref[...] += jnp.sum(dhn * hh * r, axis=0, keepdims=True)
            dh_ref[...] = dres_ref[...] + _rms_bwd(dhn, hh, r, g_ref[...])

    res = pl.pallas_call(
        _host(body, grid, 5, 2, n_x, _pair_copies if grads else _scatter_copies), name="inproj_dx", grid=grid,
        in_specs=[pl.BlockSpec((tm, SHW), lambda i, k: (i, k)),
                  pl.BlockSpec((1, D, SHW), lambda i, k: (k, 0, 0)),
                  pl.BlockSpec((tm, D), lambda i, k: (i, 0)),
                  pl.BlockSpec((1, D), lambda i, k: (0, 0)),
                  pl.BlockSpec((tm, D), lambda i, k: (i, 0))] + [HBM_SPEC] * n_x,
        out_specs=[pl.BlockSpec((tm, D), lambda i, k: (i, 0)), pl.BlockSpec((1, D), lambda i, k: (0, 0))]
        + [HBM_SPEC] * n_x,
        out_shape=[jax.ShapeDtypeStruct((T, D), F32), jax.ShapeDtypeStruct((1, D), F32)] + landing,
        scratch_shapes=[pltpu.VMEM((tm, D), F32)] + _hosted_sems(n_x),
        compiler_params=_cp(("arbitrary", "arbitrary")),
    )(du, w, h, g, dres, *sent)
    return res[0], res[1], list(res[2:])


def _sum_pair(core, grads, gots):
    n = len(grads)

    def body(c_ref, *refs):
        for a in range(n):
            refs[2 * n + a][...] = (refs[a][...] + refs[n + a][...]).astype(BF16)

    mine = [pl.BlockSpec((1,) + s.shape[1:], lambda k, c: (k, c[0], 0)) for s in gots]
    same = [pl.BlockSpec((1,) + s.shape[1:], lambda k, c: (k, 0, 0)) for s in gots]
    return pl.pallas_call(
        body, name="sum_pair",
        grid_spec=pltpu.PrefetchScalarGridSpec(
            num_scalar_prefetch=1, grid=(NCHIP,), in_specs=mine + same, out_specs=same),
        out_shape=[jax.ShapeDtypeStruct(s.shape, BF16) for s in gots],
        compiler_params=_cp(("arbitrary",), VMEM_BIG),
    )(core, *grads, *gots)


def _sum_chips(chip, partials, gots):
    n = len(partials)

    def body(c_ref, *refs):
        for a in range(n):
            acc = refs[a][0].astype(F32)
            for j in range(NCHIP - 1):
                acc = acc + refs[n + a][j].astype(F32)
            refs[2 * n + a][...] = acc

    return pl.pallas_call(
        body, name="sum_chips",
        grid_spec=pltpu.PrefetchScalarGridSpec(
            num_scalar_prefetch=1, grid=(1,),
            in_specs=[pl.BlockSpec((1,) + s.shape[1:], lambda i, c: (c[0], 0, 0)) for s in partials]
            + [pl.BlockSpec(s.shape, lambda i, c: (0, 0, 0)) for s in gots],
            out_specs=[pl.BlockSpec(s.shape[1:], lambda i, c: (0, 0)) for s in partials]),
        out_shape=[jax.ShapeDtypeStruct(s.shape[1:], F32) for s in partials],
        compiler_params=_cp(("arbitrary",), VMEM_BIG),
    )(chip, *partials, *gots)


def _adamw(w, g, m, v, rows):
    R, C = w.shape
    c1 = 1.0 - ADAM_B1 ** ADAM_STEP
    c2 = 1.0 - ADAM_B2 ** ADAM_STEP

    def body(w_ref, g_ref, m_ref, v_ref, d_ref, nm_ref, nv_ref):
        gg = g_ref[...]
        nm = ADAM_B1 * m_ref[...] + (1.0 - ADAM_B1) * gg
        nv = ADAM_B2 * v_ref[...] + (1.0 - ADAM_B2) * (gg * gg)
        d_ref[...] = -ADAM_LR * ((nm / c1) / (jnp.sqrt(nv / c2) + ADAM_EPS) + ADAM_WD * w_ref[...])
        nm_ref[...] = nm
        nv_ref[...] = nv

    spec = pl.BlockSpec((rows, C), lambda i: (i, 0))
    sh = jax.ShapeDtypeStruct((R, C), F32)
    return pl.pallas_call(
        body, name="adamw", grid=(R // rows,), in_specs=[spec] * 4, out_specs=[spec] * 3,
        out_shape=[sh, sh, sh], compiler_params=_cp(("arbitrary",)),
    )(w, g, m, v)


HBM_SPEC = pl.BlockSpec(memory_space=pltpu.HBM)


def _place():
    return lax.axis_index("x"), lax.axis_index("y"), lax.axis_index("c")


def _all_gather_split(shard):
    def body(in_ref, out_ref, send_sems, recv_sems):
        fetch, passed = _gather_split_copies([in_ref], [out_ref], send_sems, recv_sems)
        for cp in fetch:
            cp.start()
        for got, onward in zip(fetch, passed):
            got.wait_recv()
            onward.start()
        for cp in passed:
            cp.wait_recv()
        for cp in fetch + passed:
            cp.wait_send()

    return pl.pallas_call(
        body, name="all_gather_split", in_specs=[HBM_SPEC], out_specs=HBM_SPEC,
        out_shape=jax.ShapeDtypeStruct((NCHIP,) + shard.shape, shard.dtype),
        scratch_shapes=[pltpu.SemaphoreType.DMA((6,)), pltpu.SemaphoreType.DMA((6,))],
    )(shard)


def _pair_exchange(grads):
    n = len(grads)

    def body(*refs):
        copies = _pair_copies(refs[:n], refs[n:2 * n], refs[2 * n], refs[2 * n + 1])
        for cp in copies:
            cp.start()
        for cp in copies:
            cp.wait()

    return pl.pallas_call(
        body, name="pair_exchange", in_specs=[HBM_SPEC] * n, out_specs=[HBM_SPEC] * n,
        out_shape=[jax.ShapeDtypeStruct((NCHIP, g.shape[1] // 2, g.shape[2]), F32) for g in grads],
        scratch_shapes=[pltpu.SemaphoreType.DMA((n,)), pltpu.SemaphoreType.DMA((n,))],
    )(*grads)


def _pair_share(mine):
    n = len(mine)

    def body(*refs):
        x, y, c = _place()
        copies = [pltpu.make_async_remote_copy(
            src_ref=refs[a], dst_ref=refs[n + a], send_sem=refs[2 * n].at[a], recv_sem=refs[2 * n + 1].at[a],
            device_id=(x, y, 1 - c), device_id_type=MESH) for a in range(n)]
        for cp in copies:
            cp.start()
        for cp in copies:
            cp.wait()

    return pl.pallas_call(
        body, name="pair_share", in_specs=[HBM_SPEC] * n, out_specs=[HBM_SPEC] * n,
        out_shape=[jax.ShapeDtypeStruct(t.shape, F32) for t in mine],
        scratch_shapes=[pltpu.SemaphoreType.DMA((n,)), pltpu.SemaphoreType.DMA((n,))],
    )(*mine)


def _small_allreduce(mine):
    def body(m_ref, o_ref, slots, send_sems, recv_sems):
        x, y, c = _place()
        me = 4 * x + 2 * y + c
        slots[me] = m_ref[...]
        copies = []
        for r in range(1, 8):
            rx, ry, rc = (r >> 2) & 1, (r >> 1) & 1, r & 1
            peer = (x + rx - 2 * x * rx, y + ry - 2 * y * ry, c + rc - 2 * c * rc)
            cp = pltpu.make_async_remote_copy(
                src_ref=m_ref, dst_ref=slots.at[me], send_sem=send_sems.at[r - 1], recv_sem=recv_sems.at[r - 1],
                device_id=peer, device_id_type=MESH)
            cp.start()
            copies.append(cp)
        for cp in copies:
            cp.wait()
        acc = slots[0]
        for j in range(1, 8):
            acc = acc + slots[j]
        o_ref[...] = acc

    return pl.pallas_call(
        body, name="small_allreduce",
        in_specs=[pl.BlockSpec(memory_space=pltpu.VMEM)], out_specs=pl.BlockSpec(memory_space=pltpu.VMEM),
        out_shape=jax.ShapeDtypeStruct((SMALL_ROWS, LANE), F32),
        scratch_shapes=[pltpu.VMEM((8, SMALL_ROWS, LANE), F32), pltpu.SemaphoreType.DMA((7,)),
                        pltpu.SemaphoreType.DMA((7,))],
    )(mine)


def _seg_matrix():
    i = lax.broadcasted_iota(jnp.int32, (AD, AD), 0) // DH
    j = lax.broadcasted_iota(jnp.int32, (AD, AD), 1) // DH
    return (i == j).astype(BF16)


TAIL = ("w_out", "w_ple_gate", "w_ple", "w_pw", "dw_w")


def _local_step(x, p, tgt, sm, shards, chip, ci):
    seg = _seg_matrix()
    core = jnp.reshape(ci, (1,)).astype(jnp.int32)
    chip_idx = jnp.reshape(chip, (1,)).astype(jnp.int32)
    own = lambda g, s: lax.dynamic_update_index_in_dim(g, s, chip, 0)
    w_in_next = own(_all_gather_split(shards[0]["w_in"]), shards[0]["w_in"])
    h = x
    saved = []
    for l in range(DEPTH):
        w_in = w_in_next
        row = lambda name: sm[name][l:l + 1]
        u, hn = _rms_inproj(h, row("norm_g"), w_in)
        todo = [shards[l][k] for k in TAIL] + ([shards[l + 1]["w_in"]] if l + 1 < DEPTH else [])
        o, ya, tot, got = _attn_fwd(u, jnp.tile(row("attn_out_g"), (1, HG)), todo)
        got = [own(g, s) for g, s in zip(got, todo)]
        w_out = got[0].reshape(D, D)
        w_gate = got[1].reshape(D, D)
        w_ple = got[2]
        w_pw = got[3].reshape(CD, CD)
        dw = got[4]
        if l + 1 < DEPTH:
            w_in_next = got[5]
        c1 = _glu_conv(u, dw, row("dw_b"))
        c3, yc, h1, gate, pe, h2 = _layer_tail(
            c1, u, ya, h, p[l], row("conv_ln_g"), row("conv_ln_b"), w_pw, row("conv_out_g"), w_out,
            row("ple_norm_g"), w_gate, w_ple)
        saved.append(dict(h=h, u=u, hn=hn, o=o, ya=ya, tot=tot, c1=c1, c3=c3, yc=yc, h1=h1, gate=gate, pe=pe,
                          w_in=w_in, w_out=w_out, w_gate=w_gate, w_pw=w_pw, dw=dw))
        h = h2
    loss_blk, dh, dfg = _loss_head(h, tgt, sm["final_g"])
    small = [None] * DEPTH
    pending, partials, arrived = [], {}, {}
    pair_sum = lambda grads: _sum_pair(core, grads, _pair_exchange(grads))
    for l in reversed(range(DEPTH)):
        s = saved[l]
        row = lambda name: sm[name][l:l + 1]
        dh1, dy, dwg, dwp, dwo, dpg = _ple_out_bwd(
            dh, s["h1"], s["gate"], s["pe"], p[l], s["ya"], s["yc"], row("ple_norm_g"), s["w_gate"], s["w_out"])
        ag_t = jnp.tile(row("attn_out_g"), (1, AD // DH))
        do, dga, dgc, dc1, dwpw, dag, dcg, dlg, dlb = _branch_bwd(
            dy, s["o"], s["u"], s["c1"], s["c3"], ag_t, row("conv_ln_g"), row("conv_ln_b"), s["w_pw"],
            row("conv_out_g"), seg)
        dcv, dcgate, ddw, ddb = _conv_bwd(dc1, s["u"], s["dw"])
        tail = [dwo.reshape(NCHIP, 256, D), dwg.reshape(NCHIP, 256, D), dwp, dwpw.reshape(NCHIP, 128, CD), ddw]
        if l == 0:
            partials[(l, "tail")] = pair_sum(tail)
            pending.append((l, "tail"))
        send = [t for key in pending for t in partials[key]]
        dq, dk, dv, got = _attn_bwd(s["u"], do, s["tot"], send)
        for key in pending:
            arrived[key], got = got[:len(partials[key])], got[len(partials[key]):]
        pending = []
        du = jnp.concatenate([dq, dk, dv, dga, dcv, dcgate, dgc], axis=1).astype(BF16)
        dwin = _inproj_dw(s["hn"], du)
        if l == 0:
            partials[(l, "w_in")] = pair_sum([dwin])
            dh, dng, arrived[(l, "w_in")] = _inproj_dx(du, s["w_in"], s["h"], row("norm_g"), dh1,
                                                      partials[(l, "w_in")])
        else:
            dh, dng, halves = _inproj_dx(du, s["w_in"], s["h"], row("norm_g"), dh1, grads=tail + [dwin])
            tail_p = _sum_pair(core, tail + [dwin], halves)
            partials[(l, "tail")], partials[(l, "w_in")] = tail_p[:-1], tail_p[-1:]
            pending = [(l, "tail"), (l, "w_in")]
        small[l] = dict(norm_g=dng, attn_out_g=dag.reshape(AD // DH, DH).sum(axis=0, keepdims=True), dw_b=ddb,
                        conv_ln_g=dlg, conv_ln_b=dlb, conv_out_g=dcg, ple_norm_g=dpg)
    big = []
    for l in range(DEPTH):
        ps = partials[(l, "w_in")] + partials[(l, "tail")]
        mine = _sum_chips(chip_idx, ps, arrived[(l, "w_in")] + arrived[(l, "tail")])
        theirs = _pair_share(mine)
        big.append(dict(zip(BIG, [jnp.where(core[0] == 0, jnp.concatenate([m, t]), jnp.concatenate([t, m]))
                                  for m, t in zip(mine, theirs)])))
    return loss_blk[0, 0], dh, big, small, dfg


BIG = ("w_in", "w_out", "w_ple_gate", "w_ple", "w_pw", "dw_w")
SMALL2 = ("norm_g", "ple_norm_g", "dw_b", "conv_ln_g", "conv_ln_b", "conv_out_g", "attn_out_g")


def _pack_small(two, final):
    flat = jnp.concatenate([two[k].reshape(-1) for k in SMALL2] + [final.reshape(-1)])
    flat = jnp.concatenate([flat, jnp.zeros((SMALL_ROWS * LANE - flat.shape[0],), F32)])
    return flat.reshape(SMALL_ROWS, LANE)


def _unpack_small(packed, like_two, like_final):
    flat = packed.reshape(-1)
    out, off = {}, 0
    for k in SMALL2:
        n = like_two[k].size
        out[k] = flat[off:off + n].reshape(like_two[k].shape)
        off += n
    return out, flat[off:off + like_final.size].reshape(like_final.shape)


def kernel(x, p, norm_g, w_in, attn_out_g, dw_w, dw_b, conv_ln_g, conv_ln_b, w_pw, conv_out_g, w_out, ple_norm_g, w_ple_gate, w_ple, final_g, loss_target, m_norm_g, m_w_in, m_attn_out_g, m_dw_w, m_dw_b, m_conv_ln_g, m_conv_ln_b, m_w_pw, m_conv_out_g, m_w_out, m_ple_norm_g, m_w_ple_gate, m_w_ple, m_final_g, v_norm_g, v_w_in, v_attn_out_g, v_dw_w, v_dw_b, v_conv_ln_g, v_conv_ln_b, v_w_pw, v_conv_out_g, v_w_out, v_ple_norm_g, v_w_ple_gate, v_w_ple, v_final_g):
    W = dict(norm_g=norm_g, w_in=w_in, attn_out_g=attn_out_g, dw_w=dw_w, dw_b=dw_b, conv_ln_g=conv_ln_g,
             conv_ln_b=conv_ln_b, w_pw=w_pw, conv_out_g=conv_out_g, w_out=w_out, ple_norm_g=ple_norm_g,
             w_ple_gate=w_ple_gate, w_ple=w_ple, final_g=final_g)
    M = dict(norm_g=m_norm_g, w_in=m_w_in, attn_out_g=m_attn_out_g, dw_w=m_dw_w, dw_b=m_dw_b,
             conv_ln_g=m_conv_ln_g, conv_ln_b=m_conv_ln_b, w_pw=m_w_pw, conv_out_g=m_conv_out_g, w_out=m_w_out,
             ple_norm_g=m_ple_norm_g, w_ple_gate=m_w_ple_gate, w_ple=m_w_ple, final_g=m_final_g)
    V = dict(norm_g=v_norm_g, w_in=v_w_in, attn_out_g=v_attn_out_g, dw_w=v_dw_w, dw_b=v_dw_b,
             conv_ln_g=v_conv_ln_g, conv_ln_b=v_conv_ln_b, w_pw=v_w_pw, conv_out_g=v_conv_out_g, w_out=v_w_out,
             ple_norm_g=v_ple_norm_g, w_ple_gate=v_w_ple_gate, w_ple=v_w_ple, final_g=v_final_g)
    order = ("norm_g", "w_in", "attn_out_g", "dw_w", "dw_b", "conv_ln_g", "conv_ln_b", "w_pw", "conv_out_g",
             "w_out", "ple_norm_g", "w_ple_gate", "w_ple", "final_g")

    pad_taps = lambda a: jnp.pad(a, ((0, 0), (0, CWP - CW), (0, 0)))
    cast = dict(w_in=w_in.astype(BF16), w_out=w_out.astype(BF16), w_ple_gate=w_ple_gate.astype(BF16),
                w_ple=w_ple.astype(BF16), w_pw=w_pw.astype(BF16), dw_w=pad_taps(dw_w))
    shards = [{k: v[l] for k, v in cast.items()} for l in range(DEPTH)]
    xi, yi, ci = lax.axis_index("x"), lax.axis_index("y"), lax.axis_index("c")
    chip = 2 * xi + yi

    sm = {k: W[k] for k in SMALL2}
    sm["final_g"] = final_g.reshape(1, D)
    loss_part, grad_x, big, small, dfg = _local_step(x[0], p[:, 0], loss_target[0], sm, shards, chip, ci)
    g_big = {name: jnp.stack([big[l][name] for l in range(DEPTH)]).reshape(cast[name].shape) for name in BIG}

    small_two = {k: jnp.concatenate([small[l][k] for l in range(DEPTH)], axis=0) for k in SMALL2}
    g_small_packed = _small_allreduce(_pack_small(small_two, dfg).at[SMALL_ROWS - 1, LANE - 1].set(loss_part))
    loss = g_small_packed[SMALL_ROWS - 1, LANE - 1]
    g_small, g_final = _unpack_small(g_small_packed, {k: W[k] for k in SMALL2}, final_g)

    grads, deltas, new_m, new_v = {}, {}, {}, {}
    for name in BIG:
        wv = pad_taps(W[name]) if name == "dw_w" else W[name]
        mv = pad_taps(M[name]) if name == "dw_w" else M[name]
        vv = pad_taps(V[name]) if name == "dw_w" else V[name]
        gg = g_big[name]
        cols = wv.shape[-1]
        rows_total = wv.size // cols
        tile_rows = min(rows_total, 256)
        d2, m2, v2 = _adamw(wv.reshape(rows_total, cols), gg.reshape(rows_total, cols),
                            mv.reshape(rows_total, cols), vv.reshape(rows_total, cols), tile_rows)
        if name == "dw_w":
            cut = lambda a: a.reshape(DEPTH, CWP, LANE)[:, :CW]
            grads[name], deltas[name], new_m[name], new_v[name] = cut(gg), cut(d2), cut(m2), cut(v2)
        else:
            grads[name] = gg
            deltas[name], new_m[name], new_v[name] = (t.reshape(wv.shape) for t in (d2, m2, v2))
    ws = _pack_small({k: W[k] for k in SMALL2}, final_g)
    ms = _pack_small({k: M[k] for k in SMALL2}, m_final_g)
    vs = _pack_small({k: V[k] for k in SMALL2}, v_final_g)
    ds, nms, nvs = _adamw(ws, g_small_packed, ms, vs, SMALL_ROWS)
    for packed, dst in ((ds, deltas), (nms, new_m), (nvs, new_v)):
        two, fin = _unpack_small(packed, {k: W[k] for k in SMALL2}, final_g)
        dst.update(two)
        dst["final_g"] = fin
    grads.update(g_small)
    grads["final_g"] = g_final

    return (loss, grad_x[None], *[grads[n] for n in order], *[deltas[n] for n in order],
            *[new_m[n] for n in order], *[new_v[n] for n in order])
```

```python
import functools

import jax
import jax.numpy as jnp
from jax import lax
from jax.experimental import pallas as pl
from jax.experimental.pallas import tpu as pltpu

F32 = jnp.float32
BF16 = jnp.bfloat16

T = 2048
D = 1024
DIN = 3584
NCHIP = 4
SHW = DIN // NCHIP
AD = 512
CD = 512
DH = 64
CW = 31
CWP = 32
PLE = 256
DEPTH = 2
EPS = 1e-6
AQ = 256
HG = 4
GW = HG * DH
SR = HG * AQ
NG = AD // GW
LANE = 128
TM = 1024

ADAM_LR = 0.001
ADAM_B1 = 0.9
ADAM_B2 = 0.999
ADAM_EPS = 1e-08
ADAM_WD = 0.01
ADAM_STEP = 10

SMALL_ROWS = 80

VMEM_BIG = 56 * 1024 * 1024
MESH = pl.DeviceIdType.MESH


def _cp(sem=None, vmem=None):
    kw = {}
    if sem is not None:
        kw["dimension_semantics"] = sem
    if vmem is not None:
        kw["vmem_limit_bytes"] = vmem
    return pltpu.CompilerParams(**kw)


def _dot(a, b):
    return jnp.dot(a, b, preferred_element_type=F32)


def _dot_nt(a, b):
    return lax.dot_general(a, b, (((1,), (1,)), ((), ())), preferred_element_type=F32)


def _dot_tn(a, b):
    return lax.dot_general(a, b, (((0,), (0,)), ((), ())), preferred_element_type=F32)


def _dot2(x, m):
    hi = x.astype(BF16)
    lo = (x - hi.astype(F32)).astype(BF16)
    return _dot(hi, m) + _dot(lo, m)


def _sig(x):
    return 1.0 / (1.0 + jnp.exp(-x))


def _softplus(z):
    return jnp.maximum(z, 0.0) + jnp.log(1.0 + jnp.exp(-jnp.abs(z)))


def _rstd(x):
    return lax.rsqrt(jnp.mean(x * x, axis=-1, keepdims=True) + EPS)


def _rms_bwd(dy, x, r, g):
    dn = dy * g
    return r * dn - x * (r * r * r) * jnp.mean(dn * x, axis=-1, keepdims=True)


def _rms_inproj(h, g, w):
    tm = min(TM, T)

    def body(h_ref, g_ref, w_ref, u_ref, hn_ref, hn_s):
        @pl.when(pl.program_id(1) == 0)
        def _():
            hh = h_ref[...]
            hn = (hh * _rstd(hh) * g_ref[...]).astype(BF16)
            hn_s[...] = hn
            hn_ref[...] = hn
        u_ref[...] = _dot(hn_s[...], w_ref[0])

    return pl.pallas_call(
        body, name="rms_inproj", grid=(T // tm, NCHIP),
        in_specs=[pl.BlockSpec((tm, D), lambda i, k: (i, 0)),
                  pl.BlockSpec((1, D), lambda i, k: (0, 0)),
                  pl.BlockSpec((1, D, SHW), lambda i, k: (k, 0, 0))],
        out_specs=[pl.BlockSpec((tm, SHW), lambda i, k: (i, k)),
                   pl.BlockSpec((tm, D), lambda i, k: (i, 0))],
        out_shape=[jax.ShapeDtypeStruct((T, DIN), F32), jax.ShapeDtypeStruct((T, D), BF16)],
        scratch_shapes=[pltpu.VMEM((tm, D), BF16)],
        compiler_params=_cp(("arbitrary", "arbitrary"), VMEM_BIG),
    )(h, g, w)


def _attn_tiles():
    row = lax.broadcasted_iota(jnp.int32, (SR, AQ), 0) & (AQ - 1)
    col = lax.broadcasted_iota(jnp.int32, (SR, AQ), 1)
    tr = lax.broadcasted_iota(jnp.int32, (AQ, AQ), 0)
    tc = lax.broadcasted_iota(jnp.int32, (AQ, AQ), 1)
    lane_head = lax.broadcasted_iota(jnp.int32, (1, GW), 1) // DH
    return col < row, tr, tc, [lane_head == h for h in range(HG)]


def _stack_heads(t, heads):
    return jnp.concatenate([jnp.where(m, t, 0.0) for m in heads], axis=0)


def _unstack_heads(t, heads):
    out = t[:AQ]
    for h in range(1, HG):
        out = jnp.where(heads[h], t[h * AQ:(h + 1) * AQ], out)
    return out


def _tri_sum(x, tri):
    hi = x.astype(BF16)
    lo = (x - hi.astype(F32)).astype(BF16)
    both = _dot(jnp.concatenate([hi, lo], axis=0), tri)
    return both[:SR] + both[SR:]


def _scatter_copies(ps, gots, send_sems, recv_sems):
    x, y, c = _place()
    peers = [(1 - x, y), (x, 1 - y), (1 - x, 1 - y)]
    return [pltpu.make_async_remote_copy(
        src_ref=ps[a].at[2 * px + py], dst_ref=gots[a].at[r], send_sem=send_sems.at[3 * a + r],
        recv_sem=recv_sems.at[3 * a + r], device_id=(px, py, c), device_id_type=MESH)
        for a in range(len(ps)) for r, (px, py) in enumerate(peers)]


def _gather_split_copies(ins, outs, send_sems, recv_sems):
    x, y, c = _place()
    me = 2 * x + y
    chips = [(1 - x, y), (x, 1 - y), (1 - x, 1 - y)]
    fetch, passed = [], []
    for a in range(len(ins)):
        half = ins[a].shape[0] // 2
        mine = pl.ds(c * half, half)
        for r, (px, py) in enumerate(chips):
            fetch.append(pltpu.make_async_remote_copy(
                src_ref=ins[a].at[mine], dst_ref=outs[a].at[me, mine], send_sem=send_sems.at[6 * a + r],
                recv_sem=recv_sems.at[6 * a + r], device_id=(px, py, c), device_id_type=MESH))
            landed = outs[a].at[2 * px + py, mine]
            passed.append(pltpu.make_async_remote_copy(
                src_ref=landed, dst_ref=landed, send_sem=send_sems.at[6 * a + 3 + r],
                recv_sem=recv_sems.at[6 * a + 3 + r], device_id=(x, y, 1 - c), device_id_type=MESH))
    return fetch, passed


def _pair_copies(ins, outs, send_sems, recv_sems):
    x, y, c = _place()
    copies = []
    for a in range(len(ins)):
        half = ins[a].shape[1] // 2
        copies.append(pltpu.make_async_remote_copy(
            src_ref=ins[a].at[:, pl.ds((1 - c) * half, half), :], dst_ref=outs[a], send_sem=send_sems.at[a],
            recv_sem=recv_sems.at[a], device_id=(x, y, 1 - c), device_id_type=MESH))
    return copies


def _host(body, grid, n_in, n_out, n_x, make_copies, mid=None):
    if not n_x:
        return body

    def hosting(*refs):
        a, b = n_in + n_x, n_in + 2 * n_x + n_out
        copies = make_copies(refs[n_in:a], refs[a + n_out:b], refs[-2], refs[-1])
        stages = copies if isinstance(copies, tuple) else (copies,)
        ids = [pl.program_id(d) for d in range(len(grid))]
        at = lambda step: functools.reduce(jnp.logical_and, [i == s for i, s in zip(ids, step)])

        @pl.when(at([0] * len(grid)))
        def _():
            for cp in stages[0]:
                cp.start()

        if len(stages) == 2:
            @pl.when(at(mid))
            def _():
                for cp in stages[0]:
                    cp.wait_recv()
                for cp in stages[1]:
                    cp.start()

        body(*refs[:n_in], *refs[a:a + n_out], *refs[b:-2])

        @pl.when(at([g - 1 for g in grid]))
        def _():
            if len(stages) == 2:
                for cp in stages[1]:
                    cp.wait_recv()
                for cp in stages[0] + stages[1]:
                    cp.wait_send()
            else:
                for cp in stages[0]:
                    cp.wait()

    return hosting


def _hosted_sems(n_x, per_array=3):
    n = per_array * n_x
    return [pltpu.SemaphoreType.DMA((n,)), pltpu.SemaphoreType.DMA((n,))] if n_x else []


def _attn_fwd(u, agw, shards=()):
    n = len(shards)
    grid = (NG, T // AQ)

    def body(q_ref, k_ref, v_ref, g_ref, ag_ref, o_ref, y_ref, tot_ref, kb_s, vb_s):
        qi = pl.program_id(1)

        @pl.when(qi == 0)
        def _():
            kb_s[...] = k_ref[...].astype(BF16)
            vb_s[...] = v_ref[...].astype(BF16)

        causal, tr, tc, heads = _attn_tiles()
        upper = (tr > tc).astype(BF16)
        qs = _stack_heads(q_ref[...] * 0.125, heads).astype(BF16)

        def block(kb, carry, masked):
            run, acc = carry
            k0 = pl.multiple_of(kb * AQ, AQ)
            kk = kb_s[pl.ds(k0, AQ), :]
            vv = vb_s[pl.ds(k0, AQ), :]
            z = _dot_nt(qs, kk)
            sp = _softplus(z)
            lm = jnp.where(causal, -sp, 0.0) if masked else -sp
            att = jnp.exp((z - sp) + _tri_sum(lm, upper) + run)
            if masked:
                att = jnp.where(causal, att, 0.0)
            acc = acc + _dot(att.astype(BF16), vv)
            return run + jnp.sum(lm, axis=1, keepdims=True), acc

        carry = block(qi, (jnp.zeros((SR, 1), F32), jnp.zeros((SR, GW), F32)), True)
        run, acc = lax.fori_loop(0, qi, lambda i, c: block(qi - 1 - i, c, False), carry)
        o = _unstack_heads(acc, heads)
        osq = o * o
        ms = jnp.zeros((AQ, GW), F32)
        for m in heads:
            ms = jnp.where(m, jnp.sum(jnp.where(m, osq, 0.0), axis=1, keepdims=True), ms)
        g = g_ref[...]
        o_ref[...] = o
        y_ref[...] = (o * lax.rsqrt(ms * (1.0 / DH) + EPS) * ag_ref[...] * (g * _sig(g))).astype(BF16)
        tot_ref[0] = _unstack_heads(jnp.broadcast_to(run, (SR, GW)), heads)

    res = pl.pallas_call(
        _host(body, grid, 5, 3, n, _gather_split_copies, mid=(NG - 1, grid[1] // 2)), name="attn_fwd", grid=grid,
        in_specs=[pl.BlockSpec((AQ, GW), lambda hg, qi: (qi, hg)),
                  pl.BlockSpec((T, GW), lambda hg, qi: (0, NG + hg)),
                  pl.BlockSpec((T, GW), lambda hg, qi: (0, 2 * NG + hg)),
                  pl.BlockSpec((AQ, GW), lambda hg, qi: (qi, 3 * NG + hg)),
                  pl.BlockSpec((1, GW), lambda hg, qi: (0, 0))] + [HBM_SPEC] * n,
        out_specs=[pl.BlockSpec((AQ, GW), lambda hg, qi: (qi, hg)),
                   pl.BlockSpec((AQ, GW), lambda hg, qi: (qi, hg)),
                   pl.BlockSpec((1, AQ, GW), lambda hg, qi: (hg, qi, 0))] + [HBM_SPEC] * n,
        out_shape=[jax.ShapeDtypeStruct((T, AD), F32), jax.ShapeDtypeStruct((T, AD), BF16),
                   jax.ShapeDtypeStruct((NG, T, GW), F32)]
        + [jax.ShapeDtypeStruct((NCHIP,) + s.shape, s.dtype) for s in shards],
        scratch_shapes=[pltpu.VMEM((T, GW), BF16), pltpu.VMEM((T, GW), BF16)] + _hosted_sems(n, 6),
        compiler_params=_cp(("arbitrary", "arbitrary"), VMEM_BIG),
    )(u, u, u, u, agw, *shards)
    return res[0], res[1], res[2], list(res[3:])


def _glu_conv(u, dw, db):
    tr = 256

    def body(cv_ref, cg_ref, w_ref, b_ref, c1_ref, pad_s):
        pad_s[pl.ds(0, CWP), :] = jnp.zeros((CWP, LANE), F32)
        pad_s[pl.ds(CWP, T), :] = cv_ref[...] * _sig(cg_ref[...])
        wv = w_ref[0]
        bias = b_ref[...]

        def tile(i, carry):
            r0 = pl.multiple_of(i * tr, tr)
            acc = jnp.zeros((tr, LANE), F32) + bias
            for w in range(CW):
                acc = acc + pad_s[pl.ds(r0 + (CWP - CW + 1) + w, tr), :] * wv[w:w + 1, :]
            c1_ref[pl.ds(r0, tr), :] = acc
            return carry

        lax.fori_loop(0, T // tr, tile, 0)

    return pl.pallas_call(
        body, name="glu_conv", grid=(CD // LANE,),
        in_specs=[pl.BlockSpec((T, LANE), lambda cb: (0, 16 + cb)),
                  pl.BlockSpec((T, LANE), lambda cb: (0, 20 + cb)),
                  pl.BlockSpec((1, CWP, LANE), lambda cb: (cb, 0, 0)),
                  pl.BlockSpec((1, LANE), lambda cb: (0, cb))],
        out_specs=pl.BlockSpec((T, LANE), lambda cb: (0, cb)),
        out_shape=jax.ShapeDtypeStruct((T, CD), F32),
        scratch_shapes=[pltpu.VMEM((T + CWP, LANE), F32)],
        compiler_params=_cp(("arbitrary",)),
    )(u, u, dw, db)


def _ln_silu(c1, lg, lb):
    mu = jnp.mean(c1, axis=-1, keepdims=True)
    xc = c1 - mu
    rs = lax.rsqrt(jnp.mean(xc * xc, axis=-1, keepdims=True) + EPS)
    xh = xc * rs
    ln = xh * lg + lb
    s = _sig(ln)
    return xh, rs, ln, s


def _layer_tail(c1, u, ya, h, p, lg, lb, wpw, cg, wout, pg, wgate, wple):
    tm = 256

    def body(c1_ref, gc_ref, ya_ref, h_ref, p_ref, lg_ref, lb_ref, wpw_ref, cg_ref, wout_ref,
             pg_ref, wgate_ref, wple_ref, c3_ref, yc_ref, h1_ref, gate_ref, pe_ref, h2_ref):
        _, _, ln, s = _ln_silu(c1_ref[...], lg_ref[...], lb_ref[...])
        c2 = (ln * s).astype(BF16)
        c3 = _dot(c2, wpw_ref[...])
        gc = gc_ref[...]
        yc = (c3 * _rstd(c3) * cg_ref[...] * (gc * _sig(gc))).astype(BF16)
        c3_ref[...] = c3
        yc_ref[...] = yc
        y = _dot(ya_ref[...], wout_ref[pl.ds(0, AD), :]) + _dot(yc, wout_ref[pl.ds(AD, CD), :])
        h1 = h_ref[...] + y
        hn2 = (h1 * _rstd(h1) * pg_ref[...]).astype(BF16)
        gate = _sig(_dot(hn2, wgate_ref[...]))
        pb = p_ref[...].astype(BF16)
        pe = jnp.concatenate([_dot(pb, wple_ref[k]) for k in range(NCHIP)], axis=1)
        h1_ref[...] = h1
        gate_ref[...] = gate
        pe_ref[...] = pe
        h2_ref[...] = h1 + pe * gate

    row = lambda w: pl.BlockSpec((tm, w), lambda i: (i, 0))
    full = lambda *s: pl.BlockSpec(s, lambda i: (0,) * len(s))
    return pl.pallas_call(
        body, name="layer_tail", grid=(T // tm,),
        in_specs=[row(CD), pl.BlockSpec((tm, CD), lambda i: (i, 6)), row(AD), row(D), row(PLE),
                  full(1, CD), full(1, CD), full(CD, CD), full(1, CD), full(D, D),
                  full(1, D), full(D, D), full(NCHIP, PLE, PLE)],
        out_specs=[row(CD), row(CD), row(D), row(D), row(D), row(D)],
        out_shape=[jax.ShapeDtypeStruct((T, CD), F32), jax.ShapeDtypeStruct((T, CD), BF16)]
        + [jax.ShapeDtypeStruct((T, D), F32)] * 4,
        compiler_params=_cp(("arbitrary",), VMEM_BIG),
    )(c1, u, ya, h, p, lg, lb, wpw, cg, wout, pg, wgate, wple)


def _loss_head(h, tgt, fg):
    tm = 256

    def body(h_ref, t_ref, g_ref, loss_ref, dh_ref, dg_ref):
        @pl.when(pl.program_id(0) == 0)
        def _():
            loss_ref[...] = jnp.zeros_like(loss_ref)
            dg_ref[...] = jnp.zeros_like(dg_ref)
        hh = h_ref[...]
        g = g_ref[...]
        r = _rstd(hh)
        e = hh * r * g - t_ref[...]
        loss_ref[...] += 0.5 * jnp.sum(jnp.mean(e * e, axis=-1, keepdims=True))
        dy = e * (1.0 / D)
        dg_ref[...] += jnp.sum(dy * hh * r, axis=0, keepdims=True)
        dh_ref[...] = _rms_bwd(dy, hh, r, g)

    return pl.pallas_call(
        body, name="loss_head", grid=(T // tm,),
        in_specs=[pl.BlockSpec((tm, D), lambda i: (i, 0)), pl.BlockSpec((tm, D), lambda i: (i, 0)),
                  pl.BlockSpec((1, D), lambda i: (0, 0))],
        out_specs=[pl.BlockSpec((8, LANE), lambda i: (0, 0)), pl.BlockSpec((tm, D), lambda i: (i, 0)),
                   pl.BlockSpec((1, D), lambda i: (0, 0))],
        out_shape=[jax.ShapeDtypeStruct((8, LANE), F32), jax.ShapeDtypeStruct((T, D), F32),
                   jax.ShapeDtypeStruct((1, D), F32)],
        compiler_params=_cp(("arbitrary",)),
    )(h, tgt, fg)


def _ple_out_bwd(dh2, h1, gate, pe, p, ya, yc, pg, wgate, wout):
    tm = 256

    def body(dh2_ref, h1_ref, gate_ref, pe_ref, p_ref, ya_ref, yc_ref, pg_ref, wgate_ref, wout_ref,
             dh1_ref, dy_ref, dwg_ref, dwp_ref, dwo_ref, dpg_ref):
        @pl.when(pl.program_id(0) == 0)
        def _():
            dwg_ref[...] = jnp.zeros_like(dwg_ref)
            dwp_ref[...] = jnp.zeros_like(dwp_ref)
            dwo_ref[...] = jnp.zeros_like(dwo_ref)
            dpg_ref[...] = jnp.zeros_like(dpg_ref)
        dh2 = dh2_ref[...]
        h1 = h1_ref[...]
        gate = gate_ref[...]
        pg = pg_ref[...]
        dpe = (dh2 * gate).astype(BF16)
        dgp = (dh2 * pe_ref[...] * gate * (1.0 - gate)).astype(BF16)
        r = _rstd(h1)
        hn = h1 * r
        dwg_ref[...] += _dot_tn((hn * pg).astype(BF16), dgp)
        dhn2 = _dot_nt(dgp, wgate_ref[...])
        dpg_ref[...] += jnp.sum(dhn2 * hn, axis=0, keepdims=True)
        dh1 = dh2 + _rms_bwd(dhn2, h1, r, pg)
        pb = p_ref[...].astype(BF16)
        for k in range(NCHIP):
            dwp_ref[k] += _dot_tn(pb, dpe[:, k * PLE:(k + 1) * PLE])
        dh1b = dh1.astype(BF16)
        dy_ref[...] = _dot_nt(dh1b, wout_ref[...])
        dwo_ref[pl.ds(0, AD), :] += _dot_tn(ya_ref[...], dh1b)
        dwo_ref[pl.ds(AD, CD), :] += _dot_tn(yc_ref[...], dh1b)
        dh1_ref[...] = dh1

    row = lambda w: pl.BlockSpec((tm, w), lambda i: (i, 0))
    full = lambda *s: pl.BlockSpec(s, lambda i: (0,) * len(s))
    return pl.pallas_call(
        body, name="ple_out_bwd", grid=(T // tm,),
        in_specs=[row(D), row(D), row(D), row(D), row(PLE), row(AD), row(CD),
                  full(1, D), full(D, D), full(D, D)],
        out_specs=[row(D), row(D), full(D, D), full(NCHIP, PLE, PLE), full(D, D), full(1, D)],
        out_shape=[jax.ShapeDtypeStruct((T, D), F32), jax.ShapeDtypeStruct((T, D), F32),
                   jax.ShapeDtypeStruct((D, D), F32), jax.ShapeDtypeStruct((NCHIP, PLE, PLE), F32),
                   jax.ShapeDtypeStruct((D, D), F32), jax.ShapeDtypeStruct((1, D), F32)],
        compiler_params=_cp(("arbitrary",), VMEM_BIG),
    )(dh2, h1, gate, pe, p, ya, yc, pg, wgate, wout)


def _branch_bwd(dy, o, u, c1, c3, ag, lg, lb, wpw, cg, seg):
    tm = 256

    def body(dya_ref, dyc_ref, o_ref, ga_ref, gc_ref, c1_ref, c3_ref, ag_ref, lg_ref, lb_ref, wpw_ref,
             cg_ref, seg_ref, do_ref, dga_ref, dgc_ref, dc1_ref, dwpw_ref, dag_ref, dcg_ref, dlg_ref, dlb_ref):
        @pl.when(pl.program_id(0) == 0)
        def _():
            for r_ in (dwpw_ref, dag_ref, dcg_ref, dlg_ref, dlb_ref):
                r_[...] = jnp.zeros_like(r_)
        dya = dya_ref[...]
        o = o_ref[...]
        ga = ga_ref[...]
        ag_v = ag_ref[...]
        seg_m = seg_ref[...]
        r = lax.rsqrt(_dot2(o * o, seg_m) * (1.0 / DH) + EPS)
        onr = o * r
        sg = _sig(ga)
        dga_ref[...] = dya * (onr * ag_v) * (sg * (1.0 + ga * (1.0 - sg)))
        don = dya * (ga * sg)
        dag_ref[...] += jnp.sum(don * onr, axis=0, keepdims=True)
        dn = don * ag_v
        do_ref[...] = r * dn - o * (r * r * r) * (_dot2(dn * o, seg_m) * (1.0 / DH))
        dyc = dyc_ref[...]
        c3 = c3_ref[...]
        gc = gc_ref[...]
        cg_v = cg_ref[...]
        r3 = _rstd(c3)
        cn = c3 * r3
        sc = _sig(gc)
        dgc_ref[...] = dyc * (cn * cg_v) * (sc * (1.0 + gc * (1.0 - sc)))
        dcn = dyc * (gc * sc)
        dcg_ref[...] += jnp.sum(dcn * cn, axis=0, keepdims=True)
        dc3 = _rms_bwd(dcn, c3, r3, cg_v).astype(BF16)
        lg_v = lg_ref[...]
        xh, rs, ln, s = _ln_silu(c1_ref[...], lg_v, lb_ref[...])
        c2 = (ln * s).astype(BF16)
        dwpw_ref[...] += _dot_tn(c2, dc3)
        dc2 = _dot_nt(dc3, wpw_ref[...])
        dln = dc2 * (s * (1.0 + ln * (1.0 - s)))
        dlb_ref[...] += jnp.sum(dln, axis=0, keepdims=True)
        dlg_ref[...] += jnp.sum(dln * xh, axis=0, keepdims=True)
        dxh = dln * lg_v
        dc1_ref[...] = rs * (dxh - jnp.mean(dxh, axis=-1, keepdims=True)
                             - xh * jnp.mean(dxh * xh, axis=-1, keepdims=True))

    half = lambda j: pl.BlockSpec((tm, 512), lambda i: (i, j))
    full = lambda *s: pl.BlockSpec(s, lambda i: (0,) * len(s))
    vec = jax.ShapeDtypeStruct((1, 512), F32)
    act = jax.ShapeDtypeStruct((T, 512), F32)
    return pl.pallas_call(
        body, name="branch_bwd", grid=(T // tm,),
        in_specs=[half(0), half(1), half(0), half(3), half(6), half(0), half(0),
                  full(1, AD), full(1, CD), full(1, CD), full(CD, CD), full(1, CD), full(AD, AD)],
        out_specs=[half(0), half(0), half(0), half(0), full(CD, CD), full(1, 512), full(1, 512),
                   full(1, 512), full(1, 512)],
        out_shape=[act, act, act, act, jax.ShapeDtypeStruct((CD, CD), F32), vec, vec, vec, vec],
        compiler_params=_cp(("arbitrary",), VMEM_BIG),
    )(dy, dy, o, u, u, c1, c3, ag, lg, lb, wpw, cg, seg)


def _conv_bwd(dc1, u, dw):
    tr = 256
    off = CWP - CW + 1

    def body(d_ref, cv_ref, cg_ref, w_ref, dcv_ref, dcg_ref, ddw_ref, ddb_ref, padc_s, padd_s, acc_s):
        cv = cv_ref[...]
        sg = _sig(cg_ref[...])
        padc_s[pl.ds(0, CWP), :] = jnp.zeros((CWP, LANE), F32)
        padc_s[pl.ds(CWP, T), :] = cv * sg
        padd_s[pl.ds(0, T), :] = d_ref[...]
        padd_s[pl.ds(T, CWP), :] = jnp.zeros((CWP, LANE), F32)
        acc_s[...] = jnp.zeros_like(acc_s)
        wv = w_ref[0]

        def tile(i, carry):
            r0 = pl.multiple_of(i * tr, tr)
            dt = padd_s[pl.ds(r0, tr), :]
            dc0 = jnp.zeros((tr, LANE), F32)
            for w in range(CW):
                dc0 = dc0 + padd_s[pl.ds(r0 + (CW - 1) - w, tr), :] * wv[w:w + 1, :]
                prod = dt * padc_s[pl.ds(r0 + off + w, tr), :]
                acc_s[w] += jnp.sum(prod.reshape(tr // 8, 8, LANE), axis=0)
            cvt = cv_ref[pl.ds(r0, tr), :]
            sgt = _sig(cg_ref[pl.ds(r0, tr), :])
            dcv_ref[pl.ds(r0, tr), :] = dc0 * sgt
            dcg_ref[pl.ds(r0, tr), :] = dc0 * cvt * sgt * (1.0 - sgt)
            return carry

        lax.fori_loop(0, T // tr, tile, 0)
        ddw_ref[0] = jnp.sum(acc_s[...], axis=1)
        ddb_ref[...] = jnp.sum(d_ref[...], axis=0, keepdims=True)

    col = lambda j: pl.BlockSpec((T, LANE), lambda cb: (0, j + cb))
    return pl.pallas_call(
        body, name="conv_bwd", grid=(CD // LANE,),
        in_specs=[col(0), col(16), col(20), pl.BlockSpec((1, CWP, LANE), lambda cb: (cb, 0, 0))],
        out_specs=[col(0), col(0), pl.BlockSpec((1, CWP, LANE), lambda cb: (cb, 0, 0)),
                   pl.BlockSpec((1, LANE), lambda cb: (0, cb))],
        out_shape=[jax.ShapeDtypeStruct((T, CD), F32), jax.ShapeDtypeStruct((T, CD), F32),
                   jax.ShapeDtypeStruct((NCHIP, CWP, LANE), F32), jax.ShapeDtypeStruct((1, CD), F32)],
        scratch_shapes=[pltpu.VMEM((T + CWP, LANE), F32), pltpu.VMEM((T + CWP, LANE), F32),
                        pltpu.VMEM((CWP, 8, LANE), F32)],
        compiler_params=_cp(("arbitrary",)),
    )(dc1, u, u, dw)


def _attn_bwd(u, do, tot, partials=()):
    n_x = len(partials)
    grid = (NG, T // AQ)

    def body(q_ref, k_ref, v_ref, do_ref, tot_ref, dq_ref, dk_ref, dv_ref, kb_s, vb_s):
        qi = pl.program_id(1)

        @pl.when(qi == 0)
        def _():
            kb_s[...] = k_ref[...].astype(BF16)
            vb_s[...] = v_ref[...].astype(BF16)
            dk_ref[...] = jnp.zeros_like(dk_ref)
            dv_ref[...] = jnp.zeros_like(dv_ref)

        causal, tr, tc, heads = _attn_tiles()
        upper = (tr > tc).astype(BF16)
        lower = (tr < tc).astype(BF16)
        q = q_ref[...]
        qs = _stack_heads(q * 0.125, heads).astype(BF16)
        qus = _stack_heads(q, heads).astype(BF16)
        dos = _stack_heads(do_ref[...], heads).astype(BF16)
        totv = tot_ref[0]
        tots = jnp.concatenate([totv[:, h * DH:h * DH + 1] for h in range(HG)], axis=0)

        def block(kb, carry, masked):
            lm_left, dl_left, dq = carry
            k0 = pl.multiple_of(kb * AQ, AQ)
            kk = kb_s[pl.ds(k0, AQ), :]
            vv = vb_s[pl.ds(k0, AQ), :]
            z = _dot_nt(qs, kk)
            sp = _softplus(z)
            lm = jnp.where(causal, -sp, 0.0) if masked else -sp
            lm_incl = lm_left + jnp.sum(lm, axis=1, keepdims=True)
            att = jnp.exp((z - sp) + _tri_sum(lm, upper) + (tots - lm_incl))
            if masked:
                att = jnp.where(causal, att, 0.0)
            dl = att * _dot_nt(dos, vv)
            dv_ref[pl.ds(k0, AQ), :] += _dot_tn(att.astype(BF16), dos)
            prefix = dl_left + _tri_sum(dl, lower)
            beta = jnp.exp(z - sp)
            dz = (1.0 - beta) * dl - beta * prefix
            if masked:
                dz = jnp.where(causal, dz, 0.0)
            dzs = (dz * 0.125).astype(BF16)
            dk_ref[pl.ds(k0, AQ), :] += _dot_tn(dzs, qus)
            return lm_incl, dl_left + jnp.sum(dl, axis=1, keepdims=True), dq + _dot(dzs, kk)

        zero = jnp.zeros((SR, 1), F32)
        carry = lax.fori_loop(0, qi, lambda kb, c: block(kb, c, False),
                              (zero, zero, jnp.zeros((SR, GW), F32)))
        _, _, dq = block(qi, carry, True)
        dq_ref[...] = _unstack_heads(dq, heads)

    res = pl.pallas_call(
        _host(body, grid, 5, 3, n_x, _scatter_copies), name="attn_bwd", grid=grid,
        in_specs=[pl.BlockSpec((AQ, GW), lambda hg, qi: (qi, hg)),
                  pl.BlockSpec((T, GW), lambda hg, qi: (0, NG + hg)),
                  pl.BlockSpec((T, GW), lambda hg, qi: (0, 2 * NG + hg)),
                  pl.BlockSpec((AQ, GW), lambda hg, qi: (qi, hg)),
                  pl.BlockSpec((1, AQ, GW), lambda hg, qi: (hg, qi, 0))] + [HBM_SPEC] * n_x,
        out_specs=[pl.BlockSpec((AQ, GW), lambda hg, qi: (qi, hg)),
                   pl.BlockSpec((T, GW), lambda hg, qi: (0, hg)),
                   pl.BlockSpec((T, GW), lambda hg, qi: (0, hg))] + [HBM_SPEC] * n_x,
        out_shape=[jax.ShapeDtypeStruct((T, AD), F32)] * 3
        + [jax.ShapeDtypeStruct((NCHIP - 1,) + a.shape[1:], a.dtype) for a in partials],
        scratch_shapes=[pltpu.VMEM((T, GW), BF16), pltpu.VMEM((T, GW), BF16)] + _hosted_sems(n_x),
        compiler_params=_cp(("arbitrary", "arbitrary"), VMEM_BIG),
    )(u, u, u, do, tot, *partials)
    return res[0], res[1], res[2], list(res[3:])


def _inproj_dw(hn, du):
    tm = min(TM, T)

    def body(hn_ref, du_ref, dw_ref):
        @pl.when(pl.program_id(1) == 0)
        def _():
            dw_ref[...] = jnp.zeros_like(dw_ref)
        dw_ref[0] += _dot_tn(hn_ref[...], du_ref[...])

    return pl.pallas_call(
        body, name="inproj_dw", grid=(NCHIP, T // tm),
        in_specs=[pl.BlockSpec((tm, D), lambda k, i: (i, 0)), pl.BlockSpec((tm, SHW), lambda k, i: (i, k))],
        out_specs=pl.BlockSpec((1, D, SHW), lambda k, i: (k, 0, 0)),
        out_shape=jax.ShapeDtypeStruct((NCHIP, D, SHW), F32),
        compiler_params=_cp(("arbitrary", "arbitrary"), VMEM_BIG),
    )(hn, du)


def _inproj_dx(du, w, h, g, dres, partials=(), grads=()):
    tm = min(TM, T)
    sent = list(partials) + list(grads)
    n_x = len(sent)
    grid = (T // tm, NCHIP)
    if grads:
        landing = [jax.ShapeDtypeStruct((NCHIP, a.shape[1] // 2, a.shape[2]), F32) for a in grads]
    else:
        landing = [jax.ShapeDtypeStruct((NCHIP - 1,) + a.shape[1:], a.dtype) for a in partials]

    def body(du_ref, w_ref, h_ref, g_ref, dres_ref, dh_ref, dg_ref, acc_s):
        i, k = pl.program_id(0), pl.program_id(1)

        @pl.when(jnp.logical_and(i == 0, k == 0))
        def _():
            dg_ref[...] = jnp.zeros_like(dg_ref)

        @pl.when(k == 0)
        def _():
            acc_s[...] = _dot_nt(du_ref[...], w_ref[0])

        @pl.when(k > 0)
        def _():
            acc_s[...] += _dot_nt(du_ref[...], w_ref[0])

        @pl.when(k == NCHIP - 1)
        def _():
            hh = h_ref[...]
            r = _rstd(hh)
            dhn = acc_s[...]
            dg_ref[...] += jnp.sum(dhn * hh * r, axis=0, keepdims=True)
            dh_ref[...] = dres_ref[...] + _rms_bwd(dhn, hh, r, g_ref[...])

    res = pl.pallas_call(
        _host(body, grid, 5, 2, n_x, _pair_copies if grads else _scatter_copies), name="inproj_dx", grid=grid,
        in_specs=[pl.BlockSpec((tm, SHW), lambda i, k: (i, k)),
                  pl.BlockSpec((1, D, SHW), lambda i, k: (k, 0, 0)),
                  pl.BlockSpec((tm, D), lambda i, k: (i, 0)),
                  pl.BlockSpec((1, D), lambda i, k: (0, 0)),
                  pl.BlockSpec((tm, D), lambda i, k: (i, 0))] + [HBM_SPEC] * n_x,
        out_specs=[pl.BlockSpec((tm, D), lambda i, k: (i, 0)), pl.BlockSpec((1, D), lambda i, k: (0, 0))]
        + [HBM_SPEC] * n_x,
        out_shape=[jax.ShapeDtypeStruct((T, D), F32), jax.ShapeDtypeStruct((1, D), F32)] + landing,
        scratch_shapes=[pltpu.VMEM((tm, D), F32)] + _hosted_sems(n_x),
        compiler_params=_cp(("arbitrary", "arbitrary"), VMEM_BIG),
    )(du, w, h, g, dres, *sent)
    return res[0], res[1], list(res[2:])


def _sum_pair(core, grads, gots):
    n = len(grads)

    def body(c_ref, *refs):
        for a in range(n):
            refs[2 * n + a][...] = (refs[a][...] + refs[n + a][...]).astype(BF16)

    mine = [pl.BlockSpec((1,) + s.shape[1:], lambda k, c: (k, c[0], 0)) for s in gots]
    same = [pl.BlockSpec((1,) + s.shape[1:], lambda k, c: (k, 0, 0)) for s in gots]
    return pl.pallas_call(
        body, name="sum_pair",
        grid_spec=pltpu.PrefetchScalarGridSpec(
            num_scalar_prefetch=1, grid=(NCHIP,), in_specs=mine + same, out_specs=same),
        out_shape=[jax.ShapeDtypeStruct(s.shape, BF16) for s in gots],
        compiler_params=_cp(("arbitrary",), VMEM_BIG),
    )(core, *grads, *gots)


def _sum_chips_share(chip, partials, gots):
    n = len(partials)

    def body(c_ref, *refs):
        mine, theirs = refs[2 * n:3 * n], refs[3 * n:4 * n]
        for a in range(n):
            acc = refs[a][0].astype(F32)
            for j in range(NCHIP - 1):
                acc = acc + refs[n + a][j].astype(F32)
            mine[a][...] = acc
        x, y, c = _place()
        copies = [pltpu.make_async_remote_copy(
            src_ref=mine[a], dst_ref=theirs[a], send_sem=refs[4 * n].at[a], recv_sem=refs[4 * n + 1].at[a],
            device_id=(x, y, 1 - c), device_id_type=MESH) for a in range(n)]
        for cp in copies:
            cp.start()
        for cp in copies:
            cp.wait()

    halves = [jax.ShapeDtypeStruct(s.shape[1:], F32) for s in partials]
    res = pl.pallas_call(
        body, name="sum_chips_share",
        grid_spec=pltpu.PrefetchScalarGridSpec(
            num_scalar_prefetch=1, grid=(1,),
            in_specs=[pl.BlockSpec((1,) + s.shape[1:], lambda i, c: (c[0], 0, 0)) for s in partials]
            + [pl.BlockSpec(s.shape, lambda i, c: (0, 0, 0)) for s in gots],
            out_specs=[pl.BlockSpec(s.shape[1:], lambda i, c: (0, 0)) for s in partials] + [HBM_SPEC] * n,
            scratch_shapes=[pltpu.SemaphoreType.DMA((n,)), pltpu.SemaphoreType.DMA((n,))]),
        out_shape=halves + halves,
        compiler_params=_cp(("arbitrary",), VMEM_BIG),
    )(chip, *partials, *gots)
    return res[:n], res[n:]


def _adamw(w, g, m, v, rows):
    R, C = w.shape
    c1 = 1.0 - ADAM_B1 ** ADAM_STEP
    c2 = 1.0 - ADAM_B2 ** ADAM_STEP

    def body(w_ref, g_ref, m_ref, v_ref, d_ref, nm_ref, nv_ref):
        gg = g_ref[...]
        nm = ADAM_B1 * m_ref[...] + (1.0 - ADAM_B1) * gg
        nv = ADAM_B2 * v_ref[...] + (1.0 - ADAM_B2) * (gg * gg)
        d_ref[...] = -ADAM_LR * ((nm / c1) / (jnp.sqrt(nv / c2) + ADAM_EPS) + ADAM_WD * w_ref[...])
        nm_ref[...] = nm
        nv_ref[...] = nv

    spec = pl.BlockSpec((rows, C), lambda i: (i, 0))
    sh = jax.ShapeDtypeStruct((R, C), F32)
    return pl.pallas_call(
        body, name="adamw", grid=(R // rows,), in_specs=[spec] * 4, out_specs=[spec] * 3,
        out_shape=[sh, sh, sh], compiler_params=_cp(("arbitrary",)),
    )(w, g, m, v)


HBM_SPEC = pl.BlockSpec(memory_space=pltpu.HBM)


def _place():
    return lax.axis_index("x"), lax.axis_index("y"), lax.axis_index("c")


def _all_gather_split(shard):
    def body(in_ref, out_ref, send_sems, recv_sems):
        fetch, passed = _gather_split_copies([in_ref], [out_ref], send_sems, recv_sems)
        for cp in fetch:
            cp.start()
        for got, onward in zip(fetch, passed):
            got.wait_recv()
            onward.start()
        for cp in passed:
            cp.wait_recv()
        for cp in fetch + passed:
            cp.wait_send()

    return pl.pallas_call(
        body, name="all_gather_split", in_specs=[HBM_SPEC], out_specs=HBM_SPEC,
        out_shape=jax.ShapeDtypeStruct((NCHIP,) + shard.shape, shard.dtype),
        scratch_shapes=[pltpu.SemaphoreType.DMA((6,)), pltpu.SemaphoreType.DMA((6,))],
    )(shard)


def _pair_exchange(grads):
    n = len(grads)

    def body(*refs):
        copies = _pair_copies(refs[:n], refs[n:2 * n], refs[2 * n], refs[2 * n + 1])
        for cp in copies:
            cp.start()
        for cp in copies:
            cp.wait()

    return pl.pallas_call(
        body, name="pair_exchange", in_specs=[HBM_SPEC] * n, out_specs=[HBM_SPEC] * n,
        out_shape=[jax.ShapeDtypeStruct((NCHIP, g.shape[1] // 2, g.shape[2]), F32) for g in grads],
        scratch_shapes=[pltpu.SemaphoreType.DMA((n,)), pltpu.SemaphoreType.DMA((n,))],
    )(*grads)


def _small_allreduce(mine):
    def body(m_ref, o_ref, slots, send_sems, recv_sems):
        x, y, c = _place()
        me = 4 * x + 2 * y + c
        slots[me] = m_ref[...]
        copies = []
        for r in range(1, 8):
            rx, ry, rc = (r >> 2) & 1, (r >> 1) & 1, r & 1
            peer = (x + rx - 2 * x * rx, y + ry - 2 * y * ry, c + rc - 2 * c * rc)
            cp = pltpu.make_async_remote_copy(
                src_ref=m_ref, dst_ref=slots.at[me], send_sem=send_sems.at[r - 1], recv_sem=recv_sems.at[r - 1],
                device_id=peer, device_id_type=MESH)
            cp.start()
            copies.append(cp)
        for cp in copies:
            cp.wait()
        acc = slots[0]
        for j in range(1, 8):
            acc = acc + slots[j]
        o_ref[...] = acc

    return pl.pallas_call(
        body, name="small_allreduce",
        in_specs=[pl.BlockSpec(memory_space=pltpu.VMEM)], out_specs=pl.BlockSpec(memory_space=pltpu.VMEM),
        out_shape=jax.ShapeDtypeStruct((SMALL_ROWS, LANE), F32),
        scratch_shapes=[pltpu.VMEM((8, SMALL_ROWS, LANE), F32), pltpu.SemaphoreType.DMA((7,)),
                        pltpu.SemaphoreType.DMA((7,))],
    )(mine)


def _seg_matrix():
    i = lax.broadcasted_iota(jnp.int32, (AD, AD), 0) // DH
    j = lax.broadcasted_iota(jnp.int32, (AD, AD), 1) // DH
    return (i == j).astype(BF16)


TAIL = ("w_out", "w_ple_gate", "w_ple", "w_pw", "dw_w")


def _local_step(x, p, tgt, sm, shards, chip, ci):
    seg = _seg_matrix()
    core = jnp.reshape(ci, (1,)).astype(jnp.int32)
    chip_idx = jnp.reshape(chip, (1,)).astype(jnp.int32)
    own = lambda g, s: lax.dynamic_update_index_in_dim(g, s, chip, 0)
    w_in_next = own(_all_gather_split(shards[0]["w_in"]), shards[0]["w_in"])
    h = x
    saved = []
    for l in range(DEPTH):
        w_in = w_in_next
        row = lambda name: sm[name][l:l + 1]
        u, hn = _rms_inproj(h, row("norm_g"), w_in)
        todo = [shards[l][k] for k in TAIL] + ([shards[l + 1]["w_in"]] if l + 1 < DEPTH else [])
        o, ya, tot, got = _attn_fwd(u, jnp.tile(row("attn_out_g"), (1, HG)), todo)
        got = [own(g, s) for g, s in zip(got, todo)]
        w_out = got[0].reshape(D, D)
        w_gate = got[1].reshape(D, D)
        w_ple = got[2]
        w_pw = got[3].reshape(CD, CD)
        dw = got[4]
        if l + 1 < DEPTH:
            w_in_next = got[5]
        c1 = _glu_conv(u, dw, row("dw_b"))
        c3, yc, h1, gate, pe, h2 = _layer_tail(
            c1, u, ya, h, p[l], row("conv_ln_g"), row("conv_ln_b"), w_pw, row("conv_out_g"), w_out,
            row("ple_norm_g"), w_gate, w_ple)
        saved.append(dict(h=h, u=u, hn=hn, o=o, ya=ya, tot=tot, c1=c1, c3=c3, yc=yc, h1=h1, gate=gate, pe=pe,
                          w_in=w_in, w_out=w_out, w_gate=w_gate, w_pw=w_pw, dw=dw))
        h = h2
    loss_blk, dh, dfg = _loss_head(h, tgt, sm["final_g"])
    small = [None] * DEPTH
    pending, partials, arrived = [], {}, {}
    pair_sum = lambda grads: _sum_pair(core, grads, _pair_exchange(grads))
    for l in reversed(range(DEPTH)):
        s = saved[l]
        row = lambda name: sm[name][l:l + 1]
        dh1, dy, dwg, dwp, dwo, dpg = _ple_out_bwd(
            dh, s["h1"], s["gate"], s["pe"], p[l], s["ya"], s["yc"], row("ple_norm_g"), s["w_gate"], s["w_out"])
        ag_t = jnp.tile(row("attn_out_g"), (1, AD // DH))
        do, dga, dgc, dc1, dwpw, dag, dcg, dlg, dlb = _branch_bwd(
            dy, s["o"], s["u"], s["c1"], s["c3"], ag_t, row("conv_ln_g"), row("conv_ln_b"), s["w_pw"],
            row("conv_out_g"), seg)
        dcv, dcgate, ddw, ddb = _conv_bwd(dc1, s["u"], s["dw"])
        tail = [dwo.reshape(NCHIP, 256, D), dwg.reshape(NCHIP, 256, D), dwp, dwpw.reshape(NCHIP, 128, CD), ddw]
        if l == 0:
            partials[(l, "tail")] = pair_sum(tail)
            pending.append((l, "tail"))
        send = [t for key in pending for t in partials[key]]
        dq, dk, dv, got = _attn_bwd(s["u"], do, s["tot"], send)
        for key in pending:
            arrived[key], got = got[:len(partials[key])], got[len(partials[key]):]
        pending = []
        du = jnp.concatenate([dq, dk, dv, dga, dcv, dcgate, dgc], axis=1).astype(BF16)
        dwin = _inproj_dw(s["hn"], du)
        if l == 0:
            partials[(l, "w_in")] = pair_sum([dwin])
            dh, dng, arrived[(l, "w_in")] = _inproj_dx(du, s["w_in"], s["h"], row("norm_g"), dh1,
                                                      partials[(l, "w_in")])
        else:
            dh, dng, halves = _inproj_dx(du, s["w_in"], s["h"], row("norm_g"), dh1, grads=tail + [dwin])
            tail_p = _sum_pair(core, tail + [dwin], halves)
            partials[(l, "tail")], partials[(l, "w_in")] = tail_p[:-1], tail_p[-1:]
            pending = [(l, "tail"), (l, "w_in")]
        small[l] = dict(norm_g=dng, attn_out_g=dag.reshape(AD // DH, DH).sum(axis=0, keepdims=True), dw_b=ddb,
                        conv_ln_g=dlg, conv_ln_b=dlb, conv_out_g=dcg, ple_norm_g=dpg)
    big = []
    for l in range(DEPTH):
        ps = partials[(l, "w_in")] + partials[(l, "tail")]
        mine, theirs = _sum_chips_share(chip_idx, ps, arrived[(l, "w_in")] + arrived[(l, "tail")])
        big.append(dict(zip(BIG, [jnp.where(core[0] == 0, jnp.concatenate([m, t]), jnp.concatenate([t, m]))
                                  for m, t in zip(mine, theirs)])))
    return loss_blk[0, 0], dh, big, small, dfg


BIG = ("w_in", "w_out", "w_ple_gate", "w_ple", "w_pw", "dw_w")
SMALL2 = ("norm_g", "ple_norm_g", "dw_b", "conv_ln_g", "conv_ln_b", "conv_out_g", "attn_out_g")


def _pack_small(two, final):
    flat = jnp.concatenate([two[k].reshape(-1) for k in SMALL2] + [final.reshape(-1)])
    flat = jnp.concatenate([flat, jnp.zeros((SMALL_ROWS * LANE - flat.shape[0],), F32)])
    return flat.reshape(SMALL_ROWS, LANE)


def _unpack_small(packed, like_two, like_final):
    flat = packed.reshape(-1)
    out, off = {}, 0
    for k in SMALL2:
        n = like_two[k].size
        out[k] = flat[off:off + n].reshape(like_two[k].shape)
        off += n
    return out, flat[off:off + like_final.size].reshape(like_final.shape)


def kernel(x, p, norm_g, w_in, attn_out_g, dw_w, dw_b, conv_ln_g, conv_ln_b, w_pw, conv_out_g, w_out, ple_norm_g, w_ple_gate, w_ple, final_g, loss_target, m_norm_g, m_w_in, m_attn_out_g, m_dw_w, m_dw_b, m_conv_ln_g, m_conv_ln_b, m_w_pw, m_conv_out_g, m_w_out, m_ple_norm_g, m_w_ple_gate, m_w_ple, m_final_g, v_norm_g, v_w_in, v_attn_out_g, v_dw_w, v_dw_b, v_conv_ln_g, v_conv_ln_b, v_w_pw, v_conv_out_g, v_w_out, v_ple_norm_g, v_w_ple_gate, v_w_ple, v_final_g):
    W = dict(norm_g=norm_g, w_in=w_in, attn_out_g=attn_out_g, dw_w=dw_w, dw_b=dw_b, conv_ln_g=conv_ln_g,
             conv_ln_b=conv_ln_b, w_pw=w_pw, conv_out_g=conv_out_g, w_out=w_out, ple_norm_g=ple_norm_g,
             w_ple_gate=w_ple_gate, w_ple=w_ple, final_g=final_g)
    M = dict(norm_g=m_norm_g, w_in=m_w_in, attn_out_g=m_attn_out_g, dw_w=m_dw_w, dw_b=m_dw_b,
             conv_ln_g=m_conv_ln_g, conv_ln_b=m_conv_ln_b, w_pw=m_w_pw, conv_out_g=m_conv_out_g, w_out=m_w_out,
             ple_norm_g=m_ple_norm_g, w_ple_gate=m_w_ple_gate, w_ple=m_w_ple, final_g=m_final_g)
    V = dict(norm_g=v_norm_g, w_in=v_w_in, attn_out_g=v_attn_out_g, dw_w=v_dw_w, dw_b=v_dw_b,
             conv_ln_g=v_conv_ln_g, conv_ln_b=v_conv_ln_b, w_pw=v_w_pw, conv_out_g=v_conv_out_g, w_out=v_w_out,
             ple_norm_g=v_ple_norm_g, w_ple_gate=v_w_ple_gate, w_ple=v_w_ple, final_g=v_final_g)
    order = ("norm_g", "w_in", "attn_out_g", "dw_w", "dw_b", "conv_ln_g", "conv_ln_b", "w_pw", "conv_out_g",
             "w_out", "ple_norm_g", "w_ple_gate", "w_ple", "final_g")

    pad_taps = lambda a: jnp.pad(a, ((0, 0), (0, CWP - CW), (0, 0)))
    cast = dict(w_in=w_in.astype(BF16), w_out=w_out.astype(BF16), w_ple_gate=w_ple_gate.astype(BF16),
                w_ple=w_ple.astype(BF16), w_pw=w_pw.astype(BF16), dw_w=pad_taps(dw_w))
    shards = [{k: v[l] for k, v in cast.items()} for l in range(DEPTH)]
    xi, yi, ci = lax.axis_index("x"), lax.axis_index("y"), lax.axis_index("c")
    chip = 2 * xi + yi

    sm = {k: W[k] for k in SMALL2}
    sm["final_g"] = final_g.reshape(1, D)
    loss_part, grad_x, big, small, dfg = _local_step(x[0], p[:, 0], loss_target[0], sm, shards, chip, ci)
    g_big = {name: jnp.stack([big[l][name] for l in range(DEPTH)]).reshape(cast[name].shape) for name in BIG}

    small_two = {k: jnp.concatenate([small[l][k] for l in range(DEPTH)], axis=0) for k in SMALL2}
    g_small_packed = _small_allreduce(_pack_small(small_two, dfg).at[SMALL_ROWS - 1, LANE - 1].set(loss_part))
    loss = g_small_packed[SMALL_ROWS - 1, LANE - 1]
    g_small, g_final = _unpack_small(g_small_packed, {k: W[k] for k in SMALL2}, final_g)

    grads, deltas, new_m, new_v = {}, {}, {}, {}
    for name in BIG:
        wv = pad_taps(W[name]) if name == "dw_w" else W[name]
        mv = pad_taps(M[name]) if name == "dw_w" else M[name]
        vv = pad_taps(V[name]) if name == "dw_w" else V[name]
        gg = g_big[name]
        cols = wv.shape[-1]
        rows_total = wv.size // cols
        tile_rows = min(rows_total, 256)
        d2, m2, v2 = _adamw(wv.reshape(rows_total, cols), gg.reshape(rows_total, cols),
                            mv.reshape(rows_total, cols), vv.reshape(rows_total, cols), tile_rows)
        if name == "dw_w":
            cut = lambda a: a.reshape(DEPTH, CWP, LANE)[:, :CW]
            grads[name], deltas[name], new_m[name], new_v[name] = cut(gg), cut(d2), cut(m2), cut(v2)
        else:
            grads[name] = gg
            deltas[name], new_m[name], new_v[name] = (t.reshape(wv.shape) for t in (d2, m2, v2))
    ws = _pack_small({k: W[k] for k in SMALL2}, final_g)
    ms = _pack_small({k: M[k] for k in SMALL2}, m_final_g)
    vs = _pack_small({k: V[k] for k in SMALL2}, v_final_g)
    ds, nms, nvs = _adamw(ws, g_small_packed, ms, vs, SMALL_ROWS)
    for packed, dst in ((ds, deltas), (nms, new_m), (nvs, new_v)):
        two, fin = _unpack_small(packed, {k: W[k] for k in SMALL2}, final_g)
        dst.update(two)
        dst["final_g"] = fin
    grads.update(g_small)
    grads["final_g"] = g_final

    return (loss, grad_x[None], *[grads[n] for n in order], *[deltas[n] for n in order],
            *[new_m[n] for n in order], *[new_v[n] for n in order])
```

```python
import functools

import jax
import jax.numpy as jnp
from jax import lax
from jax.experimental import pallas as pl
from jax.experimental.pallas import tpu as pltpu

F32 = jnp.float32
BF16 = jnp.bfloat16

T = 2048
D = 1024
DIN = 3584
NCHIP = 4
SHW = DIN // NCHIP
AD = 512
CD = 512
DH = 64
CW = 31
CWP = 32
PLE = 256
DEPTH = 2
EPS = 1e-6
AQ = 256
HG = 4
GW = HG * DH
SR = HG * AQ
NG = AD // GW
LANE = 128
TM = 1024

ADAM_LR = 0.001
ADAM_B1 = 0.9
ADAM_B2 = 0.999
ADAM_EPS = 1e-08
ADAM_WD = 0.01
ADAM_STEP = 10

SMALL_ROWS = 80

VMEM_BIG = 56 * 1024 * 1024
MESH = pl.DeviceIdType.MESH


def _cp(sem=None, vmem=None):
    kw = {}
    if sem is not None:
        kw["dimension_semantics"] = sem
    if vmem is not None:
        kw["vmem_limit_bytes"] = vmem
    return pltpu.CompilerParams(**kw)


def _dot(a, b):
    return jnp.dot(a, b, preferred_element_type=F32)


def _dot_nt(a, b):
    return lax.dot_general(a, b, (((1,), (1,)), ((), ())), preferred_element_type=F32)


def _dot_tn(a, b):
    return lax.dot_general(a, b, (((0,), (0,)), ((), ())), preferred_element_type=F32)


def _dot2(x, m):
    hi = x.astype(BF16)
    lo = (x - hi.astype(F32)).astype(BF16)
    return _dot(hi, m) + _dot(lo, m)


def _sig(x):
    return 1.0 / (1.0 + jnp.exp(-x))


def _softplus(z):
    return jnp.maximum(z, 0.0) + jnp.log(1.0 + jnp.exp(-jnp.abs(z)))


def _rstd(x):
    return lax.rsqrt(jnp.mean(x * x, axis=-1, keepdims=True) + EPS)


def _rms_bwd(dy, x, r, g):
    dn = dy * g
    return r * dn - x * (r * r * r) * jnp.mean(dn * x, axis=-1, keepdims=True)


def _rms_inproj(h, g, w):
    tm = min(TM, T)

    def body(h_ref, g_ref, w_ref, u_ref, hn_ref, hn_s):
        @pl.when(pl.program_id(1) == 0)
        def _():
            hh = h_ref[...]
            hn = (hh * _rstd(hh) * g_ref[...]).astype(BF16)
            hn_s[...] = hn
            hn_ref[...] = hn
        u_ref[...] = _dot(hn_s[...], w_ref[0])

    return pl.pallas_call(
        body, name="rms_inproj", grid=(T // tm, NCHIP),
        in_specs=[pl.BlockSpec((tm, D), lambda i, k: (i, 0)),
                  pl.BlockSpec((1, D), lambda i, k: (0, 0)),
                  pl.BlockSpec((1, D, SHW), lambda i, k: (k, 0, 0))],
        out_specs=[pl.BlockSpec((tm, SHW), lambda i, k: (i, k)),
                   pl.BlockSpec((tm, D), lambda i, k: (i, 0))],
        out_shape=[jax.ShapeDtypeStruct((T, DIN), F32), jax.ShapeDtypeStruct((T, D), BF16)],
        scratch_shapes=[pltpu.VMEM((tm, D), BF16)],
        compiler_params=_cp(("arbitrary", "arbitrary"), VMEM_BIG),
    )(h, g, w)


def _attn_tiles():
    row = lax.broadcasted_iota(jnp.int32, (SR, AQ), 0) & (AQ - 1)
    col = lax.broadcasted_iota(jnp.int32, (SR, AQ), 1)
    tr = lax.broadcasted_iota(jnp.int32, (AQ, AQ), 0)
    tc = lax.broadcasted_iota(jnp.int32, (AQ, AQ), 1)
    lane_head = lax.broadcasted_iota(jnp.int32, (1, GW), 1) // DH
    return col < row, tr, tc, [lane_head == h for h in range(HG)]


def _stack_heads(t, heads):
    return jnp.concatenate([jnp.where(m, t, 0.0) for m in heads], axis=0)


def _unstack_heads(t, heads):
    out = t[:AQ]
    for h in range(1, HG):
        out = jnp.where(heads[h], t[h * AQ:(h + 1) * AQ], out)
    return out


def _tri_sum(x, tri):
    hi = x.astype(BF16)
    lo = (x - hi.astype(F32)).astype(BF16)
    both = _dot(jnp.concatenate([hi, lo], axis=0), tri)
    return both[:SR] + both[SR:]


def _scatter_copies(ps, gots, send_sems, recv_sems):
    x, y, c = _place()
    peers = [(1 - x, y), (x, 1 - y), (1 - x, 1 - y)]
    return [pltpu.make_async_remote_copy(
        src_ref=ps[a].at[2 * px + py], dst_ref=gots[a].at[r], send_sem=send_sems.at[3 * a + r],
        recv_sem=recv_sems.at[3 * a + r], device_id=(px, py, c), device_id_type=MESH)
        for a in range(len(ps)) for r, (px, py) in enumerate(peers)]


def _gather_split_copies(ins, outs, send_sems, recv_sems):
    x, y, c = _place()
    me = 2 * x + y
    chips = [(1 - x, y), (x, 1 - y), (1 - x, 1 - y)]
    fetch, passed = [], []
    for a in range(len(ins)):
        half = ins[a].shape[0] // 2
        mine = pl.ds(c * half, half)
        for r, (px, py) in enumerate(chips):
            fetch.append(pltpu.make_async_remote_copy(
                src_ref=ins[a].at[mine], dst_ref=outs[a].at[me, mine], send_sem=send_sems.at[6 * a + r],
                recv_sem=recv_sems.at[6 * a + r], device_id=(px, py, c), device_id_type=MESH))
            landed = outs[a].at[2 * px + py, mine]
            passed.append(pltpu.make_async_remote_copy(
                src_ref=landed, dst_ref=landed, send_sem=send_sems.at[6 * a + 3 + r],
                recv_sem=recv_sems.at[6 * a + 3 + r], device_id=(x, y, 1 - c), device_id_type=MESH))
    return fetch, passed


def _pair_copies(ins, outs, send_sems, recv_sems):
    x, y, c = _place()
    copies = []
    for a in range(len(ins)):
        half = ins[a].shape[1] // 2
        copies.append(pltpu.make_async_remote_copy(
            src_ref=ins[a].at[:, pl.ds((1 - c) * half, half), :], dst_ref=outs[a], send_sem=send_sems.at[a],
            recv_sem=recv_sems.at[a], device_id=(x, y, 1 - c), device_id_type=MESH))
    return copies


def _host(body, grid, n_in, n_out, n_x, make_copies, mid=None):
    if not n_x:
        return body

    def hosting(*refs):
        a, b = n_in + n_x, n_in + 2 * n_x + n_out
        copies = make_copies(refs[n_in:a], refs[a + n_out:b], refs[-2], refs[-1])
        stages = copies if isinstance(copies, tuple) else (copies,)
        ids = [pl.program_id(d) for d in range(len(grid))]
        at = lambda step: functools.reduce(jnp.logical_and, [i == s for i, s in zip(ids, step)])

        @pl.when(at([0] * len(grid)))
        def _():
            for cp in stages[0]:
                cp.start()

        if len(stages) == 2:
            @pl.when(at(mid))
            def _():
                for cp in stages[0]:
                    cp.wait_recv()
                for cp in stages[1]:
                    cp.start()

        body(*refs[:n_in], *refs[a:a + n_out], *refs[b:-2])

        @pl.when(at([g - 1 for g in grid]))
        def _():
            if len(stages) == 2:
                for cp in stages[1]:
                    cp.wait_recv()
                for cp in stages[0] + stages[1]:
                    cp.wait_send()
            else:
                for cp in stages[0]:
                    cp.wait()

    return hosting


def _hosted_sems(n_x, per_array=3):
    n = per_array * n_x
    return [pltpu.SemaphoreType.DMA((n,)), pltpu.SemaphoreType.DMA((n,))] if n_x else []


def _attn_fwd(u, agw, shards=()):
    n = len(shards)
    grid = (NG, T // AQ)

    def body(q_ref, k_ref, v_ref, g_ref, ag_ref, o_ref, y_ref, tot_ref, kb_s, vb_s):
        qi = pl.program_id(1)

        @pl.when(qi == 0)
        def _():
            kb_s[...] = k_ref[...].astype(BF16)
            vb_s[...] = v_ref[...].astype(BF16)

        causal, tr, tc, heads = _attn_tiles()
        upper = (tr > tc).astype(BF16)
        qs = _stack_heads(q_ref[...] * 0.125, heads).astype(BF16)

        def block(kb, carry, masked):
            run, acc = carry
            k0 = pl.multiple_of(kb * AQ, AQ)
            kk = kb_s[pl.ds(k0, AQ), :]
            vv = vb_s[pl.ds(k0, AQ), :]
            z = _dot_nt(qs, kk)
            sp = _softplus(z)
            lm = jnp.where(causal, -sp, 0.0) if masked else -sp
            att = jnp.exp((z - sp) + _tri_sum(lm, upper) + run)
            if masked:
                att = jnp.where(causal, att, 0.0)
            acc = acc + _dot(att.astype(BF16), vv)
            return run + jnp.sum(lm, axis=1, keepdims=True), acc

        carry = block(qi, (jnp.zeros((SR, 1), F32), jnp.zeros((SR, GW), F32)), True)
        run, acc = lax.fori_loop(0, qi, lambda i, c: block(qi - 1 - i, c, False), carry)
        o = _unstack_heads(acc, heads)
        osq = o * o
        ms = jnp.zeros((AQ, GW), F32)
        for m in heads:
            ms = jnp.where(m, jnp.sum(jnp.where(m, osq, 0.0), axis=1, keepdims=True), ms)
        g = g_ref[...]
        o_ref[...] = o
        y_ref[...] = (o * lax.rsqrt(ms * (1.0 / DH) + EPS) * ag_ref[...] * (g * _sig(g))).astype(BF16)
        tot_ref[0] = _unstack_heads(jnp.broadcast_to(run, (SR, GW)), heads)

    res = pl.pallas_call(
        _host(body, grid, 5, 3, n, _gather_split_copies, mid=(NG - 1, grid[1] // 2)), name="attn_fwd", grid=grid,
        in_specs=[pl.BlockSpec((AQ, GW), lambda hg, qi: (qi, hg)),
                  pl.BlockSpec((T, GW), lambda hg, qi: (0, NG + hg)),
                  pl.BlockSpec((T, GW), lambda hg, qi: (0, 2 * NG + hg)),
                  pl.BlockSpec((AQ, GW), lambda hg, qi: (qi, 3 * NG + hg)),
                  pl.BlockSpec((1, GW), lambda hg, qi: (0, 0))] + [HBM_SPEC] * n,
        out_specs=[pl.BlockSpec((AQ, GW), lambda hg, qi: (qi, hg)),
                   pl.BlockSpec((AQ, GW), lambda hg, qi: (qi, hg)),
                   pl.BlockSpec((1, AQ, GW), lambda hg, qi: (hg, qi, 0))] + [HBM_SPEC] * n,
        out_shape=[jax.ShapeDtypeStruct((T, AD), F32), jax.ShapeDtypeStruct((T, AD), BF16),
                   jax.ShapeDtypeStruct((NG, T, GW), F32)]
        + [jax.ShapeDtypeStruct((NCHIP,) + s.shape, s.dtype) for s in shards],
        scratch_shapes=[pltpu.VMEM((T, GW), BF16), pltpu.VMEM((T, GW), BF16)] + _hosted_sems(n, 6),
        compiler_params=_cp(("arbitrary", "arbitrary"), VMEM_BIG),
    )(u, u, u, u, agw, *shards)
    return res[0], res[1], res[2], list(res[3:])


def _glu_conv(u, dw, db):
    tr = 256

    def body(cv_ref, cg_ref, w_ref, b_ref, c1_ref, pad_s):
        pad_s[pl.ds(0, CWP), :] = jnp.zeros((CWP, LANE), F32)
        pad_s[pl.ds(CWP, T), :] = cv_ref[...] * _sig(cg_ref[...])
        wv = w_ref[0]
        bias = b_ref[...]

        def tile(i, carry):
            r0 = pl.multiple_of(i * tr, tr)
            acc = jnp.zeros((tr, LANE), F32) + bias
            for w in range(CW):
                acc = acc + pad_s[pl.ds(r0 + (CWP - CW + 1) + w, tr), :] * wv[w:w + 1, :]
            c1_ref[pl.ds(r0, tr), :] = acc
            return carry

        lax.fori_loop(0, T // tr, tile, 0)

    return pl.pallas_call(
        body, name="glu_conv", grid=(CD // LANE,),
        in_specs=[pl.BlockSpec((T, LANE), lambda cb: (0, 16 + cb)),
                  pl.BlockSpec((T, LANE), lambda cb: (0, 20 + cb)),
                  pl.BlockSpec((1, CWP, LANE), lambda cb: (cb, 0, 0)),
                  pl.BlockSpec((1, LANE), lambda cb: (0, cb))],
        out_specs=pl.BlockSpec((T, LANE), lambda cb: (0, cb)),
        out_shape=jax.ShapeDtypeStruct((T, CD), F32),
        scratch_shapes=[pltpu.VMEM((T + CWP, LANE), F32)],
        compiler_params=_cp(("arbitrary",)),
    )(u, u, dw, db)


def _ln_silu(c1, lg, lb):
    mu = jnp.mean(c1, axis=-1, keepdims=True)
    xc = c1 - mu
    rs = lax.rsqrt(jnp.mean(xc * xc, axis=-1, keepdims=True) + EPS)
    xh = xc * rs
    ln = xh * lg + lb
    s = _sig(ln)
    return xh, rs, ln, s


def _layer_tail(c1, u, ya, h, p, lg, lb, wpw, cg, wout, pg, wgate, wple):
    tm = 256

    def body(c1_ref, gc_ref, ya_ref, h_ref, p_ref, lg_ref, lb_ref, wpw_ref, cg_ref, wout_ref,
             pg_ref, wgate_ref, wple_ref, c3_ref, yc_ref, h1_ref, gate_ref, pe_ref, h2_ref):
        _, _, ln, s = _ln_silu(c1_ref[...], lg_ref[...], lb_ref[...])
        c2 = (ln * s).astype(BF16)
        c3 = _dot(c2, wpw_ref[...])
        gc = gc_ref[...]
        yc = (c3 * _rstd(c3) * cg_ref[...] * (gc * _sig(gc))).astype(BF16)
        c3_ref[...] = c3
        yc_ref[...] = yc
        y = _dot(ya_ref[...], wout_ref[pl.ds(0, AD), :]) + _dot(yc, wout_ref[pl.ds(AD, CD), :])
        h1 = h_ref[...] + y
        hn2 = (h1 * _rstd(h1) * pg_ref[...]).astype(BF16)
        gate = _sig(_dot(hn2, wgate_ref[...]))
        pb = p_ref[...].astype(BF16)
        pe = jnp.concatenate([_dot(pb, wple_ref[k]) for k in range(NCHIP)], axis=1)
        h1_ref[...] = h1
        gate_ref[...] = gate.astype(BF16)
        pe_ref[...] = pe.astype(BF16)
        h2_ref[...] = h1 + pe * gate

    row = lambda w: pl.BlockSpec((tm, w), lambda i: (i, 0))
    full = lambda *s: pl.BlockSpec(s, lambda i: (0,) * len(s))
    return pl.pallas_call(
        body, name="layer_tail", grid=(T // tm,),
        in_specs=[row(CD), pl.BlockSpec((tm, CD), lambda i: (i, 6)), row(AD), row(D), row(PLE),
                  full(1, CD), full(1, CD), full(CD, CD), full(1, CD), full(D, D),
                  full(1, D), full(D, D), full(NCHIP, PLE, PLE)],
        out_specs=[row(CD), row(CD), row(D), row(D), row(D), row(D)],
        out_shape=[jax.ShapeDtypeStruct((T, CD), F32), jax.ShapeDtypeStruct((T, CD), BF16),
                   jax.ShapeDtypeStruct((T, D), F32), jax.ShapeDtypeStruct((T, D), BF16),
                   jax.ShapeDtypeStruct((T, D), BF16), jax.ShapeDtypeStruct((T, D), F32)],
        compiler_params=_cp(("arbitrary",), VMEM_BIG),
    )(c1, u, ya, h, p, lg, lb, wpw, cg, wout, pg, wgate, wple)


def _loss_head(h, tgt, fg):
    tm = 256

    def body(h_ref, t_ref, g_ref, loss_ref, dh_ref, dg_ref):
        @pl.when(pl.program_id(0) == 0)
        def _():
            loss_ref[...] = jnp.zeros_like(loss_ref)
            dg_ref[...] = jnp.zeros_like(dg_ref)
        hh = h_ref[...]
        g = g_ref[...]
        r = _rstd(hh)
        e = hh * r * g - t_ref[...]
        loss_ref[...] += 0.5 * jnp.sum(jnp.mean(e * e, axis=-1, keepdims=True))
        dy = e * (1.0 / D)
        dg_ref[...] += jnp.sum(dy * hh * r, axis=0, keepdims=True)
        dh_ref[...] = _rms_bwd(dy, hh, r, g)

    return pl.pallas_call(
        body, name="loss_head", grid=(T // tm,),
        in_specs=[pl.BlockSpec((tm, D), lambda i: (i, 0)), pl.BlockSpec((tm, D), lambda i: (i, 0)),
                  pl.BlockSpec((1, D), lambda i: (0, 0))],
        out_specs=[pl.BlockSpec((8, LANE), lambda i: (0, 0)), pl.BlockSpec((tm, D), lambda i: (i, 0)),
                   pl.BlockSpec((1, D), lambda i: (0, 0))],
        out_shape=[jax.ShapeDtypeStruct((8, LANE), F32), jax.ShapeDtypeStruct((T, D), F32),
                   jax.ShapeDtypeStruct((1, D), F32)],
        compiler_params=_cp(("arbitrary",)),
    )(h, tgt, fg)


def _ple_out_bwd(dh2, h1, gate, pe, p, ya, yc, pg, wgate, wout):
    tm = 256

    def body(dh2_ref, h1_ref, gate_ref, pe_ref, p_ref, ya_ref, yc_ref, pg_ref, wgate_ref, wout_ref,
             dh1_ref, dy_ref, dwg_ref, dwp_ref, dwo_ref, dpg_ref):
        @pl.when(pl.program_id(0) == 0)
        def _():
            dwg_ref[...] = jnp.zeros_like(dwg_ref)
            dwp_ref[...] = jnp.zeros_like(dwp_ref)
            dwo_ref[...] = jnp.zeros_like(dwo_ref)
            dpg_ref[...] = jnp.zeros_like(dpg_ref)
        dh2 = dh2_ref[...]
        h1 = h1_ref[...]
        gate = gate_ref[...].astype(F32)
        pg = pg_ref[...]
        dpe = (dh2 * gate).astype(BF16)
        dgp = (dh2 * pe_ref[...].astype(F32) * gate * (1.0 - gate)).astype(BF16)
        r = _rstd(h1)
        hn = h1 * r
        dwg_ref[...] += _dot_tn((hn * pg).astype(BF16), dgp)
        dhn2 = _dot_nt(dgp, wgate_ref[...])
        dpg_ref[...] += jnp.sum(dhn2 * hn, axis=0, keepdims=True)
        dh1 = dh2 + _rms_bwd(dhn2, h1, r, pg)
        pb = p_ref[...].astype(BF16)
        for k in range(NCHIP):
            dwp_ref[k] += _dot_tn(pb, dpe[:, k * PLE:(k + 1) * PLE])
        dh1b = dh1.astype(BF16)
        dy_ref[...] = _dot_nt(dh1b, wout_ref[...])
        dwo_ref[pl.ds(0, AD), :] += _dot_tn(ya_ref[...], dh1b)
        dwo_ref[pl.ds(AD, CD), :] += _dot_tn(yc_ref[...], dh1b)
        dh1_ref[...] = dh1

    row = lambda w: pl.BlockSpec((tm, w), lambda i: (i, 0))
    full = lambda *s: pl.BlockSpec(s, lambda i: (0,) * len(s))
    return pl.pallas_call(
        body, name="ple_out_bwd", grid=(T // tm,),
        in_specs=[row(D), row(D), row(D), row(D), row(PLE), row(AD), row(CD),
                  full(1, D), full(D, D), full(D, D)],
        out_specs=[row(D), row(D), full(D, D), full(NCHIP, PLE, PLE), full(D, D), full(1, D)],
        out_shape=[jax.ShapeDtypeStruct((T, D), F32), jax.ShapeDtypeStruct((T, D), F32),
                   jax.ShapeDtypeStruct((D, D), F32), jax.ShapeDtypeStruct((NCHIP, PLE, PLE), F32),
                   jax.ShapeDtypeStruct((D, D), F32), jax.ShapeDtypeStruct((1, D), F32)],
        compiler_params=_cp(("arbitrary",), VMEM_BIG),
    )(dh2, h1, gate, pe, p, ya, yc, pg, wgate, wout)


def _branch_bwd(dy, o, u, c1, c3, ag, lg, lb, wpw, cg, seg):
    tm = 256

    def body(dya_ref, dyc_ref, o_ref, ga_ref, gc_ref, c1_ref, c3_ref, ag_ref, lg_ref, lb_ref, wpw_ref,
             cg_ref, seg_ref, do_ref, dga_ref, dgc_ref, dc1_ref, dwpw_ref, dag_ref, dcg_ref, dlg_ref, dlb_ref):
        @pl.when(pl.program_id(0) == 0)
        def _():
            for r_ in (dwpw_ref, dag_ref, dcg_ref, dlg_ref, dlb_ref):
                r_[...] = jnp.zeros_like(r_)
        dya = dya_ref[...]
        o = o_ref[...]
        ga = ga_ref[...]
        ag_v = ag_ref[...]
        seg_m = seg_ref[...]
        r = lax.rsqrt(_dot2(o * o, seg_m) * (1.0 / DH) + EPS)
        onr = o * r
        sg = _sig(ga)
        dga_ref[...] = (dya * (onr * ag_v) * (sg * (1.0 + ga * (1.0 - sg)))).astype(BF16)
        don = dya * (ga * sg)
        dag_ref[...] += jnp.sum(don * onr, axis=0, keepdims=True)
        dn = don * ag_v
        do_ref[...] = r * dn - o * (r * r * r) * (_dot2(dn * o, seg_m) * (1.0 / DH))
        dyc = dyc_ref[...]
        c3 = c3_ref[...]
        gc = gc_ref[...]
        cg_v = cg_ref[...]
        r3 = _rstd(c3)
        cn = c3 * r3
        sc = _sig(gc)
        dgc_ref[...] = (dyc * (cn * cg_v) * (sc * (1.0 + gc * (1.0 - sc)))).astype(BF16)
        dcn = dyc * (gc * sc)
        dcg_ref[...] += jnp.sum(dcn * cn, axis=0, keepdims=True)
        dc3 = _rms_bwd(dcn, c3, r3, cg_v).astype(BF16)
        lg_v = lg_ref[...]
        xh, rs, ln, s = _ln_silu(c1_ref[...], lg_v, lb_ref[...])
        c2 = (ln * s).astype(BF16)
        dwpw_ref[...] += _dot_tn(c2, dc3)
        dc2 = _dot_nt(dc3, wpw_ref[...])
        dln = dc2 * (s * (1.0 + ln * (1.0 - s)))
        dlb_ref[...] += jnp.sum(dln, axis=0, keepdims=True)
        dlg_ref[...] += jnp.sum(dln * xh, axis=0, keepdims=True)
        dxh = dln * lg_v
        dc1_ref[...] = rs * (dxh - jnp.mean(dxh, axis=-1, keepdims=True)
                             - xh * jnp.mean(dxh * xh, axis=-1, keepdims=True))

    half = lambda j: pl.BlockSpec((tm, 512), lambda i: (i, j))
    full = lambda *s: pl.BlockSpec(s, lambda i: (0,) * len(s))
    vec = jax.ShapeDtypeStruct((1, 512), F32)
    act = jax.ShapeDtypeStruct((T, 512), F32)
    return pl.pallas_call(
        body, name="branch_bwd", grid=(T // tm,),
        in_specs=[half(0), half(1), half(0), half(3), half(6), half(0), half(0),
                  full(1, AD), full(1, CD), full(1, CD), full(CD, CD), full(1, CD), full(AD, AD)],
        out_specs=[half(0), half(0), half(0), half(0), full(CD, CD), full(1, 512), full(1, 512),
                   full(1, 512), full(1, 512)],
        out_shape=[act, jax.ShapeDtypeStruct((T, 512), BF16), jax.ShapeDtypeStruct((T, 512), BF16), act,
                   jax.ShapeDtypeStruct((CD, CD), F32), vec, vec, vec, vec],
        compiler_params=_cp(("arbitrary",), VMEM_BIG),
    )(dy, dy, o, u, u, c1, c3, ag, lg, lb, wpw, cg, seg)


def _conv_bwd(dc1, u, dw):
    tr = 64
    off = CWP - CW + 1

    def body(d_ref, cv_ref, cg_ref, w_ref, dcv_ref, dcg_ref, ddw_ref, ddb_ref, padc_s, padd_s, acc_s):
        cv = cv_ref[...]
        sg = _sig(cg_ref[...])
        padc_s[pl.ds(0, CWP), :] = jnp.zeros((CWP, LANE), F32)
        padc_s[pl.ds(CWP, T), :] = cv * sg
        padd_s[pl.ds(0, T), :] = d_ref[...]
        padd_s[pl.ds(T, CWP), :] = jnp.zeros((CWP, LANE), F32)
        acc_s[...] = jnp.zeros_like(acc_s)
        wv = w_ref[0]

        def tile(i, carry):
            r0 = pl.multiple_of(i * tr, tr)
            dt = padd_s[pl.ds(r0, tr), :]
            dc0 = jnp.zeros((tr, LANE), F32)
            for w in range(CW):
                dc0 = dc0 + padd_s[pl.ds(r0 + (CW - 1) - w, tr), :] * wv[w:w + 1, :]
                prod = dt * padc_s[pl.ds(r0 + off + w, tr), :]
                acc_s[w] += jnp.sum(prod.reshape(tr // 8, 8, LANE), axis=0)
            cvt = cv_ref[pl.ds(r0, tr), :]
            sgt = _sig(cg_ref[pl.ds(r0, tr), :])
            dcv_ref[pl.ds(r0, tr), :] = (dc0 * sgt).astype(BF16)
            dcg_ref[pl.ds(r0, tr), :] = (dc0 * cvt * sgt * (1.0 - sgt)).astype(BF16)
            return carry

        lax.fori_loop(0, T // tr, tile, 0)
        ddw_ref[0] = jnp.sum(acc_s[...], axis=1)
        ddb_ref[...] = jnp.sum(d_ref[...], axis=0, keepdims=True)

    col = lambda j: pl.BlockSpec((T, LANE), lambda cb: (0, j + cb))
    return pl.pallas_call(
        body, name="conv_bwd", grid=(CD // LANE,),
        in_specs=[col(0), col(16), col(20), pl.BlockSpec((1, CWP, LANE), lambda cb: (cb, 0, 0))],
        out_specs=[col(0), col(0), pl.BlockSpec((1, CWP, LANE), lambda cb: (cb, 0, 0)),
                   pl.BlockSpec((1, LANE), lambda cb: (0, cb))],
        out_shape=[jax.ShapeDtypeStruct((T, CD), BF16), jax.ShapeDtypeStruct((T, CD), BF16),
                   jax.ShapeDtypeStruct((NCHIP, CWP, LANE), F32), jax.ShapeDtypeStruct((1, CD), F32)],
        scratch_shapes=[pltpu.VMEM((T + CWP, LANE), F32), pltpu.VMEM((T + CWP, LANE), F32),
                        pltpu.VMEM((CWP, 8, LANE), F32)],
        compiler_params=_cp(("arbitrary",)),
    )(dc1, u, u, dw)


def _attn_bwd(u, do, tot, partials=()):
    n_x = len(partials)
    grid = (NG, T // AQ)

    def body(q_ref, k_ref, v_ref, do_ref, tot_ref, dq_ref, dk_ref, dv_ref, kb_s, vb_s, dk_s, dv_s):
        qi = pl.program_id(1)

        @pl.when(qi == 0)
        def _():
            kb_s[...] = k_ref[...].astype(BF16)
            vb_s[...] = v_ref[...].astype(BF16)
            dk_s[...] = jnp.zeros_like(dk_s)
            dv_s[...] = jnp.zeros_like(dv_s)

        causal, tr, tc, heads = _attn_tiles()
        upper = (tr > tc).astype(BF16)
        lower = (tr < tc).astype(BF16)
        q = q_ref[...]
        qs = _stack_heads(q * 0.125, heads).astype(BF16)
        qus = _stack_heads(q, heads).astype(BF16)
        dos = _stack_heads(do_ref[...], heads).astype(BF16)
        totv = tot_ref[0]
        tots = jnp.concatenate([totv[:, h * DH:h * DH + 1] for h in range(HG)], axis=0)

        def block(kb, carry, masked):
            lm_left, dl_left, dq = carry
            k0 = pl.multiple_of(kb * AQ, AQ)
            kk = kb_s[pl.ds(k0, AQ), :]
            vv = vb_s[pl.ds(k0, AQ), :]
            z = _dot_nt(qs, kk)
            sp = _softplus(z)
            lm = jnp.where(causal, -sp, 0.0) if masked else -sp
            lm_incl = lm_left + jnp.sum(lm, axis=1, keepdims=True)
            att = jnp.exp((z - sp) + _tri_sum(lm, upper) + (tots - lm_incl))
            if masked:
                att = jnp.where(causal, att, 0.0)
            dl = att * _dot_nt(dos, vv)
            dv_s[pl.ds(k0, AQ), :] += _dot_tn(att.astype(BF16), dos)
            prefix = dl_left + _tri_sum(dl, lower)
            beta = jnp.exp(z - sp)
            dz = (1.0 - beta) * dl - beta * prefix
            if masked:
                dz = jnp.where(causal, dz, 0.0)
            dzs = (dz * 0.125).astype(BF16)
            dk_s[pl.ds(k0, AQ), :] += _dot_tn(dzs, qus)
            return lm_incl, dl_left + jnp.sum(dl, axis=1, keepdims=True), dq + _dot(dzs, kk)

        zero = jnp.zeros((SR, 1), F32)
        carry = lax.fori_loop(0, qi, lambda kb, c: block(kb, c, False),
                              (zero, zero, jnp.zeros((SR, GW), F32)))
        _, _, dq = block(qi, carry, True)
        dq_ref[...] = _unstack_heads(dq, heads).astype(BF16)

        @pl.when(qi == grid[1] - 1)
        def _():
            dk_ref[...] = dk_s[...].astype(BF16)
            dv_ref[...] = dv_s[...].astype(BF16)

    res = pl.pallas_call(
        _host(body, grid, 5, 3, n_x, _scatter_copies), name="attn_bwd", grid=grid,
        in_specs=[pl.BlockSpec((AQ, GW), lambda hg, qi: (qi, hg)),
                  pl.BlockSpec((T, GW), lambda hg, qi: (0, NG + hg)),
                  pl.BlockSpec((T, GW), lambda hg, qi: (0, 2 * NG + hg)),
                  pl.BlockSpec((AQ, GW), lambda hg, qi: (qi, hg)),
                  pl.BlockSpec((1, AQ, GW), lambda hg, qi: (hg, qi, 0))] + [HBM_SPEC] * n_x,
        out_specs=[pl.BlockSpec((AQ, GW), lambda hg, qi: (qi, hg)),
                   pl.BlockSpec((T, GW), lambda hg, qi: (0, hg)),
                   pl.BlockSpec((T, GW), lambda hg, qi: (0, hg))] + [HBM_SPEC] * n_x,
        out_shape=[jax.ShapeDtypeStruct((T, AD), BF16)] * 3
        + [jax.ShapeDtypeStruct((NCHIP - 1,) + a.shape[1:], a.dtype) for a in partials],
        scratch_shapes=[pltpu.VMEM((T, GW), BF16), pltpu.VMEM((T, GW), BF16), pltpu.VMEM((T, GW), F32),
                        pltpu.VMEM((T, GW), F32)] + _hosted_sems(n_x),
        compiler_params=_cp(("arbitrary", "arbitrary"), VMEM_BIG),
    )(u, u, u, do, tot, *partials)
    return res[0], res[1], res[2], list(res[3:])


def _inproj_dw(hn, du):
    tm = min(TM, T)

    def body(hn_ref, du_ref, dw_ref):
        @pl.when(pl.program_id(1) == 0)
        def _():
            dw_ref[...] = jnp.zeros_like(dw_ref)
        dw_ref[0] += _dot_tn(hn_ref[...], du_ref[...])

    return pl.pallas_call(
        body, name="inproj_dw", grid=(NCHIP, T // tm),
        in_specs=[pl.BlockSpec((tm, D), lambda k, i: (i, 0)), pl.BlockSpec((tm, SHW), lambda k, i: (i, k))],
        out_specs=pl.BlockSpec((1, D, SHW), lambda k, i: (k, 0, 0)),
        out_shape=jax.ShapeDtypeStruct((NCHIP, D, SHW), F32),
        compiler_params=_cp(("arbitrary", "arbitrary"), VMEM_BIG),
    )(hn, du)


def _inproj_dx(du, w, h, g, dres, partials=(), grads=()):
    tm = min(TM, T)
    sent = list(partials) + list(grads)
    n_x = len(sent)
    grid = (T // tm, NCHIP)
    if grads:
        landing = [jax.ShapeDtypeStruct((NCHIP, a.shape[1] // 2, a.shape[2]), F32) for a in grads]
    else:
        landing = [jax.ShapeDtypeStruct((NCHIP - 1,) + a.shape[1:], a.dtype) for a in partials]

    def body(du_ref, w_ref, h_ref, g_ref, dres_ref, dh_ref, dg_ref, acc_s):
        i, k = pl.program_id(0), pl.program_id(1)

        @pl.when(jnp.logical_and(i == 0, k == 0))
        def _():
            dg_ref[...] = jnp.zeros_like(dg_ref)

        @pl.when(k == 0)
        def _():
            acc_s[...] = _dot_nt(du_ref[...], w_ref[0])

        @pl.when(k > 0)
        def _():
            acc_s[...] += _dot_nt(du_ref[...], w_ref[0])

        @pl.when(k == NCHIP - 1)
        def _():
            hh = h_ref[...]
            r = _rstd(hh)
            dhn = acc_s[...]
            dg_ref[...] += jnp.sum(dhn * hh * r, axis=0, keepdims=True)
            dh_ref[...] = dres_ref[...] + _rms_bwd(dhn, hh, r, g_ref[...])

    res = pl.pallas_call(
        _host(body, grid, 5, 2, n_x, _pair_copies if grads else _scatter_copies), name="inproj_dx", grid=grid,
        in_specs=[pl.BlockSpec((tm, SHW), lambda i, k: (i, k)),
                  pl.BlockSpec((1, D, SHW), lambda i, k: (k, 0, 0)),
                  pl.BlockSpec((tm, D), lambda i, k: (i, 0)),
                  pl.BlockSpec((1, D), lambda i, k: (0, 0)),
                  pl.BlockSpec((tm, D), lambda i, k: (i, 0))] + [HBM_SPEC] * n_x,
        out_specs=[pl.BlockSpec((tm, D), lambda i, k: (i, 0)), pl.BlockSpec((1, D), lambda i, k: (0, 0))]
        + [HBM_SPEC] * n_x,
        out_shape=[jax.ShapeDtypeStruct((T, D), F32), jax.ShapeDtypeStruct((1, D), F32)] + landing,
        scratch_shapes=[pltpu.VMEM((tm, D), F32)] + _hosted_sems(n_x),
        compiler_params=_cp(("arbitrary", "arbitrary"), VMEM_BIG),
    )(du, w, h, g, dres, *sent)
    return res[0], res[1], list(res[2:])


def _sum_pair(core, grads, gots):
    n = len(grads)

    def body(c_ref, *refs):
        for a in range(n):
            refs[2 * n + a][...] = (refs[a][...] + refs[n + a][...]).astype(BF16)

    mine = [pl.BlockSpec((1,) + s.shape[1:], lambda k, c: (k, c[0], 0)) for s in gots]
    same = [pl.BlockSpec((1,) + s.shape[1:], lambda k, c: (k, 0, 0)) for s in gots]
    return pl.pallas_call(
        body, name="sum_pair",
        grid_spec=pltpu.PrefetchScalarGridSpec(
            num_scalar_prefetch=1, grid=(NCHIP,), in_specs=mine + same, out_specs=same),
        out_shape=[jax.ShapeDtypeStruct(s.shape, BF16) for s in gots],
        compiler_params=_cp(("arbitrary",), VMEM_BIG),
    )(core, *grads, *gots)


def _sum_chips_share(chip, partials, gots):
    n = len(partials)

    def body(c_ref, *refs):
        mine, theirs = refs[2 * n:3 * n], refs[3 * n:4 * n]
        for a in range(n):
            acc = refs[a][0].astype(F32)
            for j in range(NCHIP - 1):
                acc = acc + refs[n + a][j].astype(F32)
            mine[a][...] = acc
        x, y, c = _place()
        copies = [pltpu.make_async_remote_copy(
            src_ref=mine[a], dst_ref=theirs[a], send_sem=refs[4 * n].at[a], recv_sem=refs[4 * n + 1].at[a],
            device_id=(x, y, 1 - c), device_id_type=MESH) for a in range(n)]
        for cp in copies:
            cp.start()
        for cp in copies:
            cp.wait()

    halves = [jax.ShapeDtypeStruct(s.shape[1:], F32) for s in partials]
    res = pl.pallas_call(
        body, name="sum_chips_share",
        grid_spec=pltpu.PrefetchScalarGridSpec(
            num_scalar_prefetch=1, grid=(1,),
            in_specs=[pl.BlockSpec((1,) + s.shape[1:], lambda i, c: (c[0], 0, 0)) for s in partials]
            + [pl.BlockSpec(s.shape, lambda i, c: (0, 0, 0)) for s in gots],
            out_specs=[pl.BlockSpec(s.shape[1:], lambda i, c: (0, 0)) for s in partials] + [HBM_SPEC] * n,
            scratch_shapes=[pltpu.SemaphoreType.DMA((n,)), pltpu.SemaphoreType.DMA((n,))]),
        out_shape=halves + halves,
        compiler_params=_cp(("arbitrary",), VMEM_BIG),
    )(chip, *partials, *gots)
    return res[:n], res[n:]


def _adamw(w, g, m, v, rows):
    R, C = w.shape
    c1 = 1.0 - ADAM_B1 ** ADAM_STEP
    c2 = 1.0 - ADAM_B2 ** ADAM_STEP

    def body(w_ref, g_ref, m_ref, v_ref, d_ref, nm_ref, nv_ref):
        gg = g_ref[...]
        nm = ADAM_B1 * m_ref[...] + (1.0 - ADAM_B1) * gg
        nv = ADAM_B2 * v_ref[...] + (1.0 - ADAM_B2) * (gg * gg)
        d_ref[...] = -ADAM_LR * ((nm / c1) / (jnp.sqrt(nv / c2) + ADAM_EPS) + ADAM_WD * w_ref[...])
        nm_ref[...] = nm
        nv_ref[...] = nv

    spec = pl.BlockSpec((rows, C), lambda i: (i, 0))
    sh = jax.ShapeDtypeStruct((R, C), F32)
    return pl.pallas_call(
        body, name="adamw", grid=(R // rows,), in_specs=[spec] * 4, out_specs=[spec] * 3,
        out_shape=[sh, sh, sh], compiler_params=_cp(("arbitrary",)),
    )(w, g, m, v)


HBM_SPEC = pl.BlockSpec(memory_space=pltpu.HBM)


def _place():
    return lax.axis_index("x"), lax.axis_index("y"), lax.axis_index("c")


def _all_gather_split(shard):
    def body(in_ref, out_ref, send_sems, recv_sems):
        fetch, passed = _gather_split_copies([in_ref], [out_ref], send_sems, recv_sems)
        for cp in fetch:
            cp.start()
        for got, onward in zip(fetch, passed):
            got.wait_recv()
            onward.start()
        for cp in passed:
            cp.wait_recv()
        for cp in fetch + passed:
            cp.wait_send()

    return pl.pallas_call(
        body, name="all_gather_split", in_specs=[HBM_SPEC], out_specs=HBM_SPEC,
        out_shape=jax.ShapeDtypeStruct((NCHIP,) + shard.shape, shard.dtype),
        scratch_shapes=[pltpu.SemaphoreType.DMA((6,)), pltpu.SemaphoreType.DMA((6,))],
    )(shard)


def _pair_exchange(grads):
    n = len(grads)

    def body(*refs):
        copies = _pair_copies(refs[:n], refs[n:2 * n], refs[2 * n], refs[2 * n + 1])
        for cp in copies:
            cp.start()
        for cp in copies:
            cp.wait()

    return pl.pallas_call(
        body, name="pair_exchange", in_specs=[HBM_SPEC] * n, out_specs=[HBM_SPEC] * n,
        out_shape=[jax.ShapeDtypeStruct((NCHIP, g.shape[1] // 2, g.shape[2]), F32) for g in grads],
        scratch_shapes=[pltpu.SemaphoreType.DMA((n,)), pltpu.SemaphoreType.DMA((n,))],
    )(*grads)


def _small_allreduce(mine):
    def body(m_ref, o_ref, slots, send_sems, recv_sems):
        x, y, c = _place()
        me = 4 * x + 2 * y + c
        slots[me] = m_ref[...]
        copies = []
        for r in range(1, 8):
            rx, ry, rc = (r >> 2) & 1, (r >> 1) & 1, r & 1
            peer = (x + rx - 2 * x * rx, y + ry - 2 * y * ry, c + rc - 2 * c * rc)
            cp = pltpu.make_async_remote_copy(
                src_ref=m_ref, dst_ref=slots.at[me], send_sem=send_sems.at[r - 1], recv_sem=recv_sems.at[r - 1],
                device_id=peer, device_id_type=MESH)
            cp.start()
            copies.append(cp)
        for cp in copies:
            cp.wait()
        acc = slots[0]
        for j in range(1, 8):
            acc = acc + slots[j]
        o_ref[...] = acc

    return pl.pallas_call(
        body, name="small_allreduce",
        in_specs=[pl.BlockSpec(memory_space=pltpu.VMEM)], out_specs=pl.BlockSpec(memory_space=pltpu.VMEM),
        out_shape=jax.ShapeDtypeStruct((SMALL_ROWS, LANE), F32),
        scratch_shapes=[pltpu.VMEM((8, SMALL_ROWS, LANE), F32), pltpu.SemaphoreType.DMA((7,)),
                        pltpu.SemaphoreType.DMA((7,))],
    )(mine)


def _seg_matrix():
    i = lax.broadcasted_iota(jnp.int32, (AD, AD), 0) // DH
    j = lax.broadcasted_iota(jnp.int32, (AD, AD), 1) // DH
    return (i == j).astype(BF16)


TAIL = ("w_out", "w_ple_gate", "w_ple", "w_pw", "dw_w")


def _local_step(x, p, tgt, sm, shards, chip, ci):
    seg = _seg_matrix()
    core = jnp.reshape(ci, (1,)).astype(jnp.int32)
    chip_idx = jnp.reshape(chip, (1,)).astype(jnp.int32)
    own = lambda g, s: lax.dynamic_update_index_in_dim(g, s, chip, 0)
    w_in_next = own(_all_gather_split(shards[0]["w_in"]), shards[0]["w_in"])
    h = x
    saved = []
    for l in range(DEPTH):
        w_in = w_in_next
        row = lambda name: sm[name][l:l + 1]
        u, hn = _rms_inproj(h, row("norm_g"), w_in)
        todo = [shards[l][k] for k in TAIL] + ([shards[l + 1]["w_in"]] if l + 1 < DEPTH else [])
        o, ya, tot, got = _attn_fwd(u, jnp.tile(row("attn_out_g"), (1, HG)), todo)
        got = [own(g, s) for g, s in zip(got, todo)]
        w_out = got[0].reshape(D, D)
        w_gate = got[1].reshape(D, D)
        w_ple = got[2]
        w_pw = got[3].reshape(CD, CD)
        dw = got[4]
        if l + 1 < DEPTH:
            w_in_next = got[5]
        c1 = _glu_conv(u, dw, row("dw_b"))
        c3, yc, h1, gate, pe, h2 = _layer_tail(
            c1, u, ya, h, p[l], row("conv_ln_g"), row("conv_ln_b"), w_pw, row("conv_out_g"), w_out,
            row("ple_norm_g"), w_gate, w_ple)
        saved.append(dict(h=h, u=u, hn=hn, o=o, ya=ya, tot=tot, c1=c1, c3=c3, yc=yc, h1=h1, gate=gate, pe=pe,
                          w_in=w_in, w_out=w_out, w_gate=w_gate, w_pw=w_pw, dw=dw))
        h = h2
    loss_blk, dh, dfg = _loss_head(h, tgt, sm["final_g"])
    small = [None] * DEPTH
    pending, partials, arrived = [], {}, {}
    pair_sum = lambda grads: _sum_pair(core, grads, _pair_exchange(grads))
    for l in reversed(range(DEPTH)):
        s = saved[l]
        row = lambda name: sm[name][l:l + 1]
        dh1, dy, dwg, dwp, dwo, dpg = _ple_out_bwd(
            dh, s["h1"], s["gate"], s["pe"], p[l], s["ya"], s["yc"], row("ple_norm_g"), s["w_gate"], s["w_out"])
        ag_t = jnp.tile(row("attn_out_g"), (1, AD // DH))
        do, dga, dgc, dc1, dwpw, dag, dcg, dlg, dlb = _branch_bwd(
            dy, s["o"], s["u"], s["c1"], s["c3"], ag_t, row("conv_ln_g"), row("conv_ln_b"), s["w_pw"],
            row("conv_out_g"), seg)
        dcv, dcgate, ddw, ddb = _conv_bwd(dc1, s["u"], s["dw"])
        tail = [dwo.reshape(NCHIP, 256, D), dwg.reshape(NCHIP, 256, D), dwp, dwpw.reshape(NCHIP, 128, CD), ddw]
        if l == 0:
            partials[(l, "tail")] = pair_sum(tail)
            pending.append((l, "tail"))
        send = [t for key in pending for t in partials[key]]
        dq, dk, dv, got = _attn_bwd(s["u"], do, s["tot"], send)
        for key in pending:
            arrived[key], got = got[:len(partials[key])], got[len(partials[key]):]
        pending = []
        du = jnp.concatenate([dq, dk, dv, dga, dcv, dcgate, dgc], axis=1)
        dwin = _inproj_dw(s["hn"], du)
        if l == 0:
            partials[(l, "w_in")] = pair_sum([dwin])
            dh, dng, arrived[(l, "w_in")] = _inproj_dx(du, s["w_in"], s["h"], row("norm_g"), dh1,
                                                      partials[(l, "w_in")])
        else:
            dh, dng, halves = _inproj_dx(du, s["w_in"], s["h"], row("norm_g"), dh1, grads=tail + [dwin])
            tail_p = _sum_pair(core, tail + [dwin], halves)
            partials[(l, "tail")], partials[(l, "w_in")] = tail_p[:-1], tail_p[-1:]
            pending = [(l, "tail"), (l, "w_in")]
        small[l] = dict(norm_g=dng, attn_out_g=dag.reshape(AD // DH, DH).sum(axis=0, keepdims=True), dw_b=ddb,
                        conv_ln_g=dlg, conv_ln_b=dlb, conv_out_g=dcg, ple_norm_g=dpg)
    big = []
    for l in range(DEPTH):
        ps = partials[(l, "w_in")] + partials[(l, "tail")]
        mine, theirs = _sum_chips_share(chip_idx, ps, arrived[(l, "w_in")] + arrived[(l, "tail")])
        big.append(dict(zip(BIG, [jnp.where(core[0] == 0, jnp.concatenate([m, t]), jnp.concatenate([t, m]))
                                  for m, t in zip(mine, theirs)])))
    return loss_blk[0, 0], dh, big, small, dfg


BIG = ("w_in", "w_out", "w_ple_gate", "w_ple", "w_pw", "dw_w")
SMALL2 = ("norm_g", "ple_norm_g", "dw_b", "conv_ln_g", "conv_ln_b", "conv_out_g", "attn_out_g")


def _pack_small(two, final):
    flat = jnp.concatenate([two[k].reshape(-1) for k in SMALL2] + [final.reshape(-1)])
    flat = jnp.concatenate([flat, jnp.zeros((SMALL_ROWS * LANE - flat.shape[0],), F32)])
    return flat.reshape(SMALL_ROWS, LANE)


def _unpack_small(packed, like_two, like_final):
    flat = packed.reshape(-1)
    out, off = {}, 0
    for k in SMALL2:
        n = like_two[k].size
        out[k] = flat[off:off + n].reshape(like_two[k].shape)
        off += n
    return out, flat[off:off + like_final.size].reshape(like_final.shape)


def kernel(x, p, norm_g, w_in, attn_out_g, dw_w, dw_b, conv_ln_g, conv_ln_b, w_pw, conv_out_g, w_out, ple_norm_g, w_ple_gate, w_ple, final_g, loss_target, m_norm_g, m_w_in, m_attn_out_g, m_dw_w, m_dw_b, m_conv_ln_g, m_conv_ln_b, m_w_pw, m_conv_out_g, m_w_out, m_ple_norm_g, m_w_ple_gate, m_w_ple, m_final_g, v_norm_g, v_w_in, v_attn_out_g, v_dw_w, v_dw_b, v_conv_ln_g, v_conv_ln_b, v_w_pw, v_conv_out_g, v_w_out, v_ple_norm_g, v_w_ple_gate, v_w_ple, v_final_g):
    W = dict(norm_g=norm_g, w_in=w_in, attn_out_g=attn_out_g, dw_w=dw_w, dw_b=dw_b, conv_ln_g=conv_ln_g,
             conv_ln_b=conv_ln_b, w_pw=w_pw, conv_out_g=conv_out_g, w_out=w_out, ple_norm_g=ple_norm_g,
             w_ple_gate=w_ple_gate, w_ple=w_ple, final_g=final_g)
    M = dict(norm_g=m_norm_g, w_in=m_w_in, attn_out_g=m_attn_out_g, dw_w=m_dw_w, dw_b=m_dw_b,
             conv_ln_g=m_conv_ln_g, conv_ln_b=m_conv_ln_b, w_pw=m_w_pw, conv_out_g=m_conv_out_g, w_out=m_w_out,
             ple_norm_g=m_ple_norm_g, w_ple_gate=m_w_ple_gate, w_ple=m_w_ple, final_g=m_final_g)
    V = dict(norm_g=v_norm_g, w_in=v_w_in, attn_out_g=v_attn_out_g, dw_w=v_dw_w, dw_b=v_dw_b,
             conv_ln_g=v_conv_ln_g, conv_ln_b=v_conv_ln_b, w_pw=v_w_pw, conv_out_g=v_conv_out_g, w_out=v_w_out,
             ple_norm_g=v_ple_norm_g, w_ple_gate=v_w_ple_gate, w_ple=v_w_ple, final_g=v_final_g)
    order = ("norm_g", "w_in", "attn_out_g", "dw_w", "dw_b", "conv_ln_g", "conv_ln_b", "w_pw", "conv_out_g",
             "w_out", "ple_norm_g", "w_ple_gate", "w_ple", "final_g")

    pad_taps = lambda a: jnp.pad(a, ((0, 0), (0, CWP - CW), (0, 0)))
    cast = dict(w_in=w_in.astype(BF16), w_out=w_out.astype(BF16), w_ple_gate=w_ple_gate.astype(BF16),
                w_ple=w_ple.astype(BF16), w_pw=w_pw.astype(BF16), dw_w=pad_taps(dw_w))
    shards = [{k: v[l] for k, v in cast.items()} for l in range(DEPTH)]
    xi, yi, ci = lax.axis_index("x"), lax.axis_index("y"), lax.axis_index("c")
    chip = 2 * xi + yi

    sm = {k: W[k] for k in SMALL2}
    sm["final_g"] = final_g.reshape(1, D)
    loss_part, grad_x, big, small, dfg = _local_step(x[0], p[:, 0], loss_target[0], sm, shards, chip, ci)
    g_big = {name: jnp.stack([big[l][name] for l in range(DEPTH)]).reshape(cast[name].shape) for name in BIG}

    small_two = {k: jnp.concatenate([small[l][k] for l in range(DEPTH)], axis=0) for k in SMALL2}
    g_small_packed = _small_allreduce(_pack_small(small_two, dfg).at[SMALL_ROWS - 1, LANE - 1].set(loss_part))
    loss = g_small_packed[SMALL_ROWS - 1, LANE - 1]
    g_small, g_final = _unpack_small(g_small_packed, {k: W[k] for k in SMALL2}, final_g)

    grads, deltas, new_m, new_v = {}, {}, {}, {}
    for name in BIG:
        wv = pad_taps(W[name]) if name == "dw_w" else W[name]
        mv = pad_taps(M[name]) if name == "dw_w" else M[name]
        vv = pad_taps(V[name]) if name == "dw_w" else V[name]
        gg = g_big[name]
        cols = wv.shape[-1]
        rows_total = wv.size // cols
        tile_rows = min(rows_total, 256)
        d2, m2, v2 = _adamw(wv.reshape(rows_total, cols), gg.reshape(rows_total, cols),
                            mv.reshape(rows_total, cols), vv.reshape(rows_total, cols), tile_rows)
        if name == "dw_w":
            cut = lambda a: a.reshape(DEPTH, CWP, LANE)[:, :CW]
            grads[name], deltas[name], new_m[name], new_v[name] = cut(gg), cut(d2), cut(m2), cut(v2)
        else:
            grads[name] = gg
            deltas[name], new_m[name], new_v[name] = (t.reshape(wv.shape) for t in (d2, m2, v2))
    ws = _pack_small({k: W[k] for k in SMALL2}, final_g)
    ms = _pack_small({k: M[k] for k in SMALL2}, m_final_g)
    vs = _pack_small({k: V[k] for k in SMALL2}, v_final_g)
    ds, nms, nvs = _adamw(ws, g_small_packed, ms, vs, SMALL_ROWS)
    for packed, dst in ((ds, deltas), (nms, new_m), (nvs, new_v)):
        two, fin = _unpack_small(packed, {k: W[k] for k in SMALL2}, final_g)
        dst.update(two)
        dst["final_g"] = fin
    grads.update(g_small)
    grads["final_g"] = g_final

    return (loss, grad_x[None], *[grads[n] for n in order], *[deltas[n] for n in order],
            *[new_m[n] for n in order], *[new_v[n] for n in order])
```

```python
import functools

import jax
import jax.numpy as jnp
from jax import lax
from jax.experimental import pallas as pl
from jax.experimental.pallas import tpu as pltpu

F32 = jnp.float32
BF16 = jnp.bfloat16

T = 2048
D = 1024
DIN = 3584
NCHIP = 4
SHW = DIN // NCHIP
AD = 512
CD = 512
DH = 64
CW = 31
CWP = 32
PLE = 256
DEPTH = 2
EPS = 1e-6
AQ = 256
HG = 4
GW = HG * DH
SR = HG * AQ
NG = AD // GW
LANE = 128
TM = 1024

ADAM_LR = 0.001
ADAM_B1 = 0.9
ADAM_B2 = 0.999
ADAM_EPS = 1e-08
ADAM_WD = 0.01
ADAM_STEP = 10

SMALL_ROWS = 80

VMEM_BIG = 56 * 1024 * 1024
MESH = pl.DeviceIdType.MESH


def _cp(sem=None, vmem=None):
    kw = {}
    if sem is not None:
        kw["dimension_semantics"] = sem
    if vmem is not None:
        kw["vmem_limit_bytes"] = vmem
    return pltpu.CompilerParams(**kw)


def _dot(a, b):
    return jnp.dot(a, b, preferred_element_type=F32)


def _dot_nt(a, b):
    return lax.dot_general(a, b, (((1,), (1,)), ((), ())), preferred_element_type=F32)


def _dot_tn(a, b):
    return lax.dot_general(a, b, (((0,), (0,)), ((), ())), preferred_element_type=F32)


def _dot2(x, m):
    hi = x.astype(BF16)
    lo = (x - hi.astype(F32)).astype(BF16)
    return _dot(hi, m) + _dot(lo, m)


def _sig(x):
    return 1.0 / (1.0 + jnp.exp(-x))


def _softplus(z):
    return jnp.maximum(z, 0.0) + jnp.log(1.0 + jnp.exp(-jnp.abs(z)))


def _rstd(x):
    return lax.rsqrt(jnp.mean(x * x, axis=-1, keepdims=True) + EPS)


def _rms_bwd(dy, x, r, g):
    dn = dy * g
    return r * dn - x * (r * r * r) * jnp.mean(dn * x, axis=-1, keepdims=True)


def _rms_inproj(h, g, w):
    tm = min(TM, T)

    def body(h_ref, g_ref, w_ref, u_ref, hn_ref, hn_s):
        @pl.when(pl.program_id(1) == 0)
        def _():
            hh = h_ref[...]
            hn = (hh * _rstd(hh) * g_ref[...]).astype(BF16)
            hn_s[...] = hn
            hn_ref[...] = hn
        u_ref[...] = _dot(hn_s[...], w_ref[0])

    return pl.pallas_call(
        body, name="rms_inproj", grid=(T // tm, NCHIP),
        in_specs=[pl.BlockSpec((tm, D), lambda i, k: (i, 0)),
                  pl.BlockSpec((1, D), lambda i, k: (0, 0)),
                  pl.BlockSpec((1, D, SHW), lambda i, k: (k, 0, 0))],
        out_specs=[pl.BlockSpec((tm, SHW), lambda i, k: (i, k)),
                   pl.BlockSpec((tm, D), lambda i, k: (i, 0))],
        out_shape=[jax.ShapeDtypeStruct((T, DIN), F32), jax.ShapeDtypeStruct((T, D), BF16)],
        scratch_shapes=[pltpu.VMEM((tm, D), BF16)],
        compiler_params=_cp(("arbitrary", "arbitrary"), VMEM_BIG),
    )(h, g, w)


def _attn_tiles():
    row = lax.broadcasted_iota(jnp.int32, (SR, AQ), 0) & (AQ - 1)
    col = lax.broadcasted_iota(jnp.int32, (SR, AQ), 1)
    tr = lax.broadcasted_iota(jnp.int32, (AQ, AQ), 0)
    tc = lax.broadcasted_iota(jnp.int32, (AQ, AQ), 1)
    lane_head = lax.broadcasted_iota(jnp.int32, (1, GW), 1) // DH
    return col < row, tr, tc, [lane_head == h for h in range(HG)]


def _stack_heads(t, heads):
    return jnp.concatenate([jnp.where(m, t, 0.0) for m in heads], axis=0)


def _unstack_heads(t, heads):
    out = t[:AQ]
    for h in range(1, HG):
        out = jnp.where(heads[h], t[h * AQ:(h + 1) * AQ], out)
    return out


def _tri_sum(x, tri):
    hi = x.astype(BF16)
    lo = (x - hi.astype(F32)).astype(BF16)
    both = _dot(jnp.concatenate([hi, lo], axis=0), tri)
    return both[:SR] + both[SR:]


def _scatter_copies(ps, gots, send_sems, recv_sems):
    x, y, c = _place()
    peers = [(1 - x, y), (x, 1 - y), (1 - x, 1 - y)]
    return [pltpu.make_async_remote_copy(
        src_ref=ps[a].at[2 * px + py], dst_ref=gots[a].at[r], send_sem=send_sems.at[3 * a + r],
        recv_sem=recv_sems.at[3 * a + r], device_id=(px, py, c), device_id_type=MESH)
        for a in range(len(ps)) for r, (px, py) in enumerate(peers)]


def _gather_split_copies(ins, outs, send_sems, recv_sems):
    x, y, c = _place()
    me = 2 * x + y
    chips = [(1 - x, y), (x, 1 - y), (1 - x, 1 - y)]
    fetch, passed = [], []
    for a in range(len(ins)):
        half = ins[a].shape[0] // 2
        mine = pl.ds(c * half, half)
        for r, (px, py) in enumerate(chips):
            fetch.append(pltpu.make_async_remote_copy(
                src_ref=ins[a].at[mine], dst_ref=outs[a].at[me, mine], send_sem=send_sems.at[6 * a + r],
                recv_sem=recv_sems.at[6 * a + r], device_id=(px, py, c), device_id_type=MESH))
            landed = outs[a].at[2 * px + py, mine]
            passed.append(pltpu.make_async_remote_copy(
                src_ref=landed, dst_ref=landed, send_sem=send_sems.at[6 * a + 3 + r],
                recv_sem=recv_sems.at[6 * a + 3 + r], device_id=(x, y, 1 - c), device_id_type=MESH))
    return fetch, passed


def _pair_copies(ins, outs, send_sems, recv_sems):
    x, y, c = _place()
    copies = []
    for a in range(len(ins)):
        half = ins[a].shape[1] // 2
        copies.append(pltpu.make_async_remote_copy(
            src_ref=ins[a].at[:, pl.ds((1 - c) * half, half), :], dst_ref=outs[a], send_sem=send_sems.at[a],
            recv_sem=recv_sems.at[a], device_id=(x, y, 1 - c), device_id_type=MESH))
    return copies


def _host(body, grid, n_in, n_out, n_x, make_copies, mid=None):
    if not n_x:
        return body

    def hosting(*refs):
        a, b = n_in + n_x, n_in + 2 * n_x + n_out
        copies = make_copies(refs[n_in:a], refs[a + n_out:b], refs[-2], refs[-1])
        stages = copies if isinstance(copies, tuple) else (copies,)
        ids = [pl.program_id(d) for d in range(len(grid))]
        at = lambda step: functools.reduce(jnp.logical_and, [i == s for i, s in zip(ids, step)])

        @pl.when(at([0] * len(grid)))
        def _():
            for cp in stages[0]:
                cp.start()

        if len(stages) == 2:
            @pl.when(at(mid))
            def _():
                for cp in stages[0]:
                    cp.wait_recv()
                for cp in stages[1]:
                    cp.start()

        body(*refs[:n_in], *refs[a:a + n_out], *refs[b:-2])

        @pl.when(at([g - 1 for g in grid]))
        def _():
            if len(stages) == 2:
                for cp in stages[1]:
                    cp.wait_recv()
                for cp in stages[0] + stages[1]:
                    cp.wait_send()
            else:
                for cp in stages[0]:
                    cp.wait()

    return hosting


def _hosted_sems(n_x, per_array=3):
    n = per_array * n_x
    return [pltpu.SemaphoreType.DMA((n,)), pltpu.SemaphoreType.DMA((n,))] if n_x else []


RC = 256


def _chunk_causal(r):
    row = lax.broadcasted_iota(jnp.int32, (RC, AQ), 0) + (r * RC) % AQ
    return lax.broadcasted_iota(jnp.int32, (RC, AQ), 1) < row


def _attn_fwd(u, agw, shards=()):
    n = len(shards)
    grid = (T // AQ,)

    def body(q_ref, k_ref, v_ref, g_ref, ag_ref, o_ref, y_ref, tot_ref,
             kb_s, vb_s, qs_s, z_s, zs_s, lmb_s, suf_s, att_s, acc_s, run_s):
        qi = pl.program_id(0)

        @pl.when(qi == 0)
        def _():
            kb_s[...] = k_ref[...].astype(BF16)
            vb_s[...] = v_ref[...].astype(BF16)

        _, tr, tc, heads = _attn_tiles()
        upper = (tr > tc).astype(BF16)
        for g in range(NG):
            qs_s[g] = _stack_heads(q_ref[:, g * GW:(g + 1) * GW] * 0.125, heads).astype(BF16)
        acc_s[...] = jnp.zeros_like(acc_s)
        run_s[...] = jnp.zeros_like(run_s)

        def block(kb, masked):
            k0 = pl.multiple_of(kb * AQ, AQ)
            for g in range(NG):
                lanes = pl.ds(g * GW, GW)
                z_s[g] = _dot_nt(qs_s[g], kb_s[pl.ds(k0, AQ), lanes])
                for r in range(SR // RC):
                    rows = pl.ds(r * RC, RC)
                    z = z_s[g, rows, :]
                    zs = jnp.minimum(z, 0.0) - jnp.log(1.0 + jnp.exp(-jnp.abs(z)))
                    lm = zs - z
                    if masked:
                        lm = jnp.where(_chunk_causal(r), lm, 0.0)
                    run = run_s[g, rows, :]
                    zs_s[g, rows, :] = zs + run[:, 0:1]
                    hi = lm.astype(BF16)
                    lmb_s[g, rows, :] = hi
                    lmb_s[g, pl.ds(SR + r * RC, RC), :] = (lm - hi.astype(F32)).astype(BF16)
                    run_s[g, rows, :] = run + jnp.sum(lm, axis=1, keepdims=True)
                suf_s[g] = _dot(lmb_s[g], upper)
                for r in range(SR // RC):
                    rows = pl.ds(r * RC, RC)
                    att = jnp.exp(zs_s[g, rows, :] + suf_s[g, rows, :] + suf_s[g, pl.ds(SR + r * RC, RC), :])
                    if masked:
                        att = jnp.where(_chunk_causal(r), att, 0.0)
                    att_s[g, rows, :] = att.astype(BF16)
                acc_s[g] += _dot(att_s[g], vb_s[pl.ds(k0, AQ), lanes])

        block(qi, True)

        def step(i, c):
            block(qi - 1 - i, False)
            return c

        lax.fori_loop(0, qi, step, 0)
        gate = g_ref[...]
        agv = ag_ref[...]
        for g in range(NG):
            lanes = slice(g * GW, (g + 1) * GW)
            o = _unstack_heads(acc_s[g], heads)
            osq = o * o
            ms = jnp.zeros((AQ, GW), F32)
            for m in heads:
                ms = jnp.where(m, jnp.sum(jnp.where(m, osq, 0.0), axis=1, keepdims=True), ms)
            gg = gate[:, lanes]
            o_ref[:, lanes] = o
            y_ref[:, lanes] = (o * lax.rsqrt(ms * (1.0 / DH) + EPS) * agv[:, lanes] * (gg * _sig(gg))).astype(BF16)
            tot_ref[g] = _unstack_heads(jnp.broadcast_to(run_s[g][:, 0:1], (SR, GW)), heads)

    tile = lambda dt, rows=SR: pltpu.VMEM((NG, rows, AQ), dt)
    scratch = [pltpu.VMEM((T, AD), BF16), pltpu.VMEM((T, AD), BF16), pltpu.VMEM((NG, SR, GW), BF16),
               tile(F32), tile(F32), tile(BF16, 2 * SR), tile(F32, 2 * SR), tile(BF16),
               pltpu.VMEM((NG, SR, GW), F32), pltpu.VMEM((NG, SR, LANE), F32)]
    col = lambda j: pl.BlockSpec((AQ, AD), lambda qi: (qi, j))
    res = pl.pallas_call(
        _host(body, grid, 5, 3, n, _gather_split_copies, mid=(grid[0] * 3 // 4,)), name="attn_fwd", grid=grid,
        in_specs=[col(0), pl.BlockSpec((T, AD), lambda qi: (0, 1)), pl.BlockSpec((T, AD), lambda qi: (0, 2)),
                  col(3), pl.BlockSpec((1, AD), lambda qi: (0, 0))] + [HBM_SPEC] * n,
        out_specs=[col(0), col(0), pl.BlockSpec((NG, AQ, GW), lambda qi: (0, qi, 0))] + [HBM_SPEC] * n,
        out_shape=[jax.ShapeDtypeStruct((T, AD), F32), jax.ShapeDtypeStruct((T, AD), BF16),
                   jax.ShapeDtypeStruct((NG, T, GW), F32)]
        + [jax.ShapeDtypeStruct((NCHIP,) + s.shape, s.dtype) for s in shards],
        scratch_shapes=scratch + _hosted_sems(n, 6),
        compiler_params=_cp(("arbitrary",), VMEM_BIG),
    )(u, u, u, u, agw, *shards)
    return res[0], res[1], res[2], list(res[3:])


def _glu_conv(u, dw, db):
    tr = 256

    def body(cv_ref, cg_ref, w_ref, b_ref, c1_ref, pad_s):
        pad_s[pl.ds(0, CWP), :] = jnp.zeros((CWP, LANE), F32)
        pad_s[pl.ds(CWP, T), :] = cv_ref[...] * _sig(cg_ref[...])
        wv = w_ref[0]
        bias = b_ref[...]

        def tile(i, carry):
            r0 = pl.multiple_of(i * tr, tr)
            acc = jnp.zeros((tr, LANE), F32) + bias
            for w in range(CW):
                acc = acc + pad_s[pl.ds(r0 + (CWP - CW + 1) + w, tr), :] * wv[w:w + 1, :]
            c1_ref[pl.ds(r0, tr), :] = acc
            return carry

        lax.fori_loop(0, T // tr, tile, 0)

    return pl.pallas_call(
        body, name="glu_conv", grid=(CD // LANE,),
        in_specs=[pl.BlockSpec((T, LANE), lambda cb: (0, 16 + cb)),
                  pl.BlockSpec((T, LANE), lambda cb: (0, 20 + cb)),
                  pl.BlockSpec((1, CWP, LANE), lambda cb: (cb, 0, 0)),
                  pl.BlockSpec((1, LANE), lambda cb: (0, cb))],
        out_specs=pl.BlockSpec((T, LANE), lambda cb: (0, cb)),
        out_shape=jax.ShapeDtypeStruct((T, CD), F32),
        scratch_shapes=[pltpu.VMEM((T + CWP, LANE), F32)],
        compiler_params=_cp(("arbitrary",)),
    )(u, u, dw, db)


def _ln_silu(c1, lg, lb):
    mu = jnp.mean(c1, axis=-1, keepdims=True)
    xc = c1 - mu
    rs = lax.rsqrt(jnp.mean(xc * xc, axis=-1, keepdims=True) + EPS)
    xh = xc * rs
    ln = xh * lg + lb
    s = _sig(ln)
    return xh, rs, ln, s


def _layer_tail(c1, u, ya, h, p, lg, lb, wpw, cg, wout, pg, wgate, wple):
    tm = 256

    def body(c1_ref, gc_ref, ya_ref, h_ref, p_ref, lg_ref, lb_ref, wpw_ref, cg_ref, wout_ref,
             pg_ref, wgate_ref, wple_ref, c3_ref, yc_ref, h1_ref, gate_ref, pe_ref, h2_ref):
        _, _, ln, s = _ln_silu(c1_ref[...], lg_ref[...], lb_ref[...])
        c2 = (ln * s).astype(BF16)
        c3 = _dot(c2, wpw_ref[...])
        gc = gc_ref[...]
        yc = (c3 * _rstd(c3) * cg_ref[...] * (gc * _sig(gc))).astype(BF16)
        c3_ref[...] = c3
        yc_ref[...] = yc
        y = _dot(ya_ref[...], wout_ref[pl.ds(0, AD), :]) + _dot(yc, wout_ref[pl.ds(AD, CD), :])
        h1 = h_ref[...] + y
        hn2 = (h1 * _rstd(h1) * pg_ref[...]).astype(BF16)
        gate = _sig(_dot(hn2, wgate_ref[...]))
        pb = p_ref[...].astype(BF16)
        pe = jnp.concatenate([_dot(pb, wple_ref[k]) for k in range(NCHIP)], axis=1)
        h1_ref[...] = h1
        gate_ref[...] = gate.astype(BF16)
        pe_ref[...] = pe.astype(BF16)
        h2_ref[...] = h1 + pe * gate

    row = lambda w: pl.BlockSpec((tm, w), lambda i: (i, 0))
    full = lambda *s: pl.BlockSpec(s, lambda i: (0,) * len(s))
    return pl.pallas_call(
        body, name="layer_tail", grid=(T // tm,),
        in_specs=[row(CD), pl.BlockSpec((tm, CD), lambda i: (i, 6)), row(AD), row(D), row(PLE),
                  full(1, CD), full(1, CD), full(CD, CD), full(1, CD), full(D, D),
                  full(1, D), full(D, D), full(NCHIP, PLE, PLE)],
        out_specs=[row(CD), row(CD), row(D), row(D), row(D), row(D)],
        out_shape=[jax.ShapeDtypeStruct((T, CD), F32), jax.ShapeDtypeStruct((T, CD), BF16),
                   jax.ShapeDtypeStruct((T, D), F32), jax.ShapeDtypeStruct((T, D), BF16),
                   jax.ShapeDtypeStruct((T, D), BF16), jax.ShapeDtypeStruct((T, D), F32)],
        compiler_params=_cp(("arbitrary",), VMEM_BIG),
    )(c1, u, ya, h, p, lg, lb, wpw, cg, wout, pg, wgate, wple)


def _loss_head(h, tgt, fg):
    tm = 256

    def body(h_ref, t_ref, g_ref, loss_ref, dh_ref, dg_ref):
        @pl.when(pl.program_id(0) == 0)
        def _():
            loss_ref[...] = jnp.zeros_like(loss_ref)
            dg_ref[...] = jnp.zeros_like(dg_ref)
        hh = h_ref[...]
        g = g_ref[...]
        r = _rstd(hh)
        e = hh * r * g - t_ref[...]
        loss_ref[...] += 0.5 * jnp.sum(jnp.mean(e * e, axis=-1, keepdims=True))
        dy = e * (1.0 / D)
        dg_ref[...] += jnp.sum(dy * hh * r, axis=0, keepdims=True)
        dh_ref[...] = _rms_bwd(dy, hh, r, g)

    return pl.pallas_call(
        body, name="loss_head", grid=(T // tm,),
        in_specs=[pl.BlockSpec((tm, D), lambda i: (i, 0)), pl.BlockSpec((tm, D), lambda i: (i, 0)),
                  pl.BlockSpec((1, D), lambda i: (0, 0))],
        out_specs=[pl.BlockSpec((8, LANE), lambda i: (0, 0)), pl.BlockSpec((tm, D), lambda i: (i, 0)),
                   pl.BlockSpec((1, D), lambda i: (0, 0))],
        out_shape=[jax.ShapeDtypeStruct((8, LANE), F32), jax.ShapeDtypeStruct((T, D), F32),
                   jax.ShapeDtypeStruct((1, D), F32)],
        compiler_params=_cp(("arbitrary",)),
    )(h, tgt, fg)


def _ple_out_bwd(dh2, h1, gate, pe, p, ya, yc, pg, wgate, wout):
    tm = 256

    def body(dh2_ref, h1_ref, gate_ref, pe_ref, p_ref, ya_ref, yc_ref, pg_ref, wgate_ref, wout_ref,
             dh1_ref, dy_ref, dwg_ref, dwp_ref, dwo_ref, dpg_ref):
        @pl.when(pl.program_id(0) == 0)
        def _():
            dwg_ref[...] = jnp.zeros_like(dwg_ref)
            dwp_ref[...] = jnp.zeros_like(dwp_ref)
            dwo_ref[...] = jnp.zeros_like(dwo_ref)
            dpg_ref[...] = jnp.zeros_like(dpg_ref)
        dh2 = dh2_ref[...]
        h1 = h1_ref[...]
        gate = gate_ref[...].astype(F32)
        pg = pg_ref[...]
        dpe = (dh2 * gate).astype(BF16)
        dgp = (dh2 * pe_ref[...].astype(F32) * gate * (1.0 - gate)).astype(BF16)
        r = _rstd(h1)
        hn = h1 * r
        dwg_ref[...] += _dot_tn((hn * pg).astype(BF16), dgp)
        dhn2 = _dot_nt(dgp, wgate_ref[...])
        dpg_ref[...] += jnp.sum(dhn2 * hn, axis=0, keepdims=True)
        dh1 = dh2 + _rms_bwd(dhn2, h1, r, pg)
        pb = p_ref[...].astype(BF16)
        for k in range(NCHIP):
            dwp_ref[k] += _dot_tn(pb, dpe[:, k * PLE:(k + 1) * PLE])
        dh1b = dh1.astype(BF16)
        dy_ref[...] = _dot_nt(dh1b, wout_ref[...])
        dwo_ref[pl.ds(0, AD), :] += _dot_tn(ya_ref[...], dh1b)
        dwo_ref[pl.ds(AD, CD), :] += _dot_tn(yc_ref[...], dh1b)
        dh1_ref[...] = dh1

    row = lambda w: pl.BlockSpec((tm, w), lambda i: (i, 0))
    full = lambda *s: pl.BlockSpec(s, lambda i: (0,) * len(s))
    return pl.pallas_call(
        body, name="ple_out_bwd", grid=(T // tm,),
        in_specs=[row(D), row(D), row(D), row(D), row(PLE), row(AD), row(CD),
                  full(1, D), full(D, D), full(D, D)],
        out_specs=[row(D), row(D), full(D, D), full(NCHIP, PLE, PLE), full(D, D), full(1, D)],
        out_shape=[jax.ShapeDtypeStruct((T, D), F32), jax.ShapeDtypeStruct((T, D), F32),
                   jax.ShapeDtypeStruct((D, D), F32), jax.ShapeDtypeStruct((NCHIP, PLE, PLE), F32),
                   jax.ShapeDtypeStruct((D, D), F32), jax.ShapeDtypeStruct((1, D), F32)],
        compiler_params=_cp(("arbitrary",), VMEM_BIG),
    )(dh2, h1, gate, pe, p, ya, yc, pg, wgate, wout)


def _branch_bwd(dy, o, u, c1, c3, ag, lg, lb, wpw, cg, seg):
    tm = 256

    def body(dya_ref, dyc_ref, o_ref, ga_ref, gc_ref, c1_ref, c3_ref, ag_ref, lg_ref, lb_ref, wpw_ref,
             cg_ref, seg_ref, do_ref, dga_ref, dgc_ref, dc1_ref, dwpw_ref, dag_ref, dcg_ref, dlg_ref, dlb_ref):
        @pl.when(pl.program_id(0) == 0)
        def _():
            for r_ in (dwpw_ref, dag_ref, dcg_ref, dlg_ref, dlb_ref):
                r_[...] = jnp.zeros_like(r_)
        dya = dya_ref[...]
        o = o_ref[...]
        ga = ga_ref[...]
        ag_v = ag_ref[...]
        seg_m = seg_ref[...]
        r = lax.rsqrt(_dot2(o * o, seg_m) * (1.0 / DH) + EPS)
        onr = o * r
        sg = _sig(ga)
        dga_ref[...] = (dya * (onr * ag_v) * (sg * (1.0 + ga * (1.0 - sg)))).astype(BF16)
        don = dya * (ga * sg)
        dag_ref[...] += jnp.sum(don * onr, axis=0, keepdims=True)
        dn = don * ag_v
        do_ref[...] = r * dn - o * (r * r * r) * (_dot2(dn * o, seg_m) * (1.0 / DH))
        dyc = dyc_ref[...]
        c3 = c3_ref[...]
        gc = gc_ref[...]
        cg_v = cg_ref[...]
        r3 = _rstd(c3)
        cn = c3 * r3
        sc = _sig(gc)
        dgc_ref[...] = (dyc * (cn * cg_v) * (sc * (1.0 + gc * (1.0 - sc)))).astype(BF16)
        dcn = dyc * (gc * sc)
        dcg_ref[...] += jnp.sum(dcn * cn, axis=0, keepdims=True)
        dc3 = _rms_bwd(dcn, c3, r3, cg_v).astype(BF16)
        lg_v = lg_ref[...]
        xh, rs, ln, s = _ln_silu(c1_ref[...], lg_v, lb_ref[...])
        c2 = (ln * s).astype(BF16)
        dwpw_ref[...] += _dot_tn(c2, dc3)
        dc2 = _dot_nt(dc3, wpw_ref[...])
        dln = dc2 * (s * (1.0 + ln * (1.0 - s)))
        dlb_ref[...] += jnp.sum(dln, axis=0, keepdims=True)
        dlg_ref[...] += jnp.sum(dln * xh, axis=0, keepdims=True)
        dxh = dln * lg_v
        dc1_ref[...] = rs * (dxh - jnp.mean(dxh, axis=-1, keepdims=True)
                             - xh * jnp.mean(dxh * xh, axis=-1, keepdims=True))

    half = lambda j: pl.BlockSpec((tm, 512), lambda i: (i, j))
    full = lambda *s: pl.BlockSpec(s, lambda i: (0,) * len(s))
    vec = jax.ShapeDtypeStruct((1, 512), F32)
    act = jax.ShapeDtypeStruct((T, 512), F32)
    return pl.pallas_call(
        body, name="branch_bwd", grid=(T // tm,),
        in_specs=[half(0), half(1), half(0), half(3), half(6), half(0), half(0),
                  full(1, AD), full(1, CD), full(1, CD), full(CD, CD), full(1, CD), full(AD, AD)],
        out_specs=[half(0), half(0), half(0), half(0), full(CD, CD), full(1, 512), full(1, 512),
                   full(1, 512), full(1, 512)],
        out_shape=[act, jax.ShapeDtypeStruct((T, 512), BF16), jax.ShapeDtypeStruct((T, 512), BF16), act,
                   jax.ShapeDtypeStruct((CD, CD), F32), vec, vec, vec, vec],
        compiler_params=_cp(("arbitrary",), VMEM_BIG),
    )(dy, dy, o, u, u, c1, c3, ag, lg, lb, wpw, cg, seg)


def _conv_bwd(dc1, u, dw):
    tr = 64
    off = CWP - CW + 1

    def body(d_ref, cv_ref, cg_ref, w_ref, dcv_ref, dcg_ref, ddw_ref, ddb_ref, padc_s, padd_s, acc_s):
        cv = cv_ref[...]
        sg = _sig(cg_ref[...])
        padc_s[pl.ds(0, CWP), :] = jnp.zeros((CWP, LANE), F32)
        padc_s[pl.ds(CWP, T), :] = cv * sg
        padd_s[pl.ds(0, T), :] = d_ref[...]
        padd_s[pl.ds(T, CWP), :] = jnp.zeros((CWP, LANE), F32)
        acc_s[...] = jnp.zeros_like(acc_s)
        wv = w_ref[0]

        def tile(i, carry):
            r0 = pl.multiple_of(i * tr, tr)
            dt = padd_s[pl.ds(r0, tr), :]
            dc0 = jnp.zeros((tr, LANE), F32)
            for w in range(CW):
                dc0 = dc0 + padd_s[pl.ds(r0 + (CW - 1) - w, tr), :] * wv[w:w + 1, :]
                prod = dt * padc_s[pl.ds(r0 + off + w, tr), :]
                acc_s[w] += jnp.sum(prod.reshape(tr // 8, 8, LANE), axis=0)
            cvt = cv_ref[pl.ds(r0, tr), :]
            sgt = _sig(cg_ref[pl.ds(r0, tr), :])
            dcv_ref[pl.ds(r0, tr), :] = (dc0 * sgt).astype(BF16)
            dcg_ref[pl.ds(r0, tr), :] = (dc0 * cvt * sgt * (1.0 - sgt)).astype(BF16)
            return carry

        lax.fori_loop(0, T // tr, tile, 0)
        ddw_ref[0] = jnp.sum(acc_s[...], axis=1)
        ddb_ref[...] = jnp.sum(d_ref[...], axis=0, keepdims=True)

    col = lambda j: pl.BlockSpec((T, LANE), lambda cb: (0, j + cb))
    return pl.pallas_call(
        body, name="conv_bwd", grid=(CD // LANE,),
        in_specs=[col(0), col(16), col(20), pl.BlockSpec((1, CWP, LANE), lambda cb: (cb, 0, 0))],
        out_specs=[col(0), col(0), pl.BlockSpec((1, CWP, LANE), lambda cb: (cb, 0, 0)),
                   pl.BlockSpec((1, LANE), lambda cb: (0, cb))],
        out_shape=[jax.ShapeDtypeStruct((T, CD), BF16), jax.ShapeDtypeStruct((T, CD), BF16),
                   jax.ShapeDtypeStruct((NCHIP, CWP, LANE), F32), jax.ShapeDtypeStruct((1, CD), F32)],
        scratch_shapes=[pltpu.VMEM((T + CWP, LANE), F32), pltpu.VMEM((T + CWP, LANE), F32),
                        pltpu.VMEM((CWP, 8, LANE), F32)],
        compiler_params=_cp(("arbitrary",)),
    )(dc1, u, u, dw)


def _attn_bwd(u, do, tot, partials=()):
    n_x = len(partials)
    grid = (NG, T // AQ)

    def body(q_ref, k_ref, v_ref, do_ref, tot_ref, dq_ref, dk_ref, dv_ref, kb_s, vb_s, dk_s, dv_s):
        qi = pl.program_id(1)

        @pl.when(qi == 0)
        def _():
            kb_s[...] = k_ref[...].astype(BF16)
            vb_s[...] = v_ref[...].astype(BF16)
            dk_s[...] = jnp.zeros_like(dk_s)
            dv_s[...] = jnp.zeros_like(dv_s)

        causal, tr, tc, heads = _attn_tiles()
        upper = (tr > tc).astype(BF16)
        lower = (tr < tc).astype(BF16)
        q = q_ref[...]
        qs = _stack_heads(q * 0.125, heads).astype(BF16)
        qus = _stack_heads(q, heads).astype(BF16)
        dos = _stack_heads(do_ref[...], heads).astype(BF16)
        totv = tot_ref[0]
        tots = jnp.concatenate([totv[:, h * DH:h * DH + 1] for h in range(HG)], axis=0)

        def block(kb, carry, masked):
            lm_left, dl_left, dq = carry
            k0 = pl.multiple_of(kb * AQ, AQ)
            kk = kb_s[pl.ds(k0, AQ), :]
            vv = vb_s[pl.ds(k0, AQ), :]
            z = _dot_nt(qs, kk)
            sp = _softplus(z)
            lm = jnp.where(causal, -sp, 0.0) if masked else -sp
            lm_incl = lm_left + jnp.sum(lm, axis=1, keepdims=True)
            att = jnp.exp((z - sp) + _tri_sum(lm, upper) + (tots - lm_incl))
            if masked:
                att = jnp.where(causal, att, 0.0)
            dl = att * _dot_nt(dos, vv)
            dv_s[pl.ds(k0, AQ), :] += _dot_tn(att.astype(BF16), dos)
            prefix = dl_left + _tri_sum(dl, lower)
            beta = jnp.exp(z - sp)
            dz = (1.0 - beta) * dl - beta * prefix
            if masked:
                dz = jnp.where(causal, dz, 0.0)
            dzs = (dz * 0.125).astype(BF16)
            dk_s[pl.ds(k0, AQ), :] += _dot_tn(dzs, qus)
            return lm_incl, dl_left + jnp.sum(dl, axis=1, keepdims=True), dq + _dot(dzs, kk)

        zero = jnp.zeros((SR, 1), F32)
        carry = lax.fori_loop(0, qi, lambda kb, c: block(kb, c, False),
                              (zero, zero, jnp.zeros((SR, GW), F32)))
        _, _, dq = block(qi, carry, True)
        dq_ref[...] = _unstack_heads(dq, heads).astype(BF16)

        @pl.when(qi == grid[1] - 1)
        def _():
            dk_ref[...] = dk_s[...].astype(BF16)
            dv_ref[...] = dv_s[...].astype(BF16)

    res = pl.pallas_call(
        _host(body, grid, 5, 3, n_x, _scatter_copies), name="attn_bwd", grid=grid,
        in_specs=[pl.BlockSpec((AQ, GW), lambda hg, qi: (qi, hg)),
                  pl.BlockSpec((T, GW), lambda hg, qi: (0, NG + hg)),
                  pl.BlockSpec((T, GW), lambda hg, qi: (0, 2 * NG + hg)),
                  pl.BlockSpec((AQ, GW), lambda hg, qi: (qi, hg)),
                  pl.BlockSpec((1, AQ, GW), lambda hg, qi: (hg, qi, 0))] + [HBM_SPEC] * n_x,
        out_specs=[pl.BlockSpec((AQ, GW), lambda hg, qi: (qi, hg)),
                   pl.BlockSpec((T, GW), lambda hg, qi: (0, hg)),
                   pl.BlockSpec((T, GW), lambda hg, qi: (0, hg))] + [HBM_SPEC] * n_x,
        out_shape=[jax.ShapeDtypeStruct((T, AD), BF16)] * 3
        + [jax.ShapeDtypeStruct((NCHIP - 1,) + a.shape[1:], a.dtype) for a in partials],
        scratch_shapes=[pltpu.VMEM((T, GW), BF16), pltpu.VMEM((T, GW), BF16), pltpu.VMEM((T, GW), F32),
                        pltpu.VMEM((T, GW), F32)] + _hosted_sems(n_x),
        compiler_params=_cp(("arbitrary", "arbitrary"), VMEM_BIG),
    )(u, u, u, do, tot, *partials)
    return res[0], res[1], res[2], list(res[3:])


def _inproj_dw(hn, du):
    tm = min(TM, T)

    def body(hn_ref, du_ref, dw_ref):
        @pl.when(pl.program_id(1) == 0)
        def _():
            dw_ref[...] = jnp.zeros_like(dw_ref)
        dw_ref[0] += _dot_tn(hn_ref[...], du_ref[...])

    return pl.pallas_call(
        body, name="inproj_dw", grid=(NCHIP, T // tm),
        in_specs=[pl.BlockSpec((tm, D), lambda k, i: (i, 0)), pl.BlockSpec((tm, SHW), lambda k, i: (i, k))],
        out_specs=pl.BlockSpec((1, D, SHW), lambda k, i: (k, 0, 0)),
        out_shape=jax.ShapeDtypeStruct((NCHIP, D, SHW), F32),
        compiler_params=_cp(("arbitrary", "arbitrary"), VMEM_BIG),
    )(hn, du)


def _inproj_dx(du, w, h, g, dres, partials=(), grads=()):
    tm = min(TM, T)
    sent = list(partials) + list(grads)
    n_x = len(sent)
    grid = (T // tm, NCHIP)
    if grads:
        landing = [jax.ShapeDtypeStruct((NCHIP, a.shape[1] // 2, a.shape[2]), F32) for a in grads]
    else:
        landing = [jax.ShapeDtypeStruct((NCHIP - 1,) + a.shape[1:], a.dtype) for a in partials]

    def body(du_ref, w_ref, h_ref, g_ref, dres_ref, dh_ref, dg_ref, acc_s):
        i, k = pl.program_id(0), pl.program_id(1)

        @pl.when(jnp.logical_and(i == 0, k == 0))
        def _():
            dg_ref[...] = jnp.zeros_like(dg_ref)

        @pl.when(k == 0)
        def _():
            acc_s[...] = _dot_nt(du_ref[...], w_ref[0])

        @pl.when(k > 0)
        def _():
            acc_s[...] += _dot_nt(du_ref[...], w_ref[0])

        @pl.when(k == NCHIP - 1)
        def _():
            hh = h_ref[...]
            r = _rstd(hh)
            dhn = acc_s[...]
            dg_ref[...] += jnp.sum(dhn * hh * r, axis=0, keepdims=True)
            dh_ref[...] = dres_ref[...] + _rms_bwd(dhn, hh, r, g_ref[...])

    res = pl.pallas_call(
        _host(body, grid, 5, 2, n_x, _pair_copies if grads else _scatter_copies), name="inproj_dx", grid=grid,
        in_specs=[pl.BlockSpec((tm, SHW), lambda i, k: (i, k)),
                  pl.BlockSpec((1, D, SHW), lambda i, k: (k, 0, 0)),
                  pl.BlockSpec((tm, D), lambda i, k: (i, 0)),
                  pl.BlockSpec((1, D), lambda i, k: (0, 0)),
                  pl.BlockSpec((tm, D), lambda i, k: (i, 0))] + [HBM_SPEC] * n_x,
        out_specs=[pl.BlockSpec((tm, D), lambda i, k: (i, 0)), pl.BlockSpec((1, D), lambda i, k: (0, 0))]
        + [HBM_SPEC] * n_x,
        out_shape=[jax.ShapeDtypeStruct((T, D), F32), jax.ShapeDtypeStruct((1, D), F32)] + landing,
        scratch_shapes=[pltpu.VMEM((tm, D), F32)] + _hosted_sems(n_x),
        compiler_params=_cp(("arbitrary", "arbitrary"), VMEM_BIG),
    )(du, w, h, g, dres, *sent)
    return res[0], res[1], list(res[2:])


def _sum_pair(core, grads, gots):
    n = len(grads)

    def body(c_ref, *refs):
        for a in range(n):
            refs[2 * n + a][...] = (refs[a][...] + refs[n + a][...]).astype(BF16)

    mine = [pl.BlockSpec((1,) + s.shape[1:], lambda k, c: (k, c[0], 0)) for s in gots]
    same = [pl.BlockSpec((1,) + s.shape[1:], lambda k, c: (k, 0, 0)) for s in gots]
    return pl.pallas_call(
        body, name="sum_pair",
        grid_spec=pltpu.PrefetchScalarGridSpec(
            num_scalar_prefetch=1, grid=(NCHIP,), in_specs=mine + same, out_specs=same),
        out_shape=[jax.ShapeDtypeStruct(s.shape, BF16) for s in gots],
        compiler_params=_cp(("arbitrary",), VMEM_BIG),
    )(core, *grads, *gots)


def _sum_chips_share(chip, partials, gots):
    n = len(partials)

    def body(c_ref, *refs):
        mine, theirs = refs[2 * n:3 * n], refs[3 * n:4 * n]
        for a in range(n):
            acc = refs[a][0].astype(F32)
            for j in range(NCHIP - 1):
                acc = acc + refs[n + a][j].astype(F32)
            mine[a][...] = acc
        x, y, c = _place()
        copies = [pltpu.make_async_remote_copy(
            src_ref=mine[a], dst_ref=theirs[a], send_sem=refs[4 * n].at[a], recv_sem=refs[4 * n + 1].at[a],
            device_id=(x, y, 1 - c), device_id_type=MESH) for a in range(n)]
        for cp in copies:
            cp.start()
        for cp in copies:
            cp.wait()

    halves = [jax.ShapeDtypeStruct(s.shape[1:], F32) for s in partials]
    res = pl.pallas_call(
        body, name="sum_chips_share",
        grid_spec=pltpu.PrefetchScalarGridSpec(
            num_scalar_prefetch=1, grid=(1,),
            in_specs=[pl.BlockSpec((1,) + s.shape[1:], lambda i, c: (c[0], 0, 0)) for s in partials]
            + [pl.BlockSpec(s.shape, lambda i, c: (0, 0, 0)) for s in gots],
            out_specs=[pl.BlockSpec(s.shape[1:], lambda i, c: (0, 0)) for s in partials] + [HBM_SPEC] * n,
            scratch_shapes=[pltpu.SemaphoreType.DMA((n,)), pltpu.SemaphoreType.DMA((n,))]),
        out_shape=halves + halves,
        compiler_params=_cp(("arbitrary",), VMEM_BIG),
    )(chip, *partials, *gots)
    return res[:n], res[n:]


def _adamw(w, g, m, v, rows):
    R, C = w.shape
    c1 = 1.0 - ADAM_B1 ** ADAM_STEP
    c2 = 1.0 - ADAM_B2 ** ADAM_STEP

    def body(w_ref, g_ref, m_ref, v_ref, d_ref, nm_ref, nv_ref):
        gg = g_ref[...]
        nm = ADAM_B1 * m_ref[...] + (1.0 - ADAM_B1) * gg
        nv = ADAM_B2 * v_ref[...] + (1.0 - ADAM_B2) * (gg * gg)
        d_ref[...] = -ADAM_LR * ((nm / c1) / (jnp.sqrt(nv / c2) + ADAM_EPS) + ADAM_WD * w_ref[...])
        nm_ref[...] = nm
        nv_ref[...] = nv

    spec = pl.BlockSpec((rows, C), lambda i: (i, 0))
    sh = jax.ShapeDtypeStruct((R, C), F32)
    return pl.pallas_call(
        body, name="adamw", grid=(R // rows,), in_specs=[spec] * 4, out_specs=[spec] * 3,
        out_shape=[sh, sh, sh], compiler_params=_cp(("arbitrary",)),
    )(w, g, m, v)


HBM_SPEC = pl.BlockSpec(memory_space=pltpu.HBM)


def _place():
    return lax.axis_index("x"), lax.axis_index("y"), lax.axis_index("c")


def _all_gather_split(shard):
    def body(in_ref, out_ref, send_sems, recv_sems):
        fetch, passed = _gather_split_copies([in_ref], [out_ref], send_sems, recv_sems)
        for cp in fetch:
            cp.start()
        for got, onward in zip(fetch, passed):
            got.wait_recv()
            onward.start()
        for cp in passed:
            cp.wait_recv()
        for cp in fetch + passed:
            cp.wait_send()

    return pl.pallas_call(
        body, name="all_gather_split", in_specs=[HBM_SPEC], out_specs=HBM_SPEC,
        out_shape=jax.ShapeDtypeStruct((NCHIP,) + shard.shape, shard.dtype),
        scratch_shapes=[pltpu.SemaphoreType.DMA((6,)), pltpu.SemaphoreType.DMA((6,))],
    )(shard)


def _pair_exchange(grads):
    n = len(grads)

    def body(*refs):
        copies = _pair_copies(refs[:n], refs[n:2 * n], refs[2 * n], refs[2 * n + 1])
        for cp in copies:
            cp.start()
        for cp in copies:
            cp.wait()

    return pl.pallas_call(
        body, name="pair_exchange", in_specs=[HBM_SPEC] * n, out_specs=[HBM_SPEC] * n,
        out_shape=[jax.ShapeDtypeStruct((NCHIP, g.shape[1] // 2, g.shape[2]), F32) for g in grads],
        scratch_shapes=[pltpu.SemaphoreType.DMA((n,)), pltpu.SemaphoreType.DMA((n,))],
    )(*grads)


def _small_allreduce(mine):
    def body(m_ref, o_ref, slots, send_sems, recv_sems):
        x, y, c = _place()
        me = 4 * x + 2 * y + c
        slots[me] = m_ref[...]
        copies = []
        for r in range(1, 8):
            rx, ry, rc = (r >> 2) & 1, (r >> 1) & 1, r & 1
            peer = (x + rx - 2 * x * rx, y + ry - 2 * y * ry, c + rc - 2 * c * rc)
            cp = pltpu.make_async_remote_copy(
                src_ref=m_ref, dst_ref=slots.at[me], send_sem=send_sems.at[r - 1], recv_sem=recv_sems.at[r - 1],
                device_id=peer, device_id_type=MESH)
            cp.start()
            copies.append(cp)
        for cp in copies:
            cp.wait()
        acc = slots[0]
        for j in range(1, 8):
            acc = acc + slots[j]
        o_ref[...] = acc

    return pl.pallas_call(
        body, name="small_allreduce",
        in_specs=[pl.BlockSpec(memory_space=pltpu.VMEM)], out_specs=pl.BlockSpec(memory_space=pltpu.VMEM),
        out_shape=jax.ShapeDtypeStruct((SMALL_ROWS, LANE), F32),
        scratch_shapes=[pltpu.VMEM((8, SMALL_ROWS, LANE), F32), pltpu.SemaphoreType.DMA((7,)),
                        pltpu.SemaphoreType.DMA((7,))],
    )(mine)


def _seg_matrix():
    i = lax.broadcasted_iota(jnp.int32, (AD, AD), 0) // DH
    j = lax.broadcasted_iota(jnp.int32, (AD, AD), 1) // DH
    return (i == j).astype(BF16)


TAIL = ("w_out", "w_ple_gate", "w_ple", "w_pw", "dw_w")


def _local_step(x, p, tgt, sm, shards, chip, ci):
    seg = _seg_matrix()
    core = jnp.reshape(ci, (1,)).astype(jnp.int32)
    chip_idx = jnp.reshape(chip, (1,)).astype(jnp.int32)
    own = lambda g, s: lax.dynamic_update_index_in_dim(g, s, chip, 0)
    w_in_next = own(_all_gather_split(shards[0]["w_in"]), shards[0]["w_in"])
    h = x
    saved = []
    for l in range(DEPTH):
        w_in = w_in_next
        row = lambda name: sm[name][l:l + 1]
        u, hn = _rms_inproj(h, row("norm_g"), w_in)
        todo = [shards[l][k] for k in TAIL] + ([shards[l + 1]["w_in"]] if l + 1 < DEPTH else [])
        o, ya, tot, got = _attn_fwd(u, jnp.tile(row("attn_out_g"), (1, AD // DH)), todo)
        got = [own(g, s) for g, s in zip(got, todo)]
        w_out = got[0].reshape(D, D)
        w_gate = got[1].reshape(D, D)
        w_ple = got[2]
        w_pw = got[3].reshape(CD, CD)
        dw = got[4]
        if l + 1 < DEPTH:
            w_in_next = got[5]
        c1 = _glu_conv(u, dw, row("dw_b"))
        c3, yc, h1, gate, pe, h2 = _layer_tail(
            c1, u, ya, h, p[l], row("conv_ln_g"), row("conv_ln_b"), w_pw, row("conv_out_g"), w_out,
            row("ple_norm_g"), w_gate, w_ple)
        saved.append(dict(h=h, u=u, hn=hn, o=o, ya=ya, tot=tot, c1=c1, c3=c3, yc=yc, h1=h1, gate=gate, pe=pe,
                          w_in=w_in, w_out=w_out, w_gate=w_gate, w_pw=w_pw, dw=dw))
        h = h2
    loss_blk, dh, dfg = _loss_head(h, tgt, sm["final_g"])
    small = [None] * DEPTH
    pending, partials, arrived = [], {}, {}
    pair_sum = lambda grads: _sum_pair(core, grads, _pair_exchange(grads))
    for l in reversed(range(DEPTH)):
        s = saved[l]
        row = lambda name: sm[name][l:l + 1]
        dh1, dy, dwg, dwp, dwo, dpg = _ple_out_bwd(
            dh, s["h1"], s["gate"], s["pe"], p[l], s["ya"], s["yc"], row("ple_norm_g"), s["w_gate"], s["w_out"])
        ag_t = jnp.tile(row("attn_out_g"), (1, AD // DH))
        do, dga, dgc, dc1, dwpw, dag, dcg, dlg, dlb = _branch_bwd(
            dy, s["o"], s["u"], s["c1"], s["c3"], ag_t, row("conv_ln_g"), row("conv_ln_b"), s["w_pw"],
            row("conv_out_g"), seg)
        dcv, dcgate, ddw, ddb = _conv_bwd(dc1, s["u"], s["dw"])
        tail = [dwo.reshape(NCHIP, 256, D), dwg.reshape(NCHIP, 256, D), dwp, dwpw.reshape(NCHIP, 128, CD), ddw]
        if l == 0:
            partials[(l, "tail")] = pair_sum(tail)
            pending.append((l, "tail"))
        send = [t for key in pending for t in partials[key]]
        dq, dk, dv, got = _attn_bwd(s["u"], do, s["tot"], send)
        for key in pending:
            arrived[key], got = got[:len(partials[key])], got[len(partials[key]):]
        pending = []
        du = jnp.concatenate([dq, dk, dv, dga, dcv, dcgate, dgc], axis=1)
        dwin = _inproj_dw(s["hn"], du)
        if l == 0:
            partials[(l, "w_in")] = pair_sum([dwin])
            dh, dng, arrived[(l, "w_in")] = _inproj_dx(du, s["w_in"], s["h"], row("norm_g"), dh1,
                                                      partials[(l, "w_in")])
        else:
            dh, dng, halves = _inproj_dx(du, s["w_in"], s["h"], row("norm_g"), dh1, grads=tail + [dwin])
            tail_p = _sum_pair(core, tail + [dwin], halves)
            partials[(l, "tail")], partials[(l, "w_in")] = tail_p[:-1], tail_p[-1:]
            pending = [(l, "tail"), (l, "w_in")]
        small[l] = dict(norm_g=dng, attn_out_g=dag.reshape(AD // DH, DH).sum(axis=0, keepdims=True), dw_b=ddb,
                        conv_ln_g=dlg, conv_ln_b=dlb, conv_out_g=dcg, ple_norm_g=dpg)
    big = []
    for l in range(DEPTH):
        ps = partials[(l, "w_in")] + partials[(l, "tail")]
        mine, theirs = _sum_chips_share(chip_idx, ps, arrived[(l, "w_in")] + arrived[(l, "tail")])
        big.append(dict(zip(BIG, [jnp.where(core[0] == 0, jnp.concatenate([m, t]), jnp.concatenate([t, m]))
                                  for m, t in zip(mine, theirs)])))
    return loss_blk[0, 0], dh, big, small, dfg


BIG = ("w_in", "w_out", "w_ple_gate", "w_ple", "w_pw", "dw_w")
SMALL2 = ("norm_g", "ple_norm_g", "dw_b", "conv_ln_g", "conv_ln_b", "conv_out_g", "attn_out_g")


def _pack_small(two, final):
    flat = jnp.concatenate([two[k].reshape(-1) for k in SMALL2] + [final.reshape(-1)])
    flat = jnp.concatenate([flat, jnp.zeros((SMALL_ROWS * LANE - flat.shape[0],), F32)])
    return flat.reshape(SMALL_ROWS, LANE)


def _unpack_small(packed, like_two, like_final):
    flat = packed.reshape(-1)
    out, off = {}, 0
    for k in SMALL2:
        n = like_two[k].size
        out[k] = flat[off:off + n].reshape(like_two[k].shape)
        off += n
    return out, flat[off:off + like_final.size].reshape(like_final.shape)


def kernel(x, p, norm_g, w_in, attn_out_g, dw_w, dw_b, conv_ln_g, conv_ln_b, w_pw, conv_out_g, w_out, ple_norm_g, w_ple_gate, w_ple, final_g, loss_target, m_norm_g, m_w_in, m_attn_out_g, m_dw_w, m_dw_b, m_conv_ln_g, m_conv_ln_b, m_w_pw, m_conv_out_g, m_w_out, m_ple_norm_g, m_w_ple_gate, m_w_ple, m_final_g, v_norm_g, v_w_in, v_attn_out_g, v_dw_w, v_dw_b, v_conv_ln_g, v_conv_ln_b, v_w_pw, v_conv_out_g, v_w_out, v_ple_norm_g, v_w_ple_gate, v_w_ple, v_final_g):
    W = dict(norm_g=norm_g, w_in=w_in, attn_out_g=attn_out_g, dw_w=dw_w, dw_b=dw_b, conv_ln_g=conv_ln_g,
             conv_ln_b=conv_ln_b, w_pw=w_pw, conv_out_g=conv_out_g, w_out=w_out, ple_norm_g=ple_norm_g,
             w_ple_gate=w_ple_gate, w_ple=w_ple, final_g=final_g)
    M = dict(norm_g=m_norm_g, w_in=m_w_in, attn_out_g=m_attn_out_g, dw_w=m_dw_w, dw_b=m_dw_b,
             conv_ln_g=m_conv_ln_g, conv_ln_b=m_conv_ln_b, w_pw=m_w_pw, conv_out_g=m_conv_out_g, w_out=m_w_out,
             ple_norm_g=m_ple_norm_g, w_ple_gate=m_w_ple_gate, w_ple=m_w_ple, final_g=m_final_g)
    V = dict(norm_g=v_norm_g, w_in=v_w_in, attn_out_g=v_attn_out_g, dw_w=v_dw_w, dw_b=v_dw_b,
             conv_ln_g=v_conv_ln_g, conv_ln_b=v_conv_ln_b, w_pw=v_w_pw, conv_out_g=v_conv_out_g, w_out=v_w_out,
             ple_norm_g=v_ple_norm_g, w_ple_gate=v_w_ple_gate, w_ple=v_w_ple, final_g=v_final_g)
    order = ("norm_g", "w_in", "attn_out_g", "dw_w", "dw_b", "conv_ln_g", "conv_ln_b", "w_pw", "conv_out_g",
             "w_out", "ple_norm_g", "w_ple_gate", "w_ple", "final_g")

    pad_taps = lambda a: jnp.pad(a, ((0, 0), (0, CWP - CW), (0, 0)))
    cast = dict(w_in=w_in.astype(BF16), w_out=w_out.astype(BF16), w_ple_gate=w_ple_gate.astype(BF16),
                w_ple=w_ple.astype(BF16), w_pw=w_pw.astype(BF16), dw_w=pad_taps(dw_w))
    shards = [{k: v[l] for k, v in cast.items()} for l in range(DEPTH)]
    xi, yi, ci = lax.axis_index("x"), lax.axis_index("y"), lax.axis_index("c")
    chip = 2 * xi + yi

    sm = {k: W[k] for k in SMALL2}
    sm["final_g"] = final_g.reshape(1, D)
    loss_part, grad_x, big, small, dfg = _local_step(x[0], p[:, 0], loss_target[0], sm, shards, chip, ci)
    g_big = {name: jnp.stack([big[l][name] for l in range(DEPTH)]).reshape(cast[name].shape) for name in BIG}

    small_two = {k: jnp.concatenate([small[l][k] for l in range(DEPTH)], axis=0) for k in SMALL2}
    g_small_packed = _small_allreduce(_pack_small(small_two, dfg).at[SMALL_ROWS - 1, LANE - 1].set(loss_part))
    loss = g_small_packed[SMALL_ROWS - 1, LANE - 1]
    g_small, g_final = _unpack_small(g_small_packed, {k: W[k] for k in SMALL2}, final_g)

    grads, deltas, new_m, new_v = {}, {}, {}, {}
    for name in BIG:
        wv = pad_taps(W[name]) if name == "dw_w" else W[name]
        mv = pad_taps(M[name]) if name == "dw_w" else M[name]
        vv = pad_taps(V[name]) if name == "dw_w" else V[name]
        gg = g_big[name]
        cols = wv.shape[-1]
        rows_total = wv.size // cols
        tile_rows = min(rows_total, 256)
        d2, m2, v2 = _adamw(wv.reshape(rows_total, cols), gg.reshape(rows_total, cols),
                            mv.reshape(rows_total, cols), vv.reshape(rows_total, cols), tile_rows)
        if name == "dw_w":
            cut = lambda a: a.reshape(DEPTH, CWP, LANE)[:, :CW]
            grads[name], deltas[name], new_m[name], new_v[name] = cut(gg), cut(d2), cut(m2), cut(v2)
        else:
            grads[name] = gg
            deltas[name], new_m[name], new_v[name] = (t.reshape(wv.shape) for t in (d2, m2, v2))
    ws = _pack_small({k: W[k] for k in SMALL2}, final_g)
    ms = _pack_small({k: M[k] for k in SMALL2}, m_final_g)
    vs = _pack_small({k: V[k] for k in SMALL2}, v_final_g)
    ds, nms, nvs = _adamw(ws, g_small_packed, ms, vs, SMALL_ROWS)
    for packed, dst in ((ds, deltas), (nms, new_m), (nvs, new_v)):
        two, fin = _unpack_small(packed, {k: W[k] for k in SMALL2}, final_g)
        dst.update(two)
        dst["final_g"] = fin
    grads.update(g_small)
    grads["final_g"] = g_final

    return (loss, grad_x[None], *[grads[n] for n in order], *[deltas[n] for n in order],
            *[new_m[n] for n in order], *[new_v[n] for n in order])
```

```python
import functools

import jax
import jax.numpy as jnp
from jax import lax
from jax.experimental import pallas as pl
from jax.experimental.pallas import tpu as pltpu

F32 = jnp.float32
BF16 = jnp.bfloat16

T = 2048
D = 1024
DIN = 3584
NCHIP = 4
SHW = DIN // NCHIP
AD = 512
CD = 512
DH = 64
CW = 31
CWP = 32
PLE = 256
DEPTH = 2
EPS = 1e-6
AQ = 256
HG = 4
GW = HG * DH
SR = HG * AQ
NG = AD // GW
LANE = 128
TM = 1024

ADAM_LR = 0.001
ADAM_B1 = 0.9
ADAM_B2 = 0.999
ADAM_EPS = 1e-08
ADAM_WD = 0.01
ADAM_STEP = 10

SMALL_ROWS = 80

VMEM_BIG = 56 * 1024 * 1024
MESH = pl.DeviceIdType.MESH


def _cp(sem=None, vmem=None):
    kw = {}
    if sem is not None:
        kw["dimension_semantics"] = sem
    if vmem is not None:
        kw["vmem_limit_bytes"] = vmem
    return pltpu.CompilerParams(**kw)


def _dot(a, b):
    return jnp.dot(a, b, preferred_element_type=F32)


def _dot_nt(a, b):
    return lax.dot_general(a, b, (((1,), (1,)), ((), ())), preferred_element_type=F32)


def _dot_tn(a, b):
    return lax.dot_general(a, b, (((0,), (0,)), ((), ())), preferred_element_type=F32)


def _dot2(x, m):
    hi = x.astype(BF16)
    lo = (x - hi.astype(F32)).astype(BF16)
    return _dot(hi, m) + _dot(lo, m)


def _sig(x):
    return 1.0 / (1.0 + jnp.exp(-x))


def _softplus(z):
    return jnp.maximum(z, 0.0) + jnp.log(1.0 + jnp.exp(-jnp.abs(z)))


def _rstd(x):
    return lax.rsqrt(jnp.mean(x * x, axis=-1, keepdims=True) + EPS)


def _rms_bwd(dy, x, r, g):
    dn = dy * g
    return r * dn - x * (r * r * r) * jnp.mean(dn * x, axis=-1, keepdims=True)


def _rms_inproj(h, g, w):
    tm = min(TM, T)

    def body(h_ref, g_ref, w_ref, u_ref, hn_ref, hn_s):
        @pl.when(pl.program_id(1) == 0)
        def _():
            hh = h_ref[...]
            hn = (hh * _rstd(hh) * g_ref[...]).astype(BF16)
            hn_s[...] = hn
            hn_ref[...] = hn
        u_ref[...] = _dot(hn_s[...], w_ref[0])

    return pl.pallas_call(
        body, name="rms_inproj", grid=(T // tm, NCHIP),
        in_specs=[pl.BlockSpec((tm, D), lambda i, k: (i, 0)),
                  pl.BlockSpec((1, D), lambda i, k: (0, 0)),
                  pl.BlockSpec((1, D, SHW), lambda i, k: (k, 0, 0))],
        out_specs=[pl.BlockSpec((tm, SHW), lambda i, k: (i, k)),
                   pl.BlockSpec((tm, D), lambda i, k: (i, 0))],
        out_shape=[jax.ShapeDtypeStruct((T, DIN), F32), jax.ShapeDtypeStruct((T, D), BF16)],
        scratch_shapes=[pltpu.VMEM((tm, D), BF16)],
        compiler_params=_cp(("arbitrary", "arbitrary"), VMEM_BIG),
    )(h, g, w)


def _attn_tiles():
    row = lax.broadcasted_iota(jnp.int32, (SR, AQ), 0) & (AQ - 1)
    col = lax.broadcasted_iota(jnp.int32, (SR, AQ), 1)
    tr = lax.broadcasted_iota(jnp.int32, (AQ, AQ), 0)
    tc = lax.broadcasted_iota(jnp.int32, (AQ, AQ), 1)
    lane_head = lax.broadcasted_iota(jnp.int32, (1, GW), 1) // DH
    return col < row, tr, tc, [lane_head == h for h in range(HG)]


def _stack_heads(t, heads):
    return jnp.concatenate([jnp.where(m, t, 0.0) for m in heads], axis=0)


def _unstack_heads(t, heads):
    out = t[:AQ]
    for h in range(1, HG):
        out = jnp.where(heads[h], t[h * AQ:(h + 1) * AQ], out)
    return out


def _tri_sum(x, tri):
    hi = x.astype(BF16)
    lo = (x - hi.astype(F32)).astype(BF16)
    both = _dot(jnp.concatenate([hi, lo], axis=0), tri)
    return both[:SR] + both[SR:]


def _scatter_copies(ps, gots, send_sems, recv_sems):
    x, y, c = _place()
    peers = [(1 - x, y), (x, 1 - y), (1 - x, 1 - y)]
    return [pltpu.make_async_remote_copy(
        src_ref=ps[a].at[2 * px + py], dst_ref=gots[a].at[r], send_sem=send_sems.at[3 * a + r],
        recv_sem=recv_sems.at[3 * a + r], device_id=(px, py, c), device_id_type=MESH)
        for a in range(len(ps)) for r, (px, py) in enumerate(peers)]


def _gather_split_copies(ins, outs, send_sems, recv_sems):
    x, y, c = _place()
    me = 2 * x + y
    chips = [(1 - x, y), (x, 1 - y), (1 - x, 1 - y)]
    fetch, passed = [], []
    for a in range(len(ins)):
        half = ins[a].shape[0] // 2
        mine = pl.ds(c * half, half)
        for r, (px, py) in enumerate(chips):
            fetch.append(pltpu.make_async_remote_copy(
                src_ref=ins[a].at[mine], dst_ref=outs[a].at[me, mine], send_sem=send_sems.at[6 * a + r],
                recv_sem=recv_sems.at[6 * a + r], device_id=(px, py, c), device_id_type=MESH))
            landed = outs[a].at[2 * px + py, mine]
            passed.append(pltpu.make_async_remote_copy(
                src_ref=landed, dst_ref=landed, send_sem=send_sems.at[6 * a + 3 + r],
                recv_sem=recv_sems.at[6 * a + 3 + r], device_id=(x, y, 1 - c), device_id_type=MESH))
    return fetch, passed


def _pair_copies(ins, outs, send_sems, recv_sems):
    x, y, c = _place()
    copies = []
    for a in range(len(ins)):
        half = ins[a].shape[1] // 2
        copies.append(pltpu.make_async_remote_copy(
            src_ref=ins[a].at[:, pl.ds((1 - c) * half, half), :], dst_ref=outs[a], send_sem=send_sems.at[a],
            recv_sem=recv_sems.at[a], device_id=(x, y, 1 - c), device_id_type=MESH))
    return copies


def _host(body, grid, n_in, n_out, n_x, make_copies, mid=None):
    if not n_x:
        return body

    def hosting(*refs):
        a, b = n_in + n_x, n_in + 2 * n_x + n_out
        copies = make_copies(refs[n_in:a], refs[a + n_out:b], refs[-2], refs[-1])
        stages = copies if isinstance(copies, tuple) else (copies,)
        ids = [pl.program_id(d) for d in range(len(grid))]
        at = lambda step: functools.reduce(jnp.logical_and, [i == s for i, s in zip(ids, step)])

        @pl.when(at([0] * len(grid)))
        def _():
            for cp in stages[0]:
                cp.start()

        if len(stages) == 2:
            @pl.when(at(mid))
            def _():
                for cp in stages[0]:
                    cp.wait_recv()
                for cp in stages[1]:
                    cp.start()

        body(*refs[:n_in], *refs[a:a + n_out], *refs[b:-2])

        @pl.when(at([g - 1 for g in grid]))
        def _():
            if len(stages) == 2:
                for cp in stages[1]:
                    cp.wait_recv()
                for cp in stages[0] + stages[1]:
                    cp.wait_send()
            else:
                for cp in stages[0]:
                    cp.wait()

    return hosting


def _hosted_sems(n_x, per_array=3):
    n = per_array * n_x
    return [pltpu.SemaphoreType.DMA((n,)), pltpu.SemaphoreType.DMA((n,))] if n_x else []


RC = 256


def _chunk_causal(r):
    row = lax.broadcasted_iota(jnp.int32, (RC, AQ), 0) + (r * RC) % AQ
    return lax.broadcasted_iota(jnp.int32, (RC, AQ), 1) < row


def _attn_fwd(u, agw, shards=()):
    n = len(shards)
    grid = (T // AQ,)

    def body(q_ref, k_ref, v_ref, g_ref, ag_ref, o_ref, y_ref, tot_ref,
             kb_s, vb_s, qs_s, z_s, zs_s, lmb_s, suf_s, att_s, acc_s, run_s):
        qi = pl.program_id(0)

        @pl.when(qi == 0)
        def _():
            kb_s[...] = k_ref[...].astype(BF16)
            vb_s[...] = v_ref[...].astype(BF16)

        _, tr, tc, heads = _attn_tiles()
        upper = (tr > tc).astype(BF16)
        for g in range(NG):
            qs_s[g] = _stack_heads(q_ref[:, g * GW:(g + 1) * GW] * 0.125, heads).astype(BF16)
        acc_s[...] = jnp.zeros_like(acc_s)
        run_s[...] = jnp.zeros_like(run_s)

        def block(kb, masked):
            k0 = pl.multiple_of(kb * AQ, AQ)
            for g in range(NG):
                lanes = pl.ds(g * GW, GW)
                z_s[g] = _dot_nt(qs_s[g], kb_s[pl.ds(k0, AQ), lanes])
                for r in range(SR // RC):
                    rows = pl.ds(r * RC, RC)
                    z = z_s[g, rows, :]
                    zs = jnp.minimum(z, 0.0) - jnp.log(1.0 + jnp.exp(-jnp.abs(z)))
                    lm = zs - z
                    if masked:
                        lm = jnp.where(_chunk_causal(r), lm, 0.0)
                    run = run_s[g, rows, :]
                    zs_s[g, rows, :] = zs + run[:, 0:1]
                    hi = lm.astype(BF16)
                    lmb_s[g, rows, :] = hi
                    lmb_s[g, pl.ds(SR + r * RC, RC), :] = (lm - hi.astype(F32)).astype(BF16)
                    run_s[g, rows, :] = run + jnp.sum(lm, axis=1, keepdims=True)
                suf_s[g] = _dot(lmb_s[g], upper)
                for r in range(SR // RC):
                    rows = pl.ds(r * RC, RC)
                    att = jnp.exp(zs_s[g, rows, :] + suf_s[g, rows, :] + suf_s[g, pl.ds(SR + r * RC, RC), :])
                    if masked:
                        att = jnp.where(_chunk_causal(r), att, 0.0)
                    att_s[g, rows, :] = att.astype(BF16)
                acc_s[g] += _dot(att_s[g], vb_s[pl.ds(k0, AQ), lanes])

        block(qi, True)

        def step(i, c):
            block(qi - 1 - i, False)
            return c

        lax.fori_loop(0, qi, step, 0)
        gate = g_ref[...]
        agv = ag_ref[...]
        for g in range(NG):
            lanes = slice(g * GW, (g + 1) * GW)
            o = _unstack_heads(acc_s[g], heads)
            osq = o * o
            ms = jnp.zeros((AQ, GW), F32)
            for m in heads:
                ms = jnp.where(m, jnp.sum(jnp.where(m, osq, 0.0), axis=1, keepdims=True), ms)
            gg = gate[:, lanes]
            o_ref[:, lanes] = o
            y_ref[:, lanes] = (o * lax.rsqrt(ms * (1.0 / DH) + EPS) * agv[:, lanes] * (gg * _sig(gg))).astype(BF16)
            tot_ref[g] = _unstack_heads(jnp.broadcast_to(run_s[g][:, 0:1], (SR, GW)), heads)

    tile = lambda dt, rows=SR: pltpu.VMEM((NG, rows, AQ), dt)
    scratch = [pltpu.VMEM((T, AD), BF16), pltpu.VMEM((T, AD), BF16), pltpu.VMEM((NG, SR, GW), BF16),
               tile(F32), tile(F32), tile(BF16, 2 * SR), tile(F32, 2 * SR), tile(BF16),
               pltpu.VMEM((NG, SR, GW), F32), pltpu.VMEM((NG, SR, LANE), F32)]
    col = lambda j: pl.BlockSpec((AQ, AD), lambda qi: (qi, j))
    res = pl.pallas_call(
        _host(body, grid, 5, 3, n, _gather_split_copies, mid=(grid[0] * 3 // 4,)), name="attn_fwd", grid=grid,
        in_specs=[col(0), pl.BlockSpec((T, AD), lambda qi: (0, 1)), pl.BlockSpec((T, AD), lambda qi: (0, 2)),
                  col(3), pl.BlockSpec((1, AD), lambda qi: (0, 0))] + [HBM_SPEC] * n,
        out_specs=[col(0), col(0), pl.BlockSpec((NG, AQ, GW), lambda qi: (0, qi, 0))] + [HBM_SPEC] * n,
        out_shape=[jax.ShapeDtypeStruct((T, AD), F32), jax.ShapeDtypeStruct((T, AD), BF16),
                   jax.ShapeDtypeStruct((NG, T, GW), F32)]
        + [jax.ShapeDtypeStruct((NCHIP,) + s.shape, s.dtype) for s in shards],
        scratch_shapes=scratch + _hosted_sems(n, 6),
        compiler_params=_cp(("arbitrary",), VMEM_BIG),
    )(u, u, u, u, agw, *shards)
    return res[0], res[1], res[2], list(res[3:])


def _glu_conv(u, dw, db):
    tr = 256

    def body(cv_ref, cg_ref, w_ref, b_ref, c1_ref, pad_s):
        pad_s[pl.ds(0, CWP), :] = jnp.zeros((CWP, LANE), F32)
        pad_s[pl.ds(CWP, T), :] = cv_ref[...] * _sig(cg_ref[...])
        wv = w_ref[0]
        bias = b_ref[...]

        def tile(i, carry):
            r0 = pl.multiple_of(i * tr, tr)
            acc = jnp.zeros((tr, LANE), F32) + bias
            for w in range(CW):
                acc = acc + pad_s[pl.ds(r0 + (CWP - CW + 1) + w, tr), :] * wv[w:w + 1, :]
            c1_ref[pl.ds(r0, tr), :] = acc
            return carry

        lax.fori_loop(0, T // tr, tile, 0)

    return pl.pallas_call(
        body, name="glu_conv", grid=(CD // LANE,),
        in_specs=[pl.BlockSpec((T, LANE), lambda cb: (0, 16 + cb)),
                  pl.BlockSpec((T, LANE), lambda cb: (0, 20 + cb)),
                  pl.BlockSpec((1, CWP, LANE), lambda cb: (cb, 0, 0)),
                  pl.BlockSpec((1, LANE), lambda cb: (0, cb))],
        out_specs=pl.BlockSpec((T, LANE), lambda cb: (0, cb)),
        out_shape=jax.ShapeDtypeStruct((T, CD), F32),
        scratch_shapes=[pltpu.VMEM((T + CWP, LANE), F32)],
        compiler_params=_cp(("arbitrary",)),
    )(u, u, dw, db)


def _ln_silu(c1, lg, lb):
    mu = jnp.mean(c1, axis=-1, keepdims=True)
    xc = c1 - mu
    rs = lax.rsqrt(jnp.mean(xc * xc, axis=-1, keepdims=True) + EPS)
    xh = xc * rs
    ln = xh * lg + lb
    s = _sig(ln)
    return xh, rs, ln, s


def _layer_tail(c1, u, ya, h, p, lg, lb, wpw, cg, wout, pg, wgate, wple):
    tm = 256

    def body(c1_ref, gc_ref, ya_ref, h_ref, p_ref, lg_ref, lb_ref, wpw_ref, cg_ref, wout_ref,
             pg_ref, wgate_ref, wple_ref, c3_ref, yc_ref, h1_ref, gate_ref, pe_ref, h2_ref):
        _, _, ln, s = _ln_silu(c1_ref[...], lg_ref[...], lb_ref[...])
        c2 = (ln * s).astype(BF16)
        c3 = _dot(c2, wpw_ref[...])
        gc = gc_ref[...]
        yc = (c3 * _rstd(c3) * cg_ref[...] * (gc * _sig(gc))).astype(BF16)
        c3_ref[...] = c3
        yc_ref[...] = yc
        y = _dot(ya_ref[...], wout_ref[pl.ds(0, AD), :]) + _dot(yc, wout_ref[pl.ds(AD, CD), :])
        h1 = h_ref[...] + y
        hn2 = (h1 * _rstd(h1) * pg_ref[...]).astype(BF16)
        gate = _sig(_dot(hn2, wgate_ref[...]))
        pb = p_ref[...].astype(BF16)
        pe = jnp.concatenate([_dot(pb, wple_ref[k]) for k in range(NCHIP)], axis=1)
        h1_ref[...] = h1
        gate_ref[...] = gate.astype(BF16)
        pe_ref[...] = pe.astype(BF16)
        h2_ref[...] = h1 + pe * gate

    row = lambda w: pl.BlockSpec((tm, w), lambda i: (i, 0))
    full = lambda *s: pl.BlockSpec(s, lambda i: (0,) * len(s))
    return pl.pallas_call(
        body, name="layer_tail", grid=(T // tm,),
        in_specs=[row(CD), pl.BlockSpec((tm, CD), lambda i: (i, 6)), row(AD), row(D), row(PLE),
                  full(1, CD), full(1, CD), full(CD, CD), full(1, CD), full(D, D),
                  full(1, D), full(D, D), full(NCHIP, PLE, PLE)],
        out_specs=[row(CD), row(CD), row(D), row(D), row(D), row(D)],
        out_shape=[jax.ShapeDtypeStruct((T, CD), F32), jax.ShapeDtypeStruct((T, CD), BF16),
                   jax.ShapeDtypeStruct((T, D), F32), jax.ShapeDtypeStruct((T, D), BF16),
                   jax.ShapeDtypeStruct((T, D), BF16), jax.ShapeDtypeStruct((T, D), F32)],
        compiler_params=_cp(("arbitrary",), VMEM_BIG),
    )(c1, u, ya, h, p, lg, lb, wpw, cg, wout, pg, wgate, wple)


def _loss_head(h, tgt, fg):
    tm = 256

    def body(h_ref, t_ref, g_ref, loss_ref, dh_ref, dg_ref):
        @pl.when(pl.program_id(0) == 0)
        def _():
            loss_ref[...] = jnp.zeros_like(loss_ref)
            dg_ref[...] = jnp.zeros_like(dg_ref)
        hh = h_ref[...]
        g = g_ref[...]
        r = _rstd(hh)
        e = hh * r * g - t_ref[...]
        loss_ref[...] += 0.5 * jnp.sum(jnp.mean(e * e, axis=-1, keepdims=True))
        dy = e * (1.0 / D)
        dg_ref[...] += jnp.sum(dy * hh * r, axis=0, keepdims=True)
        dh_ref[...] = _rms_bwd(dy, hh, r, g)

    return pl.pallas_call(
        body, name="loss_head", grid=(T // tm,),
        in_specs=[pl.BlockSpec((tm, D), lambda i: (i, 0)), pl.BlockSpec((tm, D), lambda i: (i, 0)),
                  pl.BlockSpec((1, D), lambda i: (0, 0))],
        out_specs=[pl.BlockSpec((8, LANE), lambda i: (0, 0)), pl.BlockSpec((tm, D), lambda i: (i, 0)),
                   pl.BlockSpec((1, D), lambda i: (0, 0))],
        out_shape=[jax.ShapeDtypeStruct((8, LANE), F32), jax.ShapeDtypeStruct((T, D), F32),
                   jax.ShapeDtypeStruct((1, D), F32)],
        compiler_params=_cp(("arbitrary",)),
    )(h, tgt, fg)


def _ple_out_bwd(dh2, h1, gate, pe, p, ya, yc, pg, wgate, wout):
    tm = 256

    def body(dh2_ref, h1_ref, gate_ref, pe_ref, p_ref, ya_ref, yc_ref, pg_ref, wgate_ref, wout_ref,
             dh1_ref, dy_ref, dwg_ref, dwp_ref, dwo_ref, dpg_ref):
        @pl.when(pl.program_id(0) == 0)
        def _():
            dwg_ref[...] = jnp.zeros_like(dwg_ref)
            dwp_ref[...] = jnp.zeros_like(dwp_ref)
            dwo_ref[...] = jnp.zeros_like(dwo_ref)
            dpg_ref[...] = jnp.zeros_like(dpg_ref)
        dh2 = dh2_ref[...]
        h1 = h1_ref[...]
        gate = gate_ref[...].astype(F32)
        pg = pg_ref[...]
        dpe = (dh2 * gate).astype(BF16)
        dgp = (dh2 * pe_ref[...].astype(F32) * gate * (1.0 - gate)).astype(BF16)
        r = _rstd(h1)
        hn = h1 * r
        dwg_ref[...] += _dot_tn((hn * pg).astype(BF16), dgp)
        dhn2 = _dot_nt(dgp, wgate_ref[...])
        dpg_ref[...] += jnp.sum(dhn2 * hn, axis=0, keepdims=True)
        dh1 = dh2 + _rms_bwd(dhn2, h1, r, pg)
        pb = p_ref[...].astype(BF16)
        for k in range(NCHIP):
            dwp_ref[k] += _dot_tn(pb, dpe[:, k * PLE:(k + 1) * PLE])
        dh1b = dh1.astype(BF16)
        dy_ref[...] = _dot_nt(dh1b, wout_ref[...])
        dwo_ref[pl.ds(0, AD), :] += _dot_tn(ya_ref[...], dh1b)
        dwo_ref[pl.ds(AD, CD), :] += _dot_tn(yc_ref[...], dh1b)
        dh1_ref[...] = dh1

    row = lambda w: pl.BlockSpec((tm, w), lambda i: (i, 0))
    full = lambda *s: pl.BlockSpec(s, lambda i: (0,) * len(s))
    return pl.pallas_call(
        body, name="ple_out_bwd", grid=(T // tm,),
        in_specs=[row(D), row(D), row(D), row(D), row(PLE), row(AD), row(CD),
                  full(1, D), full(D, D), full(D, D)],
        out_specs=[row(D), row(D), full(D, D), full(NCHIP, PLE, PLE), full(D, D), full(1, D)],
        out_shape=[jax.ShapeDtypeStruct((T, D), F32), jax.ShapeDtypeStruct((T, D), F32),
                   jax.ShapeDtypeStruct((D, D), F32), jax.ShapeDtypeStruct((NCHIP, PLE, PLE), F32),
                   jax.ShapeDtypeStruct((D, D), F32), jax.ShapeDtypeStruct((1, D), F32)],
        compiler_params=_cp(("arbitrary",), VMEM_BIG),
    )(dh2, h1, gate, pe, p, ya, yc, pg, wgate, wout)


def _branch_bwd(dy, o, u, c1, c3, ag, lg, lb, wpw, cg, seg):
    tm = 256

    def body(dya_ref, dyc_ref, o_ref, ga_ref, gc_ref, c1_ref, c3_ref, ag_ref, lg_ref, lb_ref, wpw_ref,
             cg_ref, seg_ref, do_ref, dga_ref, dgc_ref, dc1_ref, dwpw_ref, dag_ref, dcg_ref, dlg_ref, dlb_ref):
        @pl.when(pl.program_id(0) == 0)
        def _():
            for r_ in (dwpw_ref, dag_ref, dcg_ref, dlg_ref, dlb_ref):
                r_[...] = jnp.zeros_like(r_)
        dya = dya_ref[...]
        o = o_ref[...]
        ga = ga_ref[...]
        ag_v = ag_ref[...]
        seg_m = seg_ref[...]
        r = lax.rsqrt(_dot2(o * o, seg_m) * (1.0 / DH) + EPS)
        onr = o * r
        sg = _sig(ga)
        dga_ref[...] = (dya * (onr * ag_v) * (sg * (1.0 + ga * (1.0 - sg)))).astype(BF16)
        don = dya * (ga * sg)
        dag_ref[...] += jnp.sum(don * onr, axis=0, keepdims=True)
        dn = don * ag_v
        do_ref[...] = r * dn - o * (r * r * r) * (_dot2(dn * o, seg_m) * (1.0 / DH))
        dyc = dyc_ref[...]
        c3 = c3_ref[...]
        gc = gc_ref[...]
        cg_v = cg_ref[...]
        r3 = _rstd(c3)
        cn = c3 * r3
        sc = _sig(gc)
        dgc_ref[...] = (dyc * (cn * cg_v) * (sc * (1.0 + gc * (1.0 - sc)))).astype(BF16)
        dcn = dyc * (gc * sc)
        dcg_ref[...] += jnp.sum(dcn * cn, axis=0, keepdims=True)
        dc3 = _rms_bwd(dcn, c3, r3, cg_v).astype(BF16)
        lg_v = lg_ref[...]
        xh, rs, ln, s = _ln_silu(c1_ref[...], lg_v, lb_ref[...])
        c2 = (ln * s).astype(BF16)
        dwpw_ref[...] += _dot_tn(c2, dc3)
        dc2 = _dot_nt(dc3, wpw_ref[...])
        dln = dc2 * (s * (1.0 + ln * (1.0 - s)))
        dlb_ref[...] += jnp.sum(dln, axis=0, keepdims=True)
        dlg_ref[...] += jnp.sum(dln * xh, axis=0, keepdims=True)
        dxh = dln * lg_v
        dc1_ref[...] = rs * (dxh - jnp.mean(dxh, axis=-1, keepdims=True)
                             - xh * jnp.mean(dxh * xh, axis=-1, keepdims=True))

    half = lambda j: pl.BlockSpec((tm, 512), lambda i: (i, j))
    full = lambda *s: pl.BlockSpec(s, lambda i: (0,) * len(s))
    vec = jax.ShapeDtypeStruct((1, 512), F32)
    act = jax.ShapeDtypeStruct((T, 512), F32)
    return pl.pallas_call(
        body, name="branch_bwd", grid=(T // tm,),
        in_specs=[half(0), half(1), half(0), half(3), half(6), half(0), half(0),
                  full(1, AD), full(1, CD), full(1, CD), full(CD, CD), full(1, CD), full(AD, AD)],
        out_specs=[half(0), half(0), half(0), half(0), full(CD, CD), full(1, 512), full(1, 512),
                   full(1, 512), full(1, 512)],
        out_shape=[act, jax.ShapeDtypeStruct((T, 512), BF16), jax.ShapeDtypeStruct((T, 512), BF16), act,
                   jax.ShapeDtypeStruct((CD, CD), F32), vec, vec, vec, vec],
        compiler_params=_cp(("arbitrary",), VMEM_BIG),
    )(dy, dy, o, u, u, c1, c3, ag, lg, lb, wpw, cg, seg)


def _conv_bwd(dc1, u, dw):
    tr = 64
    off = CWP - CW + 1

    def body(d_ref, cv_ref, cg_ref, w_ref, dcv_ref, dcg_ref, ddw_ref, ddb_ref, padc_s, padd_s, acc_s):
        cv = cv_ref[...]
        sg = _sig(cg_ref[...])
        padc_s[pl.ds(0, CWP), :] = jnp.zeros((CWP, LANE), F32)
        padc_s[pl.ds(CWP, T), :] = cv * sg
        padd_s[pl.ds(0, T), :] = d_ref[...]
        padd_s[pl.ds(T, CWP), :] = jnp.zeros((CWP, LANE), F32)
        acc_s[...] = jnp.zeros_like(acc_s)
        wv = w_ref[0]

        def tile(i, carry):
            r0 = pl.multiple_of(i * tr, tr)
            dt = padd_s[pl.ds(r0, tr), :]
            dc0 = jnp.zeros((tr, LANE), F32)
            for w in range(CW):
                dc0 = dc0 + padd_s[pl.ds(r0 + (CW - 1) - w, tr), :] * wv[w:w + 1, :]
                prod = dt * padc_s[pl.ds(r0 + off + w, tr), :]
                acc_s[w] += jnp.sum(prod.reshape(tr // 8, 8, LANE), axis=0)
            cvt = cv_ref[pl.ds(r0, tr), :]
            sgt = _sig(cg_ref[pl.ds(r0, tr), :])
            dcv_ref[pl.ds(r0, tr), :] = (dc0 * sgt).astype(BF16)
            dcg_ref[pl.ds(r0, tr), :] = (dc0 * cvt * sgt * (1.0 - sgt)).astype(BF16)
            return carry

        lax.fori_loop(0, T // tr, tile, 0)
        ddw_ref[0] = jnp.sum(acc_s[...], axis=1)
        ddb_ref[...] = jnp.sum(d_ref[...], axis=0, keepdims=True)

    col = lambda j: pl.BlockSpec((T, LANE), lambda cb: (0, j + cb))
    return pl.pallas_call(
        body, name="conv_bwd", grid=(CD // LANE,),
        in_specs=[col(0), col(16), col(20), pl.BlockSpec((1, CWP, LANE), lambda cb: (cb, 0, 0))],
        out_specs=[col(0), col(0), pl.BlockSpec((1, CWP, LANE), lambda cb: (cb, 0, 0)),
                   pl.BlockSpec((1, LANE), lambda cb: (0, cb))],
        out_shape=[jax.ShapeDtypeStruct((T, CD), BF16), jax.ShapeDtypeStruct((T, CD), BF16),
                   jax.ShapeDtypeStruct((NCHIP, CWP, LANE), F32), jax.ShapeDtypeStruct((1, CD), F32)],
        scratch_shapes=[pltpu.VMEM((T + CWP, LANE), F32), pltpu.VMEM((T + CWP, LANE), F32),
                        pltpu.VMEM((CWP, 8, LANE), F32)],
        compiler_params=_cp(("arbitrary",)),
    )(dc1, u, u, dw)


def _attn_bwd(u, do, tot, partials=()):
    n_x = len(partials)
    grid = (T // AQ,)

    def body(q_ref, k_ref, v_ref, do_ref, tot_ref, dq_ref, dk_ref, dv_ref, kb_s, vb_s, dk_s, dv_s):
        qi = pl.program_id(0)

        @pl.when(qi == 0)
        def _():
            kb_s[...] = k_ref[...].astype(BF16)
            vb_s[...] = v_ref[...].astype(BF16)
            dk_s[...] = jnp.zeros_like(dk_s)
            dv_s[...] = jnp.zeros_like(dv_s)

        causal, tr, tc, heads = _attn_tiles()
        upper = (tr > tc).astype(BF16)
        lower = (tr < tc).astype(BF16)
        qs, qus, dos, tots = [], [], [], []
        for g in range(NG):
            lanes = slice(g * GW, (g + 1) * GW)
            q = q_ref[:, lanes]
            qs.append(_stack_heads(q * 0.125, heads).astype(BF16))
            qus.append(_stack_heads(q, heads).astype(BF16))
            dos.append(_stack_heads(do_ref[:, lanes], heads).astype(BF16))
            totv = tot_ref[g]
            tots.append(jnp.concatenate([totv[:, h * DH:h * DH + 1] for h in range(HG)], axis=0))

        def block(kb, carry, masked):
            k0 = pl.multiple_of(kb * AQ, AQ)
            out = []
            for g in range(NG):
                lanes = pl.ds(g * GW, GW)
                lm_left, dl_left, dq = carry[g]
                kk = kb_s[pl.ds(k0, AQ), lanes]
                vv = vb_s[pl.ds(k0, AQ), lanes]
                z = _dot_nt(qs[g], kk)
                sp = _softplus(z)
                lm = jnp.where(causal, -sp, 0.0) if masked else -sp
                lm_incl = lm_left + jnp.sum(lm, axis=1, keepdims=True)
                att = jnp.exp((z - sp) + _tri_sum(lm, upper) + (tots[g] - lm_incl))
                if masked:
                    att = jnp.where(causal, att, 0.0)
                dl = att * _dot_nt(dos[g], vv)
                dv_s[pl.ds(k0, AQ), lanes] += _dot_tn(att.astype(BF16), dos[g])
                prefix = dl_left + _tri_sum(dl, lower)
                beta = jnp.exp(z - sp)
                dz = (1.0 - beta) * dl - beta * prefix
                if masked:
                    dz = jnp.where(causal, dz, 0.0)
                dzs = (dz * 0.125).astype(BF16)
                dk_s[pl.ds(k0, AQ), lanes] += _dot_tn(dzs, qus[g])
                out.append((lm_incl, dl_left + jnp.sum(dl, axis=1, keepdims=True), dq + _dot(dzs, kk)))
            return tuple(out)

        zero = jnp.zeros((SR, 1), F32)
        init = tuple((zero, zero, jnp.zeros((SR, GW), F32)) for _ in range(NG))
        carry = lax.fori_loop(0, qi, lambda kb, c: block(kb, c, False), init)
        carry = block(qi, carry, True)
        for g in range(NG):
            dq_ref[:, g * GW:(g + 1) * GW] = _unstack_heads(carry[g][2], heads).astype(BF16)

        @pl.when(qi == grid[0] - 1)
        def _():
            dk_ref[...] = dk_s[...].astype(BF16)
            dv_ref[...] = dv_s[...].astype(BF16)

    col = lambda j: pl.BlockSpec((AQ, AD), lambda qi: (qi, j))
    whole = lambda j: pl.BlockSpec((T, AD), lambda qi: (0, j), pipeline_mode=pl.Buffered(1))
    res = pl.pallas_call(
        _host(body, grid, 5, 3, n_x, _scatter_copies), name="attn_bwd", grid=grid,
        in_specs=[col(0), whole(1), whole(2), col(0), pl.BlockSpec((NG, AQ, GW), lambda qi: (0, qi, 0))]
        + [HBM_SPEC] * n_x,
        out_specs=[col(0), whole(0), whole(0)] + [HBM_SPEC] * n_x,
        out_shape=[jax.ShapeDtypeStruct((T, AD), BF16)] * 3
        + [jax.ShapeDtypeStruct((NCHIP - 1,) + a.shape[1:], a.dtype) for a in partials],
        scratch_shapes=[pltpu.VMEM((T, AD), BF16), pltpu.VMEM((T, AD), BF16), pltpu.VMEM((T, AD), F32),
                        pltpu.VMEM((T, AD), F32)] + _hosted_sems(n_x),
        compiler_params=_cp(("arbitrary",), VMEM_BIG),
    )(u, u, u, do, tot, *partials)
    return res[0], res[1], res[2], list(res[3:])


def _inproj_dw(hn, du):
    tm = min(TM, T)

    def body(hn_ref, du_ref, dw_ref):
        @pl.when(pl.program_id(1) == 0)
        def _():
            dw_ref[...] = jnp.zeros_like(dw_ref)
        dw_ref[0] += _dot_tn(hn_ref[...], du_ref[...])

    return pl.pallas_call(
        body, name="inproj_dw", grid=(NCHIP, T // tm),
        in_specs=[pl.BlockSpec((tm, D), lambda k, i: (i, 0)), pl.BlockSpec((tm, SHW), lambda k, i: (i, k))],
        out_specs=pl.BlockSpec((1, D, SHW), lambda k, i: (k, 0, 0)),
        out_shape=jax.ShapeDtypeStruct((NCHIP, D, SHW), F32),
        compiler_params=_cp(("arbitrary", "arbitrary"), VMEM_BIG),
    )(hn, du)


def _inproj_dx(du, w, h, g, dres, partials=(), grads=()):
    tm = min(TM, T)
    sent = list(partials) + list(grads)
    n_x = len(sent)
    grid = (T // tm, NCHIP)
    if grads:
        landing = [jax.ShapeDtypeStruct((NCHIP, a.shape[1] // 2, a.shape[2]), F32) for a in grads]
    else:
        landing = [jax.ShapeDtypeStruct((NCHIP - 1,) + a.shape[1:], a.dtype) for a in partials]

    def body(du_ref, w_ref, h_ref, g_ref, dres_ref, dh_ref, dg_ref, acc_s):
        i, k = pl.program_id(0), pl.program_id(1)

        @pl.when(jnp.logical_and(i == 0, k == 0))
        def _():
            dg_ref[...] = jnp.zeros_like(dg_ref)

        @pl.when(k == 0)
        def _():
            acc_s[...] = _dot_nt(du_ref[...], w_ref[0])

        @pl.when(k > 0)
        def _():
            acc_s[...] += _dot_nt(du_ref[...], w_ref[0])

        @pl.when(k == NCHIP - 1)
        def _():
            hh = h_ref[...]
            r = _rstd(hh)
            dhn = acc_s[...]
            dg_ref[...] += jnp.sum(dhn * hh * r, axis=0, keepdims=True)
            dh_ref[...] = dres_ref[...] + _rms_bwd(dhn, hh, r, g_ref[...])

    res = pl.pallas_call(
        _host(body, grid, 5, 2, n_x, _pair_copies if grads else _scatter_copies), name="inproj_dx", grid=grid,
        in_specs=[pl.BlockSpec((tm, SHW), lambda i, k: (i, k)),
                  pl.BlockSpec((1, D, SHW), lambda i, k: (k, 0, 0)),
                  pl.BlockSpec((tm, D), lambda i, k: (i, 0)),
                  pl.BlockSpec((1, D), lambda i, k: (0, 0)),
                  pl.BlockSpec((tm, D), lambda i, k: (i, 0))] + [HBM_SPEC] * n_x,
        out_specs=[pl.BlockSpec((tm, D), lambda i, k: (i, 0)), pl.BlockSpec((1, D), lambda i, k: (0, 0))]
        + [HBM_SPEC] * n_x,
        out_shape=[jax.ShapeDtypeStruct((T, D), F32), jax.ShapeDtypeStruct((1, D), F32)] + landing,
        scratch_shapes=[pltpu.VMEM((tm, D), F32)] + _hosted_sems(n_x),
        compiler_params=_cp(("arbitrary", "arbitrary"), VMEM_BIG),
    )(du, w, h, g, dres, *sent)
    return res[0], res[1], list(res[2:])


def _sum_pair(core, grads, gots):
    n = len(grads)

    def body(c_ref, *refs):
        for a in range(n):
            refs[2 * n + a][...] = (refs[a][...] + refs[n + a][...]).astype(BF16)

    mine = [pl.BlockSpec((1,) + s.shape[1:], lambda k, c: (k, c[0], 0)) for s in gots]
    same = [pl.BlockSpec((1,) + s.shape[1:], lambda k, c: (k, 0, 0)) for s in gots]
    return pl.pallas_call(
        body, name="sum_pair",
        grid_spec=pltpu.PrefetchScalarGridSpec(
            num_scalar_prefetch=1, grid=(NCHIP,), in_specs=mine + same, out_specs=same),
        out_shape=[jax.ShapeDtypeStruct(s.shape, BF16) for s in gots],
        compiler_params=_cp(("arbitrary",), VMEM_BIG),
    )(core, *grads, *gots)


def _sum_chips_share(chip, partials, gots):
    n = len(partials)

    def body(c_ref, *refs):
        mine, theirs = refs[2 * n:3 * n], refs[3 * n:4 * n]
        for a in range(n):
            acc = refs[a][0].astype(F32)
            for j in range(NCHIP - 1):
                acc = acc + refs[n + a][j].astype(F32)
            mine[a][...] = acc
        x, y, c = _place()
        copies = [pltpu.make_async_remote_copy(
            src_ref=mine[a], dst_ref=theirs[a], send_sem=refs[4 * n].at[a], recv_sem=refs[4 * n + 1].at[a],
            device_id=(x, y, 1 - c), device_id_type=MESH) for a in range(n)]
        for cp in copies:
            cp.start()
        for cp in copies:
            cp.wait()

    halves = [jax.ShapeDtypeStruct(s.shape[1:], F32) for s in partials]
    res = pl.pallas_call(
        body, name="sum_chips_share",
        grid_spec=pltpu.PrefetchScalarGridSpec(
            num_scalar_prefetch=1, grid=(1,),
            in_specs=[pl.BlockSpec((1,) + s.shape[1:], lambda i, c: (c[0], 0, 0)) for s in partials]
            + [pl.BlockSpec(s.shape, lambda i, c: (0, 0, 0)) for s in gots],
            out_specs=[pl.BlockSpec(s.shape[1:], lambda i, c: (0, 0)) for s in partials] + [HBM_SPEC] * n,
            scratch_shapes=[pltpu.SemaphoreType.DMA((n,)), pltpu.SemaphoreType.DMA((n,))]),
        out_shape=halves + halves,
        compiler_params=_cp(("arbitrary",), VMEM_BIG),
    )(chip, *partials, *gots)
    return res[:n], res[n:]


def _adamw(w, g, m, v, rows):
    R, C = w.shape
    c1 = 1.0 - ADAM_B1 ** ADAM_STEP
    c2 = 1.0 - ADAM_B2 ** ADAM_STEP

    def body(w_ref, g_ref, m_ref, v_ref, d_ref, nm_ref, nv_ref):
        gg = g_ref[...]
        nm = ADAM_B1 * m_ref[...] + (1.0 - ADAM_B1) * gg
        nv = ADAM_B2 * v_ref[...] + (1.0 - ADAM_B2) * (gg * gg)
        d_ref[...] = -ADAM_LR * ((nm / c1) / (jnp.sqrt(nv / c2) + ADAM_EPS) + ADAM_WD * w_ref[...])
        nm_ref[...] = nm
        nv_ref[...] = nv

    spec = pl.BlockSpec((rows, C), lambda i: (i, 0))
    sh = jax.ShapeDtypeStruct((R, C), F32)
    return pl.pallas_call(
        body, name="adamw", grid=(R // rows,), in_specs=[spec] * 4, out_specs=[spec] * 3,
        out_shape=[sh, sh, sh], compiler_params=_cp(("arbitrary",)),
    )(w, g, m, v)


HBM_SPEC = pl.BlockSpec(memory_space=pltpu.HBM)


def _place():
    return lax.axis_index("x"), lax.axis_index("y"), lax.axis_index("c")


def _all_gather_split(shard):
    def body(in_ref, out_ref, send_sems, recv_sems):
        fetch, passed = _gather_split_copies([in_ref], [out_ref], send_sems, recv_sems)
        for cp in fetch:
            cp.start()
        for got, onward in zip(fetch, passed):
            got.wait_recv()
            onward.start()
        for cp in passed:
            cp.wait_recv()
        for cp in fetch + passed:
            cp.wait_send()

    return pl.pallas_call(
        body, name="all_gather_split", in_specs=[HBM_SPEC], out_specs=HBM_SPEC,
        out_shape=jax.ShapeDtypeStruct((NCHIP,) + shard.shape, shard.dtype),
        scratch_shapes=[pltpu.SemaphoreType.DMA((6,)), pltpu.SemaphoreType.DMA((6,))],
    )(shard)


def _pair_exchange(grads):
    n = len(grads)

    def body(*refs):
        copies = _pair_copies(refs[:n], refs[n:2 * n], refs[2 * n], refs[2 * n + 1])
        for cp in copies:
            cp.start()
        for cp in copies:
            cp.wait()

    return pl.pallas_call(
        body, name="pair_exchange", in_specs=[HBM_SPEC] * n, out_specs=[HBM_SPEC] * n,
        out_shape=[jax.ShapeDtypeStruct((NCHIP, g.shape[1] // 2, g.shape[2]), F32) for g in grads],
        scratch_shapes=[pltpu.SemaphoreType.DMA((n,)), pltpu.SemaphoreType.DMA((n,))],
    )(*grads)


def _small_allreduce(mine):
    def body(m_ref, o_ref, slots, send_sems, recv_sems):
        x, y, c = _place()
        me = 4 * x + 2 * y + c
        slots[me] = m_ref[...]
        copies = []
        for r in range(1, 8):
            rx, ry, rc = (r >> 2) & 1, (r >> 1) & 1, r & 1
            peer = (x + rx - 2 * x * rx, y + ry - 2 * y * ry, c + rc - 2 * c * rc)
            cp = pltpu.make_async_remote_copy(
                src_ref=m_ref, dst_ref=slots.at[me], send_sem=send_sems.at[r - 1], recv_sem=recv_sems.at[r - 1],
                device_id=peer, device_id_type=MESH)
            cp.start()
            copies.append(cp)
        for cp in copies:
            cp.wait()
        acc = slots[0]
        for j in range(1, 8):
            acc = acc + slots[j]
        o_ref[...] = acc

    return pl.pallas_call(
        body, name="small_allreduce",
        in_specs=[pl.BlockSpec(memory_space=pltpu.VMEM)], out_specs=pl.BlockSpec(memory_space=pltpu.VMEM),
        out_shape=jax.ShapeDtypeStruct((SMALL_ROWS, LANE), F32),
        scratch_shapes=[pltpu.VMEM((8, SMALL_ROWS, LANE), F32), pltpu.SemaphoreType.DMA((7,)),
                        pltpu.SemaphoreType.DMA((7,))],
    )(mine)


def _seg_matrix():
    i = lax.broadcasted_iota(jnp.int32, (AD, AD), 0) // DH
    j = lax.broadcasted_iota(jnp.int32, (AD, AD), 1) // DH
    return (i == j).astype(BF16)


TAIL = ("w_out", "w_ple_gate", "w_ple", "w_pw", "dw_w")


def _local_step(x, p, tgt, sm, shards, chip, ci):
    seg = _seg_matrix()
    core = jnp.reshape(ci, (1,)).astype(jnp.int32)
    chip_idx = jnp.reshape(chip, (1,)).astype(jnp.int32)
    own = lambda g, s: lax.dynamic_update_index_in_dim(g, s, chip, 0)
    w_in_next = own(_all_gather_split(shards[0]["w_in"]), shards[0]["w_in"])
    h = x
    saved = []
    for l in range(DEPTH):
        w_in = w_in_next
        row = lambda name: sm[name][l:l + 1]
        u, hn = _rms_inproj(h, row("norm_g"), w_in)
        todo = [shards[l][k] for k in TAIL] + ([shards[l + 1]["w_in"]] if l + 1 < DEPTH else [])
        o, ya, tot, got = _attn_fwd(u, jnp.tile(row("attn_out_g"), (1, AD // DH)), todo)
        got = [own(g, s) for g, s in zip(got, todo)]
        w_out = got[0].reshape(D, D)
        w_gate = got[1].reshape(D, D)
        w_ple = got[2]
        w_pw = got[3].reshape(CD, CD)
        dw = got[4]
        if l + 1 < DEPTH:
            w_in_next = got[5]
        c1 = _glu_conv(u, dw, row("dw_b"))
        c3, yc, h1, gate, pe, h2 = _layer_tail(
            c1, u, ya, h, p[l], row("conv_ln_g"), row("conv_ln_b"), w_pw, row("conv_out_g"), w_out,
            row("ple_norm_g"), w_gate, w_ple)
        saved.append(dict(h=h, u=u, hn=hn, o=o, ya=ya, tot=tot, c1=c1, c3=c3, yc=yc, h1=h1, gate=gate, pe=pe,
                          w_in=w_in, w_out=w_out, w_gate=w_gate, w_pw=w_pw, dw=dw))
        h = h2
    loss_blk, dh, dfg = _loss_head(h, tgt, sm["final_g"])
    small = [None] * DEPTH
    pending, partials, arrived = [], {}, {}
    pair_sum = lambda grads: _sum_pair(core, grads, _pair_exchange(grads))
    for l in reversed(range(DEPTH)):
        s = saved[l]
        row = lambda name: sm[name][l:l + 1]
        dh1, dy, dwg, dwp, dwo, dpg = _ple_out_bwd(
            dh, s["h1"], s["gate"], s["pe"], p[l], s["ya"], s["yc"], row("ple_norm_g"), s["w_gate"], s["w_out"])
        ag_t = jnp.tile(row("attn_out_g"), (1, AD // DH))
        do, dga, dgc, dc1, dwpw, dag, dcg, dlg, dlb = _branch_bwd(
            dy, s["o"], s["u"], s["c1"], s["c3"], ag_t, row("conv_ln_g"), row("conv_ln_b"), s["w_pw"],
            row("conv_out_g"), seg)
        dcv, dcgate, ddw, ddb = _conv_bwd(dc1, s["u"], s["dw"])
        tail = [dwo.reshape(NCHIP, 256, D), dwg.reshape(NCHIP, 256, D), dwp, dwpw.reshape(NCHIP, 128, CD), ddw]
        if l == 0:
            partials[(l, "tail")] = pair_sum(tail)
            pending.append((l, "tail"))
        send = [t for key in pending for t in partials[key]]
        dq, dk, dv, got = _attn_bwd(s["u"], do, s["tot"], send)
        for key in pending:
            arrived[key], got = got[:len(partials[key])], got[len(partials[key]):]
        pending = []
        du = jnp.concatenate([dq, dk, dv, dga, dcv, dcgate, dgc], axis=1)
        dwin = _inproj_dw(s["hn"], du)
        if l == 0:
            partials[(l, "w_in")] = pair_sum([dwin])
            dh, dng, arrived[(l, "w_in")] = _inproj_dx(du, s["w_in"], s["h"], row("norm_g"), dh1,
                                                      partials[(l, "w_in")])
        else:
            dh, dng, halves = _inproj_dx(du, s["w_in"], s["h"], row("norm_g"), dh1, grads=tail + [dwin])
            tail_p = _sum_pair(core, tail + [dwin], halves)
            partials[(l, "tail")], partials[(l, "w_in")] = tail_p[:-1], tail_p[-1:]
            pending = [(l, "tail"), (l, "w_in")]
        small[l] = dict(norm_g=dng, attn_out_g=dag.reshape(AD // DH, DH).sum(axis=0, keepdims=True), dw_b=ddb,
                        conv_ln_g=dlg, conv_ln_b=dlb, conv_out_g=dcg, ple_norm_g=dpg)
    big = []
    for l in range(DEPTH):
        ps = partials[(l, "w_in")] + partials[(l, "tail")]
        mine, theirs = _sum_chips_share(chip_idx, ps, arrived[(l, "w_in")] + arrived[(l, "tail")])
        big.append(dict(zip(BIG, [jnp.where(core[0] == 0, jnp.concatenate([m, t]), jnp.concatenate([t, m]))
                                  for m, t in zip(mine, theirs)])))
    return loss_blk[0, 0], dh, big, small, dfg


BIG = ("w_in", "w_out", "w_ple_gate", "w_ple", "w_pw", "dw_w")
SMALL2 = ("norm_g", "ple_norm_g", "dw_b", "conv_ln_g", "conv_ln_b", "conv_out_g", "attn_out_g")


def _pack_small(two, final):
    flat = jnp.concatenate([two[k].reshape(-1) for k in SMALL2] + [final.reshape(-1)])
    flat = jnp.concatenate([flat, jnp.zeros((SMALL_ROWS * LANE - flat.shape[0],), F32)])
    return flat.reshape(SMALL_ROWS, LANE)


def _unpack_small(packed, like_two, like_final):
    flat = packed.reshape(-1)
    out, off = {}, 0
    for k in SMALL2:
        n = like_two[k].size
        out[k] = flat[off:off + n].reshape(like_two[k].shape)
        off += n
    return out, flat[off:off + like_final.size].reshape(like_final.shape)


def kernel(x, p, norm_g, w_in, attn_out_g, dw_w, dw_b, conv_ln_g, conv_ln_b, w_pw, conv_out_g, w_out, ple_norm_g, w_ple_gate, w_ple, final_g, loss_target, m_norm_g, m_w_in, m_attn_out_g, m_dw_w, m_dw_b, m_conv_ln_g, m_conv_ln_b, m_w_pw, m_conv_out_g, m_w_out, m_ple_norm_g, m_w_ple_gate, m_w_ple, m_final_g, v_norm_g, v_w_in, v_attn_out_g, v_dw_w, v_dw_b, v_conv_ln_g, v_conv_ln_b, v_w_pw, v_conv_out_g, v_w_out, v_ple_norm_g, v_w_ple_gate, v_w_ple, v_final_g):
    W = dict(norm_g=norm_g, w_in=w_in, attn_out_g=attn_out_g, dw_w=dw_w, dw_b=dw_b, conv_ln_g=conv_ln_g,
             conv_ln_b=conv_ln_b, w_pw=w_pw, conv_out_g=conv_out_g, w_out=w_out, ple_norm_g=ple_norm_g,
             w_ple_gate=w_ple_gate, w_ple=w_ple, final_g=final_g)
    M = dict(norm_g=m_norm_g, w_in=m_w_in, attn_out_g=m_attn_out_g, dw_w=m_dw_w, dw_b=m_dw_b,
             conv_ln_g=m_conv_ln_g, conv_ln_b=m_conv_ln_b, w_pw=m_w_pw, conv_out_g=m_conv_out_g, w_out=m_w_out,
             ple_norm_g=m_ple_norm_g, w_ple_gate=m_w_ple_gate, w_ple=m_w_ple, final_g=m_final_g)
    V = dict(norm_g=v_norm_g, w_in=v_w_in, attn_out_g=v_attn_out_g, dw_w=v_dw_w, dw_b=v_dw_b,
             conv_ln_g=v_conv_ln_g, conv_ln_b=v_conv_ln_b, w_pw=v_w_pw, conv_out_g=v_conv_out_g, w_out=v_w_out,
             ple_norm_g=v_ple_norm_g, w_ple_gate=v_w_ple_gate, w_ple=v_w_ple, final_g=v_final_g)
    order = ("norm_g", "w_in", "attn_out_g", "dw_w", "dw_b", "conv_ln_g", "conv_ln_b", "w_pw", "conv_out_g",
             "w_out", "ple_norm_g", "w_ple_gate", "w_ple", "final_g")

    pad_taps = lambda a: jnp.pad(a, ((0, 0), (0, CWP - CW), (0, 0)))
    cast = dict(w_in=w_in.astype(BF16), w_out=w_out.astype(BF16), w_ple_gate=w_ple_gate.astype(BF16),
                w_ple=w_ple.astype(BF16), w_pw=w_pw.astype(BF16), dw_w=pad_taps(dw_w))
    shards = [{k: v[l] for k, v in cast.items()} for l in range(DEPTH)]
    xi, yi, ci = lax.axis_index("x"), lax.axis_index("y"), lax.axis_index("c")
    chip = 2 * xi + yi

    sm = {k: W[k] for k in SMALL2}
    sm["final_g"] = final_g.reshape(1, D)
    loss_part, grad_x, big, small, dfg = _local_step(x[0], p[:, 0], loss_target[0], sm, shards, chip, ci)
    g_big = {name: jnp.stack([big[l][name] for l in range(DEPTH)]).reshape(cast[name].shape) for name in BIG}

    small_two = {k: jnp.concatenate([small[l][k] for l in range(DEPTH)], axis=0) for k in SMALL2}
    g_small_packed = _small_allreduce(_pack_small(small_two, dfg).at[SMALL_ROWS - 1, LANE - 1].set(loss_part))
    loss = g_small_packed[SMALL_ROWS - 1, LANE - 1]
    g_small, g_final = _unpack_small(g_small_packed, {k: W[k] for k in SMALL2}, final_g)

    grads, deltas, new_m, new_v = {}, {}, {}, {}
    for name in BIG:
        wv = pad_taps(W[name]) if name == "dw_w" else W[name]
        mv = pad_taps(M[name]) if name == "dw_w" else M[name]
        vv = pad_taps(V[name]) if name == "dw_w" else V[name]
        gg = g_big[name]
        cols = wv.shape[-1]
        rows_total = wv.size // cols
        tile_rows = min(rows_total, 256)
        d2, m2, v2 = _adamw(wv.reshape(rows_total, cols), gg.reshape(rows_total, cols),
                            mv.reshape(rows_total, cols), vv.reshape(rows_total, cols), tile_rows)
        if name == "dw_w":
            cut = lambda a: a.reshape(DEPTH, CWP, LANE)[:, :CW]
            grads[name], deltas[name], new_m[name], new_v[name] = cut(gg), cut(d2), cut(m2), cut(v2)
        else:
            grads[name] = gg
            deltas[name], new_m[name], new_v[name] = (t.reshape(wv.shape) for t in (d2, m2, v2))
    ws = _pack_small({k: W[k] for k in SMALL2}, final_g)
    ms = _pack_small({k: M[k] for k in SMALL2}, m_final_g)
    vs = _pack_small({k: V[k] for k in SMALL2}, v_final_g)
    ds, nms, nvs = _adamw(ws, g_small_packed, ms, vs, SMALL_ROWS)
    for packed, dst in ((ds, deltas), (nms, new_m), (nvs, new_v)):
        two, fin = _unpack_small(packed, {k: W[k] for k in SMALL2}, final_g)
        dst.update(two)
        dst["final_g"] = fin
    grads.update(g_small)
    grads["final_g"] = g_final

    return (loss, grad_x[None], *[grads[n] for n in order], *[deltas[n] for n in order],
            *[new_m[n] for n in order], *[new_v[n] for n in order])
```

```python
import functools

import jax
import jax.numpy as jnp
from jax import lax
from jax.experimental import pallas as pl
from jax.experimental.pallas import tpu as pltpu

F32 = jnp.float32
BF16 = jnp.bfloat16

T = 2048
D = 1024
DIN = 3584
NCHIP = 4
SHW = DIN // NCHIP
AD = 512
CD = 512
DH = 64
CW = 31
CWP = 32
PLE = 256
DEPTH = 2
EPS = 1e-6
AQ = 256
HG = 4
GW = HG * DH
SR = HG * AQ
NG = AD // GW
LANE = 128
TM = 1024

ADAM_LR = 0.001
ADAM_B1 = 0.9
ADAM_B2 = 0.999
ADAM_EPS = 1e-08
ADAM_WD = 0.01
ADAM_STEP = 10

SMALL_ROWS = 80

VMEM_BIG = 56 * 1024 * 1024
MESH = pl.DeviceIdType.MESH


def _cp(sem=None, vmem=None):
    kw = {}
    if sem is not None:
        kw["dimension_semantics"] = sem
    if vmem is not None:
        kw["vmem_limit_bytes"] = vmem
    return pltpu.CompilerParams(**kw)


def _dot(a, b):
    return jnp.dot(a, b, preferred_element_type=F32)


def _dot_nt(a, b):
    return lax.dot_general(a, b, (((1,), (1,)), ((), ())), preferred_element_type=F32)


def _dot_tn(a, b):
    return lax.dot_general(a, b, (((0,), (0,)), ((), ())), preferred_element_type=F32)


def _dot2(x, m):
    hi = x.astype(BF16)
    lo = (x - hi.astype(F32)).astype(BF16)
    return _dot(hi, m) + _dot(lo, m)


def _sig(x):
    return 1.0 / (1.0 + jnp.exp(-x))


def _softplus(z):
    return jnp.maximum(z, 0.0) + jnp.log(1.0 + jnp.exp(-jnp.abs(z)))


def _rstd(x):
    return lax.rsqrt(jnp.mean(x * x, axis=-1, keepdims=True) + EPS)


def _rms_bwd(dy, x, r, g):
    dn = dy * g
    return r * dn - x * (r * r * r) * jnp.mean(dn * x, axis=-1, keepdims=True)


def _rms_inproj(h, g, w):
    tm = min(TM, T)

    def body(h_ref, g_ref, w_ref, u_ref, hn_ref, hn_s):
        @pl.when(pl.program_id(1) == 0)
        def _():
            hh = h_ref[...]
            hn = (hh * _rstd(hh) * g_ref[...]).astype(BF16)
            hn_s[...] = hn
            hn_ref[...] = hn
        u_ref[...] = _dot(hn_s[...], w_ref[0])

    return pl.pallas_call(
        body, name="rms_inproj", grid=(T // tm, NCHIP),
        in_specs=[pl.BlockSpec((tm, D), lambda i, k: (i, 0)),
                  pl.BlockSpec((1, D), lambda i, k: (0, 0)),
                  pl.BlockSpec((1, D, SHW), lambda i, k: (k, 0, 0))],
        out_specs=[pl.BlockSpec((tm, SHW), lambda i, k: (i, k)),
                   pl.BlockSpec((tm, D), lambda i, k: (i, 0))],
        out_shape=[jax.ShapeDtypeStruct((T, DIN), F32), jax.ShapeDtypeStruct((T, D), BF16)],
        scratch_shapes=[pltpu.VMEM((tm, D), BF16)],
        compiler_params=_cp(("arbitrary", "arbitrary"), VMEM_BIG),
    )(h, g, w)


def _attn_tiles():
    row = lax.broadcasted_iota(jnp.int32, (SR, AQ), 0) & (AQ - 1)
    col = lax.broadcasted_iota(jnp.int32, (SR, AQ), 1)
    tr = lax.broadcasted_iota(jnp.int32, (AQ, AQ), 0)
    tc = lax.broadcasted_iota(jnp.int32, (AQ, AQ), 1)
    lane_head = lax.broadcasted_iota(jnp.int32, (1, GW), 1) // DH
    return col < row, tr, tc, [lane_head == h for h in range(HG)]


def _stack_heads(t, heads):
    return jnp.concatenate([jnp.where(m, t, 0.0) for m in heads], axis=0)


def _unstack_heads(t, heads):
    out = t[:AQ]
    for h in range(1, HG):
        out = jnp.where(heads[h], t[h * AQ:(h + 1) * AQ], out)
    return out


def _tri_sum(x, tri):
    hi = x.astype(BF16)
    lo = (x - hi.astype(F32)).astype(BF16)
    both = _dot(jnp.concatenate([hi, lo], axis=0), tri)
    return both[:SR] + both[SR:]


def _scatter_copies(ps, gots, send_sems, recv_sems):
    x, y, c = _place()
    peers = [(1 - x, y), (x, 1 - y), (1 - x, 1 - y)]
    return [pltpu.make_async_remote_copy(
        src_ref=ps[a].at[2 * px + py], dst_ref=gots[a].at[r], send_sem=send_sems.at[3 * a + r],
        recv_sem=recv_sems.at[3 * a + r], device_id=(px, py, c), device_id_type=MESH)
        for a in range(len(ps)) for r, (px, py) in enumerate(peers)]


GATHER_SEMS = 7


def _gather_tree_copies(ins, outs, send_sems, recv_sems):
    x, y, c = _place()
    me, xn, yn, dg = 2 * x + y, 2 * (1 - x) + y, 2 * x + (1 - y), 2 * (1 - x) + (1 - y)
    to_x, to_y, sibling = (1 - x, y, c), (x, 1 - y, c), (x, y, 1 - c)
    direct, relayed, passed = [], [], []
    for a in range(len(ins)):
        half = ins[a].shape[0] // 2
        mine = pl.ds(c * half, half)
        first, second = pl.ds(c * half, half // 2), pl.ds(c * half + half // 2, half // 2)

        def copy(i, src, dst, to, k=GATHER_SEMS * a):
            return pltpu.make_async_remote_copy(src_ref=src, dst_ref=dst, send_sem=send_sems.at[k + i],
                                                recv_sem=recv_sems.at[k + i], device_id=to, device_id_type=MESH)

        own, slot = ins[a].at[mine], outs[a].at[me, mine]
        direct += [copy(0, own, slot, to_x), copy(1, own, slot, to_y)]
        relayed += [copy(2, outs[a].at[xn, first], outs[a].at[xn, first], to_y),
                    copy(3, outs[a].at[yn, second], outs[a].at[yn, second], to_x)]
        passed += [copy(4 + i, outs[a].at[j, mine], outs[a].at[j, mine], sibling) for i, j in enumerate((xn, yn, dg))]
    return direct, relayed, passed


def _pair_copies(ins, outs, send_sems, recv_sems):
    x, y, c = _place()
    copies = []
    for a in range(len(ins)):
        half = ins[a].shape[1] // 2
        copies.append(pltpu.make_async_remote_copy(
            src_ref=ins[a].at[:, pl.ds((1 - c) * half, half), :], dst_ref=outs[a], send_sem=send_sems.at[a],
            recv_sem=recv_sems.at[a], device_id=(x, y, 1 - c), device_id_type=MESH))
    return copies


def _host(body, grid, n_in, n_out, n_x, make_copies, mids=()):
    if not n_x:
        return body

    def hosting(*refs):
        a, b = n_in + n_x, n_in + 2 * n_x + n_out
        copies = make_copies(refs[n_in:a], refs[a + n_out:b], refs[-2], refs[-1])
        stages = copies if isinstance(copies, tuple) else (copies,)
        ids = [pl.program_id(d) for d in range(len(grid))]
        at = lambda step: functools.reduce(jnp.logical_and, [i == s for i, s in zip(ids, step)])

        @pl.when(at([0] * len(grid)))
        def _():
            for cp in stages[0]:
                cp.start()

        for before, after, step in zip(stages, stages[1:], mids):
            @pl.when(at(step))
            def _(before=before, after=after):
                for cp in before:
                    cp.wait_recv()
                for cp in after:
                    cp.start()

        body(*refs[:n_in], *refs[a:a + n_out], *refs[b:-2])

        @pl.when(at([g - 1 for g in grid]))
        def _():
            for cp in stages[-1]:
                cp.wait_recv()
            for stage in stages:
                for cp in stage:
                    cp.wait_send()

    return hosting


def _hosted_sems(n_x, per_array=3):
    n = per_array * n_x
    return [pltpu.SemaphoreType.DMA((n,)), pltpu.SemaphoreType.DMA((n,))] if n_x else []


RC = 256


def _chunk_causal(r):
    row = lax.broadcasted_iota(jnp.int32, (RC, AQ), 0) + (r * RC) % AQ
    return lax.broadcasted_iota(jnp.int32, (RC, AQ), 1) < row


def _attn_fwd(u, agw, shards=()):
    n = len(shards)
    grid = (T // AQ,)

    def body(q_ref, k_ref, v_ref, g_ref, ag_ref, o_ref, y_ref, tot_ref,
             kb_s, vb_s, qs_s, z_s, zs_s, lmb_s, suf_s, att_s, acc_s, run_s):
        qi = pl.program_id(0)

        @pl.when(qi == 0)
        def _():
            kb_s[...] = k_ref[...].astype(BF16)
            vb_s[...] = v_ref[...].astype(BF16)

        _, tr, tc, heads = _attn_tiles()
        upper = (tr > tc).astype(BF16)
        for g in range(NG):
            qs_s[g] = _stack_heads(q_ref[:, g * GW:(g + 1) * GW] * 0.125, heads).astype(BF16)
        acc_s[...] = jnp.zeros_like(acc_s)
        run_s[...] = jnp.zeros_like(run_s)

        def block(kb, masked):
            k0 = pl.multiple_of(kb * AQ, AQ)
            for g in range(NG):
                lanes = pl.ds(g * GW, GW)
                z_s[g] = _dot_nt(qs_s[g], kb_s[pl.ds(k0, AQ), lanes])
                for r in range(SR // RC):
                    rows = pl.ds(r * RC, RC)
                    z = z_s[g, rows, :]
                    zs = jnp.minimum(z, 0.0) - jnp.log(1.0 + jnp.exp(-jnp.abs(z)))
                    lm = zs - z
                    if masked:
                        lm = jnp.where(_chunk_causal(r), lm, 0.0)
                    run = run_s[g, rows, :]
                    zs_s[g, rows, :] = zs + run[:, 0:1]
                    hi = lm.astype(BF16)
                    lmb_s[g, rows, :] = hi
                    lmb_s[g, pl.ds(SR + r * RC, RC), :] = (lm - hi.astype(F32)).astype(BF16)
                    run_s[g, rows, :] = run + jnp.sum(lm, axis=1, keepdims=True)
                suf_s[g] = _dot(lmb_s[g], upper)
                for r in range(SR // RC):
                    rows = pl.ds(r * RC, RC)
                    att = jnp.exp(zs_s[g, rows, :] + suf_s[g, rows, :] + suf_s[g, pl.ds(SR + r * RC, RC), :])
                    if masked:
                        att = jnp.where(_chunk_causal(r), att, 0.0)
                    att_s[g, rows, :] = att.astype(BF16)
                acc_s[g] += _dot(att_s[g], vb_s[pl.ds(k0, AQ), lanes])

        block(qi, True)

        def step(i, c):
            block(qi - 1 - i, False)
            return c

        lax.fori_loop(0, qi, step, 0)
        gate = g_ref[...]
        agv = ag_ref[...]
        for g in range(NG):
            lanes = slice(g * GW, (g + 1) * GW)
            o = _unstack_heads(acc_s[g], heads)
            osq = o * o
            ms = jnp.zeros((AQ, GW), F32)
            for m in heads:
                ms = jnp.where(m, jnp.sum(jnp.where(m, osq, 0.0), axis=1, keepdims=True), ms)
            gg = gate[:, lanes]
            o_ref[:, lanes] = o
            y_ref[:, lanes] = (o * lax.rsqrt(ms * (1.0 / DH) + EPS) * agv[:, lanes] * (gg * _sig(gg))).astype(BF16)
            tot_ref[g] = _unstack_heads(jnp.broadcast_to(run_s[g][:, 0:1], (SR, GW)), heads)

    tile = lambda dt, rows=SR: pltpu.VMEM((NG, rows, AQ), dt)
    scratch = [pltpu.VMEM((T, AD), BF16), pltpu.VMEM((T, AD), BF16), pltpu.VMEM((NG, SR, GW), BF16),
               tile(F32), tile(F32), tile(BF16, 2 * SR), tile(F32, 2 * SR), tile(BF16),
               pltpu.VMEM((NG, SR, GW), F32), pltpu.VMEM((NG, SR, LANE), F32)]
    col = lambda j: pl.BlockSpec((AQ, AD), lambda qi: (qi, j))
    res = pl.pallas_call(
        _host(body, grid, 5, 3, n, _gather_tree_copies, mids=((grid[0] * 5 // 8,), (grid[0] * 7 // 8,))), name="attn_fwd", grid=grid,
        in_specs=[col(0), pl.BlockSpec((T, AD), lambda qi: (0, 1)), pl.BlockSpec((T, AD), lambda qi: (0, 2)),
                  col(3), pl.BlockSpec((1, AD), lambda qi: (0, 0))] + [HBM_SPEC] * n,
        out_specs=[col(0), col(0), pl.BlockSpec((NG, AQ, GW), lambda qi: (0, qi, 0))] + [HBM_SPEC] * n,
        out_shape=[jax.ShapeDtypeStruct((T, AD), F32), jax.ShapeDtypeStruct((T, AD), BF16),
                   jax.ShapeDtypeStruct((NG, T, GW), F32)]
        + [jax.ShapeDtypeStruct((NCHIP,) + s.shape, s.dtype) for s in shards],
        scratch_shapes=scratch + _hosted_sems(n, GATHER_SEMS),
        compiler_params=_cp(("arbitrary",), VMEM_BIG),
    )(u, u, u, u, agw, *shards)
    return res[0], res[1], res[2], list(res[3:])


def _glu_conv(u, dw, db):
    tr = 256

    def body(cv_ref, cg_ref, w_ref, b_ref, c1_ref, pad_s):
        pad_s[pl.ds(0, CWP), :] = jnp.zeros((CWP, LANE), F32)
        pad_s[pl.ds(CWP, T), :] = cv_ref[...] * _sig(cg_ref[...])
        wv = w_ref[0]
        bias = b_ref[...]

        def tile(i, carry):
            r0 = pl.multiple_of(i * tr, tr)
            acc = jnp.zeros((tr, LANE), F32) + bias
            for w in range(CW):
                acc = acc + pad_s[pl.ds(r0 + (CWP - CW + 1) + w, tr), :] * wv[w:w + 1, :]
            c1_ref[pl.ds(r0, tr), :] = acc
            return carry

        lax.fori_loop(0, T // tr, tile, 0)

    return pl.pallas_call(
        body, name="glu_conv", grid=(CD // LANE,),
        in_specs=[pl.BlockSpec((T, LANE), lambda cb: (0, 16 + cb)),
                  pl.BlockSpec((T, LANE), lambda cb: (0, 20 + cb)),
                  pl.BlockSpec((1, CWP, LANE), lambda cb: (cb, 0, 0)),
                  pl.BlockSpec((1, LANE), lambda cb: (0, cb))],
        out_specs=pl.BlockSpec((T, LANE), lambda cb: (0, cb)),
        out_shape=jax.ShapeDtypeStruct((T, CD), F32),
        scratch_shapes=[pltpu.VMEM((T + CWP, LANE), F32)],
        compiler_params=_cp(("arbitrary",)),
    )(u, u, dw, db)


def _ln_silu(c1, lg, lb):
    mu = jnp.mean(c1, axis=-1, keepdims=True)
    xc = c1 - mu
    rs = lax.rsqrt(jnp.mean(xc * xc, axis=-1, keepdims=True) + EPS)
    xh = xc * rs
    ln = xh * lg + lb
    s = _sig(ln)
    return xh, rs, ln, s


def _layer_tail(c1, u, ya, h, p, lg, lb, wpw, cg, wout, pg, wgate, wple):
    tm = 256

    def body(c1_ref, gc_ref, ya_ref, h_ref, p_ref, lg_ref, lb_ref, wpw_ref, cg_ref, wout_ref,
             pg_ref, wgate_ref, wple_ref, c3_ref, yc_ref, h1_ref, gate_ref, pe_ref, h2_ref):
        _, _, ln, s = _ln_silu(c1_ref[...], lg_ref[...], lb_ref[...])
        c2 = (ln * s).astype(BF16)
        c3 = _dot(c2, wpw_ref[...])
        gc = gc_ref[...]
        yc = (c3 * _rstd(c3) * cg_ref[...] * (gc * _sig(gc))).astype(BF16)
        c3_ref[...] = c3
        yc_ref[...] = yc
        y = _dot(ya_ref[...], wout_ref[pl.ds(0, AD), :]) + _dot(yc, wout_ref[pl.ds(AD, CD), :])
        h1 = h_ref[...] + y
        hn2 = (h1 * _rstd(h1) * pg_ref[...]).astype(BF16)
        gate = _sig(_dot(hn2, wgate_ref[...]))
        pb = p_ref[...].astype(BF16)
        pe = jnp.concatenate([_dot(pb, wple_ref[k]) for k in range(NCHIP)], axis=1)
        h1_ref[...] = h1
        gate_ref[...] = gate.astype(BF16)
        pe_ref[...] = pe.astype(BF16)
        h2_ref[...] = h1 + pe * gate

    row = lambda w: pl.BlockSpec((tm, w), lambda i: (i, 0))
    full = lambda *s: pl.BlockSpec(s, lambda i: (0,) * len(s))
    return pl.pallas_call(
        body, name="layer_tail", grid=(T // tm,),
        in_specs=[row(CD), pl.BlockSpec((tm, CD), lambda i: (i, 6)), row(AD), row(D), row(PLE),
                  full(1, CD), full(1, CD), full(CD, CD), full(1, CD), full(D, D),
                  full(1, D), full(D, D), full(NCHIP, PLE, PLE)],
        out_specs=[row(CD), row(CD), row(D), row(D), row(D), row(D)],
        out_shape=[jax.ShapeDtypeStruct((T, CD), F32), jax.ShapeDtypeStruct((T, CD), BF16),
                   jax.ShapeDtypeStruct((T, D), F32), jax.ShapeDtypeStruct((T, D), BF16),
                   jax.ShapeDtypeStruct((T, D), BF16), jax.ShapeDtypeStruct((T, D), F32)],
        compiler_params=_cp(("arbitrary",), VMEM_BIG),
    )(c1, u, ya, h, p, lg, lb, wpw, cg, wout, pg, wgate, wple)


def _loss_head(h, tgt, fg):
    tm = 256

    def body(h_ref, t_ref, g_ref, loss_ref, dh_ref, dg_ref):
        @pl.when(pl.program_id(0) == 0)
        def _():
            loss_ref[...] = jnp.zeros_like(loss_ref)
            dg_ref[...] = jnp.zeros_like(dg_ref)
        hh = h_ref[...]
        g = g_ref[...]
        r = _rstd(hh)
        e = hh * r * g - t_ref[...]
        loss_ref[...] += 0.5 * jnp.sum(jnp.mean(e * e, axis=-1, keepdims=True))
        dy = e * (1.0 / D)
        dg_ref[...] += jnp.sum(dy * hh * r, axis=0, keepdims=True)
        dh_ref[...] = _rms_bwd(dy, hh, r, g)

    return pl.pallas_call(
        body, name="loss_head", grid=(T // tm,),
        in_specs=[pl.BlockSpec((tm, D), lambda i: (i, 0)), pl.BlockSpec((tm, D), lambda i: (i, 0)),
                  pl.BlockSpec((1, D), lambda i: (0, 0))],
        out_specs=[pl.BlockSpec((8, LANE), lambda i: (0, 0)), pl.BlockSpec((tm, D), lambda i: (i, 0)),
                   pl.BlockSpec((1, D), lambda i: (0, 0))],
        out_shape=[jax.ShapeDtypeStruct((8, LANE), F32), jax.ShapeDtypeStruct((T, D), F32),
                   jax.ShapeDtypeStruct((1, D), F32)],
        compiler_params=_cp(("arbitrary",)),
    )(h, tgt, fg)


def _ple_out_bwd(dh2, h1, gate, pe, p, ya, yc, pg, wgate, wout):
    tm = 256

    def body(dh2_ref, h1_ref, gate_ref, pe_ref, p_ref, ya_ref, yc_ref, pg_ref, wgate_ref, wout_ref,
             dh1_ref, dy_ref, dwg_ref, dwp_ref, dwo_ref, dpg_ref):
        @pl.when(pl.program_id(0) == 0)
        def _():
            dwg_ref[...] = jnp.zeros_like(dwg_ref)
            dwp_ref[...] = jnp.zeros_like(dwp_ref)
            dwo_ref[...] = jnp.zeros_like(dwo_ref)
            dpg_ref[...] = jnp.zeros_like(dpg_ref)
        dh2 = dh2_ref[...]
        h1 = h1_ref[...]
        gate = gate_ref[...].astype(F32)
        pg = pg_ref[...]
        dpe = (dh2 * gate).astype(BF16)
        dgp = (dh2 * pe_ref[...].astype(F32) * gate * (1.0 - gate)).astype(BF16)
        r = _rstd(h1)
        hn = h1 * r
        dwg_ref[...] += _dot_tn((hn * pg).astype(BF16), dgp)
        dhn2 = _dot_nt(dgp, wgate_ref[...])
        dpg_ref[...] += jnp.sum(dhn2 * hn, axis=0, keepdims=True)
        dh1 = dh2 + _rms_bwd(dhn2, h1, r, pg)
        pb = p_ref[...].astype(BF16)
        for k in range(NCHIP):
            dwp_ref[k] += _dot_tn(pb, dpe[:, k * PLE:(k + 1) * PLE])
        dh1b = dh1.astype(BF16)
        dy_ref[...] = _dot_nt(dh1b, wout_ref[...])
        dwo_ref[pl.ds(0, AD), :] += _dot_tn(ya_ref[...], dh1b)
        dwo_ref[pl.ds(AD, CD), :] += _dot_tn(yc_ref[...], dh1b)
        dh1_ref[...] = dh1

    row = lambda w: pl.BlockSpec((tm, w), lambda i: (i, 0))
    full = lambda *s: pl.BlockSpec(s, lambda i: (0,) * len(s))
    return pl.pallas_call(
        body, name="ple_out_bwd", grid=(T // tm,),
        in_specs=[row(D), row(D), row(D), row(D), row(PLE), row(AD), row(CD),
                  full(1, D), full(D, D), full(D, D)],
        out_specs=[row(D), row(D), full(D, D), full(NCHIP, PLE, PLE), full(D, D), full(1, D)],
        out_shape=[jax.ShapeDtypeStruct((T, D), F32), jax.ShapeDtypeStruct((T, D), F32),
                   jax.ShapeDtypeStruct((D, D), F32), jax.ShapeDtypeStruct((NCHIP, PLE, PLE), F32),
                   jax.ShapeDtypeStruct((D, D), F32), jax.ShapeDtypeStruct((1, D), F32)],
        compiler_params=_cp(("arbitrary",), VMEM_BIG),
    )(dh2, h1, gate, pe, p, ya, yc, pg, wgate, wout)


def _branch_bwd(dy, o, u, c1, c3, ag, lg, lb, wpw, cg, seg):
    tm = 256

    def body(dya_ref, dyc_ref, o_ref, ga_ref, gc_ref, c1_ref, c3_ref, ag_ref, lg_ref, lb_ref, wpw_ref,
             cg_ref, seg_ref, do_ref, dga_ref, dgc_ref, dc1_ref, dwpw_ref, dag_ref, dcg_ref, dlg_ref, dlb_ref):
        @pl.when(pl.program_id(0) == 0)
        def _():
            for r_ in (dwpw_ref, dag_ref, dcg_ref, dlg_ref, dlb_ref):
                r_[...] = jnp.zeros_like(r_)
        dya = dya_ref[...]
        o = o_ref[...]
        ga = ga_ref[...]
        ag_v = ag_ref[...]
        seg_m = seg_ref[...]
        r = lax.rsqrt(_dot2(o * o, seg_m) * (1.0 / DH) + EPS)
        onr = o * r
        sg = _sig(ga)
        dga_ref[...] = (dya * (onr * ag_v) * (sg * (1.0 + ga * (1.0 - sg)))).astype(BF16)
        don = dya * (ga * sg)
        dag_ref[...] += jnp.sum(don * onr, axis=0, keepdims=True)
        dn = don * ag_v
        do_ref[...] = r * dn - o * (r * r * r) * (_dot2(dn * o, seg_m) * (1.0 / DH))
        dyc = dyc_ref[...]
        c3 = c3_ref[...]
        gc = gc_ref[...]
        cg_v = cg_ref[...]
        r3 = _rstd(c3)
        cn = c3 * r3
        sc = _sig(gc)
        dgc_ref[...] = (dyc * (cn * cg_v) * (sc * (1.0 + gc * (1.0 - sc)))).astype(BF16)
        dcn = dyc * (gc * sc)
        dcg_ref[...] += jnp.sum(dcn * cn, axis=0, keepdims=True)
        dc3 = _rms_bwd(dcn, c3, r3, cg_v).astype(BF16)
        lg_v = lg_ref[...]
        xh, rs, ln, s = _ln_silu(c1_ref[...], lg_v, lb_ref[...])
        c2 = (ln * s).astype(BF16)
        dwpw_ref[...] += _dot_tn(c2, dc3)
        dc2 = _dot_nt(dc3, wpw_ref[...])
        dln = dc2 * (s * (1.0 + ln * (1.0 - s)))
        dlb_ref[...] += jnp.sum(dln, axis=0, keepdims=True)
        dlg_ref[...] += jnp.sum(dln * xh, axis=0, keepdims=True)
        dxh = dln * lg_v
        dc1_ref[...] = rs * (dxh - jnp.mean(dxh, axis=-1, keepdims=True)
                             - xh * jnp.mean(dxh * xh, axis=-1, keepdims=True))

    half = lambda j: pl.BlockSpec((tm, 512), lambda i: (i, j))
    full = lambda *s: pl.BlockSpec(s, lambda i: (0,) * len(s))
    vec = jax.ShapeDtypeStruct((1, 512), F32)
    act = jax.ShapeDtypeStruct((T, 512), F32)
    return pl.pallas_call(
        body, name="branch_bwd", grid=(T // tm,),
        in_specs=[half(0), half(1), half(0), half(3), half(6), half(0), half(0),
                  full(1, AD), full(1, CD), full(1, CD), full(CD, CD), full(1, CD), full(AD, AD)],
        out_specs=[half(0), half(0), half(0), half(0), full(CD, CD), full(1, 512), full(1, 512),
                   full(1, 512), full(1, 512)],
        out_shape=[act, jax.ShapeDtypeStruct((T, 512), BF16), jax.ShapeDtypeStruct((T, 512), BF16), act,
                   jax.ShapeDtypeStruct((CD, CD), F32), vec, vec, vec, vec],
        compiler_params=_cp(("arbitrary",), VMEM_BIG),
    )(dy, dy, o, u, u, c1, c3, ag, lg, lb, wpw, cg, seg)


def _conv_bwd(dc1, u, dw):
    tr = 64
    off = CWP - CW + 1

    def body(d_ref, cv_ref, cg_ref, w_ref, dcv_ref, dcg_ref, ddw_ref, ddb_ref, padc_s, padd_s, acc_s):
        cv = cv_ref[...]
        sg = _sig(cg_ref[...])
        padc_s[pl.ds(0, CWP), :] = jnp.zeros((CWP, LANE), F32)
        padc_s[pl.ds(CWP, T), :] = cv * sg
        padd_s[pl.ds(0, T), :] = d_ref[...]
        padd_s[pl.ds(T, CWP), :] = jnp.zeros((CWP, LANE), F32)
        acc_s[...] = jnp.zeros_like(acc_s)
        wv = w_ref[0]

        def tile(i, carry):
            r0 = pl.multiple_of(i * tr, tr)
            dt = padd_s[pl.ds(r0, tr), :]
            dc0 = jnp.zeros((tr, LANE), F32)
            for w in range(CW):
                dc0 = dc0 + padd_s[pl.ds(r0 + (CW - 1) - w, tr), :] * wv[w:w + 1, :]
                prod = dt * padc_s[pl.ds(r0 + off + w, tr), :]
                acc_s[w] += jnp.sum(prod.reshape(tr // 8, 8, LANE), axis=0)
            cvt = cv_ref[pl.ds(r0, tr), :]
            sgt = _sig(cg_ref[pl.ds(r0, tr), :])
            dcv_ref[pl.ds(r0, tr), :] = (dc0 * sgt).astype(BF16)
            dcg_ref[pl.ds(r0, tr), :] = (dc0 * cvt * sgt * (1.0 - sgt)).astype(BF16)
            return carry

        lax.fori_loop(0, T // tr, tile, 0)
        ddw_ref[0] = jnp.sum(acc_s[...], axis=1)
        ddb_ref[...] = jnp.sum(d_ref[...], axis=0, keepdims=True)

    col = lambda j: pl.BlockSpec((T, LANE), lambda cb: (0, j + cb))
    return pl.pallas_call(
        body, name="conv_bwd", grid=(CD // LANE,),
        in_specs=[col(0), col(16), col(20), pl.BlockSpec((1, CWP, LANE), lambda cb: (cb, 0, 0))],
        out_specs=[col(0), col(0), pl.BlockSpec((1, CWP, LANE), lambda cb: (cb, 0, 0)),
                   pl.BlockSpec((1, LANE), lambda cb: (0, cb))],
        out_shape=[jax.ShapeDtypeStruct((T, CD), BF16), jax.ShapeDtypeStruct((T, CD), BF16),
                   jax.ShapeDtypeStruct((NCHIP, CWP, LANE), F32), jax.ShapeDtypeStruct((1, CD), F32)],
        scratch_shapes=[pltpu.VMEM((T + CWP, LANE), F32), pltpu.VMEM((T + CWP, LANE), F32),
                        pltpu.VMEM((CWP, 8, LANE), F32)],
        compiler_params=_cp(("arbitrary",)),
    )(dc1, u, u, dw)


def _attn_bwd(u, do, tot, partials=()):
    n_x = len(partials)
    grid = (T // AQ,)

    def body(q_ref, k_ref, v_ref, do_ref, tot_ref, dq_ref, dk_ref, dv_ref, kb_s, vb_s, dk_s, dv_s):
        qi = pl.program_id(0)

        @pl.when(qi == 0)
        def _():
            kb_s[...] = k_ref[...].astype(BF16)
            vb_s[...] = v_ref[...].astype(BF16)
            dk_s[...] = jnp.zeros_like(dk_s)
            dv_s[...] = jnp.zeros_like(dv_s)

        causal, tr, tc, heads = _attn_tiles()
        upper = (tr > tc).astype(BF16)
        lower = (tr < tc).astype(BF16)
        qs, qus, dos, tots = [], [], [], []
        for g in range(NG):
            lanes = slice(g * GW, (g + 1) * GW)
            q = q_ref[:, lanes]
            qs.append(_stack_heads(q * 0.125, heads).astype(BF16))
            qus.append(_stack_heads(q, heads).astype(BF16))
            dos.append(_stack_heads(do_ref[:, lanes], heads).astype(BF16))
            totv = tot_ref[g]
            tots.append(jnp.concatenate([totv[:, h * DH:h * DH + 1] for h in range(HG)], axis=0))

        def block(kb, carry, masked):
            k0 = pl.multiple_of(kb * AQ, AQ)
            out = []
            for g in range(NG):
                lanes = pl.ds(g * GW, GW)
                lm_left, dl_left, dq = carry[g]
                kk = kb_s[pl.ds(k0, AQ), lanes]
                vv = vb_s[pl.ds(k0, AQ), lanes]
                z = _dot_nt(qs[g], kk)
                sp = _softplus(z)
                lm = jnp.where(causal, -sp, 0.0) if masked else -sp
                lm_incl = lm_left + jnp.sum(lm, axis=1, keepdims=True)
                att = jnp.exp((z - sp) + _tri_sum(lm, upper) + (tots[g] - lm_incl))
                if masked:
                    att = jnp.where(causal, att, 0.0)
                dl = att * _dot_nt(dos[g], vv)
                dv_s[pl.ds(k0, AQ), lanes] += _dot_tn(att.astype(BF16), dos[g])
                prefix = dl_left + _tri_sum(dl, lower)
                beta = jnp.exp(z - sp)
                dz = (1.0 - beta) * dl - beta * prefix
                if masked:
                    dz = jnp.where(causal, dz, 0.0)
                dzs = (dz * 0.125).astype(BF16)
                dk_s[pl.ds(k0, AQ), lanes] += _dot_tn(dzs, qus[g])
                out.append((lm_incl, dl_left + jnp.sum(dl, axis=1, keepdims=True), dq + _dot(dzs, kk)))
            return tuple(out)

        zero = jnp.zeros((SR, 1), F32)
        init = tuple((zero, zero, jnp.zeros((SR, GW), F32)) for _ in range(NG))
        carry = lax.fori_loop(0, qi, lambda kb, c: block(kb, c, False), init)
        carry = block(qi, carry, True)
        for g in range(NG):
            dq_ref[:, g * GW:(g + 1) * GW] = _unstack_heads(carry[g][2], heads).astype(BF16)

        @pl.when(qi == grid[0] - 1)
        def _():
            dk_ref[...] = dk_s[...].astype(BF16)
            dv_ref[...] = dv_s[...].astype(BF16)

    col = lambda j: pl.BlockSpec((AQ, AD), lambda qi: (qi, j))
    whole = lambda j: pl.BlockSpec((T, AD), lambda qi: (0, j), pipeline_mode=pl.Buffered(1))
    res = pl.pallas_call(
        _host(body, grid, 5, 3, n_x, _scatter_copies), name="attn_bwd", grid=grid,
        in_specs=[col(0), whole(1), whole(2), col(0), pl.BlockSpec((NG, AQ, GW), lambda qi: (0, qi, 0))]
        + [HBM_SPEC] * n_x,
        out_specs=[col(0), whole(0), whole(0)] + [HBM_SPEC] * n_x,
        out_shape=[jax.ShapeDtypeStruct((T, AD), BF16)] * 3
        + [jax.ShapeDtypeStruct((NCHIP - 1,) + a.shape[1:], a.dtype) for a in partials],
        scratch_shapes=[pltpu.VMEM((T, AD), BF16), pltpu.VMEM((T, AD), BF16), pltpu.VMEM((T, AD), F32),
                        pltpu.VMEM((T, AD), F32)] + _hosted_sems(n_x),
        compiler_params=_cp(("arbitrary",), VMEM_BIG),
    )(u, u, u, do, tot, *partials)
    return res[0], res[1], res[2], list(res[3:])


def _inproj_dw(hn, du):
    tm = min(TM, T)

    def body(hn_ref, du_ref, dw_ref):
        @pl.when(pl.program_id(1) == 0)
        def _():
            dw_ref[...] = jnp.zeros_like(dw_ref)
        dw_ref[0] += _dot_tn(hn_ref[...], du_ref[...])

    return pl.pallas_call(
        body, name="inproj_dw", grid=(NCHIP, T // tm),
        in_specs=[pl.BlockSpec((tm, D), lambda k, i: (i, 0)), pl.BlockSpec((tm, SHW), lambda k, i: (i, k))],
        out_specs=pl.BlockSpec((1, D, SHW), lambda k, i: (k, 0, 0)),
        out_shape=jax.ShapeDtypeStruct((NCHIP, D, SHW), F32),
        compiler_params=_cp(("arbitrary", "arbitrary"), VMEM_BIG),
    )(hn, du)


def _inproj_dx(du, w, h, g, dres, partials=(), grads=()):
    tm = min(TM, T)
    sent = list(partials) + list(grads)
    n_x = len(sent)
    grid = (T // tm, NCHIP)
    if grads:
        landing = [jax.ShapeDtypeStruct((NCHIP, a.shape[1] // 2, a.shape[2]), F32) for a in grads]
    else:
        landing = [jax.ShapeDtypeStruct((NCHIP - 1,) + a.shape[1:], a.dtype) for a in partials]

    def body(du_ref, w_ref, h_ref, g_ref, dres_ref, dh_ref, dg_ref, acc_s):
        i, k = pl.program_id(0), pl.program_id(1)

        @pl.when(jnp.logical_and(i == 0, k == 0))
        def _():
            dg_ref[...] = jnp.zeros_like(dg_ref)

        @pl.when(k == 0)
        def _():
            acc_s[...] = _dot_nt(du_ref[...], w_ref[0])

        @pl.when(k > 0)
        def _():
            acc_s[...] += _dot_nt(du_ref[...], w_ref[0])

        @pl.when(k == NCHIP - 1)
        def _():
            hh = h_ref[...]
            r = _rstd(hh)
            dhn = acc_s[...]
            dg_ref[...] += jnp.sum(dhn * hh * r, axis=0, keepdims=True)
            dh_ref[...] = dres_ref[...] + _rms_bwd(dhn, hh, r, g_ref[...])

    res = pl.pallas_call(
        _host(body, grid, 5, 2, n_x, _pair_copies if grads else _scatter_copies), name="inproj_dx", grid=grid,
        in_specs=[pl.BlockSpec((tm, SHW), lambda i, k: (i, k)),
                  pl.BlockSpec((1, D, SHW), lambda i, k: (k, 0, 0)),
                  pl.BlockSpec((tm, D), lambda i, k: (i, 0)),
                  pl.BlockSpec((1, D), lambda i, k: (0, 0)),
                  pl.BlockSpec((tm, D), lambda i, k: (i, 0))] + [HBM_SPEC] * n_x,
        out_specs=[pl.BlockSpec((tm, D), lambda i, k: (i, 0)), pl.BlockSpec((1, D), lambda i, k: (0, 0))]
        + [HBM_SPEC] * n_x,
        out_shape=[jax.ShapeDtypeStruct((T, D), F32), jax.ShapeDtypeStruct((1, D), F32)] + landing,
        scratch_shapes=[pltpu.VMEM((tm, D), F32)] + _hosted_sems(n_x),
        compiler_params=_cp(("arbitrary", "arbitrary"), VMEM_BIG),
    )(du, w, h, g, dres, *sent)
    return res[0], res[1], list(res[2:])


def _sum_pair(core, grads, gots):
    n = len(grads)

    def body(c_ref, *refs):
        for a in range(n):
            refs[2 * n + a][...] = (refs[a][...] + refs[n + a][...]).astype(BF16)

    mine = [pl.BlockSpec((1,) + s.shape[1:], lambda k, c: (k, c[0], 0)) for s in gots]
    same = [pl.BlockSpec((1,) + s.shape[1:], lambda k, c: (k, 0, 0)) for s in gots]
    return pl.pallas_call(
        body, name="sum_pair",
        grid_spec=pltpu.PrefetchScalarGridSpec(
            num_scalar_prefetch=1, grid=(NCHIP,), in_specs=mine + same, out_specs=same),
        out_shape=[jax.ShapeDtypeStruct(s.shape, BF16) for s in gots],
        compiler_params=_cp(("arbitrary",), VMEM_BIG),
    )(core, *grads, *gots)


def _sum_chips_share(chip, partials, gots):
    n = len(partials)

    def body(c_ref, *refs):
        mine, theirs = refs[2 * n:3 * n], refs[3 * n:4 * n]
        for a in range(n):
            acc = refs[a][0].astype(F32)
            for j in range(NCHIP - 1):
                acc = acc + refs[n + a][j].astype(F32)
            mine[a][...] = acc
        x, y, c = _place()
        copies = [pltpu.make_async_remote_copy(
            src_ref=mine[a], dst_ref=theirs[a], send_sem=refs[4 * n].at[a], recv_sem=refs[4 * n + 1].at[a],
            device_id=(x, y, 1 - c), device_id_type=MESH) for a in range(n)]
        for cp in copies:
            cp.start()
        for cp in copies:
            cp.wait()

    halves = [jax.ShapeDtypeStruct(s.shape[1:], F32) for s in partials]
    res = pl.pallas_call(
        body, name="sum_chips_share",
        grid_spec=pltpu.PrefetchScalarGridSpec(
            num_scalar_prefetch=1, grid=(1,),
            in_specs=[pl.BlockSpec((1,) + s.shape[1:], lambda i, c: (c[0], 0, 0)) for s in partials]
            + [pl.BlockSpec(s.shape, lambda i, c: (0, 0, 0)) for s in gots],
            out_specs=[pl.BlockSpec(s.shape[1:], lambda i, c: (0, 0)) for s in partials] + [HBM_SPEC] * n,
            scratch_shapes=[pltpu.SemaphoreType.DMA((n,)), pltpu.SemaphoreType.DMA((n,))]),
        out_shape=halves + halves,
        compiler_params=_cp(("arbitrary",), VMEM_BIG),
    )(chip, *partials, *gots)
    return res[:n], res[n:]


def _adamw(w, g, m, v, rows):
    R, C = w.shape
    c1 = 1.0 - ADAM_B1 ** ADAM_STEP
    c2 = 1.0 - ADAM_B2 ** ADAM_STEP

    def body(w_ref, g_ref, m_ref, v_ref, d_ref, nm_ref, nv_ref):
        gg = g_ref[...]
        nm = ADAM_B1 * m_ref[...] + (1.0 - ADAM_B1) * gg
        nv = ADAM_B2 * v_ref[...] + (1.0 - ADAM_B2) * (gg * gg)
        d_ref[...] = -ADAM_LR * ((nm / c1) / (jnp.sqrt(nv / c2) + ADAM_EPS) + ADAM_WD * w_ref[...])
        nm_ref[...] = nm
        nv_ref[...] = nv

    spec = pl.BlockSpec((rows, C), lambda i: (i, 0))
    sh = jax.ShapeDtypeStruct((R, C), F32)
    return pl.pallas_call(
        body, name="adamw", grid=(R // rows,), in_specs=[spec] * 4, out_specs=[spec] * 3,
        out_shape=[sh, sh, sh], compiler_params=_cp(("arbitrary",)),
    )(w, g, m, v)


HBM_SPEC = pl.BlockSpec(memory_space=pltpu.HBM)


def _place():
    return lax.axis_index("x"), lax.axis_index("y"), lax.axis_index("c")


def _all_gather_split(shard):
    def body(in_ref, out_ref, send_sems, recv_sems):
        direct, relayed, passed = _gather_tree_copies([in_ref], [out_ref], send_sems, recv_sems)
        for cp in direct:
            cp.start()
        for i in range(2):
            direct[i].wait_recv()
            relayed[i].start()
            passed[i].start()
        for cp in relayed:
            cp.wait_recv()
        passed[2].start()
        for cp in passed:
            cp.wait_recv()
        for cp in direct + relayed + passed:
            cp.wait_send()

    return pl.pallas_call(
        body, name="all_gather_split", in_specs=[HBM_SPEC], out_specs=HBM_SPEC,
        out_shape=jax.ShapeDtypeStruct((NCHIP,) + shard.shape, shard.dtype),
        scratch_shapes=_hosted_sems(1, GATHER_SEMS),
    )(shard)


def _pair_exchange(grads):
    n = len(grads)

    def body(*refs):
        copies = _pair_copies(refs[:n], refs[n:2 * n], refs[2 * n], refs[2 * n + 1])
        for cp in copies:
            cp.start()
        for cp in copies:
            cp.wait()

    return pl.pallas_call(
        body, name="pair_exchange", in_specs=[HBM_SPEC] * n, out_specs=[HBM_SPEC] * n,
        out_shape=[jax.ShapeDtypeStruct((NCHIP, g.shape[1] // 2, g.shape[2]), F32) for g in grads],
        scratch_shapes=[pltpu.SemaphoreType.DMA((n,)), pltpu.SemaphoreType.DMA((n,))],
    )(*grads)


def _small_allreduce(mine):
    def body(m_ref, o_ref, slots, send_sems, recv_sems):
        x, y, c = _place()
        me = 4 * x + 2 * y + c
        slots[me] = m_ref[...]
        copies = []
        for r in range(1, 8):
            rx, ry, rc = (r >> 2) & 1, (r >> 1) & 1, r & 1
            peer = (x + rx - 2 * x * rx, y + ry - 2 * y * ry, c + rc - 2 * c * rc)
            cp = pltpu.make_async_remote_copy(
                src_ref=m_ref, dst_ref=slots.at[me], send_sem=send_sems.at[r - 1], recv_sem=recv_sems.at[r - 1],
                device_id=peer, device_id_type=MESH)
            cp.start()
            copies.append(cp)
        for cp in copies:
            cp.wait()
        acc = slots[0]
        for j in range(1, 8):
            acc = acc + slots[j]
        o_ref[...] = acc

    return pl.pallas_call(
        body, name="small_allreduce",
        in_specs=[pl.BlockSpec(memory_space=pltpu.VMEM)], out_specs=pl.BlockSpec(memory_space=pltpu.VMEM),
        out_shape=jax.ShapeDtypeStruct((SMALL_ROWS, LANE), F32),
        scratch_shapes=[pltpu.VMEM((8, SMALL_ROWS, LANE), F32), pltpu.SemaphoreType.DMA((7,)),
                        pltpu.SemaphoreType.DMA((7,))],
    )(mine)


def _seg_matrix():
    i = lax.broadcasted_iota(jnp.int32, (AD, AD), 0) // DH
    j = lax.broadcasted_iota(jnp.int32, (AD, AD), 1) // DH
    return (i == j).astype(BF16)


TAIL = ("w_out", "w_ple_gate", "w_ple", "w_pw", "dw_w")


def _local_step(x, p, tgt, sm, shards, chip, ci):
    seg = _seg_matrix()
    core = jnp.reshape(ci, (1,)).astype(jnp.int32)
    chip_idx = jnp.reshape(chip, (1,)).astype(jnp.int32)
    own = lambda g, s: lax.dynamic_update_index_in_dim(g, s, chip, 0)
    w_in_next = own(_all_gather_split(shards[0]["w_in"]), shards[0]["w_in"])
    h = x
    saved = []
    for l in range(DEPTH):
        w_in = w_in_next
        row = lambda name: sm[name][l:l + 1]
        u, hn = _rms_inproj(h, row("norm_g"), w_in)
        todo = [shards[l][k] for k in TAIL] + ([shards[l + 1]["w_in"]] if l + 1 < DEPTH else [])
        o, ya, tot, got = _attn_fwd(u, jnp.tile(row("attn_out_g"), (1, AD // DH)), todo)
        got = [own(g, s) for g, s in zip(got, todo)]
        w_out = got[0].reshape(D, D)
        w_gate = got[1].reshape(D, D)
        w_ple = got[2]
        w_pw = got[3].reshape(CD, CD)
        dw = got[4]
        if l + 1 < DEPTH:
            w_in_next = got[5]
        c1 = _glu_conv(u, dw, row("dw_b"))
        c3, yc, h1, gate, pe, h2 = _layer_tail(
            c1, u, ya, h, p[l], row("conv_ln_g"), row("conv_ln_b"), w_pw, row("conv_out_g"), w_out,
            row("ple_norm_g"), w_gate, w_ple)
        saved.append(dict(h=h, u=u, hn=hn, o=o, ya=ya, tot=tot, c1=c1, c3=c3, yc=yc, h1=h1, gate=gate, pe=pe,
                          w_in=w_in, w_out=w_out, w_gate=w_gate, w_pw=w_pw, dw=dw))
        h = h2
    loss_blk, dh, dfg = _loss_head(h, tgt, sm["final_g"])
    small = [None] * DEPTH
    pending, partials, arrived = [], {}, {}
    pair_sum = lambda grads: _sum_pair(core, grads, _pair_exchange(grads))
    for l in reversed(range(DEPTH)):
        s = saved[l]
        row = lambda name: sm[name][l:l + 1]
        dh1, dy, dwg, dwp, dwo, dpg = _ple_out_bwd(
            dh, s["h1"], s["gate"], s["pe"], p[l], s["ya"], s["yc"], row("ple_norm_g"), s["w_gate"], s["w_out"])
        ag_t = jnp.tile(row("attn_out_g"), (1, AD // DH))
        do, dga, dgc, dc1, dwpw, dag, dcg, dlg, dlb = _branch_bwd(
            dy, s["o"], s["u"], s["c1"], s["c3"], ag_t, row("conv_ln_g"), row("conv_ln_b"), s["w_pw"],
            row("conv_out_g"), seg)
        dcv, dcgate, ddw, ddb = _conv_bwd(dc1, s["u"], s["dw"])
        tail = [dwo.reshape(NCHIP, 256, D), dwg.reshape(NCHIP, 256, D), dwp, dwpw.reshape(NCHIP, 128, CD), ddw]
        if l == 0:
            partials[(l, "tail")] = pair_sum(tail)
            pending.append((l, "tail"))
        send = [t for key in pending for t in partials[key]]
        dq, dk, dv, got = _attn_bwd(s["u"], do, s["tot"], send)
        for key in pending:
            arrived[key], got = got[:len(partials[key])], got[len(partials[key]):]
        pending = []
        du = jnp.concatenate([dq, dk, dv, dga, dcv, dcgate, dgc], axis=1)
        dwin = _inproj_dw(s["hn"], du)
        if l == 0:
            partials[(l, "w_in")] = pair_sum([dwin])
            dh, dng, arrived[(l, "w_in")] = _inproj_dx(du, s["w_in"], s["h"], row("norm_g"), dh1,
                                                      partials[(l, "w_in")])
        else:
            dh, dng, halves = _inproj_dx(du, s["w_in"], s["h"], row("norm_g"), dh1, grads=tail + [dwin])
            tail_p = _sum_pair(core, tail + [dwin], halves)
            partials[(l, "tail")], partials[(l, "w_in")] = tail_p[:-1], tail_p[-1:]
            pending = [(l, "tail"), (l, "w_in")]
        small[l] = dict(norm_g=dng, attn_out_g=dag.reshape(AD // DH, DH).sum(axis=0, keepdims=True), dw_b=ddb,
                        conv_ln_g=dlg, conv_ln_b=dlb, conv_out_g=dcg, ple_norm_g=dpg)
    big = []
    for l in range(DEPTH):
        ps = partials[(l, "w_in")] + partials[(l, "tail")]
        mine, theirs = _sum_chips_share(chip_idx, ps, arrived[(l, "w_in")] + arrived[(l, "tail")])
        big.append(dict(zip(BIG, [jnp.where(core[0] == 0, jnp.concatenate([m, t]), jnp.concatenate([t, m]))
                                  for m, t in zip(mine, theirs)])))
    return loss_blk[0, 0], dh, big, small, dfg


BIG = ("w_in", "w_out", "w_ple_gate", "w_ple", "w_pw", "dw_w")
SMALL2 = ("norm_g", "ple_norm_g", "dw_b", "conv_ln_g", "conv_ln_b", "conv_out_g", "attn_out_g")


def _pack_small(two, final):
    flat = jnp.concatenate([two[k].reshape(-1) for k in SMALL2] + [final.reshape(-1)])
    flat = jnp.concatenate([flat, jnp.zeros((SMALL_ROWS * LANE - flat.shape[0],), F32)])
    return flat.reshape(SMALL_ROWS, LANE)


def _unpack_small(packed, like_two, like_final):
    flat = packed.reshape(-1)
    out, off = {}, 0
    for k in SMALL2:
        n = like_two[k].size
        out[k] = flat[off:off + n].reshape(like_two[k].shape)
        off += n
    return out, flat[off:off + like_final.size].reshape(like_final.shape)


def kernel(x, p, norm_g, w_in, attn_out_g, dw_w, dw_b, conv_ln_g, conv_ln_b, w_pw, conv_out_g, w_out, ple_norm_g, w_ple_gate, w_ple, final_g, loss_target, m_norm_g, m_w_in, m_attn_out_g, m_dw_w, m_dw_b, m_conv_ln_g, m_conv_ln_b, m_w_pw, m_conv_out_g, m_w_out, m_ple_norm_g, m_w_ple_gate, m_w_ple, m_final_g, v_norm_g, v_w_in, v_attn_out_g, v_dw_w, v_dw_b, v_conv_ln_g, v_conv_ln_b, v_w_pw, v_conv_out_g, v_w_out, v_ple_norm_g, v_w_ple_gate, v_w_ple, v_final_g):
    W = dict(norm_g=norm_g, w_in=w_in, attn_out_g=attn_out_g, dw_w=dw_w, dw_b=dw_b, conv_ln_g=conv_ln_g,
             conv_ln_b=conv_ln_b, w_pw=w_pw, conv_out_g=conv_out_g, w_out=w_out, ple_norm_g=ple_norm_g,
             w_ple_gate=w_ple_gate, w_ple=w_ple, final_g=final_g)
    M = dict(norm_g=m_norm_g, w_in=m_w_in, attn_out_g=m_attn_out_g, dw_w=m_dw_w, dw_b=m_dw_b,
             conv_ln_g=m_conv_ln_g, conv_ln_b=m_conv_ln_b, w_pw=m_w_pw, conv_out_g=m_conv_out_g, w_out=m_w_out,
             ple_norm_g=m_ple_norm_g, w_ple_gate=m_w_ple_gate, w_ple=m_w_ple, final_g=m_final_g)
    V = dict(norm_g=v_norm_g, w_in=v_w_in, attn_out_g=v_attn_out_g, dw_w=v_dw_w, dw_b=v_dw_b,
             conv_ln_g=v_conv_ln_g, conv_ln_b=v_conv_ln_b, w_pw=v_w_pw, conv_out_g=v_conv_out_g, w_out=v_w_out,
             ple_norm_g=v_ple_norm_g, w_ple_gate=v_w_ple_gate, w_ple=v_w_ple, final_g=v_final_g)
    order = ("norm_g", "w_in", "attn_out_g", "dw_w", "dw_b", "conv_ln_g", "conv_ln_b", "w_pw", "conv_out_g",
             "w_out", "ple_norm_g", "w_ple_gate", "w_ple", "final_g")

    pad_taps = lambda a: jnp.pad(a, ((0, 0), (0, CWP - CW), (0, 0)))
    cast = dict(w_in=w_in.astype(BF16), w_out=w_out.astype(BF16), w_ple_gate=w_ple_gate.astype(BF16),
                w_ple=w_ple.astype(BF16), w_pw=w_pw.astype(BF16), dw_w=pad_taps(dw_w))
    shards = [{k: v[l] for k, v in cast.items()} for l in range(DEPTH)]
    xi, yi, ci = lax.axis_index("x"), lax.axis_index("y"), lax.axis_index("c")
    chip = 2 * xi + yi

    sm = {k: W[k] for k in SMALL2}
    sm["final_g"] = final_g.reshape(1, D)
    loss_part, grad_x, big, small, dfg = _local_step(x[0], p[:, 0], loss_target[0], sm, shards, chip, ci)
    g_big = {name: jnp.stack([big[l][name] for l in range(DEPTH)]).reshape(cast[name].shape) for name in BIG}

    small_two = {k: jnp.concatenate([small[l][k] for l in range(DEPTH)], axis=0) for k in SMALL2}
    g_small_packed = _small_allreduce(_pack_small(small_two, dfg).at[SMALL_ROWS - 1, LANE - 1].set(loss_part))
    loss = g_small_packed[SMALL_ROWS - 1, LANE - 1]
    g_small, g_final = _unpack_small(g_small_packed, {k: W[k] for k in SMALL2}, final_g)

    grads, deltas, new_m, new_v = {}, {}, {}, {}
    for name in BIG:
        wv = pad_taps(W[name]) if name == "dw_w" else W[name]
        mv = pad_taps(M[name]) if name == "dw_w" else M[name]
        vv = pad_taps(V[name]) if name == "dw_w" else V[name]
        gg = g_big[name]
        cols = wv.shape[-1]
        rows_total = wv.size // cols
        tile_rows = min(rows_total, 256)
        d2, m2, v2 = _adamw(wv.reshape(rows_total, cols), gg.reshape(rows_total, cols),
                            mv.reshape(rows_total, cols), vv.reshape(rows_total, cols), tile_rows)
        if name == "dw_w":
            cut = lambda a: a.reshape(DEPTH, CWP, LANE)[:, :CW]
            grads[name], deltas[name], new_m[name], new_v[name] = cut(gg), cut(d2), cut(m2), cut(v2)
        else:
            grads[name] = gg
            deltas[name], new_m[name], new_v[name] = (t.reshape(wv.shape) for t in (d2, m2, v2))
    ws = _pack_small({k: W[k] for k in SMALL2}, final_g)
    ms = _pack_small({k: M[k] for k in SMALL2}, m_final_g)
    vs = _pack_small({k: V[k] for k in SMALL2}, v_final_g)
    ds, nms, nvs = _adamw(ws, g_small_packed, ms, vs, SMALL_ROWS)
    for packed, dst in ((ds, deltas), (nms, new_m), (nvs, new_v)):
        two, fin = _unpack_small(packed, {k: W[k] for k in SMALL2}, final_g)
        dst.update(two)
        dst["final_g"] = fin
    grads.update(g_small)
    grads["final_g"] = g_final

    return (loss, grad_x[None], *[grads[n] for n in order], *[deltas[n] for n in order],
            *[new_m[n] for n in order], *[new_v[n] for n in order])
```

```python
import functools

import jax
import jax.numpy as jnp
from jax import lax
from jax.experimental import pallas as pl
from jax.experimental.pallas import tpu as pltpu

F32 = jnp.float32
BF16 = jnp.bfloat16

T = 2048
D = 1024
DIN = 3584
NCHIP = 4
SHW = DIN // NCHIP
AD = 512
CD = 512
DH = 64
CW = 31
CWP = 32
PLE = 256
DEPTH = 2
EPS = 1e-6
AQ = 256
HG = 4
GW = HG * DH
SR = HG * AQ
NG = AD // GW
LANE = 128
TM = 1024

ADAM_LR = 0.001
ADAM_B1 = 0.9
ADAM_B2 = 0.999
ADAM_EPS = 1e-08
ADAM_WD = 0.01
ADAM_STEP = 10

SMALL_PK = 16
LOSS_ROW = 15

VMEM_BIG = 56 * 1024 * 1024
MESH = pl.DeviceIdType.MESH


def _cp(sem=None, vmem=None):
    kw = {}
    if sem is not None:
        kw["dimension_semantics"] = sem
    if vmem is not None:
        kw["vmem_limit_bytes"] = vmem
    return pltpu.CompilerParams(**kw)


def _dot(a, b):
    return jnp.dot(a, b, preferred_element_type=F32)


def _dot_nt(a, b):
    return lax.dot_general(a, b, (((1,), (1,)), ((), ())), preferred_element_type=F32)


def _dot_tn(a, b):
    return lax.dot_general(a, b, (((0,), (0,)), ((), ())), preferred_element_type=F32)


def _dot2(x, m):
    hi = x.astype(BF16)
    lo = (x - hi.astype(F32)).astype(BF16)
    return _dot(hi, m) + _dot(lo, m)


def _sig(x):
    return 1.0 / (1.0 + jnp.exp(-x))


def _softplus(z):
    return jnp.maximum(z, 0.0) + jnp.log(1.0 + jnp.exp(-jnp.abs(z)))


def _rstd(x):
    return lax.rsqrt(jnp.mean(x * x, axis=-1, keepdims=True) + EPS)


def _rms_bwd(dy, x, r, g):
    dn = dy * g
    return r * dn - x * (r * r * r) * jnp.mean(dn * x, axis=-1, keepdims=True)


def _rms_inproj(h, g, w):
    tm = min(TM, T)

    def body(h_ref, g_ref, w_ref, u_ref, hn_ref, hn_s):
        @pl.when(pl.program_id(1) == 0)
        def _():
            hh = h_ref[...]
            hn = (hh * _rstd(hh) * g_ref[...]).astype(BF16)
            hn_s[...] = hn
            hn_ref[...] = hn
        u_ref[...] = _dot(hn_s[...], w_ref[0])

    return pl.pallas_call(
        body, name="rms_inproj", grid=(T // tm, NCHIP),
        in_specs=[pl.BlockSpec((tm, D), lambda i, k: (i, 0)),
                  pl.BlockSpec((1, D), lambda i, k: (0, 0)),
                  pl.BlockSpec((1, D, SHW), lambda i, k: (k, 0, 0))],
        out_specs=[pl.BlockSpec((tm, SHW), lambda i, k: (i, k)),
                   pl.BlockSpec((tm, D), lambda i, k: (i, 0))],
        out_shape=[jax.ShapeDtypeStruct((T, DIN), F32), jax.ShapeDtypeStruct((T, D), BF16)],
        scratch_shapes=[pltpu.VMEM((tm, D), BF16)],
        compiler_params=_cp(("arbitrary", "arbitrary"), VMEM_BIG),
    )(h, g, w)


def _attn_tiles():
    row = lax.broadcasted_iota(jnp.int32, (SR, AQ), 0) & (AQ - 1)
    col = lax.broadcasted_iota(jnp.int32, (SR, AQ), 1)
    tr = lax.broadcasted_iota(jnp.int32, (AQ, AQ), 0)
    tc = lax.broadcasted_iota(jnp.int32, (AQ, AQ), 1)
    lane_head = lax.broadcasted_iota(jnp.int32, (1, GW), 1) // DH
    return col < row, tr, tc, [lane_head == h for h in range(HG)]


def _stack_heads(t, heads):
    return jnp.concatenate([jnp.where(m, t, 0.0) for m in heads], axis=0)


def _unstack_heads(t, heads):
    out = t[:AQ]
    for h in range(1, HG):
        out = jnp.where(heads[h], t[h * AQ:(h + 1) * AQ], out)
    return out


def _tri_sum(x, tri):
    hi = x.astype(BF16)
    lo = (x - hi.astype(F32)).astype(BF16)
    both = _dot(jnp.concatenate([hi, lo], axis=0), tri)
    return both[:SR] + both[SR:]


def _scatter_copies(ps, gots, send_sems, recv_sems):
    x, y, c = _place()
    peers = [(1 - x, y), (x, 1 - y), (1 - x, 1 - y)]
    return [pltpu.make_async_remote_copy(
        src_ref=ps[a].at[2 * px + py], dst_ref=gots[a].at[r], send_sem=send_sems.at[3 * a + r],
        recv_sem=recv_sems.at[3 * a + r], device_id=(px, py, c), device_id_type=MESH)
        for a in range(len(ps)) for r, (px, py) in enumerate(peers)]


GATHER_SEMS = 7


def _gather_tree_copies(ins, outs, send_sems, recv_sems):
    x, y, c = _place()
    me, xn, yn, dg = 2 * x + y, 2 * (1 - x) + y, 2 * x + (1 - y), 2 * (1 - x) + (1 - y)
    to_x, to_y, sibling = (1 - x, y, c), (x, 1 - y, c), (x, y, 1 - c)
    direct, relayed, passed = [], [], []
    for a in range(len(ins)):
        half = ins[a].shape[0] // 2
        mine = pl.ds(c * half, half)
        first, second = pl.ds(c * half, half // 2), pl.ds(c * half + half // 2, half // 2)

        def copy(i, src, dst, to, k=GATHER_SEMS * a):
            return pltpu.make_async_remote_copy(src_ref=src, dst_ref=dst, send_sem=send_sems.at[k + i],
                                                recv_sem=recv_sems.at[k + i], device_id=to, device_id_type=MESH)

        own, slot = ins[a].at[mine], outs[a].at[me, mine]
        direct += [copy(0, own, slot, to_x), copy(1, own, slot, to_y)]
        relayed += [copy(2, outs[a].at[xn, first], outs[a].at[xn, first], to_y),
                    copy(3, outs[a].at[yn, second], outs[a].at[yn, second], to_x)]
        passed += [copy(4 + i, outs[a].at[j, mine], outs[a].at[j, mine], sibling) for i, j in enumerate((xn, yn, dg))]
    return direct, relayed, passed


def _pair_copies(ins, outs, send_sems, recv_sems):
    x, y, c = _place()
    copies = []
    for a in range(len(ins)):
        half = ins[a].shape[1] // 2
        copies.append(pltpu.make_async_remote_copy(
            src_ref=ins[a].at[:, pl.ds((1 - c) * half, half), :], dst_ref=outs[a], send_sem=send_sems.at[a],
            recv_sem=recv_sems.at[a], device_id=(x, y, 1 - c), device_id_type=MESH))
    return copies


def _host(body, grid, n_in, n_out, n_x, make_copies, mids=()):
    if not n_x:
        return body

    def hosting(*refs):
        a, b = n_in + n_x, n_in + 2 * n_x + n_out
        copies = make_copies(refs[n_in:a], refs[a + n_out:b], refs[-2], refs[-1])
        stages = copies if isinstance(copies, tuple) else (copies,)
        ids = [pl.program_id(d) for d in range(len(grid))]
        at = lambda step: functools.reduce(jnp.logical_and, [i == s for i, s in zip(ids, step)])

        @pl.when(at([0] * len(grid)))
        def _():
            for cp in stages[0]:
                cp.start()

        for before, after, step in zip(stages, stages[1:], mids):
            @pl.when(at(step))
            def _(before=before, after=after):
                for cp in before:
                    cp.wait_recv()
                for cp in after:
                    cp.start()

        body(*refs[:n_in], *refs[a:a + n_out], *refs[b:-2])

        @pl.when(at([g - 1 for g in grid]))
        def _():
            for cp in stages[-1]:
                cp.wait_recv()
            for stage in stages:
                for cp in stage:
                    cp.wait_send()

    return hosting


def _hosted_sems(n_x, per_array=3):
    n = per_array * n_x
    return [pltpu.SemaphoreType.DMA((n,)), pltpu.SemaphoreType.DMA((n,))] if n_x else []


RC = 256


def _chunk_causal(r):
    row = lax.broadcasted_iota(jnp.int32, (RC, AQ), 0) + (r * RC) % AQ
    return lax.broadcasted_iota(jnp.int32, (RC, AQ), 1) < row


def _attn_fwd(u, agw, shards=()):
    n = len(shards)
    grid = (T // AQ,)

    def body(q_ref, k_ref, v_ref, g_ref, ag_ref, o_ref, y_ref, tot_ref,
             kb_s, vb_s, qs_s, z_s, zs_s, lmb_s, suf_s, att_s, acc_s, run_s):
        qi = pl.program_id(0)

        @pl.when(qi == 0)
        def _():
            kb_s[...] = k_ref[...].astype(BF16)
            vb_s[...] = v_ref[...].astype(BF16)

        _, tr, tc, heads = _attn_tiles()
        upper = (tr > tc).astype(BF16)
        for g in range(NG):
            qs_s[g] = _stack_heads(q_ref[:, g * GW:(g + 1) * GW] * 0.125, heads).astype(BF16)
        acc_s[...] = jnp.zeros_like(acc_s)
        run_s[...] = jnp.zeros_like(run_s)

        def block(kb, masked):
            k0 = pl.multiple_of(kb * AQ, AQ)
            for g in range(NG):
                lanes = pl.ds(g * GW, GW)
                z_s[g] = _dot_nt(qs_s[g], kb_s[pl.ds(k0, AQ), lanes])
                for r in range(SR // RC):
                    rows = pl.ds(r * RC, RC)
                    z = z_s[g, rows, :]
                    zs = jnp.minimum(z, 0.0) - jnp.log(1.0 + jnp.exp(-jnp.abs(z)))
                    lm = zs - z
                    if masked:
                        lm = jnp.where(_chunk_causal(r), lm, 0.0)
                    run = run_s[g, rows, :]
                    zs_s[g, rows, :] = zs + run[:, 0:1]
                    hi = lm.astype(BF16)
                    lmb_s[g, rows, :] = hi
                    lmb_s[g, pl.ds(SR + r * RC, RC), :] = (lm - hi.astype(F32)).astype(BF16)
                    run_s[g, rows, :] = run + jnp.sum(lm, axis=1, keepdims=True)
                suf_s[g] = _dot(lmb_s[g], upper)
                for r in range(SR // RC):
                    rows = pl.ds(r * RC, RC)
                    att = jnp.exp(zs_s[g, rows, :] + suf_s[g, rows, :] + suf_s[g, pl.ds(SR + r * RC, RC), :])
                    if masked:
                        att = jnp.where(_chunk_causal(r), att, 0.0)
                    att_s[g, rows, :] = att.astype(BF16)
                acc_s[g] += _dot(att_s[g], vb_s[pl.ds(k0, AQ), lanes])

        block(qi, True)

        def step(i, c):
            block(qi - 1 - i, False)
            return c

        lax.fori_loop(0, qi, step, 0)
        gate = g_ref[...]
        agv = ag_ref[...]
        for g in range(NG):
            lanes = slice(g * GW, (g + 1) * GW)
            o = _unstack_heads(acc_s[g], heads)
            osq = o * o
            ms = jnp.zeros((AQ, GW), F32)
            for m in heads:
                ms = jnp.where(m, jnp.sum(jnp.where(m, osq, 0.0), axis=1, keepdims=True), ms)
            gg = gate[:, lanes]
            o_ref[:, lanes] = o
            y_ref[:, lanes] = (o * lax.rsqrt(ms * (1.0 / DH) + EPS) * agv[:, lanes] * (gg * _sig(gg))).astype(BF16)
            tot_ref[g] = _unstack_heads(jnp.broadcast_to(run_s[g][:, 0:1], (SR, GW)), heads)

    tile = lambda dt, rows=SR: pltpu.VMEM((NG, rows, AQ), dt)
    scratch = [pltpu.VMEM((T, AD), BF16), pltpu.VMEM((T, AD), BF16), pltpu.VMEM((NG, SR, GW), BF16),
               tile(F32), tile(F32), tile(BF16, 2 * SR), tile(F32, 2 * SR), tile(BF16),
               pltpu.VMEM((NG, SR, GW), F32), pltpu.VMEM((NG, SR, LANE), F32)]
    col = lambda j: pl.BlockSpec((AQ, AD), lambda qi: (qi, j))
    res = pl.pallas_call(
        _host(body, grid, 5, 3, n, _gather_tree_copies, mids=((grid[0] * 5 // 8,), (grid[0] * 7 // 8,))), name="attn_fwd", grid=grid,
        in_specs=[col(0), pl.BlockSpec((T, AD), lambda qi: (0, 1)), pl.BlockSpec((T, AD), lambda qi: (0, 2)),
                  col(3), pl.BlockSpec((1, AD), lambda qi: (0, 0))] + [HBM_SPEC] * n,
        out_specs=[col(0), col(0), pl.BlockSpec((NG, AQ, GW), lambda qi: (0, qi, 0))] + [HBM_SPEC] * n,
        out_shape=[jax.ShapeDtypeStruct((T, AD), F32), jax.ShapeDtypeStruct((T, AD), BF16),
                   jax.ShapeDtypeStruct((NG, T, GW), F32)]
        + [jax.ShapeDtypeStruct((NCHIP,) + s.shape, s.dtype) for s in shards],
        scratch_shapes=scratch + _hosted_sems(n, GATHER_SEMS),
        compiler_params=_cp(("arbitrary",), VMEM_BIG),
    )(u, u, u, u, agw, *shards)
    return res[0], res[1], res[2], list(res[3:])


def _glu_conv(u, dw, db):
    tr = 256

    def body(cv_ref, cg_ref, w_ref, b_ref, c1_ref, pad_s):
        pad_s[pl.ds(0, CWP), :] = jnp.zeros((CWP, LANE), F32)
        pad_s[pl.ds(CWP, T), :] = cv_ref[...] * _sig(cg_ref[...])
        wv = w_ref[0]
        bias = b_ref[...]

        def tile(i, carry):
            r0 = pl.multiple_of(i * tr, tr)
            acc = jnp.zeros((tr, LANE), F32) + bias
            for w in range(CW):
                acc = acc + pad_s[pl.ds(r0 + (CWP - CW + 1) + w, tr), :] * wv[w:w + 1, :]
            c1_ref[pl.ds(r0, tr), :] = acc
            return carry

        lax.fori_loop(0, T // tr, tile, 0)

    return pl.pallas_call(
        body, name="glu_conv", grid=(CD // LANE,),
        in_specs=[pl.BlockSpec((T, LANE), lambda cb: (0, 16 + cb)),
                  pl.BlockSpec((T, LANE), lambda cb: (0, 20 + cb)),
                  pl.BlockSpec((1, CWP, LANE), lambda cb: (cb, 0, 0)),
                  pl.BlockSpec((1, LANE), lambda cb: (0, cb))],
        out_specs=pl.BlockSpec((T, LANE), lambda cb: (0, cb)),
        out_shape=jax.ShapeDtypeStruct((T, CD), F32),
        scratch_shapes=[pltpu.VMEM((T + CWP, LANE), F32)],
        compiler_params=_cp(("arbitrary",)),
    )(u, u, dw, db)


def _ln_silu(c1, lg, lb):
    mu = jnp.mean(c1, axis=-1, keepdims=True)
    xc = c1 - mu
    rs = lax.rsqrt(jnp.mean(xc * xc, axis=-1, keepdims=True) + EPS)
    xh = xc * rs
    ln = xh * lg + lb
    s = _sig(ln)
    return xh, rs, ln, s


def _layer_tail(c1, u, ya, h, p, lg, lb, wpw, cg, wout, pg, wgate, wple):
    tm = 256

    def body(c1_ref, gc_ref, ya_ref, h_ref, p_ref, lg_ref, lb_ref, wpw_ref, cg_ref, wout_ref,
             pg_ref, wgate_ref, wple_ref, c3_ref, yc_ref, h1_ref, gate_ref, pe_ref, h2_ref):
        _, _, ln, s = _ln_silu(c1_ref[...], lg_ref[...], lb_ref[...])
        c2 = (ln * s).astype(BF16)
        c3 = _dot(c2, wpw_ref[...])
        gc = gc_ref[...]
        yc = (c3 * _rstd(c3) * cg_ref[...] * (gc * _sig(gc))).astype(BF16)
        c3_ref[...] = c3
        yc_ref[...] = yc
        y = _dot(ya_ref[...], wout_ref[pl.ds(0, AD), :]) + _dot(yc, wout_ref[pl.ds(AD, CD), :])
        h1 = h_ref[...] + y
        hn2 = (h1 * _rstd(h1) * pg_ref[...]).astype(BF16)
        gate = _sig(_dot(hn2, wgate_ref[...]))
        pb = p_ref[...].astype(BF16)
        pe = jnp.concatenate([_dot(pb, wple_ref[k]) for k in range(NCHIP)], axis=1)
        h1_ref[...] = h1
        gate_ref[...] = gate.astype(BF16)
        pe_ref[...] = pe.astype(BF16)
        h2_ref[...] = h1 + pe * gate

    row = lambda w: pl.BlockSpec((tm, w), lambda i: (i, 0))
    full = lambda *s: pl.BlockSpec(s, lambda i: (0,) * len(s))
    return pl.pallas_call(
        body, name="layer_tail", grid=(T // tm,),
        in_specs=[row(CD), pl.BlockSpec((tm, CD), lambda i: (i, 6)), row(AD), row(D), row(PLE),
                  full(1, CD), full(1, CD), full(CD, CD), full(1, CD), full(D, D),
                  full(1, D), full(D, D), full(NCHIP, PLE, PLE)],
        out_specs=[row(CD), row(CD), row(D), row(D), row(D), row(D)],
        out_shape=[jax.ShapeDtypeStruct((T, CD), F32), jax.ShapeDtypeStruct((T, CD), BF16),
                   jax.ShapeDtypeStruct((T, D), F32), jax.ShapeDtypeStruct((T, D), BF16),
                   jax.ShapeDtypeStruct((T, D), BF16), jax.ShapeDtypeStruct((T, D), F32)],
        compiler_params=_cp(("arbitrary",), VMEM_BIG),
    )(c1, u, ya, h, p, lg, lb, wpw, cg, wout, pg, wgate, wple)


def _loss_head(h, tgt, fg):
    tm = 256

    def body(h_ref, t_ref, g_ref, loss_ref, dh_ref, dg_ref):
        @pl.when(pl.program_id(0) == 0)
        def _():
            loss_ref[...] = jnp.zeros_like(loss_ref)
            dg_ref[...] = jnp.zeros_like(dg_ref)
        hh = h_ref[...]
        g = g_ref[...]
        r = _rstd(hh)
        e = hh * r * g - t_ref[...]
        loss_ref[...] += 0.5 * jnp.sum(jnp.mean(e * e, axis=-1, keepdims=True))
        dy = e * (1.0 / D)
        dg_ref[...] += jnp.sum(dy * hh * r, axis=0, keepdims=True)
        dh_ref[...] = _rms_bwd(dy, hh, r, g)

    return pl.pallas_call(
        body, name="loss_head", grid=(T // tm,),
        in_specs=[pl.BlockSpec((tm, D), lambda i: (i, 0)), pl.BlockSpec((tm, D), lambda i: (i, 0)),
                  pl.BlockSpec((1, D), lambda i: (0, 0))],
        out_specs=[pl.BlockSpec((8, LANE), lambda i: (0, 0)), pl.BlockSpec((tm, D), lambda i: (i, 0)),
                   pl.BlockSpec((1, D), lambda i: (0, 0))],
        out_shape=[jax.ShapeDtypeStruct((8, LANE), F32), jax.ShapeDtypeStruct((T, D), F32),
                   jax.ShapeDtypeStruct((1, D), F32)],
        compiler_params=_cp(("arbitrary",)),
    )(h, tgt, fg)


def _ple_out_bwd(dh2, h1, gate, pe, p, ya, yc, pg, wgate, wout):
    tm = 256

    def body(dh2_ref, h1_ref, gate_ref, pe_ref, p_ref, ya_ref, yc_ref, pg_ref, wgate_ref, wout_ref,
             dh1_ref, dy_ref, dwg_ref, dwp_ref, dwo_ref, dpg_ref):
        @pl.when(pl.program_id(0) == 0)
        def _():
            dwg_ref[...] = jnp.zeros_like(dwg_ref)
            dwp_ref[...] = jnp.zeros_like(dwp_ref)
            dwo_ref[...] = jnp.zeros_like(dwo_ref)
            dpg_ref[...] = jnp.zeros_like(dpg_ref)
        dh2 = dh2_ref[...]
        h1 = h1_ref[...]
        gate = gate_ref[...].astype(F32)
        pg = pg_ref[...]
        dpe = (dh2 * gate).astype(BF16)
        dgp = (dh2 * pe_ref[...].astype(F32) * gate * (1.0 - gate)).astype(BF16)
        r = _rstd(h1)
        hn = h1 * r
        dwg_ref[...] += _dot_tn((hn * pg).astype(BF16), dgp)
        dhn2 = _dot_nt(dgp, wgate_ref[...])
        dpg_ref[...] += jnp.sum(dhn2 * hn, axis=0, keepdims=True)
        dh1 = dh2 + _rms_bwd(dhn2, h1, r, pg)
        pb = p_ref[...].astype(BF16)
        for k in range(NCHIP):
            dwp_ref[k] += _dot_tn(pb, dpe[:, k * PLE:(k + 1) * PLE])
        dh1b = dh1.astype(BF16)
        dy_ref[...] = _dot_nt(dh1b, wout_ref[...])
        dwo_ref[pl.ds(0, AD), :] += _dot_tn(ya_ref[...], dh1b)
        dwo_ref[pl.ds(AD, CD), :] += _dot_tn(yc_ref[...], dh1b)
        dh1_ref[...] = dh1

    row = lambda w: pl.BlockSpec((tm, w), lambda i: (i, 0))
    full = lambda *s: pl.BlockSpec(s, lambda i: (0,) * len(s))
    return pl.pallas_call(
        body, name="ple_out_bwd", grid=(T // tm,),
        in_specs=[row(D), row(D), row(D), row(D), row(PLE), row(AD), row(CD),
                  full(1, D), full(D, D), full(D, D)],
        out_specs=[row(D), row(D), full(D, D), full(NCHIP, PLE, PLE), full(D, D), full(1, D)],
        out_shape=[jax.ShapeDtypeStruct((T, D), F32), jax.ShapeDtypeStruct((T, D), F32),
                   jax.ShapeDtypeStruct((D, D), F32), jax.ShapeDtypeStruct((NCHIP, PLE, PLE), F32),
                   jax.ShapeDtypeStruct((D, D), F32), jax.ShapeDtypeStruct((1, D), F32)],
        compiler_params=_cp(("arbitrary",), VMEM_BIG),
    )(dh2, h1, gate, pe, p, ya, yc, pg, wgate, wout)


def _branch_bwd(dy, o, u, c1, c3, ag, lg, lb, wpw, cg, seg):
    tm = 256

    def body(dya_ref, dyc_ref, o_ref, ga_ref, gc_ref, c1_ref, c3_ref, ag_ref, lg_ref, lb_ref, wpw_ref,
             cg_ref, seg_ref, do_ref, dga_ref, dgc_ref, dc1_ref, dwpw_ref, dag_ref, dcg_ref, dlg_ref, dlb_ref):
        @pl.when(pl.program_id(0) == 0)
        def _():
            for r_ in (dwpw_ref, dag_ref, dcg_ref, dlg_ref, dlb_ref):
                r_[...] = jnp.zeros_like(r_)
        dya = dya_ref[...]
        o = o_ref[...]
        ga = ga_ref[...]
        ag_v = ag_ref[...]
        seg_m = seg_ref[...]
        r = lax.rsqrt(_dot2(o * o, seg_m) * (1.0 / DH) + EPS)
        onr = o * r
        sg = _sig(ga)
        dga_ref[...] = (dya * (onr * ag_v) * (sg * (1.0 + ga * (1.0 - sg)))).astype(BF16)
        don = dya * (ga * sg)
        dag_ref[...] += jnp.sum(don * onr, axis=0, keepdims=True)
        dn = don * ag_v
        do_ref[...] = r * dn - o * (r * r * r) * (_dot2(dn * o, seg_m) * (1.0 / DH))
        dyc = dyc_ref[...]
        c3 = c3_ref[...]
        gc = gc_ref[...]
        cg_v = cg_ref[...]
        r3 = _rstd(c3)
        cn = c3 * r3
        sc = _sig(gc)
        dgc_ref[...] = (dyc * (cn * cg_v) * (sc * (1.0 + gc * (1.0 - sc)))).astype(BF16)
        dcn = dyc * (gc * sc)
        dcg_ref[...] += jnp.sum(dcn * cn, axis=0, keepdims=True)
        dc3 = _rms_bwd(dcn, c3, r3, cg_v).astype(BF16)
        lg_v = lg_ref[...]
        xh, rs, ln, s = _ln_silu(c1_ref[...], lg_v, lb_ref[...])
        c2 = (ln * s).astype(BF16)
        dwpw_ref[...] += _dot_tn(c2, dc3)
        dc2 = _dot_nt(dc3, wpw_ref[...])
        dln = dc2 * (s * (1.0 + ln * (1.0 - s)))
        dlb_ref[...] += jnp.sum(dln, axis=0, keepdims=True)
        dlg_ref[...] += jnp.sum(dln * xh, axis=0, keepdims=True)
        dxh = dln * lg_v
        dc1_ref[...] = rs * (dxh - jnp.mean(dxh, axis=-1, keepdims=True)
                             - xh * jnp.mean(dxh * xh, axis=-1, keepdims=True))

    half = lambda j: pl.BlockSpec((tm, 512), lambda i: (i, j))
    full = lambda *s: pl.BlockSpec(s, lambda i: (0,) * len(s))
    vec = jax.ShapeDtypeStruct((1, 512), F32)
    act = jax.ShapeDtypeStruct((T, 512), F32)
    return pl.pallas_call(
        body, name="branch_bwd", grid=(T // tm,),
        in_specs=[half(0), half(1), half(0), half(3), half(6), half(0), half(0),
                  full(1, AD), full(1, CD), full(1, CD), full(CD, CD), full(1, CD), full(AD, AD)],
        out_specs=[half(0), half(0), half(0), half(0), full(CD, CD), full(1, 512), full(1, 512),
                   full(1, 512), full(1, 512)],
        out_shape=[act, jax.ShapeDtypeStruct((T, 512), BF16), jax.ShapeDtypeStruct((T, 512), BF16), act,
                   jax.ShapeDtypeStruct((CD, CD), F32), vec, vec, vec, vec],
        compiler_params=_cp(("arbitrary",), VMEM_BIG),
    )(dy, dy, o, u, u, c1, c3, ag, lg, lb, wpw, cg, seg)


def _conv_bwd(dc1, u, dw):
    tr = 64
    off = CWP - CW + 1

    def body(d_ref, cv_ref, cg_ref, w_ref, dcv_ref, dcg_ref, ddw_ref, ddb_ref, padc_s, padd_s, acc_s):
        cv = cv_ref[...]
        sg = _sig(cg_ref[...])
        padc_s[pl.ds(0, CWP), :] = jnp.zeros((CWP, LANE), F32)
        padc_s[pl.ds(CWP, T), :] = cv * sg
        padd_s[pl.ds(0, T), :] = d_ref[...]
        padd_s[pl.ds(T, CWP), :] = jnp.zeros((CWP, LANE), F32)
        acc_s[...] = jnp.zeros_like(acc_s)
        wv = w_ref[0]

        def tile(i, carry):
            r0 = pl.multiple_of(i * tr, tr)
            dt = padd_s[pl.ds(r0, tr), :]
            dc0 = jnp.zeros((tr, LANE), F32)
            for w in range(CW):
                dc0 = dc0 + padd_s[pl.ds(r0 + (CW - 1) - w, tr), :] * wv[w:w + 1, :]
                prod = dt * padc_s[pl.ds(r0 + off + w, tr), :]
                acc_s[w] += jnp.sum(prod.reshape(tr // 8, 8, LANE), axis=0)
            cvt = cv_ref[pl.ds(r0, tr), :]
            sgt = _sig(cg_ref[pl.ds(r0, tr), :])
            dcv_ref[pl.ds(r0, tr), :] = (dc0 * sgt).astype(BF16)
            dcg_ref[pl.ds(r0, tr), :] = (dc0 * cvt * sgt * (1.0 - sgt)).astype(BF16)
            return carry

        lax.fori_loop(0, T // tr, tile, 0)
        ddw_ref[0] = jnp.sum(acc_s[...], axis=1)
        ddb_ref[...] = jnp.sum(d_ref[...], axis=0, keepdims=True)

    col = lambda j: pl.BlockSpec((T, LANE), lambda cb: (0, j + cb))
    return pl.pallas_call(
        body, name="conv_bwd", grid=(CD // LANE,),
        in_specs=[col(0), col(16), col(20), pl.BlockSpec((1, CWP, LANE), lambda cb: (cb, 0, 0))],
        out_specs=[col(0), col(0), pl.BlockSpec((1, CWP, LANE), lambda cb: (cb, 0, 0)),
                   pl.BlockSpec((1, LANE), lambda cb: (0, cb))],
        out_shape=[jax.ShapeDtypeStruct((T, CD), BF16), jax.ShapeDtypeStruct((T, CD), BF16),
                   jax.ShapeDtypeStruct((NCHIP, CWP, LANE), F32), jax.ShapeDtypeStruct((1, CD), F32)],
        scratch_shapes=[pltpu.VMEM((T + CWP, LANE), F32), pltpu.VMEM((T + CWP, LANE), F32),
                        pltpu.VMEM((CWP, 8, LANE), F32)],
        compiler_params=_cp(("arbitrary",)),
    )(dc1, u, u, dw)


def _attn_bwd(u, do, tot, partials=()):
    n_x = len(partials)
    grid = (T // AQ,)

    def body(q_ref, k_ref, v_ref, do_ref, tot_ref, dq_ref, dk_ref, dv_ref, kb_s, vb_s, dk_s, dv_s):
        qi = pl.program_id(0)

        @pl.when(qi == 0)
        def _():
            kb_s[...] = k_ref[...].astype(BF16)
            vb_s[...] = v_ref[...].astype(BF16)
            dk_s[...] = jnp.zeros_like(dk_s)
            dv_s[...] = jnp.zeros_like(dv_s)

        causal, tr, tc, heads = _attn_tiles()
        upper = (tr > tc).astype(BF16)
        lower = (tr < tc).astype(BF16)
        qs, qus, dos, tots = [], [], [], []
        for g in range(NG):
            lanes = slice(g * GW, (g + 1) * GW)
            q = q_ref[:, lanes]
            qs.append(_stack_heads(q * 0.125, heads).astype(BF16))
            qus.append(_stack_heads(q, heads).astype(BF16))
            dos.append(_stack_heads(do_ref[:, lanes], heads).astype(BF16))
            totv = tot_ref[g]
            tots.append(jnp.concatenate([totv[:, h * DH:h * DH + 1] for h in range(HG)], axis=0))

        def block(kb, carry, masked):
            k0 = pl.multiple_of(kb * AQ, AQ)
            out = []
            for g in range(NG):
                lanes = pl.ds(g * GW, GW)
                lm_left, dl_left, dq = carry[g]
                kk = kb_s[pl.ds(k0, AQ), lanes]
                vv = vb_s[pl.ds(k0, AQ), lanes]
                z = _dot_nt(qs[g], kk)
                sp = _softplus(z)
                lm = jnp.where(causal, -sp, 0.0) if masked else -sp
                lm_incl = lm_left + jnp.sum(lm, axis=1, keepdims=True)
                att = jnp.exp((z - sp) + _tri_sum(lm, upper) + (tots[g] - lm_incl))
                if masked:
                    att = jnp.where(causal, att, 0.0)
                dl = att * _dot_nt(dos[g], vv)
                dv_s[pl.ds(k0, AQ), lanes] += _dot_tn(att.astype(BF16), dos[g])
                prefix = dl_left + _tri_sum(dl, lower)
                beta = jnp.exp(z - sp)
                dz = (1.0 - beta) * dl - beta * prefix
                if masked:
                    dz = jnp.where(causal, dz, 0.0)
                dzs = (dz * 0.125).astype(BF16)
                dk_s[pl.ds(k0, AQ), lanes] += _dot_tn(dzs, qus[g])
                out.append((lm_incl, dl_left + jnp.sum(dl, axis=1, keepdims=True), dq + _dot(dzs, kk)))
            return tuple(out)

        zero = jnp.zeros((SR, 1), F32)
        init = tuple((zero, zero, jnp.zeros((SR, GW), F32)) for _ in range(NG))
        carry = lax.fori_loop(0, qi, lambda kb, c: block(kb, c, False), init)
        carry = block(qi, carry, True)
        for g in range(NG):
            dq_ref[:, g * GW:(g + 1) * GW] = _unstack_heads(carry[g][2], heads).astype(BF16)

        @pl.when(qi == grid[0] - 1)
        def _():
            dk_ref[...] = dk_s[...].astype(BF16)
            dv_ref[...] = dv_s[...].astype(BF16)

    col = lambda j: pl.BlockSpec((AQ, AD), lambda qi: (qi, j))
    whole = lambda j: pl.BlockSpec((T, AD), lambda qi: (0, j), pipeline_mode=pl.Buffered(1))
    res = pl.pallas_call(
        _host(body, grid, 5, 3, n_x, _scatter_copies), name="attn_bwd", grid=grid,
        in_specs=[col(0), whole(1), whole(2), col(0), pl.BlockSpec((NG, AQ, GW), lambda qi: (0, qi, 0))]
        + [HBM_SPEC] * n_x,
        out_specs=[col(0), whole(0), whole(0)] + [HBM_SPEC] * n_x,
        out_shape=[jax.ShapeDtypeStruct((T, AD), BF16)] * 3
        + [jax.ShapeDtypeStruct((NCHIP - 1,) + a.shape[1:], a.dtype) for a in partials],
        scratch_shapes=[pltpu.VMEM((T, AD), BF16), pltpu.VMEM((T, AD), BF16), pltpu.VMEM((T, AD), F32),
                        pltpu.VMEM((T, AD), F32)] + _hosted_sems(n_x),
        compiler_params=_cp(("arbitrary",), VMEM_BIG),
    )(u, u, u, do, tot, *partials)
    return res[0], res[1], res[2], list(res[3:])


def _inproj_dw(hn, du):
    tm = min(TM, T)

    def body(hn_ref, du_ref, dw_ref):
        @pl.when(pl.program_id(1) == 0)
        def _():
            dw_ref[...] = jnp.zeros_like(dw_ref)
        dw_ref[0] += _dot_tn(hn_ref[...], du_ref[...])

    return pl.pallas_call(
        body, name="inproj_dw", grid=(NCHIP, T // tm),
        in_specs=[pl.BlockSpec((tm, D), lambda k, i: (i, 0)), pl.BlockSpec((tm, SHW), lambda k, i: (i, k))],
        out_specs=pl.BlockSpec((1, D, SHW), lambda k, i: (k, 0, 0)),
        out_shape=jax.ShapeDtypeStruct((NCHIP, D, SHW), F32),
        compiler_params=_cp(("arbitrary", "arbitrary"), VMEM_BIG),
    )(hn, du)


def _inproj_dx(du, w, h, g, dres, partials=(), grads=()):
    tm = min(TM, T)
    sent = list(partials) + list(grads)
    n_x = len(sent)
    grid = (T // tm, NCHIP)
    if grads:
        landing = [jax.ShapeDtypeStruct((NCHIP, a.shape[1] // 2, a.shape[2]), F32) for a in grads]
    else:
        landing = [jax.ShapeDtypeStruct((NCHIP - 1,) + a.shape[1:], a.dtype) for a in partials]

    def body(du_ref, w_ref, h_ref, g_ref, dres_ref, dh_ref, dg_ref, acc_s):
        i, k = pl.program_id(0), pl.program_id(1)

        @pl.when(jnp.logical_and(i == 0, k == 0))
        def _():
            dg_ref[...] = jnp.zeros_like(dg_ref)

        @pl.when(k == 0)
        def _():
            acc_s[...] = _dot_nt(du_ref[...], w_ref[0])

        @pl.when(k > 0)
        def _():
            acc_s[...] += _dot_nt(du_ref[...], w_ref[0])

        @pl.when(k == NCHIP - 1)
        def _():
            hh = h_ref[...]
            r = _rstd(hh)
            dhn = acc_s[...]
            dg_ref[...] += jnp.sum(dhn * hh * r, axis=0, keepdims=True)
            dh_ref[...] = dres_ref[...] + _rms_bwd(dhn, hh, r, g_ref[...])

    res = pl.pallas_call(
        _host(body, grid, 5, 2, n_x, _pair_copies if grads else _scatter_copies), name="inproj_dx", grid=grid,
        in_specs=[pl.BlockSpec((tm, SHW), lambda i, k: (i, k)),
                  pl.BlockSpec((1, D, SHW), lambda i, k: (k, 0, 0)),
                  pl.BlockSpec((tm, D), lambda i, k: (i, 0)),
                  pl.BlockSpec((1, D), lambda i, k: (0, 0)),
                  pl.BlockSpec((tm, D), lambda i, k: (i, 0))] + [HBM_SPEC] * n_x,
        out_specs=[pl.BlockSpec((tm, D), lambda i, k: (i, 0)), pl.BlockSpec((1, D), lambda i, k: (0, 0))]
        + [HBM_SPEC] * n_x,
        out_shape=[jax.ShapeDtypeStruct((T, D), F32), jax.ShapeDtypeStruct((1, D), F32)] + landing,
        scratch_shapes=[pltpu.VMEM((tm, D), F32)] + _hosted_sems(n_x),
        compiler_params=_cp(("arbitrary", "arbitrary"), VMEM_BIG),
    )(du, w, h, g, dres, *sent)
    return res[0], res[1], list(res[2:])


def _sum_pair(core, grads, gots):
    n = len(grads)

    def body(c_ref, *refs):
        for a in range(n):
            refs[2 * n + a][...] = (refs[a][...] + refs[n + a][...]).astype(BF16)

    mine = [pl.BlockSpec((1,) + s.shape[1:], lambda k, c: (k, c[0], 0)) for s in gots]
    same = [pl.BlockSpec((1,) + s.shape[1:], lambda k, c: (k, 0, 0)) for s in gots]
    return pl.pallas_call(
        body, name="sum_pair",
        grid_spec=pltpu.PrefetchScalarGridSpec(
            num_scalar_prefetch=1, grid=(NCHIP,), in_specs=mine + same, out_specs=same),
        out_shape=[jax.ShapeDtypeStruct(s.shape, BF16) for s in gots],
        compiler_params=_cp(("arbitrary",), VMEM_BIG),
    )(core, *grads, *gots)


def _sum_chips_share(chip, partials, gots):
    n = len(partials)

    def body(c_ref, *refs):
        mine, theirs = refs[2 * n:3 * n], refs[3 * n:4 * n]
        for a in range(n):
            acc = refs[a][0].astype(F32)
            for j in range(NCHIP - 1):
                acc = acc + refs[n + a][j].astype(F32)
            mine[a][...] = acc
        x, y, c = _place()
        copies = [pltpu.make_async_remote_copy(
            src_ref=mine[a], dst_ref=theirs[a], send_sem=refs[4 * n].at[a], recv_sem=refs[4 * n + 1].at[a],
            device_id=(x, y, 1 - c), device_id_type=MESH) for a in range(n)]
        for cp in copies:
            cp.start()
        for cp in copies:
            cp.wait()

    halves = [jax.ShapeDtypeStruct(s.shape[1:], F32) for s in partials]
    res = pl.pallas_call(
        body, name="sum_chips_share",
        grid_spec=pltpu.PrefetchScalarGridSpec(
            num_scalar_prefetch=1, grid=(1,),
            in_specs=[pl.BlockSpec((1,) + s.shape[1:], lambda i, c: (c[0], 0, 0)) for s in partials]
            + [pl.BlockSpec(s.shape, lambda i, c: (0, 0, 0)) for s in gots],
            out_specs=[pl.BlockSpec(s.shape[1:], lambda i, c: (0, 0)) for s in partials] + [HBM_SPEC] * n,
            scratch_shapes=[pltpu.SemaphoreType.DMA((n,)), pltpu.SemaphoreType.DMA((n,))]),
        out_shape=halves + halves,
        compiler_params=_cp(("arbitrary",), VMEM_BIG),
    )(chip, *partials, *gots)
    return res[:n], res[n:]


def _adam_math(w, g, m, v):
    nm = ADAM_B1 * m + (1.0 - ADAM_B1) * g
    nv = ADAM_B2 * v + (1.0 - ADAM_B2) * (g * g)
    m_hat = nm / (1.0 - ADAM_B1 ** ADAM_STEP)
    v_hat = nv / (1.0 - ADAM_B2 ** ADAM_STEP)
    return -ADAM_LR * (m_hat / (jnp.sqrt(v_hat) + ADAM_EPS) + ADAM_WD * w), nm, nv


def _adamw(w, g, m, v, rows):
    R, C = w.shape

    def body(w_ref, g_ref, m_ref, v_ref, d_ref, nm_ref, nv_ref):
        d_ref[...], nm_ref[...], nv_ref[...] = _adam_math(w_ref[...], g_ref[...], m_ref[...], v_ref[...])

    spec = pl.BlockSpec((rows, C), lambda i: (i, 0))
    sh = jax.ShapeDtypeStruct((R, C), F32)
    return pl.pallas_call(
        body, name="adamw", grid=(R // rows,), in_specs=[spec] * 4, out_specs=[spec] * 3,
        out_shape=[sh, sh, sh], compiler_params=_cp(("arbitrary",)),
    )(w, g, m, v)


def _small_adamw(tot, ws, ms, vs):
    n = len(ws)

    def body(*refs):
        tot_ref = refs[0]
        w_refs, m_refs, v_refs = refs[1:1 + n], refs[1 + n:1 + 2 * n], refs[1 + 2 * n:1 + 3 * n]
        outs = refs[1 + 3 * n:]
        for i in range(n):
            rows, width = ws[i].shape
            g = tot_ref[pl.ds(SMALL_ROW[i], rows), pl.ds(0, width)]
            outs[4 * i][...] = g
            outs[4 * i + 1][...], outs[4 * i + 2][...], outs[4 * i + 3][...] = _adam_math(
                w_refs[i][...], g, m_refs[i][...], v_refs[i][...])
        outs[4 * n][...] = tot_ref[pl.ds(LOSS_ROW, 1), pl.ds(0, LANE)]

    vmem = pl.BlockSpec(memory_space=pltpu.VMEM)
    res = pl.pallas_call(
        body, name="small_adamw", in_specs=[vmem] * (1 + 3 * n), out_specs=[vmem] * (4 * n + 1),
        out_shape=[jax.ShapeDtypeStruct(w.shape, F32) for w in ws for _ in range(4)]
        + [jax.ShapeDtypeStruct((1, LANE), F32)],
    )(tot, *ws, *ms, *vs)
    return [res[4 * i:4 * i + 4] for i in range(n)], res[4 * n]


HBM_SPEC = pl.BlockSpec(memory_space=pltpu.HBM)


def _place():
    return lax.axis_index("x"), lax.axis_index("y"), lax.axis_index("c")


def _all_gather_split(shard):
    def body(in_ref, out_ref, send_sems, recv_sems):
        direct, relayed, passed = _gather_tree_copies([in_ref], [out_ref], send_sems, recv_sems)
        for cp in direct:
            cp.start()
        for i in range(2):
            direct[i].wait_recv()
            relayed[i].start()
            passed[i].start()
        for cp in relayed:
            cp.wait_recv()
        passed[2].start()
        for cp in passed:
            cp.wait_recv()
        for cp in direct + relayed + passed:
            cp.wait_send()

    return pl.pallas_call(
        body, name="all_gather_split", in_specs=[HBM_SPEC], out_specs=HBM_SPEC,
        out_shape=jax.ShapeDtypeStruct((NCHIP,) + shard.shape, shard.dtype),
        scratch_shapes=_hosted_sems(1, GATHER_SEMS),
    )(shard)


def _pair_exchange(grads):
    n = len(grads)

    def body(*refs):
        copies = _pair_copies(refs[:n], refs[n:2 * n], refs[2 * n], refs[2 * n + 1])
        for cp in copies:
            cp.start()
        for cp in copies:
            cp.wait()

    return pl.pallas_call(
        body, name="pair_exchange", in_specs=[HBM_SPEC] * n, out_specs=[HBM_SPEC] * n,
        out_shape=[jax.ShapeDtypeStruct((NCHIP, g.shape[1] // 2, g.shape[2]), F32) for g in grads],
        scratch_shapes=[pltpu.SemaphoreType.DMA((n,)), pltpu.SemaphoreType.DMA((n,))],
    )(*grads)


def _small_allreduce(rows, loss_blk):
    n = len(rows)

    def body(*refs):
        loss_ref, o_ref, pk, slots, send_sems, recv_sems = refs[n:]
        pk[...] = jnp.zeros_like(pk)
        for i in range(n):
            pk[pl.ds(i, 1), pl.ds(0, rows[i].shape[1])] = refs[i][...]
        pk[pl.ds(LOSS_ROW, 1), pl.ds(0, LANE)] = loss_ref[pl.ds(0, 1), :]
        x, y, c = _place()
        me = 4 * x + 2 * y + c
        slots[me] = pk[...]
        copies = []
        for r in range(1, 8):
            rx, ry, rc = (r >> 2) & 1, (r >> 1) & 1, r & 1
            peer = (x + rx - 2 * x * rx, y + ry - 2 * y * ry, c + rc - 2 * c * rc)
            cp = pltpu.make_async_remote_copy(
                src_ref=pk, dst_ref=slots.at[me], send_sem=send_sems.at[r - 1], recv_sem=recv_sems.at[r - 1],
                device_id=peer, device_id_type=MESH)
            cp.start()
            copies.append(cp)
        for cp in copies:
            cp.wait()
        acc = slots[0]
        for j in range(1, 8):
            acc = acc + slots[j]
        o_ref[...] = acc

    vmem = pl.BlockSpec(memory_space=pltpu.VMEM)
    return pl.pallas_call(
        body, name="small_allreduce", in_specs=[vmem] * (n + 1), out_specs=vmem,
        out_shape=jax.ShapeDtypeStruct((SMALL_PK, D), F32),
        scratch_shapes=[pltpu.VMEM((SMALL_PK, D), F32), pltpu.VMEM((8, SMALL_PK, D), F32),
                        pltpu.SemaphoreType.DMA((7,)), pltpu.SemaphoreType.DMA((7,))],
    )(*rows, loss_blk)


def _seg_matrix():
    i = lax.broadcasted_iota(jnp.int32, (AD, AD), 0) // DH
    j = lax.broadcasted_iota(jnp.int32, (AD, AD), 1) // DH
    return (i == j).astype(BF16)


TAIL = ("w_out", "w_ple_gate", "w_ple", "w_pw", "dw_w")


def _local_step(x, p, tgt, sm, shards, chip, ci):
    seg = _seg_matrix()
    core = jnp.reshape(ci, (1,)).astype(jnp.int32)
    chip_idx = jnp.reshape(chip, (1,)).astype(jnp.int32)
    own = lambda g, s: lax.dynamic_update_index_in_dim(g, s, chip, 0)
    w_in_next = own(_all_gather_split(shards[0]["w_in"]), shards[0]["w_in"])
    h = x
    saved = []
    for l in range(DEPTH):
        w_in = w_in_next
        row = lambda name: sm[name][l:l + 1]
        u, hn = _rms_inproj(h, row("norm_g"), w_in)
        todo = [shards[l][k] for k in TAIL] + ([shards[l + 1]["w_in"]] if l + 1 < DEPTH else [])
        o, ya, tot, got = _attn_fwd(u, jnp.tile(row("attn_out_g"), (1, AD // DH)), todo)
        got = [own(g, s) for g, s in zip(got, todo)]
        w_out = got[0].reshape(D, D)
        w_gate = got[1].reshape(D, D)
        w_ple = got[2]
        w_pw = got[3].reshape(CD, CD)
        dw = got[4]
        if l + 1 < DEPTH:
            w_in_next = got[5]
        c1 = _glu_conv(u, dw, row("dw_b"))
        c3, yc, h1, gate, pe, h2 = _layer_tail(
            c1, u, ya, h, p[l], row("conv_ln_g"), row("conv_ln_b"), w_pw, row("conv_out_g"), w_out,
            row("ple_norm_g"), w_gate, w_ple)
        saved.append(dict(h=h, u=u, hn=hn, o=o, ya=ya, tot=tot, c1=c1, c3=c3, yc=yc, h1=h1, gate=gate, pe=pe,
                          w_in=w_in, w_out=w_out, w_gate=w_gate, w_pw=w_pw, dw=dw))
        h = h2
    loss_blk, dh, dfg = _loss_head(h, tgt, sm["final_g"])
    small = [None] * DEPTH
    pending, partials, arrived = [], {}, {}
    pair_sum = lambda grads: _sum_pair(core, grads, _pair_exchange(grads))
    for l in reversed(range(DEPTH)):
        s = saved[l]
        row = lambda name: sm[name][l:l + 1]
        dh1, dy, dwg, dwp, dwo, dpg = _ple_out_bwd(
            dh, s["h1"], s["gate"], s["pe"], p[l], s["ya"], s["yc"], row("ple_norm_g"), s["w_gate"], s["w_out"])
        ag_t = jnp.tile(row("attn_out_g"), (1, AD // DH))
        do, dga, dgc, dc1, dwpw, dag, dcg, dlg, dlb = _branch_bwd(
            dy, s["o"], s["u"], s["c1"], s["c3"], ag_t, row("conv_ln_g"), row("conv_ln_b"), s["w_pw"],
            row("conv_out_g"), seg)
        dcv, dcgate, ddw, ddb = _conv_bwd(dc1, s["u"], s["dw"])
        tail = [dwo.reshape(NCHIP, 256, D), dwg.reshape(NCHIP, 256, D), dwp, dwpw.reshape(NCHIP, 128, CD), ddw]
        if l == 0:
            partials[(l, "tail")] = pair_sum(tail)
            pending.append((l, "tail"))
        send = [t for key in pending for t in partials[key]]
        dq, dk, dv, got = _attn_bwd(s["u"], do, s["tot"], send)
        for key in pending:
            arrived[key], got = got[:len(partials[key])], got[len(partials[key]):]
        pending = []
        du = jnp.concatenate([dq, dk, dv, dga, dcv, dcgate, dgc], axis=1)
        dwin = _inproj_dw(s["hn"], du)
        if l == 0:
            partials[(l, "w_in")] = pair_sum([dwin])
            dh, dng, arrived[(l, "w_in")] = _inproj_dx(du, s["w_in"], s["h"], row("norm_g"), dh1,
                                                      partials[(l, "w_in")])
        else:
            dh, dng, halves = _inproj_dx(du, s["w_in"], s["h"], row("norm_g"), dh1, grads=tail + [dwin])
            tail_p = _sum_pair(core, tail + [dwin], halves)
            partials[(l, "tail")], partials[(l, "w_in")] = tail_p[:-1], tail_p[-1:]
            pending = [(l, "tail"), (l, "w_in")]
        small[l] = dict(norm_g=dng, attn_out_g=dag.reshape(AD // DH, DH).sum(axis=0, keepdims=True), dw_b=ddb,
                        conv_ln_g=dlg, conv_ln_b=dlb, conv_out_g=dcg, ple_norm_g=dpg)
    big = []
    for l in range(DEPTH):
        ps = partials[(l, "w_in")] + partials[(l, "tail")]
        mine, theirs = _sum_chips_share(chip_idx, ps, arrived[(l, "w_in")] + arrived[(l, "tail")])
        big.append(dict(zip(BIG, [jnp.where(core[0] == 0, jnp.concatenate([m, t]), jnp.concatenate([t, m]))
                                  for m, t in zip(mine, theirs)])))
    return loss_blk, dh, big, small, dfg


BIG = ("w_in", "w_out", "w_ple_gate", "w_ple", "w_pw", "dw_w")
SMALL2 = ("norm_g", "ple_norm_g", "dw_b", "conv_ln_g", "conv_ln_b", "conv_out_g", "attn_out_g")
SMALL_ROW = (0, 2, 4, 6, 8, 10, 12, 14)


def kernel(x, p, norm_g, w_in, attn_out_g, dw_w, dw_b, conv_ln_g, conv_ln_b, w_pw, conv_out_g, w_out, ple_norm_g, w_ple_gate, w_ple, final_g, loss_target, m_norm_g, m_w_in, m_attn_out_g, m_dw_w, m_dw_b, m_conv_ln_g, m_conv_ln_b, m_w_pw, m_conv_out_g, m_w_out, m_ple_norm_g, m_w_ple_gate, m_w_ple, m_final_g, v_norm_g, v_w_in, v_attn_out_g, v_dw_w, v_dw_b, v_conv_ln_g, v_conv_ln_b, v_w_pw, v_conv_out_g, v_w_out, v_ple_norm_g, v_w_ple_gate, v_w_ple, v_final_g):
    W = dict(norm_g=norm_g, w_in=w_in, attn_out_g=attn_out_g, dw_w=dw_w, dw_b=dw_b, conv_ln_g=conv_ln_g,
             conv_ln_b=conv_ln_b, w_pw=w_pw, conv_out_g=conv_out_g, w_out=w_out, ple_norm_g=ple_norm_g,
             w_ple_gate=w_ple_gate, w_ple=w_ple, final_g=final_g)
    M = dict(norm_g=m_norm_g, w_in=m_w_in, attn_out_g=m_attn_out_g, dw_w=m_dw_w, dw_b=m_dw_b,
             conv_ln_g=m_conv_ln_g, conv_ln_b=m_conv_ln_b, w_pw=m_w_pw, conv_out_g=m_conv_out_g, w_out=m_w_out,
             ple_norm_g=m_ple_norm_g, w_ple_gate=m_w_ple_gate, w_ple=m_w_ple, final_g=m_final_g)
    V = dict(norm_g=v_norm_g, w_in=v_w_in, attn_out_g=v_attn_out_g, dw_w=v_dw_w, dw_b=v_dw_b,
             conv_ln_g=v_conv_ln_g, conv_ln_b=v_conv_ln_b, w_pw=v_w_pw, conv_out_g=v_conv_out_g, w_out=v_w_out,
             ple_norm_g=v_ple_norm_g, w_ple_gate=v_w_ple_gate, w_ple=v_w_ple, final_g=v_final_g)
    order = ("norm_g", "w_in", "attn_out_g", "dw_w", "dw_b", "conv_ln_g", "conv_ln_b", "w_pw", "conv_out_g",
             "w_out", "ple_norm_g", "w_ple_gate", "w_ple", "final_g")

    pad_taps = lambda a: jnp.pad(a, ((0, 0), (0, CWP - CW), (0, 0)))
    cast = dict(w_in=w_in.astype(BF16), w_out=w_out.astype(BF16), w_ple_gate=w_ple_gate.astype(BF16),
                w_ple=w_ple.astype(BF16), w_pw=w_pw.astype(BF16), dw_w=pad_taps(dw_w))
    shards = [{k: v[l] for k, v in cast.items()} for l in range(DEPTH)]
    xi, yi, ci = lax.axis_index("x"), lax.axis_index("y"), lax.axis_index("c")
    chip = 2 * xi + yi

    sm = {k: W[k] for k in SMALL2}
    sm["final_g"] = final_g.reshape(1, D)
    loss_part, grad_x, big, small, dfg = _local_step(x[0], p[:, 0], loss_target[0], sm, shards, chip, ci)
    g_big = {name: jnp.stack([big[l][name] for l in range(DEPTH)]).reshape(cast[name].shape) for name in BIG}

    rows = [small[l][k] for k in SMALL2 for l in range(DEPTH)] + [dfg]
    small_names = SMALL2 + ("final_g",)
    as_rows = lambda t: t.reshape(1, D) if t.ndim == 1 else t
    results, loss_row = _small_adamw(
        _small_allreduce(rows, loss_part), [as_rows(W[k]) for k in small_names],
        [as_rows(M[k]) for k in small_names], [as_rows(V[k]) for k in small_names])
    loss = loss_row[0, 0]

    grads, deltas, new_m, new_v = {}, {}, {}, {}
    for name in BIG:
        wv = pad_taps(W[name]) if name == "dw_w" else W[name]
        mv = pad_taps(M[name]) if name == "dw_w" else M[name]
        vv = pad_taps(V[name]) if name == "dw_w" else V[name]
        gg = g_big[name]
        cols = wv.shape[-1]
        rows_total = wv.size // cols
        tile_rows = min(rows_total, 256)
        d2, m2, v2 = _adamw(wv.reshape(rows_total, cols), gg.reshape(rows_total, cols),
                            mv.reshape(rows_total, cols), vv.reshape(rows_total, cols), tile_rows)
        if name == "dw_w":
            cut = lambda a: a.reshape(DEPTH, CWP, LANE)[:, :CW]
            grads[name], deltas[name], new_m[name], new_v[name] = cut(gg), cut(d2), cut(m2), cut(v2)
        else:
            grads[name] = gg
            deltas[name], new_m[name], new_v[name] = (t.reshape(wv.shape) for t in (d2, m2, v2))
    for k, four in zip(small_names, results):
        grads[k], deltas[k], new_m[k], new_v[k] = (t.reshape(W[k].shape) for t in four)

    return (loss, grad_x[None], *[grads[n] for n in order], *[deltas[n] for n in order],
            *[new_m[n] for n in order], *[new_v[n] for n in order])
```

```python
import functools

import jax
import jax.numpy as jnp
from jax import lax
from jax.experimental import pallas as pl
from jax.experimental.pallas import tpu as pltpu

F32 = jnp.float32
BF16 = jnp.bfloat16

T = 2048
D = 1024
DIN = 3584
NCHIP = 4
SHW = DIN // NCHIP
AD = 512
CD = 512
DH = 64
CW = 31
CWP = 32
PLE = 256
DEPTH = 2
EPS = 1e-6
AQ = 256
HG = 4
GW = HG * DH
SR = HG * AQ
NG = AD // GW
LANE = 128
TM = 1024

ADAM_LR = 0.001
ADAM_B1 = 0.9
ADAM_B2 = 0.999
ADAM_EPS = 1e-08
ADAM_WD = 0.01
ADAM_STEP = 10

SMALL_PK = 16
LOSS_ROW = 15

VMEM_BIG = 56 * 1024 * 1024
MESH = pl.DeviceIdType.MESH


def _cp(sem=None, vmem=None):
    kw = {}
    if sem is not None:
        kw["dimension_semantics"] = sem
    if vmem is not None:
        kw["vmem_limit_bytes"] = vmem
    return pltpu.CompilerParams(**kw)


def _dot(a, b):
    return jnp.dot(a, b, preferred_element_type=F32)


def _dot_nt(a, b):
    return lax.dot_general(a, b, (((1,), (1,)), ((), ())), preferred_element_type=F32)


def _dot_tn(a, b):
    return lax.dot_general(a, b, (((0,), (0,)), ((), ())), preferred_element_type=F32)


def _dot2(x, m):
    hi = x.astype(BF16)
    lo = (x - hi.astype(F32)).astype(BF16)
    return _dot(hi, m) + _dot(lo, m)


def _sig(x):
    return 1.0 / (1.0 + jnp.exp(-x))


def _softplus(z):
    return jnp.maximum(z, 0.0) + jnp.log(1.0 + jnp.exp(-jnp.abs(z)))


def _rstd(x):
    return lax.rsqrt(jnp.mean(x * x, axis=-1, keepdims=True) + EPS)


def _rms_bwd(dy, x, r, g):
    dn = dy * g
    return r * dn - x * (r * r * r) * jnp.mean(dn * x, axis=-1, keepdims=True)


def _rms_inproj(h, g, w):
    tm = min(TM, T)

    def body(h_ref, g_ref, w_ref, u_ref, hn_ref, hn_s):
        @pl.when(pl.program_id(1) == 0)
        def _():
            hh = h_ref[...]
            hn = (hh * _rstd(hh) * g_ref[...]).astype(BF16)
            hn_s[...] = hn
            hn_ref[...] = hn
        u_ref[...] = _dot(hn_s[...], w_ref[0])

    return pl.pallas_call(
        body, name="rms_inproj", grid=(T // tm, NCHIP),
        in_specs=[pl.BlockSpec((tm, D), lambda i, k: (i, 0)),
                  pl.BlockSpec((1, D), lambda i, k: (0, 0)),
                  pl.BlockSpec((1, D, SHW), lambda i, k: (k, 0, 0))],
        out_specs=[pl.BlockSpec((tm, SHW), lambda i, k: (i, k)),
                   pl.BlockSpec((tm, D), lambda i, k: (i, 0))],
        out_shape=[jax.ShapeDtypeStruct((T, DIN), F32), jax.ShapeDtypeStruct((T, D), BF16)],
        scratch_shapes=[pltpu.VMEM((tm, D), BF16)],
        compiler_params=_cp(("arbitrary", "arbitrary"), VMEM_BIG),
    )(h, g, w)


def _attn_tiles():
    row = lax.broadcasted_iota(jnp.int32, (SR, AQ), 0) & (AQ - 1)
    col = lax.broadcasted_iota(jnp.int32, (SR, AQ), 1)
    tr = lax.broadcasted_iota(jnp.int32, (AQ, AQ), 0)
    tc = lax.broadcasted_iota(jnp.int32, (AQ, AQ), 1)
    lane_head = lax.broadcasted_iota(jnp.int32, (1, GW), 1) // DH
    return col < row, tr, tc, [lane_head == h for h in range(HG)]


def _stack_heads(t, heads):
    return jnp.concatenate([jnp.where(m, t, 0.0) for m in heads], axis=0)


def _unstack_heads(t, heads):
    out = t[:AQ]
    for h in range(1, HG):
        out = jnp.where(heads[h], t[h * AQ:(h + 1) * AQ], out)
    return out


def _tri_sum(x, tri):
    hi = x.astype(BF16)
    lo = (x - hi.astype(F32)).astype(BF16)
    both = _dot(jnp.concatenate([hi, lo], axis=0), tri)
    return both[:SR] + both[SR:]


def _scatter_copies(ps, gots, send_sems, recv_sems):
    x, y, c = _place()
    peers = [(1 - x, y), (x, 1 - y), (1 - x, 1 - y)]
    return [pltpu.make_async_remote_copy(
        src_ref=ps[a].at[2 * px + py], dst_ref=gots[a].at[r], send_sem=send_sems.at[3 * a + r],
        recv_sem=recv_sems.at[3 * a + r], device_id=(px, py, c), device_id_type=MESH)
        for a in range(len(ps)) for r, (px, py) in enumerate(peers)]


GATHER_SEMS = 7


def _gather_tree_copies(ins, outs, send_sems, recv_sems):
    x, y, c = _place()
    me, xn, yn, dg = 2 * x + y, 2 * (1 - x) + y, 2 * x + (1 - y), 2 * (1 - x) + (1 - y)
    to_x, to_y, sibling = (1 - x, y, c), (x, 1 - y, c), (x, y, 1 - c)
    direct, relayed, passed = [], [], []
    for a in range(len(ins)):
        half = ins[a].shape[0] // 2
        mine = pl.ds(c * half, half)
        first, second = pl.ds(c * half, half // 2), pl.ds(c * half + half // 2, half // 2)

        def copy(i, src, dst, to, k=GATHER_SEMS * a):
            return pltpu.make_async_remote_copy(src_ref=src, dst_ref=dst, send_sem=send_sems.at[k + i],
                                                recv_sem=recv_sems.at[k + i], device_id=to, device_id_type=MESH)

        own, slot = ins[a].at[mine], outs[a].at[me, mine]
        direct += [copy(0, own, slot, to_x), copy(1, own, slot, to_y)]
        relayed += [copy(2, outs[a].at[xn, first], outs[a].at[xn, first], to_y),
                    copy(3, outs[a].at[yn, second], outs[a].at[yn, second], to_x)]
        passed += [copy(4 + i, outs[a].at[j, mine], outs[a].at[j, mine], sibling) for i, j in enumerate((xn, yn, dg))]
    return direct, relayed, passed


def _pair_copies(ins, outs, send_sems, recv_sems):
    x, y, c = _place()
    copies = []
    for a in range(len(ins)):
        half = ins[a].shape[1] // 2
        copies.append(pltpu.make_async_remote_copy(
            src_ref=ins[a].at[:, pl.ds((1 - c) * half, half), :], dst_ref=outs[a], send_sem=send_sems.at[a],
            recv_sem=recv_sems.at[a], device_id=(x, y, 1 - c), device_id_type=MESH))
    return copies


def _host(body, grid, n_in, n_out, n_x, make_copies, mids=()):
    if not n_x:
        return body

    def hosting(*refs):
        a, b = n_in + n_x, n_in + 2 * n_x + n_out
        copies = make_copies(refs[n_in:a], refs[a + n_out:b], refs[-2], refs[-1])
        stages = copies if isinstance(copies, tuple) else (copies,)
        ids = [pl.program_id(d) for d in range(len(grid))]
        at = lambda step: functools.reduce(jnp.logical_and, [i == s for i, s in zip(ids, step)])

        @pl.when(at([0] * len(grid)))
        def _():
            for cp in stages[0]:
                cp.start()

        for before, after, step in zip(stages, stages[1:], mids):
            @pl.when(at(step))
            def _(before=before, after=after):
                for cp in before:
                    cp.wait_recv()
                for cp in after:
                    cp.start()

        body(*refs[:n_in], *refs[a:a + n_out], *refs[b:-2])

        @pl.when(at([g - 1 for g in grid]))
        def _():
            for cp in stages[-1]:
                cp.wait_recv()
            for stage in stages:
                for cp in stage:
                    cp.wait_send()

    return hosting


def _hosted_sems(n_x, per_array=3):
    n = per_array * n_x
    return [pltpu.SemaphoreType.DMA((n,)), pltpu.SemaphoreType.DMA((n,))] if n_x else []


RC = 256


def _chunk_causal(r):
    row = lax.broadcasted_iota(jnp.int32, (RC, AQ), 0) + (r * RC) % AQ
    return lax.broadcasted_iota(jnp.int32, (RC, AQ), 1) < row


def _attn_fwd(u, agw, shards=()):
    n = len(shards)
    grid = (T // AQ,)

    def body(q_ref, k_ref, v_ref, g_ref, ag_ref, o_ref, y_ref, tot_ref,
             kb_s, vb_s, qs_s, z_s, zs_s, lmb_s, suf_s, att_s, acc_s, run_s):
        qi = pl.program_id(0)

        @pl.when(qi == 0)
        def _():
            kb_s[...] = k_ref[...].astype(BF16)
            vb_s[...] = v_ref[...].astype(BF16)

        _, tr, tc, heads = _attn_tiles()
        upper = (tr > tc).astype(BF16)
        for g in range(NG):
            qs_s[g] = _stack_heads(q_ref[:, g * GW:(g + 1) * GW] * 0.125, heads).astype(BF16)
        acc_s[...] = jnp.zeros_like(acc_s)
        run_s[...] = jnp.zeros_like(run_s)

        def block(kb, masked):
            k0 = pl.multiple_of(kb * AQ, AQ)
            for g in range(NG):
                lanes = pl.ds(g * GW, GW)
                z_s[g] = _dot_nt(qs_s[g], kb_s[pl.ds(k0, AQ), lanes])
                for r in range(SR // RC):
                    rows = pl.ds(r * RC, RC)
                    z = z_s[g, rows, :]
                    zs = jnp.minimum(z, 0.0) - jnp.log(1.0 + jnp.exp(-jnp.abs(z)))
                    lm = zs - z
                    if masked:
                        lm = jnp.where(_chunk_causal(r), lm, 0.0)
                    run = run_s[g, rows, :]
                    zs_s[g, rows, :] = zs + run[:, 0:1]
                    hi = lm.astype(BF16)
                    lmb_s[g, rows, :] = hi
                    lmb_s[g, pl.ds(SR + r * RC, RC), :] = (lm - hi.astype(F32)).astype(BF16)
                    run_s[g, rows, :] = run + jnp.sum(lm, axis=1, keepdims=True)
                suf_s[g] = _dot(lmb_s[g], upper)
                for r in range(SR // RC):
                    rows = pl.ds(r * RC, RC)
                    att = jnp.exp(zs_s[g, rows, :] + suf_s[g, rows, :] + suf_s[g, pl.ds(SR + r * RC, RC), :])
                    if masked:
                        att = jnp.where(_chunk_causal(r), att, 0.0)
                    att_s[g, rows, :] = att.astype(BF16)
                acc_s[g] += _dot(att_s[g], vb_s[pl.ds(k0, AQ), lanes])

        block(qi, True)

        def step(i, c):
            block(qi - 1 - i, False)
            return c

        lax.fori_loop(0, qi, step, 0)
        gate = g_ref[...]
        agv = ag_ref[...]
        for g in range(NG):
            lanes = slice(g * GW, (g + 1) * GW)
            o = _unstack_heads(acc_s[g], heads)
            osq = o * o
            ms = jnp.zeros((AQ, GW), F32)
            for m in heads:
                ms = jnp.where(m, jnp.sum(jnp.where(m, osq, 0.0), axis=1, keepdims=True), ms)
            gg = gate[:, lanes]
            o_ref[:, lanes] = o
            y_ref[:, lanes] = (o * lax.rsqrt(ms * (1.0 / DH) + EPS) * agv[:, lanes] * (gg * _sig(gg))).astype(BF16)
            tot_ref[g] = _unstack_heads(jnp.broadcast_to(run_s[g][:, 0:1], (SR, GW)), heads)

    tile = lambda dt, rows=SR: pltpu.VMEM((NG, rows, AQ), dt)
    scratch = [pltpu.VMEM((T, AD), BF16), pltpu.VMEM((T, AD), BF16), pltpu.VMEM((NG, SR, GW), BF16),
               tile(F32), tile(F32), tile(BF16, 2 * SR), tile(F32, 2 * SR), tile(BF16),
               pltpu.VMEM((NG, SR, GW), F32), pltpu.VMEM((NG, SR, LANE), F32)]
    col = lambda j: pl.BlockSpec((AQ, AD), lambda qi: (qi, j))
    res = pl.pallas_call(
        _host(body, grid, 5, 3, n, _gather_tree_copies, mids=((grid[0] * 5 // 8,), (grid[0] * 7 // 8,))), name="attn_fwd", grid=grid,
        in_specs=[col(0), pl.BlockSpec((T, AD), lambda qi: (0, 1)), pl.BlockSpec((T, AD), lambda qi: (0, 2)),
                  col(3), pl.BlockSpec((1, AD), lambda qi: (0, 0))] + [HBM_SPEC] * n,
        out_specs=[col(0), col(0), pl.BlockSpec((NG, AQ, GW), lambda qi: (0, qi, 0))] + [HBM_SPEC] * n,
        out_shape=[jax.ShapeDtypeStruct((T, AD), F32), jax.ShapeDtypeStruct((T, AD), BF16),
                   jax.ShapeDtypeStruct((NG, T, GW), F32)]
        + [jax.ShapeDtypeStruct((NCHIP,) + s.shape, s.dtype) for s in shards],
        scratch_shapes=scratch + _hosted_sems(n, GATHER_SEMS),
        compiler_params=_cp(("arbitrary",), VMEM_BIG),
    )(u, u, u, u, agw, *shards)
    return res[0], res[1], res[2], list(res[3:])


def _glu_conv(u, dw, db):
    tr = 256

    def body(cv_ref, cg_ref, w_ref, b_ref, c1_ref, pad_s):
        pad_s[pl.ds(0, CWP), :] = jnp.zeros((CWP, LANE), F32)
        pad_s[pl.ds(CWP, T), :] = cv_ref[...] * _sig(cg_ref[...])
        wv = w_ref[0]
        bias = b_ref[...]

        def tile(i, carry):
            r0 = pl.multiple_of(i * tr, tr)
            acc = jnp.zeros((tr, LANE), F32) + bias
            for w in range(CW):
                acc = acc + pad_s[pl.ds(r0 + (CWP - CW + 1) + w, tr), :] * wv[w:w + 1, :]
            c1_ref[pl.ds(r0, tr), :] = acc
            return carry

        lax.fori_loop(0, T // tr, tile, 0)

    return pl.pallas_call(
        body, name="glu_conv", grid=(CD // LANE,),
        in_specs=[pl.BlockSpec((T, LANE), lambda cb: (0, 16 + cb)),
                  pl.BlockSpec((T, LANE), lambda cb: (0, 20 + cb)),
                  pl.BlockSpec((1, CWP, LANE), lambda cb: (cb, 0, 0)),
                  pl.BlockSpec((1, LANE), lambda cb: (0, cb))],
        out_specs=pl.BlockSpec((T, LANE), lambda cb: (0, cb)),
        out_shape=jax.ShapeDtypeStruct((T, CD), F32),
        scratch_shapes=[pltpu.VMEM((T + CWP, LANE), F32)],
        compiler_params=_cp(("arbitrary",)),
    )(u, u, dw, db)


def _ln_silu(c1, lg, lb):
    mu = jnp.mean(c1, axis=-1, keepdims=True)
    xc = c1 - mu
    rs = lax.rsqrt(jnp.mean(xc * xc, axis=-1, keepdims=True) + EPS)
    xh = xc * rs
    ln = xh * lg + lb
    s = _sig(ln)
    return xh, rs, ln, s


def _layer_tail(c1, u, ya, h, p, lg, lb, wpw, cg, wout, pg, wgate, wple):
    tm = 256

    def body(c1_ref, gc_ref, ya_ref, h_ref, p_ref, lg_ref, lb_ref, wpw_ref, cg_ref, wout_ref,
             pg_ref, wgate_ref, wple_ref, c3_ref, yc_ref, h1_ref, gate_ref, pe_ref, h2_ref):
        _, _, ln, s = _ln_silu(c1_ref[...], lg_ref[...], lb_ref[...])
        c2 = (ln * s).astype(BF16)
        c3 = _dot(c2, wpw_ref[...])
        gc = gc_ref[...]
        yc = (c3 * _rstd(c3) * cg_ref[...] * (gc * _sig(gc))).astype(BF16)
        c3_ref[...] = c3
        yc_ref[...] = yc
        y = _dot(ya_ref[...], wout_ref[pl.ds(0, AD), :]) + _dot(yc, wout_ref[pl.ds(AD, CD), :])
        h1 = h_ref[...] + y
        hn2 = (h1 * _rstd(h1) * pg_ref[...]).astype(BF16)
        gate = _sig(_dot(hn2, wgate_ref[...]))
        pb = p_ref[...].astype(BF16)
        pe = jnp.concatenate([_dot(pb, wple_ref[k]) for k in range(NCHIP)], axis=1)
        h1_ref[...] = h1
        gate_ref[...] = gate.astype(BF16)
        pe_ref[...] = pe.astype(BF16)
        h2_ref[...] = h1 + pe * gate

    row = lambda w: pl.BlockSpec((tm, w), lambda i: (i, 0))
    full = lambda *s: pl.BlockSpec(s, lambda i: (0,) * len(s))
    return pl.pallas_call(
        body, name="layer_tail", grid=(T // tm,),
        in_specs=[row(CD), pl.BlockSpec((tm, CD), lambda i: (i, 6)), row(AD), row(D), row(PLE),
                  full(1, CD), full(1, CD), full(CD, CD), full(1, CD), full(D, D),
                  full(1, D), full(D, D), full(NCHIP, PLE, PLE)],
        out_specs=[row(CD), row(CD), row(D), row(D), row(D), row(D)],
        out_shape=[jax.ShapeDtypeStruct((T, CD), F32), jax.ShapeDtypeStruct((T, CD), BF16),
                   jax.ShapeDtypeStruct((T, D), F32), jax.ShapeDtypeStruct((T, D), BF16),
                   jax.ShapeDtypeStruct((T, D), BF16), jax.ShapeDtypeStruct((T, D), F32)],
        compiler_params=_cp(("arbitrary",), VMEM_BIG),
    )(c1, u, ya, h, p, lg, lb, wpw, cg, wout, pg, wgate, wple)


def _loss_head(h, tgt, fg):
    tm = 256

    def body(h_ref, t_ref, g_ref, loss_ref, dh_ref, dg_ref):
        @pl.when(pl.program_id(0) == 0)
        def _():
            loss_ref[...] = jnp.zeros_like(loss_ref)
            dg_ref[...] = jnp.zeros_like(dg_ref)
        hh = h_ref[...]
        g = g_ref[...]
        r = _rstd(hh)
        e = hh * r * g - t_ref[...]
        loss_ref[...] += 0.5 * jnp.sum(jnp.mean(e * e, axis=-1, keepdims=True))
        dy = e * (1.0 / D)
        dg_ref[...] += jnp.sum(dy * hh * r, axis=0, keepdims=True)
        dh_ref[...] = _rms_bwd(dy, hh, r, g)

    return pl.pallas_call(
        body, name="loss_head", grid=(T // tm,),
        in_specs=[pl.BlockSpec((tm, D), lambda i: (i, 0)), pl.BlockSpec((tm, D), lambda i: (i, 0)),
                  pl.BlockSpec((1, D), lambda i: (0, 0))],
        out_specs=[pl.BlockSpec((8, LANE), lambda i: (0, 0)), pl.BlockSpec((tm, D), lambda i: (i, 0)),
                   pl.BlockSpec((1, D), lambda i: (0, 0))],
        out_shape=[jax.ShapeDtypeStruct((8, LANE), F32), jax.ShapeDtypeStruct((T, D), F32),
                   jax.ShapeDtypeStruct((1, D), F32)],
        compiler_params=_cp(("arbitrary",)),
    )(h, tgt, fg)


def _ple_out_bwd(dh2, h1, gate, pe, p, ya, yc, pg, wgate, wout):
    tm = 256

    def body(dh2_ref, h1_ref, gate_ref, pe_ref, p_ref, ya_ref, yc_ref, pg_ref, wgate_ref, wout_ref,
             dh1_ref, dy_ref, dwg_ref, dwp_ref, dwo_ref, dpg_ref):
        @pl.when(pl.program_id(0) == 0)
        def _():
            dwg_ref[...] = jnp.zeros_like(dwg_ref)
            dwp_ref[...] = jnp.zeros_like(dwp_ref)
            dwo_ref[...] = jnp.zeros_like(dwo_ref)
            dpg_ref[...] = jnp.zeros_like(dpg_ref)
        dh2 = dh2_ref[...]
        h1 = h1_ref[...]
        gate = gate_ref[...].astype(F32)
        pg = pg_ref[...]
        dpe = (dh2 * gate).astype(BF16)
        dgp = (dh2 * pe_ref[...].astype(F32) * gate * (1.0 - gate)).astype(BF16)
        r = _rstd(h1)
        hn = h1 * r
        dwg_ref[...] += _dot_tn((hn * pg).astype(BF16), dgp)
        dhn2 = _dot_nt(dgp, wgate_ref[...])
        dpg_ref[...] += jnp.sum(dhn2 * hn, axis=0, keepdims=True)
        dh1 = dh2 + _rms_bwd(dhn2, h1, r, pg)
        pb = p_ref[...].astype(BF16)
        for k in range(NCHIP):
            dwp_ref[k] += _dot_tn(pb, dpe[:, k * PLE:(k + 1) * PLE])
        dh1b = dh1.astype(BF16)
        dy_ref[...] = _dot_nt(dh1b, wout_ref[...])
        dwo_ref[pl.ds(0, AD), :] += _dot_tn(ya_ref[...], dh1b)
        dwo_ref[pl.ds(AD, CD), :] += _dot_tn(yc_ref[...], dh1b)
        dh1_ref[...] = dh1

    row = lambda w: pl.BlockSpec((tm, w), lambda i: (i, 0))
    full = lambda *s: pl.BlockSpec(s, lambda i: (0,) * len(s))
    return pl.pallas_call(
        body, name="ple_out_bwd", grid=(T // tm,),
        in_specs=[row(D), row(D), row(D), row(D), row(PLE), row(AD), row(CD),
                  full(1, D), full(D, D), full(D, D)],
        out_specs=[row(D), row(D), full(D, D), full(NCHIP, PLE, PLE), full(D, D), full(1, D)],
        out_shape=[jax.ShapeDtypeStruct((T, D), F32), jax.ShapeDtypeStruct((T, D), F32),
                   jax.ShapeDtypeStruct((D, D), F32), jax.ShapeDtypeStruct((NCHIP, PLE, PLE), F32),
                   jax.ShapeDtypeStruct((D, D), F32), jax.ShapeDtypeStruct((1, D), F32)],
        compiler_params=_cp(("arbitrary",), VMEM_BIG),
    )(dh2, h1, gate, pe, p, ya, yc, pg, wgate, wout)


def _branch_bwd(dy, o, u, c1, c3, ag, lg, lb, wpw, cg, seg):
    tm = 256

    def body(dya_ref, dyc_ref, o_ref, ga_ref, gc_ref, c1_ref, c3_ref, ag_ref, lg_ref, lb_ref, wpw_ref,
             cg_ref, seg_ref, do_ref, dga_ref, dgc_ref, dc1_ref, dwpw_ref, dag_ref, dcg_ref, dlg_ref, dlb_ref):
        @pl.when(pl.program_id(0) == 0)
        def _():
            for r_ in (dwpw_ref, dag_ref, dcg_ref, dlg_ref, dlb_ref):
                r_[...] = jnp.zeros_like(r_)
        dya = dya_ref[...]
        o = o_ref[...]
        ga = ga_ref[...]
        ag_v = ag_ref[...]
        seg_m = seg_ref[...]
        r = lax.rsqrt(_dot2(o * o, seg_m) * (1.0 / DH) + EPS)
        onr = o * r
        sg = _sig(ga)
        dga_ref[...] = (dya * (onr * ag_v) * (sg * (1.0 + ga * (1.0 - sg)))).astype(BF16)
        don = dya * (ga * sg)
        dag_ref[...] += jnp.sum(don * onr, axis=0, keepdims=True)
        dn = don * ag_v
        do_ref[...] = r * dn - o * (r * r * r) * (_dot2(dn * o, seg_m) * (1.0 / DH))
        dyc = dyc_ref[...]
        c3 = c3_ref[...]
        gc = gc_ref[...]
        cg_v = cg_ref[...]
        r3 = _rstd(c3)
        cn = c3 * r3
        sc = _sig(gc)
        dgc_ref[...] = (dyc * (cn * cg_v) * (sc * (1.0 + gc * (1.0 - sc)))).astype(BF16)
        dcn = dyc * (gc * sc)
        dcg_ref[...] += jnp.sum(dcn * cn, axis=0, keepdims=True)
        dc3 = _rms_bwd(dcn, c3, r3, cg_v).astype(BF16)
        lg_v = lg_ref[...]
        xh, rs, ln, s = _ln_silu(c1_ref[...], lg_v, lb_ref[...])
        c2 = (ln * s).astype(BF16)
        dwpw_ref[...] += _dot_tn(c2, dc3)
        dc2 = _dot_nt(dc3, wpw_ref[...])
        dln = dc2 * (s * (1.0 + ln * (1.0 - s)))
        dlb_ref[...] += jnp.sum(dln, axis=0, keepdims=True)
        dlg_ref[...] += jnp.sum(dln * xh, axis=0, keepdims=True)
        dxh = dln * lg_v
        dc1_ref[...] = rs * (dxh - jnp.mean(dxh, axis=-1, keepdims=True)
                             - xh * jnp.mean(dxh * xh, axis=-1, keepdims=True))

    half = lambda j: pl.BlockSpec((tm, 512), lambda i: (i, j))
    full = lambda *s: pl.BlockSpec(s, lambda i: (0,) * len(s))
    vec = jax.ShapeDtypeStruct((1, 512), F32)
    act = jax.ShapeDtypeStruct((T, 512), F32)
    return pl.pallas_call(
        body, name="branch_bwd", grid=(T // tm,),
        in_specs=[half(0), half(1), half(0), half(3), half(6), half(0), half(0),
                  full(1, AD), full(1, CD), full(1, CD), full(CD, CD), full(1, CD), full(AD, AD)],
        out_specs=[half(0), half(0), half(0), half(0), full(CD, CD), full(1, 512), full(1, 512),
                   full(1, 512), full(1, 512)],
        out_shape=[act, jax.ShapeDtypeStruct((T, 512), BF16), jax.ShapeDtypeStruct((T, 512), BF16), act,
                   jax.ShapeDtypeStruct((CD, CD), F32), vec, vec, vec, vec],
        compiler_params=_cp(("arbitrary",), VMEM_BIG),
    )(dy, dy, o, u, u, c1, c3, ag, lg, lb, wpw, cg, seg)


def _conv_bwd(dc1, u, dw):
    tr = 64
    off = CWP - CW + 1

    def body(d_ref, cv_ref, cg_ref, w_ref, dcv_ref, dcg_ref, ddw_ref, ddb_ref, padc_s, padd_s, acc_s):
        cv = cv_ref[...]
        sg = _sig(cg_ref[...])
        padc_s[pl.ds(0, CWP), :] = jnp.zeros((CWP, LANE), F32)
        padc_s[pl.ds(CWP, T), :] = cv * sg
        padd_s[pl.ds(0, T), :] = d_ref[...]
        padd_s[pl.ds(T, CWP), :] = jnp.zeros((CWP, LANE), F32)
        acc_s[...] = jnp.zeros_like(acc_s)
        wv = w_ref[0]

        def tile(i, carry):
            r0 = pl.multiple_of(i * tr, tr)
            dt = padd_s[pl.ds(r0, tr), :]
            dc0 = jnp.zeros((tr, LANE), F32)
            for w in range(CW):
                dc0 = dc0 + padd_s[pl.ds(r0 + (CW - 1) - w, tr), :] * wv[w:w + 1, :]
                prod = dt * padc_s[pl.ds(r0 + off + w, tr), :]
                acc_s[w] += jnp.sum(prod.reshape(tr // 8, 8, LANE), axis=0)
            cvt = cv_ref[pl.ds(r0, tr), :]
            sgt = _sig(cg_ref[pl.ds(r0, tr), :])
            dcv_ref[pl.ds(r0, tr), :] = (dc0 * sgt).astype(BF16)
            dcg_ref[pl.ds(r0, tr), :] = (dc0 * cvt * sgt * (1.0 - sgt)).astype(BF16)
            return carry

        lax.fori_loop(0, T // tr, tile, 0)
        ddw_ref[0] = jnp.sum(acc_s[...], axis=1)
        ddb_ref[...] = jnp.sum(d_ref[...], axis=0, keepdims=True)

    col = lambda j: pl.BlockSpec((T, LANE), lambda cb: (0, j + cb))
    return pl.pallas_call(
        body, name="conv_bwd", grid=(CD // LANE,),
        in_specs=[col(0), col(16), col(20), pl.BlockSpec((1, CWP, LANE), lambda cb: (cb, 0, 0))],
        out_specs=[col(0), col(0), pl.BlockSpec((1, CWP, LANE), lambda cb: (cb, 0, 0)),
                   pl.BlockSpec((1, LANE), lambda cb: (0, cb))],
        out_shape=[jax.ShapeDtypeStruct((T, CD), BF16), jax.ShapeDtypeStruct((T, CD), BF16),
                   jax.ShapeDtypeStruct((NCHIP, CWP, LANE), F32), jax.ShapeDtypeStruct((1, CD), F32)],
        scratch_shapes=[pltpu.VMEM((T + CWP, LANE), F32), pltpu.VMEM((T + CWP, LANE), F32),
                        pltpu.VMEM((CWP, 8, LANE), F32)],
        compiler_params=_cp(("arbitrary",)),
    )(dc1, u, u, dw)


def _attn_bwd(u, do, tot, partials=()):
    n_x = len(partials)
    grid = (T // AQ,)

    def body(q_ref, k_ref, v_ref, do_ref, tot_ref, dq_ref, dk_ref, dv_ref, kb_s, vb_s, dk_s, dv_s):
        qi = pl.program_id(0)

        @pl.when(qi == 0)
        def _():
            kb_s[...] = k_ref[...].astype(BF16)
            vb_s[...] = v_ref[...].astype(BF16)
            dk_s[...] = jnp.zeros_like(dk_s)
            dv_s[...] = jnp.zeros_like(dv_s)

        causal, tr, tc, heads = _attn_tiles()
        upper = (tr > tc).astype(BF16)
        lower = (tr < tc).astype(BF16)
        qs, qus, dos, tots = [], [], [], []
        for g in range(NG):
            lanes = slice(g * GW, (g + 1) * GW)
            q = q_ref[:, lanes]
            qs.append(_stack_heads(q * 0.125, heads).astype(BF16))
            qus.append(_stack_heads(q, heads).astype(BF16))
            dos.append(_stack_heads(do_ref[:, lanes], heads).astype(BF16))
            totv = tot_ref[g]
            tots.append(jnp.concatenate([totv[:, h * DH:h * DH + 1] for h in range(HG)], axis=0))

        def block(kb, carry, masked):
            k0 = pl.multiple_of(kb * AQ, AQ)
            out = []
            for g in range(NG):
                lanes = pl.ds(g * GW, GW)
                lm_left, dl_left, dq = carry[g]
                kk = kb_s[pl.ds(k0, AQ), lanes]
                vv = vb_s[pl.ds(k0, AQ), lanes]
                z = _dot_nt(qs[g], kk)
                sp = _softplus(z)
                lm = jnp.where(causal, -sp, 0.0) if masked else -sp
                lm_incl = lm_left + jnp.sum(lm, axis=1, keepdims=True)
                att = jnp.exp((z - sp) + _tri_sum(lm, upper) + (tots[g] - lm_incl))
                if masked:
                    att = jnp.where(causal, att, 0.0)
                dl = att * _dot_nt(dos[g], vv)
                dv_s[pl.ds(k0, AQ), lanes] += _dot_tn(att.astype(BF16), dos[g])
                prefix = dl_left + _tri_sum(dl, lower)
                beta = jnp.exp(z - sp)
                dz = (1.0 - beta) * dl - beta * prefix
                if masked:
                    dz = jnp.where(causal, dz, 0.0)
                dzs = (dz * 0.125).astype(BF16)
                dk_s[pl.ds(k0, AQ), lanes] += _dot_tn(dzs, qus[g])
                out.append((lm_incl, dl_left + jnp.sum(dl, axis=1, keepdims=True), dq + _dot(dzs, kk)))
            return tuple(out)

        zero = jnp.zeros((SR, 1), F32)
        init = tuple((zero, zero, jnp.zeros((SR, GW), F32)) for _ in range(NG))
        carry = lax.fori_loop(0, qi, lambda kb, c: block(kb, c, False), init)
        carry = block(qi, carry, True)
        for g in range(NG):
            dq_ref[:, g * GW:(g + 1) * GW] = _unstack_heads(carry[g][2], heads).astype(BF16)

        @pl.when(qi == grid[0] - 1)
        def _():
            dk_ref[...] = dk_s[...].astype(BF16)
            dv_ref[...] = dv_s[...].astype(BF16)

    col = lambda j: pl.BlockSpec((AQ, AD), lambda qi: (qi, j))
    whole = lambda j: pl.BlockSpec((T, AD), lambda qi: (0, j), pipeline_mode=pl.Buffered(1))
    res = pl.pallas_call(
        _host(body, grid, 5, 3, n_x, _scatter_copies), name="attn_bwd", grid=grid,
        in_specs=[col(0), whole(1), whole(2), col(0), pl.BlockSpec((NG, AQ, GW), lambda qi: (0, qi, 0))]
        + [HBM_SPEC] * n_x,
        out_specs=[col(0), whole(0), whole(0)] + [HBM_SPEC] * n_x,
        out_shape=[jax.ShapeDtypeStruct((T, AD), BF16)] * 3
        + [jax.ShapeDtypeStruct((NCHIP - 1,) + a.shape[1:], a.dtype) for a in partials],
        scratch_shapes=[pltpu.VMEM((T, AD), BF16), pltpu.VMEM((T, AD), BF16), pltpu.VMEM((T, AD), F32),
                        pltpu.VMEM((T, AD), F32)] + _hosted_sems(n_x),
        compiler_params=_cp(("arbitrary",), VMEM_BIG),
    )(u, u, u, do, tot, *partials)
    return res[0], res[1], res[2], list(res[3:])


def _inproj_dw(hn, du):
    tm = min(TM, T)

    def body(hn_ref, du_ref, dw_ref):
        @pl.when(pl.program_id(1) == 0)
        def _():
            dw_ref[...] = jnp.zeros_like(dw_ref)
        dw_ref[0] += _dot_tn(hn_ref[...], du_ref[...])

    return pl.pallas_call(
        body, name="inproj_dw", grid=(NCHIP, T // tm),
        in_specs=[pl.BlockSpec((tm, D), lambda k, i: (i, 0)), pl.BlockSpec((tm, SHW), lambda k, i: (i, k))],
        out_specs=pl.BlockSpec((1, D, SHW), lambda k, i: (k, 0, 0)),
        out_shape=jax.ShapeDtypeStruct((NCHIP, D, SHW), F32),
        compiler_params=_cp(("arbitrary", "arbitrary"), VMEM_BIG),
    )(hn, du)


def _inproj_dx(du, w, h, g, dres, partials=(), grads=()):
    tm = min(TM, T)
    sent = list(partials) + list(grads)
    n_x = len(sent)
    grid = (T // tm, NCHIP)
    if grads:
        landing = [jax.ShapeDtypeStruct((NCHIP, a.shape[1] // 2, a.shape[2]), F32) for a in grads]
    else:
        landing = [jax.ShapeDtypeStruct((NCHIP - 1,) + a.shape[1:], a.dtype) for a in partials]

    def body(du_ref, w_ref, h_ref, g_ref, dres_ref, dh_ref, dg_ref, acc_s):
        i, k = pl.program_id(0), pl.program_id(1)

        @pl.when(jnp.logical_and(i == 0, k == 0))
        def _():
            dg_ref[...] = jnp.zeros_like(dg_ref)

        @pl.when(k == 0)
        def _():
            acc_s[...] = _dot_nt(du_ref[...], w_ref[0])

        @pl.when(k > 0)
        def _():
            acc_s[...] += _dot_nt(du_ref[...], w_ref[0])

        @pl.when(k == NCHIP - 1)
        def _():
            hh = h_ref[...]
            r = _rstd(hh)
            dhn = acc_s[...]
            dg_ref[...] += jnp.sum(dhn * hh * r, axis=0, keepdims=True)
            dh_ref[...] = dres_ref[...] + _rms_bwd(dhn, hh, r, g_ref[...])

    res = pl.pallas_call(
        _host(body, grid, 5, 2, n_x, _pair_copies if grads else _scatter_copies), name="inproj_dx", grid=grid,
        in_specs=[pl.BlockSpec((tm, SHW), lambda i, k: (i, k)),
                  pl.BlockSpec((1, D, SHW), lambda i, k: (k, 0, 0)),
                  pl.BlockSpec((tm, D), lambda i, k: (i, 0)),
                  pl.BlockSpec((1, D), lambda i, k: (0, 0)),
                  pl.BlockSpec((tm, D), lambda i, k: (i, 0))] + [HBM_SPEC] * n_x,
        out_specs=[pl.BlockSpec((tm, D), lambda i, k: (i, 0)), pl.BlockSpec((1, D), lambda i, k: (0, 0))]
        + [HBM_SPEC] * n_x,
        out_shape=[jax.ShapeDtypeStruct((T, D), F32), jax.ShapeDtypeStruct((1, D), F32)] + landing,
        scratch_shapes=[pltpu.VMEM((tm, D), F32)] + _hosted_sems(n_x),
        compiler_params=_cp(("arbitrary", "arbitrary"), VMEM_BIG),
    )(du, w, h, g, dres, *sent)
    return res[0], res[1], list(res[2:])


def _sum_pair(core, grads, gots):
    n = len(grads)

    def body(c_ref, *refs):
        for a in range(n):
            refs[2 * n + a][...] = (refs[a][...] + refs[n + a][...]).astype(BF16)

    mine = [pl.BlockSpec((1,) + s.shape[1:], lambda k, c: (k, c[0], 0)) for s in gots]
    same = [pl.BlockSpec((1,) + s.shape[1:], lambda k, c: (k, 0, 0)) for s in gots]
    return pl.pallas_call(
        body, name="sum_pair",
        grid_spec=pltpu.PrefetchScalarGridSpec(
            num_scalar_prefetch=1, grid=(NCHIP,), in_specs=mine + same, out_specs=same),
        out_shape=[jax.ShapeDtypeStruct(s.shape, BF16) for s in gots],
        compiler_params=_cp(("arbitrary",), VMEM_BIG),
    )(core, *grads, *gots)


def _sum_chips_share(chip, partials, gots):
    flat_p = [p for layer in partials for p in layer]
    flat_g = [g for layer in gots for g in layer]
    n, per_layer = len(flat_p), len(partials[0])

    def body(c_ref, *refs):
        full, sums = refs[2 * n:2 * n + per_layer], refs[2 * n + per_layer:3 * n + per_layer]
        send_sems, recv_sems, local_sems = refs[3 * n + per_layer:]
        x, y, c = _place()
        copies = []
        for i in range(n):
            acc = refs[i][0].astype(F32)
            for j in range(NCHIP - 1):
                acc = acc + refs[n + i][j].astype(F32)
            sums[i][...] = acc
            half = flat_p[i].shape[1]
            rows = full[i % per_layer].at[i // per_layer, pl.ds(c * half, half)]
            copies.append(pltpu.make_async_copy(sums[i], rows, local_sems.at[i]))
            copies.append(pltpu.make_async_remote_copy(
                src_ref=sums[i], dst_ref=rows, send_sem=send_sems.at[i], recv_sem=recv_sems.at[i],
                device_id=(x, y, 1 - c), device_id_type=MESH))
        for cp in copies:
            cp.start()
        for cp in copies:
            cp.wait()

    return pl.pallas_call(
        body, name="sum_chips_share",
        grid_spec=pltpu.PrefetchScalarGridSpec(
            num_scalar_prefetch=1, grid=(1,),
            in_specs=[pl.BlockSpec((1,) + s.shape[1:], lambda i, c: (c[0], 0, 0)) for s in flat_p]
            + [pl.BlockSpec(s.shape, lambda i, c: (0, 0, 0)) for s in flat_g],
            out_specs=[HBM_SPEC] * per_layer,
            scratch_shapes=[pltpu.VMEM(s.shape[1:], F32) for s in flat_p]
            + [pltpu.SemaphoreType.DMA((n,)), pltpu.SemaphoreType.DMA((n,)), pltpu.SemaphoreType.DMA((n,))]),
        out_shape=[jax.ShapeDtypeStruct((len(partials), 2 * s.shape[1], s.shape[2]), F32) for s in partials[0]],
        compiler_params=_cp(("arbitrary",), VMEM_BIG),
    )(chip, *flat_p, *flat_g)


def _adam_math(w, g, m, v):
    nm = ADAM_B1 * m + (1.0 - ADAM_B1) * g
    nv = ADAM_B2 * v + (1.0 - ADAM_B2) * (g * g)
    m_hat = nm / (1.0 - ADAM_B1 ** ADAM_STEP)
    v_hat = nv / (1.0 - ADAM_B2 ** ADAM_STEP)
    return -ADAM_LR * (m_hat / (jnp.sqrt(v_hat) + ADAM_EPS) + ADAM_WD * w), nm, nv


def _adamw(w, g, m, v, rows):
    R, C = w.shape

    def body(w_ref, g_ref, m_ref, v_ref, d_ref, nm_ref, nv_ref):
        d_ref[...], nm_ref[...], nv_ref[...] = _adam_math(w_ref[...], g_ref[...], m_ref[...], v_ref[...])

    spec = pl.BlockSpec((rows, C), lambda i: (i, 0))
    sh = jax.ShapeDtypeStruct((R, C), F32)
    return pl.pallas_call(
        body, name="adamw", grid=(R // rows,), in_specs=[spec] * 4, out_specs=[spec] * 3,
        out_shape=[sh, sh, sh], compiler_params=_cp(("arbitrary",)),
    )(w, g, m, v)


def _small_adamw(tot, ws, ms, vs):
    n = len(ws)

    def body(*refs):
        tot_ref = refs[0]
        w_refs, m_refs, v_refs = refs[1:1 + n], refs[1 + n:1 + 2 * n], refs[1 + 2 * n:1 + 3 * n]
        outs = refs[1 + 3 * n:]
        for i in range(n):
            rows, width = ws[i].shape
            g = tot_ref[pl.ds(SMALL_ROW[i], rows), pl.ds(0, width)]
            outs[4 * i][...] = g
            outs[4 * i + 1][...], outs[4 * i + 2][...], outs[4 * i + 3][...] = _adam_math(
                w_refs[i][...], g, m_refs[i][...], v_refs[i][...])
        outs[4 * n][...] = tot_ref[pl.ds(LOSS_ROW, 1), pl.ds(0, LANE)]

    vmem = pl.BlockSpec(memory_space=pltpu.VMEM)
    res = pl.pallas_call(
        body, name="small_adamw", in_specs=[vmem] * (1 + 3 * n), out_specs=[vmem] * (4 * n + 1),
        out_shape=[jax.ShapeDtypeStruct(w.shape, F32) for w in ws for _ in range(4)]
        + [jax.ShapeDtypeStruct((1, LANE), F32)],
    )(tot, *ws, *ms, *vs)
    return [res[4 * i:4 * i + 4] for i in range(n)], res[4 * n]


HBM_SPEC = pl.BlockSpec(memory_space=pltpu.HBM)


def _place():
    return lax.axis_index("x"), lax.axis_index("y"), lax.axis_index("c")


def _all_gather_split(shard):
    def body(in_ref, out_ref, send_sems, recv_sems):
        direct, relayed, passed = _gather_tree_copies([in_ref], [out_ref], send_sems, recv_sems)
        for cp in direct:
            cp.start()
        for i in range(2):
            direct[i].wait_recv()
            relayed[i].start()
            passed[i].start()
        for cp in relayed:
            cp.wait_recv()
        passed[2].start()
        for cp in passed:
            cp.wait_recv()
        for cp in direct + relayed + passed:
            cp.wait_send()

    return pl.pallas_call(
        body, name="all_gather_split", in_specs=[HBM_SPEC], out_specs=HBM_SPEC,
        out_shape=jax.ShapeDtypeStruct((NCHIP,) + shard.shape, shard.dtype),
        scratch_shapes=_hosted_sems(1, GATHER_SEMS),
    )(shard)


def _pair_exchange(grads):
    n = len(grads)

    def body(*refs):
        copies = _pair_copies(refs[:n], refs[n:2 * n], refs[2 * n], refs[2 * n + 1])
        for cp in copies:
            cp.start()
        for cp in copies:
            cp.wait()

    return pl.pallas_call(
        body, name="pair_exchange", in_specs=[HBM_SPEC] * n, out_specs=[HBM_SPEC] * n,
        out_shape=[jax.ShapeDtypeStruct((NCHIP, g.shape[1] // 2, g.shape[2]), F32) for g in grads],
        scratch_shapes=[pltpu.SemaphoreType.DMA((n,)), pltpu.SemaphoreType.DMA((n,))],
    )(*grads)


def _small_allreduce(rows, loss_blk):
    n = len(rows)

    def body(*refs):
        loss_ref, o_ref, pk, slots, send_sems, recv_sems = refs[n:]
        pk[...] = jnp.zeros_like(pk)
        for i in range(n):
            pk[pl.ds(i, 1), pl.ds(0, rows[i].shape[1])] = refs[i][...]
        pk[pl.ds(LOSS_ROW, 1), pl.ds(0, LANE)] = loss_ref[pl.ds(0, 1), :]
        x, y, c = _place()
        me = 4 * x + 2 * y + c
        slots[me] = pk[...]
        copies = []
        for r in range(1, 8):
            rx, ry, rc = (r >> 2) & 1, (r >> 1) & 1, r & 1
            peer = (x + rx - 2 * x * rx, y + ry - 2 * y * ry, c + rc - 2 * c * rc)
            cp = pltpu.make_async_remote_copy(
                src_ref=pk, dst_ref=slots.at[me], send_sem=send_sems.at[r - 1], recv_sem=recv_sems.at[r - 1],
                device_id=peer, device_id_type=MESH)
            cp.start()
            copies.append(cp)
        for cp in copies:
            cp.wait()
        acc = slots[0]
        for j in range(1, 8):
            acc = acc + slots[j]
        o_ref[...] = acc

    vmem = pl.BlockSpec(memory_space=pltpu.VMEM)
    return pl.pallas_call(
        body, name="small_allreduce", in_specs=[vmem] * (n + 1), out_specs=vmem,
        out_shape=jax.ShapeDtypeStruct((SMALL_PK, D), F32),
        scratch_shapes=[pltpu.VMEM((SMALL_PK, D), F32), pltpu.VMEM((8, SMALL_PK, D), F32),
                        pltpu.SemaphoreType.DMA((7,)), pltpu.SemaphoreType.DMA((7,))],
    )(*rows, loss_blk)


def _seg_matrix():
    i = lax.broadcasted_iota(jnp.int32, (AD, AD), 0) // DH
    j = lax.broadcasted_iota(jnp.int32, (AD, AD), 1) // DH
    return (i == j).astype(BF16)


TAIL = ("w_out", "w_ple_gate", "w_ple", "w_pw", "dw_w")


def _local_step(x, p, tgt, sm, shards, chip, ci):
    seg = _seg_matrix()
    core = jnp.reshape(ci, (1,)).astype(jnp.int32)
    chip_idx = jnp.reshape(chip, (1,)).astype(jnp.int32)
    own = lambda g, s: lax.dynamic_update_index_in_dim(g, s, chip, 0)
    w_in_next = own(_all_gather_split(shards[0]["w_in"]), shards[0]["w_in"])
    h = x
    saved = []
    for l in range(DEPTH):
        w_in = w_in_next
        row = lambda name: sm[name][l:l + 1]
        u, hn = _rms_inproj(h, row("norm_g"), w_in)
        todo = [shards[l][k] for k in TAIL] + ([shards[l + 1]["w_in"]] if l + 1 < DEPTH else [])
        o, ya, tot, got = _attn_fwd(u, jnp.tile(row("attn_out_g"), (1, AD // DH)), todo)
        got = [own(g, s) for g, s in zip(got, todo)]
        w_out = got[0].reshape(D, D)
        w_gate = got[1].reshape(D, D)
        w_ple = got[2]
        w_pw = got[3].reshape(CD, CD)
        dw = got[4]
        if l + 1 < DEPTH:
            w_in_next = got[5]
        c1 = _glu_conv(u, dw, row("dw_b"))
        c3, yc, h1, gate, pe, h2 = _layer_tail(
            c1, u, ya, h, p[l], row("conv_ln_g"), row("conv_ln_b"), w_pw, row("conv_out_g"), w_out,
            row("ple_norm_g"), w_gate, w_ple)
        saved.append(dict(h=h, u=u, hn=hn, o=o, ya=ya, tot=tot, c1=c1, c3=c3, yc=yc, h1=h1, gate=gate, pe=pe,
                          w_in=w_in, w_out=w_out, w_gate=w_gate, w_pw=w_pw, dw=dw))
        h = h2
    loss_blk, dh, dfg = _loss_head(h, tgt, sm["final_g"])
    small = [None] * DEPTH
    pending, partials, arrived = [], {}, {}
    pair_sum = lambda grads: _sum_pair(core, grads, _pair_exchange(grads))
    for l in reversed(range(DEPTH)):
        s = saved[l]
        row = lambda name: sm[name][l:l + 1]
        dh1, dy, dwg, dwp, dwo, dpg = _ple_out_bwd(
            dh, s["h1"], s["gate"], s["pe"], p[l], s["ya"], s["yc"], row("ple_norm_g"), s["w_gate"], s["w_out"])
        ag_t = jnp.tile(row("attn_out_g"), (1, AD // DH))
        do, dga, dgc, dc1, dwpw, dag, dcg, dlg, dlb = _branch_bwd(
            dy, s["o"], s["u"], s["c1"], s["c3"], ag_t, row("conv_ln_g"), row("conv_ln_b"), s["w_pw"],
            row("conv_out_g"), seg)
        dcv, dcgate, ddw, ddb = _conv_bwd(dc1, s["u"], s["dw"])
        tail = [dwo.reshape(NCHIP, 256, D), dwg.reshape(NCHIP, 256, D), dwp, dwpw.reshape(NCHIP, 128, CD), ddw]
        if l == 0:
            partials[(l, "tail")] = pair_sum(tail)
            pending.append((l, "tail"))
        send = [t for key in pending for t in partials[key]]
        dq, dk, dv, got = _attn_bwd(s["u"], do, s["tot"], send)
        for key in pending:
            arrived[key], got = got[:len(partials[key])], got[len(partials[key]):]
        pending = []
        du = jnp.concatenate([dq, dk, dv, dga, dcv, dcgate, dgc], axis=1)
        dwin = _inproj_dw(s["hn"], du)
        if l == 0:
            partials[(l, "w_in")] = pair_sum([dwin])
            dh, dng, arrived[(l, "w_in")] = _inproj_dx(du, s["w_in"], s["h"], row("norm_g"), dh1,
                                                      partials[(l, "w_in")])
        else:
            dh, dng, halves = _inproj_dx(du, s["w_in"], s["h"], row("norm_g"), dh1, grads=tail + [dwin])
            tail_p = _sum_pair(core, tail + [dwin], halves)
            partials[(l, "tail")], partials[(l, "w_in")] = tail_p[:-1], tail_p[-1:]
            pending = [(l, "tail"), (l, "w_in")]
        small[l] = dict(norm_g=dng, attn_out_g=dag.reshape(AD // DH, DH).sum(axis=0, keepdims=True), dw_b=ddb,
                        conv_ln_g=dlg, conv_ln_b=dlb, conv_out_g=dcg, ple_norm_g=dpg)
    both = lambda d: [d[(l, "w_in")] + d[(l, "tail")] for l in range(DEPTH)]
    big = dict(zip(BIG, _sum_chips_share(chip_idx, both(partials), both(arrived))))
    return loss_blk, dh, big, small, dfg


BIG = ("w_in", "w_out", "w_ple_gate", "w_ple", "w_pw", "dw_w")
SMALL2 = ("norm_g", "ple_norm_g", "dw_b", "conv_ln_g", "conv_ln_b", "conv_out_g", "attn_out_g")
SMALL_ROW = (0, 2, 4, 6, 8, 10, 12, 14)


def kernel(x, p, norm_g, w_in, attn_out_g, dw_w, dw_b, conv_ln_g, conv_ln_b, w_pw, conv_out_g, w_out, ple_norm_g, w_ple_gate, w_ple, final_g, loss_target, m_norm_g, m_w_in, m_attn_out_g, m_dw_w, m_dw_b, m_conv_ln_g, m_conv_ln_b, m_w_pw, m_conv_out_g, m_w_out, m_ple_norm_g, m_w_ple_gate, m_w_ple, m_final_g, v_norm_g, v_w_in, v_attn_out_g, v_dw_w, v_dw_b, v_conv_ln_g, v_conv_ln_b, v_w_pw, v_conv_out_g, v_w_out, v_ple_norm_g, v_w_ple_gate, v_w_ple, v_final_g):
    W = dict(norm_g=norm_g, w_in=w_in, attn_out_g=attn_out_g, dw_w=dw_w, dw_b=dw_b, conv_ln_g=conv_ln_g,
             conv_ln_b=conv_ln_b, w_pw=w_pw, conv_out_g=conv_out_g, w_out=w_out, ple_norm_g=ple_norm_g,
             w_ple_gate=w_ple_gate, w_ple=w_ple, final_g=final_g)
    M = dict(norm_g=m_norm_g, w_in=m_w_in, attn_out_g=m_attn_out_g, dw_w=m_dw_w, dw_b=m_dw_b,
             conv_ln_g=m_conv_ln_g, conv_ln_b=m_conv_ln_b, w_pw=m_w_pw, conv_out_g=m_conv_out_g, w_out=m_w_out,
             ple_norm_g=m_ple_norm_g, w_ple_gate=m_w_ple_gate, w_ple=m_w_ple, final_g=m_final_g)
    V = dict(norm_g=v_norm_g, w_in=v_w_in, attn_out_g=v_attn_out_g, dw_w=v_dw_w, dw_b=v_dw_b,
             conv_ln_g=v_conv_ln_g, conv_ln_b=v_conv_ln_b, w_pw=v_w_pw, conv_out_g=v_conv_out_g, w_out=v_w_out,
             ple_norm_g=v_ple_norm_g, w_ple_gate=v_w_ple_gate, w_ple=v_w_ple, final_g=v_final_g)
    order = ("norm_g", "w_in", "attn_out_g", "dw_w", "dw_b", "conv_ln_g", "conv_ln_b", "w_pw", "conv_out_g",
             "w_out", "ple_norm_g", "w_ple_gate", "w_ple", "final_g")

    pad_taps = lambda a: jnp.pad(a, ((0, 0), (0, CWP - CW), (0, 0)))
    cast = dict(w_in=w_in.astype(BF16), w_out=w_out.astype(BF16), w_ple_gate=w_ple_gate.astype(BF16),
                w_ple=w_ple.astype(BF16), w_pw=w_pw.astype(BF16), dw_w=pad_taps(dw_w))
    shards = [{k: v[l] for k, v in cast.items()} for l in range(DEPTH)]
    xi, yi, ci = lax.axis_index("x"), lax.axis_index("y"), lax.axis_index("c")
    chip = 2 * xi + yi

    sm = {k: W[k] for k in SMALL2}
    sm["final_g"] = final_g.reshape(1, D)
    loss_part, grad_x, big, small, dfg = _local_step(x[0], p[:, 0], loss_target[0], sm, shards, chip, ci)
    g_big = {name: big[name].reshape(cast[name].shape) for name in BIG}

    rows = [small[l][k] for k in SMALL2 for l in range(DEPTH)] + [dfg]
    small_names = SMALL2 + ("final_g",)
    as_rows = lambda t: t.reshape(1, D) if t.ndim == 1 else t
    results, loss_row = _small_adamw(
        _small_allreduce(rows, loss_part), [as_rows(W[k]) for k in small_names],
        [as_rows(M[k]) for k in small_names], [as_rows(V[k]) for k in small_names])
    loss = loss_row[0, 0]

    grads, deltas, new_m, new_v = {}, {}, {}, {}
    for name in BIG:
        wv = pad_taps(W[name]) if name == "dw_w" else W[name]
        mv = pad_taps(M[name]) if name == "dw_w" else M[name]
        vv = pad_taps(V[name]) if name == "dw_w" else V[name]
        gg = g_big[name]
        cols = wv.shape[-1]
        rows_total = wv.size // cols
        tile_rows = min(rows_total, 256)
        d2, m2, v2 = _adamw(wv.reshape(rows_total, cols), gg.reshape(rows_total, cols),
                            mv.reshape(rows_total, cols), vv.reshape(rows_total, cols), tile_rows)
        if name == "dw_w":
            cut = lambda a: a.reshape(DEPTH, CWP, LANE)[:, :CW]
            grads[name], deltas[name], new_m[name], new_v[name] = cut(gg), cut(d2), cut(m2), cut(v2)
        else:
            grads[name] = gg
            deltas[name], new_m[name], new_v[name] = (t.reshape(wv.shape) for t in (d2, m2, v2))
    for k, four in zip(small_names, results):
        grads[k], deltas[k], new_m[k], new_v[k] = (t.reshape(W[k].shape) for t in four)

    return (loss, grad_x[None], *[grads[n] for n in order], *[deltas[n] for n in order],
            *[new_m[n] for n in order], *[new_v[n] for n in order])
```

```python
import functools

import jax
import jax.numpy as jnp
from jax import lax
from jax.experimental import pallas as pl
from jax.experimental.pallas import tpu as pltpu

F32 = jnp.float32
BF16 = jnp.bfloat16

T = 2048
D = 1024
DIN = 3584
NCHIP = 4
SHW = DIN // NCHIP
AD = 512
CD = 512
DH = 64
CW = 31
CWP = 32
PLE = 256
DEPTH = 2
EPS = 1e-6
AQ = 256
HG = 4
GW = HG * DH
SR = HG * AQ
NG = AD // GW
LANE = 128
TM = 1024

ADAM_LR = 0.001
ADAM_B1 = 0.9
ADAM_B2 = 0.999
ADAM_EPS = 1e-08
ADAM_WD = 0.01
ADAM_STEP = 10

SMALL_PK = 16
LOSS_ROW = 15

VMEM_BIG = 56 * 1024 * 1024
MESH = pl.DeviceIdType.MESH


def _cp(sem=None, vmem=None):
    kw = {}
    if sem is not None:
        kw["dimension_semantics"] = sem
    if vmem is not None:
        kw["vmem_limit_bytes"] = vmem
    return pltpu.CompilerParams(**kw)


def _dot(a, b):
    return jnp.dot(a, b, preferred_element_type=F32)


def _dot_nt(a, b):
    return lax.dot_general(a, b, (((1,), (1,)), ((), ())), preferred_element_type=F32)


def _dot_tn(a, b):
    return lax.dot_general(a, b, (((0,), (0,)), ((), ())), preferred_element_type=F32)


def _dot2(x, m):
    hi = x.astype(BF16)
    lo = (x - hi.astype(F32)).astype(BF16)
    return _dot(hi, m) + _dot(lo, m)


def _sig(x):
    return 1.0 / (1.0 + jnp.exp(-x))


def _softplus(z):
    return jnp.maximum(z, 0.0) + jnp.log(1.0 + jnp.exp(-jnp.abs(z)))


def _rstd(x):
    return lax.rsqrt(jnp.mean(x * x, axis=-1, keepdims=True) + EPS)


def _rms_bwd(dy, x, r, g):
    dn = dy * g
    return r * dn - x * (r * r * r) * jnp.mean(dn * x, axis=-1, keepdims=True)


def _rms_inproj(h, g, w):
    tm = min(TM, T)

    def body(h_ref, g_ref, w_ref, u_ref, hn_ref, hn_s):
        @pl.when(pl.program_id(1) == 0)
        def _():
            hh = h_ref[...]
            hn = (hh * _rstd(hh) * g_ref[...]).astype(BF16)
            hn_s[...] = hn
            hn_ref[...] = hn
        u_ref[...] = _dot(hn_s[...], w_ref[0])

    return pl.pallas_call(
        body, name="rms_inproj", grid=(T // tm, NCHIP),
        in_specs=[pl.BlockSpec((tm, D), lambda i, k: (i, 0)),
                  pl.BlockSpec((1, D), lambda i, k: (0, 0)),
                  pl.BlockSpec((1, D, SHW), lambda i, k: (k, 0, 0))],
        out_specs=[pl.BlockSpec((tm, SHW), lambda i, k: (i, k)),
                   pl.BlockSpec((tm, D), lambda i, k: (i, 0))],
        out_shape=[jax.ShapeDtypeStruct((T, DIN), F32), jax.ShapeDtypeStruct((T, D), BF16)],
        scratch_shapes=[pltpu.VMEM((tm, D), BF16)],
        compiler_params=_cp(("arbitrary", "arbitrary"), VMEM_BIG),
    )(h, g, w)


def _attn_tiles():
    row = lax.broadcasted_iota(jnp.int32, (SR, AQ), 0) & (AQ - 1)
    col = lax.broadcasted_iota(jnp.int32, (SR, AQ), 1)
    tr = lax.broadcasted_iota(jnp.int32, (AQ, AQ), 0)
    tc = lax.broadcasted_iota(jnp.int32, (AQ, AQ), 1)
    lane_head = lax.broadcasted_iota(jnp.int32, (1, GW), 1) // DH
    return col < row, tr, tc, [lane_head == h for h in range(HG)]


def _stack_heads(t, heads):
    return jnp.concatenate([jnp.where(m, t, 0.0) for m in heads], axis=0)


def _unstack_heads(t, heads):
    out = t[:AQ]
    for h in range(1, HG):
        out = jnp.where(heads[h], t[h * AQ:(h + 1) * AQ], out)
    return out


def _tri_sum(x, tri):
    hi = x.astype(BF16)
    lo = (x - hi.astype(F32)).astype(BF16)
    both = _dot(jnp.concatenate([hi, lo], axis=0), tri)
    return both[:SR] + both[SR:]


def _scatter_copies(ps, gots, send_sems, recv_sems):
    x, y, c = _place()
    peers = [(1 - x, y), (x, 1 - y), (1 - x, 1 - y)]
    return [pltpu.make_async_remote_copy(
        src_ref=ps[a].at[2 * px + py], dst_ref=gots[a].at[r], send_sem=send_sems.at[3 * a + r],
        recv_sem=recv_sems.at[3 * a + r], device_id=(px, py, c), device_id_type=MESH)
        for a in range(len(ps)) for r, (px, py) in enumerate(peers)]


GATHER_SEMS = 7


def _gather_tree_copies(ins, outs, send_sems, recv_sems):
    x, y, c = _place()
    me, xn, yn, dg = 2 * x + y, 2 * (1 - x) + y, 2 * x + (1 - y), 2 * (1 - x) + (1 - y)
    to_x, to_y, sibling = (1 - x, y, c), (x, 1 - y, c), (x, y, 1 - c)
    direct, relayed, passed = [], [], []
    for a in range(len(ins)):
        half = ins[a].shape[0] // 2
        mine = pl.ds(c * half, half)
        first, second = pl.ds(c * half, half // 2), pl.ds(c * half + half // 2, half // 2)

        def copy(i, src, dst, to, k=GATHER_SEMS * a):
            return pltpu.make_async_remote_copy(src_ref=src, dst_ref=dst, send_sem=send_sems.at[k + i],
                                                recv_sem=recv_sems.at[k + i], device_id=to, device_id_type=MESH)

        own, slot = ins[a].at[mine], outs[a].at[me, mine]
        direct += [copy(0, own, slot, to_x), copy(1, own, slot, to_y)]
        relayed += [copy(2, outs[a].at[xn, first], outs[a].at[xn, first], to_y),
                    copy(3, outs[a].at[yn, second], outs[a].at[yn, second], to_x)]
        passed += [copy(4 + i, outs[a].at[j, mine], outs[a].at[j, mine], sibling) for i, j in enumerate((xn, yn, dg))]
    return direct, relayed, passed


def _pair_copies(ins, outs, send_sems, recv_sems):
    x, y, c = _place()
    copies = []
    for a in range(len(ins)):
        half = ins[a].shape[1] // 2
        copies.append(pltpu.make_async_remote_copy(
            src_ref=ins[a].at[:, pl.ds((1 - c) * half, half), :], dst_ref=outs[a], send_sem=send_sems.at[a],
            recv_sem=recv_sems.at[a], device_id=(x, y, 1 - c), device_id_type=MESH))
    return copies


def _host(body, grid, n_in, n_out, n_x, make_copies, mids=()):
    if not n_x:
        return body

    def hosting(*refs):
        a, b = n_in + n_x, n_in + 2 * n_x + n_out
        copies = make_copies(refs[n_in:a], refs[a + n_out:b], refs[-2], refs[-1])
        stages = copies if isinstance(copies, tuple) else (copies,)
        ids = [pl.program_id(d) for d in range(len(grid))]
        at = lambda step: functools.reduce(jnp.logical_and, [i == s for i, s in zip(ids, step)])

        @pl.when(at([0] * len(grid)))
        def _():
            for cp in stages[0]:
                cp.start()

        for before, after, step in zip(stages, stages[1:], mids):
            @pl.when(at(step))
            def _(before=before, after=after):
                for cp in before:
                    cp.wait_recv()
                for cp in after:
                    cp.start()

        body(*refs[:n_in], *refs[a:a + n_out], *refs[b:-2])

        @pl.when(at([g - 1 for g in grid]))
        def _():
            for cp in stages[-1]:
                cp.wait_recv()
            for stage in stages:
                for cp in stage:
                    cp.wait_send()

    return hosting


def _hosted_sems(n_x, per_array=3):
    n = per_array * n_x
    return [pltpu.SemaphoreType.DMA((n,)), pltpu.SemaphoreType.DMA((n,))] if n_x else []


RC = 256


def _chunk_causal(r):
    row = lax.broadcasted_iota(jnp.int32, (RC, AQ), 0) + (r * RC) % AQ
    return lax.broadcasted_iota(jnp.int32, (RC, AQ), 1) < row


def _attn_fwd(u, agw, shards=()):
    n = len(shards)
    grid = (T // AQ,)

    def body(q_ref, k_ref, v_ref, g_ref, ag_ref, o_ref, y_ref, tot_ref,
             kb_s, vb_s, qs_s, z_s, zs_s, lmb_s, suf_s, att_s, acc_s, run_s):
        qi = pl.program_id(0)

        @pl.when(qi == 0)
        def _():
            kb_s[...] = k_ref[...].astype(BF16)
            vb_s[...] = v_ref[...].astype(BF16)

        _, tr, tc, heads = _attn_tiles()
        upper = (tr > tc).astype(BF16)
        same_head = ((tr // DH) == (tc // DH)).astype(BF16)
        for g in range(NG):
            qs_s[g] = _stack_heads(q_ref[:, g * GW:(g + 1) * GW] * 0.125, heads).astype(BF16)
        acc_s[...] = jnp.zeros_like(acc_s)
        run_s[...] = jnp.zeros_like(run_s)

        def block(kb, masked):
            k0 = pl.multiple_of(kb * AQ, AQ)
            for g in range(NG):
                lanes = pl.ds(g * GW, GW)
                z_s[g] = _dot_nt(qs_s[g], kb_s[pl.ds(k0, AQ), lanes])
                for r in range(SR // RC):
                    rows = pl.ds(r * RC, RC)
                    z = z_s[g, rows, :]
                    zs = jnp.minimum(z, 0.0) - jnp.log(1.0 + jnp.exp(-jnp.abs(z)))
                    lm = zs - z
                    if masked:
                        lm = jnp.where(_chunk_causal(r), lm, 0.0)
                    run = run_s[g, rows, :]
                    zs_s[g, rows, :] = zs + run[:, 0:1]
                    hi = lm.astype(BF16)
                    lmb_s[g, rows, :] = hi
                    lmb_s[g, pl.ds(SR + r * RC, RC), :] = (lm - hi.astype(F32)).astype(BF16)
                    run_s[g, rows, :] = run + jnp.sum(lm, axis=1, keepdims=True)
                suf_s[g] = _dot(lmb_s[g], upper)
                for r in range(SR // RC):
                    rows = pl.ds(r * RC, RC)
                    att = jnp.exp(zs_s[g, rows, :] + suf_s[g, rows, :] + suf_s[g, pl.ds(SR + r * RC, RC), :])
                    if masked:
                        att = jnp.where(_chunk_causal(r), att, 0.0)
                    att_s[g, rows, :] = att.astype(BF16)
                acc_s[g] += _dot(att_s[g], vb_s[pl.ds(k0, AQ), lanes])

        block(qi, True)

        def step(i, c):
            block(qi - 1 - i, False)
            return c

        lax.fori_loop(0, qi, step, 0)
        gate = g_ref[...]
        agv = ag_ref[...]
        for g in range(NG):
            lanes = slice(g * GW, (g + 1) * GW)
            o = _unstack_heads(acc_s[g], heads)
            osq = o * o
            ms = _dot2(osq, same_head)
            gg = gate[:, lanes]
            o_ref[:, lanes] = o
            y_ref[:, lanes] = (o * lax.rsqrt(ms * (1.0 / DH) + EPS) * agv[:, lanes] * (gg * _sig(gg))).astype(BF16)
            tot_ref[g] = _unstack_heads(jnp.broadcast_to(run_s[g][:, 0:1], (SR, GW)), heads)

    tile = lambda dt, rows=SR: pltpu.VMEM((NG, rows, AQ), dt)
    scratch = [pltpu.VMEM((T, AD), BF16), pltpu.VMEM((T, AD), BF16), pltpu.VMEM((NG, SR, GW), BF16),
               tile(F32), tile(F32), tile(BF16, 2 * SR), tile(F32, 2 * SR), tile(BF16),
               pltpu.VMEM((NG, SR, GW), F32), pltpu.VMEM((NG, SR, LANE), F32)]
    col = lambda j: pl.BlockSpec((AQ, AD), lambda qi: (qi, j))
    res = pl.pallas_call(
        _host(body, grid, 5, 3, n, _gather_tree_copies, mids=((grid[0] * 5 // 8,), (grid[0] * 7 // 8,))), name="attn_fwd", grid=grid,
        in_specs=[col(0), pl.BlockSpec((T, AD), lambda qi: (0, 1)), pl.BlockSpec((T, AD), lambda qi: (0, 2)),
                  col(3), pl.BlockSpec((1, AD), lambda qi: (0, 0))] + [HBM_SPEC] * n,
        out_specs=[col(0), col(0), pl.BlockSpec((NG, AQ, GW), lambda qi: (0, qi, 0))] + [HBM_SPEC] * n,
        out_shape=[jax.ShapeDtypeStruct((T, AD), F32), jax.ShapeDtypeStruct((T, AD), BF16),
                   jax.ShapeDtypeStruct((NG, T, GW), F32)]
        + [jax.ShapeDtypeStruct((NCHIP,) + s.shape, s.dtype) for s in shards],
        scratch_shapes=scratch + _hosted_sems(n, GATHER_SEMS),
        compiler_params=_cp(("arbitrary",), VMEM_BIG),
    )(u, u, u, u, agw, *shards)
    return res[0], res[1], res[2], list(res[3:])


def _glu_conv(u, dw, db):
    tr = 256

    def body(cv_ref, cg_ref, w_ref, b_ref, c1_ref, pad_s):
        pad_s[pl.ds(0, CWP), :] = jnp.zeros((CWP, LANE), F32)
        pad_s[pl.ds(CWP, T), :] = cv_ref[...] * _sig(cg_ref[...])
        wv = w_ref[0]
        bias = b_ref[...]

        def tile(i, carry):
            r0 = pl.multiple_of(i * tr, tr)
            acc = jnp.zeros((tr, LANE), F32) + bias
            for w in range(CW):
                acc = acc + pad_s[pl.ds(r0 + (CWP - CW + 1) + w, tr), :] * wv[w:w + 1, :]
            c1_ref[pl.ds(r0, tr), :] = acc
            return carry

        lax.fori_loop(0, T // tr, tile, 0)

    return pl.pallas_call(
        body, name="glu_conv", grid=(CD // LANE,),
        in_specs=[pl.BlockSpec((T, LANE), lambda cb: (0, 16 + cb)),
                  pl.BlockSpec((T, LANE), lambda cb: (0, 20 + cb)),
                  pl.BlockSpec((1, CWP, LANE), lambda cb: (cb, 0, 0)),
                  pl.BlockSpec((1, LANE), lambda cb: (0, cb))],
        out_specs=pl.BlockSpec((T, LANE), lambda cb: (0, cb)),
        out_shape=jax.ShapeDtypeStruct((T, CD), F32),
        scratch_shapes=[pltpu.VMEM((T + CWP, LANE), F32)],
        compiler_params=_cp(("arbitrary",)),
    )(u, u, dw, db)


def _ln_silu(c1, lg, lb):
    mu = jnp.mean(c1, axis=-1, keepdims=True)
    xc = c1 - mu
    rs = lax.rsqrt(jnp.mean(xc * xc, axis=-1, keepdims=True) + EPS)
    xh = xc * rs
    ln = xh * lg + lb
    s = _sig(ln)
    return xh, rs, ln, s


def _layer_tail(c1, u, ya, h, p, lg, lb, wpw, cg, wout, pg, wgate, wple, head=None):
    tm = 256

    def body(c1_ref, gc_ref, ya_ref, h_ref, p_ref, lg_ref, lb_ref, wpw_ref, cg_ref, wout_ref,
             pg_ref, wgate_ref, wple_ref, *rest):
        c3_ref, yc_ref, h1_ref, gate_ref, pe_ref, h2_ref = rest[-6 - 2 * bool(head):][:6]
        _, _, ln, s = _ln_silu(c1_ref[...], lg_ref[...], lb_ref[...])
        c2 = (ln * s).astype(BF16)
        c3 = _dot(c2, wpw_ref[...])
        gc = gc_ref[...]
        yc = (c3 * _rstd(c3) * cg_ref[...] * (gc * _sig(gc))).astype(BF16)
        c3_ref[...] = c3
        yc_ref[...] = yc
        y = _dot(ya_ref[...], wout_ref[pl.ds(0, AD), :]) + _dot(yc, wout_ref[pl.ds(AD, CD), :])
        h1 = h_ref[...] + y
        hn2 = (h1 * _rstd(h1) * pg_ref[...]).astype(BF16)
        gate = _sig(_dot(hn2, wgate_ref[...]))
        pb = p_ref[...].astype(BF16)
        pe = jnp.concatenate([_dot(pb, wple_ref[k]) for k in range(NCHIP)], axis=1)
        h1_ref[...] = h1
        gate_ref[...] = gate.astype(BF16)
        pe_ref[...] = pe.astype(BF16)
        h2 = h1 + pe * gate
        if not head:
            h2_ref[...] = h2
            return
        t_ref, fg_ref, loss_ref, dfg_ref = rest[0], rest[1], rest[-2], rest[-1]

        @pl.when(pl.program_id(0) == 0)
        def _():
            loss_ref[...] = jnp.zeros_like(loss_ref)
            dfg_ref[...] = jnp.zeros_like(dfg_ref)
        fg = fg_ref[...]
        r = _rstd(h2)
        e = h2 * r * fg - t_ref[...]
        loss_ref[...] += 0.5 * jnp.sum(jnp.mean(e * e, axis=-1, keepdims=True))
        dy = e * (1.0 / D)
        dfg_ref[...] += jnp.sum(dy * h2 * r, axis=0, keepdims=True)
        h2_ref[...] = _rms_bwd(dy, h2, r, fg)

    row = lambda w: pl.BlockSpec((tm, w), lambda i: (i, 0))
    full = lambda *s: pl.BlockSpec(s, lambda i: (0,) * len(s))
    extra = bool(head)
    return pl.pallas_call(
        body, name="layer_tail", grid=(T // tm,),
        in_specs=[row(CD), pl.BlockSpec((tm, CD), lambda i: (i, 6)), row(AD), row(D), row(PLE),
                  full(1, CD), full(1, CD), full(CD, CD), full(1, CD), full(D, D),
                  full(1, D), full(D, D), full(NCHIP, PLE, PLE)] + [row(D), full(1, D)] * extra,
        out_specs=[row(CD), row(CD), row(D), row(D), row(D), row(D)] + [full(8, LANE), full(1, D)] * extra,
        out_shape=[jax.ShapeDtypeStruct((T, CD), F32), jax.ShapeDtypeStruct((T, CD), BF16),
                   jax.ShapeDtypeStruct((T, D), F32), jax.ShapeDtypeStruct((T, D), BF16),
                   jax.ShapeDtypeStruct((T, D), BF16), jax.ShapeDtypeStruct((T, D), F32)]
        + [jax.ShapeDtypeStruct((8, LANE), F32), jax.ShapeDtypeStruct((1, D), F32)] * extra,
        compiler_params=_cp(("arbitrary",), VMEM_BIG),
    )(c1, u, ya, h, p, lg, lb, wpw, cg, wout, pg, wgate, wple, *(head or ()))


def _ple_out_bwd(dh2, h1, gate, pe, p, ya, yc, pg, wgate, wout):
    tm = 256

    def body(dh2_ref, h1_ref, gate_ref, pe_ref, p_ref, ya_ref, yc_ref, pg_ref, wgate_ref, wout_ref,
             dh1_ref, dy_ref, dwg_ref, dwp_ref, dwo_ref, dpg_ref):
        @pl.when(pl.program_id(0) == 0)
        def _():
            dwg_ref[...] = jnp.zeros_like(dwg_ref)
            dwp_ref[...] = jnp.zeros_like(dwp_ref)
            dwo_ref[...] = jnp.zeros_like(dwo_ref)
            dpg_ref[...] = jnp.zeros_like(dpg_ref)
        dh2 = dh2_ref[...]
        h1 = h1_ref[...]
        gate = gate_ref[...].astype(F32)
        pg = pg_ref[...]
        dpe = (dh2 * gate).astype(BF16)
        dgp = (dh2 * pe_ref[...].astype(F32) * gate * (1.0 - gate)).astype(BF16)
        r = _rstd(h1)
        hn = h1 * r
        dwg_ref[...] += _dot_tn((hn * pg).astype(BF16), dgp)
        dhn2 = _dot_nt(dgp, wgate_ref[...])
        dpg_ref[...] += jnp.sum(dhn2 * hn, axis=0, keepdims=True)
        dh1 = dh2 + _rms_bwd(dhn2, h1, r, pg)
        pb = p_ref[...].astype(BF16)
        for k in range(NCHIP):
            dwp_ref[k] += _dot_tn(pb, dpe[:, k * PLE:(k + 1) * PLE])
        dh1b = dh1.astype(BF16)
        dy_ref[...] = _dot_nt(dh1b, wout_ref[...])
        dwo_ref[pl.ds(0, AD), :] += _dot_tn(ya_ref[...], dh1b)
        dwo_ref[pl.ds(AD, CD), :] += _dot_tn(yc_ref[...], dh1b)
        dh1_ref[...] = dh1

    row = lambda w: pl.BlockSpec((tm, w), lambda i: (i, 0))
    full = lambda *s: pl.BlockSpec(s, lambda i: (0,) * len(s))
    return pl.pallas_call(
        body, name="ple_out_bwd", grid=(T // tm,),
        in_specs=[row(D), row(D), row(D), row(D), row(PLE), row(AD), row(CD),
                  full(1, D), full(D, D), full(D, D)],
        out_specs=[row(D), row(D), full(D, D), full(NCHIP, PLE, PLE), full(D, D), full(1, D)],
        out_shape=[jax.ShapeDtypeStruct((T, D), F32), jax.ShapeDtypeStruct((T, D), F32),
                   jax.ShapeDtypeStruct((D, D), F32), jax.ShapeDtypeStruct((NCHIP, PLE, PLE), F32),
                   jax.ShapeDtypeStruct((D, D), F32), jax.ShapeDtypeStruct((1, D), F32)],
        compiler_params=_cp(("arbitrary",), VMEM_BIG),
    )(dh2, h1, gate, pe, p, ya, yc, pg, wgate, wout)


def _branch_bwd(dy, o, u, c1, c3, ag, lg, lb, wpw, cg, seg):
    tm = 256

    def body(dya_ref, dyc_ref, o_ref, ga_ref, gc_ref, c1_ref, c3_ref, ag_ref, lg_ref, lb_ref, wpw_ref,
             cg_ref, seg_ref, do_ref, dga_ref, dgc_ref, dc1_ref, dwpw_ref, dag_ref, dcg_ref, dlg_ref, dlb_ref):
        @pl.when(pl.program_id(0) == 0)
        def _():
            for r_ in (dwpw_ref, dag_ref, dcg_ref, dlg_ref, dlb_ref):
                r_[...] = jnp.zeros_like(r_)
        dya = dya_ref[...]
        o = o_ref[...]
        ga = ga_ref[...]
        ag_v = ag_ref[...]
        seg_m = seg_ref[...]
        r = lax.rsqrt(_dot2(o * o, seg_m) * (1.0 / DH) + EPS)
        onr = o * r
        sg = _sig(ga)
        dga_ref[...] = (dya * (onr * ag_v) * (sg * (1.0 + ga * (1.0 - sg)))).astype(BF16)
        don = dya * (ga * sg)
        dag_ref[...] += jnp.sum(don * onr, axis=0, keepdims=True)
        dn = don * ag_v
        do_ref[...] = r * dn - o * (r * r * r) * (_dot2(dn * o, seg_m) * (1.0 / DH))
        dyc = dyc_ref[...]
        c3 = c3_ref[...]
        gc = gc_ref[...]
        cg_v = cg_ref[...]
        r3 = _rstd(c3)
        cn = c3 * r3
        sc = _sig(gc)
        dgc_ref[...] = (dyc * (cn * cg_v) * (sc * (1.0 + gc * (1.0 - sc)))).astype(BF16)
        dcn = dyc * (gc * sc)
        dcg_ref[...] += jnp.sum(dcn * cn, axis=0, keepdims=True)
        dc3 = _rms_bwd(dcn, c3, r3, cg_v).astype(BF16)
        lg_v = lg_ref[...]
        xh, rs, ln, s = _ln_silu(c1_ref[...], lg_v, lb_ref[...])
        c2 = (ln * s).astype(BF16)
        dwpw_ref[...] += _dot_tn(c2, dc3)
        dc2 = _dot_nt(dc3, wpw_ref[...])
        dln = dc2 * (s * (1.0 + ln * (1.0 - s)))
        dlb_ref[...] += jnp.sum(dln, axis=0, keepdims=True)
        dlg_ref[...] += jnp.sum(dln * xh, axis=0, keepdims=True)
        dxh = dln * lg_v
        dc1_ref[...] = rs * (dxh - jnp.mean(dxh, axis=-1, keepdims=True)
                             - xh * jnp.mean(dxh * xh, axis=-1, keepdims=True))

    half = lambda j: pl.BlockSpec((tm, 512), lambda i: (i, j))
    full = lambda *s: pl.BlockSpec(s, lambda i: (0,) * len(s))
    vec = jax.ShapeDtypeStruct((1, 512), F32)
    act = jax.ShapeDtypeStruct((T, 512), F32)
    return pl.pallas_call(
        body, name="branch_bwd", grid=(T // tm,),
        in_specs=[half(0), half(1), half(0), half(3), half(6), half(0), half(0),
                  full(1, AD), full(1, CD), full(1, CD), full(CD, CD), full(1, CD), full(AD, AD)],
        out_specs=[half(0), half(0), half(0), half(0), full(CD, CD), full(1, 512), full(1, 512),
                   full(1, 512), full(1, 512)],
        out_shape=[act, jax.ShapeDtypeStruct((T, 512), BF16), jax.ShapeDtypeStruct((T, 512), BF16), act,
                   jax.ShapeDtypeStruct((CD, CD), F32), vec, vec, vec, vec],
        compiler_params=_cp(("arbitrary",), VMEM_BIG),
    )(dy, dy, o, u, u, c1, c3, ag, lg, lb, wpw, cg, seg)


def _conv_bwd(dc1, u, dw):
    tr = 64
    off = CWP - CW + 1

    def body(d_ref, cv_ref, cg_ref, w_ref, dcv_ref, dcg_ref, ddw_ref, ddb_ref, padc_s, padd_s, acc_s):
        cv = cv_ref[...]
        sg = _sig(cg_ref[...])
        padc_s[pl.ds(0, CWP), :] = jnp.zeros((CWP, LANE), F32)
        padc_s[pl.ds(CWP, T), :] = cv * sg
        padd_s[pl.ds(0, T), :] = d_ref[...]
        padd_s[pl.ds(T, CWP), :] = jnp.zeros((CWP, LANE), F32)
        acc_s[...] = jnp.zeros_like(acc_s)
        wv = w_ref[0]

        def tile(i, carry):
            r0 = pl.multiple_of(i * tr, tr)
            dt = padd_s[pl.ds(r0, tr), :]
            dc0 = jnp.zeros((tr, LANE), F32)
            for w in range(CW):
                dc0 = dc0 + padd_s[pl.ds(r0 + (CW - 1) - w, tr), :] * wv[w:w + 1, :]
                prod = dt * padc_s[pl.ds(r0 + off + w, tr), :]
                acc_s[w] += jnp.sum(prod.reshape(tr // 8, 8, LANE), axis=0)
            cvt = cv_ref[pl.ds(r0, tr), :]
            sgt = _sig(cg_ref[pl.ds(r0, tr), :])
            dcv_ref[pl.ds(r0, tr), :] = (dc0 * sgt).astype(BF16)
            dcg_ref[pl.ds(r0, tr), :] = (dc0 * cvt * sgt * (1.0 - sgt)).astype(BF16)
            return carry

        lax.fori_loop(0, T // tr, tile, 0)
        ddw_ref[0] = jnp.sum(acc_s[...], axis=1)
        ddb_ref[...] = jnp.sum(d_ref[...], axis=0, keepdims=True)

    col = lambda j: pl.BlockSpec((T, LANE), lambda cb: (0, j + cb))
    return pl.pallas_call(
        body, name="conv_bwd", grid=(CD // LANE,),
        in_specs=[col(0), col(16), col(20), pl.BlockSpec((1, CWP, LANE), lambda cb: (cb, 0, 0))],
        out_specs=[col(0), col(0), pl.BlockSpec((1, CWP, LANE), lambda cb: (cb, 0, 0)),
                   pl.BlockSpec((1, LANE), lambda cb: (0, cb))],
        out_shape=[jax.ShapeDtypeStruct((T, CD), BF16), jax.ShapeDtypeStruct((T, CD), BF16),
                   jax.ShapeDtypeStruct((NCHIP, CWP, LANE), F32), jax.ShapeDtypeStruct((1, CD), F32)],
        scratch_shapes=[pltpu.VMEM((T + CWP, LANE), F32), pltpu.VMEM((T + CWP, LANE), F32),
                        pltpu.VMEM((CWP, 8, LANE), F32)],
        compiler_params=_cp(("arbitrary",)),
    )(dc1, u, u, dw)


def _attn_bwd(u, do, tot, partials=()):
    n_x = len(partials)
    grid = (T // AQ,)

    def body(q_ref, k_ref, v_ref, do_ref, tot_ref, dq_ref, dk_ref, dv_ref, kb_s, vb_s, dk_s, dv_s):
        qi = pl.program_id(0)

        @pl.when(qi == 0)
        def _():
            kb_s[...] = k_ref[...].astype(BF16)
            vb_s[...] = v_ref[...].astype(BF16)
            dk_s[...] = jnp.zeros_like(dk_s)
            dv_s[...] = jnp.zeros_like(dv_s)

        causal, tr, tc, heads = _attn_tiles()
        upper = (tr > tc).astype(BF16)
        lower = (tr < tc).astype(BF16)
        qs, qus, dos, tots = [], [], [], []
        for g in range(NG):
            lanes = slice(g * GW, (g + 1) * GW)
            q = q_ref[:, lanes]
            qs.append(_stack_heads(q * 0.125, heads).astype(BF16))
            qus.append(_stack_heads(q, heads).astype(BF16))
            dos.append(_stack_heads(do_ref[:, lanes], heads).astype(BF16))
            totv = tot_ref[g]
            tots.append(jnp.concatenate([totv[:, h * DH:h * DH + 1] for h in range(HG)], axis=0))

        def block(kb, carry, masked):
            k0 = pl.multiple_of(kb * AQ, AQ)
            out = []
            for g in range(NG):
                lanes = pl.ds(g * GW, GW)
                lm_left, dl_left, dq = carry[g]
                kk = kb_s[pl.ds(k0, AQ), lanes]
                vv = vb_s[pl.ds(k0, AQ), lanes]
                z = _dot_nt(qs[g], kk)
                sp = _softplus(z)
                lm = jnp.where(causal, -sp, 0.0) if masked else -sp
                lm_incl = lm_left + jnp.sum(lm, axis=1, keepdims=True)
                att = jnp.exp((z - sp) + _tri_sum(lm, upper) + (tots[g] - lm_incl))
                if masked:
                    att = jnp.where(causal, att, 0.0)
                dl = att * _dot_nt(dos[g], vv)
                dv_s[pl.ds(k0, AQ), lanes] += _dot_tn(att.astype(BF16), dos[g])
                prefix = dl_left + _tri_sum(dl, lower)
                beta = jnp.exp(z - sp)
                dz = (1.0 - beta) * dl - beta * prefix
                if masked:
                    dz = jnp.where(causal, dz, 0.0)
                dzs = (dz * 0.125).astype(BF16)
                dk_s[pl.ds(k0, AQ), lanes] += _dot_tn(dzs, qus[g])
                out.append((lm_incl, dl_left + jnp.sum(dl, axis=1, keepdims=True), dq + _dot(dzs, kk)))
            return tuple(out)

        zero = jnp.zeros((SR, 1), F32)
        init = tuple((zero, zero, jnp.zeros((SR, GW), F32)) for _ in range(NG))
        carry = lax.fori_loop(0, qi, lambda kb, c: block(kb, c, False), init)
        carry = block(qi, carry, True)
        for g in range(NG):
            dq_ref[:, g * GW:(g + 1) * GW] = _unstack_heads(carry[g][2], heads).astype(BF16)

        @pl.when(qi == grid[0] - 1)
        def _():
            dk_ref[...] = dk_s[...].astype(BF16)
            dv_ref[...] = dv_s[...].astype(BF16)

    col = lambda j: pl.BlockSpec((AQ, AD), lambda qi: (qi, j))
    whole = lambda j: pl.BlockSpec((T, AD), lambda qi: (0, j), pipeline_mode=pl.Buffered(1))
    res = pl.pallas_call(
        _host(body, grid, 5, 3, n_x, _scatter_copies), name="attn_bwd", grid=grid,
        in_specs=[col(0), whole(1), whole(2), col(0), pl.BlockSpec((NG, AQ, GW), lambda qi: (0, qi, 0))]
        + [HBM_SPEC] * n_x,
        out_specs=[col(0), whole(0), whole(0)] + [HBM_SPEC] * n_x,
        out_shape=[jax.ShapeDtypeStruct((T, AD), BF16)] * 3
        + [jax.ShapeDtypeStruct((NCHIP - 1,) + a.shape[1:], a.dtype) for a in partials],
        scratch_shapes=[pltpu.VMEM((T, AD), BF16), pltpu.VMEM((T, AD), BF16), pltpu.VMEM((T, AD), F32),
                        pltpu.VMEM((T, AD), F32)] + _hosted_sems(n_x),
        compiler_params=_cp(("arbitrary",), VMEM_BIG),
    )(u, u, u, do, tot, *partials)
    return res[0], res[1], res[2], list(res[3:])


def _inproj_dw(hn, du):
    tm = min(TM, T)

    def body(hn_ref, du_ref, dw_ref):
        @pl.when(pl.program_id(1) == 0)
        def _():
            dw_ref[...] = jnp.zeros_like(dw_ref)
        dw_ref[0] += _dot_tn(hn_ref[...], du_ref[...])

    return pl.pallas_call(
        body, name="inproj_dw", grid=(NCHIP, T // tm),
        in_specs=[pl.BlockSpec((tm, D), lambda k, i: (i, 0)), pl.BlockSpec((tm, SHW), lambda k, i: (i, k))],
        out_specs=pl.BlockSpec((1, D, SHW), lambda k, i: (k, 0, 0)),
        out_shape=jax.ShapeDtypeStruct((NCHIP, D, SHW), F32),
        compiler_params=_cp(("arbitrary", "arbitrary"), VMEM_BIG),
    )(hn, du)


def _inproj_dx(du, w, h, g, dres, partials=(), grads=()):
    tm = min(TM, T)
    sent = list(partials) + list(grads)
    n_x = len(sent)
    grid = (T // tm, NCHIP)
    if grads:
        landing = [jax.ShapeDtypeStruct((NCHIP, a.shape[1] // 2, a.shape[2]), F32) for a in grads]
    else:
        landing = [jax.ShapeDtypeStruct((NCHIP - 1,) + a.shape[1:], a.dtype) for a in partials]

    def body(du_ref, w_ref, h_ref, g_ref, dres_ref, dh_ref, dg_ref, acc_s):
        i, k = pl.program_id(0), pl.program_id(1)

        @pl.when(jnp.logical_and(i == 0, k == 0))
        def _():
            dg_ref[...] = jnp.zeros_like(dg_ref)

        @pl.when(k == 0)
        def _():
            acc_s[...] = _dot_nt(du_ref[...], w_ref[0])

        @pl.when(k > 0)
        def _():
            acc_s[...] += _dot_nt(du_ref[...], w_ref[0])

        @pl.when(k == NCHIP - 1)
        def _():
            hh = h_ref[...]
            r = _rstd(hh)
            dhn = acc_s[...]
            dg_ref[...] += jnp.sum(dhn * hh * r, axis=0, keepdims=True)
            dh_ref[...] = dres_ref[...] + _rms_bwd(dhn, hh, r, g_ref[...])

    res = pl.pallas_call(
        _host(body, grid, 5, 2, n_x, _pair_copies if grads else _scatter_copies), name="inproj_dx", grid=grid,
        in_specs=[pl.BlockSpec((tm, SHW), lambda i, k: (i, k)),
                  pl.BlockSpec((1, D, SHW), lambda i, k: (k, 0, 0)),
                  pl.BlockSpec((tm, D), lambda i, k: (i, 0)),
                  pl.BlockSpec((1, D), lambda i, k: (0, 0)),
                  pl.BlockSpec((tm, D), lambda i, k: (i, 0))] + [HBM_SPEC] * n_x,
        out_specs=[pl.BlockSpec((tm, D), lambda i, k: (i, 0)), pl.BlockSpec((1, D), lambda i, k: (0, 0))]
        + [HBM_SPEC] * n_x,
        out_shape=[jax.ShapeDtypeStruct((T, D), F32), jax.ShapeDtypeStruct((1, D), F32)] + landing,
        scratch_shapes=[pltpu.VMEM((tm, D), F32)] + _hosted_sems(n_x),
        compiler_params=_cp(("arbitrary", "arbitrary"), VMEM_BIG),
    )(du, w, h, g, dres, *sent)
    return res[0], res[1], list(res[2:])


def _sum_pair(core, grads, gots):
    n = len(grads)

    def body(c_ref, *refs):
        for a in range(n):
            refs[2 * n + a][...] = (refs[a][...] + refs[n + a][...]).astype(BF16)

    mine = [pl.BlockSpec((1,) + s.shape[1:], lambda k, c: (k, c[0], 0)) for s in gots]
    same = [pl.BlockSpec((1,) + s.shape[1:], lambda k, c: (k, 0, 0)) for s in gots]
    return pl.pallas_call(
        body, name="sum_pair",
        grid_spec=pltpu.PrefetchScalarGridSpec(
            num_scalar_prefetch=1, grid=(NCHIP,), in_specs=mine + same, out_specs=same),
        out_shape=[jax.ShapeDtypeStruct(s.shape, BF16) for s in gots],
        compiler_params=_cp(("arbitrary",), VMEM_BIG),
    )(core, *grads, *gots)


def _sum_chips_share(chip, partials, gots):
    flat_p = [p for layer in partials for p in layer]
    flat_g = [g for layer in gots for g in layer]
    n, per_layer = len(flat_p), len(partials[0])

    def body(c_ref, *refs):
        full, sums = refs[2 * n:2 * n + per_layer], refs[2 * n + per_layer:3 * n + per_layer]
        send_sems, recv_sems, local_sems = refs[3 * n + per_layer:]
        x, y, c = _place()
        copies = []
        for i in range(n):
            acc = refs[i][0].astype(F32)
            for j in range(NCHIP - 1):
                acc = acc + refs[n + i][j].astype(F32)
            sums[i][...] = acc
            half = flat_p[i].shape[1]
            rows = full[i % per_layer].at[i // per_layer, pl.ds(c * half, half)]
            copies.append(pltpu.make_async_copy(sums[i], rows, local_sems.at[i]))
            copies.append(pltpu.make_async_remote_copy(
                src_ref=sums[i], dst_ref=rows, send_sem=send_sems.at[i], recv_sem=recv_sems.at[i],
                device_id=(x, y, 1 - c), device_id_type=MESH))
        for cp in copies:
            cp.start()
        for cp in copies:
            cp.wait()

    return pl.pallas_call(
        body, name="sum_chips_share",
        grid_spec=pltpu.PrefetchScalarGridSpec(
            num_scalar_prefetch=1, grid=(1,),
            in_specs=[pl.BlockSpec((1,) + s.shape[1:], lambda i, c: (c[0], 0, 0)) for s in flat_p]
            + [pl.BlockSpec(s.shape, lambda i, c: (0, 0, 0)) for s in flat_g],
            out_specs=[HBM_SPEC] * per_layer,
            scratch_shapes=[pltpu.VMEM(s.shape[1:], F32) for s in flat_p]
            + [pltpu.SemaphoreType.DMA((n,)), pltpu.SemaphoreType.DMA((n,)), pltpu.SemaphoreType.DMA((n,))]),
        out_shape=[jax.ShapeDtypeStruct((len(partials), 2 * s.shape[1], s.shape[2]), F32) for s in partials[0]],
        compiler_params=_cp(("arbitrary",), VMEM_BIG),
    )(chip, *flat_p, *flat_g)


def _adam_math(w, g, m, v):
    nm = ADAM_B1 * m + (1.0 - ADAM_B1) * g
    nv = ADAM_B2 * v + (1.0 - ADAM_B2) * (g * g)
    m_hat = nm / (1.0 - ADAM_B1 ** ADAM_STEP)
    v_hat = nv / (1.0 - ADAM_B2 ** ADAM_STEP)
    return -ADAM_LR * (m_hat / (jnp.sqrt(v_hat) + ADAM_EPS) + ADAM_WD * w), nm, nv


def _adamw(w, g, m, v, rows):
    R, C = w.shape

    def body(w_ref, g_ref, m_ref, v_ref, d_ref, nm_ref, nv_ref):
        d_ref[...], nm_ref[...], nv_ref[...] = _adam_math(w_ref[...], g_ref[...], m_ref[...], v_ref[...])

    spec = pl.BlockSpec((rows, C), lambda i: (i, 0))
    sh = jax.ShapeDtypeStruct((R, C), F32)
    return pl.pallas_call(
        body, name="adamw", grid=(R // rows,), in_specs=[spec] * 4, out_specs=[spec] * 3,
        out_shape=[sh, sh, sh], compiler_params=_cp(("arbitrary",)),
    )(w, g, m, v)


def _small_adamw(tot, ws, ms, vs):
    n = len(ws)

    def body(*refs):
        tot_ref = refs[0]
        w_refs, m_refs, v_refs = refs[1:1 + n], refs[1 + n:1 + 2 * n], refs[1 + 2 * n:1 + 3 * n]
        outs = refs[1 + 3 * n:]
        for i in range(n):
            rows, width = ws[i].shape
            g = tot_ref[pl.ds(SMALL_ROW[i], rows), pl.ds(0, width)]
            outs[4 * i][...] = g
            outs[4 * i + 1][...], outs[4 * i + 2][...], outs[4 * i + 3][...] = _adam_math(
                w_refs[i][...], g, m_refs[i][...], v_refs[i][...])
        outs[4 * n][...] = tot_ref[pl.ds(LOSS_ROW, 1), pl.ds(0, LANE)]

    vmem = pl.BlockSpec(memory_space=pltpu.VMEM)
    res = pl.pallas_call(
        body, name="small_adamw", in_specs=[vmem] * (1 + 3 * n), out_specs=[vmem] * (4 * n + 1),
        out_shape=[jax.ShapeDtypeStruct(w.shape, F32) for w in ws for _ in range(4)]
        + [jax.ShapeDtypeStruct((1, LANE), F32)],
    )(tot, *ws, *ms, *vs)
    return [res[4 * i:4 * i + 4] for i in range(n)], res[4 * n]


HBM_SPEC = pl.BlockSpec(memory_space=pltpu.HBM)


def _place():
    return lax.axis_index("x"), lax.axis_index("y"), lax.axis_index("c")


def _all_gather_split(shard):
    def body(in_ref, out_ref, send_sems, recv_sems):
        direct, relayed, passed = _gather_tree_copies([in_ref], [out_ref], send_sems, recv_sems)
        for cp in direct:
            cp.start()
        for i in range(2):
            direct[i].wait_recv()
            relayed[i].start()
            passed[i].start()
        for cp in relayed:
            cp.wait_recv()
        passed[2].start()
        for cp in passed:
            cp.wait_recv()
        for cp in direct + relayed + passed:
            cp.wait_send()

    return pl.pallas_call(
        body, name="all_gather_split", in_specs=[HBM_SPEC], out_specs=HBM_SPEC,
        out_shape=jax.ShapeDtypeStruct((NCHIP,) + shard.shape, shard.dtype),
        scratch_shapes=_hosted_sems(1, GATHER_SEMS),
    )(shard)


def _pair_exchange(grads):
    n = len(grads)

    def body(*refs):
        copies = _pair_copies(refs[:n], refs[n:2 * n], refs[2 * n], refs[2 * n + 1])
        for cp in copies:
            cp.start()
        for cp in copies:
            cp.wait()

    return pl.pallas_call(
        body, name="pair_exchange", in_specs=[HBM_SPEC] * n, out_specs=[HBM_SPEC] * n,
        out_shape=[jax.ShapeDtypeStruct((NCHIP, g.shape[1] // 2, g.shape[2]), F32) for g in grads],
        scratch_shapes=[pltpu.SemaphoreType.DMA((n,)), pltpu.SemaphoreType.DMA((n,))],
    )(*grads)


def _small_allreduce(rows, loss_blk):
    n = len(rows)

    def body(*refs):
        loss_ref, o_ref, pk, slots, send_sems, recv_sems = refs[n:]
        pk[...] = jnp.zeros_like(pk)
        for i in range(n):
            pk[pl.ds(i, 1), pl.ds(0, rows[i].shape[1])] = refs[i][...]
        pk[pl.ds(LOSS_ROW, 1), pl.ds(0, LANE)] = loss_ref[pl.ds(0, 1), :]
        x, y, c = _place()
        me = 4 * x + 2 * y + c
        slots[me] = pk[...]
        copies = []
        for r in range(1, 8):
            rx, ry, rc = (r >> 2) & 1, (r >> 1) & 1, r & 1
            peer = (x + rx - 2 * x * rx, y + ry - 2 * y * ry, c + rc - 2 * c * rc)
            cp = pltpu.make_async_remote_copy(
                src_ref=pk, dst_ref=slots.at[me], send_sem=send_sems.at[r - 1], recv_sem=recv_sems.at[r - 1],
                device_id=peer, device_id_type=MESH)
            cp.start()
            copies.append(cp)
        for cp in copies:
            cp.wait()
        acc = slots[0]
        for j in range(1, 8):
            acc = acc + slots[j]
        o_ref[...] = acc

    vmem = pl.BlockSpec(memory_space=pltpu.VMEM)
    return pl.pallas_call(
        body, name="small_allreduce", in_specs=[vmem] * (n + 1), out_specs=vmem,
        out_shape=jax.ShapeDtypeStruct((SMALL_PK, D), F32),
        scratch_shapes=[pltpu.VMEM((SMALL_PK, D), F32), pltpu.VMEM((8, SMALL_PK, D), F32),
                        pltpu.SemaphoreType.DMA((7,)), pltpu.SemaphoreType.DMA((7,))],
    )(*rows, loss_blk)


def _seg_matrix():
    i = lax.broadcasted_iota(jnp.int32, (AD, AD), 0) // DH
    j = lax.broadcasted_iota(jnp.int32, (AD, AD), 1) // DH
    return (i == j).astype(BF16)


TAIL = ("w_out", "w_ple_gate", "w_ple", "w_pw", "dw_w")


def _local_step(x, p, tgt, sm, shards, chip, ci):
    seg = _seg_matrix()
    core = jnp.reshape(ci, (1,)).astype(jnp.int32)
    chip_idx = jnp.reshape(chip, (1,)).astype(jnp.int32)
    own = lambda g, s: lax.dynamic_update_index_in_dim(g, s, chip, 0)
    w_in_next = own(_all_gather_split(shards[0]["w_in"]), shards[0]["w_in"])
    h = x
    saved = []
    for l in range(DEPTH):
        w_in = w_in_next
        row = lambda name: sm[name][l:l + 1]
        u, hn = _rms_inproj(h, row("norm_g"), w_in)
        todo = [shards[l][k] for k in TAIL] + ([shards[l + 1]["w_in"]] if l + 1 < DEPTH else [])
        o, ya, tot, got = _attn_fwd(u, jnp.tile(row("attn_out_g"), (1, AD // DH)), todo)
        got = [own(g, s) for g, s in zip(got, todo)]
        w_out = got[0].reshape(D, D)
        w_gate = got[1].reshape(D, D)
        w_ple = got[2]
        w_pw = got[3].reshape(CD, CD)
        dw = got[4]
        if l + 1 < DEPTH:
            w_in_next = got[5]
        c1 = _glu_conv(u, dw, row("dw_b"))
        c3, yc, h1, gate, pe, h2, *at_end = _layer_tail(
            c1, u, ya, h, p[l], row("conv_ln_g"), row("conv_ln_b"), w_pw, row("conv_out_g"), w_out,
            row("ple_norm_g"), w_gate, w_ple, head=(tgt, sm["final_g"]) if l == DEPTH - 1 else None)
        saved.append(dict(h=h, u=u, hn=hn, o=o, ya=ya, tot=tot, c1=c1, c3=c3, yc=yc, h1=h1, gate=gate, pe=pe,
                          w_in=w_in, w_out=w_out, w_gate=w_gate, w_pw=w_pw, dw=dw))
        h = h2
    dh, (loss_blk, dfg) = h, at_end
    small = [None] * DEPTH
    pending, partials, arrived = [], {}, {}
    pair_sum = lambda grads: _sum_pair(core, grads, _pair_exchange(grads))
    for l in reversed(range(DEPTH)):
        s = saved[l]
        row = lambda name: sm[name][l:l + 1]
        dh1, dy, dwg, dwp, dwo, dpg = _ple_out_bwd(
            dh, s["h1"], s["gate"], s["pe"], p[l], s["ya"], s["yc"], row("ple_norm_g"), s["w_gate"], s["w_out"])
        ag_t = jnp.tile(row("attn_out_g"), (1, AD // DH))
        do, dga, dgc, dc1, dwpw, dag, dcg, dlg, dlb = _branch_bwd(
            dy, s["o"], s["u"], s["c1"], s["c3"], ag_t, row("conv_ln_g"), row("conv_ln_b"), s["w_pw"],
            row("conv_out_g"), seg)
        dcv, dcgate, ddw, ddb = _conv_bwd(dc1, s["u"], s["dw"])
        tail = [dwo.reshape(NCHIP, 256, D), dwg.reshape(NCHIP, 256, D), dwp, dwpw.reshape(NCHIP, 128, CD), ddw]
        if l == 0:
            partials[(l, "tail")] = pair_sum(tail)
            pending.append((l, "tail"))
        send = [t for key in pending for t in partials[key]]
        dq, dk, dv, got = _attn_bwd(s["u"], do, s["tot"], send)
        for key in pending:
            arrived[key], got = got[:len(partials[key])], got[len(partials[key]):]
        pending = []
        du = jnp.concatenate([dq, dk, dv, dga, dcv, dcgate, dgc], axis=1)
        dwin = _inproj_dw(s["hn"], du)
        if l == 0:
            partials[(l, "w_in")] = pair_sum([dwin])
            dh, dng, arrived[(l, "w_in")] = _inproj_dx(du, s["w_in"], s["h"], row("norm_g"), dh1,
                                                      partials[(l, "w_in")])
        else:
            dh, dng, halves = _inproj_dx(du, s["w_in"], s["h"], row("norm_g"), dh1, grads=tail + [dwin])
            tail_p = _sum_pair(core, tail + [dwin], halves)
            partials[(l, "tail")], partials[(l, "w_in")] = tail_p[:-1], tail_p[-1:]
            pending = [(l, "tail"), (l, "w_in")]
        small[l] = dict(norm_g=dng, attn_out_g=dag.reshape(AD // DH, DH).sum(axis=0, keepdims=True), dw_b=ddb,
                        conv_ln_g=dlg, conv_ln_b=dlb, conv_out_g=dcg, ple_norm_g=dpg)
    both = lambda d: [d[(l, "w_in")] + d[(l, "tail")] for l in range(DEPTH)]
    big = dict(zip(BIG, _sum_chips_share(chip_idx, both(partials), both(arrived))))
    return loss_blk, dh, big, small, dfg


BIG = ("w_in", "w_out", "w_ple_gate", "w_ple", "w_pw", "dw_w")
SMALL2 = ("norm_g", "ple_norm_g", "dw_b", "conv_ln_g", "conv_ln_b", "conv_out_g", "attn_out_g")
SMALL_ROW = (0, 2, 4, 6, 8, 10, 12, 14)


def kernel(x, p, norm_g, w_in, attn_out_g, dw_w, dw_b, conv_ln_g, conv_ln_b, w_pw, conv_out_g, w_out, ple_norm_g, w_ple_gate, w_ple, final_g, loss_target, m_norm_g, m_w_in, m_attn_out_g, m_dw_w, m_dw_b, m_conv_ln_g, m_conv_ln_b, m_w_pw, m_conv_out_g, m_w_out, m_ple_norm_g, m_w_ple_gate, m_w_ple, m_final_g, v_norm_g, v_w_in, v_attn_out_g, v_dw_w, v_dw_b, v_conv_ln_g, v_conv_ln_b, v_w_pw, v_conv_out_g, v_w_out, v_ple_norm_g, v_w_ple_gate, v_w_ple, v_final_g):
    W = dict(norm_g=norm_g, w_in=w_in, attn_out_g=attn_out_g, dw_w=dw_w, dw_b=dw_b, conv_ln_g=conv_ln_g,
             conv_ln_b=conv_ln_b, w_pw=w_pw, conv_out_g=conv_out_g, w_out=w_out, ple_norm_g=ple_norm_g,
             w_ple_gate=w_ple_gate, w_ple=w_ple, final_g=final_g)
    M = dict(norm_g=m_norm_g, w_in=m_w_in, attn_out_g=m_attn_out_g, dw_w=m_dw_w, dw_b=m_dw_b,
             conv_ln_g=m_conv_ln_g, conv_ln_b=m_conv_ln_b, w_pw=m_w_pw, conv_out_g=m_conv_out_g, w_out=m_w_out,
             ple_norm_g=m_ple_norm_g, w_ple_gate=m_w_ple_gate, w_ple=m_w_ple, final_g=m_final_g)
    V = dict(norm_g=v_norm_g, w_in=v_w_in, attn_out_g=v_attn_out_g, dw_w=v_dw_w, dw_b=v_dw_b,
             conv_ln_g=v_conv_ln_g, conv_ln_b=v_conv_ln_b, w_pw=v_w_pw, conv_out_g=v_conv_out_g, w_out=v_w_out,
             ple_norm_g=v_ple_norm_g, w_ple_gate=v_w_ple_gate, w_ple=v_w_ple, final_g=v_final_g)
    order = ("norm_g", "w_in", "attn_out_g", "dw_w", "dw_b", "conv_ln_g", "conv_ln_b", "w_pw", "conv_out_g",
             "w_out", "ple_norm_g", "w_ple_gate", "w_ple", "final_g")

    pad_taps = lambda a: jnp.pad(a, ((0, 0), (0, CWP - CW), (0, 0)))
    cast = dict(w_in=w_in.astype(BF16), w_out=w_out.astype(BF16), w_ple_gate=w_ple_gate.astype(BF16),
                w_ple=w_ple.astype(BF16), w_pw=w_pw.astype(BF16), dw_w=pad_taps(dw_w))
    shards = [{k: v[l] for k, v in cast.items()} for l in range(DEPTH)]
    xi, yi, ci = lax.axis_index("x"), lax.axis_index("y"), lax.axis_index("c")
    chip = 2 * xi + yi

    sm = {k: W[k] for k in SMALL2}
    sm["final_g"] = final_g.reshape(1, D)
    loss_part, grad_x, big, small, dfg = _local_step(x[0], p[:, 0], loss_target[0], sm, shards, chip, ci)
    g_big = {name: big[name].reshape(cast[name].shape) for name in BIG}

    rows = [small[l][k] for k in SMALL2 for l in range(DEPTH)] + [dfg]
    small_names = SMALL2 + ("final_g",)
    as_rows = lambda t: t.reshape(1, D) if t.ndim == 1 else t
    results, loss_row = _small_adamw(
        _small_allreduce(rows, loss_part), [as_rows(W[k]) for k in small_names],
        [as_rows(M[k]) for k in small_names], [as_rows(V[k]) for k in small_names])
    loss = loss_row[0, 0]

    grads, deltas, new_m, new_v = {}, {}, {}, {}
    for name in BIG:
        wv = pad_taps(W[name]) if name == "dw_w" else W[name]
        mv = pad_taps(M[name]) if name == "dw_w" else M[name]
        vv = pad_taps(V[name]) if name == "dw_w" else V[name]
        gg = g_big[name]
        cols = wv.shape[-1]
        rows_total = wv.size // cols
        tile_rows = min(rows_total, 256)
        d2, m2, v2 = _adamw(wv.reshape(rows_total, cols), gg.reshape(rows_total, cols),
                            mv.reshape(rows_total, cols), vv.reshape(rows_total, cols), tile_rows)
        if name == "dw_w":
            cut = lambda a: a.reshape(DEPTH, CWP, LANE)[:, :CW]
            grads[name], deltas[name], new_m[name], new_v[name] = cut(gg), cut(d2), cut(m2), cut(v2)
        else:
            grads[name] = gg
            deltas[name], new_m[name], new_v[name] = (t.reshape(wv.shape) for t in (d2, m2, v2))
    for k, four in zip(small_names, results):
        grads[k], deltas[k], new_m[k], new_v[k] = (t.reshape(W[k].shape) for t in four)

    return (loss, grad_x[None], *[grads[n] for n in order], *[deltas[n] for n in order],
            *[new_m[n] for n in order], *[new_v[n] for n in order])
```

```python
import functools

import jax
import jax.numpy as jnp
from jax import lax
from jax.experimental import pallas as pl
from jax.experimental.pallas import tpu as pltpu

F32 = jnp.float32
BF16 = jnp.bfloat16

T = 2048
D = 1024
DIN = 3584
NCHIP = 4
SHW = DIN // NCHIP
AD = 512
CD = 512
DH = 64
CW = 31
CWP = 32
PLE = 256
DEPTH = 2
EPS = 1e-6
AQ = 256
HG = 4
GW = HG * DH
SR = HG * AQ
NG = AD // GW
LANE = 128
TM = 1024
TR = 512

ADAM_LR = 0.001
ADAM_B1 = 0.9
ADAM_B2 = 0.999
ADAM_EPS = 1e-08
ADAM_WD = 0.01
ADAM_STEP = 10

SMALL_PK = 16
LOSS_ROW = 15

VMEM_BIG = 56 * 1024 * 1024
MESH = pl.DeviceIdType.MESH


def _cp(sem=None, vmem=None):
    kw = {}
    if sem is not None:
        kw["dimension_semantics"] = sem
    if vmem is not None:
        kw["vmem_limit_bytes"] = vmem
    return pltpu.CompilerParams(**kw)


def _dot(a, b):
    return jnp.dot(a, b, preferred_element_type=F32)


def _dot_nt(a, b):
    return lax.dot_general(a, b, (((1,), (1,)), ((), ())), preferred_element_type=F32)


def _dot_tn(a, b):
    return lax.dot_general(a, b, (((0,), (0,)), ((), ())), preferred_element_type=F32)


def _dot2(x, m):
    hi = x.astype(BF16)
    lo = (x - hi.astype(F32)).astype(BF16)
    return _dot(hi, m) + _dot(lo, m)


def _sig(x):
    return 1.0 / (1.0 + jnp.exp(-x))


def _softplus(z):
    return jnp.maximum(z, 0.0) + jnp.log(1.0 + jnp.exp(-jnp.abs(z)))


def _rstd(x):
    return lax.rsqrt(jnp.mean(x * x, axis=-1, keepdims=True) + EPS)


def _rms_bwd(dy, x, r, g):
    dn = dy * g
    return r * dn - x * (r * r * r) * jnp.mean(dn * x, axis=-1, keepdims=True)


def _rms_inproj(h, g, w):
    tm = min(TM, T)

    def body(h_ref, g_ref, w_ref, u_ref, hn_ref, hn_s):
        @pl.when(pl.program_id(1) == 0)
        def _():
            hh = h_ref[...]
            hn = (hh * _rstd(hh) * g_ref[...]).astype(BF16)
            hn_s[...] = hn
            hn_ref[...] = hn
        u_ref[...] = _dot(hn_s[...], w_ref[0])

    return pl.pallas_call(
        body, name="rms_inproj", grid=(T // tm, NCHIP),
        in_specs=[pl.BlockSpec((tm, D), lambda i, k: (i, 0)),
                  pl.BlockSpec((1, D), lambda i, k: (0, 0)),
                  pl.BlockSpec((1, D, SHW), lambda i, k: (k, 0, 0))],
        out_specs=[pl.BlockSpec((tm, SHW), lambda i, k: (i, k)),
                   pl.BlockSpec((tm, D), lambda i, k: (i, 0))],
        out_shape=[jax.ShapeDtypeStruct((T, DIN), F32), jax.ShapeDtypeStruct((T, D), BF16)],
        scratch_shapes=[pltpu.VMEM((tm, D), BF16)],
        compiler_params=_cp(("arbitrary", "arbitrary"), VMEM_BIG),
    )(h, g, w)


def _attn_tiles():
    row = lax.broadcasted_iota(jnp.int32, (SR, AQ), 0) & (AQ - 1)
    col = lax.broadcasted_iota(jnp.int32, (SR, AQ), 1)
    tr = lax.broadcasted_iota(jnp.int32, (AQ, AQ), 0)
    tc = lax.broadcasted_iota(jnp.int32, (AQ, AQ), 1)
    lane_head = lax.broadcasted_iota(jnp.int32, (1, GW), 1) // DH
    return col < row, tr, tc, [lane_head == h for h in range(HG)]


def _stack_heads(t, heads):
    return jnp.concatenate([jnp.where(m, t, 0.0) for m in heads], axis=0)


def _unstack_heads(t, heads):
    out = t[:AQ]
    for h in range(1, HG):
        out = jnp.where(heads[h], t[h * AQ:(h + 1) * AQ], out)
    return out


def _tri_sum(x, tri):
    hi = x.astype(BF16)
    lo = (x - hi.astype(F32)).astype(BF16)
    both = _dot(jnp.concatenate([hi, lo], axis=0), tri)
    return both[:SR] + both[SR:]


def _scatter_copies(ps, gots, send_sems, recv_sems):
    x, y, c = _place()
    peers = [(1 - x, y), (x, 1 - y), (1 - x, 1 - y)]
    return [pltpu.make_async_remote_copy(
        src_ref=ps[a].at[2 * px + py], dst_ref=gots[a].at[r], send_sem=send_sems.at[3 * a + r],
        recv_sem=recv_sems.at[3 * a + r], device_id=(px, py, c), device_id_type=MESH)
        for a in range(len(ps)) for r, (px, py) in enumerate(peers)]


GATHER_SEMS = 7


def _gather_tree_copies(ins, outs, send_sems, recv_sems):
    x, y, c = _place()
    me, xn, yn, dg = 2 * x + y, 2 * (1 - x) + y, 2 * x + (1 - y), 2 * (1 - x) + (1 - y)
    to_x, to_y, sibling = (1 - x, y, c), (x, 1 - y, c), (x, y, 1 - c)
    direct, relayed, passed = [], [], []
    for a in range(len(ins)):
        half = ins[a].shape[0] // 2
        mine = pl.ds(c * half, half)
        first, second = pl.ds(c * half, half // 2), pl.ds(c * half + half // 2, half // 2)

        def copy(i, src, dst, to, k=GATHER_SEMS * a):
            return pltpu.make_async_remote_copy(src_ref=src, dst_ref=dst, send_sem=send_sems.at[k + i],
                                                recv_sem=recv_sems.at[k + i], device_id=to, device_id_type=MESH)

        own, slot = ins[a].at[mine], outs[a].at[me, mine]
        direct += [copy(0, own, slot, to_x), copy(1, own, slot, to_y)]
        relayed += [copy(2, outs[a].at[xn, first], outs[a].at[xn, first], to_y),
                    copy(3, outs[a].at[yn, second], outs[a].at[yn, second], to_x)]
        passed += [copy(4 + i, outs[a].at[j, mine], outs[a].at[j, mine], sibling) for i, j in enumerate((xn, yn, dg))]
    return direct, relayed, passed


def _pair_copies(ins, outs, send_sems, recv_sems):
    x, y, c = _place()
    copies = []
    for a in range(len(ins)):
        half = ins[a].shape[1] // 2
        copies.append(pltpu.make_async_remote_copy(
            src_ref=ins[a].at[:, pl.ds((1 - c) * half, half), :], dst_ref=outs[a], send_sem=send_sems.at[a],
            recv_sem=recv_sems.at[a], device_id=(x, y, 1 - c), device_id_type=MESH))
    return copies


def _host(body, grid, n_in, n_out, n_x, make_copies, mids=()):
    if not n_x:
        return body

    def hosting(*refs):
        a, b = n_in + n_x, n_in + 2 * n_x + n_out
        copies = make_copies(refs[n_in:a], refs[a + n_out:b], refs[-2], refs[-1])
        stages = copies if isinstance(copies, tuple) else (copies,)
        ids = [pl.program_id(d) for d in range(len(grid))]
        at = lambda step: functools.reduce(jnp.logical_and, [i == s for i, s in zip(ids, step)])

        @pl.when(at([0] * len(grid)))
        def _():
            for cp in stages[0]:
                cp.start()

        for before, after, step in zip(stages, stages[1:], mids):
            @pl.when(at(step))
            def _(before=before, after=after):
                for cp in before:
                    cp.wait_recv()
                for cp in after:
                    cp.start()

        body(*refs[:n_in], *refs[a:a + n_out], *refs[b:-2])

        @pl.when(at([g - 1 for g in grid]))
        def _():
            for cp in stages[-1]:
                cp.wait_recv()
            for stage in stages:
                for cp in stage:
                    cp.wait_send()

    return hosting


def _hosted_sems(n_x, per_array=3):
    n = per_array * n_x
    return [pltpu.SemaphoreType.DMA((n,)), pltpu.SemaphoreType.DMA((n,))] if n_x else []


RC = 256


def _chunk_causal(r):
    row = lax.broadcasted_iota(jnp.int32, (RC, AQ), 0) + (r * RC) % AQ
    return lax.broadcasted_iota(jnp.int32, (RC, AQ), 1) < row


def _attn_fwd(u, agw, shards=()):
    n = len(shards)
    grid = (T // AQ,)

    def body(q_ref, k_ref, v_ref, g_ref, ag_ref, o_ref, y_ref, tot_ref,
             kb_s, vb_s, qs_s, z_s, zs_s, lmb_s, suf_s, att_s, acc_s, run_s):
        qi = pl.program_id(0)

        @pl.when(qi == 0)
        def _():
            kb_s[...] = k_ref[...].astype(BF16)
            vb_s[...] = v_ref[...].astype(BF16)

        _, tr, tc, heads = _attn_tiles()
        upper = (tr > tc).astype(BF16)
        same_head = ((tr // DH) == (tc // DH)).astype(BF16)
        for g in range(NG):
            qs_s[g] = _stack_heads(q_ref[:, g * GW:(g + 1) * GW] * 0.125, heads).astype(BF16)
        acc_s[...] = jnp.zeros_like(acc_s)
        run_s[...] = jnp.zeros_like(run_s)

        def block(kb, masked):
            k0 = pl.multiple_of(kb * AQ, AQ)
            for g in range(NG):
                lanes = pl.ds(g * GW, GW)
                z_s[g] = _dot_nt(qs_s[g], kb_s[pl.ds(k0, AQ), lanes])
                for r in range(SR // RC):
                    rows = pl.ds(r * RC, RC)
                    z = z_s[g, rows, :]
                    zs = jnp.minimum(z, 0.0) - jnp.log(1.0 + jnp.exp(-jnp.abs(z)))
                    lm = zs - z
                    if masked:
                        lm = jnp.where(_chunk_causal(r), lm, 0.0)
                    run = run_s[g, rows, :]
                    zs_s[g, rows, :] = zs + run[:, 0:1]
                    hi = lm.astype(BF16)
                    lmb_s[g, rows, :] = hi
                    lmb_s[g, pl.ds(SR + r * RC, RC), :] = (lm - hi.astype(F32)).astype(BF16)
                    run_s[g, rows, :] = run + jnp.sum(lm, axis=1, keepdims=True)
                suf_s[g] = _dot(lmb_s[g], upper)
                for r in range(SR // RC):
                    rows = pl.ds(r * RC, RC)
                    att = jnp.exp(zs_s[g, rows, :] + suf_s[g, rows, :] + suf_s[g, pl.ds(SR + r * RC, RC), :])
                    if masked:
                        att = jnp.where(_chunk_causal(r), att, 0.0)
                    att_s[g, rows, :] = att.astype(BF16)
                acc_s[g] += _dot(att_s[g], vb_s[pl.ds(k0, AQ), lanes])

        block(qi, True)

        def step(i, c):
            block(qi - 1 - i, False)
            return c

        lax.fori_loop(0, qi, step, 0)
        gate = g_ref[...]
        agv = ag_ref[...]
        for g in range(NG):
            lanes = slice(g * GW, (g + 1) * GW)
            o = _unstack_heads(acc_s[g], heads)
            osq = o * o
            ms = _dot2(osq, same_head)
            gg = gate[:, lanes]
            o_ref[:, lanes] = o
            y_ref[:, lanes] = (o * lax.rsqrt(ms * (1.0 / DH) + EPS) * agv[:, lanes] * (gg * _sig(gg))).astype(BF16)
            tot_ref[g] = _unstack_heads(jnp.broadcast_to(run_s[g][:, 0:1], (SR, GW)), heads)

    tile = lambda dt, rows=SR: pltpu.VMEM((NG, rows, AQ), dt)
    scratch = [pltpu.VMEM((T, AD), BF16), pltpu.VMEM((T, AD), BF16), pltpu.VMEM((NG, SR, GW), BF16),
               tile(F32), tile(F32), tile(BF16, 2 * SR), tile(F32, 2 * SR), tile(BF16),
               pltpu.VMEM((NG, SR, GW), F32), pltpu.VMEM((NG, SR, LANE), F32)]
    col = lambda j: pl.BlockSpec((AQ, AD), lambda qi: (qi, j))
    res = pl.pallas_call(
        _host(body, grid, 5, 3, n, _gather_tree_copies, mids=((grid[0] * 5 // 8,), (grid[0] * 7 // 8,))), name="attn_fwd", grid=grid,
        in_specs=[col(0), pl.BlockSpec((T, AD), lambda qi: (0, 1)), pl.BlockSpec((T, AD), lambda qi: (0, 2)),
                  col(3), pl.BlockSpec((1, AD), lambda qi: (0, 0))] + [HBM_SPEC] * n,
        out_specs=[col(0), col(0), pl.BlockSpec((NG, AQ, GW), lambda qi: (0, qi, 0))] + [HBM_SPEC] * n,
        out_shape=[jax.ShapeDtypeStruct((T, AD), F32), jax.ShapeDtypeStruct((T, AD), BF16),
                   jax.ShapeDtypeStruct((NG, T, GW), F32)]
        + [jax.ShapeDtypeStruct((NCHIP,) + s.shape, s.dtype) for s in shards],
        scratch_shapes=scratch + _hosted_sems(n, GATHER_SEMS),
        compiler_params=_cp(("arbitrary",), VMEM_BIG),
    )(u, u, u, u, agw, *shards)
    return res[0], res[1], res[2], list(res[3:])


def _glu_conv(u, dw, db):
    tr = 256

    def body(cv_ref, cg_ref, w_ref, b_ref, c1_ref, pad_s):
        pad_s[pl.ds(0, CWP), :] = jnp.zeros((CWP, LANE), F32)
        pad_s[pl.ds(CWP, T), :] = cv_ref[...] * _sig(cg_ref[...])
        wv = w_ref[0]
        bias = b_ref[...]

        def tile(i, carry):
            r0 = pl.multiple_of(i * tr, tr)
            acc = jnp.zeros((tr, LANE), F32) + bias
            for w in range(CW):
                acc = acc + pad_s[pl.ds(r0 + (CWP - CW + 1) + w, tr), :] * wv[w:w + 1, :]
            c1_ref[pl.ds(r0, tr), :] = acc
            return carry

        lax.fori_loop(0, T // tr, tile, 0)

    return pl.pallas_call(
        body, name="glu_conv", grid=(CD // LANE,),
        in_specs=[pl.BlockSpec((T, LANE), lambda cb: (0, 16 + cb)),
                  pl.BlockSpec((T, LANE), lambda cb: (0, 20 + cb)),
                  pl.BlockSpec((1, CWP, LANE), lambda cb: (cb, 0, 0)),
                  pl.BlockSpec((1, LANE), lambda cb: (0, cb))],
        out_specs=pl.BlockSpec((T, LANE), lambda cb: (0, cb)),
        out_shape=jax.ShapeDtypeStruct((T, CD), F32),
        scratch_shapes=[pltpu.VMEM((T + CWP, LANE), F32)],
        compiler_params=_cp(("arbitrary",)),
    )(u, u, dw, db)


def _ln_silu(c1, lg, lb):
    mu = jnp.mean(c1, axis=-1, keepdims=True)
    xc = c1 - mu
    rs = lax.rsqrt(jnp.mean(xc * xc, axis=-1, keepdims=True) + EPS)
    xh = xc * rs
    ln = xh * lg + lb
    s = _sig(ln)
    return xh, rs, ln, s


def _layer_tail(c1, u, ya, h, p, lg, lb, wpw, cg, wout, pg, wgate, wple, head=None):
    tm = min(TR, T)

    def body(c1_ref, gc_ref, ya_ref, h_ref, p_ref, lg_ref, lb_ref, wpw_ref, cg_ref, wout_ref,
             pg_ref, wgate_ref, wple_ref, *rest):
        c3_ref, yc_ref, h1_ref, gate_ref, pe_ref, h2_ref = rest[-6 - 2 * bool(head):][:6]
        _, _, ln, s = _ln_silu(c1_ref[...], lg_ref[...], lb_ref[...])
        c2 = (ln * s).astype(BF16)
        c3 = _dot(c2, wpw_ref[...])
        gc = gc_ref[...]
        yc = (c3 * _rstd(c3) * cg_ref[...] * (gc * _sig(gc))).astype(BF16)
        c3_ref[...] = c3
        yc_ref[...] = yc
        y = _dot(ya_ref[...], wout_ref[pl.ds(0, AD), :]) + _dot(yc, wout_ref[pl.ds(AD, CD), :])
        h1 = h_ref[...] + y
        hn2 = (h1 * _rstd(h1) * pg_ref[...]).astype(BF16)
        gate = _sig(_dot(hn2, wgate_ref[...]))
        pb = p_ref[...].astype(BF16)
        pe = jnp.concatenate([_dot(pb, wple_ref[k]) for k in range(NCHIP)], axis=1)
        h1_ref[...] = h1
        gate_ref[...] = gate.astype(BF16)
        pe_ref[...] = pe.astype(BF16)
        h2 = h1 + pe * gate
        if not head:
            h2_ref[...] = h2
            return
        t_ref, fg_ref, loss_ref, dfg_ref = rest[0], rest[1], rest[-2], rest[-1]

        @pl.when(pl.program_id(0) == 0)
        def _():
            loss_ref[...] = jnp.zeros_like(loss_ref)
            dfg_ref[...] = jnp.zeros_like(dfg_ref)
        fg = fg_ref[...]
        r = _rstd(h2)
        e = h2 * r * fg - t_ref[...]
        loss_ref[...] += 0.5 * jnp.sum(jnp.mean(e * e, axis=-1, keepdims=True))
        dy = e * (1.0 / D)
        dfg_ref[...] += jnp.sum(dy * h2 * r, axis=0, keepdims=True)
        h2_ref[...] = _rms_bwd(dy, h2, r, fg)

    row = lambda w: pl.BlockSpec((tm, w), lambda i: (i, 0))
    full = lambda *s: pl.BlockSpec(s, lambda i: (0,) * len(s), pipeline_mode=pl.Buffered(1))
    extra = bool(head)
    return pl.pallas_call(
        body, name="layer_tail", grid=(T // tm,),
        in_specs=[row(CD), pl.BlockSpec((tm, CD), lambda i: (i, 6)), row(AD), row(D), row(PLE),
                  full(1, CD), full(1, CD), full(CD, CD), full(1, CD), full(D, D),
                  full(1, D), full(D, D), full(NCHIP, PLE, PLE)] + [row(D), full(1, D)] * extra,
        out_specs=[row(CD), row(CD), row(D), row(D), row(D), row(D)] + [full(8, LANE), full(1, D)] * extra,
        out_shape=[jax.ShapeDtypeStruct((T, CD), F32), jax.ShapeDtypeStruct((T, CD), BF16),
                   jax.ShapeDtypeStruct((T, D), F32), jax.ShapeDtypeStruct((T, D), BF16),
                   jax.ShapeDtypeStruct((T, D), BF16), jax.ShapeDtypeStruct((T, D), F32)]
        + [jax.ShapeDtypeStruct((8, LANE), F32), jax.ShapeDtypeStruct((1, D), F32)] * extra,
        compiler_params=_cp(("arbitrary",), VMEM_BIG),
    )(c1, u, ya, h, p, lg, lb, wpw, cg, wout, pg, wgate, wple, *(head or ()))


def _ple_out_bwd(dh2, h1, gate, pe, p, ya, yc, pg, wgate, wout):
    tm = min(TR, T)

    def body(dh2_ref, h1_ref, gate_ref, pe_ref, p_ref, ya_ref, yc_ref, pg_ref, wgate_ref, wout_ref,
             dh1_ref, dy_ref, dwg_ref, dwp_ref, dwo_ref, dpg_ref):
        @pl.when(pl.program_id(0) == 0)
        def _():
            dwg_ref[...] = jnp.zeros_like(dwg_ref)
            dwp_ref[...] = jnp.zeros_like(dwp_ref)
            dwo_ref[...] = jnp.zeros_like(dwo_ref)
            dpg_ref[...] = jnp.zeros_like(dpg_ref)
        dh2 = dh2_ref[...]
        h1 = h1_ref[...]
        gate = gate_ref[...].astype(F32)
        pg = pg_ref[...]
        dpe = (dh2 * gate).astype(BF16)
        dgp = (dh2 * pe_ref[...].astype(F32) * gate * (1.0 - gate)).astype(BF16)
        r = _rstd(h1)
        hn = h1 * r
        dwg_ref[...] += _dot_tn((hn * pg).astype(BF16), dgp)
        dhn2 = _dot_nt(dgp, wgate_ref[...])
        dpg_ref[...] += jnp.sum(dhn2 * hn, axis=0, keepdims=True)
        dh1 = dh2 + _rms_bwd(dhn2, h1, r, pg)
        pb = p_ref[...].astype(BF16)
        for k in range(NCHIP):
            dwp_ref[k] += _dot_tn(pb, dpe[:, k * PLE:(k + 1) * PLE])
        dh1b = dh1.astype(BF16)
        dy_ref[...] = _dot_nt(dh1b, wout_ref[...])
        dwo_ref[pl.ds(0, AD), :] += _dot_tn(ya_ref[...], dh1b)
        dwo_ref[pl.ds(AD, CD), :] += _dot_tn(yc_ref[...], dh1b)
        dh1_ref[...] = dh1

    row = lambda w: pl.BlockSpec((tm, w), lambda i: (i, 0))
    full = lambda *s: pl.BlockSpec(s, lambda i: (0,) * len(s), pipeline_mode=pl.Buffered(1))
    return pl.pallas_call(
        body, name="ple_out_bwd", grid=(T // tm,),
        in_specs=[row(D), row(D), row(D), row(D), row(PLE), row(AD), row(CD),
                  full(1, D), full(D, D), full(D, D)],
        out_specs=[row(D), row(D), full(D, D), full(NCHIP, PLE, PLE), full(D, D), full(1, D)],
        out_shape=[jax.ShapeDtypeStruct((T, D), F32), jax.ShapeDtypeStruct((T, D), F32),
                   jax.ShapeDtypeStruct((D, D), F32), jax.ShapeDtypeStruct((NCHIP, PLE, PLE), F32),
                   jax.ShapeDtypeStruct((D, D), F32), jax.ShapeDtypeStruct((1, D), F32)],
        compiler_params=_cp(("arbitrary",), VMEM_BIG),
    )(dh2, h1, gate, pe, p, ya, yc, pg, wgate, wout)


def _branch_bwd(dy, o, u, c1, c3, ag, lg, lb, wpw, cg, seg):
    tm = min(TR, T)

    def body(dya_ref, dyc_ref, o_ref, ga_ref, gc_ref, c1_ref, c3_ref, ag_ref, lg_ref, lb_ref, wpw_ref,
             cg_ref, seg_ref, do_ref, dga_ref, dgc_ref, dc1_ref, dwpw_ref, dag_ref, dcg_ref, dlg_ref, dlb_ref):
        @pl.when(pl.program_id(0) == 0)
        def _():
            for r_ in (dwpw_ref, dag_ref, dcg_ref, dlg_ref, dlb_ref):
                r_[...] = jnp.zeros_like(r_)
        dya = dya_ref[...]
        o = o_ref[...]
        ga = ga_ref[...]
        ag_v = ag_ref[...]
        seg_m = seg_ref[...]
        r = lax.rsqrt(_dot2(o * o, seg_m) * (1.0 / DH) + EPS)
        onr = o * r
        sg = _sig(ga)
        dga_ref[...] = (dya * (onr * ag_v) * (sg * (1.0 + ga * (1.0 - sg)))).astype(BF16)
        don = dya * (ga * sg)
        dag_ref[...] += jnp.sum(don * onr, axis=0, keepdims=True)
        dn = don * ag_v
        do_ref[...] = r * dn - o * (r * r * r) * (_dot2(dn * o, seg_m) * (1.0 / DH))
        dyc = dyc_ref[...]
        c3 = c3_ref[...]
        gc = gc_ref[...]
        cg_v = cg_ref[...]
        r3 = _rstd(c3)
        cn = c3 * r3
        sc = _sig(gc)
        dgc_ref[...] = (dyc * (cn * cg_v) * (sc * (1.0 + gc * (1.0 - sc)))).astype(BF16)
        dcn = dyc * (gc * sc)
        dcg_ref[...] += jnp.sum(dcn * cn, axis=0, keepdims=True)
        dc3 = _rms_bwd(dcn, c3, r3, cg_v).astype(BF16)
        lg_v = lg_ref[...]
        xh, rs, ln, s = _ln_silu(c1_ref[...], lg_v, lb_ref[...])
        c2 = (ln * s).astype(BF16)
        dwpw_ref[...] += _dot_tn(c2, dc3)
        dc2 = _dot_nt(dc3, wpw_ref[...])
        dln = dc2 * (s * (1.0 + ln * (1.0 - s)))
        dlb_ref[...] += jnp.sum(dln, axis=0, keepdims=True)
        dlg_ref[...] += jnp.sum(dln * xh, axis=0, keepdims=True)
        dxh = dln * lg_v
        dc1_ref[...] = rs * (dxh - jnp.mean(dxh, axis=-1, keepdims=True)
                             - xh * jnp.mean(dxh * xh, axis=-1, keepdims=True))

    half = lambda j: pl.BlockSpec((tm, 512), lambda i: (i, j))
    full = lambda *s: pl.BlockSpec(s, lambda i: (0,) * len(s), pipeline_mode=pl.Buffered(1))
    vec = jax.ShapeDtypeStruct((1, 512), F32)
    act = jax.ShapeDtypeStruct((T, 512), F32)
    return pl.pallas_call(
        body, name="branch_bwd", grid=(T // tm,),
        in_specs=[half(0), half(1), half(0), half(3), half(6), half(0), half(0),
                  full(1, AD), full(1, CD), full(1, CD), full(CD, CD), full(1, CD), full(AD, AD)],
        out_specs=[half(0), half(0), half(0), half(0), full(CD, CD), full(1, 512), full(1, 512),
                   full(1, 512), full(1, 512)],
        out_shape=[act, jax.ShapeDtypeStruct((T, 512), BF16), jax.ShapeDtypeStruct((T, 512), BF16), act,
                   jax.ShapeDtypeStruct((CD, CD), F32), vec, vec, vec, vec],
        compiler_params=_cp(("arbitrary",), VMEM_BIG),
    )(dy, dy, o, u, u, c1, c3, ag, lg, lb, wpw, cg, seg)


def _conv_bwd(dc1, u, dw):
    tr = 64
    off = CWP - CW + 1

    def body(d_ref, cv_ref, cg_ref, w_ref, dcv_ref, dcg_ref, ddw_ref, ddb_ref, padc_s, padd_s, acc_s):
        cv = cv_ref[...]
        sg = _sig(cg_ref[...])
        padc_s[pl.ds(0, CWP), :] = jnp.zeros((CWP, LANE), F32)
        padc_s[pl.ds(CWP, T), :] = cv * sg
        padd_s[pl.ds(0, T), :] = d_ref[...]
        padd_s[pl.ds(T, CWP), :] = jnp.zeros((CWP, LANE), F32)
        acc_s[...] = jnp.zeros_like(acc_s)
        wv = w_ref[0]

        def tile(i, carry):
            r0 = pl.multiple_of(i * tr, tr)
            dt = padd_s[pl.ds(r0, tr), :]
            dc0 = jnp.zeros((tr, LANE), F32)
            for w in range(CW):
                dc0 = dc0 + padd_s[pl.ds(r0 + (CW - 1) - w, tr), :] * wv[w:w + 1, :]
                prod = dt * padc_s[pl.ds(r0 + off + w, tr), :]
                acc_s[w] += jnp.sum(prod.reshape(tr // 8, 8, LANE), axis=0)
            cvt = cv_ref[pl.ds(r0, tr), :]
            sgt = _sig(cg_ref[pl.ds(r0, tr), :])
            dcv_ref[pl.ds(r0, tr), :] = (dc0 * sgt).astype(BF16)
            dcg_ref[pl.ds(r0, tr), :] = (dc0 * cvt * sgt * (1.0 - sgt)).astype(BF16)
            return carry

        lax.fori_loop(0, T // tr, tile, 0)
        ddw_ref[0] = jnp.sum(acc_s[...], axis=1)
        ddb_ref[...] = jnp.sum(d_ref[...], axis=0, keepdims=True)

    col = lambda j: pl.BlockSpec((T, LANE), lambda cb: (0, j + cb))
    return pl.pallas_call(
        body, name="conv_bwd", grid=(CD // LANE,),
        in_specs=[col(0), col(16), col(20), pl.BlockSpec((1, CWP, LANE), lambda cb: (cb, 0, 0))],
        out_specs=[col(0), col(0), pl.BlockSpec((1, CWP, LANE), lambda cb: (cb, 0, 0)),
                   pl.BlockSpec((1, LANE), lambda cb: (0, cb))],
        out_shape=[jax.ShapeDtypeStruct((T, CD), BF16), jax.ShapeDtypeStruct((T, CD), BF16),
                   jax.ShapeDtypeStruct((NCHIP, CWP, LANE), F32), jax.ShapeDtypeStruct((1, CD), F32)],
        scratch_shapes=[pltpu.VMEM((T + CWP, LANE), F32), pltpu.VMEM((T + CWP, LANE), F32),
                        pltpu.VMEM((CWP, 8, LANE), F32)],
        compiler_params=_cp(("arbitrary",)),
    )(dc1, u, u, dw)


def _attn_bwd(u, do, tot, partials=()):
    n_x = len(partials)
    grid = (T // AQ,)

    def body(q_ref, k_ref, v_ref, do_ref, tot_ref, dq_ref, dk_ref, dv_ref, kb_s, vb_s, dk_s, dv_s):
        qi = pl.program_id(0)

        @pl.when(qi == 0)
        def _():
            kb_s[...] = k_ref[...].astype(BF16)
            vb_s[...] = v_ref[...].astype(BF16)
            dk_s[...] = jnp.zeros_like(dk_s)
            dv_s[...] = jnp.zeros_like(dv_s)

        causal, tr, tc, heads = _attn_tiles()
        upper = (tr > tc).astype(BF16)
        lower = (tr < tc).astype(BF16)
        qs, qus, dos, tots = [], [], [], []
        for g in range(NG):
            lanes = slice(g * GW, (g + 1) * GW)
            q = q_ref[:, lanes]
            qs.append(_stack_heads(q * 0.125, heads).astype(BF16))
            qus.append(_stack_heads(q, heads).astype(BF16))
            dos.append(_stack_heads(do_ref[:, lanes], heads).astype(BF16))
            totv = tot_ref[g]
            tots.append(jnp.concatenate([totv[:, h * DH:h * DH + 1] for h in range(HG)], axis=0))

        def block(kb, carry, masked):
            k0 = pl.multiple_of(kb * AQ, AQ)
            out = []
            for g in range(NG):
                lanes = pl.ds(g * GW, GW)
                lm_left, dl_left, dq = carry[g]
                kk = kb_s[pl.ds(k0, AQ), lanes]
                vv = vb_s[pl.ds(k0, AQ), lanes]
                z = _dot_nt(qs[g], kk)
                sp = _softplus(z)
                lm = jnp.where(causal, -sp, 0.0) if masked else -sp
                lm_incl = lm_left + jnp.sum(lm, axis=1, keepdims=True)
                att = jnp.exp((z - sp) + _tri_sum(lm, upper) + (tots[g] - lm_incl))
                if masked:
                    att = jnp.where(causal, att, 0.0)
                dl = att * _dot_nt(dos[g], vv)
                dv_s[pl.ds(k0, AQ), lanes] += _dot_tn(att.astype(BF16), dos[g])
                prefix = dl_left + _tri_sum(dl, lower)
                beta = jnp.exp(z - sp)
                dz = (1.0 - beta) * dl - beta * prefix
                if masked:
                    dz = jnp.where(causal, dz, 0.0)
                dzs = (dz * 0.125).astype(BF16)
                dk_s[pl.ds(k0, AQ), lanes] += _dot_tn(dzs, qus[g])
                out.append((lm_incl, dl_left + jnp.sum(dl, axis=1, keepdims=True), dq + _dot(dzs, kk)))
            return tuple(out)

        zero = jnp.zeros((SR, 1), F32)
        init = tuple((zero, zero, jnp.zeros((SR, GW), F32)) for _ in range(NG))
        carry = lax.fori_loop(0, qi, lambda kb, c: block(kb, c, False), init)
        carry = block(qi, carry, True)
        for g in range(NG):
            dq_ref[:, g * GW:(g + 1) * GW] = _unstack_heads(carry[g][2], heads).astype(BF16)

        @pl.when(qi == grid[0] - 1)
        def _():
            dk_ref[...] = dk_s[...].astype(BF16)
            dv_ref[...] = dv_s[...].astype(BF16)

    col = lambda j: pl.BlockSpec((AQ, AD), lambda qi: (qi, j))
    whole = lambda j: pl.BlockSpec((T, AD), lambda qi: (0, j), pipeline_mode=pl.Buffered(1))
    res = pl.pallas_call(
        _host(body, grid, 5, 3, n_x, _scatter_copies), name="attn_bwd", grid=grid,
        in_specs=[col(0), whole(1), whole(2), col(0), pl.BlockSpec((NG, AQ, GW), lambda qi: (0, qi, 0))]
        + [HBM_SPEC] * n_x,
        out_specs=[col(0), whole(0), whole(0)] + [HBM_SPEC] * n_x,
        out_shape=[jax.ShapeDtypeStruct((T, AD), BF16)] * 3
        + [jax.ShapeDtypeStruct((NCHIP - 1,) + a.shape[1:], a.dtype) for a in partials],
        scratch_shapes=[pltpu.VMEM((T, AD), BF16), pltpu.VMEM((T, AD), BF16), pltpu.VMEM((T, AD), F32),
                        pltpu.VMEM((T, AD), F32)] + _hosted_sems(n_x),
        compiler_params=_cp(("arbitrary",), VMEM_BIG),
    )(u, u, u, do, tot, *partials)
    return res[0], res[1], res[2], list(res[3:])


def _inproj_dw(hn, du):
    tm = min(TM, T)

    def body(hn_ref, du_ref, dw_ref):
        @pl.when(pl.program_id(1) == 0)
        def _():
            dw_ref[...] = jnp.zeros_like(dw_ref)
        dw_ref[0] += _dot_tn(hn_ref[...], du_ref[...])

    return pl.pallas_call(
        body, name="inproj_dw", grid=(NCHIP, T // tm),
        in_specs=[pl.BlockSpec((tm, D), lambda k, i: (i, 0)), pl.BlockSpec((tm, SHW), lambda k, i: (i, k))],
        out_specs=pl.BlockSpec((1, D, SHW), lambda k, i: (k, 0, 0)),
        out_shape=jax.ShapeDtypeStruct((NCHIP, D, SHW), F32),
        compiler_params=_cp(("arbitrary", "arbitrary"), VMEM_BIG),
    )(hn, du)


def _inproj_dx(du, w, h, g, dres, partials=(), grads=()):
    tm = min(TM, T)
    sent = list(partials) + list(grads)
    n_x = len(sent)
    grid = (T // tm, NCHIP)
    if grads:
        landing = [jax.ShapeDtypeStruct((NCHIP, a.shape[1] // 2, a.shape[2]), F32) for a in grads]
    else:
        landing = [jax.ShapeDtypeStruct((NCHIP - 1,) + a.shape[1:], a.dtype) for a in partials]

    def body(du_ref, w_ref, h_ref, g_ref, dres_ref, dh_ref, dg_ref, acc_s):
        i, k = pl.program_id(0), pl.program_id(1)

        @pl.when(jnp.logical_and(i == 0, k == 0))
        def _():
            dg_ref[...] = jnp.zeros_like(dg_ref)

        @pl.when(k == 0)
        def _():
            acc_s[...] = _dot_nt(du_ref[...], w_ref[0])

        @pl.when(k > 0)
        def _():
            acc_s[...] += _dot_nt(du_ref[...], w_ref[0])

        @pl.when(k == NCHIP - 1)
        def _():
            hh = h_ref[...]
            r = _rstd(hh)
            dhn = acc_s[...]
            dg_ref[...] += jnp.sum(dhn * hh * r, axis=0, keepdims=True)
            dh_ref[...] = dres_ref[...] + _rms_bwd(dhn, hh, r, g_ref[...])

    res = pl.pallas_call(
        _host(body, grid, 5, 2, n_x, _pair_copies if grads else _scatter_copies), name="inproj_dx", grid=grid,
        in_specs=[pl.BlockSpec((tm, SHW), lambda i, k: (i, k)),
                  pl.BlockSpec((1, D, SHW), lambda i, k: (k, 0, 0)),
                  pl.BlockSpec((tm, D), lambda i, k: (i, 0)),
                  pl.BlockSpec((1, D), lambda i, k: (0, 0)),
                  pl.BlockSpec((tm, D), lambda i, k: (i, 0))] + [HBM_SPEC] * n_x,
        out_specs=[pl.BlockSpec((tm, D), lambda i, k: (i, 0)), pl.BlockSpec((1, D), lambda i, k: (0, 0))]
        + [HBM_SPEC] * n_x,
        out_shape=[jax.ShapeDtypeStruct((T, D), F32), jax.ShapeDtypeStruct((1, D), F32)] + landing,
        scratch_shapes=[pltpu.VMEM((tm, D), F32)] + _hosted_sems(n_x),
        compiler_params=_cp(("arbitrary", "arbitrary"), VMEM_BIG),
    )(du, w, h, g, dres, *sent)
    return res[0], res[1], list(res[2:])


def _sum_pair(core, grads, gots):
    n = len(grads)

    def body(c_ref, *refs):
        for a in range(n):
            refs[2 * n + a][...] = (refs[a][...] + refs[n + a][...]).astype(BF16)

    mine = [pl.BlockSpec((1,) + s.shape[1:], lambda k, c: (k, c[0], 0)) for s in gots]
    same = [pl.BlockSpec((1,) + s.shape[1:], lambda k, c: (k, 0, 0)) for s in gots]
    return pl.pallas_call(
        body, name="sum_pair",
        grid_spec=pltpu.PrefetchScalarGridSpec(
            num_scalar_prefetch=1, grid=(NCHIP,), in_specs=mine + same, out_specs=same),
        out_shape=[jax.ShapeDtypeStruct(s.shape, BF16) for s in gots],
        compiler_params=_cp(("arbitrary",), VMEM_BIG),
    )(core, *grads, *gots)


def _sum_chips_share(chip, partials, gots):
    flat_p = [p for layer in partials for p in layer]
    flat_g = [g for layer in gots for g in layer]
    n, per_layer = len(flat_p), len(partials[0])

    def body(c_ref, *refs):
        full, sums = refs[2 * n:2 * n + per_layer], refs[2 * n + per_layer:3 * n + per_layer]
        send_sems, recv_sems, local_sems = refs[3 * n + per_layer:]
        x, y, c = _place()
        copies = []
        for i in range(n):
            acc = refs[i][0].astype(F32)
            for j in range(NCHIP - 1):
                acc = acc + refs[n + i][j].astype(F32)
            sums[i][...] = acc
            half = flat_p[i].shape[1]
            rows = full[i % per_layer].at[i // per_layer, pl.ds(c * half, half)]
            copies.append(pltpu.make_async_copy(sums[i], rows, local_sems.at[i]))
            copies.append(pltpu.make_async_remote_copy(
                src_ref=sums[i], dst_ref=rows, send_sem=send_sems.at[i], recv_sem=recv_sems.at[i],
                device_id=(x, y, 1 - c), device_id_type=MESH))
        for cp in copies:
            cp.start()
        for cp in copies:
            cp.wait()

    return pl.pallas_call(
        body, name="sum_chips_share",
        grid_spec=pltpu.PrefetchScalarGridSpec(
            num_scalar_prefetch=1, grid=(1,),
            in_specs=[pl.BlockSpec((1,) + s.shape[1:], lambda i, c: (c[0], 0, 0)) for s in flat_p]
            + [pl.BlockSpec(s.shape, lambda i, c: (0, 0, 0)) for s in flat_g],
            out_specs=[HBM_SPEC] * per_layer,
            scratch_shapes=[pltpu.VMEM(s.shape[1:], F32) for s in flat_p]
            + [pltpu.SemaphoreType.DMA((n,)), pltpu.SemaphoreType.DMA((n,)), pltpu.SemaphoreType.DMA((n,))]),
        out_shape=[jax.ShapeDtypeStruct((len(partials), 2 * s.shape[1], s.shape[2]), F32) for s in partials[0]],
        compiler_params=_cp(("arbitrary",), VMEM_BIG),
    )(chip, *flat_p, *flat_g)


def _adam_math(w, g, m, v):
    nm = ADAM_B1 * m + (1.0 - ADAM_B1) * g
    nv = ADAM_B2 * v + (1.0 - ADAM_B2) * (g * g)
    m_hat = nm / (1.0 - ADAM_B1 ** ADAM_STEP)
    v_hat = nv / (1.0 - ADAM_B2 ** ADAM_STEP)
    return -ADAM_LR * (m_hat / (jnp.sqrt(v_hat) + ADAM_EPS) + ADAM_WD * w), nm, nv


def _adamw(w, g, m, v, rows):
    R, C = w.shape

    def body(w_ref, g_ref, m_ref, v_ref, d_ref, nm_ref, nv_ref):
        d_ref[...], nm_ref[...], nv_ref[...] = _adam_math(w_ref[...], g_ref[...], m_ref[...], v_ref[...])

    spec = pl.BlockSpec((rows, C), lambda i: (i, 0))
    sh = jax.ShapeDtypeStruct((R, C), F32)
    return pl.pallas_call(
        body, name="adamw", grid=(R // rows,), in_specs=[spec] * 4, out_specs=[spec] * 3,
        out_shape=[sh, sh, sh], compiler_params=_cp(("arbitrary",)),
    )(w, g, m, v)


def _small_adamw(tot, ws, ms, vs):
    n = len(ws)

    def body(*refs):
        tot_ref = refs[0]
        w_refs, m_refs, v_refs = refs[1:1 + n], refs[1 + n:1 + 2 * n], refs[1 + 2 * n:1 + 3 * n]
        outs = refs[1 + 3 * n:]
        for i in range(n):
            rows, width = ws[i].shape
            g = tot_ref[pl.ds(SMALL_ROW[i], rows), pl.ds(0, width)]
            outs[4 * i][...] = g
            outs[4 * i + 1][...], outs[4 * i + 2][...], outs[4 * i + 3][...] = _adam_math(
                w_refs[i][...], g, m_refs[i][...], v_refs[i][...])
        outs[4 * n][...] = tot_ref[pl.ds(LOSS_ROW, 1), pl.ds(0, LANE)]

    vmem = pl.BlockSpec(memory_space=pltpu.VMEM)
    res = pl.pallas_call(
        body, name="small_adamw", in_specs=[vmem] * (1 + 3 * n), out_specs=[vmem] * (4 * n + 1),
        out_shape=[jax.ShapeDtypeStruct(w.shape, F32) for w in ws for _ in range(4)]
        + [jax.ShapeDtypeStruct((1, LANE), F32)],
    )(tot, *ws, *ms, *vs)
    return [res[4 * i:4 * i + 4] for i in range(n)], res[4 * n]


HBM_SPEC = pl.BlockSpec(memory_space=pltpu.HBM)


def _place():
    return lax.axis_index("x"), lax.axis_index("y"), lax.axis_index("c")


def _all_gather_split(shard):
    def body(in_ref, out_ref, send_sems, recv_sems):
        direct, relayed, passed = _gather_tree_copies([in_ref], [out_ref], send_sems, recv_sems)
        for cp in direct:
            cp.start()
        for i in range(2):
            direct[i].wait_recv()
            relayed[i].start()
            passed[i].start()
        for cp in relayed:
            cp.wait_recv()
        passed[2].start()
        for cp in passed:
            cp.wait_recv()
        for cp in direct + relayed + passed:
            cp.wait_send()

    return pl.pallas_call(
        body, name="all_gather_split", in_specs=[HBM_SPEC], out_specs=HBM_SPEC,
        out_shape=jax.ShapeDtypeStruct((NCHIP,) + shard.shape, shard.dtype),
        scratch_shapes=_hosted_sems(1, GATHER_SEMS),
    )(shard)


def _pair_exchange(grads):
    n = len(grads)

    def body(*refs):
        copies = _pair_copies(refs[:n], refs[n:2 * n], refs[2 * n], refs[2 * n + 1])
        for cp in copies:
            cp.start()
        for cp in copies:
            cp.wait()

    return pl.pallas_call(
        body, name="pair_exchange", in_specs=[HBM_SPEC] * n, out_specs=[HBM_SPEC] * n,
        out_shape=[jax.ShapeDtypeStruct((NCHIP, g.shape[1] // 2, g.shape[2]), F32) for g in grads],
        scratch_shapes=[pltpu.SemaphoreType.DMA((n,)), pltpu.SemaphoreType.DMA((n,))],
    )(*grads)


def _small_allreduce(rows, loss_blk):
    n = len(rows)

    def body(*refs):
        loss_ref, o_ref, pk, slots, send_sems, recv_sems = refs[n:]
        pk[...] = jnp.zeros_like(pk)
        for i in range(n):
            pk[pl.ds(i, 1), pl.ds(0, rows[i].shape[1])] = refs[i][...]
        pk[pl.ds(LOSS_ROW, 1), pl.ds(0, LANE)] = loss_ref[pl.ds(0, 1), :]
        x, y, c = _place()
        me = 4 * x + 2 * y + c
        slots[me] = pk[...]
        copies = []
        for r in range(1, 8):
            rx, ry, rc = (r >> 2) & 1, (r >> 1) & 1, r & 1
            peer = (x + rx - 2 * x * rx, y + ry - 2 * y * ry, c + rc - 2 * c * rc)
            cp = pltpu.make_async_remote_copy(
                src_ref=pk, dst_ref=slots.at[me], send_sem=send_sems.at[r - 1], recv_sem=recv_sems.at[r - 1],
                device_id=peer, device_id_type=MESH)
            cp.start()
            copies.append(cp)
        for cp in copies:
            cp.wait()
        acc = slots[0]
        for j in range(1, 8):
            acc = acc + slots[j]
        o_ref[...] = acc

    vmem = pl.BlockSpec(memory_space=pltpu.VMEM)
    return pl.pallas_call(
        body, name="small_allreduce", in_specs=[vmem] * (n + 1), out_specs=vmem,
        out_shape=jax.ShapeDtypeStruct((SMALL_PK, D), F32),
        scratch_shapes=[pltpu.VMEM((SMALL_PK, D), F32), pltpu.VMEM((8, SMALL_PK, D), F32),
                        pltpu.SemaphoreType.DMA((7,)), pltpu.SemaphoreType.DMA((7,))],
    )(*rows, loss_blk)


def _seg_matrix():
    i = lax.broadcasted_iota(jnp.int32, (AD, AD), 0) // DH
    j = lax.broadcasted_iota(jnp.int32, (AD, AD), 1) // DH
    return (i == j).astype(BF16)


TAIL = ("w_out", "w_ple_gate", "w_ple", "w_pw", "dw_w")


def _local_step(x, p, tgt, sm, shards, chip, ci):
    seg = _seg_matrix()
    core = jnp.reshape(ci, (1,)).astype(jnp.int32)
    chip_idx = jnp.reshape(chip, (1,)).astype(jnp.int32)
    own = lambda g, s: lax.dynamic_update_index_in_dim(g, s, chip, 0)
    w_in_next = own(_all_gather_split(shards[0]["w_in"]), shards[0]["w_in"])
    h = x
    saved = []
    for l in range(DEPTH):
        w_in = w_in_next
        row = lambda name: sm[name][l:l + 1]
        u, hn = _rms_inproj(h, row("norm_g"), w_in)
        todo = [shards[l][k] for k in TAIL] + ([shards[l + 1]["w_in"]] if l + 1 < DEPTH else [])
        o, ya, tot, got = _attn_fwd(u, jnp.tile(row("attn_out_g"), (1, AD // DH)), todo)
        got = [own(g, s) for g, s in zip(got, todo)]
        w_out = got[0].reshape(D, D)
        w_gate = got[1].reshape(D, D)
        w_ple = got[2]
        w_pw = got[3].reshape(CD, CD)
        dw = got[4]
        if l + 1 < DEPTH:
            w_in_next = got[5]
        c1 = _glu_conv(u, dw, row("dw_b"))
        c3, yc, h1, gate, pe, h2, *at_end = _layer_tail(
            c1, u, ya, h, p[l], row("conv_ln_g"), row("conv_ln_b"), w_pw, row("conv_out_g"), w_out,
            row("ple_norm_g"), w_gate, w_ple, head=(tgt, sm["final_g"]) if l == DEPTH - 1 else None)
        saved.append(dict(h=h, u=u, hn=hn, o=o, ya=ya, tot=tot, c1=c1, c3=c3, yc=yc, h1=h1, gate=gate, pe=pe,
                          w_in=w_in, w_out=w_out, w_gate=w_gate, w_pw=w_pw, dw=dw))
        h = h2
    dh, (loss_blk, dfg) = h, at_end
    small = [None] * DEPTH
    pending, partials, arrived = [], {}, {}
    pair_sum = lambda grads: _sum_pair(core, grads, _pair_exchange(grads))
    for l in reversed(range(DEPTH)):
        s = saved[l]
        row = lambda name: sm[name][l:l + 1]
        dh1, dy, dwg, dwp, dwo, dpg = _ple_out_bwd(
            dh, s["h1"], s["gate"], s["pe"], p[l], s["ya"], s["yc"], row("ple_norm_g"), s["w_gate"], s["w_out"])
        ag_t = jnp.tile(row("attn_out_g"), (1, AD // DH))
        do, dga, dgc, dc1, dwpw, dag, dcg, dlg, dlb = _branch_bwd(
            dy, s["o"], s["u"], s["c1"], s["c3"], ag_t, row("conv_ln_g"), row("conv_ln_b"), s["w_pw"],
            row("conv_out_g"), seg)
        dcv, dcgate, ddw, ddb = _conv_bwd(dc1, s["u"], s["dw"])
        tail = [dwo.reshape(NCHIP, 256, D), dwg.reshape(NCHIP, 256, D), dwp, dwpw.reshape(NCHIP, 128, CD), ddw]
        if l == 0:
            partials[(l, "tail")] = pair_sum(tail)
            pending.append((l, "tail"))
        send = [t for key in pending for t in partials[key]]
        dq, dk, dv, got = _attn_bwd(s["u"], do, s["tot"], send)
        for key in pending:
            arrived[key], got = got[:len(partials[key])], got[len(partials[key]):]
        pending = []
        du = jnp.concatenate([dq, dk, dv, dga, dcv, dcgate, dgc], axis=1)
        dwin = _inproj_dw(s["hn"], du)
        if l == 0:
            partials[(l, "w_in")] = pair_sum([dwin])
            dh, dng, arrived[(l, "w_in")] = _inproj_dx(du, s["w_in"], s["h"], row("norm_g"), dh1,
                                                      partials[(l, "w_in")])
        else:
            dh, dng, halves = _inproj_dx(du, s["w_in"], s["h"], row("norm_g"), dh1, grads=tail + [dwin])
            tail_p = _sum_pair(core, tail + [dwin], halves)
            partials[(l, "tail")], partials[(l, "w_in")] = tail_p[:-1], tail_p[-1:]
            pending = [(l, "tail"), (l, "w_in")]
        small[l] = dict(norm_g=dng, attn_out_g=dag.reshape(AD // DH, DH).sum(axis=0, keepdims=True), dw_b=ddb,
                        conv_ln_g=dlg, conv_ln_b=dlb, conv_out_g=dcg, ple_norm_g=dpg)
    both = lambda d: [d[(l, "w_in")] + d[(l, "tail")] for l in range(DEPTH)]
    big = dict(zip(BIG, _sum_chips_share(chip_idx, both(partials), both(arrived))))
    return loss_blk, dh, big, small, dfg


BIG = ("w_in", "w_out", "w_ple_gate", "w_ple", "w_pw", "dw_w")
SMALL2 = ("norm_g", "ple_norm_g", "dw_b", "conv_ln_g", "conv_ln_b", "conv_out_g", "attn_out_g")
SMALL_ROW = (0, 2, 4, 6, 8, 10, 12, 14)


def kernel(x, p, norm_g, w_in, attn_out_g, dw_w, dw_b, conv_ln_g, conv_ln_b, w_pw, conv_out_g, w_out, ple_norm_g, w_ple_gate, w_ple, final_g, loss_target, m_norm_g, m_w_in, m_attn_out_g, m_dw_w, m_dw_b, m_conv_ln_g, m_conv_ln_b, m_w_pw, m_conv_out_g, m_w_out, m_ple_norm_g, m_w_ple_gate, m_w_ple, m_final_g, v_norm_g, v_w_in, v_attn_out_g, v_dw_w, v_dw_b, v_conv_ln_g, v_conv_ln_b, v_w_pw, v_conv_out_g, v_w_out, v_ple_norm_g, v_w_ple_gate, v_w_ple, v_final_g):
    W = dict(norm_g=norm_g, w_in=w_in, attn_out_g=attn_out_g, dw_w=dw_w, dw_b=dw_b, conv_ln_g=conv_ln_g,
             conv_ln_b=conv_ln_b, w_pw=w_pw, conv_out_g=conv_out_g, w_out=w_out, ple_norm_g=ple_norm_g,
             w_ple_gate=w_ple_gate, w_ple=w_ple, final_g=final_g)
    M = dict(norm_g=m_norm_g, w_in=m_w_in, attn_out_g=m_attn_out_g, dw_w=m_dw_w, dw_b=m_dw_b,
             conv_ln_g=m_conv_ln_g, conv_ln_b=m_conv_ln_b, w_pw=m_w_pw, conv_out_g=m_conv_out_g, w_out=m_w_out,
             ple_norm_g=m_ple_norm_g, w_ple_gate=m_w_ple_gate, w_ple=m_w_ple, final_g=m_final_g)
    V = dict(norm_g=v_norm_g, w_in=v_w_in, attn_out_g=v_attn_out_g, dw_w=v_dw_w, dw_b=v_dw_b,
             conv_ln_g=v_conv_ln_g, conv_ln_b=v_conv_ln_b, w_pw=v_w_pw, conv_out_g=v_conv_out_g, w_out=v_w_out,
             ple_norm_g=v_ple_norm_g, w_ple_gate=v_w_ple_gate, w_ple=v_w_ple, final_g=v_final_g)
    order = ("norm_g", "w_in", "attn_out_g", "dw_w", "dw_b", "conv_ln_g", "conv_ln_b", "w_pw", "conv_out_g",
             "w_out", "ple_norm_g", "w_ple_gate", "w_ple", "final_g")

    pad_taps = lambda a: jnp.pad(a, ((0, 0), (0, CWP - CW), (0, 0)))
    cast = dict(w_in=w_in.astype(BF16), w_out=w_out.astype(BF16), w_ple_gate=w_ple_gate.astype(BF16),
                w_ple=w_ple.astype(BF16), w_pw=w_pw.astype(BF16), dw_w=pad_taps(dw_w))
    shards = [{k: v[l] for k, v in cast.items()} for l in range(DEPTH)]
    xi, yi, ci = lax.axis_index("x"), lax.axis_index("y"), lax.axis_index("c")
    chip = 2 * xi + yi

    sm = {k: W[k] for k in SMALL2}
    sm["final_g"] = final_g.reshape(1, D)
    loss_part, grad_x, big, small, dfg = _local_step(x[0], p[:, 0], loss_target[0], sm, shards, chip, ci)
    g_big = {name: big[name].reshape(cast[name].shape) for name in BIG}

    rows = [small[l][k] for k in SMALL2 for l in range(DEPTH)] + [dfg]
    small_names = SMALL2 + ("final_g",)
    as_rows = lambda t: t.reshape(1, D) if t.ndim == 1 else t
    results, loss_row = _small_adamw(
        _small_allreduce(rows, loss_part), [as_rows(W[k]) for k in small_names],
        [as_rows(M[k]) for k in small_names], [as_rows(V[k]) for k in small_names])
    loss = loss_row[0, 0]

    grads, deltas, new_m, new_v = {}, {}, {}, {}
    for name in BIG:
        wv = pad_taps(W[name]) if name == "dw_w" else W[name]
        mv = pad_taps(M[name]) if name == "dw_w" else M[name]
        vv = pad_taps(V[name]) if name == "dw_w" else V[name]
        gg = g_big[name]
        cols = wv.shape[-1]
        rows_total = wv.size // cols
        tile_rows = min(rows_total, 256)
        d2, m2, v2 = _adamw(wv.reshape(rows_total, cols), gg.reshape(rows_total, cols),
                            mv.reshape(rows_total, cols), vv.reshape(rows_total, cols), tile_rows)
        if name == "dw_w":
            cut = lambda a: a.reshape(DEPTH, CWP, LANE)[:, :CW]
            grads[name], deltas[name], new_m[name], new_v[name] = cut(gg), cut(d2), cut(m2), cut(v2)
        else:
            grads[name] = gg
            deltas[name], new_m[name], new_v[name] = (t.reshape(wv.shape) for t in (d2, m2, v2))
    for k, four in zip(small_names, results):
        grads[k], deltas[k], new_m[k], new_v[k] = (t.reshape(W[k].shape) for t in four)

    return (loss, grad_x[None], *[grads[n] for n in order], *[deltas[n] for n in order],
            *[new_m[n] for n in order], *[new_v[n] for n in order])
```

```python
import functools

import jax
import jax.numpy as jnp
from jax import lax
from jax.experimental import pallas as pl
from jax.experimental.pallas import tpu as pltpu

F32 = jnp.float32
BF16 = jnp.bfloat16

T = 2048
D = 1024
DIN = 3584
NCHIP = 4
SHW = DIN // NCHIP
AD = 512
CD = 512
DH = 64
CW = 31
CWP = 32
PLE = 256
DEPTH = 2
EPS = 1e-6
AQ = 256
HG = 4
GW = HG * DH
SR = HG * AQ
NG = AD // GW
LANE = 128
TM = 1024
TR = 256

ADAM_LR = 0.001
ADAM_B1 = 0.9
ADAM_B2 = 0.999
ADAM_EPS = 1e-08
ADAM_WD = 0.01
ADAM_STEP = 10

SMALL_PK = 16
LOSS_ROW = 15

VMEM_BIG = 56 * 1024 * 1024
MESH = pl.DeviceIdType.MESH


def _cp(sem=None, vmem=None):
    kw = {}
    if sem is not None:
        kw["dimension_semantics"] = sem
    if vmem is not None:
        kw["vmem_limit_bytes"] = vmem
    return pltpu.CompilerParams(**kw)


def _dot(a, b):
    return jnp.dot(a, b, preferred_element_type=F32)


def _dot_nt(a, b):
    return lax.dot_general(a, b, (((1,), (1,)), ((), ())), preferred_element_type=F32)


def _dot_tn(a, b):
    return lax.dot_general(a, b, (((0,), (0,)), ((), ())), preferred_element_type=F32)


def _dot2(x, m):
    hi = x.astype(BF16)
    lo = (x - hi.astype(F32)).astype(BF16)
    return _dot(hi, m) + _dot(lo, m)


def _sig(x):
    return 1.0 / (1.0 + jnp.exp(-x))


def _softplus(z):
    return jnp.maximum(z, 0.0) + jnp.log(1.0 + jnp.exp(-jnp.abs(z)))


def _rstd(x):
    return lax.rsqrt(jnp.mean(x * x, axis=-1, keepdims=True) + EPS)


def _rms_bwd(dy, x, r, g):
    dn = dy * g
    return r * dn - x * (r * r * r) * jnp.mean(dn * x, axis=-1, keepdims=True)


def _rms_inproj(h, g, w):
    tm = min(TM, T)

    def body(h_ref, g_ref, w_ref, u_ref, hn_ref, hn_s):
        @pl.when(pl.program_id(1) == 0)
        def _():
            hh = h_ref[...]
            hn = (hh * _rstd(hh) * g_ref[...]).astype(BF16)
            hn_s[...] = hn
            hn_ref[...] = hn
        u_ref[...] = _dot(hn_s[...], w_ref[0])

    return pl.pallas_call(
        body, name="rms_inproj", grid=(T // tm, NCHIP),
        in_specs=[pl.BlockSpec((tm, D), lambda i, k: (i, 0)),
                  pl.BlockSpec((1, D), lambda i, k: (0, 0)),
                  pl.BlockSpec((1, D, SHW), lambda i, k: (k, 0, 0))],
        out_specs=[pl.BlockSpec((tm, SHW), lambda i, k: (i, k)),
                   pl.BlockSpec((tm, D), lambda i, k: (i, 0))],
        out_shape=[jax.ShapeDtypeStruct((T, DIN), F32), jax.ShapeDtypeStruct((T, D), BF16)],
        scratch_shapes=[pltpu.VMEM((tm, D), BF16)],
        compiler_params=_cp(("arbitrary", "arbitrary"), VMEM_BIG),
    )(h, g, w)


def _attn_tiles():
    row = lax.broadcasted_iota(jnp.int32, (SR, AQ), 0) & (AQ - 1)
    col = lax.broadcasted_iota(jnp.int32, (SR, AQ), 1)
    tr = lax.broadcasted_iota(jnp.int32, (AQ, AQ), 0)
    tc = lax.broadcasted_iota(jnp.int32, (AQ, AQ), 1)
    lane_head = lax.broadcasted_iota(jnp.int32, (1, GW), 1) // DH
    return col < row, tr, tc, [lane_head == h for h in range(HG)]


def _stack_heads(t, heads):
    return jnp.concatenate([jnp.where(m, t, 0.0) for m in heads], axis=0)


def _unstack_heads(t, heads):
    out = t[:AQ]
    for h in range(1, HG):
        out = jnp.where(heads[h], t[h * AQ:(h + 1) * AQ], out)
    return out


def _tri_sum(x, tri):
    hi = x.astype(BF16)
    lo = (x - hi.astype(F32)).astype(BF16)
    both = _dot(jnp.concatenate([hi, lo], axis=0), tri)
    return both[:SR] + both[SR:]


def _scatter_copies(ps, gots, send_sems, recv_sems):
    x, y, c = _place()
    peers = [(1 - x, y), (x, 1 - y), (1 - x, 1 - y)]
    return [pltpu.make_async_remote_copy(
        src_ref=ps[a].at[2 * px + py], dst_ref=gots[a].at[r], send_sem=send_sems.at[3 * a + r],
        recv_sem=recv_sems.at[3 * a + r], device_id=(px, py, c), device_id_type=MESH)
        for a in range(len(ps)) for r, (px, py) in enumerate(peers)]


GATHER_SEMS = 7


def _gather_tree_copies(ins, outs, send_sems, recv_sems):
    x, y, c = _place()
    me, xn, yn, dg = 2 * x + y, 2 * (1 - x) + y, 2 * x + (1 - y), 2 * (1 - x) + (1 - y)
    to_x, to_y, sibling = (1 - x, y, c), (x, 1 - y, c), (x, y, 1 - c)
    direct, relayed, passed = [], [], []
    for a in range(len(ins)):
        half = ins[a].shape[0] // 2
        mine = pl.ds(c * half, half)
        first, second = pl.ds(c * half, half // 2), pl.ds(c * half + half // 2, half // 2)

        def copy(i, src, dst, to, k=GATHER_SEMS * a):
            return pltpu.make_async_remote_copy(src_ref=src, dst_ref=dst, send_sem=send_sems.at[k + i],
                                                recv_sem=recv_sems.at[k + i], device_id=to, device_id_type=MESH)

        own, slot = ins[a].at[mine], outs[a].at[me, mine]
        direct += [copy(0, own, slot, to_x), copy(1, own, slot, to_y)]
        relayed += [copy(2, outs[a].at[xn, first], outs[a].at[xn, first], to_y),
                    copy(3, outs[a].at[yn, second], outs[a].at[yn, second], to_x)]
        passed += [copy(4 + i, outs[a].at[j, mine], outs[a].at[j, mine], sibling) for i, j in enumerate((xn, yn, dg))]
    return direct, relayed, passed


def _pair_copies(ins, outs, send_sems, recv_sems):
    x, y, c = _place()
    copies = []
    for a in range(len(ins)):
        half = ins[a].shape[1] // 2
        copies.append(pltpu.make_async_remote_copy(
            src_ref=ins[a].at[:, pl.ds((1 - c) * half, half), :], dst_ref=outs[a], send_sem=send_sems.at[a],
            recv_sem=recv_sems.at[a], device_id=(x, y, 1 - c), device_id_type=MESH))
    return copies


def _host(body, grid, n_in, n_out, n_x, make_copies, mids=()):
    if not n_x:
        return body

    def hosting(*refs):
        a, b = n_in + n_x, n_in + 2 * n_x + n_out
        copies = make_copies(refs[n_in:a], refs[a + n_out:b], refs[-2], refs[-1])
        stages = copies if isinstance(copies, tuple) else (copies,)
        ids = [pl.program_id(d) for d in range(len(grid))]
        at = lambda step: functools.reduce(jnp.logical_and, [i == s for i, s in zip(ids, step)])

        @pl.when(at([0] * len(grid)))
        def _():
            for cp in stages[0]:
                cp.start()

        for before, after, step in zip(stages, stages[1:], mids):
            @pl.when(at(step))
            def _(before=before, after=after):
                for cp in before:
                    cp.wait_recv()
                for cp in after:
                    cp.start()

        body(*refs[:n_in], *refs[a:a + n_out], *refs[b:-2])

        @pl.when(at([g - 1 for g in grid]))
        def _():
            for cp in stages[-1]:
                cp.wait_recv()
            for stage in stages:
                for cp in stage:
                    cp.wait_send()

    return hosting


def _hosted_sems(n_x, per_array=3):
    n = per_array * n_x
    return [pltpu.SemaphoreType.DMA((n,)), pltpu.SemaphoreType.DMA((n,))] if n_x else []


RC = 256


def _chunk_causal(r):
    row = lax.broadcasted_iota(jnp.int32, (RC, AQ), 0) + (r * RC) % AQ
    return lax.broadcasted_iota(jnp.int32, (RC, AQ), 1) < row


def _attn_fwd(u, agw, shards=()):
    n = len(shards)
    grid = (T // AQ,)

    def body(q_ref, k_ref, v_ref, g_ref, ag_ref, o_ref, y_ref, tot_ref,
             kb_s, vb_s, qs_s, z_s, zs_s, lmb_s, suf_s, att_s, acc_s, run_s):
        qi = pl.program_id(0)

        @pl.when(qi == 0)
        def _():
            kb_s[...] = k_ref[...].astype(BF16)
            vb_s[...] = v_ref[...].astype(BF16)

        _, tr, tc, heads = _attn_tiles()
        upper = (tr > tc).astype(BF16)
        same_head = ((tr // DH) == (tc // DH)).astype(BF16)
        for g in range(NG):
            qs_s[g] = _stack_heads(q_ref[:, g * GW:(g + 1) * GW] * 0.125, heads).astype(BF16)
        acc_s[...] = jnp.zeros_like(acc_s)
        run_s[...] = jnp.zeros_like(run_s)

        def block(kb, masked):
            k0 = pl.multiple_of(kb * AQ, AQ)
            for g in range(NG):
                lanes = pl.ds(g * GW, GW)
                z_s[g] = _dot_nt(qs_s[g], kb_s[pl.ds(k0, AQ), lanes])
                for r in range(SR // RC):
                    rows = pl.ds(r * RC, RC)
                    z = z_s[g, rows, :]
                    zs = jnp.minimum(z, 0.0) - jnp.log(1.0 + jnp.exp(-jnp.abs(z)))
                    lm = zs - z
                    if masked:
                        lm = jnp.where(_chunk_causal(r), lm, 0.0)
                    run = run_s[g, rows, :]
                    zs_s[g, rows, :] = zs + run[:, 0:1]
                    hi = lm.astype(BF16)
                    lmb_s[g, rows, :] = hi
                    lmb_s[g, pl.ds(SR + r * RC, RC), :] = (lm - hi.astype(F32)).astype(BF16)
                    run_s[g, rows, :] = run + jnp.sum(lm, axis=1, keepdims=True)
                suf_s[g] = _dot(lmb_s[g], upper)
                for r in range(SR // RC):
                    rows = pl.ds(r * RC, RC)
                    att = jnp.exp(zs_s[g, rows, :] + suf_s[g, rows, :] + suf_s[g, pl.ds(SR + r * RC, RC), :])
                    if masked:
                        att = jnp.where(_chunk_causal(r), att, 0.0)
                    att_s[g, rows, :] = att.astype(BF16)
                acc_s[g] += _dot(att_s[g], vb_s[pl.ds(k0, AQ), lanes])

        block(qi, True)

        def step(i, c):
            block(qi - 1 - i, False)
            return c

        lax.fori_loop(0, qi, step, 0)
        gate = g_ref[...]
        agv = ag_ref[...]
        for g in range(NG):
            lanes = slice(g * GW, (g + 1) * GW)
            o = _unstack_heads(acc_s[g], heads)
            osq = o * o
            ms = _dot2(osq, same_head)
            gg = gate[:, lanes]
            o_ref[:, lanes] = o
            y_ref[:, lanes] = (o * lax.rsqrt(ms * (1.0 / DH) + EPS) * agv[:, lanes] * (gg * _sig(gg))).astype(BF16)
            tot_ref[g] = _unstack_heads(jnp.broadcast_to(run_s[g][:, 0:1], (SR, GW)), heads)

    tile = lambda dt, rows=SR: pltpu.VMEM((NG, rows, AQ), dt)
    scratch = [pltpu.VMEM((T, AD), BF16), pltpu.VMEM((T, AD), BF16), pltpu.VMEM((NG, SR, GW), BF16),
               tile(F32), tile(F32), tile(BF16, 2 * SR), tile(F32, 2 * SR), tile(BF16),
               pltpu.VMEM((NG, SR, GW), F32), pltpu.VMEM((NG, SR, LANE), F32)]
    col = lambda j: pl.BlockSpec((AQ, AD), lambda qi: (qi, j))
    res = pl.pallas_call(
        _host(body, grid, 5, 3, n, _gather_tree_copies, mids=((grid[0] * 5 // 8,), (grid[0] * 7 // 8,))), name="attn_fwd", grid=grid,
        in_specs=[col(0), pl.BlockSpec((T, AD), lambda qi: (0, 1)), pl.BlockSpec((T, AD), lambda qi: (0, 2)),
                  col(3), pl.BlockSpec((1, AD), lambda qi: (0, 0))] + [HBM_SPEC] * n,
        out_specs=[col(0), col(0), pl.BlockSpec((NG, AQ, GW), lambda qi: (0, qi, 0))] + [HBM_SPEC] * n,
        out_shape=[jax.ShapeDtypeStruct((T, AD), F32), jax.ShapeDtypeStruct((T, AD), BF16),
                   jax.ShapeDtypeStruct((NG, T, GW), F32)]
        + [jax.ShapeDtypeStruct((NCHIP,) + s.shape, s.dtype) for s in shards],
        scratch_shapes=scratch + _hosted_sems(n, GATHER_SEMS),
        compiler_params=_cp(("arbitrary",), VMEM_BIG),
    )(u, u, u, u, agw, *shards)
    return res[0], res[1], res[2], list(res[3:])


def _glu_conv(u, dw, db):
    tr = 256

    def body(cv_ref, cg_ref, w_ref, b_ref, c1_ref, pad_s):
        pad_s[pl.ds(0, CWP), :] = jnp.zeros((CWP, LANE), F32)
        pad_s[pl.ds(CWP, T), :] = cv_ref[...] * _sig(cg_ref[...])
        wv = w_ref[0]
        bias = b_ref[...]

        def tile(i, carry):
            r0 = pl.multiple_of(i * tr, tr)
            acc = jnp.zeros((tr, LANE), F32) + bias
            for w in range(CW):
                acc = acc + pad_s[pl.ds(r0 + (CWP - CW + 1) + w, tr), :] * wv[w:w + 1, :]
            c1_ref[pl.ds(r0, tr), :] = acc
            return carry

        lax.fori_loop(0, T // tr, tile, 0)

    return pl.pallas_call(
        body, name="glu_conv", grid=(CD // LANE,),
        in_specs=[pl.BlockSpec((T, LANE), lambda cb: (0, 16 + cb)),
                  pl.BlockSpec((T, LANE), lambda cb: (0, 20 + cb)),
                  pl.BlockSpec((1, CWP, LANE), lambda cb: (cb, 0, 0)),
                  pl.BlockSpec((1, LANE), lambda cb: (0, cb))],
        out_specs=pl.BlockSpec((T, LANE), lambda cb: (0, cb)),
        out_shape=jax.ShapeDtypeStruct((T, CD), F32),
        scratch_shapes=[pltpu.VMEM((T + CWP, LANE), F32)],
        compiler_params=_cp(("arbitrary",)),
    )(u, u, dw, db)


def _ln_silu(c1, lg, lb):
    mu = jnp.mean(c1, axis=-1, keepdims=True)
    xc = c1 - mu
    rs = lax.rsqrt(jnp.mean(xc * xc, axis=-1, keepdims=True) + EPS)
    xh = xc * rs
    ln = xh * lg + lb
    s = _sig(ln)
    return xh, rs, ln, s


def _layer_tail(c1, u, ya, h, p, lg, lb, wpw, cg, wout, pg, wgate, wple, head=None):
    tm = min(TR, T)

    def body(c1_ref, gc_ref, ya_ref, h_ref, p_ref, lg_ref, lb_ref, wpw_ref, cg_ref, wout_ref,
             pg_ref, wgate_ref, wple_ref, *rest):
        c3_ref, yc_ref, h1_ref, gate_ref, pe_ref, h2_ref = rest[-6 - 2 * bool(head):][:6]
        _, _, ln, s = _ln_silu(c1_ref[...], lg_ref[...], lb_ref[...])
        c2 = (ln * s).astype(BF16)
        c3 = _dot(c2, wpw_ref[...])
        gc = gc_ref[...]
        yc = (c3 * _rstd(c3) * cg_ref[...] * (gc * _sig(gc))).astype(BF16)
        c3_ref[...] = c3
        yc_ref[...] = yc
        y = _dot(ya_ref[...], wout_ref[pl.ds(0, AD), :]) + _dot(yc, wout_ref[pl.ds(AD, CD), :])
        h1 = h_ref[...] + y
        hn2 = (h1 * _rstd(h1) * pg_ref[...]).astype(BF16)
        gate = _sig(_dot(hn2, wgate_ref[...]))
        pb = p_ref[...].astype(BF16)
        pe = jnp.concatenate([_dot(pb, wple_ref[k]) for k in range(NCHIP)], axis=1)
        h1_ref[...] = h1
        gate_ref[...] = gate.astype(BF16)
        pe_ref[...] = pe.astype(BF16)
        h2 = h1 + pe * gate
        if not head:
            h2_ref[...] = h2
            return
        t_ref, fg_ref, loss_ref, dfg_ref = rest[0], rest[1], rest[-2], rest[-1]

        @pl.when(pl.program_id(0) == 0)
        def _():
            loss_ref[...] = jnp.zeros_like(loss_ref)
            dfg_ref[...] = jnp.zeros_like(dfg_ref)
        fg = fg_ref[...]
        r = _rstd(h2)
        e = h2 * r * fg - t_ref[...]
        loss_ref[...] += 0.5 * jnp.sum(jnp.mean(e * e, axis=-1, keepdims=True))
        dy = e * (1.0 / D)
        dfg_ref[...] += jnp.sum(dy * h2 * r, axis=0, keepdims=True)
        h2_ref[...] = _rms_bwd(dy, h2, r, fg)

    row = lambda w: pl.BlockSpec((tm, w), lambda i: (i, 0))
    full = lambda *s: pl.BlockSpec(s, lambda i: (0,) * len(s), pipeline_mode=pl.Buffered(1))
    extra = bool(head)
    return pl.pallas_call(
        body, name="layer_tail", grid=(T // tm,),
        in_specs=[row(CD), pl.BlockSpec((tm, CD), lambda i: (i, 6)), row(AD), row(D), row(PLE),
                  full(1, CD), full(1, CD), full(CD, CD), full(1, CD), full(D, D),
                  full(1, D), full(D, D), full(NCHIP, PLE, PLE)] + [row(D), full(1, D)] * extra,
        out_specs=[row(CD), row(CD), row(D), row(D), row(D), row(D)] + [full(8, LANE), full(1, D)] * extra,
        out_shape=[jax.ShapeDtypeStruct((T, CD), F32), jax.ShapeDtypeStruct((T, CD), BF16),
                   jax.ShapeDtypeStruct((T, D), F32), jax.ShapeDtypeStruct((T, D), BF16),
                   jax.ShapeDtypeStruct((T, D), BF16), jax.ShapeDtypeStruct((T, D), F32)]
        + [jax.ShapeDtypeStruct((8, LANE), F32), jax.ShapeDtypeStruct((1, D), F32)] * extra,
        compiler_params=_cp(("arbitrary",), VMEM_BIG),
    )(c1, u, ya, h, p, lg, lb, wpw, cg, wout, pg, wgate, wple, *(head or ()))


def _ple_out_bwd(dh2, h1, gate, pe, p, ya, yc, pg, wgate, wout):
    tm = min(TR, T)

    def body(dh2_ref, h1_ref, gate_ref, pe_ref, p_ref, ya_ref, yc_ref, pg_ref, wgate_ref, wout_ref,
             dh1_ref, dy_ref, dwg_ref, dwp_ref, dwo_ref, dpg_ref):
        @pl.when(pl.program_id(0) == 0)
        def _():
            dwg_ref[...] = jnp.zeros_like(dwg_ref)
            dwp_ref[...] = jnp.zeros_like(dwp_ref)
            dwo_ref[...] = jnp.zeros_like(dwo_ref)
            dpg_ref[...] = jnp.zeros_like(dpg_ref)
        dh2 = dh2_ref[...]
        h1 = h1_ref[...]
        gate = gate_ref[...].astype(F32)
        pg = pg_ref[...]
        dpe = (dh2 * gate).astype(BF16)
        dgp = (dh2 * pe_ref[...].astype(F32) * gate * (1.0 - gate)).astype(BF16)
        r = _rstd(h1)
        hn = h1 * r
        dwg_ref[...] += _dot_tn((hn * pg).astype(BF16), dgp)
        dhn2 = _dot_nt(dgp, wgate_ref[...])
        dpg_ref[...] += jnp.sum(dhn2 * hn, axis=0, keepdims=True)
        dh1 = dh2 + _rms_bwd(dhn2, h1, r, pg)
        pb = p_ref[...].astype(BF16)
        for k in range(NCHIP):
            dwp_ref[k] += _dot_tn(pb, dpe[:, k * PLE:(k + 1) * PLE])
        dh1b = dh1.astype(BF16)
        dy_ref[...] = _dot_nt(dh1b, wout_ref[...])
        dwo_ref[pl.ds(0, AD), :] += _dot_tn(ya_ref[...], dh1b)
        dwo_ref[pl.ds(AD, CD), :] += _dot_tn(yc_ref[...], dh1b)
        dh1_ref[...] = dh1

    row = lambda w: pl.BlockSpec((tm, w), lambda i: (i, 0))
    full = lambda *s: pl.BlockSpec(s, lambda i: (0,) * len(s), pipeline_mode=pl.Buffered(1))
    return pl.pallas_call(
        body, name="ple_out_bwd", grid=(T // tm,),
        in_specs=[row(D), row(D), row(D), row(D), row(PLE), row(AD), row(CD),
                  full(1, D), full(D, D), full(D, D)],
        out_specs=[row(D), row(D), full(D, D), full(NCHIP, PLE, PLE), full(D, D), full(1, D)],
        out_shape=[jax.ShapeDtypeStruct((T, D), F32), jax.ShapeDtypeStruct((T, D), F32),
                   jax.ShapeDtypeStruct((D, D), F32), jax.ShapeDtypeStruct((NCHIP, PLE, PLE), F32),
                   jax.ShapeDtypeStruct((D, D), F32), jax.ShapeDtypeStruct((1, D), F32)],
        compiler_params=_cp(("arbitrary",), VMEM_BIG),
    )(dh2, h1, gate, pe, p, ya, yc, pg, wgate, wout)


def _branch_bwd(dy, o, u, c1, c3, ag, lg, lb, wpw, cg, seg):
    tm = min(TR, T)

    def body(dya_ref, dyc_ref, o_ref, ga_ref, gc_ref, c1_ref, c3_ref, ag_ref, lg_ref, lb_ref, wpw_ref,
             cg_ref, seg_ref, do_ref, dga_ref, dgc_ref, dc1_ref, dwpw_ref, dag_ref, dcg_ref, dlg_ref, dlb_ref):
        @pl.when(pl.program_id(0) == 0)
        def _():
            for r_ in (dwpw_ref, dag_ref, dcg_ref, dlg_ref, dlb_ref):
                r_[...] = jnp.zeros_like(r_)
        dya = dya_ref[...]
        o = o_ref[...]
        ga = ga_ref[...]
        ag_v = ag_ref[...]
        seg_m = seg_ref[...]
        r = lax.rsqrt(_dot2(o * o, seg_m) * (1.0 / DH) + EPS)
        onr = o * r
        sg = _sig(ga)
        dga_ref[...] = (dya * (onr * ag_v) * (sg * (1.0 + ga * (1.0 - sg)))).astype(BF16)
        don = dya * (ga * sg)
        dag_ref[...] += jnp.sum(don * onr, axis=0, keepdims=True)
        dn = don * ag_v
        do_ref[...] = r * dn - o * (r * r * r) * (_dot2(dn * o, seg_m) * (1.0 / DH))
        dyc = dyc_ref[...]
        c3 = c3_ref[...]
        gc = gc_ref[...]
        cg_v = cg_ref[...]
        r3 = _rstd(c3)
        cn = c3 * r3
        sc = _sig(gc)
        dgc_ref[...] = (dyc * (cn * cg_v) * (sc * (1.0 + gc * (1.0 - sc)))).astype(BF16)
        dcn = dyc * (gc * sc)
        dcg_ref[...] += jnp.sum(dcn * cn, axis=0, keepdims=True)
        dc3 = _rms_bwd(dcn, c3, r3, cg_v).astype(BF16)
        lg_v = lg_ref[...]
        xh, rs, ln, s = _ln_silu(c1_ref[...], lg_v, lb_ref[...])
        c2 = (ln * s).astype(BF16)
        dwpw_ref[...] += _dot_tn(c2, dc3)
        dc2 = _dot_nt(dc3, wpw_ref[...])
        dln = dc2 * (s * (1.0 + ln * (1.0 - s)))
        dlb_ref[...] += jnp.sum(dln, axis=0, keepdims=True)
        dlg_ref[...] += jnp.sum(dln * xh, axis=0, keepdims=True)
        dxh = dln * lg_v
        dc1_ref[...] = rs * (dxh - jnp.mean(dxh, axis=-1, keepdims=True)
                             - xh * jnp.mean(dxh * xh, axis=-1, keepdims=True))

    half = lambda j: pl.BlockSpec((tm, 512), lambda i: (i, j))
    full = lambda *s: pl.BlockSpec(s, lambda i: (0,) * len(s), pipeline_mode=pl.Buffered(1))
    vec = jax.ShapeDtypeStruct((1, 512), F32)
    act = jax.ShapeDtypeStruct((T, 512), F32)
    return pl.pallas_call(
        body, name="branch_bwd", grid=(T // tm,),
        in_specs=[half(0), half(1), half(0), half(3), half(6), half(0), half(0),
                  full(1, AD), full(1, CD), full(1, CD), full(CD, CD), full(1, CD), full(AD, AD)],
        out_specs=[half(0), half(0), half(0), half(0), full(CD, CD), full(1, 512), full(1, 512),
                   full(1, 512), full(1, 512)],
        out_shape=[act, jax.ShapeDtypeStruct((T, 512), BF16), jax.ShapeDtypeStruct((T, 512), BF16), act,
                   jax.ShapeDtypeStruct((CD, CD), F32), vec, vec, vec, vec],
        compiler_params=_cp(("arbitrary",), VMEM_BIG),
    )(dy, dy, o, u, u, c1, c3, ag, lg, lb, wpw, cg, seg)


def _conv_bwd(dc1, u, dw, grads=()):
    tr = 64
    n_x = len(grads)
    grid = (CD // LANE,)
    off = CWP - CW + 1

    def body(d_ref, cv_ref, cg_ref, w_ref, dcv_ref, dcg_ref, ddw_ref, ddb_ref, padc_s, padd_s, acc_s):
        cv = cv_ref[...]
        sg = _sig(cg_ref[...])
        padc_s[pl.ds(0, CWP), :] = jnp.zeros((CWP, LANE), F32)
        padc_s[pl.ds(CWP, T), :] = cv * sg
        padd_s[pl.ds(0, T), :] = d_ref[...]
        padd_s[pl.ds(T, CWP), :] = jnp.zeros((CWP, LANE), F32)
        acc_s[...] = jnp.zeros_like(acc_s)
        wv = w_ref[0]

        def tile(i, carry):
            r0 = pl.multiple_of(i * tr, tr)
            dt = padd_s[pl.ds(r0, tr), :]
            dc0 = jnp.zeros((tr, LANE), F32)
            for w in range(CW):
                dc0 = dc0 + padd_s[pl.ds(r0 + (CW - 1) - w, tr), :] * wv[w:w + 1, :]
                prod = dt * padc_s[pl.ds(r0 + off + w, tr), :]
                acc_s[w] += jnp.sum(prod.reshape(tr // 8, 8, LANE), axis=0)
            cvt = cv_ref[pl.ds(r0, tr), :]
            sgt = _sig(cg_ref[pl.ds(r0, tr), :])
            dcv_ref[pl.ds(r0, tr), :] = (dc0 * sgt).astype(BF16)
            dcg_ref[pl.ds(r0, tr), :] = (dc0 * cvt * sgt * (1.0 - sgt)).astype(BF16)
            return carry

        lax.fori_loop(0, T // tr, tile, 0)
        ddw_ref[0] = jnp.sum(acc_s[...], axis=1)
        ddb_ref[...] = jnp.sum(d_ref[...], axis=0, keepdims=True)

    col = lambda j: pl.BlockSpec((T, LANE), lambda cb: (0, j + cb))
    res = pl.pallas_call(
        _host(body, grid, 4, 4, n_x, _pair_copies), name="conv_bwd", grid=grid,
        in_specs=[col(0), col(16), col(20), pl.BlockSpec((1, CWP, LANE), lambda cb: (cb, 0, 0))] + [HBM_SPEC] * n_x,
        out_specs=[col(0), col(0), pl.BlockSpec((1, CWP, LANE), lambda cb: (cb, 0, 0)),
                   pl.BlockSpec((1, LANE), lambda cb: (0, cb))] + [HBM_SPEC] * n_x,
        out_shape=[jax.ShapeDtypeStruct((T, CD), BF16), jax.ShapeDtypeStruct((T, CD), BF16),
                   jax.ShapeDtypeStruct((NCHIP, CWP, LANE), F32), jax.ShapeDtypeStruct((1, CD), F32)]
        + [jax.ShapeDtypeStruct((NCHIP, g.shape[1] // 2, g.shape[2]), F32) for g in grads],
        scratch_shapes=[pltpu.VMEM((T + CWP, LANE), F32), pltpu.VMEM((T + CWP, LANE), F32),
                        pltpu.VMEM((CWP, 8, LANE), F32)] + _hosted_sems(n_x),
        compiler_params=_cp(("arbitrary",)),
    )(dc1, u, u, dw, *grads)
    return res[0], res[1], res[2], res[3], list(res[4:])


def _attn_bwd(u, do, tot, partials=()):
    n_x = len(partials)
    grid = (T // AQ,)

    def body(q_ref, k_ref, v_ref, do_ref, tot_ref, dq_ref, dk_ref, dv_ref, kb_s, vb_s, dk_s, dv_s):
        qi = pl.program_id(0)

        @pl.when(qi == 0)
        def _():
            kb_s[...] = k_ref[...].astype(BF16)
            vb_s[...] = v_ref[...].astype(BF16)
            dk_s[...] = jnp.zeros_like(dk_s)
            dv_s[...] = jnp.zeros_like(dv_s)

        causal, tr, tc, heads = _attn_tiles()
        upper = (tr > tc).astype(BF16)
        lower = (tr < tc).astype(BF16)
        qs, qus, dos, tots = [], [], [], []
        for g in range(NG):
            lanes = slice(g * GW, (g + 1) * GW)
            q = q_ref[:, lanes]
            qs.append(_stack_heads(q * 0.125, heads).astype(BF16))
            qus.append(_stack_heads(q, heads).astype(BF16))
            dos.append(_stack_heads(do_ref[:, lanes], heads).astype(BF16))
            totv = tot_ref[g]
            tots.append(jnp.concatenate([totv[:, h * DH:h * DH + 1] for h in range(HG)], axis=0))

        def block(kb, carry, masked):
            k0 = pl.multiple_of(kb * AQ, AQ)
            out = []
            for g in range(NG):
                lanes = pl.ds(g * GW, GW)
                lm_left, dl_left, dq = carry[g]
                kk = kb_s[pl.ds(k0, AQ), lanes]
                vv = vb_s[pl.ds(k0, AQ), lanes]
                z = _dot_nt(qs[g], kk)
                sp = _softplus(z)
                lm = jnp.where(causal, -sp, 0.0) if masked else -sp
                lm_incl = lm_left + jnp.sum(lm, axis=1, keepdims=True)
                att = jnp.exp((z - sp) + _tri_sum(lm, upper) + (tots[g] - lm_incl))
                if masked:
                    att = jnp.where(causal, att, 0.0)
                dl = att * _dot_nt(dos[g], vv)
                dv_s[pl.ds(k0, AQ), lanes] += _dot_tn(att.astype(BF16), dos[g])
                prefix = dl_left + _tri_sum(dl, lower)
                beta = jnp.exp(z - sp)
                dz = (1.0 - beta) * dl - beta * prefix
                if masked:
                    dz = jnp.where(causal, dz, 0.0)
                dzs = (dz * 0.125).astype(BF16)
                dk_s[pl.ds(k0, AQ), lanes] += _dot_tn(dzs, qus[g])
                out.append((lm_incl, dl_left + jnp.sum(dl, axis=1, keepdims=True), dq + _dot(dzs, kk)))
            return tuple(out)

        zero = jnp.zeros((SR, 1), F32)
        init = tuple((zero, zero, jnp.zeros((SR, GW), F32)) for _ in range(NG))
        carry = lax.fori_loop(0, qi, lambda kb, c: block(kb, c, False), init)
        carry = block(qi, carry, True)
        for g in range(NG):
            dq_ref[:, g * GW:(g + 1) * GW] = _unstack_heads(carry[g][2], heads).astype(BF16)

        @pl.when(qi == grid[0] - 1)
        def _():
            dk_ref[...] = dk_s[...].astype(BF16)
            dv_ref[...] = dv_s[...].astype(BF16)

    col = lambda j: pl.BlockSpec((AQ, AD), lambda qi: (qi, j))
    whole = lambda j: pl.BlockSpec((T, AD), lambda qi: (0, j), pipeline_mode=pl.Buffered(1))
    res = pl.pallas_call(
        _host(body, grid, 5, 3, n_x, _scatter_copies), name="attn_bwd", grid=grid,
        in_specs=[col(0), whole(1), whole(2), col(0), pl.BlockSpec((NG, AQ, GW), lambda qi: (0, qi, 0))]
        + [HBM_SPEC] * n_x,
        out_specs=[col(0), whole(0), whole(0)] + [HBM_SPEC] * n_x,
        out_shape=[jax.ShapeDtypeStruct((T, AD), BF16)] * 3
        + [jax.ShapeDtypeStruct((NCHIP - 1,) + a.shape[1:], a.dtype) for a in partials],
        scratch_shapes=[pltpu.VMEM((T, AD), BF16), pltpu.VMEM((T, AD), BF16), pltpu.VMEM((T, AD), F32),
                        pltpu.VMEM((T, AD), F32)] + _hosted_sems(n_x),
        compiler_params=_cp(("arbitrary",), VMEM_BIG),
    )(u, u, u, do, tot, *partials)
    return res[0], res[1], res[2], list(res[3:])


def _inproj_dw(hn, du):
    tm = min(TM, T)

    def body(hn_ref, du_ref, dw_ref):
        @pl.when(pl.program_id(1) == 0)
        def _():
            dw_ref[...] = jnp.zeros_like(dw_ref)
        dw_ref[0] += _dot_tn(hn_ref[...], du_ref[...])

    return pl.pallas_call(
        body, name="inproj_dw", grid=(NCHIP, T // tm),
        in_specs=[pl.BlockSpec((tm, D), lambda k, i: (i, 0)), pl.BlockSpec((tm, SHW), lambda k, i: (i, k))],
        out_specs=pl.BlockSpec((1, D, SHW), lambda k, i: (k, 0, 0)),
        out_shape=jax.ShapeDtypeStruct((NCHIP, D, SHW), F32),
        compiler_params=_cp(("arbitrary", "arbitrary"), VMEM_BIG),
    )(hn, du)


def _inproj_dx(du, w, h, g, dres, partials=(), grads=()):
    tm = min(TM, T)
    sent = list(partials) + list(grads)
    n_x = len(sent)
    grid = (T // tm, NCHIP)
    if grads:
        landing = [jax.ShapeDtypeStruct((NCHIP, a.shape[1] // 2, a.shape[2]), F32) for a in grads]
    else:
        landing = [jax.ShapeDtypeStruct((NCHIP - 1,) + a.shape[1:], a.dtype) for a in partials]

    def body(du_ref, w_ref, h_ref, g_ref, dres_ref, dh_ref, dg_ref, acc_s):
        i, k = pl.program_id(0), pl.program_id(1)

        @pl.when(jnp.logical_and(i == 0, k == 0))
        def _():
            dg_ref[...] = jnp.zeros_like(dg_ref)

        @pl.when(k == 0)
        def _():
            acc_s[...] = _dot_nt(du_ref[...], w_ref[0])

        @pl.when(k > 0)
        def _():
            acc_s[...] += _dot_nt(du_ref[...], w_ref[0])

        @pl.when(k == NCHIP - 1)
        def _():
            hh = h_ref[...]
            r = _rstd(hh)
            dhn = acc_s[...]
            dg_ref[...] += jnp.sum(dhn * hh * r, axis=0, keepdims=True)
            dh_ref[...] = dres_ref[...] + _rms_bwd(dhn, hh, r, g_ref[...])

    res = pl.pallas_call(
        _host(body, grid, 5, 2, n_x, _pair_copies if grads else _scatter_copies), name="inproj_dx", grid=grid,
        in_specs=[pl.BlockSpec((tm, SHW), lambda i, k: (i, k)),
                  pl.BlockSpec((1, D, SHW), lambda i, k: (k, 0, 0)),
                  pl.BlockSpec((tm, D), lambda i, k: (i, 0)),
                  pl.BlockSpec((1, D), lambda i, k: (0, 0)),
                  pl.BlockSpec((tm, D), lambda i, k: (i, 0))] + [HBM_SPEC] * n_x,
        out_specs=[pl.BlockSpec((tm, D), lambda i, k: (i, 0)), pl.BlockSpec((1, D), lambda i, k: (0, 0))]
        + [HBM_SPEC] * n_x,
        out_shape=[jax.ShapeDtypeStruct((T, D), F32), jax.ShapeDtypeStruct((1, D), F32)] + landing,
        scratch_shapes=[pltpu.VMEM((tm, D), F32)] + _hosted_sems(n_x),
        compiler_params=_cp(("arbitrary", "arbitrary"), VMEM_BIG),
    )(du, w, h, g, dres, *sent)
    return res[0], res[1], list(res[2:])


def _sum_pair(core, grads, gots):
    n = len(grads)

    def body(c_ref, *refs):
        for a in range(n):
            refs[2 * n + a][...] = (refs[a][...] + refs[n + a][...]).astype(BF16)

    mine = [pl.BlockSpec((1,) + s.shape[1:], lambda k, c: (k, c[0], 0)) for s in gots]
    same = [pl.BlockSpec((1,) + s.shape[1:], lambda k, c: (k, 0, 0)) for s in gots]
    return pl.pallas_call(
        body, name="sum_pair",
        grid_spec=pltpu.PrefetchScalarGridSpec(
            num_scalar_prefetch=1, grid=(NCHIP,), in_specs=mine + same, out_specs=same),
        out_shape=[jax.ShapeDtypeStruct(s.shape, BF16) for s in gots],
        compiler_params=_cp(("arbitrary",), VMEM_BIG),
    )(core, *grads, *gots)


def _sum_chips_share(chip, partials, gots):
    flat_p = [p for layer in partials for p in layer]
    flat_g = [g for layer in gots for g in layer]
    n, per_layer = len(flat_p), len(partials[0])

    def body(c_ref, *refs):
        full, sums = refs[2 * n:2 * n + per_layer], refs[2 * n + per_layer:3 * n + per_layer]
        send_sems, recv_sems, local_sems = refs[3 * n + per_layer:]
        x, y, c = _place()
        copies = []
        for i in range(n):
            acc = refs[i][0].astype(F32)
            for j in range(NCHIP - 1):
                acc = acc + refs[n + i][j].astype(F32)
            sums[i][...] = acc
            half = flat_p[i].shape[1]
            rows = full[i % per_layer].at[i // per_layer, pl.ds(c * half, half)]
            copies.append(pltpu.make_async_copy(sums[i], rows, local_sems.at[i]))
            copies.append(pltpu.make_async_remote_copy(
                src_ref=sums[i], dst_ref=rows, send_sem=send_sems.at[i], recv_sem=recv_sems.at[i],
                device_id=(x, y, 1 - c), device_id_type=MESH))
        for cp in copies:
            cp.start()
        for cp in copies:
            cp.wait()

    return pl.pallas_call(
        body, name="sum_chips_share",
        grid_spec=pltpu.PrefetchScalarGridSpec(
            num_scalar_prefetch=1, grid=(1,),
            in_specs=[pl.BlockSpec((1,) + s.shape[1:], lambda i, c: (c[0], 0, 0)) for s in flat_p]
            + [pl.BlockSpec(s.shape, lambda i, c: (0, 0, 0)) for s in flat_g],
            out_specs=[HBM_SPEC] * per_layer,
            scratch_shapes=[pltpu.VMEM(s.shape[1:], F32) for s in flat_p]
            + [pltpu.SemaphoreType.DMA((n,)), pltpu.SemaphoreType.DMA((n,)), pltpu.SemaphoreType.DMA((n,))]),
        out_shape=[jax.ShapeDtypeStruct((len(partials), 2 * s.shape[1], s.shape[2]), F32) for s in partials[0]],
        compiler_params=_cp(("arbitrary",), VMEM_BIG),
    )(chip, *flat_p, *flat_g)


def _adam_math(w, g, m, v):
    nm = ADAM_B1 * m + (1.0 - ADAM_B1) * g
    nv = ADAM_B2 * v + (1.0 - ADAM_B2) * (g * g)
    m_hat = nm / (1.0 - ADAM_B1 ** ADAM_STEP)
    v_hat = nv / (1.0 - ADAM_B2 ** ADAM_STEP)
    return -ADAM_LR * (m_hat / (jnp.sqrt(v_hat) + ADAM_EPS) + ADAM_WD * w), nm, nv


def _adamw(w, g, m, v, rows):
    R, C = w.shape

    def body(w_ref, g_ref, m_ref, v_ref, d_ref, nm_ref, nv_ref):
        d_ref[...], nm_ref[...], nv_ref[...] = _adam_math(w_ref[...], g_ref[...], m_ref[...], v_ref[...])

    spec = pl.BlockSpec((rows, C), lambda i: (i, 0))
    sh = jax.ShapeDtypeStruct((R, C), F32)
    return pl.pallas_call(
        body, name="adamw", grid=(R // rows,), in_specs=[spec] * 4, out_specs=[spec] * 3,
        out_shape=[sh, sh, sh], compiler_params=_cp(("arbitrary",)),
    )(w, g, m, v)


def _small_adamw(tot, ws, ms, vs):
    n = len(ws)

    def body(*refs):
        tot_ref = refs[0]
        w_refs, m_refs, v_refs = refs[1:1 + n], refs[1 + n:1 + 2 * n], refs[1 + 2 * n:1 + 3 * n]
        outs = refs[1 + 3 * n:]
        for i in range(n):
            rows, width = ws[i].shape
            g = tot_ref[pl.ds(SMALL_ROW[i], rows), pl.ds(0, width)]
            outs[4 * i][...] = g
            outs[4 * i + 1][...], outs[4 * i + 2][...], outs[4 * i + 3][...] = _adam_math(
                w_refs[i][...], g, m_refs[i][...], v_refs[i][...])
        outs[4 * n][...] = tot_ref[pl.ds(LOSS_ROW, 1), pl.ds(0, LANE)]

    vmem = pl.BlockSpec(memory_space=pltpu.VMEM)
    res = pl.pallas_call(
        body, name="small_adamw", in_specs=[vmem] * (1 + 3 * n), out_specs=[vmem] * (4 * n + 1),
        out_shape=[jax.ShapeDtypeStruct(w.shape, F32) for w in ws for _ in range(4)]
        + [jax.ShapeDtypeStruct((1, LANE), F32)],
    )(tot, *ws, *ms, *vs)
    return [res[4 * i:4 * i + 4] for i in range(n)], res[4 * n]


HBM_SPEC = pl.BlockSpec(memory_space=pltpu.HBM)


def _place():
    return lax.axis_index("x"), lax.axis_index("y"), lax.axis_index("c")


def _all_gather_split(shard):
    def body(in_ref, out_ref, send_sems, recv_sems):
        direct, relayed, passed = _gather_tree_copies([in_ref], [out_ref], send_sems, recv_sems)
        for cp in direct:
            cp.start()
        for i in range(2):
            direct[i].wait_recv()
            relayed[i].start()
            passed[i].start()
        for cp in relayed:
            cp.wait_recv()
        passed[2].start()
        for cp in passed:
            cp.wait_recv()
        for cp in direct + relayed + passed:
            cp.wait_send()

    return pl.pallas_call(
        body, name="all_gather_split", in_specs=[HBM_SPEC], out_specs=HBM_SPEC,
        out_shape=jax.ShapeDtypeStruct((NCHIP,) + shard.shape, shard.dtype),
        scratch_shapes=_hosted_sems(1, GATHER_SEMS),
    )(shard)


def _pair_exchange(grads):
    n = len(grads)

    def body(*refs):
        copies = _pair_copies(refs[:n], refs[n:2 * n], refs[2 * n], refs[2 * n + 1])
        for cp in copies:
            cp.start()
        for cp in copies:
            cp.wait()

    return pl.pallas_call(
        body, name="pair_exchange", in_specs=[HBM_SPEC] * n, out_specs=[HBM_SPEC] * n,
        out_shape=[jax.ShapeDtypeStruct((NCHIP, g.shape[1] // 2, g.shape[2]), F32) for g in grads],
        scratch_shapes=[pltpu.SemaphoreType.DMA((n,)), pltpu.SemaphoreType.DMA((n,))],
    )(*grads)


def _small_allreduce(rows, loss_blk):
    n = len(rows)

    def body(*refs):
        loss_ref, o_ref, pk, slots, send_sems, recv_sems = refs[n:]
        pk[...] = jnp.zeros_like(pk)
        for i in range(n):
            pk[pl.ds(i, 1), pl.ds(0, rows[i].shape[1])] = refs[i][...]
        pk[pl.ds(LOSS_ROW, 1), pl.ds(0, LANE)] = loss_ref[pl.ds(0, 1), :]
        x, y, c = _place()
        me = 4 * x + 2 * y + c
        slots[me] = pk[...]
        copies = []
        for r in range(1, 8):
            rx, ry, rc = (r >> 2) & 1, (r >> 1) & 1, r & 1
            peer = (x + rx - 2 * x * rx, y + ry - 2 * y * ry, c + rc - 2 * c * rc)
            cp = pltpu.make_async_remote_copy(
                src_ref=pk, dst_ref=slots.at[me], send_sem=send_sems.at[r - 1], recv_sem=recv_sems.at[r - 1],
                device_id=peer, device_id_type=MESH)
            cp.start()
            copies.append(cp)
        for cp in copies:
            cp.wait()
        acc = slots[0]
        for j in range(1, 8):
            acc = acc + slots[j]
        o_ref[...] = acc

    vmem = pl.BlockSpec(memory_space=pltpu.VMEM)
    return pl.pallas_call(
        body, name="small_allreduce", in_specs=[vmem] * (n + 1), out_specs=vmem,
        out_shape=jax.ShapeDtypeStruct((SMALL_PK, D), F32),
        scratch_shapes=[pltpu.VMEM((SMALL_PK, D), F32), pltpu.VMEM((8, SMALL_PK, D), F32),
                        pltpu.SemaphoreType.DMA((7,)), pltpu.SemaphoreType.DMA((7,))],
    )(*rows, loss_blk)


def _seg_matrix():
    i = lax.broadcasted_iota(jnp.int32, (AD, AD), 0) // DH
    j = lax.broadcasted_iota(jnp.int32, (AD, AD), 1) // DH
    return (i == j).astype(BF16)


TAIL = ("w_out", "w_ple_gate", "w_ple", "w_pw", "dw_w")


def _local_step(x, p, tgt, sm, shards, chip, ci):
    seg = _seg_matrix()
    core = jnp.reshape(ci, (1,)).astype(jnp.int32)
    chip_idx = jnp.reshape(chip, (1,)).astype(jnp.int32)
    own = lambda g, s: lax.dynamic_update_index_in_dim(g, s, chip, 0)
    w_in_next = own(_all_gather_split(shards[0]["w_in"]), shards[0]["w_in"])
    h = x
    saved = []
    for l in range(DEPTH):
        w_in = w_in_next
        row = lambda name: sm[name][l:l + 1]
        u, hn = _rms_inproj(h, row("norm_g"), w_in)
        todo = [shards[l][k] for k in TAIL] + ([shards[l + 1]["w_in"]] if l + 1 < DEPTH else [])
        o, ya, tot, got = _attn_fwd(u, jnp.tile(row("attn_out_g"), (1, AD // DH)), todo)
        got = [own(g, s) for g, s in zip(got, todo)]
        w_out = got[0].reshape(D, D)
        w_gate = got[1].reshape(D, D)
        w_ple = got[2]
        w_pw = got[3].reshape(CD, CD)
        dw = got[4]
        if l + 1 < DEPTH:
            w_in_next = got[5]
        c1 = _glu_conv(u, dw, row("dw_b"))
        c3, yc, h1, gate, pe, h2, *at_end = _layer_tail(
            c1, u, ya, h, p[l], row("conv_ln_g"), row("conv_ln_b"), w_pw, row("conv_out_g"), w_out,
            row("ple_norm_g"), w_gate, w_ple, head=(tgt, sm["final_g"]) if l == DEPTH - 1 else None)
        saved.append(dict(h=h, u=u, hn=hn, o=o, ya=ya, tot=tot, c1=c1, c3=c3, yc=yc, h1=h1, gate=gate, pe=pe,
                          w_in=w_in, w_out=w_out, w_gate=w_gate, w_pw=w_pw, dw=dw))
        h = h2
    dh, (loss_blk, dfg) = h, at_end
    small = [None] * DEPTH
    pending, partials, arrived = [], {}, {}
    pair_sum = lambda grads: _sum_pair(core, grads, _pair_exchange(grads))
    for l in reversed(range(DEPTH)):
        s = saved[l]
        row = lambda name: sm[name][l:l + 1]
        dh1, dy, dwg, dwp, dwo, dpg = _ple_out_bwd(
            dh, s["h1"], s["gate"], s["pe"], p[l], s["ya"], s["yc"], row("ple_norm_g"), s["w_gate"], s["w_out"])
        ag_t = jnp.tile(row("attn_out_g"), (1, AD // DH))
        do, dga, dgc, dc1, dwpw, dag, dcg, dlg, dlb = _branch_bwd(
            dy, s["o"], s["u"], s["c1"], s["c3"], ag_t, row("conv_ln_g"), row("conv_ln_b"), s["w_pw"],
            row("conv_out_g"), seg)
        tail = [dwo.reshape(NCHIP, 256, D), dwg.reshape(NCHIP, 256, D), dwp, dwpw.reshape(NCHIP, 128, CD)]
        dcv, dcgate, ddw, ddb, halves = _conv_bwd(dc1, s["u"], s["dw"], tail if l == 0 else ())
        tail.append(ddw)
        if l == 0:
            partials[(l, "tail")] = _sum_pair(core, tail, halves + list(_pair_exchange([ddw])))
            pending.append((l, "tail"))
        send = [t for key in pending for t in partials[key]]
        dq, dk, dv, got = _attn_bwd(s["u"], do, s["tot"], send)
        for key in pending:
            arrived[key], got = got[:len(partials[key])], got[len(partials[key]):]
        pending = []
        du = jnp.concatenate([dq, dk, dv, dga, dcv, dcgate, dgc], axis=1)
        dwin = _inproj_dw(s["hn"], du)
        if l == 0:
            partials[(l, "w_in")] = pair_sum([dwin])
            dh, dng, arrived[(l, "w_in")] = _inproj_dx(du, s["w_in"], s["h"], row("norm_g"), dh1,
                                                      partials[(l, "w_in")])
        else:
            dh, dng, halves = _inproj_dx(du, s["w_in"], s["h"], row("norm_g"), dh1, grads=tail + [dwin])
            tail_p = _sum_pair(core, tail + [dwin], halves)
            partials[(l, "tail")], partials[(l, "w_in")] = tail_p[:-1], tail_p[-1:]
            pending = [(l, "tail"), (l, "w_in")]
        small[l] = dict(norm_g=dng, attn_out_g=dag.reshape(AD // DH, DH).sum(axis=0, keepdims=True), dw_b=ddb,
                        conv_ln_g=dlg, conv_ln_b=dlb, conv_out_g=dcg, ple_norm_g=dpg)
    both = lambda d: [d[(l, "w_in")] + d[(l, "tail")] for l in range(DEPTH)]
    big = dict(zip(BIG, _sum_chips_share(chip_idx, both(partials), both(arrived))))
    return loss_blk, dh, big, small, dfg


BIG = ("w_in", "w_out", "w_ple_gate", "w_ple", "w_pw", "dw_w")
SMALL2 = ("norm_g", "ple_norm_g", "dw_b", "conv_ln_g", "conv_ln_b", "conv_out_g", "attn_out_g")
SMALL_ROW = (0, 2, 4, 6, 8, 10, 12, 14)


def kernel(x, p, norm_g, w_in, attn_out_g, dw_w, dw_b, conv_ln_g, conv_ln_b, w_pw, conv_out_g, w_out, ple_norm_g, w_ple_gate, w_ple, final_g, loss_target, m_norm_g, m_w_in, m_attn_out_g, m_dw_w, m_dw_b, m_conv_ln_g, m_conv_ln_b, m_w_pw, m_conv_out_g, m_w_out, m_ple_norm_g, m_w_ple_gate, m_w_ple, m_final_g, v_norm_g, v_w_in, v_attn_out_g, v_dw_w, v_dw_b, v_conv_ln_g, v_conv_ln_b, v_w_pw, v_conv_out_g, v_w_out, v_ple_norm_g, v_w_ple_gate, v_w_ple, v_final_g):
    W = dict(norm_g=norm_g, w_in=w_in, attn_out_g=attn_out_g, dw_w=dw_w, dw_b=dw_b, conv_ln_g=conv_ln_g,
             conv_ln_b=conv_ln_b, w_pw=w_pw, conv_out_g=conv_out_g, w_out=w_out, ple_norm_g=ple_norm_g,
             w_ple_gate=w_ple_gate, w_ple=w_ple, final_g=final_g)
    M = dict(norm_g=m_norm_g, w_in=m_w_in, attn_out_g=m_attn_out_g, dw_w=m_dw_w, dw_b=m_dw_b,
             conv_ln_g=m_conv_ln_g, conv_ln_b=m_conv_ln_b, w_pw=m_w_pw, conv_out_g=m_conv_out_g, w_out=m_w_out,
             ple_norm_g=m_ple_norm_g, w_ple_gate=m_w_ple_gate, w_ple=m_w_ple, final_g=m_final_g)
    V = dict(norm_g=v_norm_g, w_in=v_w_in, attn_out_g=v_attn_out_g, dw_w=v_dw_w, dw_b=v_dw_b,
             conv_ln_g=v_conv_ln_g, conv_ln_b=v_conv_ln_b, w_pw=v_w_pw, conv_out_g=v_conv_out_g, w_out=v_w_out,
             ple_norm_g=v_ple_norm_g, w_ple_gate=v_w_ple_gate, w_ple=v_w_ple, final_g=v_final_g)
    order = ("norm_g", "w_in", "attn_out_g", "dw_w", "dw_b", "conv_ln_g", "conv_ln_b", "w_pw", "conv_out_g",
             "w_out", "ple_norm_g", "w_ple_gate", "w_ple", "final_g")

    pad_taps = lambda a: jnp.pad(a, ((0, 0), (0, CWP - CW), (0, 0)))
    cast = dict(w_in=w_in.astype(BF16), w_out=w_out.astype(BF16), w_ple_gate=w_ple_gate.astype(BF16),
                w_ple=w_ple.astype(BF16), w_pw=w_pw.astype(BF16), dw_w=pad_taps(dw_w))
    shards = [{k: v[l] for k, v in cast.items()} for l in range(DEPTH)]
    xi, yi, ci = lax.axis_index("x"), lax.axis_index("y"), lax.axis_index("c")
    chip = 2 * xi + yi

    sm = {k: W[k] for k in SMALL2}
    sm["final_g"] = final_g.reshape(1, D)
    loss_part, grad_x, big, small, dfg = _local_step(x[0], p[:, 0], loss_target[0], sm, shards, chip, ci)
    g_big = {name: big[name].reshape(cast[name].shape) for name in BIG}

    rows = [small[l][k] for k in SMALL2 for l in range(DEPTH)] + [dfg]
    small_names = SMALL2 + ("final_g",)
    as_rows = lambda t: t.reshape(1, D) if t.ndim == 1 else t
    results, loss_row = _small_adamw(
        _small_allreduce(rows, loss_part), [as_rows(W[k]) for k in small_names],
        [as_rows(M[k]) for k in small_names], [as_rows(V[k]) for k in small_names])
    loss = loss_row[0, 0]

    grads, deltas, new_m, new_v = {}, {}, {}, {}
    for name in BIG:
        wv = pad_taps(W[name]) if name == "dw_w" else W[name]
        mv = pad_taps(M[name]) if name == "dw_w" else M[name]
        vv = pad_taps(V[name]) if name == "dw_w" else V[name]
        gg = g_big[name]
        cols = wv.shape[-1]
        rows_total = wv.size // cols
        tile_rows = min(rows_total, 256)
        d2, m2, v2 = _adamw(wv.reshape(rows_total, cols), gg.reshape(rows_total, cols),
                            mv.reshape(rows_total, cols), vv.reshape(rows_total, cols), tile_rows)
        if name == "dw_w":
            cut = lambda a: a.reshape(DEPTH, CWP, LANE)[:, :CW]
            grads[name], deltas[name], new_m[name], new_v[name] = cut(gg), cut(d2), cut(m2), cut(v2)
        else:
            grads[name] = gg
            deltas[name], new_m[name], new_v[name] = (t.reshape(wv.shape) for t in (d2, m2, v2))
    for k, four in zip(small_names, results):
        grads[k], deltas[k], new_m[k], new_v[k] = (t.reshape(W[k].shape) for t in four)

    return (loss, grad_x[None], *[grads[n] for n in order], *[deltas[n] for n in order],
            *[new_m[n] for n in order], *[new_v[n] for n in order])
```

```python
import functools

import jax
import jax.numpy as jnp
from jax import lax
from jax.experimental import pallas as pl
from jax.experimental.pallas import tpu as pltpu

F32 = jnp.float32
BF16 = jnp.bfloat16

T = 2048
D = 1024
DIN = 3584
NCHIP = 4
SHW = DIN // NCHIP
AD = 512
CD = 512
DH = 64
CW = 31
CWP = 32
PLE = 256
DEPTH = 2
EPS = 1e-6
AQ = 256
HG = 4
GW = HG * DH
SR = HG * AQ
NG = AD // GW
LANE = 128
TM = 1024
TR = 256

ADAM_LR = 0.001
ADAM_B1 = 0.9
ADAM_B2 = 0.999
ADAM_EPS = 1e-08
ADAM_WD = 0.01
ADAM_STEP = 10

SMALL_PK = 16
LOSS_ROW = 15

VMEM_BIG = 56 * 1024 * 1024
MESH = pl.DeviceIdType.MESH


def _cp(sem=None, vmem=None):
    kw = {}
    if sem is not None:
        kw["dimension_semantics"] = sem
    if vmem is not None:
        kw["vmem_limit_bytes"] = vmem
    return pltpu.CompilerParams(**kw)


def _dot(a, b):
    return jnp.dot(a, b, preferred_element_type=F32)


def _dot_nt(a, b):
    return lax.dot_general(a, b, (((1,), (1,)), ((), ())), preferred_element_type=F32)


def _dot_tn(a, b):
    return lax.dot_general(a, b, (((0,), (0,)), ((), ())), preferred_element_type=F32)


def _dot2(x, m):
    hi = x.astype(BF16)
    lo = (x - hi.astype(F32)).astype(BF16)
    return _dot(hi, m) + _dot(lo, m)


def _sig(x):
    return 1.0 / (1.0 + jnp.exp(-x))


def _softplus(z):
    return jnp.maximum(z, 0.0) + jnp.log(1.0 + jnp.exp(-jnp.abs(z)))


def _rstd(x):
    return lax.rsqrt(jnp.mean(x * x, axis=-1, keepdims=True) + EPS)


def _rms_bwd(dy, x, r, g):
    dn = dy * g
    return r * dn - x * (r * r * r) * jnp.mean(dn * x, axis=-1, keepdims=True)


def _rms_inproj(h, g, w):
    tm = min(TM, T)

    def body(h_ref, g_ref, w_ref, u_ref, hn_ref, hn_s):
        @pl.when(pl.program_id(1) == 0)
        def _():
            hh = h_ref[...]
            hn = (hh * _rstd(hh) * g_ref[...]).astype(BF16)
            hn_s[...] = hn
            hn_ref[...] = hn
        u_ref[...] = _dot(hn_s[...], w_ref[0])

    return pl.pallas_call(
        body, name="rms_inproj", grid=(T // tm, NCHIP),
        in_specs=[pl.BlockSpec((tm, D), lambda i, k: (i, 0)),
                  pl.BlockSpec((1, D), lambda i, k: (0, 0)),
                  pl.BlockSpec((1, D, SHW), lambda i, k: (k, 0, 0))],
        out_specs=[pl.BlockSpec((tm, SHW), lambda i, k: (i, k)),
                   pl.BlockSpec((tm, D), lambda i, k: (i, 0))],
        out_shape=[jax.ShapeDtypeStruct((T, DIN), F32), jax.ShapeDtypeStruct((T, D), BF16)],
        scratch_shapes=[pltpu.VMEM((tm, D), BF16)],
        compiler_params=_cp(("arbitrary", "arbitrary"), VMEM_BIG),
    )(h, g, w)


def _attn_tiles():
    row = lax.broadcasted_iota(jnp.int32, (SR, AQ), 0) & (AQ - 1)
    col = lax.broadcasted_iota(jnp.int32, (SR, AQ), 1)
    tr = lax.broadcasted_iota(jnp.int32, (AQ, AQ), 0)
    tc = lax.broadcasted_iota(jnp.int32, (AQ, AQ), 1)
    lane_head = lax.broadcasted_iota(jnp.int32, (1, GW), 1) // DH
    return col < row, tr, tc, [lane_head == h for h in range(HG)]


def _stack_heads(t, heads):
    return jnp.concatenate([jnp.where(m, t, 0.0) for m in heads], axis=0)


def _unstack_heads(t, heads):
    out = t[:AQ]
    for h in range(1, HG):
        out = jnp.where(heads[h], t[h * AQ:(h + 1) * AQ], out)
    return out


def _tri_sum(x, tri):
    hi = x.astype(BF16)
    lo = (x - hi.astype(F32)).astype(BF16)
    both = _dot(jnp.concatenate([hi, lo], axis=0), tri)
    return both[:SR] + both[SR:]


def _scatter_copies(ps, gots, send_sems, recv_sems):
    x, y, c = _place()
    peers = [(1 - x, y), (x, 1 - y), (1 - x, 1 - y)]
    return [pltpu.make_async_remote_copy(
        src_ref=ps[a].at[2 * px + py], dst_ref=gots[a].at[r], send_sem=send_sems.at[3 * a + r],
        recv_sem=recv_sems.at[3 * a + r], device_id=(px, py, c), device_id_type=MESH)
        for a in range(len(ps)) for r, (px, py) in enumerate(peers)]


GATHER_SEMS = 7


def _gather_tree_copies(ins, outs, send_sems, recv_sems):
    x, y, c = _place()
    me, xn, yn, dg = 2 * x + y, 2 * (1 - x) + y, 2 * x + (1 - y), 2 * (1 - x) + (1 - y)
    to_x, to_y, sibling = (1 - x, y, c), (x, 1 - y, c), (x, y, 1 - c)
    direct, relayed, passed = [], [], []
    for a in range(len(ins)):
        half = ins[a].shape[0] // 2
        mine = pl.ds(c * half, half)
        first, second = pl.ds(c * half, half // 2), pl.ds(c * half + half // 2, half // 2)

        def copy(i, src, dst, to, k=GATHER_SEMS * a):
            return pltpu.make_async_remote_copy(src_ref=src, dst_ref=dst, send_sem=send_sems.at[k + i],
                                                recv_sem=recv_sems.at[k + i], device_id=to, device_id_type=MESH)

        own, slot = ins[a].at[mine], outs[a].at[me, mine]
        direct += [copy(0, own, slot, to_x), copy(1, own, slot, to_y)]
        relayed += [copy(2, outs[a].at[xn, first], outs[a].at[xn, first], to_y),
                    copy(3, outs[a].at[yn, second], outs[a].at[yn, second], to_x)]
        passed += [copy(4 + i, outs[a].at[j, mine], outs[a].at[j, mine], sibling) for i, j in enumerate((xn, yn, dg))]
    return direct, relayed, passed


def _pair_copies(ins, outs, send_sems, recv_sems):
    x, y, c = _place()
    copies = []
    for a in range(len(ins)):
        half = ins[a].shape[1] // 2
        copies.append(pltpu.make_async_remote_copy(
            src_ref=ins[a].at[:, pl.ds((1 - c) * half, half), :], dst_ref=outs[a], send_sem=send_sems.at[a],
            recv_sem=recv_sems.at[a], device_id=(x, y, 1 - c), device_id_type=MESH))
    return copies


def _host(body, grid, n_in, n_out, n_x, make_copies, mids=()):
    if not n_x:
        return body

    def hosting(*refs):
        a, b = n_in + n_x, n_in + 2 * n_x + n_out
        copies = make_copies(refs[n_in:a], refs[a + n_out:b], refs[-2], refs[-1])
        stages = copies if isinstance(copies, tuple) else (copies,)
        ids = [pl.program_id(d) for d in range(len(grid))]
        at = lambda step: functools.reduce(jnp.logical_and, [i == s for i, s in zip(ids, step)])

        @pl.when(at([0] * len(grid)))
        def _():
            for cp in stages[0]:
                cp.start()

        for before, after, step in zip(stages, stages[1:], mids):
            @pl.when(at(step))
            def _(before=before, after=after):
                for cp in before:
                    cp.wait_recv()
                for cp in after:
                    cp.start()

        body(*refs[:n_in], *refs[a:a + n_out], *refs[b:-2])

        @pl.when(at([g - 1 for g in grid]))
        def _():
            for cp in stages[-1]:
                cp.wait_recv()
            for stage in stages:
                for cp in stage:
                    cp.wait_send()

    return hosting


def _hosted_sems(n_x, per_array=3):
    n = per_array * n_x
    return [pltpu.SemaphoreType.DMA((n,)), pltpu.SemaphoreType.DMA((n,))] if n_x else []


RC = 256


def _chunk_causal(r):
    row = lax.broadcasted_iota(jnp.int32, (RC, AQ), 0) + (r * RC) % AQ
    return lax.broadcasted_iota(jnp.int32, (RC, AQ), 1) < row


def _attn_fwd(u, agw, shards=()):
    n = len(shards)
    grid = (T // AQ,)

    def body(q_ref, k_ref, v_ref, g_ref, ag_ref, o_ref, y_ref, tot_ref,
             kb_s, vb_s, qs_s, z_s, zs_s, lmb_s, suf_s, att_s, acc_s, run_s):
        qi = pl.program_id(0)

        @pl.when(qi == 0)
        def _():
            kb_s[...] = k_ref[...].astype(BF16)
            vb_s[...] = v_ref[...].astype(BF16)

        _, tr, tc, heads = _attn_tiles()
        upper = (tr > tc).astype(BF16)
        same_head = ((tr // DH) == (tc // DH)).astype(BF16)
        for g in range(NG):
            qs_s[g] = _stack_heads(q_ref[:, g * GW:(g + 1) * GW] * 0.125, heads).astype(BF16)
        acc_s[...] = jnp.zeros_like(acc_s)
        run_s[...] = jnp.zeros_like(run_s)

        def block(kb, masked):
            k0 = pl.multiple_of(kb * AQ, AQ)
            for g in range(NG):
                lanes = pl.ds(g * GW, GW)
                z_s[g] = _dot_nt(qs_s[g], kb_s[pl.ds(k0, AQ), lanes])
                for r in range(SR // RC):
                    rows = pl.ds(r * RC, RC)
                    z = z_s[g, rows, :]
                    zs = jnp.minimum(z, 0.0) - jnp.log(1.0 + jnp.exp(-jnp.abs(z)))
                    lm = zs - z
                    if masked:
                        lm = jnp.where(_chunk_causal(r), lm, 0.0)
                    run = run_s[g, rows, :]
                    zs_s[g, rows, :] = zs + run[:, 0:1]
                    hi = lm.astype(BF16)
                    lmb_s[g, rows, :] = hi
                    lmb_s[g, pl.ds(SR + r * RC, RC), :] = (lm - hi.astype(F32)).astype(BF16)
                    run_s[g, rows, :] = run + jnp.sum(lm, axis=1, keepdims=True)
                suf_s[g] = _dot(lmb_s[g], upper)
                for r in range(SR // RC):
                    rows = pl.ds(r * RC, RC)
                    att = jnp.exp(zs_s[g, rows, :] + suf_s[g, rows, :] + suf_s[g, pl.ds(SR + r * RC, RC), :])
                    if masked:
                        att = jnp.where(_chunk_causal(r), att, 0.0)
                    att_s[g, rows, :] = att.astype(BF16)
                acc_s[g] += _dot(att_s[g], vb_s[pl.ds(k0, AQ), lanes])

        block(qi, True)

        def step(i, c):
            block(qi - 1 - i, False)
            return c

        lax.fori_loop(0, qi, step, 0)
        gate = g_ref[...]
        agv = ag_ref[...]
        for g in range(NG):
            lanes = slice(g * GW, (g + 1) * GW)
            o = _unstack_heads(acc_s[g], heads)
            osq = o * o
            ms = _dot2(osq, same_head)
            gg = gate[:, lanes]
            o_ref[:, lanes] = o
            y_ref[:, lanes] = (o * lax.rsqrt(ms * (1.0 / DH) + EPS) * agv[:, lanes] * (gg * _sig(gg))).astype(BF16)
            tot_ref[g] = _unstack_heads(jnp.broadcast_to(run_s[g][:, 0:1], (SR, GW)), heads)

    tile = lambda dt, rows=SR: pltpu.VMEM((NG, rows, AQ), dt)
    scratch = [pltpu.VMEM((T, AD), BF16), pltpu.VMEM((T, AD), BF16), pltpu.VMEM((NG, SR, GW), BF16),
               tile(F32), tile(F32), tile(BF16, 2 * SR), tile(F32, 2 * SR), tile(BF16),
               pltpu.VMEM((NG, SR, GW), F32), pltpu.VMEM((NG, SR, LANE), F32)]
    col = lambda j: pl.BlockSpec((AQ, AD), lambda qi: (qi, j))
    res = pl.pallas_call(
        _host(body, grid, 5, 3, n, _gather_tree_copies, mids=((grid[0] * 5 // 8,), (grid[0] * 7 // 8,))), name="attn_fwd", grid=grid,
        in_specs=[col(0), pl.BlockSpec((T, AD), lambda qi: (0, 1)), pl.BlockSpec((T, AD), lambda qi: (0, 2)),
                  col(3), pl.BlockSpec((1, AD), lambda qi: (0, 0))] + [HBM_SPEC] * n,
        out_specs=[col(0), col(0), pl.BlockSpec((NG, AQ, GW), lambda qi: (0, qi, 0))] + [HBM_SPEC] * n,
        out_shape=[jax.ShapeDtypeStruct((T, AD), F32), jax.ShapeDtypeStruct((T, AD), BF16),
                   jax.ShapeDtypeStruct((NG, T, GW), F32)]
        + [jax.ShapeDtypeStruct((NCHIP,) + s.shape, s.dtype) for s in shards],
        scratch_shapes=scratch + _hosted_sems(n, GATHER_SEMS),
        compiler_params=_cp(("arbitrary",), VMEM_BIG),
    )(u, u, u, u, agw, *shards)
    return res[0], res[1], res[2], list(res[3:])


def _glu_conv(u, dw, db):
    tr = 256

    def body(cv_ref, cg_ref, w_ref, b_ref, c1_ref, pad_s):
        pad_s[pl.ds(0, CWP), :] = jnp.zeros((CWP, LANE), F32)
        pad_s[pl.ds(CWP, T), :] = cv_ref[...] * _sig(cg_ref[...])
        wv = w_ref[0]
        bias = b_ref[...]

        def tile(i, carry):
            r0 = pl.multiple_of(i * tr, tr)
            acc = jnp.zeros((tr, LANE), F32) + bias
            for w in range(CW):
                acc = acc + pad_s[pl.ds(r0 + (CWP - CW + 1) + w, tr), :] * wv[w:w + 1, :]
            c1_ref[pl.ds(r0, tr), :] = acc
            return carry

        lax.fori_loop(0, T // tr, tile, 0)

    return pl.pallas_call(
        body, name="glu_conv", grid=(CD // LANE,),
        in_specs=[pl.BlockSpec((T, LANE), lambda cb: (0, 16 + cb)),
                  pl.BlockSpec((T, LANE), lambda cb: (0, 20 + cb)),
                  pl.BlockSpec((1, CWP, LANE), lambda cb: (cb, 0, 0)),
                  pl.BlockSpec((1, LANE), lambda cb: (0, cb))],
        out_specs=pl.BlockSpec((T, LANE), lambda cb: (0, cb)),
        out_shape=jax.ShapeDtypeStruct((T, CD), F32),
        scratch_shapes=[pltpu.VMEM((T + CWP, LANE), F32)],
        compiler_params=_cp(("arbitrary",)),
    )(u, u, dw, db)


def _ln_silu(c1, lg, lb):
    mu = jnp.mean(c1, axis=-1, keepdims=True)
    xc = c1 - mu
    rs = lax.rsqrt(jnp.mean(xc * xc, axis=-1, keepdims=True) + EPS)
    xh = xc * rs
    ln = xh * lg + lb
    s = _sig(ln)
    return xh, rs, ln, s


def _layer_tail(c1, u, ya, h, p, lg, lb, wpw, cg, wout, pg, wgate, wple, head=None):
    tm = min(TR, T)

    def body(c1_ref, gc_ref, ya_ref, h_ref, p_ref, lg_ref, lb_ref, wpw_ref, cg_ref, wout_ref,
             pg_ref, wgate_ref, wple_ref, *rest):
        c3_ref, yc_ref, h1_ref, gate_ref, pe_ref, h2_ref = rest[-6 - 2 * bool(head):][:6]
        _, _, ln, s = _ln_silu(c1_ref[...], lg_ref[...], lb_ref[...])
        c2 = (ln * s).astype(BF16)
        c3 = _dot(c2, wpw_ref[...])
        gc = gc_ref[...]
        yc = (c3 * _rstd(c3) * cg_ref[...] * (gc * _sig(gc))).astype(BF16)
        c3_ref[...] = c3
        yc_ref[...] = yc
        y = _dot(ya_ref[...], wout_ref[pl.ds(0, AD), :]) + _dot(yc, wout_ref[pl.ds(AD, CD), :])
        h1 = h_ref[...] + y
        hn2 = (h1 * _rstd(h1) * pg_ref[...]).astype(BF16)
        gate = _sig(_dot(hn2, wgate_ref[...]))
        pb = p_ref[...].astype(BF16)
        pe = jnp.concatenate([_dot(pb, wple_ref[k]) for k in range(NCHIP)], axis=1)
        h1_ref[...] = h1
        gate_ref[...] = gate.astype(BF16)
        pe_ref[...] = pe.astype(BF16)
        h2 = h1 + pe * gate
        if not head:
            h2_ref[...] = h2
            return
        t_ref, fg_ref, loss_ref, dfg_ref = rest[0], rest[1], rest[-2], rest[-1]

        @pl.when(pl.program_id(0) == 0)
        def _():
            loss_ref[...] = jnp.zeros_like(loss_ref)
            dfg_ref[...] = jnp.zeros_like(dfg_ref)
        fg = fg_ref[...]
        r = _rstd(h2)
        e = h2 * r * fg - t_ref[...]
        loss_ref[...] += 0.5 * jnp.sum(jnp.mean(e * e, axis=-1, keepdims=True))
        dy = e * (1.0 / D)
        dfg_ref[...] += jnp.sum(dy * h2 * r, axis=0, keepdims=True)
        h2_ref[...] = _rms_bwd(dy, h2, r, fg)

    row = lambda w: pl.BlockSpec((tm, w), lambda i: (i, 0))
    full = lambda *s: pl.BlockSpec(s, lambda i: (0,) * len(s), pipeline_mode=pl.Buffered(1))
    extra = bool(head)
    return pl.pallas_call(
        body, name="layer_tail", grid=(T // tm,),
        in_specs=[row(CD), pl.BlockSpec((tm, CD), lambda i: (i, 6)), row(AD), row(D), row(PLE),
                  full(1, CD), full(1, CD), full(CD, CD), full(1, CD), full(D, D),
                  full(1, D), full(D, D), full(NCHIP, PLE, PLE)] + [row(D), full(1, D)] * extra,
        out_specs=[row(CD), row(CD), row(D), row(D), row(D), row(D)] + [full(8, LANE), full(1, D)] * extra,
        out_shape=[jax.ShapeDtypeStruct((T, CD), F32), jax.ShapeDtypeStruct((T, CD), BF16),
                   jax.ShapeDtypeStruct((T, D), F32), jax.ShapeDtypeStruct((T, D), BF16),
                   jax.ShapeDtypeStruct((T, D), BF16), jax.ShapeDtypeStruct((T, D), F32)]
        + [jax.ShapeDtypeStruct((8, LANE), F32), jax.ShapeDtypeStruct((1, D), F32)] * extra,
        compiler_params=_cp(("arbitrary",), VMEM_BIG),
    )(c1, u, ya, h, p, lg, lb, wpw, cg, wout, pg, wgate, wple, *(head or ()))


def _ple_out_bwd(dh2, h1, gate, pe, p, ya, yc, pg, wgate, wout):
    tm = min(TR, T)

    def body(dh2_ref, h1_ref, gate_ref, pe_ref, p_ref, ya_ref, yc_ref, pg_ref, wgate_ref, wout_ref,
             dh1_ref, dy_ref, dwg_ref, dwp_ref, dwo_ref, dpg_ref):
        @pl.when(pl.program_id(0) == 0)
        def _():
            dwg_ref[...] = jnp.zeros_like(dwg_ref)
            dwp_ref[...] = jnp.zeros_like(dwp_ref)
            dwo_ref[...] = jnp.zeros_like(dwo_ref)
            dpg_ref[...] = jnp.zeros_like(dpg_ref)
        dh2 = dh2_ref[...]
        h1 = h1_ref[...]
        gate = gate_ref[...].astype(F32)
        pg = pg_ref[...]
        dpe = (dh2 * gate).astype(BF16)
        dgp = (dh2 * pe_ref[...].astype(F32) * gate * (1.0 - gate)).astype(BF16)
        r = _rstd(h1)
        hn = h1 * r
        dwg_ref[...] += _dot_tn((hn * pg).astype(BF16), dgp)
        dhn2 = _dot_nt(dgp, wgate_ref[...])
        dpg_ref[...] += jnp.sum(dhn2 * hn, axis=0, keepdims=True)
        dh1 = dh2 + _rms_bwd(dhn2, h1, r, pg)
        pb = p_ref[...].astype(BF16)
        for k in range(NCHIP):
            dwp_ref[k] += _dot_tn(pb, dpe[:, k * PLE:(k + 1) * PLE])
        dh1b = dh1.astype(BF16)
        dy_ref[...] = _dot_nt(dh1b, wout_ref[...])
        dwo_ref[pl.ds(0, AD), :] += _dot_tn(ya_ref[...], dh1b)
        dwo_ref[pl.ds(AD, CD), :] += _dot_tn(yc_ref[...], dh1b)
        dh1_ref[...] = dh1

    row = lambda w: pl.BlockSpec((tm, w), lambda i: (i, 0))
    full = lambda *s: pl.BlockSpec(s, lambda i: (0,) * len(s), pipeline_mode=pl.Buffered(1))
    return pl.pallas_call(
        body, name="ple_out_bwd", grid=(T // tm,),
        in_specs=[row(D), row(D), row(D), row(D), row(PLE), row(AD), row(CD),
                  full(1, D), full(D, D), full(D, D)],
        out_specs=[row(D), row(D), full(D, D), full(NCHIP, PLE, PLE), full(D, D), full(1, D)],
        out_shape=[jax.ShapeDtypeStruct((T, D), F32), jax.ShapeDtypeStruct((T, D), F32),
                   jax.ShapeDtypeStruct((D, D), F32), jax.ShapeDtypeStruct((NCHIP, PLE, PLE), F32),
                   jax.ShapeDtypeStruct((D, D), F32), jax.ShapeDtypeStruct((1, D), F32)],
        compiler_params=_cp(("arbitrary",), VMEM_BIG),
    )(dh2, h1, gate, pe, p, ya, yc, pg, wgate, wout)


def _branch_bwd(dy, o, u, c1, c3, ag, lg, lb, wpw, cg, seg):
    tm = min(TR, T)

    def body(dya_ref, dyc_ref, o_ref, ga_ref, gc_ref, c1_ref, c3_ref, ag_ref, lg_ref, lb_ref, wpw_ref,
             cg_ref, seg_ref, do_ref, dga_ref, dgc_ref, dc1_ref, dwpw_ref, dag_ref, dcg_ref, dlg_ref, dlb_ref):
        @pl.when(pl.program_id(0) == 0)
        def _():
            for r_ in (dwpw_ref, dag_ref, dcg_ref, dlg_ref, dlb_ref):
                r_[...] = jnp.zeros_like(r_)
        dya = dya_ref[...]
        o = o_ref[...]
        ga = ga_ref[...]
        ag_v = ag_ref[...]
        seg_m = seg_ref[...]
        r = lax.rsqrt(_dot2(o * o, seg_m) * (1.0 / DH) + EPS)
        onr = o * r
        sg = _sig(ga)
        dga_ref[...] = (dya * (onr * ag_v) * (sg * (1.0 + ga * (1.0 - sg)))).astype(BF16)
        don = dya * (ga * sg)
        dag_ref[...] += jnp.sum(don * onr, axis=0, keepdims=True)
        dn = don * ag_v
        do_ref[...] = r * dn - o * (r * r * r) * (_dot2(dn * o, seg_m) * (1.0 / DH))
        dyc = dyc_ref[...]
        c3 = c3_ref[...]
        gc = gc_ref[...]
        cg_v = cg_ref[...]
        r3 = _rstd(c3)
        cn = c3 * r3
        sc = _sig(gc)
        dgc_ref[...] = (dyc * (cn * cg_v) * (sc * (1.0 + gc * (1.0 - sc)))).astype(BF16)
        dcn = dyc * (gc * sc)
        dcg_ref[...] += jnp.sum(dcn * cn, axis=0, keepdims=True)
        dc3 = _rms_bwd(dcn, c3, r3, cg_v).astype(BF16)
        lg_v = lg_ref[...]
        xh, rs, ln, s = _ln_silu(c1_ref[...], lg_v, lb_ref[...])
        c2 = (ln * s).astype(BF16)
        dwpw_ref[...] += _dot_tn(c2, dc3)
        dc2 = _dot_nt(dc3, wpw_ref[...])
        dln = dc2 * (s * (1.0 + ln * (1.0 - s)))
        dlb_ref[...] += jnp.sum(dln, axis=0, keepdims=True)
        dlg_ref[...] += jnp.sum(dln * xh, axis=0, keepdims=True)
        dxh = dln * lg_v
        dc1_ref[...] = rs * (dxh - jnp.mean(dxh, axis=-1, keepdims=True)
                             - xh * jnp.mean(dxh * xh, axis=-1, keepdims=True))

    half = lambda j: pl.BlockSpec((tm, 512), lambda i: (i, j))
    full = lambda *s: pl.BlockSpec(s, lambda i: (0,) * len(s), pipeline_mode=pl.Buffered(1))
    vec = jax.ShapeDtypeStruct((1, 512), F32)
    act = jax.ShapeDtypeStruct((T, 512), F32)
    return pl.pallas_call(
        body, name="branch_bwd", grid=(T // tm,),
        in_specs=[half(0), half(1), half(0), half(3), half(6), half(0), half(0),
                  full(1, AD), full(1, CD), full(1, CD), full(CD, CD), full(1, CD), full(AD, AD)],
        out_specs=[half(0), half(0), half(0), half(0), full(CD, CD), full(1, 512), full(1, 512),
                   full(1, 512), full(1, 512)],
        out_shape=[act, jax.ShapeDtypeStruct((T, 512), BF16), jax.ShapeDtypeStruct((T, 512), BF16), act,
                   jax.ShapeDtypeStruct((CD, CD), F32), vec, vec, vec, vec],
        compiler_params=_cp(("arbitrary",), VMEM_BIG),
    )(dy, dy, o, u, u, c1, c3, ag, lg, lb, wpw, cg, seg)


def _conv_bwd(dc1, u, dw, grads=()):
    tr = 64
    n_x = len(grads)
    grid = (CD // LANE,)
    off = CWP - CW + 1

    def body(d_ref, cv_ref, cg_ref, w_ref, dcv_ref, dcg_ref, ddw_ref, ddb_ref, padc_s, padd_s, acc_s):
        cv = cv_ref[...]
        sg = _sig(cg_ref[...])
        padc_s[pl.ds(0, CWP), :] = jnp.zeros((CWP, LANE), F32)
        padc_s[pl.ds(CWP, T), :] = cv * sg
        padd_s[pl.ds(0, T), :] = d_ref[...]
        padd_s[pl.ds(T, CWP), :] = jnp.zeros((CWP, LANE), F32)
        acc_s[...] = jnp.zeros_like(acc_s)
        wv = w_ref[0]

        def tile(i, carry):
            r0 = pl.multiple_of(i * tr, tr)
            dt = padd_s[pl.ds(r0, tr), :]
            dc0 = jnp.zeros((tr, LANE), F32)
            for w in range(CW):
                dc0 = dc0 + padd_s[pl.ds(r0 + (CW - 1) - w, tr), :] * wv[w:w + 1, :]
                prod = dt * padc_s[pl.ds(r0 + off + w, tr), :]
                acc_s[w] += jnp.sum(prod.reshape(tr // 8, 8, LANE), axis=0)
            cvt = cv_ref[pl.ds(r0, tr), :]
            sgt = _sig(cg_ref[pl.ds(r0, tr), :])
            dcv_ref[pl.ds(r0, tr), :] = (dc0 * sgt).astype(BF16)
            dcg_ref[pl.ds(r0, tr), :] = (dc0 * cvt * sgt * (1.0 - sgt)).astype(BF16)
            return carry

        lax.fori_loop(0, T // tr, tile, 0)
        ddw_ref[0] = jnp.sum(acc_s[...], axis=1)
        ddb_ref[...] = jnp.sum(d_ref[...], axis=0, keepdims=True)

    col = lambda j: pl.BlockSpec((T, LANE), lambda cb: (0, j + cb))
    res = pl.pallas_call(
        _host(body, grid, 4, 4, n_x, _pair_copies), name="conv_bwd", grid=grid,
        in_specs=[col(0), col(16), col(20), pl.BlockSpec((1, CWP, LANE), lambda cb: (cb, 0, 0))] + [HBM_SPEC] * n_x,
        out_specs=[col(0), col(0), pl.BlockSpec((1, CWP, LANE), lambda cb: (cb, 0, 0)),
                   pl.BlockSpec((1, LANE), lambda cb: (0, cb))] + [HBM_SPEC] * n_x,
        out_shape=[jax.ShapeDtypeStruct((T, CD), BF16), jax.ShapeDtypeStruct((T, CD), BF16),
                   jax.ShapeDtypeStruct((NCHIP, CWP, LANE), F32), jax.ShapeDtypeStruct((1, CD), F32)]
        + [jax.ShapeDtypeStruct((NCHIP, g.shape[1] // 2, g.shape[2]), F32) for g in grads],
        scratch_shapes=[pltpu.VMEM((T + CWP, LANE), F32), pltpu.VMEM((T + CWP, LANE), F32),
                        pltpu.VMEM((CWP, 8, LANE), F32)] + _hosted_sems(n_x),
        compiler_params=_cp(("arbitrary",)),
    )(dc1, u, u, dw, *grads)
    return res[0], res[1], res[2], res[3], list(res[4:])


def _attn_bwd(u, do, tot, partials=()):
    n_x = len(partials)
    grid = (T // AQ,)

    def body(q_ref, k_ref, v_ref, do_ref, tot_ref, dq_ref, dk_ref, dv_ref, kb_s, vb_s, dk_s, dv_s):
        qi = pl.program_id(0)

        @pl.when(qi == 0)
        def _():
            kb_s[...] = k_ref[...].astype(BF16)
            vb_s[...] = v_ref[...].astype(BF16)
            dk_s[...] = jnp.zeros_like(dk_s)
            dv_s[...] = jnp.zeros_like(dv_s)

        causal, tr, tc, heads = _attn_tiles()
        upper = (tr > tc).astype(BF16)
        lower = (tr < tc).astype(BF16)
        qs, qus, dos, tots = [], [], [], []
        for g in range(NG):
            lanes = slice(g * GW, (g + 1) * GW)
            q = q_ref[:, lanes]
            qs.append(_stack_heads(q * 0.125, heads).astype(BF16))
            qus.append(_stack_heads(q, heads).astype(BF16))
            dos.append(_stack_heads(do_ref[:, lanes], heads).astype(BF16))
            totv = tot_ref[g]
            tots.append(jnp.concatenate([totv[:, h * DH:h * DH + 1] for h in range(HG)], axis=0))

        def block(kb, carry, masked):
            k0 = pl.multiple_of(kb * AQ, AQ)
            out = []
            for g in range(NG):
                lanes = pl.ds(g * GW, GW)
                lm_left, dl_left, dq = carry[g]
                kk = kb_s[pl.ds(k0, AQ), lanes]
                vv = vb_s[pl.ds(k0, AQ), lanes]
                z = _dot_nt(qs[g], kk)
                sp = _softplus(z)
                lm = jnp.where(causal, -sp, 0.0) if masked else -sp
                lm_incl = lm_left + jnp.sum(lm, axis=1, keepdims=True)
                att = jnp.exp((z - sp) + _tri_sum(lm, upper) + (tots[g] - lm_incl))
                if masked:
                    att = jnp.where(causal, att, 0.0)
                dl = att * _dot_nt(dos[g], vv)
                dv_s[pl.ds(k0, AQ), lanes] += _dot_tn(att.astype(BF16), dos[g])
                prefix = dl_left + _tri_sum(dl, lower)
                beta = jnp.exp(z - sp)
                dz = (1.0 - beta) * dl - beta * prefix
                if masked:
                    dz = jnp.where(causal, dz, 0.0)
                dzs = (dz * 0.125).astype(BF16)
                dk_s[pl.ds(k0, AQ), lanes] += _dot_tn(dzs, qus[g])
                out.append((lm_incl, dl_left + jnp.sum(dl, axis=1, keepdims=True), dq + _dot(dzs, kk)))
            return tuple(out)

        zero = jnp.zeros((SR, 1), F32)
        init = tuple((zero, zero, jnp.zeros((SR, GW), F32)) for _ in range(NG))
        carry = lax.fori_loop(0, qi, lambda kb, c: block(kb, c, False), init)
        carry = block(qi, carry, True)
        for g in range(NG):
            dq_ref[:, g * GW:(g + 1) * GW] = _unstack_heads(carry[g][2], heads).astype(BF16)

        @pl.when(qi == grid[0] - 1)
        def _():
            dk_ref[...] = dk_s[...].astype(BF16)
            dv_ref[...] = dv_s[...].astype(BF16)

    col = lambda j: pl.BlockSpec((AQ, AD), lambda qi: (qi, j))
    whole = lambda j: pl.BlockSpec((T, AD), lambda qi: (0, j), pipeline_mode=pl.Buffered(1))
    res = pl.pallas_call(
        _host(body, grid, 5, 3, n_x, _scatter_copies), name="attn_bwd", grid=grid,
        in_specs=[col(0), whole(1), whole(2), col(0), pl.BlockSpec((NG, AQ, GW), lambda qi: (0, qi, 0))]
        + [HBM_SPEC] * n_x,
        out_specs=[col(0), whole(0), whole(0)] + [HBM_SPEC] * n_x,
        out_shape=[jax.ShapeDtypeStruct((T, AD), BF16)] * 3
        + [jax.ShapeDtypeStruct((NCHIP - 1,) + a.shape[1:], a.dtype) for a in partials],
        scratch_shapes=[pltpu.VMEM((T, AD), BF16), pltpu.VMEM((T, AD), BF16), pltpu.VMEM((T, AD), F32),
                        pltpu.VMEM((T, AD), F32)] + _hosted_sems(n_x),
        compiler_params=_cp(("arbitrary",), VMEM_BIG),
    )(u, u, u, do, tot, *partials)
    return res[0], res[1], res[2], list(res[3:])


def _inproj_dw(hn, du):
    tm = min(TM, T)

    def body(hn_ref, du_ref, dw_ref):
        @pl.when(pl.program_id(1) == 0)
        def _():
            dw_ref[...] = jnp.zeros_like(dw_ref)
        dw_ref[0] += _dot_tn(hn_ref[...], du_ref[...])

    return pl.pallas_call(
        body, name="inproj_dw", grid=(NCHIP, T // tm),
        in_specs=[pl.BlockSpec((tm, D), lambda k, i: (i, 0)), pl.BlockSpec((tm, SHW), lambda k, i: (i, k))],
        out_specs=pl.BlockSpec((1, D, SHW), lambda k, i: (k, 0, 0)),
        out_shape=jax.ShapeDtypeStruct((NCHIP, D, SHW), F32),
        compiler_params=_cp(("arbitrary", "arbitrary"), VMEM_BIG),
    )(hn, du)


def _inproj_dx(du, w, h, g, dres, partials=(), grads=()):
    tm = min(TM, T)
    sent = list(partials) + list(grads)
    n_x = len(sent)
    grid = (T // tm, NCHIP)
    if grads:
        landing = [jax.ShapeDtypeStruct((NCHIP, a.shape[1] // 2, a.shape[2]), F32) for a in grads]
    else:
        landing = [jax.ShapeDtypeStruct((NCHIP - 1,) + a.shape[1:], a.dtype) for a in partials]

    def body(du_ref, w_ref, h_ref, g_ref, dres_ref, dh_ref, dg_ref, acc_s):
        i, k = pl.program_id(0), pl.program_id(1)

        @pl.when(jnp.logical_and(i == 0, k == 0))
        def _():
            dg_ref[...] = jnp.zeros_like(dg_ref)

        @pl.when(k == 0)
        def _():
            acc_s[...] = _dot_nt(du_ref[...], w_ref[0])

        @pl.when(k > 0)
        def _():
            acc_s[...] += _dot_nt(du_ref[...], w_ref[0])

        @pl.when(k == NCHIP - 1)
        def _():
            hh = h_ref[...]
            r = _rstd(hh)
            dhn = acc_s[...]
            dg_ref[...] += jnp.sum(dhn * hh * r, axis=0, keepdims=True)
            dh_ref[...] = dres_ref[...] + _rms_bwd(dhn, hh, r, g_ref[...])

    res = pl.pallas_call(
        _host(body, grid, 5, 2, n_x, _pair_copies if grads else _scatter_copies), name="inproj_dx", grid=grid,
        in_specs=[pl.BlockSpec((tm, SHW), lambda i, k: (i, k)),
                  pl.BlockSpec((1, D, SHW), lambda i, k: (k, 0, 0)),
                  pl.BlockSpec((tm, D), lambda i, k: (i, 0)),
                  pl.BlockSpec((1, D), lambda i, k: (0, 0)),
                  pl.BlockSpec((tm, D), lambda i, k: (i, 0))] + [HBM_SPEC] * n_x,
        out_specs=[pl.BlockSpec((tm, D), lambda i, k: (i, 0)), pl.BlockSpec((1, D), lambda i, k: (0, 0))]
        + [HBM_SPEC] * n_x,
        out_shape=[jax.ShapeDtypeStruct((T, D), F32), jax.ShapeDtypeStruct((1, D), F32)] + landing,
        scratch_shapes=[pltpu.VMEM((tm, D), F32)] + _hosted_sems(n_x),
        compiler_params=_cp(("arbitrary", "arbitrary"), VMEM_BIG),
    )(du, w, h, g, dres, *sent)
    return res[0], res[1], list(res[2:])


def _sum_pair(core, grads, gots):
    n = len(grads)

    def body(c_ref, *refs):
        for a in range(n):
            refs[2 * n + a][...] = (refs[a][...] + refs[n + a][...]).astype(BF16)

    mine = [pl.BlockSpec((1,) + s.shape[1:], lambda k, c: (k, c[0], 0)) for s in gots]
    same = [pl.BlockSpec((1,) + s.shape[1:], lambda k, c: (k, 0, 0)) for s in gots]
    return pl.pallas_call(
        body, name="sum_pair",
        grid_spec=pltpu.PrefetchScalarGridSpec(
            num_scalar_prefetch=1, grid=(NCHIP,), in_specs=mine + same, out_specs=same),
        out_shape=[jax.ShapeDtypeStruct(s.shape, BF16) for s in gots],
        compiler_params=_cp(("arbitrary",), VMEM_BIG),
    )(core, *grads, *gots)


def _sum_chips_share(chip, partials, gots):
    flat_p = [p for layer in partials for p in layer]
    flat_g = [g for layer in gots for g in layer]
    n, per_layer = len(flat_p), len(partials[0])

    def body(c_ref, *refs):
        full, sums = refs[2 * n:2 * n + per_layer], refs[2 * n + per_layer:3 * n + per_layer]
        send_sems, recv_sems, local_sems = refs[3 * n + per_layer:]
        x, y, c = _place()
        copies = []
        for i in range(n):
            acc = refs[i][0].astype(F32)
            for j in range(NCHIP - 1):
                acc = acc + refs[n + i][j].astype(F32)
            sums[i][...] = acc
            half = flat_p[i].shape[1]
            rows = full[i % per_layer].at[i // per_layer, pl.ds(c * half, half)]
            copies.append(pltpu.make_async_copy(sums[i], rows, local_sems.at[i]))
            copies.append(pltpu.make_async_remote_copy(
                src_ref=sums[i], dst_ref=rows, send_sem=send_sems.at[i], recv_sem=recv_sems.at[i],
                device_id=(x, y, 1 - c), device_id_type=MESH))
        for cp in copies:
            cp.start()
        for cp in copies:
            cp.wait()

    return pl.pallas_call(
        body, name="sum_chips_share",
        grid_spec=pltpu.PrefetchScalarGridSpec(
            num_scalar_prefetch=1, grid=(1,),
            in_specs=[pl.BlockSpec((1,) + s.shape[1:], lambda i, c: (c[0], 0, 0)) for s in flat_p]
            + [pl.BlockSpec(s.shape, lambda i, c: (0, 0, 0)) for s in flat_g],
            out_specs=[HBM_SPEC] * per_layer,
            scratch_shapes=[pltpu.VMEM(s.shape[1:], F32) for s in flat_p]
            + [pltpu.SemaphoreType.DMA((n,)), pltpu.SemaphoreType.DMA((n,)), pltpu.SemaphoreType.DMA((n,))]),
        out_shape=[jax.ShapeDtypeStruct((len(partials), 2 * s.shape[1], s.shape[2]), F32) for s in partials[0]],
        compiler_params=_cp(("arbitrary",), VMEM_BIG),
    )(chip, *flat_p, *flat_g)


def _adam_math(w, g, m, v):
    nm = ADAM_B1 * m + (1.0 - ADAM_B1) * g
    nv = ADAM_B2 * v + (1.0 - ADAM_B2) * (g * g)
    m_hat = nm / (1.0 - ADAM_B1 ** ADAM_STEP)
    v_hat = nv / (1.0 - ADAM_B2 ** ADAM_STEP)
    return -ADAM_LR * (m_hat / (jnp.sqrt(v_hat) + ADAM_EPS) + ADAM_WD * w), nm, nv


def _adamw(w, g, m, v, rows):
    R, C = w.shape

    def body(w_ref, g_ref, m_ref, v_ref, d_ref, nm_ref, nv_ref):
        d_ref[...], nm_ref[...], nv_ref[...] = _adam_math(w_ref[...], g_ref[...], m_ref[...], v_ref[...])

    spec = pl.BlockSpec((rows, C), lambda i: (i, 0))
    sh = jax.ShapeDtypeStruct((R, C), F32)
    return pl.pallas_call(
        body, name="adamw", grid=(R // rows,), in_specs=[spec] * 4, out_specs=[spec] * 3,
        out_shape=[sh, sh, sh], compiler_params=_cp(("arbitrary",)),
    )(w, g, m, v)


def _small_adamw(tot, ws, ms, vs):
    n = len(ws)

    def body(*refs):
        tot_ref = refs[0]
        w_refs, m_refs, v_refs = refs[1:1 + n], refs[1 + n:1 + 2 * n], refs[1 + 2 * n:1 + 3 * n]
        outs = refs[1 + 3 * n:]
        for i in range(n):
            rows, width = ws[i].shape
            g = tot_ref[pl.ds(SMALL_ROW[i], rows), pl.ds(0, width)]
            outs[4 * i][...] = g
            outs[4 * i + 1][...], outs[4 * i + 2][...], outs[4 * i + 3][...] = _adam_math(
                w_refs[i][...], g, m_refs[i][...], v_refs[i][...])
        outs[4 * n][...] = tot_ref[pl.ds(LOSS_ROW, 1), pl.ds(0, LANE)]

    vmem = pl.BlockSpec(memory_space=pltpu.VMEM)
    res = pl.pallas_call(
        body, name="small_adamw", in_specs=[vmem] * (1 + 3 * n), out_specs=[vmem] * (4 * n + 1),
        out_shape=[jax.ShapeDtypeStruct(w.shape, F32) for w in ws for _ in range(4)]
        + [jax.ShapeDtypeStruct((1, LANE), F32)],
    )(tot, *ws, *ms, *vs)
    return [res[4 * i:4 * i + 4] for i in range(n)], res[4 * n]


HBM_SPEC = pl.BlockSpec(memory_space=pltpu.HBM)


def _place():
    return lax.axis_index("x"), lax.axis_index("y"), lax.axis_index("c")


def _all_gather_split(shard):
    def body(in_ref, out_ref, send_sems, recv_sems):
        direct, relayed, passed = _gather_tree_copies([in_ref], [out_ref], send_sems, recv_sems)
        for cp in direct:
            cp.start()
        for i in range(2):
            direct[i].wait_recv()
            relayed[i].start()
            passed[i].start()
        for cp in relayed:
            cp.wait_recv()
        passed[2].start()
        for cp in passed:
            cp.wait_recv()
        for cp in direct + relayed + passed:
            cp.wait_send()

    return pl.pallas_call(
        body, name="all_gather_split", in_specs=[HBM_SPEC], out_specs=HBM_SPEC,
        out_shape=jax.ShapeDtypeStruct((NCHIP,) + shard.shape, shard.dtype),
        scratch_shapes=_hosted_sems(1, GATHER_SEMS),
    )(shard)


SEM_SPEC = pl.BlockSpec(memory_space=pltpu.SEMAPHORE)
ORDERED_EFFECT = pltpu.CompilerParams(has_side_effects=pltpu.SideEffectType.DATAFLOW_SIDE_EFFECTING)


def _scatter_start(partial):
    land = pltpu.with_memory_space_constraint(
        lax.empty((NCHIP - 1,) + partial.shape[1:], partial.dtype), pltpu.HBM)

    def body(p_ref, land_ref, send_sems, recv_sems, p_thru, land_thru, token):
        for cp in _scatter_copies([p_ref], [land_ref], send_sems, recv_sems):
            cp.start()
        token[...] = jnp.zeros_like(token)

    return pl.pallas_call(
        body, name="scatter_start",
        out_shape=(pltpu.SemaphoreType.DMA((NCHIP - 1,)), pltpu.SemaphoreType.DMA((NCHIP - 1,)),
                   pltpu.HBM(partial.shape, partial.dtype), pltpu.HBM(land.shape, land.dtype),
                   jax.ShapeDtypeStruct((8, LANE), F32)),
        in_specs=(HBM_SPEC, HBM_SPEC),
        out_specs=(SEM_SPEC, SEM_SPEC, HBM_SPEC, HBM_SPEC, pl.BlockSpec(memory_space=pltpu.VMEM)),
        input_output_aliases={0: 2, 1: 3}, compiler_params=ORDERED_EFFECT,
    )(pltpu.with_memory_space_constraint(partial, pltpu.HBM), land)


def _scatter_wait(send_sems, recv_sems, p_thru, land_thru, after):
    def body(p_ref, land_ref, send_sems, recv_sems, after_ref, p_dead, got_ref):
        for cp in _scatter_copies([p_ref], [land_ref], send_sems, recv_sems):
            cp.wait_send()
            cp.wait_recv()

    return pl.pallas_call(
        body, name="scatter_wait",
        out_shape=(pltpu.HBM(p_thru.shape, p_thru.dtype), pltpu.HBM(land_thru.shape, land_thru.dtype)),
        in_specs=(HBM_SPEC, HBM_SPEC, SEM_SPEC, SEM_SPEC, pl.BlockSpec(memory_space=pl.ANY)),
        out_specs=(HBM_SPEC, HBM_SPEC), input_output_aliases={0: 0, 1: 1}, compiler_params=ORDERED_EFFECT,
    )(p_thru, land_thru, send_sems, recv_sems, after)[1]


def _pair_exchange(grads):
    n = len(grads)

    def body(*refs):
        copies = _pair_copies(refs[:n], refs[n:2 * n], refs[2 * n], refs[2 * n + 1])
        for cp in copies:
            cp.start()
        for cp in copies:
            cp.wait()

    return pl.pallas_call(
        body, name="pair_exchange", in_specs=[HBM_SPEC] * n, out_specs=[HBM_SPEC] * n,
        out_shape=[jax.ShapeDtypeStruct((NCHIP, g.shape[1] // 2, g.shape[2]), F32) for g in grads],
        scratch_shapes=[pltpu.SemaphoreType.DMA((n,)), pltpu.SemaphoreType.DMA((n,))],
    )(*grads)


def _small_allreduce(rows, loss_blk):
    n = len(rows)

    def body(*refs):
        loss_ref, o_ref, pk, slots, send_sems, recv_sems = refs[n:]
        pk[...] = jnp.zeros_like(pk)
        for i in range(n):
            pk[pl.ds(i, 1), pl.ds(0, rows[i].shape[1])] = refs[i][...]
        pk[pl.ds(LOSS_ROW, 1), pl.ds(0, LANE)] = loss_ref[pl.ds(0, 1), :]
        x, y, c = _place()
        me = 4 * x + 2 * y + c
        slots[me] = pk[...]
        copies = []
        for r in range(1, 8):
            rx, ry, rc = (r >> 2) & 1, (r >> 1) & 1, r & 1
            peer = (x + rx - 2 * x * rx, y + ry - 2 * y * ry, c + rc - 2 * c * rc)
            cp = pltpu.make_async_remote_copy(
                src_ref=pk, dst_ref=slots.at[me], send_sem=send_sems.at[r - 1], recv_sem=recv_sems.at[r - 1],
                device_id=peer, device_id_type=MESH)
            cp.start()
            copies.append(cp)
        for cp in copies:
            cp.wait()
        acc = slots[0]
        for j in range(1, 8):
            acc = acc + slots[j]
        o_ref[...] = acc

    vmem = pl.BlockSpec(memory_space=pltpu.VMEM)
    return pl.pallas_call(
        body, name="small_allreduce", in_specs=[vmem] * (n + 1), out_specs=vmem,
        out_shape=jax.ShapeDtypeStruct((SMALL_PK, D), F32),
        scratch_shapes=[pltpu.VMEM((SMALL_PK, D), F32), pltpu.VMEM((8, SMALL_PK, D), F32),
                        pltpu.SemaphoreType.DMA((7,)), pltpu.SemaphoreType.DMA((7,))],
    )(*rows, loss_blk)


def _seg_matrix():
    i = lax.broadcasted_iota(jnp.int32, (AD, AD), 0) // DH
    j = lax.broadcasted_iota(jnp.int32, (AD, AD), 1) // DH
    return (i == j).astype(BF16)


TAIL = ("w_out", "w_ple_gate", "w_ple", "w_pw", "dw_w")


def _local_step(x, p, tgt, sm, shards, chip, ci):
    seg = _seg_matrix()
    core = jnp.reshape(ci, (1,)).astype(jnp.int32)
    chip_idx = jnp.reshape(chip, (1,)).astype(jnp.int32)
    own = lambda g, s: lax.dynamic_update_index_in_dim(g, s, chip, 0)
    w_in_next = own(_all_gather_split(shards[0]["w_in"]), shards[0]["w_in"])
    h = x
    saved = []
    for l in range(DEPTH):
        w_in = w_in_next
        row = lambda name: sm[name][l:l + 1]
        u, hn = _rms_inproj(h, row("norm_g"), w_in)
        todo = [shards[l][k] for k in TAIL] + ([shards[l + 1]["w_in"]] if l + 1 < DEPTH else [])
        o, ya, tot, got = _attn_fwd(u, jnp.tile(row("attn_out_g"), (1, AD // DH)), todo)
        got = [own(g, s) for g, s in zip(got, todo)]
        w_out = got[0].reshape(D, D)
        w_gate = got[1].reshape(D, D)
        w_ple = got[2]
        w_pw = got[3].reshape(CD, CD)
        dw = got[4]
        if l + 1 < DEPTH:
            w_in_next = got[5]
        c1 = _glu_conv(u, dw, row("dw_b"))
        c3, yc, h1, gate, pe, h2, *at_end = _layer_tail(
            c1, u, ya, h, p[l], row("conv_ln_g"), row("conv_ln_b"), w_pw, row("conv_out_g"), w_out,
            row("ple_norm_g"), w_gate, w_ple, head=(tgt, sm["final_g"]) if l == DEPTH - 1 else None)
        saved.append(dict(h=h, u=u, hn=hn, o=o, ya=ya, tot=tot, c1=c1, c3=c3, yc=yc, h1=h1, gate=gate, pe=pe,
                          w_in=w_in, w_out=w_out, w_gate=w_gate, w_pw=w_pw, dw=dw))
        h = h2
    dh, (loss_blk, dfg) = h, at_end
    small = [None] * DEPTH
    pending, partials, arrived = [], {}, {}
    pair_sum = lambda grads: _sum_pair(core, grads, _pair_exchange(grads))
    for l in reversed(range(DEPTH)):
        s = saved[l]
        row = lambda name: sm[name][l:l + 1]
        dh1, dy, dwg, dwp, dwo, dpg = _ple_out_bwd(
            dh, s["h1"], s["gate"], s["pe"], p[l], s["ya"], s["yc"], row("ple_norm_g"), s["w_gate"], s["w_out"])
        ag_t = jnp.tile(row("attn_out_g"), (1, AD // DH))
        do, dga, dgc, dc1, dwpw, dag, dcg, dlg, dlb = _branch_bwd(
            dy, s["o"], s["u"], s["c1"], s["c3"], ag_t, row("conv_ln_g"), row("conv_ln_b"), s["w_pw"],
            row("conv_out_g"), seg)
        tail = [dwo.reshape(NCHIP, 256, D), dwg.reshape(NCHIP, 256, D), dwp, dwpw.reshape(NCHIP, 128, CD)]
        dcv, dcgate, ddw, ddb, halves = _conv_bwd(dc1, s["u"], s["dw"], tail if l == 0 else ())
        tail.append(ddw)
        if l == 0:
            partials[(l, "tail")] = _sum_pair(core, tail, halves + list(_pair_exchange([ddw])))
            pending.append((l, "tail"))
        send = [t for key in pending for t in partials[key]]
        dq, dk, dv, got = _attn_bwd(s["u"], do, s["tot"], send)
        for key in pending:
            arrived[key], got = got[:len(partials[key])], got[len(partials[key]):]
        pending = []
        du = jnp.concatenate([dq, dk, dv, dga, dcv, dcgate, dgc], axis=1)
        dwin = _inproj_dw(s["hn"], du)
        if l == 0:
            partials[(l, "w_in")] = pair_sum([dwin])
            in_flight = _scatter_start(partials[(l, "w_in")][0])
            dh, dng, _ = _inproj_dx(du, s["w_in"], s["h"], row("norm_g") + in_flight[4][0:1, 0:1], dh1)
            arrived[(l, "w_in")] = [_scatter_wait(*in_flight[:4], dh)]
        else:
            dh, dng, halves = _inproj_dx(du, s["w_in"], s["h"], row("norm_g"), dh1, grads=tail + [dwin])
            tail_p = _sum_pair(core, tail + [dwin], halves)
            partials[(l, "tail")], partials[(l, "w_in")] = tail_p[:-1], tail_p[-1:]
            pending = [(l, "tail"), (l, "w_in")]
        small[l] = dict(norm_g=dng, attn_out_g=dag.reshape(AD // DH, DH).sum(axis=0, keepdims=True), dw_b=ddb,
                        conv_ln_g=dlg, conv_ln_b=dlb, conv_out_g=dcg, ple_norm_g=dpg)
    both = lambda d: [d[(l, "w_in")] + d[(l, "tail")] for l in range(DEPTH)]
    big = dict(zip(BIG, _sum_chips_share(chip_idx, both(partials), both(arrived))))
    return loss_blk, dh, big, small, dfg


BIG = ("w_in", "w_out", "w_ple_gate", "w_ple", "w_pw", "dw_w")
SMALL2 = ("norm_g", "ple_norm_g", "dw_b", "conv_ln_g", "conv_ln_b", "conv_out_g", "attn_out_g")
SMALL_ROW = (0, 2, 4, 6, 8, 10, 12, 14)


def kernel(x, p, norm_g, w_in, attn_out_g, dw_w, dw_b, conv_ln_g, conv_ln_b, w_pw, conv_out_g, w_out, ple_norm_g, w_ple_gate, w_ple, final_g, loss_target, m_norm_g, m_w_in, m_attn_out_g, m_dw_w, m_dw_b, m_conv_ln_g, m_conv_ln_b, m_w_pw, m_conv_out_g, m_w_out, m_ple_norm_g, m_w_ple_gate, m_w_ple, m_final_g, v_norm_g, v_w_in, v_attn_out_g, v_dw_w, v_dw_b, v_conv_ln_g, v_conv_ln_b, v_w_pw, v_conv_out_g, v_w_out, v_ple_norm_g, v_w_ple_gate, v_w_ple, v_final_g):
    W = dict(norm_g=norm_g, w_in=w_in, attn_out_g=attn_out_g, dw_w=dw_w, dw_b=dw_b, conv_ln_g=conv_ln_g,
             conv_ln_b=conv_ln_b, w_pw=w_pw, conv_out_g=conv_out_g, w_out=w_out, ple_norm_g=ple_norm_g,
             w_ple_gate=w_ple_gate, w_ple=w_ple, final_g=final_g)
    M = dict(norm_g=m_norm_g, w_in=m_w_in, attn_out_g=m_attn_out_g, dw_w=m_dw_w, dw_b=m_dw_b,
             conv_ln_g=m_conv_ln_g, conv_ln_b=m_conv_ln_b, w_pw=m_w_pw, conv_out_g=m_conv_out_g, w_out=m_w_out,
             ple_norm_g=m_ple_norm_g, w_ple_gate=m_w_ple_gate, w_ple=m_w_ple, final_g=m_final_g)
    V = dict(norm_g=v_norm_g, w_in=v_w_in, attn_out_g=v_attn_out_g, dw_w=v_dw_w, dw_b=v_dw_b,
             conv_ln_g=v_conv_ln_g, conv_ln_b=v_conv_ln_b, w_pw=v_w_pw, conv_out_g=v_conv_out_g, w_out=v_w_out,
             ple_norm_g=v_ple_norm_g, w_ple_gate=v_w_ple_gate, w_ple=v_w_ple, final_g=v_final_g)
    order = ("norm_g", "w_in", "attn_out_g", "dw_w", "dw_b", "conv_ln_g", "conv_ln_b", "w_pw", "conv_out_g",
             "w_out", "ple_norm_g", "w_ple_gate", "w_ple", "final_g")

    pad_taps = lambda a: jnp.pad(a, ((0, 0), (0, CWP - CW), (0, 0)))
    cast = dict(w_in=w_in.astype(BF16), w_out=w_out.astype(BF16), w_ple_gate=w_ple_gate.astype(BF16),
                w_ple=w_ple.astype(BF16), w_pw=w_pw.astype(BF16), dw_w=pad_taps(dw_w))
    shards = [{k: v[l] for k, v in cast.items()} for l in range(DEPTH)]
    xi, yi, ci = lax.axis_index("x"), lax.axis_index("y"), lax.axis_index("c")
    chip = 2 * xi + yi

    sm = {k: W[k] for k in SMALL2}
    sm["final_g"] = final_g.reshape(1, D)
    loss_part, grad_x, big, small, dfg = _local_step(x[0], p[:, 0], loss_target[0], sm, shards, chip, ci)
    g_big = {name: big[name].reshape(cast[name].shape) for name in BIG}

    rows = [small[l][k] for k in SMALL2 for l in range(DEPTH)] + [dfg]
    small_names = SMALL2 + ("final_g",)
    as_rows = lambda t: t.reshape(1, D) if t.ndim == 1 else t
    results, loss_row = _small_adamw(
        _small_allreduce(rows, loss_part), [as_rows(W[k]) for k in small_names],
        [as_rows(M[k]) for k in small_names], [as_rows(V[k]) for k in small_names])
    loss = loss_row[0, 0]

    grads, deltas, new_m, new_v = {}, {}, {}, {}
    for name in BIG:
        wv = pad_taps(W[name]) if name == "dw_w" else W[name]
        mv = pad_taps(M[name]) if name == "dw_w" else M[name]
        vv = pad_taps(V[name]) if name == "dw_w" else V[name]
        gg = g_big[name]
        cols = wv.shape[-1]
        rows_total = wv.size // cols
        tile_rows = min(rows_total, 256)
        d2, m2, v2 = _adamw(wv.reshape(rows_total, cols), gg.reshape(rows_total, cols),
                            mv.reshape(rows_total, cols), vv.reshape(rows_total, cols), tile_rows)
        if name == "dw_w":
            cut = lambda a: a.reshape(DEPTH, CWP, LANE)[:, :CW]
            grads[name], deltas[name], new_m[name], new_v[name] = cut(gg), cut(d2), cut(m2), cut(v2)
        else:
            grads[name] = gg
            deltas[name], new_m[name], new_v[name] = (t.reshape(wv.shape) for t in (d2, m2, v2))
    for k, four in zip(small_names, results):
        grads[k], deltas[k], new_m[k], new_v[k] = (t.reshape(W[k].shape) for t in four)

    return (loss, grad_x[None], *[grads[n] for n in order], *[deltas[n] for n in order],
            *[new_m[n] for n in order], *[new_v[n] for n in order])
```

```python
import functools

import jax
import jax.numpy as jnp
from jax import lax
from jax.experimental import pallas as pl
from jax.experimental.pallas import tpu as pltpu

F32 = jnp.float32
BF16 = jnp.bfloat16

T = 2048
D = 1024
DIN = 3584
NCHIP = 4
SHW = DIN // NCHIP
AD = 512
CD = 512
DH = 64
CW = 31
CWP = 32
PLE = 256
DEPTH = 2
EPS = 1e-6
AQ = 256
HG = 4
GW = HG * DH
SR = HG * AQ
NG = AD // GW
LANE = 128
TM = 1024
TR = 256

ADAM_LR = 0.001
ADAM_B1 = 0.9
ADAM_B2 = 0.999
ADAM_EPS = 1e-08
ADAM_WD = 0.01
ADAM_STEP = 10

SMALL_PK = 16
LOSS_ROW = 15

VMEM_BIG = 56 * 1024 * 1024
MESH = pl.DeviceIdType.MESH


def _cp(sem=None, vmem=None):
    kw = {}
    if sem is not None:
        kw["dimension_semantics"] = sem
    if vmem is not None:
        kw["vmem_limit_bytes"] = vmem
    return pltpu.CompilerParams(**kw)


def _dot(a, b):
    return jnp.dot(a, b, preferred_element_type=F32)


def _dot_nt(a, b):
    return lax.dot_general(a, b, (((1,), (1,)), ((), ())), preferred_element_type=F32)


def _dot_tn(a, b):
    return lax.dot_general(a, b, (((0,), (0,)), ((), ())), preferred_element_type=F32)


def _dot2(x, m):
    hi = x.astype(BF16)
    lo = (x - hi.astype(F32)).astype(BF16)
    return _dot(hi, m) + _dot(lo, m)


def _sig(x):
    return 1.0 / (1.0 + jnp.exp(-x))


def _softplus(z):
    return jnp.maximum(z, 0.0) + jnp.log(1.0 + jnp.exp(-jnp.abs(z)))


def _rstd(x):
    return lax.rsqrt(jnp.mean(x * x, axis=-1, keepdims=True) + EPS)


def _rms_bwd(dy, x, r, g):
    dn = dy * g
    return r * dn - x * (r * r * r) * jnp.mean(dn * x, axis=-1, keepdims=True)


def _rms_inproj(h, g, w):
    tm = min(TM, T)

    def body(h_ref, g_ref, w_ref, u_ref, hn_ref, hn_s):
        @pl.when(pl.program_id(1) == 0)
        def _():
            hh = h_ref[...]
            hn = (hh * _rstd(hh) * g_ref[...]).astype(BF16)
            hn_s[...] = hn
            hn_ref[...] = hn
        u_ref[...] = _dot(hn_s[...], w_ref[0])

    return pl.pallas_call(
        body, name="rms_inproj", grid=(T // tm, NCHIP),
        in_specs=[pl.BlockSpec((tm, D), lambda i, k: (i, 0)),
                  pl.BlockSpec((1, D), lambda i, k: (0, 0)),
                  pl.BlockSpec((1, D, SHW), lambda i, k: (k, 0, 0))],
        out_specs=[pl.BlockSpec((tm, SHW), lambda i, k: (i, k)),
                   pl.BlockSpec((tm, D), lambda i, k: (i, 0))],
        out_shape=[jax.ShapeDtypeStruct((T, DIN), F32), jax.ShapeDtypeStruct((T, D), BF16)],
        scratch_shapes=[pltpu.VMEM((tm, D), BF16)],
        compiler_params=_cp(("arbitrary", "arbitrary"), VMEM_BIG),
    )(h, g, w)


def _attn_tiles():
    row = lax.broadcasted_iota(jnp.int32, (SR, AQ), 0) & (AQ - 1)
    col = lax.broadcasted_iota(jnp.int32, (SR, AQ), 1)
    tr = lax.broadcasted_iota(jnp.int32, (AQ, AQ), 0)
    tc = lax.broadcasted_iota(jnp.int32, (AQ, AQ), 1)
    lane_head = lax.broadcasted_iota(jnp.int32, (1, GW), 1) // DH
    return col < row, tr, tc, [lane_head == h for h in range(HG)]


def _stack_heads(t, heads):
    return jnp.concatenate([jnp.where(m, t, 0.0) for m in heads], axis=0)


def _unstack_heads(t, heads):
    out = t[:AQ]
    for h in range(1, HG):
        out = jnp.where(heads[h], t[h * AQ:(h + 1) * AQ], out)
    return out


def _tri_sum(x, tri):
    hi = x.astype(BF16)
    lo = (x - hi.astype(F32)).astype(BF16)
    both = _dot(jnp.concatenate([hi, lo], axis=0), tri)
    return both[:SR] + both[SR:]


def _scatter_copies(ps, gots, send_sems, recv_sems):
    x, y, c = _place()
    peers = [(1 - x, y), (x, 1 - y), (1 - x, 1 - y)]
    return [pltpu.make_async_remote_copy(
        src_ref=ps[a].at[2 * px + py], dst_ref=gots[a].at[r], send_sem=send_sems.at[3 * a + r],
        recv_sem=recv_sems.at[3 * a + r], device_id=(px, py, c), device_id_type=MESH)
        for a in range(len(ps)) for r, (px, py) in enumerate(peers)]


GATHER_SEMS = 7


def _gather_tree_copies(ins, outs, send_sems, recv_sems):
    x, y, c = _place()
    me, xn, yn, dg = 2 * x + y, 2 * (1 - x) + y, 2 * x + (1 - y), 2 * (1 - x) + (1 - y)
    to_x, to_y, sibling = (1 - x, y, c), (x, 1 - y, c), (x, y, 1 - c)
    direct, relayed, passed = [], [], []
    for a in range(len(ins)):
        half = ins[a].shape[0] // 2
        mine = pl.ds(c * half, half)
        first, second = pl.ds(c * half, half // 2), pl.ds(c * half + half // 2, half // 2)

        def copy(i, src, dst, to, k=GATHER_SEMS * a):
            return pltpu.make_async_remote_copy(src_ref=src, dst_ref=dst, send_sem=send_sems.at[k + i],
                                                recv_sem=recv_sems.at[k + i], device_id=to, device_id_type=MESH)

        own, slot = ins[a].at[mine], outs[a].at[me, mine]
        direct += [copy(0, own, slot, to_x), copy(1, own, slot, to_y)]
        relayed += [copy(2, outs[a].at[xn, first], outs[a].at[xn, first], to_y),
                    copy(3, outs[a].at[yn, second], outs[a].at[yn, second], to_x)]
        passed += [copy(4 + i, outs[a].at[j, mine], outs[a].at[j, mine], sibling) for i, j in enumerate((xn, yn, dg))]
    return direct, relayed, passed


def _pair_copies(ins, outs, send_sems, recv_sems):
    x, y, c = _place()
    copies = []
    for a in range(len(ins)):
        half = ins[a].shape[1] // 2
        copies.append(pltpu.make_async_remote_copy(
            src_ref=ins[a].at[:, pl.ds((1 - c) * half, half), :], dst_ref=outs[a], send_sem=send_sems.at[a],
            recv_sem=recv_sems.at[a], device_id=(x, y, 1 - c), device_id_type=MESH))
    return copies


def _host(body, grid, n_in, n_out, n_x, make_copies, mids=()):
    if not n_x:
        return body

    def hosting(*refs):
        a, b = n_in + n_x, n_in + 2 * n_x + n_out
        copies = make_copies(refs[n_in:a], refs[a + n_out:b], refs[-2], refs[-1])
        stages = copies if isinstance(copies, tuple) else (copies,)
        ids = [pl.program_id(d) for d in range(len(grid))]
        at = lambda step: functools.reduce(jnp.logical_and, [i == s for i, s in zip(ids, step)])

        @pl.when(at([0] * len(grid)))
        def _():
            for cp in stages[0]:
                cp.start()

        for before, after, step in zip(stages, stages[1:], mids):
            @pl.when(at(step))
            def _(before=before, after=after):
                for cp in before:
                    cp.wait_recv()
                for cp in after:
                    cp.start()

        body(*refs[:n_in], *refs[a:a + n_out], *refs[b:-2])

        @pl.when(at([g - 1 for g in grid]))
        def _():
            for cp in stages[-1]:
                cp.wait_recv()
            for stage in stages:
                for cp in stage:
                    cp.wait_send()

    return hosting


def _hosted_sems(n_x, per_array=3):
    n = per_array * n_x
    return [pltpu.SemaphoreType.DMA((n,)), pltpu.SemaphoreType.DMA((n,))] if n_x else []


RC = 256


def _chunk_causal(r):
    row = lax.broadcasted_iota(jnp.int32, (RC, AQ), 0) + (r * RC) % AQ
    return lax.broadcasted_iota(jnp.int32, (RC, AQ), 1) < row


def _attn_fwd(u, agw, shards=()):
    n = len(shards)
    grid = (T // AQ,)

    def body(q_ref, k_ref, v_ref, g_ref, ag_ref, o_ref, y_ref, tot_ref,
             kb_s, vb_s, qs_s, z_s, zs_s, lmb_s, suf_s, att_s, acc_s, run_s):
        qi = pl.program_id(0)

        @pl.when(qi == 0)
        def _():
            kb_s[...] = k_ref[...].astype(BF16)
            vb_s[...] = v_ref[...].astype(BF16)

        _, tr, tc, heads = _attn_tiles()
        upper = (tr > tc).astype(BF16)
        same_head = ((tr // DH) == (tc // DH)).astype(BF16)
        for g in range(NG):
            qs_s[g] = _stack_heads(q_ref[:, g * GW:(g + 1) * GW] * 0.125, heads).astype(BF16)
        acc_s[...] = jnp.zeros_like(acc_s)
        run_s[...] = jnp.zeros_like(run_s)

        def block(kb, masked):
            k0 = pl.multiple_of(kb * AQ, AQ)
            for g in range(NG):
                lanes = pl.ds(g * GW, GW)
                z_s[g] = _dot_nt(qs_s[g], kb_s[pl.ds(k0, AQ), lanes])
                for r in range(SR // RC):
                    rows = pl.ds(r * RC, RC)
                    z = z_s[g, rows, :]
                    zs = jnp.minimum(z, 0.0) - jnp.log(1.0 + jnp.exp(-jnp.abs(z)))
                    lm = zs - z
                    if masked:
                        lm = jnp.where(_chunk_causal(r), lm, 0.0)
                    run = run_s[g, rows, :]
                    zs_s[g, rows, :] = zs + run[:, 0:1]
                    hi = lm.astype(BF16)
                    lmb_s[g, rows, :] = hi
                    lmb_s[g, pl.ds(SR + r * RC, RC), :] = (lm - hi.astype(F32)).astype(BF16)
                    run_s[g, rows, :] = run + jnp.sum(lm, axis=1, keepdims=True)
                suf_s[g] = _dot(lmb_s[g], upper)
                for r in range(SR // RC):
                    rows = pl.ds(r * RC, RC)
                    att = jnp.exp(zs_s[g, rows, :] + suf_s[g, rows, :] + suf_s[g, pl.ds(SR + r * RC, RC), :])
                    if masked:
                        att = jnp.where(_chunk_causal(r), att, 0.0)
                    att_s[g, rows, :] = att.astype(BF16)
                acc_s[g] += _dot(att_s[g], vb_s[pl.ds(k0, AQ), lanes])

        block(qi, True)

        def step(i, c):
            block(qi - 1 - i, False)
            return c

        lax.fori_loop(0, qi, step, 0)
        gate = g_ref[...]
        agv = ag_ref[...]
        for g in range(NG):
            lanes = slice(g * GW, (g + 1) * GW)
            o = _unstack_heads(acc_s[g], heads)
            osq = o * o
            ms = _dot2(osq, same_head)
            gg = gate[:, lanes]
            o_ref[:, lanes] = o
            y_ref[:, lanes] = (o * lax.rsqrt(ms * (1.0 / DH) + EPS) * agv[:, lanes] * (gg * _sig(gg))).astype(BF16)
            tot_ref[g] = _unstack_heads(jnp.broadcast_to(run_s[g][:, 0:1], (SR, GW)), heads)

    tile = lambda dt, rows=SR: pltpu.VMEM((NG, rows, AQ), dt)
    scratch = [pltpu.VMEM((T, AD), BF16), pltpu.VMEM((T, AD), BF16), pltpu.VMEM((NG, SR, GW), BF16),
               tile(F32), tile(F32), tile(BF16, 2 * SR), tile(F32, 2 * SR), tile(BF16),
               pltpu.VMEM((NG, SR, GW), F32), pltpu.VMEM((NG, SR, LANE), F32)]
    col = lambda j: pl.BlockSpec((AQ, AD), lambda qi: (qi, j))
    res = pl.pallas_call(
        _host(body, grid, 5, 3, n, _gather_tree_copies, mids=((grid[0] * 5 // 8,), (grid[0] * 7 // 8,))), name="attn_fwd", grid=grid,
        in_specs=[col(0), pl.BlockSpec((T, AD), lambda qi: (0, 1)), pl.BlockSpec((T, AD), lambda qi: (0, 2)),
                  col(3), pl.BlockSpec((1, AD), lambda qi: (0, 0))] + [HBM_SPEC] * n,
        out_specs=[col(0), col(0), pl.BlockSpec((NG, AQ, GW), lambda qi: (0, qi, 0))] + [HBM_SPEC] * n,
        out_shape=[jax.ShapeDtypeStruct((T, AD), F32), jax.ShapeDtypeStruct((T, AD), BF16),
                   jax.ShapeDtypeStruct((NG, T, GW), F32)]
        + [jax.ShapeDtypeStruct((NCHIP,) + s.shape, s.dtype) for s in shards],
        scratch_shapes=scratch + _hosted_sems(n, GATHER_SEMS),
        compiler_params=_cp(("arbitrary",), VMEM_BIG),
    )(u, u, u, u, agw, *shards)
    return res[0], res[1], res[2], list(res[3:])


def _glu_conv(u, dw, db):
    tr = 256

    def body(cv_ref, cg_ref, w_ref, b_ref, c1_ref, pad_s):
        pad_s[pl.ds(0, CWP), :] = jnp.zeros((CWP, LANE), F32)
        pad_s[pl.ds(CWP, T), :] = cv_ref[...] * _sig(cg_ref[...])
        wv = w_ref[0]
        bias = b_ref[...]

        def tile(i, carry):
            r0 = pl.multiple_of(i * tr, tr)
            acc = jnp.zeros((tr, LANE), F32) + bias
            for w in range(CW):
                acc = acc + pad_s[pl.ds(r0 + (CWP - CW + 1) + w, tr), :] * wv[w:w + 1, :]
            c1_ref[pl.ds(r0, tr), :] = acc
            return carry

        lax.fori_loop(0, T // tr, tile, 0)

    return pl.pallas_call(
        body, name="glu_conv", grid=(CD // LANE,),
        in_specs=[pl.BlockSpec((T, LANE), lambda cb: (0, 16 + cb)),
                  pl.BlockSpec((T, LANE), lambda cb: (0, 20 + cb)),
                  pl.BlockSpec((1, CWP, LANE), lambda cb: (cb, 0, 0)),
                  pl.BlockSpec((1, LANE), lambda cb: (0, cb))],
        out_specs=pl.BlockSpec((T, LANE), lambda cb: (0, cb)),
        out_shape=jax.ShapeDtypeStruct((T, CD), F32),
        scratch_shapes=[pltpu.VMEM((T + CWP, LANE), F32)],
        compiler_params=_cp(("arbitrary",)),
    )(u, u, dw, db)


def _ln_silu(c1, lg, lb):
    mu = jnp.mean(c1, axis=-1, keepdims=True)
    xc = c1 - mu
    rs = lax.rsqrt(jnp.mean(xc * xc, axis=-1, keepdims=True) + EPS)
    xh = xc * rs
    ln = xh * lg + lb
    s = _sig(ln)
    return xh, rs, ln, s


def _layer_tail(c1, u, ya, h, p, lg, lb, wpw, cg, wout, pg, wgate, wple, head=None):
    tm = min(TR, T)

    def body(c1_ref, gc_ref, ya_ref, h_ref, p_ref, lg_ref, lb_ref, wpw_ref, cg_ref, wout_ref,
             pg_ref, wgate_ref, wple_ref, *rest):
        c3_ref, yc_ref, h1_ref, gate_ref, pe_ref, h2_ref = rest[-6 - 2 * bool(head):][:6]
        _, _, ln, s = _ln_silu(c1_ref[...], lg_ref[...], lb_ref[...])
        c2 = (ln * s).astype(BF16)
        c3 = _dot(c2, wpw_ref[...])
        gc = gc_ref[...]
        yc = (c3 * _rstd(c3) * cg_ref[...] * (gc * _sig(gc))).astype(BF16)
        c3_ref[...] = c3
        yc_ref[...] = yc
        y = _dot(ya_ref[...], wout_ref[pl.ds(0, AD), :]) + _dot(yc, wout_ref[pl.ds(AD, CD), :])
        h1 = h_ref[...] + y
        hn2 = (h1 * _rstd(h1) * pg_ref[...]).astype(BF16)
        gate = _sig(_dot(hn2, wgate_ref[...]))
        pb = p_ref[...].astype(BF16)
        pe = jnp.concatenate([_dot(pb, wple_ref[k]) for k in range(NCHIP)], axis=1)
        h1_ref[...] = h1
        gate_ref[...] = gate.astype(BF16)
        pe_ref[...] = pe.astype(BF16)
        h2 = h1 + pe * gate
        if not head:
            h2_ref[...] = h2
            return
        t_ref, fg_ref, loss_ref, dfg_ref = rest[0], rest[1], rest[-2], rest[-1]

        @pl.when(pl.program_id(0) == 0)
        def _():
            loss_ref[...] = jnp.zeros_like(loss_ref)
            dfg_ref[...] = jnp.zeros_like(dfg_ref)
        fg = fg_ref[...]
        r = _rstd(h2)
        e = h2 * r * fg - t_ref[...]
        loss_ref[...] += 0.5 * jnp.sum(jnp.mean(e * e, axis=-1, keepdims=True))
        dy = e * (1.0 / D)
        dfg_ref[...] += jnp.sum(dy * h2 * r, axis=0, keepdims=True)
        h2_ref[...] = _rms_bwd(dy, h2, r, fg)

    row = lambda w: pl.BlockSpec((tm, w), lambda i: (i, 0))
    full = lambda *s: pl.BlockSpec(s, lambda i: (0,) * len(s), pipeline_mode=pl.Buffered(1))
    extra = bool(head)
    return pl.pallas_call(
        body, name="layer_tail", grid=(T // tm,),
        in_specs=[row(CD), pl.BlockSpec((tm, CD), lambda i: (i, 6)), row(AD), row(D), row(PLE),
                  full(1, CD), full(1, CD), full(CD, CD), full(1, CD), full(D, D),
                  full(1, D), full(D, D), full(NCHIP, PLE, PLE)] + [row(D), full(1, D)] * extra,
        out_specs=[row(CD), row(CD), row(D), row(D), row(D), row(D)] + [full(8, LANE), full(1, D)] * extra,
        out_shape=[jax.ShapeDtypeStruct((T, CD), F32), jax.ShapeDtypeStruct((T, CD), BF16),
                   jax.ShapeDtypeStruct((T, D), F32), jax.ShapeDtypeStruct((T, D), BF16),
                   jax.ShapeDtypeStruct((T, D), BF16), jax.ShapeDtypeStruct((T, D), F32)]
        + [jax.ShapeDtypeStruct((8, LANE), F32), jax.ShapeDtypeStruct((1, D), F32)] * extra,
        compiler_params=_cp(("arbitrary",), VMEM_BIG),
    )(c1, u, ya, h, p, lg, lb, wpw, cg, wout, pg, wgate, wple, *(head or ()))


def _ple_out_bwd(dh2, h1, gate, pe, p, ya, yc, pg, wgate, wout):
    tm = min(TR, T)

    def body(dh2_ref, h1_ref, gate_ref, pe_ref, p_ref, ya_ref, yc_ref, pg_ref, wgate_ref, wout_ref,
             dh1_ref, dy_ref, dwg_ref, dwp_ref, dwo_ref, dpg_ref):
        @pl.when(pl.program_id(0) == 0)
        def _():
            dwg_ref[...] = jnp.zeros_like(dwg_ref)
            dwp_ref[...] = jnp.zeros_like(dwp_ref)
            dwo_ref[...] = jnp.zeros_like(dwo_ref)
            dpg_ref[...] = jnp.zeros_like(dpg_ref)
        dh2 = dh2_ref[...]
        h1 = h1_ref[...]
        gate = gate_ref[...].astype(F32)
        pg = pg_ref[...]
        dpe = (dh2 * gate).astype(BF16)
        dgp = (dh2 * pe_ref[...].astype(F32) * gate * (1.0 - gate)).astype(BF16)
        r = _rstd(h1)
        hn = h1 * r
        dwg_ref[...] += _dot_tn((hn * pg).astype(BF16), dgp)
        dhn2 = _dot_nt(dgp, wgate_ref[...])
        dpg_ref[...] += jnp.sum(dhn2 * hn, axis=0, keepdims=True)
        dh1 = dh2 + _rms_bwd(dhn2, h1, r, pg)
        pb = p_ref[...].astype(BF16)
        for k in range(NCHIP):
            dwp_ref[k] += _dot_tn(pb, dpe[:, k * PLE:(k + 1) * PLE])
        dh1b = dh1.astype(BF16)
        dy_ref[...] = _dot_nt(dh1b, wout_ref[...])
        dwo_ref[pl.ds(0, AD), :] += _dot_tn(ya_ref[...], dh1b)
        dwo_ref[pl.ds(AD, CD), :] += _dot_tn(yc_ref[...], dh1b)
        dh1_ref[...] = dh1

    row = lambda w: pl.BlockSpec((tm, w), lambda i: (i, 0))
    full = lambda *s: pl.BlockSpec(s, lambda i: (0,) * len(s), pipeline_mode=pl.Buffered(1))
    return pl.pallas_call(
        body, name="ple_out_bwd", grid=(T // tm,),
        in_specs=[row(D), row(D), row(D), row(D), row(PLE), row(AD), row(CD),
                  full(1, D), full(D, D), full(D, D)],
        out_specs=[row(D), row(D), full(D, D), full(NCHIP, PLE, PLE), full(D, D), full(1, D)],
        out_shape=[jax.ShapeDtypeStruct((T, D), F32), jax.ShapeDtypeStruct((T, D), F32),
                   jax.ShapeDtypeStruct((D, D), F32), jax.ShapeDtypeStruct((NCHIP, PLE, PLE), F32),
                   jax.ShapeDtypeStruct((D, D), F32), jax.ShapeDtypeStruct((1, D), F32)],
        compiler_params=_cp(("arbitrary",), VMEM_BIG),
    )(dh2, h1, gate, pe, p, ya, yc, pg, wgate, wout)


def _branch_bwd(dy, o, u, c1, c3, ag, lg, lb, wpw, cg, seg):
    tm = min(TR, T)

    def body(dya_ref, dyc_ref, o_ref, ga_ref, gc_ref, c1_ref, c3_ref, ag_ref, lg_ref, lb_ref, wpw_ref,
             cg_ref, seg_ref, do_ref, dga_ref, dgc_ref, dc1_ref, dwpw_ref, dag_ref, dcg_ref, dlg_ref, dlb_ref):
        @pl.when(pl.program_id(0) == 0)
        def _():
            for r_ in (dwpw_ref, dag_ref, dcg_ref, dlg_ref, dlb_ref):
                r_[...] = jnp.zeros_like(r_)
        dya = dya_ref[...]
        o = o_ref[...]
        ga = ga_ref[...]
        ag_v = ag_ref[...]
        seg_m = seg_ref[...]
        r = lax.rsqrt(_dot2(o * o, seg_m) * (1.0 / DH) + EPS)
        onr = o * r
        sg = _sig(ga)
        dga_ref[...] = (dya * (onr * ag_v) * (sg * (1.0 + ga * (1.0 - sg)))).astype(BF16)
        don = dya * (ga * sg)
        dag_ref[...] += jnp.sum(don * onr, axis=0, keepdims=True)
        dn = don * ag_v
        do_ref[...] = r * dn - o * (r * r * r) * (_dot2(dn * o, seg_m) * (1.0 / DH))
        dyc = dyc_ref[...]
        c3 = c3_ref[...]
        gc = gc_ref[...]
        cg_v = cg_ref[...]
        r3 = _rstd(c3)
        cn = c3 * r3
        sc = _sig(gc)
        dgc_ref[...] = (dyc * (cn * cg_v) * (sc * (1.0 + gc * (1.0 - sc)))).astype(BF16)
        dcn = dyc * (gc * sc)
        dcg_ref[...] += jnp.sum(dcn * cn, axis=0, keepdims=True)
        dc3 = _rms_bwd(dcn, c3, r3, cg_v).astype(BF16)
        lg_v = lg_ref[...]
        xh, rs, ln, s = _ln_silu(c1_ref[...], lg_v, lb_ref[...])
        c2 = (ln * s).astype(BF16)
        dwpw_ref[...] += _dot_tn(c2, dc3)
        dc2 = _dot_nt(dc3, wpw_ref[...])
        dln = dc2 * (s * (1.0 + ln * (1.0 - s)))
        dlb_ref[...] += jnp.sum(dln, axis=0, keepdims=True)
        dlg_ref[...] += jnp.sum(dln * xh, axis=0, keepdims=True)
        dxh = dln * lg_v
        dc1_ref[...] = rs * (dxh - jnp.mean(dxh, axis=-1, keepdims=True)
                             - xh * jnp.mean(dxh * xh, axis=-1, keepdims=True))

    half = lambda j: pl.BlockSpec((tm, 512), lambda i: (i, j))
    full = lambda *s: pl.BlockSpec(s, lambda i: (0,) * len(s), pipeline_mode=pl.Buffered(1))
    vec = jax.ShapeDtypeStruct((1, 512), F32)
    act = jax.ShapeDtypeStruct((T, 512), F32)
    return pl.pallas_call(
        body, name="branch_bwd", grid=(T // tm,),
        in_specs=[half(0), half(1), half(0), half(3), half(6), half(0), half(0),
                  full(1, AD), full(1, CD), full(1, CD), full(CD, CD), full(1, CD), full(AD, AD)],
        out_specs=[half(0), half(0), half(0), half(0), full(CD, CD), full(1, 512), full(1, 512),
                   full(1, 512), full(1, 512)],
        out_shape=[act, jax.ShapeDtypeStruct((T, 512), BF16), jax.ShapeDtypeStruct((T, 512), BF16), act,
                   jax.ShapeDtypeStruct((CD, CD), F32), vec, vec, vec, vec],
        compiler_params=_cp(("arbitrary",), VMEM_BIG),
    )(dy, dy, o, u, u, c1, c3, ag, lg, lb, wpw, cg, seg)


def _conv_bwd(dc1, u, dw, grads=()):
    tr = 64
    n_x = len(grads)
    grid = (CD // LANE,)
    off = CWP - CW + 1

    def body(d_ref, cv_ref, cg_ref, w_ref, dcv_ref, dcg_ref, ddw_ref, ddb_ref, padc_s, padd_s, acc_s):
        cv = cv_ref[...]
        sg = _sig(cg_ref[...])
        padc_s[pl.ds(0, CWP), :] = jnp.zeros((CWP, LANE), F32)
        padc_s[pl.ds(CWP, T), :] = cv * sg
        padd_s[pl.ds(0, T), :] = d_ref[...]
        padd_s[pl.ds(T, CWP), :] = jnp.zeros((CWP, LANE), F32)
        acc_s[...] = jnp.zeros_like(acc_s)
        wv = w_ref[0]

        def tile(i, carry):
            r0 = pl.multiple_of(i * tr, tr)
            dt = padd_s[pl.ds(r0, tr), :]
            dc0 = jnp.zeros((tr, LANE), F32)
            for w in range(CW):
                dc0 = dc0 + padd_s[pl.ds(r0 + (CW - 1) - w, tr), :] * wv[w:w + 1, :]
                prod = dt * padc_s[pl.ds(r0 + off + w, tr), :]
                acc_s[w] += jnp.sum(prod.reshape(tr // 8, 8, LANE), axis=0)
            cvt = cv_ref[pl.ds(r0, tr), :]
            sgt = _sig(cg_ref[pl.ds(r0, tr), :])
            dcv_ref[pl.ds(r0, tr), :] = (dc0 * sgt).astype(BF16)
            dcg_ref[pl.ds(r0, tr), :] = (dc0 * cvt * sgt * (1.0 - sgt)).astype(BF16)
            return carry

        lax.fori_loop(0, T // tr, tile, 0)
        ddw_ref[0] = jnp.sum(acc_s[...], axis=1)
        ddb_ref[...] = jnp.sum(d_ref[...], axis=0, keepdims=True)

    col = lambda j: pl.BlockSpec((T, LANE), lambda cb: (0, j + cb))
    res = pl.pallas_call(
        _host(body, grid, 4, 4, n_x, _pair_copies), name="conv_bwd", grid=grid,
        in_specs=[col(0), col(16), col(20), pl.BlockSpec((1, CWP, LANE), lambda cb: (cb, 0, 0))] + [HBM_SPEC] * n_x,
        out_specs=[col(0), col(0), pl.BlockSpec((1, CWP, LANE), lambda cb: (cb, 0, 0)),
                   pl.BlockSpec((1, LANE), lambda cb: (0, cb))] + [HBM_SPEC] * n_x,
        out_shape=[jax.ShapeDtypeStruct((T, CD), BF16), jax.ShapeDtypeStruct((T, CD), BF16),
                   jax.ShapeDtypeStruct((NCHIP, CWP, LANE), F32), jax.ShapeDtypeStruct((1, CD), F32)]
        + [jax.ShapeDtypeStruct((NCHIP, g.shape[1] // 2, g.shape[2]), F32) for g in grads],
        scratch_shapes=[pltpu.VMEM((T + CWP, LANE), F32), pltpu.VMEM((T + CWP, LANE), F32),
                        pltpu.VMEM((CWP, 8, LANE), F32)] + _hosted_sems(n_x),
        compiler_params=_cp(("arbitrary",)),
    )(dc1, u, u, dw, *grads)
    return res[0], res[1], res[2], res[3], list(res[4:])


def _attn_bwd(u, do, tot, partials=()):
    n_x = len(partials)
    grid = (T // AQ,)

    def body(q_ref, k_ref, v_ref, do_ref, tot_ref, dq_ref, dk_ref, dv_ref, kb_s, vb_s, dk_s, dv_s):
        qi = pl.program_id(0)

        @pl.when(qi == 0)
        def _():
            kb_s[...] = k_ref[...].astype(BF16)
            vb_s[...] = v_ref[...].astype(BF16)
            dk_s[...] = jnp.zeros_like(dk_s)
            dv_s[...] = jnp.zeros_like(dv_s)

        causal, tr, tc, heads = _attn_tiles()
        upper = (tr > tc).astype(BF16)
        lower = (tr < tc).astype(BF16)
        qs, qus, dos, tots = [], [], [], []
        for g in range(NG):
            lanes = slice(g * GW, (g + 1) * GW)
            q = q_ref[:, lanes]
            qs.append(_stack_heads(q * 0.125, heads).astype(BF16))
            qus.append(_stack_heads(q, heads).astype(BF16))
            dos.append(_stack_heads(do_ref[:, lanes], heads).astype(BF16))
            totv = tot_ref[g]
            tots.append(jnp.concatenate([totv[:, h * DH:h * DH + 1] for h in range(HG)], axis=0))

        def block(kb, carry, masked):
            k0 = pl.multiple_of(kb * AQ, AQ)
            out = []
            for g in range(NG):
                lanes = pl.ds(g * GW, GW)
                lm_left, dl_left, dq = carry[g]
                kk = kb_s[pl.ds(k0, AQ), lanes]
                vv = vb_s[pl.ds(k0, AQ), lanes]
                z = _dot_nt(qs[g], kk)
                sp = _softplus(z)
                lm = jnp.where(causal, -sp, 0.0) if masked else -sp
                lm_incl = lm_left + jnp.sum(lm, axis=1, keepdims=True)
                att = jnp.exp((z - sp) + _tri_sum(lm, upper) + (tots[g] - lm_incl))
                if masked:
                    att = jnp.where(causal, att, 0.0)
                dl = att * _dot_nt(dos[g], vv)
                dv_s[pl.ds(k0, AQ), lanes] += _dot_tn(att.astype(BF16), dos[g])
                prefix = dl_left + _tri_sum(dl, lower)
                beta = jnp.exp(z - sp)
                dz = (1.0 - beta) * dl - beta * prefix
                if masked:
                    dz = jnp.where(causal, dz, 0.0)
                dzs = (dz * 0.125).astype(BF16)
                dk_s[pl.ds(k0, AQ), lanes] += _dot_tn(dzs, qus[g])
                out.append((lm_incl, dl_left + jnp.sum(dl, axis=1, keepdims=True), dq + _dot(dzs, kk)))
            return tuple(out)

        zero = jnp.zeros((SR, 1), F32)
        init = tuple((zero, zero, jnp.zeros((SR, GW), F32)) for _ in range(NG))
        carry = lax.fori_loop(0, qi, lambda kb, c: block(kb, c, False), init)
        carry = block(qi, carry, True)
        for g in range(NG):
            dq_ref[:, g * GW:(g + 1) * GW] = _unstack_heads(carry[g][2], heads).astype(BF16)

        @pl.when(qi == grid[0] - 1)
        def _():
            dk_ref[...] = dk_s[...].astype(BF16)
            dv_ref[...] = dv_s[...].astype(BF16)

    col = lambda j: pl.BlockSpec((AQ, AD), lambda qi: (qi, j))
    whole = lambda j: pl.BlockSpec((T, AD), lambda qi: (0, j), pipeline_mode=pl.Buffered(1))
    res = pl.pallas_call(
        _host(body, grid, 5, 3, n_x, _scatter_copies), name="attn_bwd", grid=grid,
        in_specs=[col(0), whole(1), whole(2), col(0), pl.BlockSpec((NG, AQ, GW), lambda qi: (0, qi, 0))]
        + [HBM_SPEC] * n_x,
        out_specs=[col(0), whole(0), whole(0)] + [HBM_SPEC] * n_x,
        out_shape=[jax.ShapeDtypeStruct((T, AD), BF16)] * 3
        + [jax.ShapeDtypeStruct((NCHIP - 1,) + a.shape[1:], a.dtype) for a in partials],
        scratch_shapes=[pltpu.VMEM((T, AD), BF16), pltpu.VMEM((T, AD), BF16), pltpu.VMEM((T, AD), F32),
                        pltpu.VMEM((T, AD), F32)] + _hosted_sems(n_x),
        compiler_params=_cp(("arbitrary",), VMEM_BIG),
    )(u, u, u, do, tot, *partials)
    return res[0], res[1], res[2], list(res[3:])


def _inproj_dw(hn, du):
    tm = min(TM, T)

    def body(hn_ref, du_ref, dw_ref):
        @pl.when(pl.program_id(1) == 0)
        def _():
            dw_ref[...] = jnp.zeros_like(dw_ref)
        dw_ref[0] += _dot_tn(hn_ref[...], du_ref[...])

    return pl.pallas_call(
        body, name="inproj_dw", grid=(NCHIP, T // tm),
        in_specs=[pl.BlockSpec((tm, D), lambda k, i: (i, 0)), pl.BlockSpec((tm, SHW), lambda k, i: (i, k))],
        out_specs=pl.BlockSpec((1, D, SHW), lambda k, i: (k, 0, 0)),
        out_shape=jax.ShapeDtypeStruct((NCHIP, D, SHW), F32),
        compiler_params=_cp(("arbitrary", "arbitrary"), VMEM_BIG),
    )(hn, du)


def _inproj_dx(du, w, h, g, dres, partials=(), grads=()):
    tm = min(TM, T)
    sent = list(partials) + list(grads)
    n_x = len(sent)
    grid = (T // tm, NCHIP)
    if grads:
        landing = [jax.ShapeDtypeStruct((NCHIP, a.shape[1] // 2, a.shape[2]), F32) for a in grads]
    else:
        landing = [jax.ShapeDtypeStruct((NCHIP - 1,) + a.shape[1:], a.dtype) for a in partials]

    def body(du_ref, w_ref, h_ref, g_ref, dres_ref, dh_ref, dg_ref, acc_s):
        i, k = pl.program_id(0), pl.program_id(1)

        @pl.when(jnp.logical_and(i == 0, k == 0))
        def _():
            dg_ref[...] = jnp.zeros_like(dg_ref)

        @pl.when(k == 0)
        def _():
            acc_s[...] = _dot_nt(du_ref[...], w_ref[0])

        @pl.when(k > 0)
        def _():
            acc_s[...] += _dot_nt(du_ref[...], w_ref[0])

        @pl.when(k == NCHIP - 1)
        def _():
            hh = h_ref[...]
            r = _rstd(hh)
            dhn = acc_s[...]
            dg_ref[...] += jnp.sum(dhn * hh * r, axis=0, keepdims=True)
            dh_ref[...] = dres_ref[...] + _rms_bwd(dhn, hh, r, g_ref[...])

    res = pl.pallas_call(
        _host(body, grid, 5, 2, n_x, _pair_copies if grads else _scatter_copies), name="inproj_dx", grid=grid,
        in_specs=[pl.BlockSpec((tm, SHW), lambda i, k: (i, k)),
                  pl.BlockSpec((1, D, SHW), lambda i, k: (k, 0, 0)),
                  pl.BlockSpec((tm, D), lambda i, k: (i, 0)),
                  pl.BlockSpec((1, D), lambda i, k: (0, 0)),
                  pl.BlockSpec((tm, D), lambda i, k: (i, 0))] + [HBM_SPEC] * n_x,
        out_specs=[pl.BlockSpec((tm, D), lambda i, k: (i, 0)), pl.BlockSpec((1, D), lambda i, k: (0, 0))]
        + [HBM_SPEC] * n_x,
        out_shape=[jax.ShapeDtypeStruct((T, D), F32), jax.ShapeDtypeStruct((1, D), F32)] + landing,
        scratch_shapes=[pltpu.VMEM((tm, D), F32)] + _hosted_sems(n_x),
        compiler_params=_cp(("arbitrary", "arbitrary"), VMEM_BIG),
    )(du, w, h, g, dres, *sent)
    return res[0], res[1], list(res[2:])


def _sum_pair(core, grads, gots):
    n = len(grads)

    def body(c_ref, *refs):
        for a in range(n):
            refs[2 * n + a][...] = (refs[a][...] + refs[n + a][...]).astype(BF16)

    mine = [pl.BlockSpec((1,) + s.shape[1:], lambda k, c: (k, c[0], 0)) for s in gots]
    same = [pl.BlockSpec((1,) + s.shape[1:], lambda k, c: (k, 0, 0)) for s in gots]
    return pl.pallas_call(
        body, name="sum_pair",
        grid_spec=pltpu.PrefetchScalarGridSpec(
            num_scalar_prefetch=1, grid=(NCHIP,), in_specs=mine + same, out_specs=same),
        out_shape=[jax.ShapeDtypeStruct(s.shape, BF16) for s in gots],
        compiler_params=_cp(("arbitrary",), VMEM_BIG),
    )(core, *grads, *gots)


def _sum_chips_share(chip, partials, gots):
    flat_p = [p for layer in partials for p in layer]
    flat_g = [g for layer in gots for g in layer]
    n, per_layer = len(flat_p), len(partials[0])

    def body(c_ref, *refs):
        full, sums = refs[2 * n:2 * n + per_layer], refs[2 * n + per_layer:3 * n + per_layer]
        send_sems, recv_sems, local_sems = refs[3 * n + per_layer:]
        x, y, c = _place()
        copies = []
        for i in range(n):
            acc = refs[i][0].astype(F32)
            for j in range(NCHIP - 1):
                acc = acc + refs[n + i][j].astype(F32)
            sums[i][...] = acc
            half = flat_p[i].shape[1]
            rows = full[i % per_layer].at[i // per_layer, pl.ds(c * half, half)]
            copies.append(pltpu.make_async_copy(sums[i], rows, local_sems.at[i]))
            copies.append(pltpu.make_async_remote_copy(
                src_ref=sums[i], dst_ref=rows, send_sem=send_sems.at[i], recv_sem=recv_sems.at[i],
                device_id=(x, y, 1 - c), device_id_type=MESH))
        for cp in copies:
            cp.start()
        for cp in copies:
            cp.wait()

    return pl.pallas_call(
        body, name="sum_chips_share",
        grid_spec=pltpu.PrefetchScalarGridSpec(
            num_scalar_prefetch=1, grid=(1,),
            in_specs=[pl.BlockSpec((1,) + s.shape[1:], lambda i, c: (c[0], 0, 0)) for s in flat_p]
            + [pl.BlockSpec(s.shape, lambda i, c: (0, 0, 0)) for s in flat_g],
            out_specs=[HBM_SPEC] * per_layer,
            scratch_shapes=[pltpu.VMEM(s.shape[1:], F32) for s in flat_p]
            + [pltpu.SemaphoreType.DMA((n,)), pltpu.SemaphoreType.DMA((n,)), pltpu.SemaphoreType.DMA((n,))]),
        out_shape=[jax.ShapeDtypeStruct((len(partials), 2 * s.shape[1], s.shape[2]), F32) for s in partials[0]],
        compiler_params=_cp(("arbitrary",), VMEM_BIG),
    )(chip, *flat_p, *flat_g)


def _adam_math(w, g, m, v):
    nm = ADAM_B1 * m + (1.0 - ADAM_B1) * g
    nv = ADAM_B2 * v + (1.0 - ADAM_B2) * (g * g)
    m_hat = nm / (1.0 - ADAM_B1 ** ADAM_STEP)
    v_hat = nv / (1.0 - ADAM_B2 ** ADAM_STEP)
    return -ADAM_LR * (m_hat / (jnp.sqrt(v_hat) + ADAM_EPS) + ADAM_WD * w), nm, nv


def _adamw(w, g, m, v, rows):
    R, C = w.shape

    def body(w_ref, g_ref, m_ref, v_ref, d_ref, nm_ref, nv_ref):
        d_ref[...], nm_ref[...], nv_ref[...] = _adam_math(w_ref[...], g_ref[...], m_ref[...], v_ref[...])

    spec = pl.BlockSpec((rows, C), lambda i: (i, 0))
    sh = jax.ShapeDtypeStruct((R, C), F32)
    return pl.pallas_call(
        body, name="adamw", grid=(R // rows,), in_specs=[spec] * 4, out_specs=[spec] * 3,
        out_shape=[sh, sh, sh], compiler_params=_cp(("arbitrary",)),
    )(w, g, m, v)


def _small_adamw(tot, ws, ms, vs):
    n = len(ws)

    def body(*refs):
        tot_ref = refs[0]
        w_refs, m_refs, v_refs = refs[1:1 + n], refs[1 + n:1 + 2 * n], refs[1 + 2 * n:1 + 3 * n]
        outs = refs[1 + 3 * n:]
        for i in range(n):
            rows, width = ws[i].shape
            g = tot_ref[pl.ds(SMALL_ROW[i], rows), pl.ds(0, width)]
            outs[4 * i][...] = g
            outs[4 * i + 1][...], outs[4 * i + 2][...], outs[4 * i + 3][...] = _adam_math(
                w_refs[i][...], g, m_refs[i][...], v_refs[i][...])
        outs[4 * n][...] = tot_ref[pl.ds(LOSS_ROW, 1), pl.ds(0, LANE)]

    vmem = pl.BlockSpec(memory_space=pltpu.VMEM)
    res = pl.pallas_call(
        body, name="small_adamw", in_specs=[vmem] * (1 + 3 * n), out_specs=[vmem] * (4 * n + 1),
        out_shape=[jax.ShapeDtypeStruct(w.shape, F32) for w in ws for _ in range(4)]
        + [jax.ShapeDtypeStruct((1, LANE), F32)],
    )(tot, *ws, *ms, *vs)
    return [res[4 * i:4 * i + 4] for i in range(n)], res[4 * n]


HBM_SPEC = pl.BlockSpec(memory_space=pltpu.HBM)


def _place():
    return lax.axis_index("x"), lax.axis_index("y"), lax.axis_index("c")


def _all_gather_split(shard):
    def body(in_ref, out_ref, send_sems, recv_sems):
        direct, relayed, passed = _gather_tree_copies([in_ref], [out_ref], send_sems, recv_sems)
        for cp in direct:
            cp.start()
        for i in range(2):
            direct[i].wait_recv()
            relayed[i].start()
            passed[i].start()
        for cp in relayed:
            cp.wait_recv()
        passed[2].start()
        for cp in passed:
            cp.wait_recv()
        for cp in direct + relayed + passed:
            cp.wait_send()

    return pl.pallas_call(
        body, name="all_gather_split", in_specs=[HBM_SPEC], out_specs=HBM_SPEC,
        out_shape=jax.ShapeDtypeStruct((NCHIP,) + shard.shape, shard.dtype),
        scratch_shapes=_hosted_sems(1, GATHER_SEMS),
    )(shard)


SEM_SPEC = pl.BlockSpec(memory_space=pltpu.SEMAPHORE)
ORDERED_EFFECT = pltpu.CompilerParams(has_side_effects=pltpu.SideEffectType.DATAFLOW_SIDE_EFFECTING)


def _scatter_start(partial):
    land = pltpu.with_memory_space_constraint(
        lax.empty((NCHIP - 1,) + partial.shape[1:], partial.dtype), pltpu.HBM)

    def body(p_ref, land_ref, send_sems, recv_sems, p_thru, land_thru, token):
        for cp in _scatter_copies([p_ref], [land_ref], send_sems, recv_sems):
            cp.start()
        token[...] = jnp.zeros_like(token)

    return pl.pallas_call(
        body, name="scatter_start",
        out_shape=(pltpu.SemaphoreType.DMA((NCHIP - 1,)), pltpu.SemaphoreType.DMA((NCHIP - 1,)),
                   pltpu.HBM(partial.shape, partial.dtype), pltpu.HBM(land.shape, land.dtype),
                   jax.ShapeDtypeStruct((8, LANE), F32)),
        in_specs=(HBM_SPEC, HBM_SPEC),
        out_specs=(SEM_SPEC, SEM_SPEC, HBM_SPEC, HBM_SPEC, pl.BlockSpec(memory_space=pltpu.VMEM)),
        input_output_aliases={0: 2, 1: 3}, compiler_params=ORDERED_EFFECT,
    )(pltpu.with_memory_space_constraint(partial, pltpu.HBM), land)


def _scatter_wait(send_sems, recv_sems, p_thru, land_thru, after):
    def body(p_ref, land_ref, send_sems, recv_sems, after_ref, p_dead, got_ref):
        for cp in _scatter_copies([p_ref], [land_ref], send_sems, recv_sems):
            cp.wait_send()
            cp.wait_recv()

    return pl.pallas_call(
        body, name="scatter_wait",
        out_shape=(pltpu.HBM(p_thru.shape, p_thru.dtype), pltpu.HBM(land_thru.shape, land_thru.dtype)),
        in_specs=(HBM_SPEC, HBM_SPEC, SEM_SPEC, SEM_SPEC, pl.BlockSpec(memory_space=pl.ANY)),
        out_specs=(HBM_SPEC, HBM_SPEC), input_output_aliases={0: 0, 1: 1}, compiler_params=ORDERED_EFFECT,
    )(p_thru, land_thru, send_sems, recv_sems, after)


def _pair_exchange(grads):
    n = len(grads)

    def body(*refs):
        copies = _pair_copies(refs[:n], refs[n:2 * n], refs[2 * n], refs[2 * n + 1])
        for cp in copies:
            cp.start()
        for cp in copies:
            cp.wait()

    return pl.pallas_call(
        body, name="pair_exchange", in_specs=[HBM_SPEC] * n, out_specs=[HBM_SPEC] * n,
        out_shape=[jax.ShapeDtypeStruct((NCHIP, g.shape[1] // 2, g.shape[2]), F32) for g in grads],
        scratch_shapes=[pltpu.SemaphoreType.DMA((n,)), pltpu.SemaphoreType.DMA((n,))],
    )(*grads)


def _small_allreduce(rows, loss_blk):
    n = len(rows)

    def body(*refs):
        loss_ref, o_ref, pk, slots, send_sems, recv_sems = refs[n:]
        pk[...] = jnp.zeros_like(pk)
        for i in range(n):
            pk[pl.ds(i, 1), pl.ds(0, rows[i].shape[1])] = refs[i][...]
        pk[pl.ds(LOSS_ROW, 1), pl.ds(0, LANE)] = loss_ref[pl.ds(0, 1), :]
        x, y, c = _place()
        me = 4 * x + 2 * y + c
        slots[me] = pk[...]
        copies = []
        for r in range(1, 8):
            rx, ry, rc = (r >> 2) & 1, (r >> 1) & 1, r & 1
            peer = (x + rx - 2 * x * rx, y + ry - 2 * y * ry, c + rc - 2 * c * rc)
            cp = pltpu.make_async_remote_copy(
                src_ref=pk, dst_ref=slots.at[me], send_sem=send_sems.at[r - 1], recv_sem=recv_sems.at[r - 1],
                device_id=peer, device_id_type=MESH)
            cp.start()
            copies.append(cp)
        for cp in copies:
            cp.wait()
        acc = slots[0]
        for j in range(1, 8):
            acc = acc + slots[j]
        o_ref[...] = acc

    vmem = pl.BlockSpec(memory_space=pltpu.VMEM)
    return pl.pallas_call(
        body, name="small_allreduce", in_specs=[vmem] * (n + 1), out_specs=vmem,
        out_shape=jax.ShapeDtypeStruct((SMALL_PK, D), F32),
        scratch_shapes=[pltpu.VMEM((SMALL_PK, D), F32), pltpu.VMEM((8, SMALL_PK, D), F32),
                        pltpu.SemaphoreType.DMA((7,)), pltpu.SemaphoreType.DMA((7,))],
    )(*rows, loss_blk)


def _seg_matrix():
    i = lax.broadcasted_iota(jnp.int32, (AD, AD), 0) // DH
    j = lax.broadcasted_iota(jnp.int32, (AD, AD), 1) // DH
    return (i == j).astype(BF16)


TAIL = ("w_out", "w_ple_gate", "w_ple", "w_pw", "dw_w")


def _local_step(x, p, tgt, sm, shards, chip, ci):
    seg = _seg_matrix()
    core = jnp.reshape(ci, (1,)).astype(jnp.int32)
    chip_idx = jnp.reshape(chip, (1,)).astype(jnp.int32)
    own = lambda g, s: lax.dynamic_update_index_in_dim(g, s, chip, 0)
    w_in_next = own(_all_gather_split(shards[0]["w_in"]), shards[0]["w_in"])
    h = x
    saved = []
    for l in range(DEPTH):
        w_in = w_in_next
        row = lambda name: sm[name][l:l + 1]
        u, hn = _rms_inproj(h, row("norm_g"), w_in)
        todo = [shards[l][k] for k in TAIL] + ([shards[l + 1]["w_in"]] if l + 1 < DEPTH else [])
        o, ya, tot, got = _attn_fwd(u, jnp.tile(row("attn_out_g"), (1, AD // DH)), todo)
        got = [own(g, s) for g, s in zip(got, todo)]
        w_out = got[0].reshape(D, D)
        w_gate = got[1].reshape(D, D)
        w_ple = got[2]
        w_pw = got[3].reshape(CD, CD)
        dw = got[4]
        if l + 1 < DEPTH:
            w_in_next = got[5]
        c1 = _glu_conv(u, dw, row("dw_b"))
        c3, yc, h1, gate, pe, h2, *at_end = _layer_tail(
            c1, u, ya, h, p[l], row("conv_ln_g"), row("conv_ln_b"), w_pw, row("conv_out_g"), w_out,
            row("ple_norm_g"), w_gate, w_ple, head=(tgt, sm["final_g"]) if l == DEPTH - 1 else None)
        saved.append(dict(h=h, u=u, hn=hn, o=o, ya=ya, tot=tot, c1=c1, c3=c3, yc=yc, h1=h1, gate=gate, pe=pe,
                          w_in=w_in, w_out=w_out, w_gate=w_gate, w_pw=w_pw, dw=dw))
        h = h2
    dh, (loss_blk, dfg) = h, at_end
    small = [None] * DEPTH
    pending, partials, arrived = [], {}, {}
    pair_sum = lambda grads: _sum_pair(core, grads, _pair_exchange(grads))
    for l in reversed(range(DEPTH)):
        s = saved[l]
        row = lambda name: sm[name][l:l + 1]
        dh1, dy, dwg, dwp, dwo, dpg = _ple_out_bwd(
            dh, s["h1"], s["gate"], s["pe"], p[l], s["ya"], s["yc"], row("ple_norm_g"), s["w_gate"], s["w_out"])
        ag_t = jnp.tile(row("attn_out_g"), (1, AD // DH))
        do, dga, dgc, dc1, dwpw, dag, dcg, dlg, dlb = _branch_bwd(
            dy, s["o"], s["u"], s["c1"], s["c3"], ag_t, row("conv_ln_g"), row("conv_ln_b"), s["w_pw"],
            row("conv_out_g"), seg)
        tail = [dwo.reshape(NCHIP, 256, D), dwg.reshape(NCHIP, 256, D), dwp, dwpw.reshape(NCHIP, 128, CD)]
        dcv, dcgate, ddw, ddb, halves = _conv_bwd(dc1, s["u"], s["dw"], tail if l == 0 else ())
        tail.append(ddw)
        if l == 0:
            partials[(l, "tail")] = _sum_pair(core, tail, halves + list(_pair_exchange([ddw])))
            pending.append((l, "tail"))
        send = [t for key in pending for t in partials[key]]
        dq, dk, dv, got = _attn_bwd(s["u"], do, s["tot"], send)
        for key in pending:
            arrived[key], got = got[:len(partials[key])], got[len(partials[key]):]
        pending = []
        du = jnp.concatenate([dq, dk, dv, dga, dcv, dcgate, dgc], axis=1)
        dwin = _inproj_dw(s["hn"], du)
        if l == 0:
            in_flight = _scatter_start(pair_sum([dwin])[0])
            dh, dng, _ = _inproj_dx(du, s["w_in"], s["h"], row("norm_g") + in_flight[4][0:1, 0:1], dh1)
            landed = _scatter_wait(*in_flight[:4], dh)
            partials[(l, "w_in")], arrived[(l, "w_in")] = [landed[0]], [landed[1]]
        else:
            dh, dng, halves = _inproj_dx(du, s["w_in"], s["h"], row("norm_g"), dh1, grads=tail + [dwin])
            tail_p = _sum_pair(core, tail + [dwin], halves)
            partials[(l, "tail")], partials[(l, "w_in")] = tail_p[:-1], tail_p[-1:]
            pending = [(l, "tail"), (l, "w_in")]
        small[l] = dict(norm_g=dng, attn_out_g=dag.reshape(AD // DH, DH).sum(axis=0, keepdims=True), dw_b=ddb,
                        conv_ln_g=dlg, conv_ln_b=dlb, conv_out_g=dcg, ple_norm_g=dpg)
    both = lambda d: [list(d[(l, "w_in")]) + list(d[(l, "tail")]) for l in range(DEPTH)]
    big = dict(zip(BIG, _sum_chips_share(chip_idx, both(partials), both(arrived))))
    return loss_blk, dh, big, small, dfg


BIG = ("w_in", "w_out", "w_ple_gate", "w_ple", "w_pw", "dw_w")
SMALL2 = ("norm_g", "ple_norm_g", "dw_b", "conv_ln_g", "conv_ln_b", "conv_out_g", "attn_out_g")
SMALL_ROW = (0, 2, 4, 6, 8, 10, 12, 14)


def kernel(x, p, norm_g, w_in, attn_out_g, dw_w, dw_b, conv_ln_g, conv_ln_b, w_pw, conv_out_g, w_out, ple_norm_g, w_ple_gate, w_ple, final_g, loss_target, m_norm_g, m_w_in, m_attn_out_g, m_dw_w, m_dw_b, m_conv_ln_g, m_conv_ln_b, m_w_pw, m_conv_out_g, m_w_out, m_ple_norm_g, m_w_ple_gate, m_w_ple, m_final_g, v_norm_g, v_w_in, v_attn_out_g, v_dw_w, v_dw_b, v_conv_ln_g, v_conv_ln_b, v_w_pw, v_conv_out_g, v_w_out, v_ple_norm_g, v_w_ple_gate, v_w_ple, v_final_g):
    W = dict(norm_g=norm_g, w_in=w_in, attn_out_g=attn_out_g, dw_w=dw_w, dw_b=dw_b, conv_ln_g=conv_ln_g,
             conv_ln_b=conv_ln_b, w_pw=w_pw, conv_out_g=conv_out_g, w_out=w_out, ple_norm_g=ple_norm_g,
             w_ple_gate=w_ple_gate, w_ple=w_ple, final_g=final_g)
    M = dict(norm_g=m_norm_g, w_in=m_w_in, attn_out_g=m_attn_out_g, dw_w=m_dw_w, dw_b=m_dw_b,
             conv_ln_g=m_conv_ln_g, conv_ln_b=m_conv_ln_b, w_pw=m_w_pw, conv_out_g=m_conv_out_g, w_out=m_w_out,
             ple_norm_g=m_ple_norm_g, w_ple_gate=m_w_ple_gate, w_ple=m_w_ple, final_g=m_final_g)
    V = dict(norm_g=v_norm_g, w_in=v_w_in, attn_out_g=v_attn_out_g, dw_w=v_dw_w, dw_b=v_dw_b,
             conv_ln_g=v_conv_ln_g, conv_ln_b=v_conv_ln_b, w_pw=v_w_pw, conv_out_g=v_conv_out_g, w_out=v_w_out,
             ple_norm_g=v_ple_norm_g, w_ple_gate=v_w_ple_gate, w_ple=v_w_ple, final_g=v_final_g)
    order = ("norm_g", "w_in", "attn_out_g", "dw_w", "dw_b", "conv_ln_g", "conv_ln_b", "w_pw", "conv_out_g",
             "w_out", "ple_norm_g", "w_ple_gate", "w_ple", "final_g")

    pad_taps = lambda a: jnp.pad(a, ((0, 0), (0, CWP - CW), (0, 0)))
    cast = dict(w_in=w_in.astype(BF16), w_out=w_out.astype(BF16), w_ple_gate=w_ple_gate.astype(BF16),
                w_ple=w_ple.astype(BF16), w_pw=w_pw.astype(BF16), dw_w=pad_taps(dw_w))
    shards = [{k: v[l] for k, v in cast.items()} for l in range(DEPTH)]
    xi, yi, ci = lax.axis_index("x"), lax.axis_index("y"), lax.axis_index("c")
    chip = 2 * xi + yi

    sm = {k: W[k] for k in SMALL2}
    sm["final_g"] = final_g.reshape(1, D)
    loss_part, grad_x, big, small, dfg = _local_step(x[0], p[:, 0], loss_target[0], sm, shards, chip, ci)
    g_big = {name: big[name].reshape(cast[name].shape) for name in BIG}

    rows = [small[l][k] for k in SMALL2 for l in range(DEPTH)] + [dfg]
    small_names = SMALL2 + ("final_g",)
    as_rows = lambda t: t.reshape(1, D) if t.ndim == 1 else t
    results, loss_row = _small_adamw(
        _small_allreduce(rows, loss_part), [as_rows(W[k]) for k in small_names],
        [as_rows(M[k]) for k in small_names], [as_rows(V[k]) for k in small_names])
    loss = loss_row[0, 0]

    grads, deltas, new_m, new_v = {}, {}, {}, {}
    for name in BIG:
        wv = pad_taps(W[name]) if name == "dw_w" else W[name]
        mv = pad_taps(M[name]) if name == "dw_w" else M[name]
        vv = pad_taps(V[name]) if name == "dw_w" else V[name]
        gg = g_big[name]
        cols = wv.shape[-1]
        rows_total = wv.size // cols
        tile_rows = min(rows_total, 256)
        d2, m2, v2 = _adamw(wv.reshape(rows_total, cols), gg.reshape(rows_total, cols),
                            mv.reshape(rows_total, cols), vv.reshape(rows_total, cols), tile_rows)
        if name == "dw_w":
            cut = lambda a: a.reshape(DEPTH, CWP, LANE)[:, :CW]
            grads[name], deltas[name], new_m[name], new_v[name] = cut(gg), cut(d2), cut(m2), cut(v2)
        else:
            grads[name] = gg
            deltas[name], new_m[name], new_v[name] = (t.reshape(wv.shape) for t in (d2, m2, v2))
    for k, four in zip(small_names, results):
        grads[k], deltas[k], new_m[k], new_v[k] = (t.reshape(W[k].shape) for t in four)

    return (loss, grad_x[None], *[grads[n] for n in order], *[deltas[n] for n in order],
            *[new_m[n] for n in order], *[new_v[n] for n in order])
```

```python
import functools

import jax
import jax.numpy as jnp
from jax import lax
from jax.experimental import pallas as pl
from jax.experimental.pallas import tpu as pltpu

F32 = jnp.float32
BF16 = jnp.bfloat16

T = 2048
D = 1024
DIN = 3584
NCHIP = 4
SHW = DIN // NCHIP
AD = 512
CD = 512
DH = 64
CW = 31
CWP = 32
PLE = 256
DEPTH = 2
EPS = 1e-6
AQ = 256
HG = 4
GW = HG * DH
SR = HG * AQ
NG = AD // GW
LANE = 128
TM = 1024
TR = 256

ADAM_LR = 0.001
ADAM_B1 = 0.9
ADAM_B2 = 0.999
ADAM_EPS = 1e-08
ADAM_WD = 0.01
ADAM_STEP = 10

SMALL_PK = 16
LOSS_ROW = 15

VMEM_BIG = 56 * 1024 * 1024
MESH = pl.DeviceIdType.MESH


def _cp(sem=None, vmem=None):
    kw = {}
    if sem is not None:
        kw["dimension_semantics"] = sem
    if vmem is not None:
        kw["vmem_limit_bytes"] = vmem
    return pltpu.CompilerParams(**kw)


def _dot(a, b):
    return jnp.dot(a, b, preferred_element_type=F32)


def _dot_nt(a, b):
    return lax.dot_general(a, b, (((1,), (1,)), ((), ())), preferred_element_type=F32)


def _dot_tn(a, b):
    return lax.dot_general(a, b, (((0,), (0,)), ((), ())), preferred_element_type=F32)


def _dot2(x, m):
    hi = x.astype(BF16)
    lo = (x - hi.astype(F32)).astype(BF16)
    return _dot(hi, m) + _dot(lo, m)


def _sig(x):
    return 1.0 / (1.0 + jnp.exp(-x))


def _softplus(z):
    return jnp.maximum(z, 0.0) + jnp.log(1.0 + jnp.exp(-jnp.abs(z)))


def _rstd(x):
    return lax.rsqrt(jnp.mean(x * x, axis=-1, keepdims=True) + EPS)


def _rms_bwd(dy, x, r, g):
    dn = dy * g
    return r * dn - x * (r * r * r) * jnp.mean(dn * x, axis=-1, keepdims=True)


def _rms_inproj(h, g, w):
    tm = min(TM, T)

    def body(h_ref, g_ref, w_ref, u_ref, hn_ref, hn_s):
        @pl.when(pl.program_id(1) == 0)
        def _():
            hh = h_ref[...]
            hn = (hh * _rstd(hh) * g_ref[...]).astype(BF16)
            hn_s[...] = hn
            hn_ref[...] = hn
        u_ref[...] = _dot(hn_s[...], w_ref[0])

    return pl.pallas_call(
        body, name="rms_inproj", grid=(T // tm, NCHIP),
        in_specs=[pl.BlockSpec((tm, D), lambda i, k: (i, 0)),
                  pl.BlockSpec((1, D), lambda i, k: (0, 0)),
                  pl.BlockSpec((1, D, SHW), lambda i, k: (k, 0, 0))],
        out_specs=[pl.BlockSpec((tm, SHW), lambda i, k: (i, k)),
                   pl.BlockSpec((tm, D), lambda i, k: (i, 0))],
        out_shape=[jax.ShapeDtypeStruct((T, DIN), F32), jax.ShapeDtypeStruct((T, D), BF16)],
        scratch_shapes=[pltpu.VMEM((tm, D), BF16)],
        compiler_params=_cp(("arbitrary", "arbitrary"), VMEM_BIG),
    )(h, g, w)


def _attn_tiles():
    row = lax.broadcasted_iota(jnp.int32, (SR, AQ), 0) & (AQ - 1)
    col = lax.broadcasted_iota(jnp.int32, (SR, AQ), 1)
    tr = lax.broadcasted_iota(jnp.int32, (AQ, AQ), 0)
    tc = lax.broadcasted_iota(jnp.int32, (AQ, AQ), 1)
    lane_head = lax.broadcasted_iota(jnp.int32, (1, GW), 1) // DH
    return col < row, tr, tc, [lane_head == h for h in range(HG)]


def _stack_heads(t, heads):
    return jnp.concatenate([jnp.where(m, t, 0.0) for m in heads], axis=0)


def _unstack_heads(t, heads):
    out = t[:AQ]
    for h in range(1, HG):
        out = jnp.where(heads[h], t[h * AQ:(h + 1) * AQ], out)
    return out


def _tri_sum(x, tri):
    hi = x.astype(BF16)
    lo = (x - hi.astype(F32)).astype(BF16)
    both = _dot(jnp.concatenate([hi, lo], axis=0), tri)
    return both[:SR] + both[SR:]


def _scatter_copies(ps, gots, send_sems, recv_sems):
    x, y, c = _place()
    peers = [(1 - x, y), (x, 1 - y), (1 - x, 1 - y)]
    return [pltpu.make_async_remote_copy(
        src_ref=ps[a].at[2 * px + py], dst_ref=gots[a].at[r], send_sem=send_sems.at[3 * a + r],
        recv_sem=recv_sems.at[3 * a + r], device_id=(px, py, c), device_id_type=MESH)
        for a in range(len(ps)) for r, (px, py) in enumerate(peers)]


GATHER_SEMS = 7


def _gather_tree_copies(ins, outs, send_sems, recv_sems):
    x, y, c = _place()
    me, xn, yn, dg = 2 * x + y, 2 * (1 - x) + y, 2 * x + (1 - y), 2 * (1 - x) + (1 - y)
    to_x, to_y, sibling = (1 - x, y, c), (x, 1 - y, c), (x, y, 1 - c)
    direct, relayed, passed = [], [], []
    for a in range(len(ins)):
        half = ins[a].shape[0] // 2
        mine = pl.ds(c * half, half)
        first, second = pl.ds(c * half, half // 2), pl.ds(c * half + half // 2, half // 2)

        def copy(i, src, dst, to, k=GATHER_SEMS * a):
            return pltpu.make_async_remote_copy(src_ref=src, dst_ref=dst, send_sem=send_sems.at[k + i],
                                                recv_sem=recv_sems.at[k + i], device_id=to, device_id_type=MESH)

        own, slot = ins[a].at[mine], outs[a].at[me, mine]
        direct += [copy(0, own, slot, to_x), copy(1, own, slot, to_y)]
        relayed += [copy(2, outs[a].at[xn, first], outs[a].at[xn, first], to_y),
                    copy(3, outs[a].at[yn, second], outs[a].at[yn, second], to_x)]
        passed += [copy(4 + i, outs[a].at[j, mine], outs[a].at[j, mine], sibling) for i, j in enumerate((xn, yn, dg))]
    return direct, relayed, passed


def _pair_copies(ins, outs, send_sems, recv_sems):
    x, y, c = _place()
    copies = []
    for a in range(len(ins)):
        half = ins[a].shape[1] // 2
        copies.append(pltpu.make_async_remote_copy(
            src_ref=ins[a].at[:, pl.ds((1 - c) * half, half), :], dst_ref=outs[a], send_sem=send_sems.at[a],
            recv_sem=recv_sems.at[a], device_id=(x, y, 1 - c), device_id_type=MESH))
    return copies


def _host(body, grid, n_in, n_out, n_x, make_copies, mids=()):
    if not n_x:
        return body

    def hosting(*refs):
        a, b = n_in + n_x, n_in + 2 * n_x + n_out
        copies = make_copies(refs[n_in:a], refs[a + n_out:b], refs[-2], refs[-1])
        stages = copies if isinstance(copies, tuple) else (copies,)
        ids = [pl.program_id(d) for d in range(len(grid))]
        at = lambda step: functools.reduce(jnp.logical_and, [i == s for i, s in zip(ids, step)])

        @pl.when(at([0] * len(grid)))
        def _():
            for cp in stages[0]:
                cp.start()

        for before, after, step in zip(stages, stages[1:], mids):
            @pl.when(at(step))
            def _(before=before, after=after):
                for cp in before:
                    cp.wait_recv()
                for cp in after:
                    cp.start()

        body(*refs[:n_in], *refs[a:a + n_out], *refs[b:-2])

        @pl.when(at([g - 1 for g in grid]))
        def _():
            for cp in stages[-1]:
                cp.wait_recv()
            for stage in stages:
                for cp in stage:
                    cp.wait_send()

    return hosting


def _hosted_sems(n_x, per_array=3):
    n = per_array * n_x
    return [pltpu.SemaphoreType.DMA((n,)), pltpu.SemaphoreType.DMA((n,))] if n_x else []


RC = 256


def _chunk_causal(r):
    row = lax.broadcasted_iota(jnp.int32, (RC, AQ), 0) + (r * RC) % AQ
    return lax.broadcasted_iota(jnp.int32, (RC, AQ), 1) < row


def _attn_fwd(u, agw, shards=()):
    n = len(shards)
    grid = (T // AQ,)

    def body(q_ref, k_ref, v_ref, g_ref, ag_ref, o_ref, y_ref, tot_ref,
             kb_s, vb_s, qs_s, z_s, zs_s, lmb_s, suf_s, att_s, acc_s, run_s):
        qi = pl.program_id(0)

        @pl.when(qi == 0)
        def _():
            kb_s[...] = k_ref[...].astype(BF16)
            vb_s[...] = v_ref[...].astype(BF16)

        _, tr, tc, heads = _attn_tiles()
        upper = (tr > tc).astype(BF16)
        same_head = ((tr // DH) == (tc // DH)).astype(BF16)
        for g in range(NG):
            qs_s[g] = _stack_heads(q_ref[:, g * GW:(g + 1) * GW] * 0.125, heads).astype(BF16)
        acc_s[...] = jnp.zeros_like(acc_s)
        run_s[...] = jnp.zeros_like(run_s)

        def block(kb, masked):
            k0 = pl.multiple_of(kb * AQ, AQ)
            for g in range(NG):
                lanes = pl.ds(g * GW, GW)
                z_s[g] = _dot_nt(qs_s[g], kb_s[pl.ds(k0, AQ), lanes])
                for r in range(SR // RC):
                    rows = pl.ds(r * RC, RC)
                    z = z_s[g, rows, :]
                    zs = jnp.minimum(z, 0.0) - jnp.log(1.0 + jnp.exp(-jnp.abs(z)))
                    lm = zs - z
                    if masked:
                        lm = jnp.where(_chunk_causal(r), lm, 0.0)
                    run = run_s[g, rows, :]
                    zs_s[g, rows, :] = zs + run[:, 0:1]
                    hi = lm.astype(BF16)
                    lmb_s[g, rows, :] = hi
                    lmb_s[g, pl.ds(SR + r * RC, RC), :] = (lm - hi.astype(F32)).astype(BF16)
                    run_s[g, rows, :] = run + jnp.sum(lm, axis=1, keepdims=True)
                suf_s[g] = _dot(lmb_s[g], upper)
                for r in range(SR // RC):
                    rows = pl.ds(r * RC, RC)
                    att = jnp.exp(zs_s[g, rows, :] + suf_s[g, rows, :] + suf_s[g, pl.ds(SR + r * RC, RC), :])
                    if masked:
                        att = jnp.where(_chunk_causal(r), att, 0.0)
                    att_s[g, rows, :] = att.astype(BF16)
                acc_s[g] += _dot(att_s[g], vb_s[pl.ds(k0, AQ), lanes])

        block(qi, True)

        def step(i, c):
            block(qi - 1 - i, False)
            return c

        lax.fori_loop(0, qi, step, 0)
        gate = g_ref[...]
        agv = ag_ref[...]
        for g in range(NG):
            lanes = slice(g * GW, (g + 1) * GW)
            o = _unstack_heads(acc_s[g], heads)
            osq = o * o
            ms = _dot2(osq, same_head)
            gg = gate[:, lanes]
            o_ref[:, lanes] = o
            y_ref[:, lanes] = (o * lax.rsqrt(ms * (1.0 / DH) + EPS) * agv[:, lanes] * (gg * _sig(gg))).astype(BF16)
            tot_ref[g] = _unstack_heads(jnp.broadcast_to(run_s[g][:, 0:1], (SR, GW)), heads)

    tile = lambda dt, rows=SR: pltpu.VMEM((NG, rows, AQ), dt)
    scratch = [pltpu.VMEM((T, AD), BF16), pltpu.VMEM((T, AD), BF16), pltpu.VMEM((NG, SR, GW), BF16),
               tile(F32), tile(F32), tile(BF16, 2 * SR), tile(F32, 2 * SR), tile(BF16),
               pltpu.VMEM((NG, SR, GW), F32), pltpu.VMEM((NG, SR, LANE), F32)]
    col = lambda j: pl.BlockSpec((AQ, AD), lambda qi: (qi, j))
    res = pl.pallas_call(
        _host(body, grid, 5, 3, n, _gather_tree_copies, mids=((grid[0] * 5 // 8,), (grid[0] * 7 // 8,))), name="attn_fwd", grid=grid,
        in_specs=[col(0), pl.BlockSpec((T, AD), lambda qi: (0, 1)), pl.BlockSpec((T, AD), lambda qi: (0, 2)),
                  col(3), pl.BlockSpec((1, AD), lambda qi: (0, 0))] + [HBM_SPEC] * n,
        out_specs=[col(0), col(0), pl.BlockSpec((NG, AQ, GW), lambda qi: (0, qi, 0))] + [HBM_SPEC] * n,
        out_shape=[jax.ShapeDtypeStruct((T, AD), F32), jax.ShapeDtypeStruct((T, AD), BF16),
                   jax.ShapeDtypeStruct((NG, T, GW), F32)]
        + [jax.ShapeDtypeStruct((NCHIP,) + s.shape, s.dtype) for s in shards],
        scratch_shapes=scratch + _hosted_sems(n, GATHER_SEMS),
        compiler_params=_cp(("arbitrary",), VMEM_BIG),
    )(u, u, u, u, agw, *shards)
    return res[0], res[1], res[2], list(res[3:])


def _glu_conv(u, dw, db):
    tr = 256

    def body(cv_ref, cg_ref, w_ref, b_ref, c1_ref, pad_s):
        pad_s[pl.ds(0, CWP), :] = jnp.zeros((CWP, LANE), F32)
        pad_s[pl.ds(CWP, T), :] = cv_ref[...] * _sig(cg_ref[...])
        wv = w_ref[0]
        bias = b_ref[...]

        def tile(i, carry):
            r0 = pl.multiple_of(i * tr, tr)
            acc = jnp.zeros((tr, LANE), F32) + bias
            for w in range(CW):
                acc = acc + pad_s[pl.ds(r0 + (CWP - CW + 1) + w, tr), :] * wv[w:w + 1, :]
            c1_ref[pl.ds(r0, tr), :] = acc
            return carry

        lax.fori_loop(0, T // tr, tile, 0)

    return pl.pallas_call(
        body, name="glu_conv", grid=(CD // LANE,),
        in_specs=[pl.BlockSpec((T, LANE), lambda cb: (0, 16 + cb)),
                  pl.BlockSpec((T, LANE), lambda cb: (0, 20 + cb)),
                  pl.BlockSpec((1, CWP, LANE), lambda cb: (cb, 0, 0)),
                  pl.BlockSpec((1, LANE), lambda cb: (0, cb))],
        out_specs=pl.BlockSpec((T, LANE), lambda cb: (0, cb)),
        out_shape=jax.ShapeDtypeStruct((T, CD), F32),
        scratch_shapes=[pltpu.VMEM((T + CWP, LANE), F32)],
        compiler_params=_cp(("arbitrary",)),
    )(u, u, dw, db)


def _ln_silu(c1, lg, lb):
    mu = jnp.mean(c1, axis=-1, keepdims=True)
    xc = c1 - mu
    rs = lax.rsqrt(jnp.mean(xc * xc, axis=-1, keepdims=True) + EPS)
    xh = xc * rs
    ln = xh * lg + lb
    s = _sig(ln)
    return xh, rs, ln, s


def _layer_tail(c1, u, ya, h, p, lg, lb, wpw, cg, wout, pg, wgate, wple, head=None):
    tm = min(TR, T)

    def body(c1_ref, gc_ref, ya_ref, h_ref, p_ref, lg_ref, lb_ref, wpw_ref, cg_ref, wout_ref,
             pg_ref, wgate_ref, wple_ref, *rest):
        c3_ref, yc_ref, h1_ref, gate_ref, pe_ref, h2_ref = rest[-6 - 2 * bool(head):][:6]
        _, _, ln, s = _ln_silu(c1_ref[...], lg_ref[...], lb_ref[...])
        c2 = (ln * s).astype(BF16)
        c3 = _dot(c2, wpw_ref[...])
        gc = gc_ref[...]
        yc = (c3 * _rstd(c3) * cg_ref[...] * (gc * _sig(gc))).astype(BF16)
        c3_ref[...] = c3
        yc_ref[...] = yc
        y = _dot(ya_ref[...], wout_ref[pl.ds(0, AD), :]) + _dot(yc, wout_ref[pl.ds(AD, CD), :])
        h1 = h_ref[...] + y
        hn2 = (h1 * _rstd(h1) * pg_ref[...]).astype(BF16)
        gate = _sig(_dot(hn2, wgate_ref[...]))
        pb = p_ref[...].astype(BF16)
        pe = jnp.concatenate([_dot(pb, wple_ref[k]) for k in range(NCHIP)], axis=1)
        h1_ref[...] = h1
        gate_ref[...] = gate.astype(BF16)
        pe_ref[...] = pe.astype(BF16)
        h2 = h1 + pe * gate
        if not head:
            h2_ref[...] = h2
            return
        t_ref, fg_ref, loss_ref, dfg_ref = rest[0], rest[1], rest[-2], rest[-1]

        @pl.when(pl.program_id(0) == 0)
        def _():
            loss_ref[...] = jnp.zeros_like(loss_ref)
            dfg_ref[...] = jnp.zeros_like(dfg_ref)
        fg = fg_ref[...]
        r = _rstd(h2)
        e = h2 * r * fg - t_ref[...]
        loss_ref[...] += 0.5 * jnp.sum(jnp.mean(e * e, axis=-1, keepdims=True))
        dy = e * (1.0 / D)
        dfg_ref[...] += jnp.sum(dy * h2 * r, axis=0, keepdims=True)
        h2_ref[...] = _rms_bwd(dy, h2, r, fg)

    row = lambda w: pl.BlockSpec((tm, w), lambda i: (i, 0))
    full = lambda *s: pl.BlockSpec(s, lambda i: (0,) * len(s), pipeline_mode=pl.Buffered(1))
    extra = bool(head)
    return pl.pallas_call(
        body, name="layer_tail", grid=(T // tm,),
        in_specs=[row(CD), pl.BlockSpec((tm, CD), lambda i: (i, 6)), row(AD), row(D), row(PLE),
                  full(1, CD), full(1, CD), full(CD, CD), full(1, CD), full(D, D),
                  full(1, D), full(D, D), full(NCHIP, PLE, PLE)] + [row(D), full(1, D)] * extra,
        out_specs=[row(CD), row(CD), row(D), row(D), row(D), row(D)] + [full(8, LANE), full(1, D)] * extra,
        out_shape=[jax.ShapeDtypeStruct((T, CD), F32), jax.ShapeDtypeStruct((T, CD), BF16),
                   jax.ShapeDtypeStruct((T, D), F32), jax.ShapeDtypeStruct((T, D), BF16),
                   jax.ShapeDtypeStruct((T, D), BF16), jax.ShapeDtypeStruct((T, D), F32)]
        + [jax.ShapeDtypeStruct((8, LANE), F32), jax.ShapeDtypeStruct((1, D), F32)] * extra,
        compiler_params=_cp(("arbitrary",), VMEM_BIG),
    )(c1, u, ya, h, p, lg, lb, wpw, cg, wout, pg, wgate, wple, *(head or ()))


def _ple_out_bwd(dh2, h1, gate, pe, p, ya, yc, pg, wgate, wout):
    tm = min(TR, T)

    def body(dh2_ref, h1_ref, gate_ref, pe_ref, p_ref, ya_ref, yc_ref, pg_ref, wgate_ref, wout_ref,
             dh1_ref, dy_ref, dwg_ref, dwp_ref, dwo_ref, dpg_ref):
        @pl.when(pl.program_id(0) == 0)
        def _():
            dwg_ref[...] = jnp.zeros_like(dwg_ref)
            dwp_ref[...] = jnp.zeros_like(dwp_ref)
            dwo_ref[...] = jnp.zeros_like(dwo_ref)
            dpg_ref[...] = jnp.zeros_like(dpg_ref)
        dh2 = dh2_ref[...]
        h1 = h1_ref[...]
        gate = gate_ref[...].astype(F32)
        pg = pg_ref[...]
        dpe = (dh2 * gate).astype(BF16)
        dgp = (dh2 * pe_ref[...].astype(F32) * gate * (1.0 - gate)).astype(BF16)
        r = _rstd(h1)
        hn = h1 * r
        dwg_ref[...] += _dot_tn((hn * pg).astype(BF16), dgp)
        dhn2 = _dot_nt(dgp, wgate_ref[...])
        dpg_ref[...] += jnp.sum(dhn2 * hn, axis=0, keepdims=True)
        dh1 = dh2 + _rms_bwd(dhn2, h1, r, pg)
        pb = p_ref[...].astype(BF16)
        for k in range(NCHIP):
            dwp_ref[k] += _dot_tn(pb, dpe[:, k * PLE:(k + 1) * PLE])
        dh1b = dh1.astype(BF16)
        dy_ref[...] = _dot_nt(dh1b, wout_ref[...])
        dwo_ref[pl.ds(0, AD), :] += _dot_tn(ya_ref[...], dh1b)
        dwo_ref[pl.ds(AD, CD), :] += _dot_tn(yc_ref[...], dh1b)
        dh1_ref[...] = dh1

    row = lambda w: pl.BlockSpec((tm, w), lambda i: (i, 0))
    full = lambda *s: pl.BlockSpec(s, lambda i: (0,) * len(s), pipeline_mode=pl.Buffered(1))
    return pl.pallas_call(
        body, name="ple_out_bwd", grid=(T // tm,),
        in_specs=[row(D), row(D), row(D), row(D), row(PLE), row(AD), row(CD),
                  full(1, D), full(D, D), full(D, D)],
        out_specs=[row(D), row(D), full(D, D), full(NCHIP, PLE, PLE), full(D, D), full(1, D)],
        out_shape=[jax.ShapeDtypeStruct((T, D), F32), jax.ShapeDtypeStruct((T, D), F32),
                   jax.ShapeDtypeStruct((D, D), F32), jax.ShapeDtypeStruct((NCHIP, PLE, PLE), F32),
                   jax.ShapeDtypeStruct((D, D), F32), jax.ShapeDtypeStruct((1, D), F32)],
        compiler_params=_cp(("arbitrary",), VMEM_BIG),
    )(dh2, h1, gate, pe, p, ya, yc, pg, wgate, wout)


def _branch_bwd(dy, o, u, c1, c3, ag, lg, lb, wpw, cg, seg):
    tm = min(TR, T)

    def body(dya_ref, dyc_ref, o_ref, ga_ref, gc_ref, c1_ref, c3_ref, ag_ref, lg_ref, lb_ref, wpw_ref,
             cg_ref, seg_ref, do_ref, dga_ref, dgc_ref, dc1_ref, dwpw_ref, dag_ref, dcg_ref, dlg_ref, dlb_ref):
        @pl.when(pl.program_id(0) == 0)
        def _():
            for r_ in (dwpw_ref, dag_ref, dcg_ref, dlg_ref, dlb_ref):
                r_[...] = jnp.zeros_like(r_)
        dya = dya_ref[...]
        o = o_ref[...]
        ga = ga_ref[...]
        ag_v = ag_ref[...]
        seg_m = seg_ref[...]
        r = lax.rsqrt(_dot2(o * o, seg_m) * (1.0 / DH) + EPS)
        onr = o * r
        sg = _sig(ga)
        dga_ref[...] = (dya * (onr * ag_v) * (sg * (1.0 + ga * (1.0 - sg)))).astype(BF16)
        don = dya * (ga * sg)
        dag_ref[...] += jnp.sum(don * onr, axis=0, keepdims=True)
        dn = don * ag_v
        do_ref[...] = r * dn - o * (r * r * r) * (_dot2(dn * o, seg_m) * (1.0 / DH))
        dyc = dyc_ref[...]
        c3 = c3_ref[...]
        gc = gc_ref[...]
        cg_v = cg_ref[...]
        r3 = _rstd(c3)
        cn = c3 * r3
        sc = _sig(gc)
        dgc_ref[...] = (dyc * (cn * cg_v) * (sc * (1.0 + gc * (1.0 - sc)))).astype(BF16)
        dcn = dyc * (gc * sc)
        dcg_ref[...] += jnp.sum(dcn * cn, axis=0, keepdims=True)
        dc3 = _rms_bwd(dcn, c3, r3, cg_v).astype(BF16)
        lg_v = lg_ref[...]
        xh, rs, ln, s = _ln_silu(c1_ref[...], lg_v, lb_ref[...])
        c2 = (ln * s).astype(BF16)
        dwpw_ref[...] += _dot_tn(c2, dc3)
        dc2 = _dot_nt(dc3, wpw_ref[...])
        dln = dc2 * (s * (1.0 + ln * (1.0 - s)))
        dlb_ref[...] += jnp.sum(dln, axis=0, keepdims=True)
        dlg_ref[...] += jnp.sum(dln * xh, axis=0, keepdims=True)
        dxh = dln * lg_v
        dc1_ref[...] = rs * (dxh - jnp.mean(dxh, axis=-1, keepdims=True)
                             - xh * jnp.mean(dxh * xh, axis=-1, keepdims=True))

    half = lambda j: pl.BlockSpec((tm, 512), lambda i: (i, j))
    full = lambda *s: pl.BlockSpec(s, lambda i: (0,) * len(s), pipeline_mode=pl.Buffered(1))
    vec = jax.ShapeDtypeStruct((1, 512), F32)
    act = jax.ShapeDtypeStruct((T, 512), F32)
    return pl.pallas_call(
        body, name="branch_bwd", grid=(T // tm,),
        in_specs=[half(0), half(1), half(0), half(3), half(6), half(0), half(0),
                  full(1, AD), full(1, CD), full(1, CD), full(CD, CD), full(1, CD), full(AD, AD)],
        out_specs=[half(0), half(0), half(0), half(0), full(CD, CD), full(1, 512), full(1, 512),
                   full(1, 512), full(1, 512)],
        out_shape=[act, jax.ShapeDtypeStruct((T, 512), BF16), jax.ShapeDtypeStruct((T, 512), BF16), act,
                   jax.ShapeDtypeStruct((CD, CD), F32), vec, vec, vec, vec],
        compiler_params=_cp(("arbitrary",), VMEM_BIG),
    )(dy, dy, o, u, u, c1, c3, ag, lg, lb, wpw, cg, seg)


def _conv_bwd(dc1, u, dw, grads=()):
    tr = 64
    n_x = len(grads)
    grid = (CD // LANE,)
    off = CWP - CW + 1

    def body(d_ref, cv_ref, cg_ref, w_ref, dcv_ref, dcg_ref, ddw_ref, ddb_ref, padc_s, padd_s, acc_s):
        cv = cv_ref[...]
        sg = _sig(cg_ref[...])
        padc_s[pl.ds(0, CWP), :] = jnp.zeros((CWP, LANE), F32)
        padc_s[pl.ds(CWP, T), :] = cv * sg
        padd_s[pl.ds(0, T), :] = d_ref[...]
        padd_s[pl.ds(T, CWP), :] = jnp.zeros((CWP, LANE), F32)
        acc_s[...] = jnp.zeros_like(acc_s)
        wv = w_ref[0]

        def tile(i, carry):
            r0 = pl.multiple_of(i * tr, tr)
            dt = padd_s[pl.ds(r0, tr), :]
            dc0 = jnp.zeros((tr, LANE), F32)
            for w in range(CW):
                dc0 = dc0 + padd_s[pl.ds(r0 + (CW - 1) - w, tr), :] * wv[w:w + 1, :]
                prod = dt * padc_s[pl.ds(r0 + off + w, tr), :]
                acc_s[w] += jnp.sum(prod.reshape(tr // 8, 8, LANE), axis=0)
            cvt = cv_ref[pl.ds(r0, tr), :]
            sgt = _sig(cg_ref[pl.ds(r0, tr), :])
            dcv_ref[pl.ds(r0, tr), :] = (dc0 * sgt).astype(BF16)
            dcg_ref[pl.ds(r0, tr), :] = (dc0 * cvt * sgt * (1.0 - sgt)).astype(BF16)
            return carry

        lax.fori_loop(0, T // tr, tile, 0)
        ddw_ref[0] = jnp.sum(acc_s[...], axis=1)
        ddb_ref[...] = jnp.sum(d_ref[...], axis=0, keepdims=True)

    col = lambda j: pl.BlockSpec((T, LANE), lambda cb: (0, j + cb))
    res = pl.pallas_call(
        _host(body, grid, 4, 4, n_x, _pair_copies), name="conv_bwd", grid=grid,
        in_specs=[col(0), col(16), col(20), pl.BlockSpec((1, CWP, LANE), lambda cb: (cb, 0, 0))] + [HBM_SPEC] * n_x,
        out_specs=[col(0), col(0), pl.BlockSpec((1, CWP, LANE), lambda cb: (cb, 0, 0)),
                   pl.BlockSpec((1, LANE), lambda cb: (0, cb))] + [HBM_SPEC] * n_x,
        out_shape=[jax.ShapeDtypeStruct((T, CD), BF16), jax.ShapeDtypeStruct((T, CD), BF16),
                   jax.ShapeDtypeStruct((NCHIP, CWP, LANE), F32), jax.ShapeDtypeStruct((1, CD), F32)]
        + [jax.ShapeDtypeStruct((NCHIP, g.shape[1] // 2, g.shape[2]), F32) for g in grads],
        scratch_shapes=[pltpu.VMEM((T + CWP, LANE), F32), pltpu.VMEM((T + CWP, LANE), F32),
                        pltpu.VMEM((CWP, 8, LANE), F32)] + _hosted_sems(n_x),
        compiler_params=_cp(("arbitrary",)),
    )(dc1, u, u, dw, *grads)
    return res[0], res[1], res[2], res[3], list(res[4:])


def _attn_bwd(u, do, tot, partials=()):
    n_x = len(partials)
    grid = (T // AQ,)

    def body(q_ref, k_ref, v_ref, do_ref, tot_ref, dq_ref, dk_ref, dv_ref, kb_s, vb_s, dk_s, dv_s):
        qi = pl.program_id(0)

        @pl.when(qi == 0)
        def _():
            kb_s[...] = k_ref[...].astype(BF16)
            vb_s[...] = v_ref[...].astype(BF16)
            dk_s[...] = jnp.zeros_like(dk_s)
            dv_s[...] = jnp.zeros_like(dv_s)

        causal, tr, tc, heads = _attn_tiles()
        upper = (tr > tc).astype(BF16)
        lower = (tr < tc).astype(BF16)
        qs, qus, dos, tots = [], [], [], []
        for g in range(NG):
            lanes = slice(g * GW, (g + 1) * GW)
            q = q_ref[:, lanes]
            qs.append(_stack_heads(q * 0.125, heads).astype(BF16))
            qus.append(_stack_heads(q, heads).astype(BF16))
            dos.append(_stack_heads(do_ref[:, lanes], heads).astype(BF16))
            totv = tot_ref[g]
            tots.append(jnp.concatenate([totv[:, h * DH:h * DH + 1] for h in range(HG)], axis=0))

        def block(kb, carry, masked):
            k0 = pl.multiple_of(kb * AQ, AQ)
            out = []
            for g in range(NG):
                lanes = pl.ds(g * GW, GW)
                lm_left, dl_left, dq = carry[g]
                kk = kb_s[pl.ds(k0, AQ), lanes]
                vv = vb_s[pl.ds(k0, AQ), lanes]
                z = _dot_nt(qs[g], kk)
                sp = _softplus(z)
                lm = jnp.where(causal, -sp, 0.0) if masked else -sp
                lm_incl = lm_left + jnp.sum(lm, axis=1, keepdims=True)
                att = jnp.exp((z - sp) + _tri_sum(lm, upper) + (tots[g] - lm_incl))
                if masked:
                    att = jnp.where(causal, att, 0.0)
                dl = att * _dot_nt(dos[g], vv)
                dv_s[pl.ds(k0, AQ), lanes] += _dot_tn(att.astype(BF16), dos[g])
                prefix = dl_left + _tri_sum(dl, lower)
                beta = jnp.exp(z - sp)
                dz = (1.0 - beta) * dl - beta * prefix
                if masked:
                    dz = jnp.where(causal, dz, 0.0)
                dzs = (dz * 0.125).astype(BF16)
                dk_s[pl.ds(k0, AQ), lanes] += _dot_tn(dzs, qus[g])
                out.append((lm_incl, dl_left + jnp.sum(dl, axis=1, keepdims=True), dq + _dot(dzs, kk)))
            return tuple(out)

        zero = jnp.zeros((SR, 1), F32)
        init = tuple((zero, zero, jnp.zeros((SR, GW), F32)) for _ in range(NG))
        carry = lax.fori_loop(0, qi, lambda kb, c: block(kb, c, False), init)
        carry = block(qi, carry, True)
        for g in range(NG):
            dq_ref[:, g * GW:(g + 1) * GW] = _unstack_heads(carry[g][2], heads).astype(BF16)

        @pl.when(qi == grid[0] - 1)
        def _():
            dk_ref[...] = dk_s[...].astype(BF16)
            dv_ref[...] = dv_s[...].astype(BF16)

    col = lambda j: pl.BlockSpec((AQ, AD), lambda qi: (qi, j))
    whole = lambda j: pl.BlockSpec((T, AD), lambda qi: (0, j), pipeline_mode=pl.Buffered(1))
    res = pl.pallas_call(
        _host(body, grid, 5, 3, n_x, _scatter_copies), name="attn_bwd", grid=grid,
        in_specs=[col(0), whole(1), whole(2), col(0), pl.BlockSpec((NG, AQ, GW), lambda qi: (0, qi, 0))]
        + [HBM_SPEC] * n_x,
        out_specs=[col(0), whole(0), whole(0)] + [HBM_SPEC] * n_x,
        out_shape=[jax.ShapeDtypeStruct((T, AD), BF16)] * 3
        + [jax.ShapeDtypeStruct((NCHIP - 1,) + a.shape[1:], a.dtype) for a in partials],
        scratch_shapes=[pltpu.VMEM((T, AD), BF16), pltpu.VMEM((T, AD), BF16), pltpu.VMEM((T, AD), F32),
                        pltpu.VMEM((T, AD), F32)] + _hosted_sems(n_x),
        compiler_params=_cp(("arbitrary",), VMEM_BIG),
    )(u, u, u, do, tot, *partials)
    return res[0], res[1], res[2], list(res[3:])


def _inproj_dw(hn, du):
    tm = min(TM, T)

    def body(hn_ref, du_ref, dw_ref):
        @pl.when(pl.program_id(1) == 0)
        def _():
            dw_ref[...] = jnp.zeros_like(dw_ref)
        dw_ref[0] += _dot_tn(hn_ref[...], du_ref[...])

    return pl.pallas_call(
        body, name="inproj_dw", grid=(NCHIP, T // tm),
        in_specs=[pl.BlockSpec((tm, D), lambda k, i: (i, 0)), pl.BlockSpec((tm, SHW), lambda k, i: (i, k))],
        out_specs=pl.BlockSpec((1, D, SHW), lambda k, i: (k, 0, 0)),
        out_shape=jax.ShapeDtypeStruct((NCHIP, D, SHW), F32),
        compiler_params=_cp(("arbitrary", "arbitrary"), VMEM_BIG),
    )(hn, du)


def _inproj_dx(du, w, h, g, dres, partials=(), grads=()):
    tm = min(TM, T)
    sent = list(partials) + list(grads)
    n_x = len(sent)
    grid = (T // tm, NCHIP)
    if grads:
        landing = [jax.ShapeDtypeStruct((NCHIP, a.shape[1] // 2, a.shape[2]), F32) for a in grads]
    else:
        landing = [jax.ShapeDtypeStruct((NCHIP - 1,) + a.shape[1:], a.dtype) for a in partials]

    def body(du_ref, w_ref, h_ref, g_ref, dres_ref, dh_ref, dg_ref, acc_s):
        i, k = pl.program_id(0), pl.program_id(1)

        @pl.when(jnp.logical_and(i == 0, k == 0))
        def _():
            dg_ref[...] = jnp.zeros_like(dg_ref)

        @pl.when(k == 0)
        def _():
            acc_s[...] = _dot_nt(du_ref[...], w_ref[0])

        @pl.when(k > 0)
        def _():
            acc_s[...] += _dot_nt(du_ref[...], w_ref[0])

        @pl.when(k == NCHIP - 1)
        def _():
            hh = h_ref[...]
            r = _rstd(hh)
            dhn = acc_s[...]
            dg_ref[...] += jnp.sum(dhn * hh * r, axis=0, keepdims=True)
            dh_ref[...] = dres_ref[...] + _rms_bwd(dhn, hh, r, g_ref[...])

    res = pl.pallas_call(
        _host(body, grid, 5, 2, n_x, _pair_copies if grads else _scatter_copies), name="inproj_dx", grid=grid,
        in_specs=[pl.BlockSpec((tm, SHW), lambda i, k: (i, k)),
                  pl.BlockSpec((1, D, SHW), lambda i, k: (k, 0, 0)),
                  pl.BlockSpec((tm, D), lambda i, k: (i, 0)),
                  pl.BlockSpec((1, D), lambda i, k: (0, 0)),
                  pl.BlockSpec((tm, D), lambda i, k: (i, 0))] + [HBM_SPEC] * n_x,
        out_specs=[pl.BlockSpec((tm, D), lambda i, k: (i, 0)), pl.BlockSpec((1, D), lambda i, k: (0, 0))]
        + [HBM_SPEC] * n_x,
        out_shape=[jax.ShapeDtypeStruct((T, D), F32), jax.ShapeDtypeStruct((1, D), F32)] + landing,
        scratch_shapes=[pltpu.VMEM((tm, D), F32)] + _hosted_sems(n_x),
        compiler_params=_cp(("arbitrary", "arbitrary"), VMEM_BIG),
    )(du, w, h, g, dres, *sent)
    return res[0], res[1], list(res[2:])


def _sum_pair(core, grads, gots):
    n = len(grads)

    def body(c_ref, *refs):
        for a in range(n):
            refs[2 * n + a][...] = (refs[a][...] + refs[n + a][...]).astype(BF16)

    mine = [pl.BlockSpec((1,) + s.shape[1:], lambda k, c: (k, c[0], 0)) for s in gots]
    same = [pl.BlockSpec((1,) + s.shape[1:], lambda k, c: (k, 0, 0)) for s in gots]
    return pl.pallas_call(
        body, name="sum_pair",
        grid_spec=pltpu.PrefetchScalarGridSpec(
            num_scalar_prefetch=1, grid=(NCHIP,), in_specs=mine + same, out_specs=same),
        out_shape=[jax.ShapeDtypeStruct(s.shape, BF16) for s in gots],
        compiler_params=_cp(("arbitrary",), VMEM_BIG),
    )(core, *grads, *gots)


def _sum_chips_share(chip, partials, gots):
    flat_p = [p for layer in partials for p in layer]
    flat_g = [g for layer in gots for g in layer]
    n, per_layer = len(flat_p), len(partials[0])

    def body(c_ref, *refs):
        full, sums = refs[2 * n:2 * n + per_layer], refs[2 * n + per_layer:3 * n + per_layer]
        send_sems, recv_sems, local_sems = refs[3 * n + per_layer:]
        x, y, c = _place()
        copies = []
        for i in range(n):
            acc = refs[i][0].astype(F32)
            for j in range(NCHIP - 1):
                acc = acc + refs[n + i][j].astype(F32)
            sums[i][...] = acc
            half = flat_p[i].shape[1]
            rows = full[i % per_layer].at[i // per_layer, pl.ds(c * half, half)]
            copies.append(pltpu.make_async_copy(sums[i], rows, local_sems.at[i]))
            copies.append(pltpu.make_async_remote_copy(
                src_ref=sums[i], dst_ref=rows, send_sem=send_sems.at[i], recv_sem=recv_sems.at[i],
                device_id=(x, y, 1 - c), device_id_type=MESH))
        for cp in copies:
            cp.start()
        for cp in copies:
            cp.wait()

    return pl.pallas_call(
        body, name="sum_chips_share",
        grid_spec=pltpu.PrefetchScalarGridSpec(
            num_scalar_prefetch=1, grid=(1,),
            in_specs=[pl.BlockSpec((1,) + s.shape[1:], lambda i, c: (c[0], 0, 0)) for s in flat_p]
            + [pl.BlockSpec(s.shape, lambda i, c: (0, 0, 0)) for s in flat_g],
            out_specs=[HBM_SPEC] * per_layer,
            scratch_shapes=[pltpu.VMEM(s.shape[1:], F32) for s in flat_p]
            + [pltpu.SemaphoreType.DMA((n,)), pltpu.SemaphoreType.DMA((n,)), pltpu.SemaphoreType.DMA((n,))]),
        out_shape=[jax.ShapeDtypeStruct((len(partials), 2 * s.shape[1], s.shape[2]), F32) for s in partials[0]],
        compiler_params=_cp(("arbitrary",), VMEM_BIG),
    )(chip, *flat_p, *flat_g)


def _adam_math(w, g, m, v):
    nm = ADAM_B1 * m + (1.0 - ADAM_B1) * g
    nv = ADAM_B2 * v + (1.0 - ADAM_B2) * (g * g)
    m_hat = nm / (1.0 - ADAM_B1 ** ADAM_STEP)
    v_hat = nv / (1.0 - ADAM_B2 ** ADAM_STEP)
    return -ADAM_LR * (m_hat / (jnp.sqrt(v_hat) + ADAM_EPS) + ADAM_WD * w), nm, nv


def _adamw(w, g, m, v, rows):
    R, C = w.shape

    def body(w_ref, g_ref, m_ref, v_ref, d_ref, nm_ref, nv_ref):
        d_ref[...], nm_ref[...], nv_ref[...] = _adam_math(w_ref[...], g_ref[...], m_ref[...], v_ref[...])

    spec = pl.BlockSpec((rows, C), lambda i: (i, 0))
    sh = jax.ShapeDtypeStruct((R, C), F32)
    return pl.pallas_call(
        body, name="adamw", grid=(R // rows,), in_specs=[spec] * 4, out_specs=[spec] * 3,
        out_shape=[sh, sh, sh], compiler_params=_cp(("arbitrary",)),
    )(w, g, m, v)


def _small_adamw(tot, ws, ms, vs):
    n = len(ws)

    def body(*refs):
        tot_ref = refs[0]
        w_refs, m_refs, v_refs = refs[1:1 + n], refs[1 + n:1 + 2 * n], refs[1 + 2 * n:1 + 3 * n]
        outs = refs[1 + 3 * n:]
        for i in range(n):
            rows, width = ws[i].shape
            g = tot_ref[pl.ds(SMALL_ROW[i], rows), pl.ds(0, width)]
            outs[4 * i][...] = g
            outs[4 * i + 1][...], outs[4 * i + 2][...], outs[4 * i + 3][...] = _adam_math(
                w_refs[i][...], g, m_refs[i][...], v_refs[i][...])
        outs[4 * n][...] = tot_ref[pl.ds(LOSS_ROW, 1), pl.ds(0, LANE)]

    vmem = pl.BlockSpec(memory_space=pltpu.VMEM)
    res = pl.pallas_call(
        body, name="small_adamw", in_specs=[vmem] * (1 + 3 * n), out_specs=[vmem] * (4 * n + 1),
        out_shape=[jax.ShapeDtypeStruct(w.shape, F32) for w in ws for _ in range(4)]
        + [jax.ShapeDtypeStruct((1, LANE), F32)],
    )(tot, *ws, *ms, *vs)
    return [res[4 * i:4 * i + 4] for i in range(n)], res[4 * n]


HBM_SPEC = pl.BlockSpec(memory_space=pltpu.HBM)


def _place():
    return lax.axis_index("x"), lax.axis_index("y"), lax.axis_index("c")


def _all_gather_split(shard):
    def body(in_ref, out_ref, send_sems, recv_sems):
        direct, relayed, passed = _gather_tree_copies([in_ref], [out_ref], send_sems, recv_sems)
        for cp in direct:
            cp.start()
        for i in range(2):
            direct[i].wait_recv()
            relayed[i].start()
            passed[i].start()
        for cp in relayed:
            cp.wait_recv()
        passed[2].start()
        for cp in passed:
            cp.wait_recv()
        for cp in direct + relayed + passed:
            cp.wait_send()

    return pl.pallas_call(
        body, name="all_gather_split", in_specs=[HBM_SPEC], out_specs=HBM_SPEC,
        out_shape=jax.ShapeDtypeStruct((NCHIP,) + shard.shape, shard.dtype),
        scratch_shapes=_hosted_sems(1, GATHER_SEMS),
    )(shard)


SEM_SPEC = pl.BlockSpec(memory_space=pltpu.SEMAPHORE)
ORDERED_EFFECT = pltpu.CompilerParams(has_side_effects=pltpu.SideEffectType.DATAFLOW_SIDE_EFFECTING)


def _scatter_start(partial):
    land = pltpu.with_memory_space_constraint(
        lax.empty((NCHIP - 1,) + partial.shape[1:], partial.dtype), pltpu.HBM)

    def body(p_ref, land_ref, send_sems, recv_sems, p_thru, land_thru, token):
        for cp in _scatter_copies([p_ref], [land_ref], send_sems, recv_sems):
            cp.start()
        token[...] = jnp.zeros_like(token)

    return pl.pallas_call(
        body, name="scatter_start",
        out_shape=(pltpu.SemaphoreType.DMA((NCHIP - 1,)), pltpu.SemaphoreType.DMA((NCHIP - 1,)),
                   pltpu.HBM(partial.shape, partial.dtype), pltpu.HBM(land.shape, land.dtype),
                   jax.ShapeDtypeStruct((8, LANE), F32)),
        in_specs=(HBM_SPEC, HBM_SPEC),
        out_specs=(SEM_SPEC, SEM_SPEC, HBM_SPEC, HBM_SPEC, pl.BlockSpec(memory_space=pltpu.VMEM)),
        input_output_aliases={0: 2, 1: 3}, compiler_params=ORDERED_EFFECT,
    )(pltpu.with_memory_space_constraint(partial, pltpu.HBM), land)


def _scatter_wait(send_sems, recv_sems, p_thru, land_thru, after):
    def body(p_ref, land_ref, send_sems, recv_sems, after_ref, p_dead, got_ref):
        for cp in _scatter_copies([p_ref], [land_ref], send_sems, recv_sems):
            cp.wait_send()
            cp.wait_recv()

    return pl.pallas_call(
        body, name="scatter_wait",
        out_shape=(pltpu.HBM(p_thru.shape, p_thru.dtype), pltpu.HBM(land_thru.shape, land_thru.dtype)),
        in_specs=(HBM_SPEC, HBM_SPEC, SEM_SPEC, SEM_SPEC, pl.BlockSpec(memory_space=pl.ANY)),
        out_specs=(HBM_SPEC, HBM_SPEC), input_output_aliases={0: 0, 1: 1}, compiler_params=ORDERED_EFFECT,
    )(p_thru, land_thru, send_sems, recv_sems, after)


def _pair_exchange(grads):
    n = len(grads)

    def body(*refs):
        copies = _pair_copies(refs[:n], refs[n:2 * n], refs[2 * n], refs[2 * n + 1])
        for cp in copies:
            cp.start()
        for cp in copies:
            cp.wait()

    return pl.pallas_call(
        body, name="pair_exchange", in_specs=[HBM_SPEC] * n, out_specs=[HBM_SPEC] * n,
        out_shape=[jax.ShapeDtypeStruct((NCHIP, g.shape[1] // 2, g.shape[2]), F32) for g in grads],
        scratch_shapes=[pltpu.SemaphoreType.DMA((n,)), pltpu.SemaphoreType.DMA((n,))],
    )(*grads)


def _small_allreduce(rows, loss_blk):
    n = len(rows)

    def body(*refs):
        loss_ref, o_ref, pk, slots, send_sems, recv_sems = refs[n:]
        pk[...] = jnp.zeros_like(pk)
        for i in range(n):
            pk[pl.ds(i, 1), pl.ds(0, rows[i].shape[1])] = refs[i][...]
        pk[pl.ds(LOSS_ROW, 1), pl.ds(0, LANE)] = loss_ref[pl.ds(0, 1), :]
        x, y, c = _place()
        me = 4 * x + 2 * y + c
        slots[me] = pk[...]
        copies = []
        for r in range(1, 8):
            rx, ry, rc = (r >> 2) & 1, (r >> 1) & 1, r & 1
            peer = (x + rx - 2 * x * rx, y + ry - 2 * y * ry, c + rc - 2 * c * rc)
            cp = pltpu.make_async_remote_copy(
                src_ref=pk, dst_ref=slots.at[me], send_sem=send_sems.at[r - 1], recv_sem=recv_sems.at[r - 1],
                device_id=peer, device_id_type=MESH)
            cp.start()
            copies.append(cp)
        for cp in copies:
            cp.wait()
        acc = slots[0]
        for j in range(1, 8):
            acc = acc + slots[j]
        o_ref[...] = acc

    vmem = pl.BlockSpec(memory_space=pltpu.VMEM)
    return pl.pallas_call(
        body, name="small_allreduce", in_specs=[vmem] * (n + 1), out_specs=vmem,
        out_shape=jax.ShapeDtypeStruct((SMALL_PK, D), F32),
        scratch_shapes=[pltpu.VMEM((SMALL_PK, D), F32), pltpu.VMEM((8, SMALL_PK, D), F32),
                        pltpu.SemaphoreType.DMA((7,)), pltpu.SemaphoreType.DMA((7,))],
    )(*rows, loss_blk)


def _seg_matrix():
    i = lax.broadcasted_iota(jnp.int32, (AD, AD), 0) // DH
    j = lax.broadcasted_iota(jnp.int32, (AD, AD), 1) // DH
    return (i == j).astype(BF16)


TAIL = ("w_out", "w_ple_gate", "w_ple", "w_pw", "dw_w")


def _local_step(x, p, tgt, sm, shards, chip, ci):
    seg = _seg_matrix()
    core = jnp.reshape(ci, (1,)).astype(jnp.int32)
    chip_idx = jnp.reshape(chip, (1,)).astype(jnp.int32)
    own = lambda g, s: lax.dynamic_update_index_in_dim(g, s, chip, 0)
    w_in_next = own(_all_gather_split(shards[0]["w_in"]), shards[0]["w_in"])
    h = x
    saved = []
    for l in range(DEPTH):
        w_in = w_in_next
        row = lambda name: sm[name][l:l + 1]
        u, hn = _rms_inproj(h, row("norm_g"), w_in)
        todo = [shards[l][k] for k in TAIL] + ([shards[l + 1]["w_in"]] if l + 1 < DEPTH else [])
        o, ya, tot, got = _attn_fwd(u, jnp.tile(row("attn_out_g"), (1, AD // DH)), todo)
        got = [own(g, s) for g, s in zip(got, todo)]
        w_out = got[0].reshape(D, D)
        w_gate = got[1].reshape(D, D)
        w_ple = got[2]
        w_pw = got[3].reshape(CD, CD)
        dw = got[4]
        if l + 1 < DEPTH:
            w_in_next = got[5]
        c1 = _glu_conv(u, dw, row("dw_b"))
        c3, yc, h1, gate, pe, h2, *at_end = _layer_tail(
            c1, u, ya, h, p[l], row("conv_ln_g"), row("conv_ln_b"), w_pw, row("conv_out_g"), w_out,
            row("ple_norm_g"), w_gate, w_ple, head=(tgt, sm["final_g"]) if l == DEPTH - 1 else None)
        saved.append(dict(h=h, u=u, hn=hn, o=o, ya=ya, tot=tot, c1=c1, c3=c3, yc=yc, h1=h1, gate=gate, pe=pe,
                          w_in=w_in, w_out=w_out, w_gate=w_gate, w_pw=w_pw, dw=dw))
        h = h2
    dh, (loss_blk, dfg) = h, at_end
    small = [None] * DEPTH
    pending, partials, arrived = [], {}, {}
    pair_sum = lambda grads: _sum_pair(core, grads, _pair_exchange(grads))
    for l in reversed(range(DEPTH)):
        s = saved[l]
        row = lambda name: sm[name][l:l + 1]
        dh1, dy, dwg, dwp, dwo, dpg = _ple_out_bwd(
            dh, s["h1"], s["gate"], s["pe"], p[l], s["ya"], s["yc"], row("ple_norm_g"), s["w_gate"], s["w_out"])
        ag_t = jnp.tile(row("attn_out_g"), (1, AD // DH))
        do, dga, dgc, dc1, dwpw, dag, dcg, dlg, dlb = _branch_bwd(
            dy, s["o"], s["u"], s["c1"], s["c3"], ag_t, row("conv_ln_g"), row("conv_ln_b"), s["w_pw"],
            row("conv_out_g"), seg)
        tail = [dwo.reshape(NCHIP, 256, D), dwg.reshape(NCHIP, 256, D), dwp, dwpw.reshape(NCHIP, 128, CD)]
        dcv, dcgate, ddw, ddb, halves = _conv_bwd(dc1, s["u"], s["dw"], tail if l == 0 else ())
        tail.append(ddw)
        if l == 0:
            partials[(l, "tail")] = _sum_pair(core, tail, halves + list(_pair_exchange([ddw])))
            pending.append((l, "tail"))
        send = [t for key in pending for t in partials[key]]
        dq, dk, dv, got = _attn_bwd(s["u"], do, s["tot"], send)
        for key in pending:
            arrived[key], got = got[:len(partials[key])], got[len(partials[key]):]
        pending = []
        du = jnp.concatenate([dq, dk, dv, dga, dcv, dcgate, dgc], axis=1)
        dwin = _inproj_dw(s["hn"], du)
        fold = lambda t: t.reshape(AD // DH, DH).sum(axis=0, keepdims=True)
        if l == 0:
            in_flight = _scatter_start(pair_sum([dwin])[0])
            dh, dng, _ = _inproj_dx(du, s["w_in"], s["h"], row("norm_g") + in_flight[4][0:1, 0:1], dh1)
            small[l] = dict(norm_g=dng, attn_out_g=fold(dag), dw_b=ddb, conv_ln_g=dlg, conv_ln_b=dlb,
                            conv_out_g=dcg, ple_norm_g=dpg)
            small_tot = _small_allreduce([small[j][k] for k in SMALL2 for j in range(DEPTH)] + [dfg], loss_blk)
            landed = _scatter_wait(*in_flight[:4], small_tot)
            partials[(l, "w_in")], arrived[(l, "w_in")] = [landed[0]], [landed[1]]
        else:
            dh, dng, halves = _inproj_dx(du, s["w_in"], s["h"], row("norm_g"), dh1, grads=tail + [dwin])
            tail_p = _sum_pair(core, tail + [dwin], halves)
            partials[(l, "tail")], partials[(l, "w_in")] = tail_p[:-1], tail_p[-1:]
            pending = [(l, "tail"), (l, "w_in")]
            small[l] = dict(norm_g=dng, attn_out_g=fold(dag), dw_b=ddb, conv_ln_g=dlg, conv_ln_b=dlb,
                            conv_out_g=dcg, ple_norm_g=dpg)
    both = lambda d: [list(d[(l, "w_in")]) + list(d[(l, "tail")]) for l in range(DEPTH)]
    big = dict(zip(BIG, _sum_chips_share(chip_idx, both(partials), both(arrived))))
    return dh, big, small_tot


BIG = ("w_in", "w_out", "w_ple_gate", "w_ple", "w_pw", "dw_w")
SMALL2 = ("norm_g", "ple_norm_g", "dw_b", "conv_ln_g", "conv_ln_b", "conv_out_g", "attn_out_g")
SMALL_ROW = (0, 2, 4, 6, 8, 10, 12, 14)


def kernel(x, p, norm_g, w_in, attn_out_g, dw_w, dw_b, conv_ln_g, conv_ln_b, w_pw, conv_out_g, w_out, ple_norm_g, w_ple_gate, w_ple, final_g, loss_target, m_norm_g, m_w_in, m_attn_out_g, m_dw_w, m_dw_b, m_conv_ln_g, m_conv_ln_b, m_w_pw, m_conv_out_g, m_w_out, m_ple_norm_g, m_w_ple_gate, m_w_ple, m_final_g, v_norm_g, v_w_in, v_attn_out_g, v_dw_w, v_dw_b, v_conv_ln_g, v_conv_ln_b, v_w_pw, v_conv_out_g, v_w_out, v_ple_norm_g, v_w_ple_gate, v_w_ple, v_final_g):
    W = dict(norm_g=norm_g, w_in=w_in, attn_out_g=attn_out_g, dw_w=dw_w, dw_b=dw_b, conv_ln_g=conv_ln_g,
             conv_ln_b=conv_ln_b, w_pw=w_pw, conv_out_g=conv_out_g, w_out=w_out, ple_norm_g=ple_norm_g,
             w_ple_gate=w_ple_gate, w_ple=w_ple, final_g=final_g)
    M = dict(norm_g=m_norm_g, w_in=m_w_in, attn_out_g=m_attn_out_g, dw_w=m_dw_w, dw_b=m_dw_b,
             conv_ln_g=m_conv_ln_g, conv_ln_b=m_conv_ln_b, w_pw=m_w_pw, conv_out_g=m_conv_out_g, w_out=m_w_out,
             ple_norm_g=m_ple_norm_g, w_ple_gate=m_w_ple_gate, w_ple=m_w_ple, final_g=m_final_g)
    V = dict(norm_g=v_norm_g, w_in=v_w_in, attn_out_g=v_attn_out_g, dw_w=v_dw_w, dw_b=v_dw_b,
             conv_ln_g=v_conv_ln_g, conv_ln_b=v_conv_ln_b, w_pw=v_w_pw, conv_out_g=v_conv_out_g, w_out=v_w_out,
             ple_norm_g=v_ple_norm_g, w_ple_gate=v_w_ple_gate, w_ple=v_w_ple, final_g=v_final_g)
    order = ("norm_g", "w_in", "attn_out_g", "dw_w", "dw_b", "conv_ln_g", "conv_ln_b", "w_pw", "conv_out_g",
             "w_out", "ple_norm_g", "w_ple_gate", "w_ple", "final_g")

    pad_taps = lambda a: jnp.pad(a, ((0, 0), (0, CWP - CW), (0, 0)))
    cast = dict(w_in=w_in.astype(BF16), w_out=w_out.astype(BF16), w_ple_gate=w_ple_gate.astype(BF16),
                w_ple=w_ple.astype(BF16), w_pw=w_pw.astype(BF16), dw_w=pad_taps(dw_w))
    shards = [{k: v[l] for k, v in cast.items()} for l in range(DEPTH)]
    xi, yi, ci = lax.axis_index("x"), lax.axis_index("y"), lax.axis_index("c")
    chip = 2 * xi + yi

    sm = {k: W[k] for k in SMALL2}
    sm["final_g"] = final_g.reshape(1, D)
    grad_x, big, small_tot = _local_step(x[0], p[:, 0], loss_target[0], sm, shards, chip, ci)
    g_big = {name: big[name].reshape(cast[name].shape) for name in BIG}

    small_names = SMALL2 + ("final_g",)
    as_rows = lambda t: t.reshape(1, D) if t.ndim == 1 else t
    results, loss_row = _small_adamw(
        small_tot, [as_rows(W[k]) for k in small_names],
        [as_rows(M[k]) for k in small_names], [as_rows(V[k]) for k in small_names])
    loss = loss_row[0, 0]

    grads, deltas, new_m, new_v = {}, {}, {}, {}
    for name in BIG:
        wv = pad_taps(W[name]) if name == "dw_w" else W[name]
        mv = pad_taps(M[name]) if name == "dw_w" else M[name]
        vv = pad_taps(V[name]) if name == "dw_w" else V[name]
        gg = g_big[name]
        cols = wv.shape[-1]
        rows_total = wv.size // cols
        tile_rows = min(rows_total, 256)
        d2, m2, v2 = _adamw(wv.reshape(rows_total, cols), gg.reshape(rows_total, cols),
                            mv.reshape(rows_total, cols), vv.reshape(rows_total, cols), tile_rows)
        if name == "dw_w":
            cut = lambda a: a.reshape(DEPTH, CWP, LANE)[:, :CW]
            grads[name], deltas[name], new_m[name], new_v[name] = cut(gg), cut(d2), cut(m2), cut(v2)
        else:
            grads[name] = gg
            deltas[name], new_m[name], new_v[name] = (t.reshape(wv.shape) for t in (d2, m2, v2))
    for k, four in zip(small_names, results):
        grads[k], deltas[k], new_m[k], new_v[k] = (t.reshape(W[k].shape) for t in four)

    return (loss, grad_x[None], *[grads[n] for n in order], *[deltas[n] for n in order],
            *[new_m[n] for n in order], *[new_v[n] for n in order])
```

```python
import functools

import jax
import jax.numpy as jnp
from jax import lax
from jax.experimental import pallas as pl
from jax.experimental.pallas import tpu as pltpu

F32 = jnp.float32
BF16 = jnp.bfloat16

T = 2048
D = 1024
DIN = 3584
NCHIP = 4
SHW = DIN // NCHIP
AD = 512
CD = 512
DH = 64
CW = 31
CWP = 32
PLE = 256
DEPTH = 2
EPS = 1e-6
AQ = 256
HG = 4
GW = HG * DH
SR = HG * AQ
NG = AD // GW
LANE = 128
TM = 1024
TR = 256

ADAM_LR = 0.001
ADAM_B1 = 0.9
ADAM_B2 = 0.999
ADAM_EPS = 1e-08
ADAM_WD = 0.01
ADAM_STEP = 10

SMALL_PK = 16
LOSS_ROW = 15

VMEM_BIG = 56 * 1024 * 1024
MESH = pl.DeviceIdType.MESH


def _cp(sem=None, vmem=None):
    kw = {}
    if sem is not None:
        kw["dimension_semantics"] = sem
    if vmem is not None:
        kw["vmem_limit_bytes"] = vmem
    return pltpu.CompilerParams(**kw)


def _dot(a, b):
    return jnp.dot(a, b, preferred_element_type=F32)


def _dot_nt(a, b):
    return lax.dot_general(a, b, (((1,), (1,)), ((), ())), preferred_element_type=F32)


def _dot_tn(a, b):
    return lax.dot_general(a, b, (((0,), (0,)), ((), ())), preferred_element_type=F32)


def _dot2(x, m):
    hi = x.astype(BF16)
    lo = (x - hi.astype(F32)).astype(BF16)
    return _dot(hi, m) + _dot(lo, m)


def _sig(x):
    return 1.0 / (1.0 + jnp.exp(-x))


def _softplus(z):
    return jnp.maximum(z, 0.0) + jnp.log(1.0 + jnp.exp(-jnp.abs(z)))


def _rstd(x):
    return lax.rsqrt(jnp.mean(x * x, axis=-1, keepdims=True) + EPS)


def _rms_bwd(dy, x, r, g):
    dn = dy * g
    return r * dn - x * (r * r * r) * jnp.mean(dn * x, axis=-1, keepdims=True)


def _rms_inproj(h, g, w):
    tm = min(TM, T)

    def body(h_ref, g_ref, w_ref, u_ref, hn_ref, hn_s):
        @pl.when(pl.program_id(1) == 0)
        def _():
            hh = h_ref[...]
            hn = (hh * _rstd(hh) * g_ref[...]).astype(BF16)
            hn_s[...] = hn
            hn_ref[...] = hn
        u_ref[...] = _dot(hn_s[...], w_ref[0])

    return pl.pallas_call(
        body, name="rms_inproj", grid=(T // tm, NCHIP),
        in_specs=[pl.BlockSpec((tm, D), lambda i, k: (i, 0)),
                  pl.BlockSpec((1, D), lambda i, k: (0, 0)),
                  pl.BlockSpec((1, D, SHW), lambda i, k: (k, 0, 0))],
        out_specs=[pl.BlockSpec((tm, SHW), lambda i, k: (i, k)),
                   pl.BlockSpec((tm, D), lambda i, k: (i, 0))],
        out_shape=[jax.ShapeDtypeStruct((T, DIN), F32), jax.ShapeDtypeStruct((T, D), BF16)],
        scratch_shapes=[pltpu.VMEM((tm, D), BF16)],
        compiler_params=_cp(("arbitrary", "arbitrary"), VMEM_BIG),
    )(h, g, w)


def _attn_tiles():
    row = lax.broadcasted_iota(jnp.int32, (SR, AQ), 0) & (AQ - 1)
    col = lax.broadcasted_iota(jnp.int32, (SR, AQ), 1)
    tr = lax.broadcasted_iota(jnp.int32, (AQ, AQ), 0)
    tc = lax.broadcasted_iota(jnp.int32, (AQ, AQ), 1)
    lane_head = lax.broadcasted_iota(jnp.int32, (1, GW), 1) // DH
    return col < row, tr, tc, [lane_head == h for h in range(HG)]


def _stack_heads(t, heads):
    return jnp.concatenate([jnp.where(m, t, 0.0) for m in heads], axis=0)


def _unstack_heads(t, heads):
    out = t[:AQ]
    for h in range(1, HG):
        out = jnp.where(heads[h], t[h * AQ:(h + 1) * AQ], out)
    return out


def _tri_sum(x, tri):
    hi = x.astype(BF16)
    lo = (x - hi.astype(F32)).astype(BF16)
    both = _dot(jnp.concatenate([hi, lo], axis=0), tri)
    return both[:SR] + both[SR:]


def _scatter_copies(ps, gots, send_sems, recv_sems):
    x, y, c = _place()
    peers = [(1 - x, y), (x, 1 - y), (1 - x, 1 - y)]
    return [pltpu.make_async_remote_copy(
        src_ref=ps[a].at[2 * px + py], dst_ref=gots[a].at[r], send_sem=send_sems.at[3 * a + r],
        recv_sem=recv_sems.at[3 * a + r], device_id=(px, py, c), device_id_type=MESH)
        for a in range(len(ps)) for r, (px, py) in enumerate(peers)]


GATHER_SEMS = 7


def _gather_tree_copies(ins, outs, send_sems, recv_sems):
    x, y, c = _place()
    me, xn, yn, dg = 2 * x + y, 2 * (1 - x) + y, 2 * x + (1 - y), 2 * (1 - x) + (1 - y)
    to_x, to_y, sibling = (1 - x, y, c), (x, 1 - y, c), (x, y, 1 - c)
    direct, relayed, passed = [], [], []
    for a in range(len(ins)):
        half = ins[a].shape[0] // 2
        mine = pl.ds(c * half, half)
        first, second = pl.ds(c * half, half // 2), pl.ds(c * half + half // 2, half // 2)

        def copy(i, src, dst, to, k=GATHER_SEMS * a):
            return pltpu.make_async_remote_copy(src_ref=src, dst_ref=dst, send_sem=send_sems.at[k + i],
                                                recv_sem=recv_sems.at[k + i], device_id=to, device_id_type=MESH)

        own, slot = ins[a].at[mine], outs[a].at[me, mine]
        direct += [copy(0, own, slot, to_x), copy(1, own, slot, to_y)]
        relayed += [copy(2, outs[a].at[xn, first], outs[a].at[xn, first], to_y),
                    copy(3, outs[a].at[yn, second], outs[a].at[yn, second], to_x)]
        passed += [copy(4 + i, outs[a].at[j, mine], outs[a].at[j, mine], sibling) for i, j in enumerate((xn, yn, dg))]
    return direct, relayed, passed


def _pair_copies(ins, outs, send_sems, recv_sems):
    x, y, c = _place()
    copies = []
    for a in range(len(ins)):
        half = ins[a].shape[1] // 2
        copies.append(pltpu.make_async_remote_copy(
            src_ref=ins[a].at[:, pl.ds((1 - c) * half, half), :], dst_ref=outs[a], send_sem=send_sems.at[a],
            recv_sem=recv_sems.at[a], device_id=(x, y, 1 - c), device_id_type=MESH))
    return copies


def _host(body, grid, n_in, n_out, n_x, make_copies, mids=()):
    if not n_x:
        return body

    def hosting(*refs):
        a, b = n_in + n_x, n_in + 2 * n_x + n_out
        copies = make_copies(refs[n_in:a], refs[a + n_out:b], refs[-2], refs[-1])
        stages = copies if isinstance(copies, tuple) else (copies,)
        ids = [pl.program_id(d) for d in range(len(grid))]
        at = lambda step: functools.reduce(jnp.logical_and, [i == s for i, s in zip(ids, step)])

        @pl.when(at([0] * len(grid)))
        def _():
            for cp in stages[0]:
                cp.start()

        for before, after, step in zip(stages, stages[1:], mids):
            @pl.when(at(step))
            def _(before=before, after=after):
                for cp in before:
                    cp.wait_recv()
                for cp in after:
                    cp.start()

        body(*refs[:n_in], *refs[a:a + n_out], *refs[b:-2])

        @pl.when(at([g - 1 for g in grid]))
        def _():
            for cp in stages[-1]:
                cp.wait_recv()
            for stage in stages:
                for cp in stage:
                    cp.wait_send()

    return hosting


def _hosted_sems(n_x, per_array=3):
    n = per_array * n_x
    return [pltpu.SemaphoreType.DMA((n,)), pltpu.SemaphoreType.DMA((n,))] if n_x else []


RC = 256


def _chunk_causal(r):
    row = lax.broadcasted_iota(jnp.int32, (RC, AQ), 0) + (r * RC) % AQ
    return lax.broadcasted_iota(jnp.int32, (RC, AQ), 1) < row


def _attn_fwd(u, agw, shards=()):
    n = len(shards)
    grid = (T // AQ,)

    def body(q_ref, k_ref, v_ref, g_ref, ag_ref, o_ref, y_ref, tot_ref,
             kb_s, vb_s, qs_s, z_s, zs_s, lmb_s, suf_s, att_s, acc_s, run_s):
        qi = pl.program_id(0)

        @pl.when(qi == 0)
        def _():
            kb_s[...] = k_ref[...].astype(BF16)
            vb_s[...] = v_ref[...].astype(BF16)

        _, tr, tc, heads = _attn_tiles()
        upper = (tr > tc).astype(BF16)
        same_head = ((tr // DH) == (tc // DH)).astype(BF16)
        for g in range(NG):
            qs_s[g] = _stack_heads(q_ref[:, g * GW:(g + 1) * GW] * 0.125, heads).astype(BF16)
        acc_s[...] = jnp.zeros_like(acc_s)
        run_s[...] = jnp.zeros_like(run_s)

        def block(kb, masked):
            k0 = pl.multiple_of(kb * AQ, AQ)
            for g in range(NG):
                lanes = pl.ds(g * GW, GW)
                z_s[g] = _dot_nt(qs_s[g], kb_s[pl.ds(k0, AQ), lanes])
                for r in range(SR // RC):
                    rows = pl.ds(r * RC, RC)
                    z = z_s[g, rows, :]
                    zs = jnp.minimum(z, 0.0) - jnp.log(1.0 + jnp.exp(-jnp.abs(z)))
                    lm = zs - z
                    if masked:
                        lm = jnp.where(_chunk_causal(r), lm, 0.0)
                    run = run_s[g, rows, :]
                    zs_s[g, rows, :] = zs + run[:, 0:1]
                    hi = lm.astype(BF16)
                    lmb_s[g, rows, :] = hi
                    lmb_s[g, pl.ds(SR + r * RC, RC), :] = (lm - hi.astype(F32)).astype(BF16)
                    run_s[g, rows, :] = run + jnp.sum(lm, axis=1, keepdims=True)
                suf_s[g] = _dot(lmb_s[g], upper)
                for r in range(SR // RC):
                    rows = pl.ds(r * RC, RC)
                    att = jnp.exp(zs_s[g, rows, :] + suf_s[g, rows, :] + suf_s[g, pl.ds(SR + r * RC, RC), :])
                    if masked:
                        att = jnp.where(_chunk_causal(r), att, 0.0)
                    att_s[g, rows, :] = att.astype(BF16)
                acc_s[g] += _dot(att_s[g], vb_s[pl.ds(k0, AQ), lanes])

        block(qi, True)

        def step(i, c):
            block(qi - 1 - i, False)
            return c

        lax.fori_loop(0, qi, step, 0)
        gate = g_ref[...]
        agv = ag_ref[...]
        for g in range(NG):
            lanes = slice(g * GW, (g + 1) * GW)
            o = _unstack_heads(acc_s[g], heads)
            osq = o * o
            ms = _dot2(osq, same_head)
            gg = gate[:, lanes]
            o_ref[:, lanes] = o
            y_ref[:, lanes] = (o * lax.rsqrt(ms * (1.0 / DH) + EPS) * agv[:, lanes] * (gg * _sig(gg))).astype(BF16)
            tot_ref[g] = _unstack_heads(jnp.broadcast_to(run_s[g][:, 0:1], (SR, GW)), heads)

    tile = lambda dt, rows=SR: pltpu.VMEM((NG, rows, AQ), dt)
    scratch = [pltpu.VMEM((T, AD), BF16), pltpu.VMEM((T, AD), BF16), pltpu.VMEM((NG, SR, GW), BF16),
               tile(F32), tile(F32), tile(BF16, 2 * SR), tile(F32, 2 * SR), tile(BF16),
               pltpu.VMEM((NG, SR, GW), F32), pltpu.VMEM((NG, SR, LANE), F32)]
    col = lambda j: pl.BlockSpec((AQ, AD), lambda qi: (qi, j))
    res = pl.pallas_call(
        _host(body, grid, 5, 3, n, _gather_tree_copies, mids=((grid[0] * 5 // 8,), (grid[0] * 7 // 8,))), name="attn_fwd", grid=grid,
        in_specs=[col(0), pl.BlockSpec((T, AD), lambda qi: (0, 1)), pl.BlockSpec((T, AD), lambda qi: (0, 2)),
                  col(3), pl.BlockSpec((1, AD), lambda qi: (0, 0))] + [HBM_SPEC] * n,
        out_specs=[col(0), col(0), pl.BlockSpec((NG, AQ, GW), lambda qi: (0, qi, 0))] + [HBM_SPEC] * n,
        out_shape=[jax.ShapeDtypeStruct((T, AD), F32), jax.ShapeDtypeStruct((T, AD), BF16),
                   jax.ShapeDtypeStruct((NG, T, GW), F32)]
        + [jax.ShapeDtypeStruct((NCHIP,) + s.shape, s.dtype) for s in shards],
        scratch_shapes=scratch + _hosted_sems(n, GATHER_SEMS),
        compiler_params=_cp(("arbitrary",), VMEM_BIG),
    )(u, u, u, u, agw, *shards)
    return res[0], res[1], res[2], list(res[3:])


def _glu_conv(u, dw, db):
    tr = 256

    def body(cv_ref, cg_ref, w_ref, b_ref, c1_ref, pad_s):
        pad_s[pl.ds(0, CWP), :] = jnp.zeros((CWP, LANE), F32)
        pad_s[pl.ds(CWP, T), :] = cv_ref[...] * _sig(cg_ref[...])
        wv = w_ref[0]
        bias = b_ref[...]

        def tile(i, carry):
            r0 = pl.multiple_of(i * tr, tr)
            acc = jnp.zeros((tr, LANE), F32) + bias
            for w in range(CW):
                acc = acc + pad_s[pl.ds(r0 + (CWP - CW + 1) + w, tr), :] * wv[w:w + 1, :]
            c1_ref[pl.ds(r0, tr), :] = acc
            return carry

        lax.fori_loop(0, T // tr, tile, 0)

    return pl.pallas_call(
        body, name="glu_conv", grid=(CD // LANE,),
        in_specs=[pl.BlockSpec((T, LANE), lambda cb: (0, 16 + cb)),
                  pl.BlockSpec((T, LANE), lambda cb: (0, 20 + cb)),
                  pl.BlockSpec((1, CWP, LANE), lambda cb: (cb, 0, 0)),
                  pl.BlockSpec((1, LANE), lambda cb: (0, cb))],
        out_specs=pl.BlockSpec((T, LANE), lambda cb: (0, cb)),
        out_shape=jax.ShapeDtypeStruct((T, CD), F32),
        scratch_shapes=[pltpu.VMEM((T + CWP, LANE), F32)],
        compiler_params=_cp(("arbitrary",)),
    )(u, u, dw, db)


def _ln_silu(c1, lg, lb):
    mu = jnp.mean(c1, axis=-1, keepdims=True)
    xc = c1 - mu
    rs = lax.rsqrt(jnp.mean(xc * xc, axis=-1, keepdims=True) + EPS)
    xh = xc * rs
    ln = xh * lg + lb
    s = _sig(ln)
    return xh, rs, ln, s


def _layer_tail(c1, u, ya, h, p, lg, lb, wpw, cg, wout, pg, wgate, wple, head=None):
    tm = min(TR, T)

    def body(c1_ref, gc_ref, ya_ref, h_ref, p_ref, lg_ref, lb_ref, wpw_ref, cg_ref, wout_ref,
             pg_ref, wgate_ref, wple_ref, *rest):
        c3_ref, yc_ref, h1_ref, gate_ref, pe_ref, h2_ref = rest[-6 - 2 * bool(head):][:6]
        _, _, ln, s = _ln_silu(c1_ref[...], lg_ref[...], lb_ref[...])
        c2 = (ln * s).astype(BF16)
        c3 = _dot(c2, wpw_ref[...])
        gc = gc_ref[...]
        yc = (c3 * _rstd(c3) * cg_ref[...] * (gc * _sig(gc))).astype(BF16)
        c3_ref[...] = c3
        yc_ref[...] = yc
        y = _dot(ya_ref[...], wout_ref[pl.ds(0, AD), :]) + _dot(yc, wout_ref[pl.ds(AD, CD), :])
        h1 = h_ref[...] + y
        hn2 = (h1 * _rstd(h1) * pg_ref[...]).astype(BF16)
        gate = _sig(_dot(hn2, wgate_ref[...]))
        pb = p_ref[...].astype(BF16)
        pe = jnp.concatenate([_dot(pb, wple_ref[k]) for k in range(NCHIP)], axis=1)
        h1_ref[...] = h1
        gate_ref[...] = gate.astype(BF16)
        pe_ref[...] = pe.astype(BF16)
        h2 = h1 + pe * gate
        if not head:
            h2_ref[...] = h2
            return
        t_ref, fg_ref, loss_ref, dfg_ref = rest[0], rest[1], rest[-2], rest[-1]

        @pl.when(pl.program_id(0) == 0)
        def _():
            loss_ref[...] = jnp.zeros_like(loss_ref)
            dfg_ref[...] = jnp.zeros_like(dfg_ref)
        fg = fg_ref[...]
        r = _rstd(h2)
        e = h2 * r * fg - t_ref[...]
        loss_ref[...] += 0.5 * jnp.sum(jnp.mean(e * e, axis=-1, keepdims=True))
        dy = e * (1.0 / D)
        dfg_ref[...] += jnp.sum(dy * h2 * r, axis=0, keepdims=True)
        h2_ref[...] = _rms_bwd(dy, h2, r, fg)

    row = lambda w: pl.BlockSpec((tm, w), lambda i: (i, 0))
    full = lambda *s: pl.BlockSpec(s, lambda i: (0,) * len(s), pipeline_mode=pl.Buffered(1))
    extra = bool(head)
    return pl.pallas_call(
        body, name="layer_tail", grid=(T // tm,),
        in_specs=[row(CD), pl.BlockSpec((tm, CD), lambda i: (i, 6)), row(AD), row(D), row(PLE),
                  full(1, CD), full(1, CD), full(CD, CD), full(1, CD), full(D, D),
                  full(1, D), full(D, D), full(NCHIP, PLE, PLE)] + [row(D), full(1, D)] * extra,
        out_specs=[row(CD), row(CD), row(D), row(D), row(D), row(D)] + [full(8, LANE), full(1, D)] * extra,
        out_shape=[jax.ShapeDtypeStruct((T, CD), F32), jax.ShapeDtypeStruct((T, CD), BF16),
                   jax.ShapeDtypeStruct((T, D), F32), jax.ShapeDtypeStruct((T, D), BF16),
                   jax.ShapeDtypeStruct((T, D), BF16), jax.ShapeDtypeStruct((T, D), F32)]
        + [jax.ShapeDtypeStruct((8, LANE), F32), jax.ShapeDtypeStruct((1, D), F32)] * extra,
        compiler_params=_cp(("arbitrary",), VMEM_BIG),
    )(c1, u, ya, h, p, lg, lb, wpw, cg, wout, pg, wgate, wple, *(head or ()))


def _ple_out_bwd(dh2, h1, gate, pe, p, ya, yc, pg, wgate, wout):
    tm = min(TR, T)

    def body(dh2_ref, h1_ref, gate_ref, pe_ref, p_ref, ya_ref, yc_ref, pg_ref, wgate_ref, wout_ref,
             dh1_ref, dy_ref, dwg_ref, dwp_ref, dwo_ref, dpg_ref):
        @pl.when(pl.program_id(0) == 0)
        def _():
            dwg_ref[...] = jnp.zeros_like(dwg_ref)
            dwp_ref[...] = jnp.zeros_like(dwp_ref)
            dwo_ref[...] = jnp.zeros_like(dwo_ref)
            dpg_ref[...] = jnp.zeros_like(dpg_ref)
        dh2 = dh2_ref[...]
        h1 = h1_ref[...]
        gate = gate_ref[...].astype(F32)
        pg = pg_ref[...]
        dpe = (dh2 * gate).astype(BF16)
        dgp = (dh2 * pe_ref[...].astype(F32) * gate * (1.0 - gate)).astype(BF16)
        r = _rstd(h1)
        hn = h1 * r
        dwg_ref[...] += _dot_tn((hn * pg).astype(BF16), dgp)
        dhn2 = _dot_nt(dgp, wgate_ref[...])
        dpg_ref[...] += jnp.sum(dhn2 * hn, axis=0, keepdims=True)
        dh1 = dh2 + _rms_bwd(dhn2, h1, r, pg)
        pb = p_ref[...].astype(BF16)
        for k in range(NCHIP):
            dwp_ref[k] += _dot_tn(pb, dpe[:, k * PLE:(k + 1) * PLE])
        dh1b = dh1.astype(BF16)
        dy_ref[...] = _dot_nt(dh1b, wout_ref[...])
        dwo_ref[pl.ds(0, AD), :] += _dot_tn(ya_ref[...], dh1b)
        dwo_ref[pl.ds(AD, CD), :] += _dot_tn(yc_ref[...], dh1b)
        dh1_ref[...] = dh1

    row = lambda w: pl.BlockSpec((tm, w), lambda i: (i, 0))
    full = lambda *s: pl.BlockSpec(s, lambda i: (0,) * len(s), pipeline_mode=pl.Buffered(1))
    return pl.pallas_call(
        body, name="ple_out_bwd", grid=(T // tm,),
        in_specs=[row(D), row(D), row(D), row(D), row(PLE), row(AD), row(CD),
                  full(1, D), full(D, D), full(D, D)],
        out_specs=[row(D), row(D), full(D, D), full(NCHIP, PLE, PLE), full(D, D), full(1, D)],
        out_shape=[jax.ShapeDtypeStruct((T, D), F32), jax.ShapeDtypeStruct((T, D), F32),
                   jax.ShapeDtypeStruct((D, D), F32), jax.ShapeDtypeStruct((NCHIP, PLE, PLE), F32),
                   jax.ShapeDtypeStruct((D, D), F32), jax.ShapeDtypeStruct((1, D), F32)],
        compiler_params=_cp(("arbitrary",), VMEM_BIG),
    )(dh2, h1, gate, pe, p, ya, yc, pg, wgate, wout)


def _branch_bwd(dy, o, u, c1, c3, ag, lg, lb, wpw, cg, seg):
    tm = min(TR, T)

    def body(dya_ref, dyc_ref, o_ref, ga_ref, gc_ref, c1_ref, c3_ref, ag_ref, lg_ref, lb_ref, wpw_ref,
             cg_ref, seg_ref, do_ref, dga_ref, dgc_ref, dc1_ref, dwpw_ref, dag_ref, dcg_ref, dlg_ref, dlb_ref):
        @pl.when(pl.program_id(0) == 0)
        def _():
            for r_ in (dwpw_ref, dag_ref, dcg_ref, dlg_ref, dlb_ref):
                r_[...] = jnp.zeros_like(r_)
        dya = dya_ref[...]
        o = o_ref[...]
        ga = ga_ref[...]
        ag_v = ag_ref[...]
        seg_m = seg_ref[...]
        r = lax.rsqrt(_dot2(o * o, seg_m) * (1.0 / DH) + EPS)
        onr = o * r
        sg = _sig(ga)
        dga_ref[...] = (dya * (onr * ag_v) * (sg * (1.0 + ga * (1.0 - sg)))).astype(BF16)
        don = dya * (ga * sg)
        dag_ref[...] += jnp.sum(don * onr, axis=0, keepdims=True)
        dn = don * ag_v
        do_ref[...] = r * dn - o * (r * r * r) * (_dot2(dn * o, seg_m) * (1.0 / DH))
        dyc = dyc_ref[...]
        c3 = c3_ref[...]
        gc = gc_ref[...]
        cg_v = cg_ref[...]
        r3 = _rstd(c3)
        cn = c3 * r3
        sc = _sig(gc)
        dgc_ref[...] = (dyc * (cn * cg_v) * (sc * (1.0 + gc * (1.0 - sc)))).astype(BF16)
        dcn = dyc * (gc * sc)
        dcg_ref[...] += jnp.sum(dcn * cn, axis=0, keepdims=True)
        dc3 = _rms_bwd(dcn, c3, r3, cg_v).astype(BF16)
        lg_v = lg_ref[...]
        xh, rs, ln, s = _ln_silu(c1_ref[...], lg_v, lb_ref[...])
        c2 = (ln * s).astype(BF16)
        dwpw_ref[...] += _dot_tn(c2, dc3)
        dc2 = _dot_nt(dc3, wpw_ref[...])
        dln = dc2 * (s * (1.0 + ln * (1.0 - s)))
        dlb_ref[...] += jnp.sum(dln, axis=0, keepdims=True)
        dlg_ref[...] += jnp.sum(dln * xh, axis=0, keepdims=True)
        dxh = dln * lg_v
        dc1_ref[...] = rs * (dxh - jnp.mean(dxh, axis=-1, keepdims=True)
                             - xh * jnp.mean(dxh * xh, axis=-1, keepdims=True))

    half = lambda j: pl.BlockSpec((tm, 512), lambda i: (i, j))
    full = lambda *s: pl.BlockSpec(s, lambda i: (0,) * len(s), pipeline_mode=pl.Buffered(1))
    vec = jax.ShapeDtypeStruct((1, 512), F32)
    act = jax.ShapeDtypeStruct((T, 512), F32)
    return pl.pallas_call(
        body, name="branch_bwd", grid=(T // tm,),
        in_specs=[half(0), half(1), half(0), half(3), half(6), half(0), half(0),
                  full(1, AD), full(1, CD), full(1, CD), full(CD, CD), full(1, CD), full(AD, AD)],
        out_specs=[half(0), half(0), half(0), half(0), full(CD, CD), full(1, 512), full(1, 512),
                   full(1, 512), full(1, 512)],
        out_shape=[act, jax.ShapeDtypeStruct((T, 512), BF16), jax.ShapeDtypeStruct((T, 512), BF16), act,
                   jax.ShapeDtypeStruct((CD, CD), F32), vec, vec, vec, vec],
        compiler_params=_cp(("arbitrary",), VMEM_BIG),
    )(dy, dy, o, u, u, c1, c3, ag, lg, lb, wpw, cg, seg)


def _conv_bwd(dc1, u, dw, grads=()):
    tr = 64
    n_x = len(grads)
    grid = (CD // LANE,)
    off = CWP - CW + 1

    def body(d_ref, cv_ref, cg_ref, w_ref, dcv_ref, dcg_ref, ddw_ref, ddb_ref, padc_s, padd_s, acc_s):
        cv = cv_ref[...]
        sg = _sig(cg_ref[...])
        padc_s[pl.ds(0, CWP), :] = jnp.zeros((CWP, LANE), F32)
        padc_s[pl.ds(CWP, T), :] = cv * sg
        padd_s[pl.ds(0, T), :] = d_ref[...]
        padd_s[pl.ds(T, CWP), :] = jnp.zeros((CWP, LANE), F32)
        acc_s[...] = jnp.zeros_like(acc_s)
        wv = w_ref[0]

        def tile(i, carry):
            r0 = pl.multiple_of(i * tr, tr)
            dt = padd_s[pl.ds(r0, tr), :]
            dc0 = jnp.zeros((tr, LANE), F32)
            for w in range(CW):
                dc0 = dc0 + padd_s[pl.ds(r0 + (CW - 1) - w, tr), :] * wv[w:w + 1, :]
                prod = dt * padc_s[pl.ds(r0 + off + w, tr), :]
                acc_s[w] += jnp.sum(prod.reshape(tr // 8, 8, LANE), axis=0)
            cvt = cv_ref[pl.ds(r0, tr), :]
            sgt = _sig(cg_ref[pl.ds(r0, tr), :])
            dcv_ref[pl.ds(r0, tr), :] = (dc0 * sgt).astype(BF16)
            dcg_ref[pl.ds(r0, tr), :] = (dc0 * cvt * sgt * (1.0 - sgt)).astype(BF16)
            return carry

        lax.fori_loop(0, T // tr, tile, 0)
        ddw_ref[0] = jnp.sum(acc_s[...], axis=1)
        ddb_ref[...] = jnp.sum(d_ref[...], axis=0, keepdims=True)

    col = lambda j: pl.BlockSpec((T, LANE), lambda cb: (0, j + cb))
    res = pl.pallas_call(
        _host(body, grid, 4, 4, n_x, _pair_copies), name="conv_bwd", grid=grid,
        in_specs=[col(0), col(16), col(20), pl.BlockSpec((1, CWP, LANE), lambda cb: (cb, 0, 0))] + [HBM_SPEC] * n_x,
        out_specs=[col(0), col(0), pl.BlockSpec((1, CWP, LANE), lambda cb: (cb, 0, 0)),
                   pl.BlockSpec((1, LANE), lambda cb: (0, cb))] + [HBM_SPEC] * n_x,
        out_shape=[jax.ShapeDtypeStruct((T, CD), BF16), jax.ShapeDtypeStruct((T, CD), BF16),
                   jax.ShapeDtypeStruct((NCHIP, CWP, LANE), F32), jax.ShapeDtypeStruct((1, CD), F32)]
        + [jax.ShapeDtypeStruct((NCHIP, g.shape[1] // 2, g.shape[2]), F32) for g in grads],
        scratch_shapes=[pltpu.VMEM((T + CWP, LANE), F32), pltpu.VMEM((T + CWP, LANE), F32),
                        pltpu.VMEM((CWP, 8, LANE), F32)] + _hosted_sems(n_x),
        compiler_params=_cp(("arbitrary",)),
    )(dc1, u, u, dw, *grads)
    return res[0], res[1], res[2], res[3], list(res[4:])


def _attn_bwd(u, do, tot, partials=()):
    n_x = len(partials)
    grid = (T // AQ,)

    def body(q_ref, k_ref, v_ref, do_ref, tot_ref, dq_ref, dk_ref, dv_ref, kb_s, vb_s, dk_s, dv_s):
        qi = pl.program_id(0)

        @pl.when(qi == 0)
        def _():
            kb_s[...] = k_ref[...].astype(BF16)
            vb_s[...] = v_ref[...].astype(BF16)
            dk_s[...] = jnp.zeros_like(dk_s)
            dv_s[...] = jnp.zeros_like(dv_s)

        causal, tr, tc, heads = _attn_tiles()
        upper = (tr > tc).astype(BF16)
        lower = (tr < tc).astype(BF16)
        qs, qus, dos, tots = [], [], [], []
        for g in range(NG):
            lanes = slice(g * GW, (g + 1) * GW)
            q = q_ref[:, lanes]
            qs.append(_stack_heads(q * 0.125, heads).astype(BF16))
            qus.append(_stack_heads(q, heads).astype(BF16))
            dos.append(_stack_heads(do_ref[:, lanes], heads).astype(BF16))
            totv = tot_ref[g]
            tots.append(jnp.concatenate([totv[:, h * DH:h * DH + 1] for h in range(HG)], axis=0))

        def block(kb, carry, masked):
            k0 = pl.multiple_of(kb * AQ, AQ)
            out = []
            for g in range(NG):
                lanes = pl.ds(g * GW, GW)
                lm_left, dl_left, dq = carry[g]
                kk = kb_s[pl.ds(k0, AQ), lanes]
                vv = vb_s[pl.ds(k0, AQ), lanes]
                z = _dot_nt(qs[g], kk)
                sp = _softplus(z)
                lm = jnp.where(causal, -sp, 0.0) if masked else -sp
                lm_incl = lm_left + jnp.sum(lm, axis=1, keepdims=True)
                att = jnp.exp((z - sp) + _tri_sum(lm, upper) + (tots[g] - lm_incl))
                if masked:
                    att = jnp.where(causal, att, 0.0)
                dl = att * _dot_nt(dos[g], vv)
                dv_s[pl.ds(k0, AQ), lanes] += _dot_tn(att.astype(BF16), dos[g])
                prefix = dl_left + _tri_sum(dl, lower)
                beta = jnp.exp(z - sp)
                dz = (1.0 - beta) * dl - beta * prefix
                if masked:
                    dz = jnp.where(causal, dz, 0.0)
                dzs = (dz * 0.125).astype(BF16)
                dk_s[pl.ds(k0, AQ), lanes] += _dot_tn(dzs, qus[g])
                out.append((lm_incl, dl_left + jnp.sum(dl, axis=1, keepdims=True), dq + _dot(dzs, kk)))
            return tuple(out)

        zero = jnp.zeros((SR, 1), F32)
        init = tuple((zero, zero, jnp.zeros((SR, GW), F32)) for _ in range(NG))
        carry = lax.fori_loop(0, qi, lambda kb, c: block(kb, c, False), init)
        carry = block(qi, carry, True)
        for g in range(NG):
            dq_ref[:, g * GW:(g + 1) * GW] = _unstack_heads(carry[g][2], heads).astype(BF16)

        @pl.when(qi == grid[0] - 1)
        def _():
            dk_ref[...] = dk_s[...].astype(BF16)
            dv_ref[...] = dv_s[...].astype(BF16)

    col = lambda j: pl.BlockSpec((AQ, AD), lambda qi: (qi, j))
    whole = lambda j: pl.BlockSpec((T, AD), lambda qi: (0, j), pipeline_mode=pl.Buffered(1))
    res = pl.pallas_call(
        _host(body, grid, 5, 3, n_x, _scatter_copies), name="attn_bwd", grid=grid,
        in_specs=[col(0), whole(1), whole(2), col(0), pl.BlockSpec((NG, AQ, GW), lambda qi: (0, qi, 0))]
        + [HBM_SPEC] * n_x,
        out_specs=[col(0), whole(0), whole(0)] + [HBM_SPEC] * n_x,
        out_shape=[jax.ShapeDtypeStruct((T, AD), BF16)] * 3
        + [jax.ShapeDtypeStruct((NCHIP - 1,) + a.shape[1:], a.dtype) for a in partials],
        scratch_shapes=[pltpu.VMEM((T, AD), BF16), pltpu.VMEM((T, AD), BF16), pltpu.VMEM((T, AD), F32),
                        pltpu.VMEM((T, AD), F32)] + _hosted_sems(n_x),
        compiler_params=_cp(("arbitrary",), VMEM_BIG),
    )(u, u, u, do, tot, *partials)
    return res[0], res[1], res[2], list(res[3:])


def _inproj_dw(hn, du):
    tm = min(TM, T)

    def body(hn_ref, du_ref, dw_ref):
        @pl.when(pl.program_id(1) == 0)
        def _():
            dw_ref[...] = jnp.zeros_like(dw_ref)
        dw_ref[0] += _dot_tn(hn_ref[...], du_ref[...])

    return pl.pallas_call(
        body, name="inproj_dw", grid=(NCHIP, T // tm),
        in_specs=[pl.BlockSpec((tm, D), lambda k, i: (i, 0)), pl.BlockSpec((tm, SHW), lambda k, i: (i, k))],
        out_specs=pl.BlockSpec((1, D, SHW), lambda k, i: (k, 0, 0)),
        out_shape=jax.ShapeDtypeStruct((NCHIP, D, SHW), F32),
        compiler_params=_cp(("arbitrary", "arbitrary"), VMEM_BIG),
    )(hn, du)


def _inproj_dx(du, w, h, g, dres, partials=(), grads=()):
    tm = min(TM, T)
    sent = list(partials) + list(grads)
    n_x = len(sent)
    grid = (T // tm, NCHIP)
    if grads:
        landing = [jax.ShapeDtypeStruct((NCHIP, a.shape[1] // 2, a.shape[2]), F32) for a in grads]
    else:
        landing = [jax.ShapeDtypeStruct((NCHIP - 1,) + a.shape[1:], a.dtype) for a in partials]

    def body(du_ref, w_ref, h_ref, g_ref, dres_ref, dh_ref, dg_ref, acc_s):
        i, k = pl.program_id(0), pl.program_id(1)

        @pl.when(jnp.logical_and(i == 0, k == 0))
        def _():
            dg_ref[...] = jnp.zeros_like(dg_ref)

        @pl.when(k == 0)
        def _():
            acc_s[...] = _dot_nt(du_ref[...], w_ref[0])

        @pl.when(k > 0)
        def _():
            acc_s[...] += _dot_nt(du_ref[...], w_ref[0])

        @pl.when(k == NCHIP - 1)
        def _():
            hh = h_ref[...]
            r = _rstd(hh)
            dhn = acc_s[...]
            dg_ref[...] += jnp.sum(dhn * hh * r, axis=0, keepdims=True)
            dh_ref[...] = dres_ref[...] + _rms_bwd(dhn, hh, r, g_ref[...])

    res = pl.pallas_call(
        _host(body, grid, 5, 2, n_x, _pair_copies if grads else _scatter_copies), name="inproj_dx", grid=grid,
        in_specs=[pl.BlockSpec((tm, SHW), lambda i, k: (i, k)),
                  pl.BlockSpec((1, D, SHW), lambda i, k: (k, 0, 0)),
                  pl.BlockSpec((tm, D), lambda i, k: (i, 0)),
                  pl.BlockSpec((1, D), lambda i, k: (0, 0)),
                  pl.BlockSpec((tm, D), lambda i, k: (i, 0))] + [HBM_SPEC] * n_x,
        out_specs=[pl.BlockSpec((tm, D), lambda i, k: (i, 0)), pl.BlockSpec((1, D), lambda i, k: (0, 0))]
        + [HBM_SPEC] * n_x,
        out_shape=[jax.ShapeDtypeStruct((T, D), F32), jax.ShapeDtypeStruct((1, D), F32)] + landing,
        scratch_shapes=[pltpu.VMEM((tm, D), F32)] + _hosted_sems(n_x),
        compiler_params=_cp(("arbitrary", "arbitrary"), VMEM_BIG),
    )(du, w, h, g, dres, *sent)
    return res[0], res[1], list(res[2:])


def _sum_pair(core, grads, gots):
    n = len(grads)

    def body(c_ref, *refs):
        for a in range(n):
            refs[2 * n + a][...] = (refs[a][...] + refs[n + a][...]).astype(BF16)

    mine = [pl.BlockSpec((1,) + s.shape[1:], lambda k, c: (k, c[0], 0)) for s in gots]
    same = [pl.BlockSpec((1,) + s.shape[1:], lambda k, c: (k, 0, 0)) for s in gots]
    return pl.pallas_call(
        body, name="sum_pair",
        grid_spec=pltpu.PrefetchScalarGridSpec(
            num_scalar_prefetch=1, grid=(NCHIP,), in_specs=mine + same, out_specs=same),
        out_shape=[jax.ShapeDtypeStruct(s.shape, BF16) for s in gots],
        compiler_params=_cp(("arbitrary",), VMEM_BIG),
    )(core, *grads, *gots)


def _sum_chips_share(chip, partials, gots):
    flat_p = [p for layer in partials for p in layer]
    flat_g = [g for layer in gots for g in layer]
    n, per_layer = len(flat_p), len(partials[0])

    def body(c_ref, *refs):
        full, sums = refs[2 * n:2 * n + per_layer], refs[2 * n + per_layer:3 * n + per_layer]
        send_sems, recv_sems, local_sems = refs[3 * n + per_layer:]
        x, y, c = _place()
        copies = []
        for i in range(n):
            acc = refs[i][0].astype(F32)
            for j in range(NCHIP - 1):
                acc = acc + refs[n + i][j].astype(F32)
            sums[i][...] = acc
            half = flat_p[i].shape[1]
            rows = full[i % per_layer].at[i // per_layer, pl.ds(c * half, half)]
            copies.append(pltpu.make_async_copy(sums[i], rows, local_sems.at[i]))
            copies.append(pltpu.make_async_remote_copy(
                src_ref=sums[i], dst_ref=rows, send_sem=send_sems.at[i], recv_sem=recv_sems.at[i],
                device_id=(x, y, 1 - c), device_id_type=MESH))
        for cp in copies:
            cp.start()
        for cp in copies:
            cp.wait()

    return pl.pallas_call(
        body, name="sum_chips_share",
        grid_spec=pltpu.PrefetchScalarGridSpec(
            num_scalar_prefetch=1, grid=(1,),
            in_specs=[pl.BlockSpec((1,) + s.shape[1:], lambda i, c: (c[0], 0, 0)) for s in flat_p]
            + [pl.BlockSpec(s.shape, lambda i, c: (0, 0, 0)) for s in flat_g],
            out_specs=[HBM_SPEC] * per_layer,
            scratch_shapes=[pltpu.VMEM(s.shape[1:], F32) for s in flat_p]
            + [pltpu.SemaphoreType.DMA((n,)), pltpu.SemaphoreType.DMA((n,)), pltpu.SemaphoreType.DMA((n,))]),
        out_shape=[jax.ShapeDtypeStruct((len(partials), 2 * s.shape[1], s.shape[2]), F32) for s in partials[0]],
        compiler_params=_cp(("arbitrary",), VMEM_BIG),
    )(chip, *flat_p, *flat_g)


def _adam_math(w, g, m, v):
    nm = ADAM_B1 * m + (1.0 - ADAM_B1) * g
    nv = ADAM_B2 * v + (1.0 - ADAM_B2) * (g * g)
    m_hat = nm / (1.0 - ADAM_B1 ** ADAM_STEP)
    v_hat = nv / (1.0 - ADAM_B2 ** ADAM_STEP)
    return -ADAM_LR * (m_hat / (jnp.sqrt(v_hat) + ADAM_EPS) + ADAM_WD * w), nm, nv


def _adamw(w, g, m, v, rows):
    R, C = w.shape

    def body(w_ref, g_ref, m_ref, v_ref, g_out, d_ref, nm_ref, nv_ref):
        g = g_ref[...]
        g_out[...] = g
        d_ref[...], nm_ref[...], nv_ref[...] = _adam_math(w_ref[...], g, m_ref[...], v_ref[...])

    spec = pl.BlockSpec((rows, C), lambda i: (i, 0))
    sh = jax.ShapeDtypeStruct((R, C), F32)
    return pl.pallas_call(
        body, name="adamw", grid=(R // rows,), in_specs=[spec] * 4, out_specs=[spec] * 4,
        out_shape=[sh, sh, sh, sh], compiler_params=_cp(("arbitrary",)),
    )(w, g, m, v)


def _small_adamw(tot, ws, ms, vs):
    n = len(ws)

    def body(*refs):
        tot_ref = refs[0]
        w_refs, m_refs, v_refs = refs[1:1 + n], refs[1 + n:1 + 2 * n], refs[1 + 2 * n:1 + 3 * n]
        outs = refs[1 + 3 * n:]
        for i in range(n):
            rows, width = ws[i].shape
            g = tot_ref[pl.ds(SMALL_ROW[i], rows), pl.ds(0, width)]
            outs[4 * i][...] = g
            outs[4 * i + 1][...], outs[4 * i + 2][...], outs[4 * i + 3][...] = _adam_math(
                w_refs[i][...], g, m_refs[i][...], v_refs[i][...])
        outs[4 * n][...] = tot_ref[pl.ds(LOSS_ROW, 1), pl.ds(0, LANE)]

    vmem = pl.BlockSpec(memory_space=pltpu.VMEM)
    res = pl.pallas_call(
        body, name="small_adamw", in_specs=[vmem] * (1 + 3 * n), out_specs=[vmem] * (4 * n + 1),
        out_shape=[jax.ShapeDtypeStruct(w.shape, F32) for w in ws for _ in range(4)]
        + [jax.ShapeDtypeStruct((1, LANE), F32)],
    )(tot, *ws, *ms, *vs)
    return [res[4 * i:4 * i + 4] for i in range(n)], res[4 * n]


HBM_SPEC = pl.BlockSpec(memory_space=pltpu.HBM)


def _place():
    return lax.axis_index("x"), lax.axis_index("y"), lax.axis_index("c")


def _all_gather_split(shard):
    def body(in_ref, out_ref, send_sems, recv_sems):
        direct, relayed, passed = _gather_tree_copies([in_ref], [out_ref], send_sems, recv_sems)
        for cp in direct:
            cp.start()
        for i in range(2):
            direct[i].wait_recv()
            relayed[i].start()
            passed[i].start()
        for cp in relayed:
            cp.wait_recv()
        passed[2].start()
        for cp in passed:
            cp.wait_recv()
        for cp in direct + relayed + passed:
            cp.wait_send()

    return pl.pallas_call(
        body, name="all_gather_split", in_specs=[HBM_SPEC], out_specs=HBM_SPEC,
        out_shape=jax.ShapeDtypeStruct((NCHIP,) + shard.shape, shard.dtype),
        scratch_shapes=_hosted_sems(1, GATHER_SEMS),
    )(shard)


SEM_SPEC = pl.BlockSpec(memory_space=pltpu.SEMAPHORE)
ORDERED_EFFECT = pltpu.CompilerParams(has_side_effects=pltpu.SideEffectType.DATAFLOW_SIDE_EFFECTING)


def _scatter_start(partial):
    land = pltpu.with_memory_space_constraint(
        lax.empty((NCHIP - 1,) + partial.shape[1:], partial.dtype), pltpu.HBM)

    def body(p_ref, land_ref, send_sems, recv_sems, p_thru, land_thru, token):
        for cp in _scatter_copies([p_ref], [land_ref], send_sems, recv_sems):
            cp.start()
        token[...] = jnp.zeros_like(token)

    return pl.pallas_call(
        body, name="scatter_start",
        out_shape=(pltpu.SemaphoreType.DMA((NCHIP - 1,)), pltpu.SemaphoreType.DMA((NCHIP - 1,)),
                   pltpu.HBM(partial.shape, partial.dtype), pltpu.HBM(land.shape, land.dtype),
                   jax.ShapeDtypeStruct((8, LANE), F32)),
        in_specs=(HBM_SPEC, HBM_SPEC),
        out_specs=(SEM_SPEC, SEM_SPEC, HBM_SPEC, HBM_SPEC, pl.BlockSpec(memory_space=pltpu.VMEM)),
        input_output_aliases={0: 2, 1: 3}, compiler_params=ORDERED_EFFECT,
    )(pltpu.with_memory_space_constraint(partial, pltpu.HBM), land)


def _scatter_wait(send_sems, recv_sems, p_thru, land_thru, after):
    def body(p_ref, land_ref, send_sems, recv_sems, after_ref, p_dead, got_ref):
        for cp in _scatter_copies([p_ref], [land_ref], send_sems, recv_sems):
            cp.wait_send()
            cp.wait_recv()

    return pl.pallas_call(
        body, name="scatter_wait",
        out_shape=(pltpu.HBM(p_thru.shape, p_thru.dtype), pltpu.HBM(land_thru.shape, land_thru.dtype)),
        in_specs=(HBM_SPEC, HBM_SPEC, SEM_SPEC, SEM_SPEC, pl.BlockSpec(memory_space=pl.ANY)),
        out_specs=(HBM_SPEC, HBM_SPEC), input_output_aliases={0: 0, 1: 1}, compiler_params=ORDERED_EFFECT,
    )(p_thru, land_thru, send_sems, recv_sems, after)


def _pair_exchange(grads):
    n = len(grads)

    def body(*refs):
        copies = _pair_copies(refs[:n], refs[n:2 * n], refs[2 * n], refs[2 * n + 1])
        for cp in copies:
            cp.start()
        for cp in copies:
            cp.wait()

    return pl.pallas_call(
        body, name="pair_exchange", in_specs=[HBM_SPEC] * n, out_specs=[HBM_SPEC] * n,
        out_shape=[jax.ShapeDtypeStruct((NCHIP, g.shape[1] // 2, g.shape[2]), F32) for g in grads],
        scratch_shapes=[pltpu.SemaphoreType.DMA((n,)), pltpu.SemaphoreType.DMA((n,))],
    )(*grads)


def _small_allreduce(rows, loss_blk):
    n = len(rows)

    def body(*refs):
        loss_ref, o_ref, pk, slots, send_sems, recv_sems = refs[n:]
        pk[...] = jnp.zeros_like(pk)
        for i in range(n):
            pk[pl.ds(i, 1), pl.ds(0, rows[i].shape[1])] = refs[i][...]
        pk[pl.ds(LOSS_ROW, 1), pl.ds(0, LANE)] = loss_ref[pl.ds(0, 1), :]
        x, y, c = _place()
        me = 4 * x + 2 * y + c
        slots[me] = pk[...]
        copies = []
        for r in range(1, 8):
            rx, ry, rc = (r >> 2) & 1, (r >> 1) & 1, r & 1
            peer = (x + rx - 2 * x * rx, y + ry - 2 * y * ry, c + rc - 2 * c * rc)
            cp = pltpu.make_async_remote_copy(
                src_ref=pk, dst_ref=slots.at[me], send_sem=send_sems.at[r - 1], recv_sem=recv_sems.at[r - 1],
                device_id=peer, device_id_type=MESH)
            cp.start()
            copies.append(cp)
        for cp in copies:
            cp.wait()
        acc = slots[0]
        for j in range(1, 8):
            acc = acc + slots[j]
        o_ref[...] = acc

    vmem = pl.BlockSpec(memory_space=pltpu.VMEM)
    return pl.pallas_call(
        body, name="small_allreduce", in_specs=[vmem] * (n + 1), out_specs=vmem,
        out_shape=jax.ShapeDtypeStruct((SMALL_PK, D), F32),
        scratch_shapes=[pltpu.VMEM((SMALL_PK, D), F32), pltpu.VMEM((8, SMALL_PK, D), F32),
                        pltpu.SemaphoreType.DMA((7,)), pltpu.SemaphoreType.DMA((7,))],
    )(*rows, loss_blk)


def _seg_matrix():
    i = lax.broadcasted_iota(jnp.int32, (AD, AD), 0) // DH
    j = lax.broadcasted_iota(jnp.int32, (AD, AD), 1) // DH
    return (i == j).astype(BF16)


TAIL = ("w_out", "w_ple_gate", "w_ple", "w_pw", "dw_w")


def _local_step(x, p, tgt, sm, shards, chip, ci):
    seg = _seg_matrix()
    core = jnp.reshape(ci, (1,)).astype(jnp.int32)
    chip_idx = jnp.reshape(chip, (1,)).astype(jnp.int32)
    own = lambda g, s: lax.dynamic_update_index_in_dim(g, s, chip, 0)
    w_in_next = own(_all_gather_split(shards[0]["w_in"]), shards[0]["w_in"])
    h = x
    saved = []
    for l in range(DEPTH):
        w_in = w_in_next
        row = lambda name: sm[name][l:l + 1]
        u, hn = _rms_inproj(h, row("norm_g"), w_in)
        todo = [shards[l][k] for k in TAIL] + ([shards[l + 1]["w_in"]] if l + 1 < DEPTH else [])
        o, ya, tot, got = _attn_fwd(u, jnp.tile(row("attn_out_g"), (1, AD // DH)), todo)
        got = [own(g, s) for g, s in zip(got, todo)]
        w_out = got[0].reshape(D, D)
        w_gate = got[1].reshape(D, D)
        w_ple = got[2]
        w_pw = got[3].reshape(CD, CD)
        dw = got[4]
        if l + 1 < DEPTH:
            w_in_next = got[5]
        c1 = _glu_conv(u, dw, row("dw_b"))
        c3, yc, h1, gate, pe, h2, *at_end = _layer_tail(
            c1, u, ya, h, p[l], row("conv_ln_g"), row("conv_ln_b"), w_pw, row("conv_out_g"), w_out,
            row("ple_norm_g"), w_gate, w_ple, head=(tgt, sm["final_g"]) if l == DEPTH - 1 else None)
        saved.append(dict(h=h, u=u, hn=hn, o=o, ya=ya, tot=tot, c1=c1, c3=c3, yc=yc, h1=h1, gate=gate, pe=pe,
                          w_in=w_in, w_out=w_out, w_gate=w_gate, w_pw=w_pw, dw=dw))
        h = h2
    dh, (loss_blk, dfg) = h, at_end
    small = [None] * DEPTH
    pending, partials, arrived = [], {}, {}
    pair_sum = lambda grads: _sum_pair(core, grads, _pair_exchange(grads))
    for l in reversed(range(DEPTH)):
        s = saved[l]
        row = lambda name: sm[name][l:l + 1]
        dh1, dy, dwg, dwp, dwo, dpg = _ple_out_bwd(
            dh, s["h1"], s["gate"], s["pe"], p[l], s["ya"], s["yc"], row("ple_norm_g"), s["w_gate"], s["w_out"])
        ag_t = jnp.tile(row("attn_out_g"), (1, AD // DH))
        do, dga, dgc, dc1, dwpw, dag, dcg, dlg, dlb = _branch_bwd(
            dy, s["o"], s["u"], s["c1"], s["c3"], ag_t, row("conv_ln_g"), row("conv_ln_b"), s["w_pw"],
            row("conv_out_g"), seg)
        tail = [dwo.reshape(NCHIP, 256, D), dwg.reshape(NCHIP, 256, D), dwp, dwpw.reshape(NCHIP, 128, CD)]
        dcv, dcgate, ddw, ddb, halves = _conv_bwd(dc1, s["u"], s["dw"], tail if l == 0 else ())
        tail.append(ddw)
        if l == 0:
            partials[(l, "tail")] = _sum_pair(core, tail, halves + list(_pair_exchange([ddw])))
            pending.append((l, "tail"))
        send = [t for key in pending for t in partials[key]]
        dq, dk, dv, got = _attn_bwd(s["u"], do, s["tot"], send)
        for key in pending:
            arrived[key], got = got[:len(partials[key])], got[len(partials[key]):]
        pending = []
        du = jnp.concatenate([dq, dk, dv, dga, dcv, dcgate, dgc], axis=1)
        dwin = _inproj_dw(s["hn"], du)
        fold = lambda t: t.reshape(AD // DH, DH).sum(axis=0, keepdims=True)
        if l == 0:
            in_flight = _scatter_start(pair_sum([dwin])[0])
            dh, dng, _ = _inproj_dx(du, s["w_in"], s["h"], row("norm_g") + in_flight[4][0:1, 0:1], dh1)
            small[l] = dict(norm_g=dng, attn_out_g=fold(dag), dw_b=ddb, conv_ln_g=dlg, conv_ln_b=dlb,
                            conv_out_g=dcg, ple_norm_g=dpg)
            small_tot = _small_allreduce([small[j][k] for k in SMALL2 for j in range(DEPTH)] + [dfg], loss_blk)
            landed = _scatter_wait(*in_flight[:4], small_tot)
            partials[(l, "w_in")], arrived[(l, "w_in")] = [landed[0]], [landed[1]]
        else:
            dh, dng, halves = _inproj_dx(du, s["w_in"], s["h"], row("norm_g"), dh1, grads=tail + [dwin])
            tail_p = _sum_pair(core, tail + [dwin], halves)
            partials[(l, "tail")], partials[(l, "w_in")] = tail_p[:-1], tail_p[-1:]
            pending = [(l, "tail"), (l, "w_in")]
            small[l] = dict(norm_g=dng, attn_out_g=fold(dag), dw_b=ddb, conv_ln_g=dlg, conv_ln_b=dlb,
                            conv_out_g=dcg, ple_norm_g=dpg)
    both = lambda d: [list(d[(l, "w_in")]) + list(d[(l, "tail")]) for l in range(DEPTH)]
    big = dict(zip(BIG, _sum_chips_share(chip_idx, both(partials), both(arrived))))
    return dh, big, small_tot


BIG = ("w_in", "w_out", "w_ple_gate", "w_ple", "w_pw", "dw_w")
SMALL2 = ("norm_g", "ple_norm_g", "dw_b", "conv_ln_g", "conv_ln_b", "conv_out_g", "attn_out_g")
SMALL_ROW = (0, 2, 4, 6, 8, 10, 12, 14)


def kernel(x, p, norm_g, w_in, attn_out_g, dw_w, dw_b, conv_ln_g, conv_ln_b, w_pw, conv_out_g, w_out, ple_norm_g, w_ple_gate, w_ple, final_g, loss_target, m_norm_g, m_w_in, m_attn_out_g, m_dw_w, m_dw_b, m_conv_ln_g, m_conv_ln_b, m_w_pw, m_conv_out_g, m_w_out, m_ple_norm_g, m_w_ple_gate, m_w_ple, m_final_g, v_norm_g, v_w_in, v_attn_out_g, v_dw_w, v_dw_b, v_conv_ln_g, v_conv_ln_b, v_w_pw, v_conv_out_g, v_w_out, v_ple_norm_g, v_w_ple_gate, v_w_ple, v_final_g):
    W = dict(norm_g=norm_g, w_in=w_in, attn_out_g=attn_out_g, dw_w=dw_w, dw_b=dw_b, conv_ln_g=conv_ln_g,
             conv_ln_b=conv_ln_b, w_pw=w_pw, conv_out_g=conv_out_g, w_out=w_out, ple_norm_g=ple_norm_g,
             w_ple_gate=w_ple_gate, w_ple=w_ple, final_g=final_g)
    M = dict(norm_g=m_norm_g, w_in=m_w_in, attn_out_g=m_attn_out_g, dw_w=m_dw_w, dw_b=m_dw_b,
             conv_ln_g=m_conv_ln_g, conv_ln_b=m_conv_ln_b, w_pw=m_w_pw, conv_out_g=m_conv_out_g, w_out=m_w_out,
             ple_norm_g=m_ple_norm_g, w_ple_gate=m_w_ple_gate, w_ple=m_w_ple, final_g=m_final_g)
    V = dict(norm_g=v_norm_g, w_in=v_w_in, attn_out_g=v_attn_out_g, dw_w=v_dw_w, dw_b=v_dw_b,
             conv_ln_g=v_conv_ln_g, conv_ln_b=v_conv_ln_b, w_pw=v_w_pw, conv_out_g=v_conv_out_g, w_out=v_w_out,
             ple_norm_g=v_ple_norm_g, w_ple_gate=v_w_ple_gate, w_ple=v_w_ple, final_g=v_final_g)
    order = ("norm_g", "w_in", "attn_out_g", "dw_w", "dw_b", "conv_ln_g", "conv_ln_b", "w_pw", "conv_out_g",
             "w_out", "ple_norm_g", "w_ple_gate", "w_ple", "final_g")

    pad_taps = lambda a: jnp.pad(a, ((0, 0), (0, CWP - CW), (0, 0)))
    cast = dict(w_in=w_in.astype(BF16), w_out=w_out.astype(BF16), w_ple_gate=w_ple_gate.astype(BF16),
                w_ple=w_ple.astype(BF16), w_pw=w_pw.astype(BF16), dw_w=pad_taps(dw_w))
    shards = [{k: v[l] for k, v in cast.items()} for l in range(DEPTH)]
    xi, yi, ci = lax.axis_index("x"), lax.axis_index("y"), lax.axis_index("c")
    chip = 2 * xi + yi

    sm = {k: W[k] for k in SMALL2}
    sm["final_g"] = final_g.reshape(1, D)
    grad_x, big, small_tot = _local_step(x[0], p[:, 0], loss_target[0], sm, shards, chip, ci)
    g_big = {name: big[name].reshape(cast[name].shape) for name in BIG}

    small_names = SMALL2 + ("final_g",)
    as_rows = lambda t: t.reshape(1, D) if t.ndim == 1 else t
    results, loss_row = _small_adamw(
        small_tot, [as_rows(W[k]) for k in small_names],
        [as_rows(M[k]) for k in small_names], [as_rows(V[k]) for k in small_names])
    loss = loss_row[0, 0]

    grads, deltas, new_m, new_v = {}, {}, {}, {}
    for name in BIG:
        wv = pad_taps(W[name]) if name == "dw_w" else W[name]
        mv = pad_taps(M[name]) if name == "dw_w" else M[name]
        vv = pad_taps(V[name]) if name == "dw_w" else V[name]
        gg = g_big[name]
        cols = wv.shape[-1]
        rows_total = wv.size // cols
        tile_rows = min(rows_total, 256)
        four = _adamw(wv.reshape(rows_total, cols), gg.reshape(rows_total, cols),
                      mv.reshape(rows_total, cols), vv.reshape(rows_total, cols), tile_rows)
        cut = (lambda a: a.reshape(DEPTH, CWP, LANE)[:, :CW]) if name == "dw_w" else (lambda a: a.reshape(wv.shape))
        grads[name], deltas[name], new_m[name], new_v[name] = (cut(t) for t in four)
    for k, four in zip(small_names, results):
        grads[k], deltas[k], new_m[k], new_v[k] = (t.reshape(W[k].shape) for t in four)

    return (loss, grad_x[None], *[grads[n] for n in order], *[deltas[n] for n in order],
            *[new_m[n] for n in order], *[new_v[n] for n in order])
```

```python
import functools

import jax
import jax.numpy as jnp
from jax import lax
from jax.experimental import pallas as pl
from jax.experimental.pallas import tpu as pltpu

F32 = jnp.float32
BF16 = jnp.bfloat16

T = 2048
D = 1024
DIN = 3584
NCHIP = 4
SHW = DIN // NCHIP
AD = 512
CD = 512
DH = 64
CW = 31
CWP = 32
PLE = 256
DEPTH = 2
EPS = 1e-6
AQ = 256
HG = 4
GW = HG * DH
SR = HG * AQ
NG = AD // GW
LANE = 128
TM = 1024
TR = 256

ADAM_LR = 0.001
ADAM_B1 = 0.9
ADAM_B2 = 0.999
ADAM_EPS = 1e-08
ADAM_WD = 0.01
ADAM_STEP = 10

SMALL_PK = 16
LOSS_ROW = 15

VMEM_BIG = 56 * 1024 * 1024
MESH = pl.DeviceIdType.MESH


def _cp(sem=None, vmem=None):
    kw = {}
    if sem is not None:
        kw["dimension_semantics"] = sem
    if vmem is not None:
        kw["vmem_limit_bytes"] = vmem
    return pltpu.CompilerParams(**kw)


def _dot(a, b):
    return jnp.dot(a, b, preferred_element_type=F32)


def _dot_nt(a, b):
    return lax.dot_general(a, b, (((1,), (1,)), ((), ())), preferred_element_type=F32)


def _dot_tn(a, b):
    return lax.dot_general(a, b, (((0,), (0,)), ((), ())), preferred_element_type=F32)


def _dot2(x, m):
    hi = x.astype(BF16)
    lo = (x - hi.astype(F32)).astype(BF16)
    return _dot(hi, m) + _dot(lo, m)


def _sig(x):
    return 1.0 / (1.0 + jnp.exp(-x))


def _softplus(z):
    return jnp.maximum(z, 0.0) + jnp.log(1.0 + jnp.exp(-jnp.abs(z)))


def _rstd(x):
    return lax.rsqrt(jnp.mean(x * x, axis=-1, keepdims=True) + EPS)


def _rms_bwd(dy, x, r, g):
    dn = dy * g
    return r * dn - x * (r * r * r) * jnp.mean(dn * x, axis=-1, keepdims=True)


def _rms_inproj(h, g, w):
    tm = min(TM, T)

    def body(h_ref, g_ref, w_ref, u_ref, hn_ref, hn_s):
        @pl.when(pl.program_id(1) == 0)
        def _():
            hh = h_ref[...]
            hn = (hh * _rstd(hh) * g_ref[...]).astype(BF16)
            hn_s[...] = hn
            hn_ref[...] = hn
        u_ref[...] = _dot(hn_s[...], w_ref[0])

    return pl.pallas_call(
        body, name="rms_inproj", grid=(T // tm, NCHIP),
        in_specs=[pl.BlockSpec((tm, D), lambda i, k: (i, 0)),
                  pl.BlockSpec((1, D), lambda i, k: (0, 0)),
                  pl.BlockSpec((1, D, SHW), lambda i, k: (k, 0, 0))],
        out_specs=[pl.BlockSpec((tm, SHW), lambda i, k: (i, k)),
                   pl.BlockSpec((tm, D), lambda i, k: (i, 0))],
        out_shape=[jax.ShapeDtypeStruct((T, DIN), F32), jax.ShapeDtypeStruct((T, D), BF16)],
        scratch_shapes=[pltpu.VMEM((tm, D), BF16)],
        compiler_params=_cp(("arbitrary", "arbitrary"), VMEM_BIG),
    )(h, g, w)


def _attn_tiles():
    row = lax.broadcasted_iota(jnp.int32, (SR, AQ), 0) & (AQ - 1)
    col = lax.broadcasted_iota(jnp.int32, (SR, AQ), 1)
    tr = lax.broadcasted_iota(jnp.int32, (AQ, AQ), 0)
    tc = lax.broadcasted_iota(jnp.int32, (AQ, AQ), 1)
    lane_head = lax.broadcasted_iota(jnp.int32, (1, GW), 1) // DH
    return col < row, tr, tc, [lane_head == h for h in range(HG)]


def _stack_heads(t, heads):
    return jnp.concatenate([jnp.where(m, t, 0.0) for m in heads], axis=0)


def _unstack_heads(t, heads):
    out = t[:AQ]
    for h in range(1, HG):
        out = jnp.where(heads[h], t[h * AQ:(h + 1) * AQ], out)
    return out


def _tri_sum(x, tri):
    hi = x.astype(BF16)
    lo = (x - hi.astype(F32)).astype(BF16)
    both = _dot(jnp.concatenate([hi, lo], axis=0), tri)
    return both[:SR] + both[SR:]


def _scatter_copies(ps, gots, send_sems, recv_sems):
    x, y, c = _place()
    peers = [(1 - x, y), (x, 1 - y), (1 - x, 1 - y)]
    return [pltpu.make_async_remote_copy(
        src_ref=ps[a].at[2 * px + py], dst_ref=gots[a].at[r], send_sem=send_sems.at[3 * a + r],
        recv_sem=recv_sems.at[3 * a + r], device_id=(px, py, c), device_id_type=MESH)
        for a in range(len(ps)) for r, (px, py) in enumerate(peers)]


GATHER_SEMS = 7


def _gather_tree_copies(ins, outs, send_sems, recv_sems):
    x, y, c = _place()
    me, xn, yn, dg = 2 * x + y, 2 * (1 - x) + y, 2 * x + (1 - y), 2 * (1 - x) + (1 - y)
    to_x, to_y, sibling = (1 - x, y, c), (x, 1 - y, c), (x, y, 1 - c)
    direct, relayed, passed = [], [], []
    for a in range(len(ins)):
        half = ins[a].shape[0] // 2
        mine = pl.ds(c * half, half)
        first, second = pl.ds(c * half, half // 2), pl.ds(c * half + half // 2, half // 2)

        def copy(i, src, dst, to, k=GATHER_SEMS * a):
            return pltpu.make_async_remote_copy(src_ref=src, dst_ref=dst, send_sem=send_sems.at[k + i],
                                                recv_sem=recv_sems.at[k + i], device_id=to, device_id_type=MESH)

        own, slot = ins[a].at[mine], outs[a].at[me, mine]
        direct += [copy(0, own, slot, to_x), copy(1, own, slot, to_y)]
        relayed += [copy(2, outs[a].at[xn, first], outs[a].at[xn, first], to_y),
                    copy(3, outs[a].at[yn, second], outs[a].at[yn, second], to_x)]
        passed += [copy(4 + i, outs[a].at[j, mine], outs[a].at[j, mine], sibling) for i, j in enumerate((xn, yn, dg))]
    return direct, relayed, passed


def _pair_copies(ins, outs, send_sems, recv_sems):
    x, y, c = _place()
    copies = []
    for a in range(len(ins)):
        half = ins[a].shape[1] // 2
        copies.append(pltpu.make_async_remote_copy(
            src_ref=ins[a].at[:, pl.ds((1 - c) * half, half), :], dst_ref=outs[a], send_sem=send_sems.at[a],
            recv_sem=recv_sems.at[a], device_id=(x, y, 1 - c), device_id_type=MESH))
    return copies


def _host(body, grid, n_in, n_out, n_x, make_copies, mids=()):
    if not n_x:
        return body

    def hosting(*refs):
        a, b = n_in + n_x, n_in + 2 * n_x + n_out
        copies = make_copies(refs[n_in:a], refs[a + n_out:b], refs[-2], refs[-1])
        stages = copies if isinstance(copies, tuple) else (copies,)
        ids = [pl.program_id(d) for d in range(len(grid))]
        at = lambda step: functools.reduce(jnp.logical_and, [i == s for i, s in zip(ids, step)])

        @pl.when(at([0] * len(grid)))
        def _():
            for cp in stages[0]:
                cp.start()

        for before, after, step in zip(stages, stages[1:], mids):
            @pl.when(at(step))
            def _(before=before, after=after):
                for cp in before:
                    cp.wait_recv()
                for cp in after:
                    cp.start()

        body(*refs[:n_in], *refs[a:a + n_out], *refs[b:-2])

        @pl.when(at([g - 1 for g in grid]))
        def _():
            for cp in stages[-1]:
                cp.wait_recv()
            for stage in stages:
                for cp in stage:
                    cp.wait_send()

    return hosting


def _hosted_sems(n_x, per_array=3):
    n = per_array * n_x
    return [pltpu.SemaphoreType.DMA((n,)), pltpu.SemaphoreType.DMA((n,))] if n_x else []


RC = 256


def _chunk_causal(r):
    row = lax.broadcasted_iota(jnp.int32, (RC, AQ), 0) + (r * RC) % AQ
    return lax.broadcasted_iota(jnp.int32, (RC, AQ), 1) < row


def _attn_fwd(u, agw, shards=()):
    n = len(shards)
    grid = (T // AQ,)

    def body(q_ref, k_ref, v_ref, g_ref, ag_ref, o_ref, y_ref, tot_ref,
             kb_s, vb_s, qs_s, z_s, zs_s, lmb_s, suf_s, att_s, acc_s, run_s):
        qi = pl.program_id(0)

        @pl.when(qi == 0)
        def _():
            kb_s[...] = k_ref[...].astype(BF16)
            vb_s[...] = v_ref[...].astype(BF16)

        _, tr, tc, heads = _attn_tiles()
        upper = (tr > tc).astype(BF16)
        same_head = ((tr // DH) == (tc // DH)).astype(BF16)
        for g in range(NG):
            qs_s[g] = _stack_heads(q_ref[:, g * GW:(g + 1) * GW] * 0.125, heads).astype(BF16)
        acc_s[...] = jnp.zeros_like(acc_s)
        run_s[...] = jnp.zeros_like(run_s)

        def block(kb, masked):
            k0 = pl.multiple_of(kb * AQ, AQ)
            for g in range(NG):
                lanes = pl.ds(g * GW, GW)
                z_s[g] = _dot_nt(qs_s[g], kb_s[pl.ds(k0, AQ), lanes])
                for r in range(SR // RC):
                    rows = pl.ds(r * RC, RC)
                    z = z_s[g, rows, :]
                    zs = jnp.minimum(z, 0.0) - jnp.log(1.0 + jnp.exp(-jnp.abs(z)))
                    lm = zs - z
                    if masked:
                        lm = jnp.where(_chunk_causal(r), lm, 0.0)
                    run = run_s[g, rows, :]
                    zs_s[g, rows, :] = zs + run[:, 0:1]
                    hi = lm.astype(BF16)
                    lmb_s[g, rows, :] = hi
                    lmb_s[g, pl.ds(SR + r * RC, RC), :] = (lm - hi.astype(F32)).astype(BF16)
                    run_s[g, rows, :] = run + jnp.sum(lm, axis=1, keepdims=True)
                suf_s[g] = _dot(lmb_s[g], upper)
                for r in range(SR // RC):
                    rows = pl.ds(r * RC, RC)
                    att = jnp.exp(zs_s[g, rows, :] + suf_s[g, rows, :] + suf_s[g, pl.ds(SR + r * RC, RC), :])
                    if masked:
                        att = jnp.where(_chunk_causal(r), att, 0.0)
                    att_s[g, rows, :] = att.astype(BF16)
                acc_s[g] += _dot(att_s[g], vb_s[pl.ds(k0, AQ), lanes])

        block(qi, True)

        def step(i, c):
            block(qi - 1 - i, False)
            return c

        lax.fori_loop(0, qi, step, 0)
        gate = g_ref[...]
        agv = ag_ref[...]
        for g in range(NG):
            lanes = slice(g * GW, (g + 1) * GW)
            o = _unstack_heads(acc_s[g], heads)
            osq = o * o
            ms = _dot2(osq, same_head)
            gg = gate[:, lanes]
            o_ref[:, lanes] = o
            y_ref[:, lanes] = (o * lax.rsqrt(ms * (1.0 / DH) + EPS) * agv[:, lanes] * (gg * _sig(gg))).astype(BF16)
            tot_ref[g] = _unstack_heads(jnp.broadcast_to(run_s[g][:, 0:1], (SR, GW)), heads)

    tile = lambda dt, rows=SR: pltpu.VMEM((NG, rows, AQ), dt)
    scratch = [pltpu.VMEM((T, AD), BF16), pltpu.VMEM((T, AD), BF16), pltpu.VMEM((NG, SR, GW), BF16),
               tile(F32), tile(F32), tile(BF16, 2 * SR), tile(F32, 2 * SR), tile(BF16),
               pltpu.VMEM((NG, SR, GW), F32), pltpu.VMEM((NG, SR, LANE), F32)]
    col = lambda j: pl.BlockSpec((AQ, AD), lambda qi: (qi, j))
    res = pl.pallas_call(
        _host(body, grid, 5, 3, n, _gather_tree_copies, mids=((grid[0] * 5 // 8,), (grid[0] * 7 // 8,))), name="attn_fwd", grid=grid,
        in_specs=[col(0), pl.BlockSpec((T, AD), lambda qi: (0, 1)), pl.BlockSpec((T, AD), lambda qi: (0, 2)),
                  col(3), pl.BlockSpec((1, AD), lambda qi: (0, 0))] + [HBM_SPEC] * n,
        out_specs=[col(0), col(0), pl.BlockSpec((NG, AQ, GW), lambda qi: (0, qi, 0))] + [HBM_SPEC] * n,
        out_shape=[jax.ShapeDtypeStruct((T, AD), F32), jax.ShapeDtypeStruct((T, AD), BF16),
                   jax.ShapeDtypeStruct((NG, T, GW), F32)]
        + [jax.ShapeDtypeStruct((NCHIP,) + s.shape, s.dtype) for s in shards],
        scratch_shapes=scratch + _hosted_sems(n, GATHER_SEMS),
        compiler_params=_cp(("arbitrary",), VMEM_BIG),
    )(u, u, u, u, agw, *shards)
    return res[0], res[1], res[2], list(res[3:])


def _glu_conv(u, dw, db):
    tr = 256

    def body(cv_ref, cg_ref, w_ref, b_ref, c1_ref, pad_s):
        pad_s[pl.ds(0, CWP), :] = jnp.zeros((CWP, LANE), F32)
        pad_s[pl.ds(CWP, T), :] = cv_ref[...] * _sig(cg_ref[...])
        wv = w_ref[0]
        bias = b_ref[...]

        def tile(i, carry):
            r0 = pl.multiple_of(i * tr, tr)
            acc = jnp.zeros((tr, LANE), F32) + bias
            for w in range(CW):
                acc = acc + pad_s[pl.ds(r0 + (CWP - CW + 1) + w, tr), :] * wv[w:w + 1, :]
            c1_ref[pl.ds(r0, tr), :] = acc
            return carry

        lax.fori_loop(0, T // tr, tile, 0)

    return pl.pallas_call(
        body, name="glu_conv", grid=(CD // LANE,),
        in_specs=[pl.BlockSpec((T, LANE), lambda cb: (0, 16 + cb)),
                  pl.BlockSpec((T, LANE), lambda cb: (0, 20 + cb)),
                  pl.BlockSpec((1, CWP, LANE), lambda cb: (cb, 0, 0)),
                  pl.BlockSpec((1, LANE), lambda cb: (0, cb))],
        out_specs=pl.BlockSpec((T, LANE), lambda cb: (0, cb)),
        out_shape=jax.ShapeDtypeStruct((T, CD), F32),
        scratch_shapes=[pltpu.VMEM((T + CWP, LANE), F32)],
        compiler_params=_cp(("arbitrary",)),
    )(u, u, dw, db)


def _ln_silu(c1, lg, lb):
    mu = jnp.mean(c1, axis=-1, keepdims=True)
    xc = c1 - mu
    rs = lax.rsqrt(jnp.mean(xc * xc, axis=-1, keepdims=True) + EPS)
    xh = xc * rs
    ln = xh * lg + lb
    s = _sig(ln)
    return xh, rs, ln, s


def _layer_tail(c1, u, ya, h, p, lg, lb, wpw, cg, wout, pg, wgate, wple, head=None):
    tm = min(TR, T)

    def body(c1_ref, gc_ref, ya_ref, h_ref, p_ref, lg_ref, lb_ref, wpw_ref, cg_ref, wout_ref,
             pg_ref, wgate_ref, wple_ref, *rest):
        c3_ref, yc_ref, h1_ref, gate_ref, pe_ref, h2_ref = rest[-6 - 2 * bool(head):][:6]
        _, _, ln, s = _ln_silu(c1_ref[...], lg_ref[...], lb_ref[...])
        c2 = (ln * s).astype(BF16)
        c3 = _dot(c2, wpw_ref[...])
        gc = gc_ref[...]
        yc = (c3 * _rstd(c3) * cg_ref[...] * (gc * _sig(gc))).astype(BF16)
        c3_ref[...] = c3
        yc_ref[...] = yc
        y = _dot(ya_ref[...], wout_ref[pl.ds(0, AD), :]) + _dot(yc, wout_ref[pl.ds(AD, CD), :])
        h1 = h_ref[...] + y
        hn2 = (h1 * _rstd(h1) * pg_ref[...]).astype(BF16)
        gate = _sig(_dot(hn2, wgate_ref[...]))
        pb = p_ref[...].astype(BF16)
        pe = jnp.concatenate([_dot(pb, wple_ref[k]) for k in range(NCHIP)], axis=1)
        h1_ref[...] = h1
        gate_ref[...] = gate.astype(BF16)
        pe_ref[...] = pe.astype(BF16)
        h2 = h1 + pe * gate
        if not head:
            h2_ref[...] = h2
            return
        t_ref, fg_ref, loss_ref, dfg_ref = rest[0], rest[1], rest[-2], rest[-1]

        @pl.when(pl.program_id(0) == 0)
        def _():
            loss_ref[...] = jnp.zeros_like(loss_ref)
            dfg_ref[...] = jnp.zeros_like(dfg_ref)
        fg = fg_ref[...]
        r = _rstd(h2)
        e = h2 * r * fg - t_ref[...]
        loss_ref[...] += 0.5 * jnp.sum(jnp.mean(e * e, axis=-1, keepdims=True))
        dy = e * (1.0 / D)
        dfg_ref[...] += jnp.sum(dy * h2 * r, axis=0, keepdims=True)
        h2_ref[...] = _rms_bwd(dy, h2, r, fg)

    row = lambda w: pl.BlockSpec((tm, w), lambda i: (i, 0))
    full = lambda *s: pl.BlockSpec(s, lambda i: (0,) * len(s), pipeline_mode=pl.Buffered(1))
    extra = bool(head)
    return pl.pallas_call(
        body, name="layer_tail", grid=(T // tm,),
        in_specs=[row(CD), pl.BlockSpec((tm, CD), lambda i: (i, 6)), row(AD), row(D), row(PLE),
                  full(1, CD), full(1, CD), full(CD, CD), full(1, CD), full(D, D),
                  full(1, D), full(D, D), full(NCHIP, PLE, PLE)] + [row(D), full(1, D)] * extra,
        out_specs=[row(CD), row(CD), row(D), row(D), row(D), row(D)] + [full(8, LANE), full(1, D)] * extra,
        out_shape=[jax.ShapeDtypeStruct((T, CD), F32), jax.ShapeDtypeStruct((T, CD), BF16),
                   jax.ShapeDtypeStruct((T, D), F32), jax.ShapeDtypeStruct((T, D), BF16),
                   jax.ShapeDtypeStruct((T, D), BF16), jax.ShapeDtypeStruct((T, D), F32)]
        + [jax.ShapeDtypeStruct((8, LANE), F32), jax.ShapeDtypeStruct((1, D), F32)] * extra,
        compiler_params=_cp(("arbitrary",), VMEM_BIG),
    )(c1, u, ya, h, p, lg, lb, wpw, cg, wout, pg, wgate, wple, *(head or ()))


def _ple_out_bwd(dh2, h1, gate, pe, p, ya, yc, pg, wgate, wout):
    tm = min(TR, T)

    def body(dh2_ref, h1_ref, gate_ref, pe_ref, p_ref, ya_ref, yc_ref, pg_ref, wgate_ref, wout_ref,
             dh1_ref, dy_ref, dwg_ref, dwp_ref, dwo_ref, dpg_ref):
        @pl.when(pl.program_id(0) == 0)
        def _():
            dwg_ref[...] = jnp.zeros_like(dwg_ref)
            dwp_ref[...] = jnp.zeros_like(dwp_ref)
            dwo_ref[...] = jnp.zeros_like(dwo_ref)
            dpg_ref[...] = jnp.zeros_like(dpg_ref)
        dh2 = dh2_ref[...]
        h1 = h1_ref[...]
        gate = gate_ref[...].astype(F32)
        pg = pg_ref[...]
        dpe = (dh2 * gate).astype(BF16)
        dgp = (dh2 * pe_ref[...].astype(F32) * gate * (1.0 - gate)).astype(BF16)
        r = _rstd(h1)
        hn = h1 * r
        dwg_ref[...] += _dot_tn((hn * pg).astype(BF16), dgp)
        dhn2 = _dot_nt(dgp, wgate_ref[...])
        dpg_ref[...] += jnp.sum(dhn2 * hn, axis=0, keepdims=True)
        dh1 = dh2 + _rms_bwd(dhn2, h1, r, pg)
        pb = p_ref[...].astype(BF16)
        for k in range(NCHIP):
            dwp_ref[k] += _dot_tn(pb, dpe[:, k * PLE:(k + 1) * PLE])
        dh1b = dh1.astype(BF16)
        dy_ref[...] = _dot_nt(dh1b, wout_ref[...])
        dwo_ref[pl.ds(0, AD), :] += _dot_tn(ya_ref[...], dh1b)
        dwo_ref[pl.ds(AD, CD), :] += _dot_tn(yc_ref[...], dh1b)
        dh1_ref[...] = dh1

    row = lambda w: pl.BlockSpec((tm, w), lambda i: (i, 0))
    full = lambda *s: pl.BlockSpec(s, lambda i: (0,) * len(s), pipeline_mode=pl.Buffered(1))
    return pl.pallas_call(
        body, name="ple_out_bwd", grid=(T // tm,),
        in_specs=[row(D), row(D), row(D), row(D), row(PLE), row(AD), row(CD),
                  full(1, D), full(D, D), full(D, D)],
        out_specs=[row(D), row(D), full(D, D), full(NCHIP, PLE, PLE), full(D, D), full(1, D)],
        out_shape=[jax.ShapeDtypeStruct((T, D), F32), jax.ShapeDtypeStruct((T, D), F32),
                   jax.ShapeDtypeStruct((D, D), F32), jax.ShapeDtypeStruct((NCHIP, PLE, PLE), F32),
                   jax.ShapeDtypeStruct((D, D), F32), jax.ShapeDtypeStruct((1, D), F32)],
        compiler_params=_cp(("arbitrary",), VMEM_BIG),
    )(dh2, h1, gate, pe, p, ya, yc, pg, wgate, wout)


def _branch_bwd(dy, o, u, c1, c3, ag, lg, lb, wpw, cg, seg):
    tm = min(TR, T)

    def body(dya_ref, dyc_ref, o_ref, ga_ref, gc_ref, c1_ref, c3_ref, ag_ref, lg_ref, lb_ref, wpw_ref,
             cg_ref, seg_ref, do_ref, dga_ref, dgc_ref, dc1_ref, dwpw_ref, dag_ref, dcg_ref, dlg_ref, dlb_ref):
        @pl.when(pl.program_id(0) == 0)
        def _():
            for r_ in (dwpw_ref, dag_ref, dcg_ref, dlg_ref, dlb_ref):
                r_[...] = jnp.zeros_like(r_)
        dya = dya_ref[...]
        o = o_ref[...]
        ga = ga_ref[...]
        ag_v = ag_ref[...]
        seg_m = seg_ref[...]
        r = lax.rsqrt(_dot2(o * o, seg_m) * (1.0 / DH) + EPS)
        onr = o * r
        sg = _sig(ga)
        dga_ref[...] = (dya * (onr * ag_v) * (sg * (1.0 + ga * (1.0 - sg)))).astype(BF16)
        don = dya * (ga * sg)
        dag_ref[...] += jnp.sum(don * onr, axis=0, keepdims=True)
        dn = don * ag_v
        do_ref[...] = r * dn - o * (r * r * r) * (_dot2(dn * o, seg_m) * (1.0 / DH))
        dyc = dyc_ref[...]
        c3 = c3_ref[...]
        gc = gc_ref[...]
        cg_v = cg_ref[...]
        r3 = _rstd(c3)
        cn = c3 * r3
        sc = _sig(gc)
        dgc_ref[...] = (dyc * (cn * cg_v) * (sc * (1.0 + gc * (1.0 - sc)))).astype(BF16)
        dcn = dyc * (gc * sc)
        dcg_ref[...] += jnp.sum(dcn * cn, axis=0, keepdims=True)
        dc3 = _rms_bwd(dcn, c3, r3, cg_v).astype(BF16)
        lg_v = lg_ref[...]
        xh, rs, ln, s = _ln_silu(c1_ref[...], lg_v, lb_ref[...])
        c2 = (ln * s).astype(BF16)
        dwpw_ref[...] += _dot_tn(c2, dc3)
        dc2 = _dot_nt(dc3, wpw_ref[...])
        dln = dc2 * (s * (1.0 + ln * (1.0 - s)))
        dlb_ref[...] += jnp.sum(dln, axis=0, keepdims=True)
        dlg_ref[...] += jnp.sum(dln * xh, axis=0, keepdims=True)
        dxh = dln * lg_v
        dc1_ref[...] = rs * (dxh - jnp.mean(dxh, axis=-1, keepdims=True)
                             - xh * jnp.mean(dxh * xh, axis=-1, keepdims=True))

    half = lambda j: pl.BlockSpec((tm, 512), lambda i: (i, j))
    full = lambda *s: pl.BlockSpec(s, lambda i: (0,) * len(s), pipeline_mode=pl.Buffered(1))
    vec = jax.ShapeDtypeStruct((1, 512), F32)
    act = jax.ShapeDtypeStruct((T, 512), F32)
    return pl.pallas_call(
        body, name="branch_bwd", grid=(T // tm,),
        in_specs=[half(0), half(1), half(0), half(3), half(6), half(0), half(0),
                  full(1, AD), full(1, CD), full(1, CD), full(CD, CD), full(1, CD), full(AD, AD)],
        out_specs=[half(0), half(0), half(0), half(0), full(CD, CD), full(1, 512), full(1, 512),
                   full(1, 512), full(1, 512)],
        out_shape=[act, jax.ShapeDtypeStruct((T, 512), BF16), jax.ShapeDtypeStruct((T, 512), BF16), act,
                   jax.ShapeDtypeStruct((CD, CD), F32), vec, vec, vec, vec],
        compiler_params=_cp(("arbitrary",), VMEM_BIG),
    )(dy, dy, o, u, u, c1, c3, ag, lg, lb, wpw, cg, seg)


def _conv_bwd(dc1, u, dw, grads=()):
    tr = 64
    n_x = len(grads)
    grid = (CD // LANE,)
    off = CWP - CW + 1

    def body(d_ref, cv_ref, cg_ref, w_ref, dcv_ref, dcg_ref, ddw_ref, ddb_ref, padc_s, padd_s, acc_s):
        cv = cv_ref[...]
        sg = _sig(cg_ref[...])
        padc_s[pl.ds(0, CWP), :] = jnp.zeros((CWP, LANE), F32)
        padc_s[pl.ds(CWP, T), :] = cv * sg
        padd_s[pl.ds(0, T), :] = d_ref[...]
        padd_s[pl.ds(T, CWP), :] = jnp.zeros((CWP, LANE), F32)
        acc_s[...] = jnp.zeros_like(acc_s)
        wv = w_ref[0]

        def tile(i, carry):
            r0 = pl.multiple_of(i * tr, tr)
            dt = padd_s[pl.ds(r0, tr), :]
            dc0 = jnp.zeros((tr, LANE), F32)
            for w in range(CW):
                dc0 = dc0 + padd_s[pl.ds(r0 + (CW - 1) - w, tr), :] * wv[w:w + 1, :]
                prod = dt * padc_s[pl.ds(r0 + off + w, tr), :]
                acc_s[w] += jnp.sum(prod.reshape(tr // 8, 8, LANE), axis=0)
            cvt = cv_ref[pl.ds(r0, tr), :]
            sgt = _sig(cg_ref[pl.ds(r0, tr), :])
            dcv_ref[pl.ds(r0, tr), :] = (dc0 * sgt).astype(BF16)
            dcg_ref[pl.ds(r0, tr), :] = (dc0 * cvt * sgt * (1.0 - sgt)).astype(BF16)
            return carry

        lax.fori_loop(0, T // tr, tile, 0)
        ddw_ref[0] = jnp.sum(acc_s[...], axis=1)
        ddb_ref[...] = jnp.sum(d_ref[...], axis=0, keepdims=True)

    col = lambda j: pl.BlockSpec((T, LANE), lambda cb: (0, j + cb))
    res = pl.pallas_call(
        _host(body, grid, 4, 4, n_x, _pair_copies), name="conv_bwd", grid=grid,
        in_specs=[col(0), col(16), col(20), pl.BlockSpec((1, CWP, LANE), lambda cb: (cb, 0, 0))] + [HBM_SPEC] * n_x,
        out_specs=[col(0), col(0), pl.BlockSpec((1, CWP, LANE), lambda cb: (cb, 0, 0)),
                   pl.BlockSpec((1, LANE), lambda cb: (0, cb))] + [HBM_SPEC] * n_x,
        out_shape=[jax.ShapeDtypeStruct((T, CD), BF16), jax.ShapeDtypeStruct((T, CD), BF16),
                   jax.ShapeDtypeStruct((NCHIP, CWP, LANE), F32), jax.ShapeDtypeStruct((1, CD), F32)]
        + [jax.ShapeDtypeStruct((NCHIP, g.shape[1] // 2, g.shape[2]), F32) for g in grads],
        scratch_shapes=[pltpu.VMEM((T + CWP, LANE), F32), pltpu.VMEM((T + CWP, LANE), F32),
                        pltpu.VMEM((CWP, 8, LANE), F32)] + _hosted_sems(n_x),
        compiler_params=_cp(("arbitrary",)),
    )(dc1, u, u, dw, *grads)
    return res[0], res[1], res[2], res[3], list(res[4:])


def _attn_bwd(u, do, tot, partials=()):
    n_x = len(partials)
    grid = (T // AQ,)

    def body(q_ref, k_ref, v_ref, do_ref, tot_ref, dq_ref, dk_ref, dv_ref, kb_s, vb_s, dk_s, dv_s):
        qi = pl.program_id(0)

        @pl.when(qi == 0)
        def _():
            kb_s[...] = k_ref[...].astype(BF16)
            vb_s[...] = v_ref[...].astype(BF16)
            dk_s[...] = jnp.zeros_like(dk_s)
            dv_s[...] = jnp.zeros_like(dv_s)

        causal, tr, tc, heads = _attn_tiles()
        upper = (tr > tc).astype(BF16)
        lower = (tr < tc).astype(BF16)
        qs, qus, dos, tots = [], [], [], []
        for g in range(NG):
            lanes = slice(g * GW, (g + 1) * GW)
            q = q_ref[:, lanes]
            qs.append(_stack_heads(q * 0.125, heads).astype(BF16))
            qus.append(_stack_heads(q, heads).astype(BF16))
            dos.append(_stack_heads(do_ref[:, lanes], heads).astype(BF16))
            totv = tot_ref[g]
            tots.append(jnp.concatenate([totv[:, h * DH:h * DH + 1] for h in range(HG)], axis=0))

        def block(kb, carry, masked):
            k0 = pl.multiple_of(kb * AQ, AQ)
            out = []
            for g in range(NG):
                lanes = pl.ds(g * GW, GW)
                lm_left, dl_left, dq = carry[g]
                kk = kb_s[pl.ds(k0, AQ), lanes]
                vv = vb_s[pl.ds(k0, AQ), lanes]
                z = _dot_nt(qs[g], kk)
                sp = _softplus(z)
                lm = jnp.where(causal, -sp, 0.0) if masked else -sp
                lm_incl = lm_left + jnp.sum(lm, axis=1, keepdims=True)
                att = jnp.exp((z - sp) + _tri_sum(lm, upper) + (tots[g] - lm_incl))
                if masked:
                    att = jnp.where(causal, att, 0.0)
                dl = att * _dot_nt(dos[g], vv)
                dv_s[pl.ds(k0, AQ), lanes] += _dot_tn(att.astype(BF16), dos[g])
                prefix = dl_left + _tri_sum(dl, lower)
                beta = jnp.exp(z - sp)
                dz = (1.0 - beta) * dl - beta * prefix
                if masked:
                    dz = jnp.where(causal, dz, 0.0)
                dzs = (dz * 0.125).astype(BF16)
                dk_s[pl.ds(k0, AQ), lanes] += _dot_tn(dzs, qus[g])
                out.append((lm_incl, dl_left + jnp.sum(dl, axis=1, keepdims=True), dq + _dot(dzs, kk)))
            return tuple(out)

        zero = jnp.zeros((SR, 1), F32)
        init = tuple((zero, zero, jnp.zeros((SR, GW), F32)) for _ in range(NG))
        carry = lax.fori_loop(0, qi, lambda kb, c: block(kb, c, False), init)
        carry = block(qi, carry, True)
        for g in range(NG):
            dq_ref[:, g * GW:(g + 1) * GW] = _unstack_heads(carry[g][2], heads).astype(BF16)

        @pl.when(qi == grid[0] - 1)
        def _():
            dk_ref[...] = dk_s[...].astype(BF16)
            dv_ref[...] = dv_s[...].astype(BF16)

    col = lambda j: pl.BlockSpec((AQ, AD), lambda qi: (qi, j))
    whole = lambda j: pl.BlockSpec((T, AD), lambda qi: (0, j), pipeline_mode=pl.Buffered(1))
    res = pl.pallas_call(
        _host(body, grid, 5, 3, n_x, _scatter_copies), name="attn_bwd", grid=grid,
        in_specs=[col(0), whole(1), whole(2), col(0), pl.BlockSpec((NG, AQ, GW), lambda qi: (0, qi, 0))]
        + [HBM_SPEC] * n_x,
        out_specs=[col(0), whole(0), whole(0)] + [HBM_SPEC] * n_x,
        out_shape=[jax.ShapeDtypeStruct((T, AD), BF16)] * 3
        + [jax.ShapeDtypeStruct((NCHIP - 1,) + a.shape[1:], a.dtype) for a in partials],
        scratch_shapes=[pltpu.VMEM((T, AD), BF16), pltpu.VMEM((T, AD), BF16), pltpu.VMEM((T, AD), F32),
                        pltpu.VMEM((T, AD), F32)] + _hosted_sems(n_x),
        compiler_params=_cp(("arbitrary",), VMEM_BIG),
    )(u, u, u, do, tot, *partials)
    return res[0], res[1], res[2], list(res[3:])


def _inproj_dw(hn, du):
    tm = min(TM, T)

    def body(hn_ref, du_ref, dw_ref):
        @pl.when(pl.program_id(1) == 0)
        def _():
            dw_ref[...] = jnp.zeros_like(dw_ref)
        dw_ref[0] += _dot_tn(hn_ref[...], du_ref[...])

    return pl.pallas_call(
        body, name="inproj_dw", grid=(NCHIP, T // tm),
        in_specs=[pl.BlockSpec((tm, D), lambda k, i: (i, 0)), pl.BlockSpec((tm, SHW), lambda k, i: (i, k))],
        out_specs=pl.BlockSpec((1, D, SHW), lambda k, i: (k, 0, 0)),
        out_shape=jax.ShapeDtypeStruct((NCHIP, D, SHW), F32),
        compiler_params=_cp(("arbitrary", "arbitrary"), VMEM_BIG),
    )(hn, du)


def _inproj_dx(du, w, h, g, dres, partials=(), grads=(), after=None):
    tm = min(TM, T)
    sent = list(partials) + list(grads)
    n_x = len(sent)
    assert after is None or not sent
    grid = (T // tm, NCHIP)
    if grads:
        landing = [jax.ShapeDtypeStruct((NCHIP, a.shape[1] // 2, a.shape[2]), F32) for a in grads]
    else:
        landing = [jax.ShapeDtypeStruct((NCHIP - 1,) + a.shape[1:], a.dtype) for a in partials]

    def body(du_ref, w_ref, h_ref, g_ref, dres_ref, dh_ref, dg_ref, acc_s):
        i, k = pl.program_id(0), pl.program_id(1)

        @pl.when(jnp.logical_and(i == 0, k == 0))
        def _():
            dg_ref[...] = jnp.zeros_like(dg_ref)

        @pl.when(k == 0)
        def _():
            acc_s[...] = _dot_nt(du_ref[...], w_ref[0])

        @pl.when(k > 0)
        def _():
            acc_s[...] += _dot_nt(du_ref[...], w_ref[0])

        @pl.when(k == NCHIP - 1)
        def _():
            hh = h_ref[...]
            r = _rstd(hh)
            dhn = acc_s[...]
            dg_ref[...] += jnp.sum(dhn * hh * r, axis=0, keepdims=True)
            dh_ref[...] = dres_ref[...] + _rms_bwd(dhn, hh, r, g_ref[...])

    behind = [] if after is None else [after]
    hosting = _host(body, grid, 5, 2, n_x, _pair_copies if grads else _scatter_copies)
    res = pl.pallas_call(
        (lambda *refs: body(*refs[:5], *refs[6:])) if behind else hosting, name="inproj_dx", grid=grid,
        in_specs=[pl.BlockSpec((tm, SHW), lambda i, k: (i, k)),
                  pl.BlockSpec((1, D, SHW), lambda i, k: (k, 0, 0)),
                  pl.BlockSpec((tm, D), lambda i, k: (i, 0)),
                  pl.BlockSpec((1, D), lambda i, k: (0, 0)),
                  pl.BlockSpec((tm, D), lambda i, k: (i, 0))] + [HBM_SPEC] * n_x
        + [pl.BlockSpec(memory_space=pl.ANY)] * len(behind),
        out_specs=[pl.BlockSpec((tm, D), lambda i, k: (i, 0)), pl.BlockSpec((1, D), lambda i, k: (0, 0))]
        + [HBM_SPEC] * n_x,
        out_shape=[jax.ShapeDtypeStruct((T, D), F32), jax.ShapeDtypeStruct((1, D), F32)] + landing,
        scratch_shapes=[pltpu.VMEM((tm, D), F32)] + _hosted_sems(n_x),
        compiler_params=_cp(("arbitrary", "arbitrary"), VMEM_BIG),
    )(du, w, h, g, dres, *sent, *behind)
    return res[0], res[1], list(res[2:])


def _sum_pair(core, grads, gots):
    n = len(grads)

    def body(c_ref, *refs):
        for a in range(n):
            refs[2 * n + a][...] = (refs[a][...] + refs[n + a][...]).astype(BF16)

    mine = [pl.BlockSpec((1,) + s.shape[1:], lambda k, c: (k, c[0], 0)) for s in gots]
    same = [pl.BlockSpec((1,) + s.shape[1:], lambda k, c: (k, 0, 0)) for s in gots]
    return pl.pallas_call(
        body, name="sum_pair",
        grid_spec=pltpu.PrefetchScalarGridSpec(
            num_scalar_prefetch=1, grid=(NCHIP,), in_specs=mine + same, out_specs=same),
        out_shape=[jax.ShapeDtypeStruct(s.shape, BF16) for s in gots],
        compiler_params=_cp(("arbitrary",), VMEM_BIG),
    )(core, *grads, *gots)


def _sum_chips_share(chip, partials, gots):
    flat_p = [p for layer in partials for p in layer]
    flat_g = [g for layer in gots for g in layer]
    n, per_layer = len(flat_p), len(partials[0])

    def body(c_ref, *refs):
        full, sums = refs[2 * n:2 * n + per_layer], refs[2 * n + per_layer:3 * n + per_layer]
        send_sems, recv_sems, local_sems = refs[3 * n + per_layer:]
        x, y, c = _place()
        copies = []
        for i in range(n):
            acc = refs[i][0].astype(F32)
            for j in range(NCHIP - 1):
                acc = acc + refs[n + i][j].astype(F32)
            sums[i][...] = acc
            half = flat_p[i].shape[1]
            rows = full[i % per_layer].at[i // per_layer, pl.ds(c * half, half)]
            copies.append(pltpu.make_async_copy(sums[i], rows, local_sems.at[i]))
            copies.append(pltpu.make_async_remote_copy(
                src_ref=sums[i], dst_ref=rows, send_sem=send_sems.at[i], recv_sem=recv_sems.at[i],
                device_id=(x, y, 1 - c), device_id_type=MESH))
        for cp in copies:
            cp.start()
        for cp in copies:
            cp.wait()

    return pl.pallas_call(
        body, name="sum_chips_share",
        grid_spec=pltpu.PrefetchScalarGridSpec(
            num_scalar_prefetch=1, grid=(1,),
            in_specs=[pl.BlockSpec((1,) + s.shape[1:], lambda i, c: (c[0], 0, 0)) for s in flat_p]
            + [pl.BlockSpec(s.shape, lambda i, c: (0, 0, 0)) for s in flat_g],
            out_specs=[HBM_SPEC] * per_layer,
            scratch_shapes=[pltpu.VMEM(s.shape[1:], F32) for s in flat_p]
            + [pltpu.SemaphoreType.DMA((n,)), pltpu.SemaphoreType.DMA((n,)), pltpu.SemaphoreType.DMA((n,))]),
        out_shape=[jax.ShapeDtypeStruct((len(partials), 2 * s.shape[1], s.shape[2]), F32) for s in partials[0]],
        compiler_params=_cp(("arbitrary",), VMEM_BIG),
    )(chip, *flat_p, *flat_g)


def _adam_math(w, g, m, v):
    nm = ADAM_B1 * m + (1.0 - ADAM_B1) * g
    nv = ADAM_B2 * v + (1.0 - ADAM_B2) * (g * g)
    m_hat = nm / (1.0 - ADAM_B1 ** ADAM_STEP)
    v_hat = nv / (1.0 - ADAM_B2 ** ADAM_STEP)
    return -ADAM_LR * (m_hat / (jnp.sqrt(v_hat) + ADAM_EPS) + ADAM_WD * w), nm, nv


def _adamw(w, g, m, v, rows):
    R, C = w.shape

    def body(w_ref, g_ref, m_ref, v_ref, g_out, d_ref, nm_ref, nv_ref):
        g = g_ref[...]
        g_out[...] = g
        d_ref[...], nm_ref[...], nv_ref[...] = _adam_math(w_ref[...], g, m_ref[...], v_ref[...])

    spec = pl.BlockSpec((rows, C), lambda i: (i, 0))
    sh = jax.ShapeDtypeStruct((R, C), F32)
    return pl.pallas_call(
        body, name="adamw", grid=(R // rows,), in_specs=[spec] * 4, out_specs=[spec] * 4,
        out_shape=[sh, sh, sh, sh], compiler_params=_cp(("arbitrary",)),
    )(w, g, m, v)


def _adamw_taps(w, g, m, v):
    def body(w_ref, g_ref, m_ref, v_ref, g_out, d_ref, nm_ref, nv_ref):
        g = g_ref[:, pl.ds(0, CW), :]
        g_out[...] = g
        d_ref[...], nm_ref[...], nv_ref[...] = _adam_math(w_ref[...], g, m_ref[...], v_ref[...])

    vmem = pl.BlockSpec(memory_space=pltpu.VMEM)
    return pl.pallas_call(
        body, name="adamw_taps", in_specs=[vmem] * 4, out_specs=[vmem] * 4,
        out_shape=[jax.ShapeDtypeStruct(w.shape, F32)] * 4,
    )(w, g, m, v)


def _small_adamw(tot, ws, ms, vs):
    n = len(ws)

    def body(*refs):
        tot_ref = refs[0]
        w_refs, m_refs, v_refs = refs[1:1 + n], refs[1 + n:1 + 2 * n], refs[1 + 2 * n:1 + 3 * n]
        outs = refs[1 + 3 * n:]
        for i in range(n):
            rows, width = ws[i].shape
            g = tot_ref[pl.ds(SMALL_ROW[i], rows), pl.ds(0, width)]
            outs[4 * i][...] = g
            outs[4 * i + 1][...], outs[4 * i + 2][...], outs[4 * i + 3][...] = _adam_math(
                w_refs[i][...], g, m_refs[i][...], v_refs[i][...])
        outs[4 * n][...] = tot_ref[pl.ds(LOSS_ROW, 1), pl.ds(0, LANE)]

    vmem = pl.BlockSpec(memory_space=pltpu.VMEM)
    res = pl.pallas_call(
        body, name="small_adamw", in_specs=[vmem] * (1 + 3 * n), out_specs=[vmem] * (4 * n + 1),
        out_shape=[jax.ShapeDtypeStruct(w.shape, F32) for w in ws for _ in range(4)]
        + [jax.ShapeDtypeStruct((1, LANE), F32)],
    )(tot, *ws, *ms, *vs)
    return [res[4 * i:4 * i + 4] for i in range(n)], res[4 * n]


HBM_SPEC = pl.BlockSpec(memory_space=pltpu.HBM)


def _place():
    return lax.axis_index("x"), lax.axis_index("y"), lax.axis_index("c")


def _all_gather_split(shard):
    def body(in_ref, out_ref, send_sems, recv_sems):
        direct, relayed, passed = _gather_tree_copies([in_ref], [out_ref], send_sems, recv_sems)
        for cp in direct:
            cp.start()
        for i in range(2):
            direct[i].wait_recv()
            relayed[i].start()
            passed[i].start()
        for cp in relayed:
            cp.wait_recv()
        passed[2].start()
        for cp in passed:
            cp.wait_recv()
        for cp in direct + relayed + passed:
            cp.wait_send()

    return pl.pallas_call(
        body, name="all_gather_split", in_specs=[HBM_SPEC], out_specs=HBM_SPEC,
        out_shape=jax.ShapeDtypeStruct((NCHIP,) + shard.shape, shard.dtype),
        scratch_shapes=_hosted_sems(1, GATHER_SEMS),
    )(shard)


SEM_SPEC = pl.BlockSpec(memory_space=pltpu.SEMAPHORE)
ORDERED_EFFECT = pltpu.CompilerParams(has_side_effects=pltpu.SideEffectType.DATAFLOW_SIDE_EFFECTING)


def _scatter_start(partial):
    land = pltpu.with_memory_space_constraint(
        lax.empty((NCHIP - 1,) + partial.shape[1:], partial.dtype), pltpu.HBM)

    def body(p_ref, land_ref, send_sems, recv_sems, p_thru, land_thru, token):
        for cp in _scatter_copies([p_ref], [land_ref], send_sems, recv_sems):
            cp.start()
        token[...] = jnp.zeros_like(token)

    return pl.pallas_call(
        body, name="scatter_start",
        out_shape=(pltpu.SemaphoreType.DMA((NCHIP - 1,)), pltpu.SemaphoreType.DMA((NCHIP - 1,)),
                   pltpu.HBM(partial.shape, partial.dtype), pltpu.HBM(land.shape, land.dtype),
                   jax.ShapeDtypeStruct((8, LANE), F32)),
        in_specs=(HBM_SPEC, HBM_SPEC),
        out_specs=(SEM_SPEC, SEM_SPEC, HBM_SPEC, HBM_SPEC, pl.BlockSpec(memory_space=pltpu.VMEM)),
        input_output_aliases={0: 2, 1: 3}, compiler_params=ORDERED_EFFECT,
    )(pltpu.with_memory_space_constraint(partial, pltpu.HBM), land)


def _scatter_wait(send_sems, recv_sems, p_thru, land_thru, after):
    def body(p_ref, land_ref, send_sems, recv_sems, after_ref, p_dead, got_ref):
        for cp in _scatter_copies([p_ref], [land_ref], send_sems, recv_sems):
            cp.wait_send()
            cp.wait_recv()

    return pl.pallas_call(
        body, name="scatter_wait",
        out_shape=(pltpu.HBM(p_thru.shape, p_thru.dtype), pltpu.HBM(land_thru.shape, land_thru.dtype)),
        in_specs=(HBM_SPEC, HBM_SPEC, SEM_SPEC, SEM_SPEC, pl.BlockSpec(memory_space=pl.ANY)),
        out_specs=(HBM_SPEC, HBM_SPEC), input_output_aliases={0: 0, 1: 1}, compiler_params=ORDERED_EFFECT,
    )(p_thru, land_thru, send_sems, recv_sems, after)


def _pair_exchange(grads):
    n = len(grads)

    def body(*refs):
        copies = _pair_copies(refs[:n], refs[n:2 * n], refs[2 * n], refs[2 * n + 1])
        for cp in copies:
            cp.start()
        for cp in copies:
            cp.wait()

    return pl.pallas_call(
        body, name="pair_exchange", in_specs=[HBM_SPEC] * n, out_specs=[HBM_SPEC] * n,
        out_shape=[jax.ShapeDtypeStruct((NCHIP, g.shape[1] // 2, g.shape[2]), F32) for g in grads],
        scratch_shapes=[pltpu.SemaphoreType.DMA((n,)), pltpu.SemaphoreType.DMA((n,))],
    )(*grads)


def _small_allreduce(rows, loss_blk, per_head=()):
    n = len(rows)

    def body(*refs):
        loss_ref, o_ref, pk, slots, send_sems, recv_sems = refs[n:]
        pk[...] = jnp.zeros_like(pk)
        for i in range(n):
            row = refs[i][...]
            if i in per_head:
                row = functools.reduce(jnp.add, [row[:, h * DH:(h + 1) * DH] for h in range(AD // DH)])
            pk[pl.ds(i, 1), pl.ds(0, row.shape[1])] = row
        pk[pl.ds(LOSS_ROW, 1), pl.ds(0, LANE)] = loss_ref[pl.ds(0, 1), :]
        x, y, c = _place()
        me = 4 * x + 2 * y + c
        slots[me] = pk[...]
        copies = []
        for r in range(1, 8):
            rx, ry, rc = (r >> 2) & 1, (r >> 1) & 1, r & 1
            peer = (x + rx - 2 * x * rx, y + ry - 2 * y * ry, c + rc - 2 * c * rc)
            cp = pltpu.make_async_remote_copy(
                src_ref=pk, dst_ref=slots.at[me], send_sem=send_sems.at[r - 1], recv_sem=recv_sems.at[r - 1],
                device_id=peer, device_id_type=MESH)
            cp.start()
            copies.append(cp)
        for cp in copies:
            cp.wait()
        acc = slots[0]
        for j in range(1, 8):
            acc = acc + slots[j]
        o_ref[...] = acc

    vmem = pl.BlockSpec(memory_space=pltpu.VMEM)
    return pl.pallas_call(
        body, name="small_allreduce", in_specs=[vmem] * (n + 1), out_specs=vmem,
        out_shape=jax.ShapeDtypeStruct((SMALL_PK, D), F32),
        scratch_shapes=[pltpu.VMEM((SMALL_PK, D), F32), pltpu.VMEM((8, SMALL_PK, D), F32),
                        pltpu.SemaphoreType.DMA((7,)), pltpu.SemaphoreType.DMA((7,))],
    )(*rows, loss_blk)


def _seg_matrix():
    i = lax.broadcasted_iota(jnp.int32, (AD, AD), 0) // DH
    j = lax.broadcasted_iota(jnp.int32, (AD, AD), 1) // DH
    return (i == j).astype(BF16)


TAIL = ("w_out", "w_ple_gate", "w_ple", "w_pw", "dw_w")


def _local_step(x, p, tgt, sm, shards, chip, ci):
    seg = _seg_matrix()
    core = jnp.reshape(ci, (1,)).astype(jnp.int32)
    chip_idx = jnp.reshape(chip, (1,)).astype(jnp.int32)
    own = lambda g, s: lax.dynamic_update_index_in_dim(g, s, chip, 0)
    w_in_next = own(_all_gather_split(shards[0]["w_in"]), shards[0]["w_in"])
    h = x
    saved = []
    for l in range(DEPTH):
        w_in = w_in_next
        row = lambda name: sm[name][l:l + 1]
        u, hn = _rms_inproj(h, row("norm_g"), w_in)
        todo = [shards[l][k] for k in TAIL] + ([shards[l + 1]["w_in"]] if l + 1 < DEPTH else [])
        o, ya, tot, got = _attn_fwd(u, jnp.tile(row("attn_out_g"), (1, AD // DH)), todo)
        got = [own(g, s) for g, s in zip(got, todo)]
        w_out = got[0].reshape(D, D)
        w_gate = got[1].reshape(D, D)
        w_ple = got[2]
        w_pw = got[3].reshape(CD, CD)
        dw = got[4]
        if l + 1 < DEPTH:
            w_in_next = got[5]
        c1 = _glu_conv(u, dw, row("dw_b"))
        c3, yc, h1, gate, pe, h2, *at_end = _layer_tail(
            c1, u, ya, h, p[l], row("conv_ln_g"), row("conv_ln_b"), w_pw, row("conv_out_g"), w_out,
            row("ple_norm_g"), w_gate, w_ple, head=(tgt, sm["final_g"]) if l == DEPTH - 1 else None)
        saved.append(dict(h=h, u=u, hn=hn, o=o, ya=ya, tot=tot, c1=c1, c3=c3, yc=yc, h1=h1, gate=gate, pe=pe,
                          w_in=w_in, w_out=w_out, w_gate=w_gate, w_pw=w_pw, dw=dw))
        h = h2
    dh, (loss_blk, dfg) = h, at_end
    small = [None] * DEPTH
    pending, partials, arrived = [], {}, {}
    pair_sum = lambda grads: _sum_pair(core, grads, _pair_exchange(grads))
    for l in reversed(range(DEPTH)):
        s = saved[l]
        row = lambda name: sm[name][l:l + 1]
        dh1, dy, dwg, dwp, dwo, dpg = _ple_out_bwd(
            dh, s["h1"], s["gate"], s["pe"], p[l], s["ya"], s["yc"], row("ple_norm_g"), s["w_gate"], s["w_out"])
        ag_t = jnp.tile(row("attn_out_g"), (1, AD // DH))
        do, dga, dgc, dc1, dwpw, dag, dcg, dlg, dlb = _branch_bwd(
            dy, s["o"], s["u"], s["c1"], s["c3"], ag_t, row("conv_ln_g"), row("conv_ln_b"), s["w_pw"],
            row("conv_out_g"), seg)
        tail = [dwo.reshape(NCHIP, 256, D), dwg.reshape(NCHIP, 256, D), dwp, dwpw.reshape(NCHIP, 128, CD)]
        dcv, dcgate, ddw, ddb, halves = _conv_bwd(dc1, s["u"], s["dw"], tail if l == 0 else ())
        tail.append(ddw)
        if l == 0:
            partials[(l, "tail")] = _sum_pair(core, tail, halves + list(_pair_exchange([ddw])))
            pending.append((l, "tail"))
        send = [t for key in pending for t in partials[key]]
        dq, dk, dv, got = _attn_bwd(s["u"], do, s["tot"], send)
        for key in pending:
            arrived[key], got = got[:len(partials[key])], got[len(partials[key]):]
        pending = []
        du = jnp.concatenate([dq, dk, dv, dga, dcv, dcgate, dgc], axis=1)
        dwin = _inproj_dw(s["hn"], du)
        if l == 0:
            in_flight = _scatter_start(pair_sum([dwin])[0])
            dh, dng, _ = _inproj_dx(du, s["w_in"], s["h"], row("norm_g"), dh1, after=in_flight[4])
            small[l] = dict(norm_g=dng, attn_out_g=dag, dw_b=ddb, conv_ln_g=dlg, conv_ln_b=dlb,
                            conv_out_g=dcg, ple_norm_g=dpg)
            per_head = tuple(DEPTH * SMALL2.index("attn_out_g") + j for j in range(DEPTH))
            small_tot = _small_allreduce([small[j][k] for k in SMALL2 for j in range(DEPTH)] + [dfg], loss_blk,
                                         per_head)
            landed = _scatter_wait(*in_flight[:4], small_tot)
            partials[(l, "w_in")], arrived[(l, "w_in")] = [landed[0]], [landed[1]]
        else:
            dh, dng, halves = _inproj_dx(du, s["w_in"], s["h"], row("norm_g"), dh1, grads=tail + [dwin])
            tail_p = _sum_pair(core, tail + [dwin], halves)
            partials[(l, "tail")], partials[(l, "w_in")] = tail_p[:-1], tail_p[-1:]
            pending = [(l, "tail"), (l, "w_in")]
            small[l] = dict(norm_g=dng, attn_out_g=dag, dw_b=ddb, conv_ln_g=dlg, conv_ln_b=dlb,
                            conv_out_g=dcg, ple_norm_g=dpg)
    both = lambda d: [list(d[(l, "w_in")]) + list(d[(l, "tail")]) for l in range(DEPTH)]
    big = dict(zip(BIG, _sum_chips_share(chip_idx, both(partials), both(arrived))))
    return dh, big, small_tot


BIG = ("w_in", "w_out", "w_ple_gate", "w_ple", "w_pw", "dw_w")
SMALL2 = ("norm_g", "ple_norm_g", "dw_b", "conv_ln_g", "conv_ln_b", "conv_out_g", "attn_out_g")
SMALL_ROW = (0, 2, 4, 6, 8, 10, 12, 14)


def kernel(x, p, norm_g, w_in, attn_out_g, dw_w, dw_b, conv_ln_g, conv_ln_b, w_pw, conv_out_g, w_out, ple_norm_g, w_ple_gate, w_ple, final_g, loss_target, m_norm_g, m_w_in, m_attn_out_g, m_dw_w, m_dw_b, m_conv_ln_g, m_conv_ln_b, m_w_pw, m_conv_out_g, m_w_out, m_ple_norm_g, m_w_ple_gate, m_w_ple, m_final_g, v_norm_g, v_w_in, v_attn_out_g, v_dw_w, v_dw_b, v_conv_ln_g, v_conv_ln_b, v_w_pw, v_conv_out_g, v_w_out, v_ple_norm_g, v_w_ple_gate, v_w_ple, v_final_g):
    W = dict(norm_g=norm_g, w_in=w_in, attn_out_g=attn_out_g, dw_w=dw_w, dw_b=dw_b, conv_ln_g=conv_ln_g,
             conv_ln_b=conv_ln_b, w_pw=w_pw, conv_out_g=conv_out_g, w_out=w_out, ple_norm_g=ple_norm_g,
             w_ple_gate=w_ple_gate, w_ple=w_ple, final_g=final_g)
    M = dict(norm_g=m_norm_g, w_in=m_w_in, attn_out_g=m_attn_out_g, dw_w=m_dw_w, dw_b=m_dw_b,
             conv_ln_g=m_conv_ln_g, conv_ln_b=m_conv_ln_b, w_pw=m_w_pw, conv_out_g=m_conv_out_g, w_out=m_w_out,
             ple_norm_g=m_ple_norm_g, w_ple_gate=m_w_ple_gate, w_ple=m_w_ple, final_g=m_final_g)
    V = dict(norm_g=v_norm_g, w_in=v_w_in, attn_out_g=v_attn_out_g, dw_w=v_dw_w, dw_b=v_dw_b,
             conv_ln_g=v_conv_ln_g, conv_ln_b=v_conv_ln_b, w_pw=v_w_pw, conv_out_g=v_conv_out_g, w_out=v_w_out,
             ple_norm_g=v_ple_norm_g, w_ple_gate=v_w_ple_gate, w_ple=v_w_ple, final_g=v_final_g)
    order = ("norm_g", "w_in", "attn_out_g", "dw_w", "dw_b", "conv_ln_g", "conv_ln_b", "w_pw", "conv_out_g",
             "w_out", "ple_norm_g", "w_ple_gate", "w_ple", "final_g")

    pad_taps = lambda a: jnp.pad(a, ((0, 0), (0, CWP - CW), (0, 0)))
    cast = dict(w_in=w_in.astype(BF16), w_out=w_out.astype(BF16), w_ple_gate=w_ple_gate.astype(BF16),
                w_ple=w_ple.astype(BF16), w_pw=w_pw.astype(BF16), dw_w=pad_taps(dw_w))
    shards = [{k: v[l] for k, v in cast.items()} for l in range(DEPTH)]
    xi, yi, ci = lax.axis_index("x"), lax.axis_index("y"), lax.axis_index("c")
    chip = 2 * xi + yi

    sm = {k: W[k] for k in SMALL2}
    sm["final_g"] = final_g.reshape(1, D)
    grad_x, big, small_tot = _local_step(x[0], p[:, 0], loss_target[0], sm, shards, chip, ci)
    g_big = {name: big[name].reshape(cast[name].shape) for name in BIG}

    small_names = SMALL2 + ("final_g",)
    as_rows = lambda t: t.reshape(1, D) if t.ndim == 1 else t
    results, loss_row = _small_adamw(
        small_tot, [as_rows(W[k]) for k in small_names],
        [as_rows(M[k]) for k in small_names], [as_rows(V[k]) for k in small_names])
    loss = loss_row[0, 0]

    grads, deltas, new_m, new_v = {}, {}, {}, {}
    for name in BIG:
        if name == "dw_w":
            grads[name], deltas[name], new_m[name], new_v[name] = _adamw_taps(W[name], g_big[name], M[name], V[name])
            continue
        cols = W[name].shape[-1]
        rows_total = W[name].size // cols
        flat = lambda a: a.reshape(rows_total, cols)
        four = _adamw(flat(W[name]), flat(g_big[name]), flat(M[name]), flat(V[name]), min(rows_total, 256))
        grads[name], deltas[name], new_m[name], new_v[name] = (t.reshape(W[name].shape) for t in four)
    for k, four in zip(small_names, results):
        grads[k], deltas[k], new_m[k], new_v[k] = (t.reshape(W[k].shape) for t in four)

    return (loss, grad_x[None], *[grads[n] for n in order], *[deltas[n] for n in order],
            *[new_m[n] for n in order], *[new_v[n] for n in order])
```

```python
import functools

import jax
import jax.numpy as jnp
from jax import lax
from jax.experimental import pallas as pl
from jax.experimental.pallas import tpu as pltpu

F32 = jnp.float32
BF16 = jnp.bfloat16

T = 2048
D = 1024
DIN = 3584
NCHIP = 4
SHW = DIN // NCHIP
AD = 512
CD = 512
DH = 64
CW = 31
CWP = 32
PLE = 256
DEPTH = 2
EPS = 1e-6
AQ = 256
HG = 4
GW = HG * DH
SR = HG * AQ
NG = AD // GW
LANE = 128
TM = 1024
TR = 256

ADAM_LR = 0.001
ADAM_B1 = 0.9
ADAM_B2 = 0.999
ADAM_EPS = 1e-08
ADAM_WD = 0.01
ADAM_STEP = 10

SMALL_PK = 16
LOSS_ROW = 15

VMEM_BIG = 56 * 1024 * 1024
MESH = pl.DeviceIdType.MESH


def _cp(sem=None, vmem=None):
    kw = {}
    if sem is not None:
        kw["dimension_semantics"] = sem
    if vmem is not None:
        kw["vmem_limit_bytes"] = vmem
    return pltpu.CompilerParams(**kw)


def _dot(a, b):
    return jnp.dot(a, b, preferred_element_type=F32)


def _dot_nt(a, b):
    return lax.dot_general(a, b, (((1,), (1,)), ((), ())), preferred_element_type=F32)


def _dot_tn(a, b):
    return lax.dot_general(a, b, (((0,), (0,)), ((), ())), preferred_element_type=F32)


def _dot2(x, m):
    hi = x.astype(BF16)
    lo = (x - hi.astype(F32)).astype(BF16)
    return _dot(hi, m) + _dot(lo, m)


def _sig(x):
    return 1.0 / (1.0 + jnp.exp(-x))


def _softplus(z):
    return jnp.maximum(z, 0.0) + jnp.log(1.0 + jnp.exp(-jnp.abs(z)))


def _rstd(x):
    return lax.rsqrt(jnp.mean(x * x, axis=-1, keepdims=True) + EPS)


def _rms_bwd(dy, x, r, g):
    dn = dy * g
    return r * dn - x * (r * r * r) * jnp.mean(dn * x, axis=-1, keepdims=True)


def _rms_inproj(h, g, w):
    tm = min(TM, T)

    def body(h_ref, g_ref, w_ref, u_ref, hn_ref, hn_s):
        @pl.when(pl.program_id(1) == 0)
        def _():
            hh = h_ref[...]
            hn = (hh * _rstd(hh) * g_ref[...]).astype(BF16)
            hn_s[...] = hn
            hn_ref[...] = hn
        u_ref[...] = _dot(hn_s[...], w_ref[0])

    return pl.pallas_call(
        body, name="rms_inproj", grid=(T // tm, NCHIP),
        in_specs=[pl.BlockSpec((tm, D), lambda i, k: (i, 0)),
                  pl.BlockSpec((1, D), lambda i, k: (0, 0)),
                  pl.BlockSpec((1, D, SHW), lambda i, k: (k, 0, 0))],
        out_specs=[pl.BlockSpec((tm, SHW), lambda i, k: (i, k)),
                   pl.BlockSpec((tm, D), lambda i, k: (i, 0))],
        out_shape=[jax.ShapeDtypeStruct((T, DIN), F32), jax.ShapeDtypeStruct((T, D), BF16)],
        scratch_shapes=[pltpu.VMEM((tm, D), BF16)],
        compiler_params=_cp(("arbitrary", "arbitrary"), VMEM_BIG),
    )(h, g, w)


def _attn_tiles():
    row = lax.broadcasted_iota(jnp.int32, (SR, AQ), 0) & (AQ - 1)
    col = lax.broadcasted_iota(jnp.int32, (SR, AQ), 1)
    tr = lax.broadcasted_iota(jnp.int32, (AQ, AQ), 0)
    tc = lax.broadcasted_iota(jnp.int32, (AQ, AQ), 1)
    lane_head = lax.broadcasted_iota(jnp.int32, (1, GW), 1) // DH
    return col < row, tr, tc, [lane_head == h for h in range(HG)]


def _stack_heads(t, heads):
    return jnp.concatenate([jnp.where(m, t, 0.0) for m in heads], axis=0)


def _unstack_heads(t, heads):
    out = t[:AQ]
    for h in range(1, HG):
        out = jnp.where(heads[h], t[h * AQ:(h + 1) * AQ], out)
    return out


def _tri_sum(x, tri):
    hi = x.astype(BF16)
    lo = (x - hi.astype(F32)).astype(BF16)
    both = _dot(jnp.concatenate([hi, lo], axis=0), tri)
    return both[:SR] + both[SR:]


def _scatter_copies(ps, gots, send_sems, recv_sems):
    x, y, c = _place()
    peers = [(1 - x, y), (x, 1 - y), (1 - x, 1 - y)]
    return [pltpu.make_async_remote_copy(
        src_ref=ps[a].at[2 * px + py], dst_ref=gots[a].at[r], send_sem=send_sems.at[3 * a + r],
        recv_sem=recv_sems.at[3 * a + r], device_id=(px, py, c), device_id_type=MESH)
        for a in range(len(ps)) for r, (px, py) in enumerate(peers)]


GATHER_SEMS = 7


def _gather_tree_copies(ins, outs, send_sems, recv_sems):
    x, y, c = _place()
    me, xn, yn, dg = 2 * x + y, 2 * (1 - x) + y, 2 * x + (1 - y), 2 * (1 - x) + (1 - y)
    to_x, to_y, sibling = (1 - x, y, c), (x, 1 - y, c), (x, y, 1 - c)
    direct, relayed, passed = [], [], []
    for a in range(len(ins)):
        half = ins[a].shape[0] // 2
        mine = pl.ds(c * half, half)
        first, second = pl.ds(c * half, half // 2), pl.ds(c * half + half // 2, half // 2)

        def copy(i, src, dst, to, k=GATHER_SEMS * a):
            return pltpu.make_async_remote_copy(src_ref=src, dst_ref=dst, send_sem=send_sems.at[k + i],
                                                recv_sem=recv_sems.at[k + i], device_id=to, device_id_type=MESH)

        own, slot = ins[a].at[mine], outs[a].at[me, mine]
        direct += [copy(0, own, slot, to_x), copy(1, own, slot, to_y)]
        relayed += [copy(2, outs[a].at[xn, first], outs[a].at[xn, first], to_y),
                    copy(3, outs[a].at[yn, second], outs[a].at[yn, second], to_x)]
        passed += [copy(4 + i, outs[a].at[j, mine], outs[a].at[j, mine], sibling) for i, j in enumerate((xn, yn, dg))]
    return direct, relayed, passed


def _pair_copies(ins, outs, send_sems, recv_sems):
    x, y, c = _place()
    copies = []
    for a in range(len(ins)):
        half = ins[a].shape[1] // 2
        copies.append(pltpu.make_async_remote_copy(
            src_ref=ins[a].at[:, pl.ds((1 - c) * half, half), :], dst_ref=outs[a], send_sem=send_sems.at[a],
            recv_sem=recv_sems.at[a], device_id=(x, y, 1 - c), device_id_type=MESH))
    return copies


def _host(body, grid, n_in, n_out, n_x, make_copies, mids=()):
    if not n_x:
        return body

    def hosting(*refs):
        a, b = n_in + n_x, n_in + 2 * n_x + n_out
        copies = make_copies(refs[n_in:a], refs[a + n_out:b], refs[-2], refs[-1])
        stages = copies if isinstance(copies, tuple) else (copies,)
        ids = [pl.program_id(d) for d in range(len(grid))]
        at = lambda step: functools.reduce(jnp.logical_and, [i == s for i, s in zip(ids, step)])

        @pl.when(at([0] * len(grid)))
        def _():
            for cp in stages[0]:
                cp.start()

        for before, after, step in zip(stages, stages[1:], mids):
            @pl.when(at(step))
            def _(before=before, after=after):
                for cp in before:
                    cp.wait_recv()
                for cp in after:
                    cp.start()

        body(*refs[:n_in], *refs[a:a + n_out], *refs[b:-2])

        @pl.when(at([g - 1 for g in grid]))
        def _():
            for cp in stages[-1]:
                cp.wait_recv()
            for stage in stages:
                for cp in stage:
                    cp.wait_send()

    return hosting


def _hosted_sems(n_x, per_array=3):
    n = per_array * n_x
    return [pltpu.SemaphoreType.DMA((n,)), pltpu.SemaphoreType.DMA((n,))] if n_x else []


RC = 256


def _chunk_causal(r):
    row = lax.broadcasted_iota(jnp.int32, (RC, AQ), 0) + (r * RC) % AQ
    return lax.broadcasted_iota(jnp.int32, (RC, AQ), 1) < row


def _attn_fwd(u, agw, shards=()):
    n = len(shards)
    grid = (T // AQ,)

    def body(q_ref, k_ref, v_ref, g_ref, ag_ref, o_ref, y_ref, tot_ref,
             kb_s, vb_s, qs_s, z_s, zs_s, lmb_s, suf_s, att_s, acc_s, run_s):
        qi = pl.program_id(0)

        @pl.when(qi == 0)
        def _():
            kb_s[...] = k_ref[...].astype(BF16)
            vb_s[...] = v_ref[...].astype(BF16)

        _, tr, tc, heads = _attn_tiles()
        upper = (tr > tc).astype(BF16)
        same_head = ((tr // DH) == (tc // DH)).astype(BF16)
        for g in range(NG):
            qs_s[g] = _stack_heads(q_ref[:, g * GW:(g + 1) * GW] * 0.125, heads).astype(BF16)
        acc_s[...] = jnp.zeros_like(acc_s)
        run_s[...] = jnp.zeros_like(run_s)

        def block(kb, masked):
            k0 = pl.multiple_of(kb * AQ, AQ)
            for g in range(NG):
                lanes = pl.ds(g * GW, GW)
                z_s[g] = _dot_nt(qs_s[g], kb_s[pl.ds(k0, AQ), lanes])
                for r in range(SR // RC):
                    rows = pl.ds(r * RC, RC)
                    z = z_s[g, rows, :]
                    zs = jnp.minimum(z, 0.0) - jnp.log(1.0 + jnp.exp(-jnp.abs(z)))
                    lm = zs - z
                    if masked:
                        lm = jnp.where(_chunk_causal(r), lm, 0.0)
                    run = run_s[g, rows, :]
                    zs_s[g, rows, :] = zs + run[:, 0:1]
                    hi = lm.astype(BF16)
                    lmb_s[g, rows, :] = hi
                    lmb_s[g, pl.ds(SR + r * RC, RC), :] = (lm - hi.astype(F32)).astype(BF16)
                    run_s[g, rows, :] = run + jnp.sum(lm, axis=1, keepdims=True)
                suf_s[g] = _dot(lmb_s[g], upper)
                for r in range(SR // RC):
                    rows = pl.ds(r * RC, RC)
                    att = jnp.exp(zs_s[g, rows, :] + suf_s[g, rows, :] + suf_s[g, pl.ds(SR + r * RC, RC), :])
                    if masked:
                        att = jnp.where(_chunk_causal(r), att, 0.0)
                    att_s[g, rows, :] = att.astype(BF16)
                acc_s[g] += _dot(att_s[g], vb_s[pl.ds(k0, AQ), lanes])

        block(qi, True)

        def step(i, c):
            block(qi - 1 - i, False)
            return c

        lax.fori_loop(0, qi, step, 0)
        gate = g_ref[...]
        agv = ag_ref[...]
        for g in range(NG):
            lanes = slice(g * GW, (g + 1) * GW)
            o = _unstack_heads(acc_s[g], heads)
            osq = o * o
            ms = _dot2(osq, same_head)
            gg = gate[:, lanes]
            o_ref[:, lanes] = o
            y_ref[:, lanes] = (o * lax.rsqrt(ms * (1.0 / DH) + EPS) * agv[:, lanes] * (gg * _sig(gg))).astype(BF16)
            tot_ref[g] = _unstack_heads(jnp.broadcast_to(run_s[g][:, 0:1], (SR, GW)), heads)

    tile = lambda dt, rows=SR: pltpu.VMEM((NG, rows, AQ), dt)
    scratch = [pltpu.VMEM((T, AD), BF16), pltpu.VMEM((T, AD), BF16), pltpu.VMEM((NG, SR, GW), BF16),
               tile(F32), tile(F32), tile(BF16, 2 * SR), tile(F32, 2 * SR), tile(BF16),
               pltpu.VMEM((NG, SR, GW), F32), pltpu.VMEM((NG, SR, LANE), F32)]
    col = lambda j: pl.BlockSpec((AQ, AD), lambda qi: (qi, j))
    res = pl.pallas_call(
        _host(body, grid, 5, 3, n, _gather_tree_copies, mids=((grid[0] * 5 // 8,), (grid[0] * 7 // 8,))), name="attn_fwd", grid=grid,
        in_specs=[col(0), pl.BlockSpec((T, AD), lambda qi: (0, 1)), pl.BlockSpec((T, AD), lambda qi: (0, 2)),
                  col(3), pl.BlockSpec((1, AD), lambda qi: (0, 0))] + [HBM_SPEC] * n,
        out_specs=[col(0), col(0), pl.BlockSpec((NG, AQ, GW), lambda qi: (0, qi, 0))] + [HBM_SPEC] * n,
        out_shape=[jax.ShapeDtypeStruct((T, AD), F32), jax.ShapeDtypeStruct((T, AD), BF16),
                   jax.ShapeDtypeStruct((NG, T, GW), F32)]
        + [jax.ShapeDtypeStruct((NCHIP,) + s.shape, s.dtype) for s in shards],
        scratch_shapes=scratch + _hosted_sems(n, GATHER_SEMS),
        compiler_params=_cp(("arbitrary",), VMEM_BIG),
    )(u, u, u, u, agw, *shards)
    return res[0], res[1], res[2], list(res[3:])


def _glu_conv(u, dw, db):
    tr = 256

    def body(cv_ref, cg_ref, w_ref, b_ref, c1_ref, pad_s):
        pad_s[pl.ds(0, CWP), :] = jnp.zeros((CWP, LANE), F32)
        pad_s[pl.ds(CWP, T), :] = cv_ref[...] * _sig(cg_ref[...])
        wv = w_ref[0]
        bias = b_ref[...]

        def tile(i, carry):
            r0 = pl.multiple_of(i * tr, tr)
            acc = jnp.zeros((tr, LANE), F32) + bias
            for w in range(CW):
                acc = acc + pad_s[pl.ds(r0 + (CWP - CW + 1) + w, tr), :] * wv[w:w + 1, :]
            c1_ref[pl.ds(r0, tr), :] = acc
            return carry

        lax.fori_loop(0, T // tr, tile, 0)

    return pl.pallas_call(
        body, name="glu_conv", grid=(CD // LANE,),
        in_specs=[pl.BlockSpec((T, LANE), lambda cb: (0, 16 + cb)),
                  pl.BlockSpec((T, LANE), lambda cb: (0, 20 + cb)),
                  pl.BlockSpec((1, CWP, LANE), lambda cb: (cb, 0, 0)),
                  pl.BlockSpec((1, LANE), lambda cb: (0, cb))],
        out_specs=pl.BlockSpec((T, LANE), lambda cb: (0, cb)),
        out_shape=jax.ShapeDtypeStruct((T, CD), F32),
        scratch_shapes=[pltpu.VMEM((T + CWP, LANE), F32)],
        compiler_params=_cp(("arbitrary",)),
    )(u, u, dw, db)


def _ln_silu(c1, lg, lb):
    mu = jnp.mean(c1, axis=-1, keepdims=True)
    xc = c1 - mu
    rs = lax.rsqrt(jnp.mean(xc * xc, axis=-1, keepdims=True) + EPS)
    xh = xc * rs
    ln = xh * lg + lb
    s = _sig(ln)
    return xh, rs, ln, s


def _layer_tail(c1, u, ya, h, p, lg, lb, wpw, cg, wout, pg, wgate, wple, head=None):
    tm = min(TR, T)

    def body(c1_ref, gc_ref, ya_ref, h_ref, p_ref, lg_ref, lb_ref, wpw_ref, cg_ref, wout_ref,
             pg_ref, wgate_ref, wple_ref, *rest):
        c3_ref, yc_ref, h1_ref, gate_ref, pe_ref, h2_ref = rest[-6 - 2 * bool(head):][:6]
        _, _, ln, s = _ln_silu(c1_ref[...], lg_ref[...], lb_ref[...])
        c2 = (ln * s).astype(BF16)
        c3 = _dot(c2, wpw_ref[...])
        gc = gc_ref[...]
        yc = (c3 * _rstd(c3) * cg_ref[...] * (gc * _sig(gc))).astype(BF16)
        c3_ref[...] = c3
        yc_ref[...] = yc
        y = _dot(ya_ref[...], wout_ref[pl.ds(0, AD), :]) + _dot(yc, wout_ref[pl.ds(AD, CD), :])
        h1 = h_ref[...] + y
        hn2 = (h1 * _rstd(h1) * pg_ref[...]).astype(BF16)
        gate = _sig(_dot(hn2, wgate_ref[...]))
        pb = p_ref[...].astype(BF16)
        pe = jnp.concatenate([_dot(pb, wple_ref[k]) for k in range(NCHIP)], axis=1)
        h1_ref[...] = h1
        gate_ref[...] = gate.astype(BF16)
        pe_ref[...] = pe.astype(BF16)
        h2 = h1 + pe * gate
        if not head:
            h2_ref[...] = h2
            return
        t_ref, fg_ref, loss_ref, dfg_ref = rest[0], rest[1], rest[-2], rest[-1]

        @pl.when(pl.program_id(0) == 0)
        def _():
            loss_ref[...] = jnp.zeros_like(loss_ref)
            dfg_ref[...] = jnp.zeros_like(dfg_ref)
        fg = fg_ref[...]
        r = _rstd(h2)
        e = h2 * r * fg - t_ref[...]
        loss_ref[...] += 0.5 * jnp.sum(jnp.mean(e * e, axis=-1, keepdims=True))
        dy = e * (1.0 / D)
        dfg_ref[...] += jnp.sum(dy * h2 * r, axis=0, keepdims=True)
        h2_ref[...] = _rms_bwd(dy, h2, r, fg)

    row = lambda w: pl.BlockSpec((tm, w), lambda i: (i, 0))
    full = lambda *s: pl.BlockSpec(s, lambda i: (0,) * len(s), pipeline_mode=pl.Buffered(1))
    extra = bool(head)
    return pl.pallas_call(
        body, name="layer_tail", grid=(T // tm,),
        in_specs=[row(CD), pl.BlockSpec((tm, CD), lambda i: (i, 6)), row(AD), row(D), row(PLE),
                  full(1, CD), full(1, CD), full(CD, CD), full(1, CD), full(D, D),
                  full(1, D), full(D, D), full(NCHIP, PLE, PLE)] + [row(D), full(1, D)] * extra,
        out_specs=[row(CD), row(CD), row(D), row(D), row(D), row(D)] + [full(8, LANE), full(1, D)] * extra,
        out_shape=[jax.ShapeDtypeStruct((T, CD), F32), jax.ShapeDtypeStruct((T, CD), BF16),
                   jax.ShapeDtypeStruct((T, D), F32), jax.ShapeDtypeStruct((T, D), BF16),
                   jax.ShapeDtypeStruct((T, D), BF16), jax.ShapeDtypeStruct((T, D), F32)]
        + [jax.ShapeDtypeStruct((8, LANE), F32), jax.ShapeDtypeStruct((1, D), F32)] * extra,
        compiler_params=_cp(("arbitrary",), VMEM_BIG),
    )(c1, u, ya, h, p, lg, lb, wpw, cg, wout, pg, wgate, wple, *(head or ()))


def _ple_out_bwd(dh2, h1, gate, pe, p, ya, yc, pg, wgate, wout):
    tm = min(TR, T)

    def body(dh2_ref, h1_ref, gate_ref, pe_ref, p_ref, ya_ref, yc_ref, pg_ref, wgate_ref, wout_ref,
             dh1_ref, dy_ref, dwg_ref, dwp_ref, dwo_ref, dpg_ref):
        @pl.when(pl.program_id(0) == 0)
        def _():
            dwg_ref[...] = jnp.zeros_like(dwg_ref)
            dwp_ref[...] = jnp.zeros_like(dwp_ref)
            dwo_ref[...] = jnp.zeros_like(dwo_ref)
            dpg_ref[...] = jnp.zeros_like(dpg_ref)
        dh2 = dh2_ref[...]
        h1 = h1_ref[...]
        gate = gate_ref[...].astype(F32)
        pg = pg_ref[...]
        dpe = (dh2 * gate).astype(BF16)
        dgp = (dh2 * pe_ref[...].astype(F32) * gate * (1.0 - gate)).astype(BF16)
        r = _rstd(h1)
        hn = h1 * r
        dwg_ref[...] += _dot_tn((hn * pg).astype(BF16), dgp)
        dhn2 = _dot_nt(dgp, wgate_ref[...])
        dpg_ref[...] += jnp.sum(dhn2 * hn, axis=0, keepdims=True)
        dh1 = dh2 + _rms_bwd(dhn2, h1, r, pg)
        pb = p_ref[...].astype(BF16)
        for k in range(NCHIP):
            dwp_ref[k] += _dot_tn(pb, dpe[:, k * PLE:(k + 1) * PLE])
        dh1b = dh1.astype(BF16)
        dy_ref[...] = _dot_nt(dh1b, wout_ref[...])
        dwo_ref[pl.ds(0, AD), :] += _dot_tn(ya_ref[...], dh1b)
        dwo_ref[pl.ds(AD, CD), :] += _dot_tn(yc_ref[...], dh1b)
        dh1_ref[...] = dh1

    row = lambda w: pl.BlockSpec((tm, w), lambda i: (i, 0))
    full = lambda *s: pl.BlockSpec(s, lambda i: (0,) * len(s), pipeline_mode=pl.Buffered(1))
    return pl.pallas_call(
        body, name="ple_out_bwd", grid=(T // tm,),
        in_specs=[row(D), row(D), row(D), row(D), row(PLE), row(AD), row(CD),
                  full(1, D), full(D, D), full(D, D)],
        out_specs=[row(D), row(D), full(D, D), full(NCHIP, PLE, PLE), full(D, D), full(1, D)],
        out_shape=[jax.ShapeDtypeStruct((T, D), F32), jax.ShapeDtypeStruct((T, D), F32),
                   jax.ShapeDtypeStruct((D, D), F32), jax.ShapeDtypeStruct((NCHIP, PLE, PLE), F32),
                   jax.ShapeDtypeStruct((D, D), F32), jax.ShapeDtypeStruct((1, D), F32)],
        compiler_params=_cp(("arbitrary",), VMEM_BIG),
    )(dh2, h1, gate, pe, p, ya, yc, pg, wgate, wout)


def _branch_bwd(dy, o, u, c1, c3, ag, lg, lb, wpw, cg, seg):
    tm = min(TR, T)

    def body(dya_ref, dyc_ref, o_ref, ga_ref, gc_ref, c1_ref, c3_ref, ag_ref, lg_ref, lb_ref, wpw_ref,
             cg_ref, seg_ref, do_ref, dga_ref, dgc_ref, dc1_ref, dwpw_ref, dag_ref, dcg_ref, dlg_ref, dlb_ref):
        @pl.when(pl.program_id(0) == 0)
        def _():
            for r_ in (dwpw_ref, dag_ref, dcg_ref, dlg_ref, dlb_ref):
                r_[...] = jnp.zeros_like(r_)
        dya = dya_ref[...]
        o = o_ref[...]
        ga = ga_ref[...]
        ag_v = ag_ref[...]
        seg_m = seg_ref[...]
        r = lax.rsqrt(_dot2(o * o, seg_m) * (1.0 / DH) + EPS)
        onr = o * r
        sg = _sig(ga)
        dga_ref[...] = (dya * (onr * ag_v) * (sg * (1.0 + ga * (1.0 - sg)))).astype(BF16)
        don = dya * (ga * sg)
        dag_ref[...] += jnp.sum(don * onr, axis=0, keepdims=True)
        dn = don * ag_v
        do_ref[...] = r * dn - o * (r * r * r) * (_dot2(dn * o, seg_m) * (1.0 / DH))
        dyc = dyc_ref[...]
        c3 = c3_ref[...]
        gc = gc_ref[...]
        cg_v = cg_ref[...]
        r3 = _rstd(c3)
        cn = c3 * r3
        sc = _sig(gc)
        dgc_ref[...] = (dyc * (cn * cg_v) * (sc * (1.0 + gc * (1.0 - sc)))).astype(BF16)
        dcn = dyc * (gc * sc)
        dcg_ref[...] += jnp.sum(dcn * cn, axis=0, keepdims=True)
        dc3 = _rms_bwd(dcn, c3, r3, cg_v).astype(BF16)
        lg_v = lg_ref[...]
        xh, rs, ln, s = _ln_silu(c1_ref[...], lg_v, lb_ref[...])
        c2 = (ln * s).astype(BF16)
        dwpw_ref[...] += _dot_tn(c2, dc3)
        dc2 = _dot_nt(dc3, wpw_ref[...])
        dln = dc2 * (s * (1.0 + ln * (1.0 - s)))
        dlb_ref[...] += jnp.sum(dln, axis=0, keepdims=True)
        dlg_ref[...] += jnp.sum(dln * xh, axis=0, keepdims=True)
        dxh = dln * lg_v
        dc1_ref[...] = rs * (dxh - jnp.mean(dxh, axis=-1, keepdims=True)
                             - xh * jnp.mean(dxh * xh, axis=-1, keepdims=True))

    half = lambda j: pl.BlockSpec((tm, 512), lambda i: (i, j))
    full = lambda *s: pl.BlockSpec(s, lambda i: (0,) * len(s), pipeline_mode=pl.Buffered(1))
    vec = jax.ShapeDtypeStruct((1, 512), F32)
    act = jax.ShapeDtypeStruct((T, 512), F32)
    return pl.pallas_call(
        body, name="branch_bwd", grid=(T // tm,),
        in_specs=[half(0), half(1), half(0), half(3), half(6), half(0), half(0),
                  full(1, AD), full(1, CD), full(1, CD), full(CD, CD), full(1, CD), full(AD, AD)],
        out_specs=[half(0), half(0), half(0), half(0), full(CD, CD), full(1, 512), full(1, 512),
                   full(1, 512), full(1, 512)],
        out_shape=[act, jax.ShapeDtypeStruct((T, 512), BF16), jax.ShapeDtypeStruct((T, 512), BF16), act,
                   jax.ShapeDtypeStruct((CD, CD), F32), vec, vec, vec, vec],
        compiler_params=_cp(("arbitrary",), VMEM_BIG),
    )(dy, dy, o, u, u, c1, c3, ag, lg, lb, wpw, cg, seg)


def _conv_bwd(dc1, u, dw, grads=()):
    tr = 64
    n_x = len(grads)
    grid = (CD // LANE,)
    off = CWP - CW + 1

    def body(d_ref, cv_ref, cg_ref, w_ref, dcv_ref, dcg_ref, ddw_ref, ddb_ref, padc_s, padd_s, acc_s):
        cv = cv_ref[...]
        sg = _sig(cg_ref[...])
        padc_s[pl.ds(0, CWP), :] = jnp.zeros((CWP, LANE), F32)
        padc_s[pl.ds(CWP, T), :] = cv * sg
        padd_s[pl.ds(0, T), :] = d_ref[...]
        padd_s[pl.ds(T, CWP), :] = jnp.zeros((CWP, LANE), F32)
        acc_s[...] = jnp.zeros_like(acc_s)
        wv = w_ref[0]

        def tile(i, carry):
            r0 = pl.multiple_of(i * tr, tr)
            dt = padd_s[pl.ds(r0, tr), :]
            dc0 = jnp.zeros((tr, LANE), F32)
            for w in range(CW):
                dc0 = dc0 + padd_s[pl.ds(r0 + (CW - 1) - w, tr), :] * wv[w:w + 1, :]
                prod = dt * padc_s[pl.ds(r0 + off + w, tr), :]
                acc_s[w] += jnp.sum(prod.reshape(tr // 8, 8, LANE), axis=0)
            cvt = cv_ref[pl.ds(r0, tr), :]
            sgt = _sig(cg_ref[pl.ds(r0, tr), :])
            dcv_ref[pl.ds(r0, tr), :] = (dc0 * sgt).astype(BF16)
            dcg_ref[pl.ds(r0, tr), :] = (dc0 * cvt * sgt * (1.0 - sgt)).astype(BF16)
            return carry

        lax.fori_loop(0, T // tr, tile, 0)
        ddw_ref[0] = jnp.sum(acc_s[...], axis=1)
        ddb_ref[...] = jnp.sum(d_ref[...], axis=0, keepdims=True)

    col = lambda j: pl.BlockSpec((T, LANE), lambda cb: (0, j + cb))
    res = pl.pallas_call(
        _host(body, grid, 4, 4, n_x, _pair_copies), name="conv_bwd", grid=grid,
        in_specs=[col(0), col(16), col(20), pl.BlockSpec((1, CWP, LANE), lambda cb: (cb, 0, 0))] + [HBM_SPEC] * n_x,
        out_specs=[col(0), col(0), pl.BlockSpec((1, CWP, LANE), lambda cb: (cb, 0, 0)),
                   pl.BlockSpec((1, LANE), lambda cb: (0, cb))] + [HBM_SPEC] * n_x,
        out_shape=[jax.ShapeDtypeStruct((T, CD), BF16), jax.ShapeDtypeStruct((T, CD), BF16),
                   jax.ShapeDtypeStruct((NCHIP, CWP, LANE), F32), jax.ShapeDtypeStruct((1, CD), F32)]
        + [jax.ShapeDtypeStruct((NCHIP, g.shape[1] // 2, g.shape[2]), F32) for g in grads],
        scratch_shapes=[pltpu.VMEM((T + CWP, LANE), F32), pltpu.VMEM((T + CWP, LANE), F32),
                        pltpu.VMEM((CWP, 8, LANE), F32)] + _hosted_sems(n_x),
        compiler_params=_cp(("arbitrary",)),
    )(dc1, u, u, dw, *grads)
    return res[0], res[1], res[2], res[3], list(res[4:])


def _attn_bwd(u, do, tot, partials=()):
    n_x = len(partials)
    grid = (T // AQ,)

    def body(q_ref, k_ref, v_ref, do_ref, tot_ref, dq_ref, dk_ref, dv_ref, kb_s, vb_s, dk_s, dv_s):
        qi = pl.program_id(0)

        @pl.when(qi == 0)
        def _():
            kb_s[...] = k_ref[...].astype(BF16)
            vb_s[...] = v_ref[...].astype(BF16)
            dk_s[...] = jnp.zeros_like(dk_s)
            dv_s[...] = jnp.zeros_like(dv_s)

        causal, tr, tc, heads = _attn_tiles()
        upper = (tr > tc).astype(BF16)
        lower = (tr < tc).astype(BF16)
        qs, qus, dos, tots = [], [], [], []
        for g in range(NG):
            lanes = slice(g * GW, (g + 1) * GW)
            q = q_ref[:, lanes]
            qs.append(_stack_heads(q * 0.125, heads).astype(BF16))
            qus.append(_stack_heads(q, heads).astype(BF16))
            dos.append(_stack_heads(do_ref[:, lanes], heads).astype(BF16))
            totv = tot_ref[g]
            tots.append(jnp.concatenate([totv[:, h * DH:h * DH + 1] for h in range(HG)], axis=0))

        def block(kb, carry, masked):
            k0 = pl.multiple_of(kb * AQ, AQ)
            out = []
            for g in range(NG):
                lanes = pl.ds(g * GW, GW)
                lm_left, dl_left, dq = carry[g]
                kk = kb_s[pl.ds(k0, AQ), lanes]
                vv = vb_s[pl.ds(k0, AQ), lanes]
                z = _dot_nt(qs[g], kk)
                sp = _softplus(z)
                lm = jnp.where(causal, -sp, 0.0) if masked else -sp
                lm_incl = lm_left + jnp.sum(lm, axis=1, keepdims=True)
                att = jnp.exp((z - sp) + _tri_sum(lm, upper) + (tots[g] - lm_incl))
                if masked:
                    att = jnp.where(causal, att, 0.0)
                dl = att * _dot_nt(dos[g], vv)
                dv_s[pl.ds(k0, AQ), lanes] += _dot_tn(att.astype(BF16), dos[g])
                prefix = dl_left + _tri_sum(dl, lower)
                beta = jnp.exp(z - sp)
                dz = (1.0 - beta) * dl - beta * prefix
                if masked:
                    dz = jnp.where(causal, dz, 0.0)
                dzs = (dz * 0.125).astype(BF16)
                dk_s[pl.ds(k0, AQ), lanes] += _dot_tn(dzs, qus[g])
                out.append((lm_incl, dl_left + jnp.sum(dl, axis=1, keepdims=True), dq + _dot(dzs, kk)))
            return tuple(out)

        zero = jnp.zeros((SR, 1), F32)
        init = tuple((zero, zero, jnp.zeros((SR, GW), F32)) for _ in range(NG))
        carry = lax.fori_loop(0, qi, lambda kb, c: block(kb, c, False), init)
        carry = block(qi, carry, True)
        for g in range(NG):
            dq_ref[:, g * GW:(g + 1) * GW] = _unstack_heads(carry[g][2], heads).astype(BF16)

        @pl.when(qi == grid[0] - 1)
        def _():
            dk_ref[...] = dk_s[...].astype(BF16)
            dv_ref[...] = dv_s[...].astype(BF16)

    col = lambda j: pl.BlockSpec((AQ, AD), lambda qi: (qi, j))
    whole = lambda j: pl.BlockSpec((T, AD), lambda qi: (0, j), pipeline_mode=pl.Buffered(1))
    res = pl.pallas_call(
        _host(body, grid, 5, 3, n_x, _scatter_copies), name="attn_bwd", grid=grid,
        in_specs=[col(0), whole(1), whole(2), col(0), pl.BlockSpec((NG, AQ, GW), lambda qi: (0, qi, 0))]
        + [HBM_SPEC] * n_x,
        out_specs=[col(0), whole(0), whole(0)] + [HBM_SPEC] * n_x,
        out_shape=[jax.ShapeDtypeStruct((T, AD), BF16)] * 3
        + [jax.ShapeDtypeStruct((NCHIP - 1,) + a.shape[1:], a.dtype) for a in partials],
        scratch_shapes=[pltpu.VMEM((T, AD), BF16), pltpu.VMEM((T, AD), BF16), pltpu.VMEM((T, AD), F32),
                        pltpu.VMEM((T, AD), F32)] + _hosted_sems(n_x),
        compiler_params=_cp(("arbitrary",), VMEM_BIG),
    )(u, u, u, do, tot, *partials)
    return res[0], res[1], res[2], list(res[3:])


def _inproj_dw(hn, du):
    tm = min(TM, T)

    def body(hn_ref, du_ref, dw_ref):
        @pl.when(pl.program_id(1) == 0)
        def _():
            dw_ref[...] = jnp.zeros_like(dw_ref)
        dw_ref[0] += _dot_tn(hn_ref[...], du_ref[...])

    return pl.pallas_call(
        body, name="inproj_dw", grid=(NCHIP, T // tm),
        in_specs=[pl.BlockSpec((tm, D), lambda k, i: (i, 0)), pl.BlockSpec((tm, SHW), lambda k, i: (i, k))],
        out_specs=pl.BlockSpec((1, D, SHW), lambda k, i: (k, 0, 0)),
        out_shape=jax.ShapeDtypeStruct((NCHIP, D, SHW), F32),
        compiler_params=_cp(("arbitrary", "arbitrary"), VMEM_BIG),
    )(hn, du)


def _inproj_dx(du, w, h, g, dres, partials=(), grads=(), after=None):
    tm = min(TM, T)
    sent = list(partials) + list(grads)
    n_x = len(sent)
    assert after is None or not sent
    grid = (T // tm, NCHIP)
    if grads:
        landing = [jax.ShapeDtypeStruct((NCHIP, a.shape[1] // 2, a.shape[2]), F32) for a in grads]
    else:
        landing = [jax.ShapeDtypeStruct((NCHIP - 1,) + a.shape[1:], a.dtype) for a in partials]

    def body(du_ref, w_ref, h_ref, g_ref, dres_ref, dh_ref, dg_ref, acc_s):
        i, k = pl.program_id(0), pl.program_id(1)

        @pl.when(jnp.logical_and(i == 0, k == 0))
        def _():
            dg_ref[...] = jnp.zeros_like(dg_ref)

        @pl.when(k == 0)
        def _():
            acc_s[...] = _dot_nt(du_ref[...], w_ref[0])

        @pl.when(k > 0)
        def _():
            acc_s[...] += _dot_nt(du_ref[...], w_ref[0])

        @pl.when(k == NCHIP - 1)
        def _():
            hh = h_ref[...]
            r = _rstd(hh)
            dhn = acc_s[...]
            dg_ref[...] += jnp.sum(dhn * hh * r, axis=0, keepdims=True)
            dh_ref[...] = dres_ref[...] + _rms_bwd(dhn, hh, r, g_ref[...])

    behind = [] if after is None else [after]
    hosting = _host(body, grid, 5, 2, n_x, _pair_copies if grads else _scatter_copies)
    res = pl.pallas_call(
        (lambda *refs: body(*refs[:5], *refs[6:])) if behind else hosting, name="inproj_dx", grid=grid,
        in_specs=[pl.BlockSpec((tm, SHW), lambda i, k: (i, k)),
                  pl.BlockSpec((1, D, SHW), lambda i, k: (k, 0, 0)),
                  pl.BlockSpec((tm, D), lambda i, k: (i, 0)),
                  pl.BlockSpec((1, D), lambda i, k: (0, 0)),
                  pl.BlockSpec((tm, D), lambda i, k: (i, 0))] + [HBM_SPEC] * n_x
        + [pl.BlockSpec(memory_space=pl.ANY)] * len(behind),
        out_specs=[pl.BlockSpec((tm, D), lambda i, k: (i, 0)), pl.BlockSpec((1, D), lambda i, k: (0, 0))]
        + [HBM_SPEC] * n_x,
        out_shape=[jax.ShapeDtypeStruct((T, D), F32), jax.ShapeDtypeStruct((1, D), F32)] + landing,
        scratch_shapes=[pltpu.VMEM((tm, D), F32)] + _hosted_sems(n_x),
        compiler_params=_cp(("arbitrary", "arbitrary"), VMEM_BIG),
    )(du, w, h, g, dres, *sent, *behind)
    return res[0], res[1], list(res[2:])


def _sum_pair(core, grads, gots):
    n = len(grads)

    def body(c_ref, *refs):
        for a in range(n):
            refs[2 * n + a][...] = (refs[a][...] + refs[n + a][...]).astype(BF16)

    mine = [pl.BlockSpec((1,) + s.shape[1:], lambda k, c: (k, c[0], 0)) for s in gots]
    same = [pl.BlockSpec((1,) + s.shape[1:], lambda k, c: (k, 0, 0)) for s in gots]
    return pl.pallas_call(
        body, name="sum_pair",
        grid_spec=pltpu.PrefetchScalarGridSpec(
            num_scalar_prefetch=1, grid=(NCHIP,), in_specs=mine + same, out_specs=same),
        out_shape=[jax.ShapeDtypeStruct(s.shape, BF16) for s in gots],
        compiler_params=_cp(("arbitrary",), VMEM_BIG),
    )(core, *grads, *gots)


def _sum_chips_share(chip, partials, gots):
    flat_p = [p for layer in partials for p in layer]
    flat_g = [g for layer in gots for g in layer]
    n, per_layer = len(flat_p), len(partials[0])

    def body(c_ref, *refs):
        full, sums = refs[2 * n:2 * n + per_layer], refs[2 * n + per_layer:3 * n + per_layer]
        send_sems, recv_sems, local_sems = refs[3 * n + per_layer:]
        x, y, c = _place()
        copies = []
        for i in range(n):
            acc = refs[i][0].astype(F32)
            for j in range(NCHIP - 1):
                acc = acc + refs[n + i][j].astype(F32)
            sums[i][...] = acc
            half = flat_p[i].shape[1]
            rows = full[i % per_layer].at[i // per_layer, pl.ds(c * half, half)]
            copies.append(pltpu.make_async_copy(sums[i], rows, local_sems.at[i]))
            copies.append(pltpu.make_async_remote_copy(
                src_ref=sums[i], dst_ref=rows, send_sem=send_sems.at[i], recv_sem=recv_sems.at[i],
                device_id=(x, y, 1 - c), device_id_type=MESH))
        for cp in copies:
            cp.start()
        for cp in copies:
            cp.wait()

    return pl.pallas_call(
        body, name="sum_chips_share",
        grid_spec=pltpu.PrefetchScalarGridSpec(
            num_scalar_prefetch=1, grid=(1,),
            in_specs=[pl.BlockSpec((1,) + s.shape[1:], lambda i, c: (c[0], 0, 0)) for s in flat_p]
            + [pl.BlockSpec(s.shape, lambda i, c: (0, 0, 0)) for s in flat_g],
            out_specs=[HBM_SPEC] * per_layer,
            scratch_shapes=[pltpu.VMEM(s.shape[1:], F32) for s in flat_p]
            + [pltpu.SemaphoreType.DMA((n,)), pltpu.SemaphoreType.DMA((n,)), pltpu.SemaphoreType.DMA((n,))]),
        out_shape=[jax.ShapeDtypeStruct((len(partials), 2 * s.shape[1], s.shape[2]), F32) for s in partials[0]],
        compiler_params=_cp(("arbitrary",), VMEM_BIG),
    )(chip, *flat_p, *flat_g)


def _adam_math(w, g, m, v):
    nm = ADAM_B1 * m + (1.0 - ADAM_B1) * g
    nv = ADAM_B2 * v + (1.0 - ADAM_B2) * (g * g)
    m_hat = nm / (1.0 - ADAM_B1 ** ADAM_STEP)
    v_hat = nv / (1.0 - ADAM_B2 ** ADAM_STEP)
    return -ADAM_LR * (m_hat / (jnp.sqrt(v_hat) + ADAM_EPS) + ADAM_WD * w), nm, nv


def _adamw(w, g, m, v, rows):
    R, C = w.shape

    def body(w_ref, g_ref, m_ref, v_ref, g_out, d_ref, nm_ref, nv_ref):
        g = g_ref[...]
        g_out[...] = g
        d_ref[...], nm_ref[...], nv_ref[...] = _adam_math(w_ref[...], g, m_ref[...], v_ref[...])

    spec = pl.BlockSpec((rows, C), lambda i: (i, 0))
    sh = jax.ShapeDtypeStruct((R, C), F32)
    return pl.pallas_call(
        body, name="adamw", grid=(R // rows,), in_specs=[spec] * 4, out_specs=[spec] * 4,
        out_shape=[sh, sh, sh, sh], compiler_params=_cp(("arbitrary",)),
    )(w, g, m, v)


def _adamw_taps(w, g, m, v):
    def body(w_ref, g_ref, m_ref, v_ref, g_out, d_ref, nm_ref, nv_ref):
        g = g_ref[:, pl.ds(0, CW), :]
        g_out[...] = g
        d_ref[...], nm_ref[...], nv_ref[...] = _adam_math(w_ref[...], g, m_ref[...], v_ref[...])

    whole = lambda a: pl.BlockSpec(a.shape, lambda i: (0, 0, 0))
    return pl.pallas_call(
        body, name="adamw_taps", grid=(1,), in_specs=[whole(w), whole(g), whole(w), whole(w)],
        out_specs=[whole(w)] * 4, out_shape=[jax.ShapeDtypeStruct(w.shape, F32)] * 4,
    )(w, g, m, v)


def _small_adamw(tot, ws, ms, vs):
    n = len(ws)

    def body(*refs):
        tot_ref = refs[0]
        w_refs, m_refs, v_refs = refs[1:1 + n], refs[1 + n:1 + 2 * n], refs[1 + 2 * n:1 + 3 * n]
        outs = refs[1 + 3 * n:]
        for i in range(n):
            rows, width = ws[i].shape
            g = tot_ref[pl.ds(SMALL_ROW[i], rows), pl.ds(0, width)]
            outs[4 * i][...] = g
            outs[4 * i + 1][...], outs[4 * i + 2][...], outs[4 * i + 3][...] = _adam_math(
                w_refs[i][...], g, m_refs[i][...], v_refs[i][...])
        outs[4 * n][...] = tot_ref[pl.ds(LOSS_ROW, 1), pl.ds(0, LANE)]

    vmem = pl.BlockSpec(memory_space=pltpu.VMEM)
    res = pl.pallas_call(
        body, name="small_adamw", in_specs=[vmem] * (1 + 3 * n), out_specs=[vmem] * (4 * n + 1),
        out_shape=[jax.ShapeDtypeStruct(w.shape, F32) for w in ws for _ in range(4)]
        + [jax.ShapeDtypeStruct((1, LANE), F32)],
    )(tot, *ws, *ms, *vs)
    return [res[4 * i:4 * i + 4] for i in range(n)], res[4 * n]


HBM_SPEC = pl.BlockSpec(memory_space=pltpu.HBM)


def _place():
    return lax.axis_index("x"), lax.axis_index("y"), lax.axis_index("c")


def _all_gather_split(shard):
    def body(in_ref, out_ref, send_sems, recv_sems):
        direct, relayed, passed = _gather_tree_copies([in_ref], [out_ref], send_sems, recv_sems)
        for cp in direct:
            cp.start()
        for i in range(2):
            direct[i].wait_recv()
            relayed[i].start()
            passed[i].start()
        for cp in relayed:
            cp.wait_recv()
        passed[2].start()
        for cp in passed:
            cp.wait_recv()
        for cp in direct + relayed + passed:
            cp.wait_send()

    return pl.pallas_call(
        body, name="all_gather_split", in_specs=[HBM_SPEC], out_specs=HBM_SPEC,
        out_shape=jax.ShapeDtypeStruct((NCHIP,) + shard.shape, shard.dtype),
        scratch_shapes=_hosted_sems(1, GATHER_SEMS),
    )(shard)


SEM_SPEC = pl.BlockSpec(memory_space=pltpu.SEMAPHORE)
ORDERED_EFFECT = pltpu.CompilerParams(has_side_effects=pltpu.SideEffectType.DATAFLOW_SIDE_EFFECTING)


def _scatter_start(partial):
    land = pltpu.with_memory_space_constraint(
        lax.empty((NCHIP - 1,) + partial.shape[1:], partial.dtype), pltpu.HBM)

    def body(p_ref, land_ref, send_sems, recv_sems, p_thru, land_thru, token):
        for cp in _scatter_copies([p_ref], [land_ref], send_sems, recv_sems):
            cp.start()
        token[...] = jnp.zeros_like(token)

    return pl.pallas_call(
        body, name="scatter_start",
        out_shape=(pltpu.SemaphoreType.DMA((NCHIP - 1,)), pltpu.SemaphoreType.DMA((NCHIP - 1,)),
                   pltpu.HBM(partial.shape, partial.dtype), pltpu.HBM(land.shape, land.dtype),
                   jax.ShapeDtypeStruct((8, LANE), F32)),
        in_specs=(HBM_SPEC, HBM_SPEC),
        out_specs=(SEM_SPEC, SEM_SPEC, HBM_SPEC, HBM_SPEC, pl.BlockSpec(memory_space=pltpu.VMEM)),
        input_output_aliases={0: 2, 1: 3}, compiler_params=ORDERED_EFFECT,
    )(pltpu.with_memory_space_constraint(partial, pltpu.HBM), land)


def _scatter_wait(send_sems, recv_sems, p_thru, land_thru, after):
    def body(p_ref, land_ref, send_sems, recv_sems, after_ref, p_dead, got_ref):
        for cp in _scatter_copies([p_ref], [land_ref], send_sems, recv_sems):
            cp.wait_send()
            cp.wait_recv()

    return pl.pallas_call(
        body, name="scatter_wait",
        out_shape=(pltpu.HBM(p_thru.shape, p_thru.dtype), pltpu.HBM(land_thru.shape, land_thru.dtype)),
        in_specs=(HBM_SPEC, HBM_SPEC, SEM_SPEC, SEM_SPEC, pl.BlockSpec(memory_space=pl.ANY)),
        out_specs=(HBM_SPEC, HBM_SPEC), input_output_aliases={0: 0, 1: 1}, compiler_params=ORDERED_EFFECT,
    )(p_thru, land_thru, send_sems, recv_sems, after)


def _pair_exchange(grads):
    n = len(grads)

    def body(*refs):
        copies = _pair_copies(refs[:n], refs[n:2 * n], refs[2 * n], refs[2 * n + 1])
        for cp in copies:
            cp.start()
        for cp in copies:
            cp.wait()

    return pl.pallas_call(
        body, name="pair_exchange", in_specs=[HBM_SPEC] * n, out_specs=[HBM_SPEC] * n,
        out_shape=[jax.ShapeDtypeStruct((NCHIP, g.shape[1] // 2, g.shape[2]), F32) for g in grads],
        scratch_shapes=[pltpu.SemaphoreType.DMA((n,)), pltpu.SemaphoreType.DMA((n,))],
    )(*grads)


def _small_allreduce(rows, loss_blk, per_head=()):
    n = len(rows)

    def body(*refs):
        loss_ref, o_ref, pk, slots, send_sems, recv_sems = refs[n:]
        pk[...] = jnp.zeros_like(pk)
        for i in range(n):
            row = refs[i][...]
            if i in per_head:
                row = functools.reduce(jnp.add, [row[:, h * DH:(h + 1) * DH] for h in range(AD // DH)])
            pk[pl.ds(i, 1), pl.ds(0, row.shape[1])] = row
        pk[pl.ds(LOSS_ROW, 1), pl.ds(0, LANE)] = loss_ref[pl.ds(0, 1), :]
        x, y, c = _place()
        me = 4 * x + 2 * y + c
        slots[me] = pk[...]
        copies = []
        for r in range(1, 8):
            rx, ry, rc = (r >> 2) & 1, (r >> 1) & 1, r & 1
            peer = (x + rx - 2 * x * rx, y + ry - 2 * y * ry, c + rc - 2 * c * rc)
            cp = pltpu.make_async_remote_copy(
                src_ref=pk, dst_ref=slots.at[me], send_sem=send_sems.at[r - 1], recv_sem=recv_sems.at[r - 1],
                device_id=peer, device_id_type=MESH)
            cp.start()
            copies.append(cp)
        for cp in copies:
            cp.wait()
        acc = slots[0]
        for j in range(1, 8):
            acc = acc + slots[j]
        o_ref[...] = acc

    vmem = pl.BlockSpec(memory_space=pltpu.VMEM)
    return pl.pallas_call(
        body, name="small_allreduce", in_specs=[vmem] * (n + 1), out_specs=vmem,
        out_shape=jax.ShapeDtypeStruct((SMALL_PK, D), F32),
        scratch_shapes=[pltpu.VMEM((SMALL_PK, D), F32), pltpu.VMEM((8, SMALL_PK, D), F32),
                        pltpu.SemaphoreType.DMA((7,)), pltpu.SemaphoreType.DMA((7,))],
    )(*rows, loss_blk)


def _seg_matrix():
    i = lax.broadcasted_iota(jnp.int32, (AD, AD), 0) // DH
    j = lax.broadcasted_iota(jnp.int32, (AD, AD), 1) // DH
    return (i == j).astype(BF16)


TAIL = ("w_out", "w_ple_gate", "w_ple", "w_pw", "dw_w")


def _local_step(x, p, tgt, sm, shards, chip, ci):
    seg = _seg_matrix()
    core = jnp.reshape(ci, (1,)).astype(jnp.int32)
    chip_idx = jnp.reshape(chip, (1,)).astype(jnp.int32)
    own = lambda g, s: lax.dynamic_update_index_in_dim(g, s, chip, 0)
    w_in_next = own(_all_gather_split(shards[0]["w_in"]), shards[0]["w_in"])
    h = x
    saved = []
    for l in range(DEPTH):
        w_in = w_in_next
        row = lambda name: sm[name][l:l + 1]
        u, hn = _rms_inproj(h, row("norm_g"), w_in)
        todo = [shards[l][k] for k in TAIL] + ([shards[l + 1]["w_in"]] if l + 1 < DEPTH else [])
        o, ya, tot, got = _attn_fwd(u, jnp.tile(row("attn_out_g"), (1, AD // DH)), todo)
        got = [own(g, s) for g, s in zip(got, todo)]
        w_out = got[0].reshape(D, D)
        w_gate = got[1].reshape(D, D)
        w_ple = got[2]
        w_pw = got[3].reshape(CD, CD)
        dw = got[4]
        if l + 1 < DEPTH:
            w_in_next = got[5]
        c1 = _glu_conv(u, dw, row("dw_b"))
        c3, yc, h1, gate, pe, h2, *at_end = _layer_tail(
            c1, u, ya, h, p[l], row("conv_ln_g"), row("conv_ln_b"), w_pw, row("conv_out_g"), w_out,
            row("ple_norm_g"), w_gate, w_ple, head=(tgt, sm["final_g"]) if l == DEPTH - 1 else None)
        saved.append(dict(h=h, u=u, hn=hn, o=o, ya=ya, tot=tot, c1=c1, c3=c3, yc=yc, h1=h1, gate=gate, pe=pe,
                          w_in=w_in, w_out=w_out, w_gate=w_gate, w_pw=w_pw, dw=dw))
        h = h2
    dh, (loss_blk, dfg) = h, at_end
    small = [None] * DEPTH
    pending, partials, arrived = [], {}, {}
    pair_sum = lambda grads: _sum_pair(core, grads, _pair_exchange(grads))
    for l in reversed(range(DEPTH)):
        s = saved[l]
        row = lambda name: sm[name][l:l + 1]
        dh1, dy, dwg, dwp, dwo, dpg = _ple_out_bwd(
            dh, s["h1"], s["gate"], s["pe"], p[l], s["ya"], s["yc"], row("ple_norm_g"), s["w_gate"], s["w_out"])
        ag_t = jnp.tile(row("attn_out_g"), (1, AD // DH))
        do, dga, dgc, dc1, dwpw, dag, dcg, dlg, dlb = _branch_bwd(
            dy, s["o"], s["u"], s["c1"], s["c3"], ag_t, row("conv_ln_g"), row("conv_ln_b"), s["w_pw"],
            row("conv_out_g"), seg)
        tail = [dwo.reshape(NCHIP, 256, D), dwg.reshape(NCHIP, 256, D), dwp, dwpw.reshape(NCHIP, 128, CD)]
        dcv, dcgate, ddw, ddb, halves = _conv_bwd(dc1, s["u"], s["dw"], tail if l == 0 else ())
        tail.append(ddw)
        if l == 0:
            partials[(l, "tail")] = _sum_pair(core, tail, halves + list(_pair_exchange([ddw])))
            pending.append((l, "tail"))
        send = [t for key in pending for t in partials[key]]
        dq, dk, dv, got = _attn_bwd(s["u"], do, s["tot"], send)
        for key in pending:
            arrived[key], got = got[:len(partials[key])], got[len(partials[key]):]
        pending = []
        du = jnp.concatenate([dq, dk, dv, dga, dcv, dcgate, dgc], axis=1)
        dwin = _inproj_dw(s["hn"], du)
        if l == 0:
            in_flight = _scatter_start(pair_sum([dwin])[0])
            dh, dng, _ = _inproj_dx(du, s["w_in"], s["h"], row("norm_g"), dh1, after=in_flight[4])
            small[l] = dict(norm_g=dng, attn_out_g=dag, dw_b=ddb, conv_ln_g=dlg, conv_ln_b=dlb,
                            conv_out_g=dcg, ple_norm_g=dpg)
            per_head = tuple(DEPTH * SMALL2.index("attn_out_g") + j for j in range(DEPTH))
            small_tot = _small_allreduce([small[j][k] for k in SMALL2 for j in range(DEPTH)] + [dfg], loss_blk,
                                         per_head)
            landed = _scatter_wait(*in_flight[:4], small_tot)
            partials[(l, "w_in")], arrived[(l, "w_in")] = [landed[0]], [landed[1]]
        else:
            dh, dng, halves = _inproj_dx(du, s["w_in"], s["h"], row("norm_g"), dh1, grads=tail + [dwin])
            tail_p = _sum_pair(core, tail + [dwin], halves)
            partials[(l, "tail")], partials[(l, "w_in")] = tail_p[:-1], tail_p[-1:]
            pending = [(l, "tail"), (l, "w_in")]
            small[l] = dict(norm_g=dng, attn_out_g=dag, dw_b=ddb, conv_ln_g=dlg, conv_ln_b=dlb,
                            conv_out_g=dcg, ple_norm_g=dpg)
    both = lambda d: [list(d[(l, "w_in")]) + list(d[(l, "tail")]) for l in range(DEPTH)]
    big = dict(zip(BIG, _sum_chips_share(chip_idx, both(partials), both(arrived))))
    return dh, big, small_tot


BIG = ("w_in", "w_out", "w_ple_gate", "w_ple", "w_pw", "dw_w")
SMALL2 = ("norm_g", "ple_norm_g", "dw_b", "conv_ln_g", "conv_ln_b", "conv_out_g", "attn_out_g")
SMALL_ROW = (0, 2, 4, 6, 8, 10, 12, 14)


def kernel(x, p, norm_g, w_in, attn_out_g, dw_w, dw_b, conv_ln_g, conv_ln_b, w_pw, conv_out_g, w_out, ple_norm_g, w_ple_gate, w_ple, final_g, loss_target, m_norm_g, m_w_in, m_attn_out_g, m_dw_w, m_dw_b, m_conv_ln_g, m_conv_ln_b, m_w_pw, m_conv_out_g, m_w_out, m_ple_norm_g, m_w_ple_gate, m_w_ple, m_final_g, v_norm_g, v_w_in, v_attn_out_g, v_dw_w, v_dw_b, v_conv_ln_g, v_conv_ln_b, v_w_pw, v_conv_out_g, v_w_out, v_ple_norm_g, v_w_ple_gate, v_w_ple, v_final_g):
    W = dict(norm_g=norm_g, w_in=w_in, attn_out_g=attn_out_g, dw_w=dw_w, dw_b=dw_b, conv_ln_g=conv_ln_g,
             conv_ln_b=conv_ln_b, w_pw=w_pw, conv_out_g=conv_out_g, w_out=w_out, ple_norm_g=ple_norm_g,
             w_ple_gate=w_ple_gate, w_ple=w_ple, final_g=final_g)
    M = dict(norm_g=m_norm_g, w_in=m_w_in, attn_out_g=m_attn_out_g, dw_w=m_dw_w, dw_b=m_dw_b,
             conv_ln_g=m_conv_ln_g, conv_ln_b=m_conv_ln_b, w_pw=m_w_pw, conv_out_g=m_conv_out_g, w_out=m_w_out,
             ple_norm_g=m_ple_norm_g, w_ple_gate=m_w_ple_gate, w_ple=m_w_ple, final_g=m_final_g)
    V = dict(norm_g=v_norm_g, w_in=v_w_in, attn_out_g=v_attn_out_g, dw_w=v_dw_w, dw_b=v_dw_b,
             conv_ln_g=v_conv_ln_g, conv_ln_b=v_conv_ln_b, w_pw=v_w_pw, conv_out_g=v_conv_out_g, w_out=v_w_out,
             ple_norm_g=v_ple_norm_g, w_ple_gate=v_w_ple_gate, w_ple=v_w_ple, final_g=v_final_g)
    order = ("norm_g", "w_in", "attn_out_g", "dw_w", "dw_b", "conv_ln_g", "conv_ln_b", "w_pw", "conv_out_g",
             "w_out", "ple_norm_g", "w_ple_gate", "w_ple", "final_g")

    pad_taps = lambda a: jnp.pad(a, ((0, 0), (0, CWP - CW), (0, 0)))
    cast = dict(w_in=w_in.astype(BF16), w_out=w_out.astype(BF16), w_ple_gate=w_ple_gate.astype(BF16),
                w_ple=w_ple.astype(BF16), w_pw=w_pw.astype(BF16), dw_w=pad_taps(dw_w))
    shards = [{k: v[l] for k, v in cast.items()} for l in range(DEPTH)]
    xi, yi, ci = lax.axis_index("x"), lax.axis_index("y"), lax.axis_index("c")
    chip = 2 * xi + yi

    sm = {k: W[k] for k in SMALL2}
    sm["final_g"] = final_g.reshape(1, D)
    grad_x, big, small_tot = _local_step(x[0], p[:, 0], loss_target[0], sm, shards, chip, ci)
    g_big = {name: big[name].reshape(cast[name].shape) for name in BIG}

    small_names = SMALL2 + ("final_g",)
    as_rows = lambda t: t.reshape(1, D) if t.ndim == 1 else t
    results, loss_row = _small_adamw(
        small_tot, [as_rows(W[k]) for k in small_names],
        [as_rows(M[k]) for k in small_names], [as_rows(V[k]) for k in small_names])
    loss = loss_row[0, 0]

    grads, deltas, new_m, new_v = {}, {}, {}, {}
    for name in BIG:
        if name == "dw_w":
            grads[name], deltas[name], new_m[name], new_v[name] = _adamw_taps(W[name], g_big[name], M[name], V[name])
            continue
        cols = W[name].shape[-1]
        rows_total = W[name].size // cols
        flat = lambda a: a.reshape(rows_total, cols)
        four = _adamw(flat(W[name]), flat(g_big[name]), flat(M[name]), flat(V[name]), min(rows_total, 256))
        grads[name], deltas[name], new_m[name], new_v[name] = (t.reshape(W[name].shape) for t in four)
    for k, four in zip(small_names, results):
        grads[k], deltas[k], new_m[k], new_v[k] = (t.reshape(W[k].shape) for t in four)

    return (loss, grad_x[None], *[grads[n] for n in order], *[deltas[n] for n in order],
            *[new_m[n] for n in order], *[new_v[n] for n in order])
```

```python
import functools

import jax
import jax.numpy as jnp
from jax import lax
from jax.experimental import pallas as pl
from jax.experimental.pallas import tpu as pltpu

F32 = jnp.float32
BF16 = jnp.bfloat16

T = 2048
D = 1024
DIN = 3584
NCHIP = 4
SHW = DIN // NCHIP
AD = 512
CD = 512
DH = 64
CW = 31
CWP = 32
PLE = 256
DEPTH = 2
EPS = 1e-6
AQ = 256
HG = 4
GW = HG * DH
SR = HG * AQ
NG = AD // GW
LANE = 128
TM = 1024
TR = 256

ADAM_LR = 0.001
ADAM_B1 = 0.9
ADAM_B2 = 0.999
ADAM_EPS = 1e-08
ADAM_WD = 0.01
ADAM_STEP = 10

SMALL_PK = 16
LOSS_ROW = 15

VMEM_BIG = 56 * 1024 * 1024
MESH = pl.DeviceIdType.MESH


def _cp(sem=None, vmem=None):
    kw = {}
    if sem is not None:
        kw["dimension_semantics"] = sem
    if vmem is not None:
        kw["vmem_limit_bytes"] = vmem
    return pltpu.CompilerParams(**kw)


def _dot(a, b):
    return jnp.dot(a, b, preferred_element_type=F32)


def _dot_nt(a, b):
    return lax.dot_general(a, b, (((1,), (1,)), ((), ())), preferred_element_type=F32)


def _dot_tn(a, b):
    return lax.dot_general(a, b, (((0,), (0,)), ((), ())), preferred_element_type=F32)


def _dot2(x, m):
    hi = x.astype(BF16)
    lo = (x - hi.astype(F32)).astype(BF16)
    return _dot(hi, m) + _dot(lo, m)


def _sig(x):
    return 1.0 / (1.0 + jnp.exp(-x))


def _softplus(z):
    return jnp.maximum(z, 0.0) + jnp.log(1.0 + jnp.exp(-jnp.abs(z)))


def _rstd(x):
    return lax.rsqrt(jnp.mean(x * x, axis=-1, keepdims=True) + EPS)


def _rms_bwd(dy, x, r, g):
    dn = dy * g
    return r * dn - x * (r * r * r) * jnp.mean(dn * x, axis=-1, keepdims=True)


def _rms_inproj(h, g, w):
    tm = min(TM, T)

    def body(h_ref, g_ref, w_ref, u_ref, hn_ref, hn_s):
        @pl.when(pl.program_id(1) == 0)
        def _():
            hh = h_ref[...]
            hn = (hh * _rstd(hh) * g_ref[...]).astype(BF16)
            hn_s[...] = hn
            hn_ref[...] = hn
        u_ref[...] = _dot(hn_s[...], w_ref[0])

    return pl.pallas_call(
        body, name="rms_inproj", grid=(T // tm, NCHIP),
        in_specs=[pl.BlockSpec((tm, D), lambda i, k: (i, 0)),
                  pl.BlockSpec((1, D), lambda i, k: (0, 0)),
                  pl.BlockSpec((1, D, SHW), lambda i, k: (k, 0, 0))],
        out_specs=[pl.BlockSpec((tm, SHW), lambda i, k: (i, k)),
                   pl.BlockSpec((tm, D), lambda i, k: (i, 0))],
        out_shape=[jax.ShapeDtypeStruct((T, DIN), F32), jax.ShapeDtypeStruct((T, D), BF16)],
        scratch_shapes=[pltpu.VMEM((tm, D), BF16)],
        compiler_params=_cp(("arbitrary", "arbitrary"), VMEM_BIG),
    )(h, g, w)


def _attn_tiles():
    row = lax.broadcasted_iota(jnp.int32, (SR, AQ), 0) & (AQ - 1)
    col = lax.broadcasted_iota(jnp.int32, (SR, AQ), 1)
    tr = lax.broadcasted_iota(jnp.int32, (AQ, AQ), 0)
    tc = lax.broadcasted_iota(jnp.int32, (AQ, AQ), 1)
    lane_head = lax.broadcasted_iota(jnp.int32, (1, GW), 1) // DH
    return col < row, tr, tc, [lane_head == h for h in range(HG)]


def _stack_heads(t, heads):
    return jnp.concatenate([jnp.where(m, t, 0.0) for m in heads], axis=0)


def _unstack_heads(t, heads):
    out = t[:AQ]
    for h in range(1, HG):
        out = jnp.where(heads[h], t[h * AQ:(h + 1) * AQ], out)
    return out


def _tri_sum(x, tri):
    hi = x.astype(BF16)
    lo = (x - hi.astype(F32)).astype(BF16)
    both = _dot(jnp.concatenate([hi, lo], axis=0), tri)
    return both[:SR] + both[SR:]


def _scatter_copies(ps, gots, send_sems, recv_sems):
    x, y, c = _place()
    peers = [(1 - x, y), (x, 1 - y), (1 - x, 1 - y)]
    return [pltpu.make_async_remote_copy(
        src_ref=ps[a].at[2 * px + py], dst_ref=gots[a].at[r], send_sem=send_sems.at[3 * a + r],
        recv_sem=recv_sems.at[3 * a + r], device_id=(px, py, c), device_id_type=MESH)
        for a in range(len(ps)) for r, (px, py) in enumerate(peers)]


GATHER_SEMS = 8


def _gather_tree_copies(ins, outs, send_sems, recv_sems):
    x, y, c = _place()
    me, xn, yn, dg = 2 * x + y, 2 * (1 - x) + y, 2 * x + (1 - y), 2 * (1 - x) + (1 - y)
    to_x, to_y, sibling = (1 - x, y, c), (x, 1 - y, c), (x, y, 1 - c)
    direct, relayed, passed = [], [], []
    for a in range(len(ins)):
        half = ins[a].shape[0] // 2
        mine = pl.ds(c * half, half)
        first, second = pl.ds(c * half, half // 2), pl.ds(c * half + half // 2, half // 2)

        def copy(i, src, dst, to, k=GATHER_SEMS * a):
            return pltpu.make_async_remote_copy(src_ref=src, dst_ref=dst, send_sem=send_sems.at[k + i],
                                                recv_sem=recv_sems.at[k + i], device_id=to, device_id_type=MESH)

        own, slot = ins[a].at[mine], outs[a].at[me, mine]
        direct += [copy(0, own, slot, to_x), copy(1, own, slot, to_y), copy(7, ins[a], outs[a].at[me], sibling)]
        relayed += [copy(2, outs[a].at[xn, first], outs[a].at[xn, first], to_y),
                    copy(3, outs[a].at[yn, second], outs[a].at[yn, second], to_x)]
        passed += [copy(4 + i, outs[a].at[j, mine], outs[a].at[j, mine], sibling) for i, j in enumerate((xn, yn, dg))]
    return direct, relayed, passed


def _pair_copies(ins, outs, send_sems, recv_sems):
    x, y, c = _place()
    copies = []
    for a in range(len(ins)):
        half = ins[a].shape[1] // 2
        copies.append(pltpu.make_async_remote_copy(
            src_ref=ins[a].at[:, pl.ds((1 - c) * half, half), :], dst_ref=outs[a], send_sem=send_sems.at[a],
            recv_sem=recv_sems.at[a], device_id=(x, y, 1 - c), device_id_type=MESH))
    return copies


def _host(body, grid, n_in, n_out, n_x, make_copies, mids=()):
    if not n_x:
        return body

    def hosting(*refs):
        a, b = n_in + n_x, n_in + 2 * n_x + n_out
        copies = make_copies(refs[n_in:a], refs[a + n_out:b], refs[-2], refs[-1])
        stages = copies if isinstance(copies, tuple) else (copies,)
        ids = [pl.program_id(d) for d in range(len(grid))]
        at = lambda step: functools.reduce(jnp.logical_and, [i == s for i, s in zip(ids, step)])

        @pl.when(at([0] * len(grid)))
        def _():
            for cp in stages[0]:
                cp.start()

        for before, after, step in zip(stages, stages[1:], mids):
            @pl.when(at(step))
            def _(before=before, after=after):
                for cp in before:
                    cp.wait_recv()
                for cp in after:
                    cp.start()

        body(*refs[:n_in], *refs[a:a + n_out], *refs[b:-2])

        @pl.when(at([g - 1 for g in grid]))
        def _():
            for cp in stages[-1]:
                cp.wait_recv()
            for stage in stages:
                for cp in stage:
                    cp.wait_send()

    return hosting


def _hosted_sems(n_x, per_array=3):
    n = per_array * n_x
    return [pltpu.SemaphoreType.DMA((n,)), pltpu.SemaphoreType.DMA((n,))] if n_x else []


RC = 256


def _chunk_causal(r):
    row = lax.broadcasted_iota(jnp.int32, (RC, AQ), 0) + (r * RC) % AQ
    return lax.broadcasted_iota(jnp.int32, (RC, AQ), 1) < row


def _attn_fwd(u, agw, shards=()):
    n = len(shards)
    grid = (T // AQ,)

    def body(q_ref, k_ref, v_ref, g_ref, ag_ref, o_ref, y_ref, tot_ref,
             kb_s, vb_s, qs_s, z_s, zs_s, lmb_s, suf_s, att_s, acc_s, run_s):
        qi = pl.program_id(0)

        @pl.when(qi == 0)
        def _():
            kb_s[...] = k_ref[...].astype(BF16)
            vb_s[...] = v_ref[...].astype(BF16)

        _, tr, tc, heads = _attn_tiles()
        upper = (tr > tc).astype(BF16)
        same_head = ((tr // DH) == (tc // DH)).astype(BF16)
        for g in range(NG):
            qs_s[g] = _stack_heads(q_ref[:, g * GW:(g + 1) * GW] * 0.125, heads).astype(BF16)
        acc_s[...] = jnp.zeros_like(acc_s)
        run_s[...] = jnp.zeros_like(run_s)

        def block(kb, masked):
            k0 = pl.multiple_of(kb * AQ, AQ)
            for g in range(NG):
                lanes = pl.ds(g * GW, GW)
                z_s[g] = _dot_nt(qs_s[g], kb_s[pl.ds(k0, AQ), lanes])
                for r in range(SR // RC):
                    rows = pl.ds(r * RC, RC)
                    z = z_s[g, rows, :]
                    zs = jnp.minimum(z, 0.0) - jnp.log(1.0 + jnp.exp(-jnp.abs(z)))
                    lm = zs - z
                    if masked:
                        lm = jnp.where(_chunk_causal(r), lm, 0.0)
                    run = run_s[g, rows, :]
                    zs_s[g, rows, :] = zs + run[:, 0:1]
                    hi = lm.astype(BF16)
                    lmb_s[g, rows, :] = hi
                    lmb_s[g, pl.ds(SR + r * RC, RC), :] = (lm - hi.astype(F32)).astype(BF16)
                    run_s[g, rows, :] = run + jnp.sum(lm, axis=1, keepdims=True)
                suf_s[g] = _dot(lmb_s[g], upper)
                for r in range(SR // RC):
                    rows = pl.ds(r * RC, RC)
                    att = jnp.exp(zs_s[g, rows, :] + suf_s[g, rows, :] + suf_s[g, pl.ds(SR + r * RC, RC), :])
                    if masked:
                        att = jnp.where(_chunk_causal(r), att, 0.0)
                    att_s[g, rows, :] = att.astype(BF16)
                acc_s[g] += _dot(att_s[g], vb_s[pl.ds(k0, AQ), lanes])

        block(qi, True)

        def step(i, c):
            block(qi - 1 - i, False)
            return c

        lax.fori_loop(0, qi, step, 0)
        gate = g_ref[...]
        agv = ag_ref[...]
        for g in range(NG):
            lanes = slice(g * GW, (g + 1) * GW)
            o = _unstack_heads(acc_s[g], heads)
            osq = o * o
            ms = _dot2(osq, same_head)
            gg = gate[:, lanes]
            o_ref[:, lanes] = o
            y_ref[:, lanes] = (o * lax.rsqrt(ms * (1.0 / DH) + EPS) * agv[:, lanes] * (gg * _sig(gg))).astype(BF16)
            tot_ref[g] = _unstack_heads(jnp.broadcast_to(run_s[g][:, 0:1], (SR, GW)), heads)

    tile = lambda dt, rows=SR: pltpu.VMEM((NG, rows, AQ), dt)
    scratch = [pltpu.VMEM((T, AD), BF16), pltpu.VMEM((T, AD), BF16), pltpu.VMEM((NG, SR, GW), BF16),
               tile(F32), tile(F32), tile(BF16, 2 * SR), tile(F32, 2 * SR), tile(BF16),
               pltpu.VMEM((NG, SR, GW), F32), pltpu.VMEM((NG, SR, LANE), F32)]
    col = lambda j: pl.BlockSpec((AQ, AD), lambda qi: (qi, j))
    res = pl.pallas_call(
        _host(body, grid, 5, 3, n, _gather_tree_copies, mids=((grid[0] * 5 // 8,), (grid[0] * 7 // 8,))), name="attn_fwd", grid=grid,
        in_specs=[col(0), pl.BlockSpec((T, AD), lambda qi: (0, 1)), pl.BlockSpec((T, AD), lambda qi: (0, 2)),
                  col(3), pl.BlockSpec((1, AD), lambda qi: (0, 0))] + [HBM_SPEC] * n,
        out_specs=[col(0), col(0), pl.BlockSpec((NG, AQ, GW), lambda qi: (0, qi, 0))] + [HBM_SPEC] * n,
        out_shape=[jax.ShapeDtypeStruct((T, AD), F32), jax.ShapeDtypeStruct((T, AD), BF16),
                   jax.ShapeDtypeStruct((NG, T, GW), F32)]
        + [jax.ShapeDtypeStruct((NCHIP,) + s.shape, s.dtype) for s in shards],
        scratch_shapes=scratch + _hosted_sems(n, GATHER_SEMS),
        compiler_params=_cp(("arbitrary",), VMEM_BIG),
    )(u, u, u, u, agw, *shards)
    return res[0], res[1], res[2], list(res[3:])


def _glu_conv(u, dw, db):
    tr = 256

    def body(cv_ref, cg_ref, w_ref, b_ref, c1_ref, pad_s):
        pad_s[pl.ds(0, CWP), :] = jnp.zeros((CWP, LANE), F32)
        pad_s[pl.ds(CWP, T), :] = cv_ref[...] * _sig(cg_ref[...])
        wv = w_ref[0]
        bias = b_ref[...]

        def tile(i, carry):
            r0 = pl.multiple_of(i * tr, tr)
            acc = jnp.zeros((tr, LANE), F32) + bias
            for w in range(CW):
                acc = acc + pad_s[pl.ds(r0 + (CWP - CW + 1) + w, tr), :] * wv[w:w + 1, :]
            c1_ref[pl.ds(r0, tr), :] = acc
            return carry

        lax.fori_loop(0, T // tr, tile, 0)

    return pl.pallas_call(
        body, name="glu_conv", grid=(CD // LANE,),
        in_specs=[pl.BlockSpec((T, LANE), lambda cb: (0, 16 + cb)),
                  pl.BlockSpec((T, LANE), lambda cb: (0, 20 + cb)),
                  pl.BlockSpec((1, CWP, LANE), lambda cb: (cb, 0, 0)),
                  pl.BlockSpec((1, LANE), lambda cb: (0, cb))],
        out_specs=pl.BlockSpec((T, LANE), lambda cb: (0, cb)),
        out_shape=jax.ShapeDtypeStruct((T, CD), F32),
        scratch_shapes=[pltpu.VMEM((T + CWP, LANE), F32)],
        compiler_params=_cp(("arbitrary",)),
    )(u, u, dw, db)


def _ln_silu(c1, lg, lb):
    mu = jnp.mean(c1, axis=-1, keepdims=True)
    xc = c1 - mu
    rs = lax.rsqrt(jnp.mean(xc * xc, axis=-1, keepdims=True) + EPS)
    xh = xc * rs
    ln = xh * lg + lb
    s = _sig(ln)
    return xh, rs, ln, s


def _layer_tail(c1, u, ya, h, p, lg, lb, wpw, cg, wout, pg, wgate, wple, head=None):
    tm = min(TR, T)

    def body(c1_ref, gc_ref, ya_ref, h_ref, p_ref, lg_ref, lb_ref, wpw_ref, cg_ref, wout_ref,
             pg_ref, wgate_ref, wple_ref, *rest):
        c3_ref, yc_ref, h1_ref, gate_ref, pe_ref, h2_ref = rest[-6 - 2 * bool(head):][:6]
        _, _, ln, s = _ln_silu(c1_ref[...], lg_ref[...], lb_ref[...])
        c2 = (ln * s).astype(BF16)
        c3 = _dot(c2, wpw_ref[...])
        gc = gc_ref[...]
        yc = (c3 * _rstd(c3) * cg_ref[...] * (gc * _sig(gc))).astype(BF16)
        c3_ref[...] = c3
        yc_ref[...] = yc
        y = _dot(ya_ref[...], wout_ref[pl.ds(0, AD), :]) + _dot(yc, wout_ref[pl.ds(AD, CD), :])
        h1 = h_ref[...] + y
        hn2 = (h1 * _rstd(h1) * pg_ref[...]).astype(BF16)
        gate = _sig(_dot(hn2, wgate_ref[...]))
        pb = p_ref[...].astype(BF16)
        pe = jnp.concatenate([_dot(pb, wple_ref[k]) for k in range(NCHIP)], axis=1)
        h1_ref[...] = h1
        gate_ref[...] = gate.astype(BF16)
        pe_ref[...] = pe.astype(BF16)
        h2 = h1 + pe * gate
        if not head:
            h2_ref[...] = h2
            return
        t_ref, fg_ref, loss_ref, dfg_ref = rest[0], rest[1], rest[-2], rest[-1]

        @pl.when(pl.program_id(0) == 0)
        def _():
            loss_ref[...] = jnp.zeros_like(loss_ref)
            dfg_ref[...] = jnp.zeros_like(dfg_ref)
        fg = fg_ref[...]
        r = _rstd(h2)
        e = h2 * r * fg - t_ref[...]
        loss_ref[...] += 0.5 * jnp.sum(jnp.mean(e * e, axis=-1, keepdims=True))
        dy = e * (1.0 / D)
        dfg_ref[...] += jnp.sum(dy * h2 * r, axis=0, keepdims=True)
        h2_ref[...] = _rms_bwd(dy, h2, r, fg)

    row = lambda w: pl.BlockSpec((tm, w), lambda i: (i, 0))
    full = lambda *s: pl.BlockSpec(s, lambda i: (0,) * len(s), pipeline_mode=pl.Buffered(1))
    extra = bool(head)
    return pl.pallas_call(
        body, name="layer_tail", grid=(T // tm,),
        in_specs=[row(CD), pl.BlockSpec((tm, CD), lambda i: (i, 6)), row(AD), row(D), row(PLE),
                  full(1, CD), full(1, CD), full(CD, CD), full(1, CD), full(D, D),
                  full(1, D), full(D, D), full(NCHIP, PLE, PLE)] + [row(D), full(1, D)] * extra,
        out_specs=[row(CD), row(CD), row(D), row(D), row(D), row(D)] + [full(8, LANE), full(1, D)] * extra,
        out_shape=[jax.ShapeDtypeStruct((T, CD), F32), jax.ShapeDtypeStruct((T, CD), BF16),
                   jax.ShapeDtypeStruct((T, D), F32), jax.ShapeDtypeStruct((T, D), BF16),
                   jax.ShapeDtypeStruct((T, D), BF16), jax.ShapeDtypeStruct((T, D), F32)]
        + [jax.ShapeDtypeStruct((8, LANE), F32), jax.ShapeDtypeStruct((1, D), F32)] * extra,
        compiler_params=_cp(("arbitrary",), VMEM_BIG),
    )(c1, u, ya, h, p, lg, lb, wpw, cg, wout, pg, wgate, wple, *(head or ()))


def _ple_out_bwd(dh2, h1, gate, pe, p, ya, yc, pg, wgate, wout):
    tm = min(TR, T)

    def body(dh2_ref, h1_ref, gate_ref, pe_ref, p_ref, ya_ref, yc_ref, pg_ref, wgate_ref, wout_ref,
             dh1_ref, dy_ref, dwg_ref, dwp_ref, dwo_ref, dpg_ref):
        @pl.when(pl.program_id(0) == 0)
        def _():
            dwg_ref[...] = jnp.zeros_like(dwg_ref)
            dwp_ref[...] = jnp.zeros_like(dwp_ref)
            dwo_ref[...] = jnp.zeros_like(dwo_ref)
            dpg_ref[...] = jnp.zeros_like(dpg_ref)
        dh2 = dh2_ref[...]
        h1 = h1_ref[...]
        gate = gate_ref[...].astype(F32)
        pg = pg_ref[...]
        dpe = (dh2 * gate).astype(BF16)
        dgp = (dh2 * pe_ref[...].astype(F32) * gate * (1.0 - gate)).astype(BF16)
        r = _rstd(h1)
        hn = h1 * r
        dwg_ref[...] += _dot_tn((hn * pg).astype(BF16), dgp)
        dhn2 = _dot_nt(dgp, wgate_ref[...])
        dpg_ref[...] += jnp.sum(dhn2 * hn, axis=0, keepdims=True)
        dh1 = dh2 + _rms_bwd(dhn2, h1, r, pg)
        pb = p_ref[...].astype(BF16)
        for k in range(NCHIP):
            dwp_ref[k] += _dot_tn(pb, dpe[:, k * PLE:(k + 1) * PLE])
        dh1b = dh1.astype(BF16)
        dy_ref[...] = _dot_nt(dh1b, wout_ref[...])
        dwo_ref[pl.ds(0, AD), :] += _dot_tn(ya_ref[...], dh1b)
        dwo_ref[pl.ds(AD, CD), :] += _dot_tn(yc_ref[...], dh1b)
        dh1_ref[...] = dh1

    row = lambda w: pl.BlockSpec((tm, w), lambda i: (i, 0))
    full = lambda *s: pl.BlockSpec(s, lambda i: (0,) * len(s), pipeline_mode=pl.Buffered(1))
    return pl.pallas_call(
        body, name="ple_out_bwd", grid=(T // tm,),
        in_specs=[row(D), row(D), row(D), row(D), row(PLE), row(AD), row(CD),
                  full(1, D), full(D, D), full(D, D)],
        out_specs=[row(D), row(D), full(D, D), full(NCHIP, PLE, PLE), full(D, D), full(1, D)],
        out_shape=[jax.ShapeDtypeStruct((T, D), F32), jax.ShapeDtypeStruct((T, D), F32),
                   jax.ShapeDtypeStruct((D, D), F32), jax.ShapeDtypeStruct((NCHIP, PLE, PLE), F32),
                   jax.ShapeDtypeStruct((D, D), F32), jax.ShapeDtypeStruct((1, D), F32)],
        compiler_params=_cp(("arbitrary",), VMEM_BIG),
    )(dh2, h1, gate, pe, p, ya, yc, pg, wgate, wout)


def _branch_bwd(dy, o, u, c1, c3, ag, lg, lb, wpw, cg, seg):
    tm = min(TR, T)

    def body(dya_ref, dyc_ref, o_ref, ga_ref, gc_ref, c1_ref, c3_ref, ag_ref, lg_ref, lb_ref, wpw_ref,
             cg_ref, seg_ref, do_ref, dga_ref, dgc_ref, dc1_ref, dwpw_ref, dag_ref, dcg_ref, dlg_ref, dlb_ref):
        @pl.when(pl.program_id(0) == 0)
        def _():
            for r_ in (dwpw_ref, dag_ref, dcg_ref, dlg_ref, dlb_ref):
                r_[...] = jnp.zeros_like(r_)
        dya = dya_ref[...]
        o = o_ref[...]
        ga = ga_ref[...]
        ag_v = ag_ref[...]
        seg_m = seg_ref[...]
        r = lax.rsqrt(_dot2(o * o, seg_m) * (1.0 / DH) + EPS)
        onr = o * r
        sg = _sig(ga)
        dga_ref[...] = (dya * (onr * ag_v) * (sg * (1.0 + ga * (1.0 - sg)))).astype(BF16)
        don = dya * (ga * sg)
        dag_ref[...] += jnp.sum(don * onr, axis=0, keepdims=True)
        dn = don * ag_v
        do_ref[...] = r * dn - o * (r * r * r) * (_dot2(dn * o, seg_m) * (1.0 / DH))
        dyc = dyc_ref[...]
        c3 = c3_ref[...]
        gc = gc_ref[...]
        cg_v = cg_ref[...]
        r3 = _rstd(c3)
        cn = c3 * r3
        sc = _sig(gc)
        dgc_ref[...] = (dyc * (cn * cg_v) * (sc * (1.0 + gc * (1.0 - sc)))).astype(BF16)
        dcn = dyc * (gc * sc)
        dcg_ref[...] += jnp.sum(dcn * cn, axis=0, keepdims=True)
        dc3 = _rms_bwd(dcn, c3, r3, cg_v).astype(BF16)
        lg_v = lg_ref[...]
        xh, rs, ln, s = _ln_silu(c1_ref[...], lg_v, lb_ref[...])
        c2 = (ln * s).astype(BF16)
        dwpw_ref[...] += _dot_tn(c2, dc3)
        dc2 = _dot_nt(dc3, wpw_ref[...])
        dln = dc2 * (s * (1.0 + ln * (1.0 - s)))
        dlb_ref[...] += jnp.sum(dln, axis=0, keepdims=True)
        dlg_ref[...] += jnp.sum(dln * xh, axis=0, keepdims=True)
        dxh = dln * lg_v
        dc1_ref[...] = rs * (dxh - jnp.mean(dxh, axis=-1, keepdims=True)
                             - xh * jnp.mean(dxh * xh, axis=-1, keepdims=True))

    half = lambda j: pl.BlockSpec((tm, 512), lambda i: (i, j))
    full = lambda *s: pl.BlockSpec(s, lambda i: (0,) * len(s), pipeline_mode=pl.Buffered(1))
    vec = jax.ShapeDtypeStruct((1, 512), F32)
    act = jax.ShapeDtypeStruct((T, 512), F32)
    return pl.pallas_call(
        body, name="branch_bwd", grid=(T // tm,),
        in_specs=[half(0), half(1), half(0), half(3), half(6), half(0), half(0),
                  full(1, AD), full(1, CD), full(1, CD), full(CD, CD), full(1, CD), full(AD, AD)],
        out_specs=[half(0), half(0), half(0), half(0), full(CD, CD), full(1, 512), full(1, 512),
                   full(1, 512), full(1, 512)],
        out_shape=[act, jax.ShapeDtypeStruct((T, 512), BF16), jax.ShapeDtypeStruct((T, 512), BF16), act,
                   jax.ShapeDtypeStruct((CD, CD), F32), vec, vec, vec, vec],
        compiler_params=_cp(("arbitrary",), VMEM_BIG),
    )(dy, dy, o, u, u, c1, c3, ag, lg, lb, wpw, cg, seg)


def _conv_bwd(dc1, u, dw, grads=()):
    tr = 64
    n_x = len(grads)
    grid = (CD // LANE,)
    off = CWP - CW + 1

    def body(d_ref, cv_ref, cg_ref, w_ref, dcv_ref, dcg_ref, ddw_ref, ddb_ref, padc_s, padd_s, acc_s):
        cv = cv_ref[...]
        sg = _sig(cg_ref[...])
        padc_s[pl.ds(0, CWP), :] = jnp.zeros((CWP, LANE), F32)
        padc_s[pl.ds(CWP, T), :] = cv * sg
        padd_s[pl.ds(0, T), :] = d_ref[...]
        padd_s[pl.ds(T, CWP), :] = jnp.zeros((CWP, LANE), F32)
        acc_s[...] = jnp.zeros_like(acc_s)
        wv = w_ref[0]

        def tile(i, carry):
            r0 = pl.multiple_of(i * tr, tr)
            dt = padd_s[pl.ds(r0, tr), :]
            dc0 = jnp.zeros((tr, LANE), F32)
            for w in range(CW):
                dc0 = dc0 + padd_s[pl.ds(r0 + (CW - 1) - w, tr), :] * wv[w:w + 1, :]
                prod = dt * padc_s[pl.ds(r0 + off + w, tr), :]
                acc_s[w] += jnp.sum(prod.reshape(tr // 8, 8, LANE), axis=0)
            cvt = cv_ref[pl.ds(r0, tr), :]
            sgt = _sig(cg_ref[pl.ds(r0, tr), :])
            dcv_ref[pl.ds(r0, tr), :] = (dc0 * sgt).astype(BF16)
            dcg_ref[pl.ds(r0, tr), :] = (dc0 * cvt * sgt * (1.0 - sgt)).astype(BF16)
            return carry

        lax.fori_loop(0, T // tr, tile, 0)
        ddw_ref[0] = jnp.sum(acc_s[...], axis=1)
        ddb_ref[...] = jnp.sum(d_ref[...], axis=0, keepdims=True)

    col = lambda j: pl.BlockSpec((T, LANE), lambda cb: (0, j + cb))
    res = pl.pallas_call(
        _host(body, grid, 4, 4, n_x, _pair_copies), name="conv_bwd", grid=grid,
        in_specs=[col(0), col(16), col(20), pl.BlockSpec((1, CWP, LANE), lambda cb: (cb, 0, 0))] + [HBM_SPEC] * n_x,
        out_specs=[col(0), col(0), pl.BlockSpec((1, CWP, LANE), lambda cb: (cb, 0, 0)),
                   pl.BlockSpec((1, LANE), lambda cb: (0, cb))] + [HBM_SPEC] * n_x,
        out_shape=[jax.ShapeDtypeStruct((T, CD), BF16), jax.ShapeDtypeStruct((T, CD), BF16),
                   jax.ShapeDtypeStruct((NCHIP, CWP, LANE), F32), jax.ShapeDtypeStruct((1, CD), F32)]
        + [jax.ShapeDtypeStruct((NCHIP, g.shape[1] // 2, g.shape[2]), F32) for g in grads],
        scratch_shapes=[pltpu.VMEM((T + CWP, LANE), F32), pltpu.VMEM((T + CWP, LANE), F32),
                        pltpu.VMEM((CWP, 8, LANE), F32)] + _hosted_sems(n_x),
        compiler_params=_cp(("arbitrary",)),
    )(dc1, u, u, dw, *grads)
    return res[0], res[1], res[2], res[3], list(res[4:])


def _attn_bwd(u, do, tot, partials=()):
    n_x = len(partials)
    grid = (T // AQ,)

    def body(q_ref, k_ref, v_ref, do_ref, tot_ref, dq_ref, dk_ref, dv_ref, kb_s, vb_s, dk_s, dv_s):
        qi = pl.program_id(0)

        @pl.when(qi == 0)
        def _():
            kb_s[...] = k_ref[...].astype(BF16)
            vb_s[...] = v_ref[...].astype(BF16)
            dk_s[...] = jnp.zeros_like(dk_s)
            dv_s[...] = jnp.zeros_like(dv_s)

        causal, tr, tc, heads = _attn_tiles()
        upper = (tr > tc).astype(BF16)
        lower = (tr < tc).astype(BF16)
        qs, qus, dos, tots = [], [], [], []
        for g in range(NG):
            lanes = slice(g * GW, (g + 1) * GW)
            q = q_ref[:, lanes]
            qs.append(_stack_heads(q * 0.125, heads).astype(BF16))
            qus.append(_stack_heads(q, heads).astype(BF16))
            dos.append(_stack_heads(do_ref[:, lanes], heads).astype(BF16))
            totv = tot_ref[g]
            tots.append(jnp.concatenate([totv[:, h * DH:h * DH + 1] for h in range(HG)], axis=0))

        def block(kb, carry, masked):
            k0 = pl.multiple_of(kb * AQ, AQ)
            out = []
            for g in range(NG):
                lanes = pl.ds(g * GW, GW)
                lm_left, dl_left, dq = carry[g]
                kk = kb_s[pl.ds(k0, AQ), lanes]
                vv = vb_s[pl.ds(k0, AQ), lanes]
                z = _dot_nt(qs[g], kk)
                sp = _softplus(z)
                lm = jnp.where(causal, -sp, 0.0) if masked else -sp
                lm_incl = lm_left + jnp.sum(lm, axis=1, keepdims=True)
                att = jnp.exp((z - sp) + _tri_sum(lm, upper) + (tots[g] - lm_incl))
                if masked:
                    att = jnp.where(causal, att, 0.0)
                dl = att * _dot_nt(dos[g], vv)
                dv_s[pl.ds(k0, AQ), lanes] += _dot_tn(att.astype(BF16), dos[g])
                prefix = dl_left + _tri_sum(dl, lower)
                beta = jnp.exp(z - sp)
                dz = (1.0 - beta) * dl - beta * prefix
                if masked:
                    dz = jnp.where(causal, dz, 0.0)
                dzs = (dz * 0.125).astype(BF16)
                dk_s[pl.ds(k0, AQ), lanes] += _dot_tn(dzs, qus[g])
                out.append((lm_incl, dl_left + jnp.sum(dl, axis=1, keepdims=True), dq + _dot(dzs, kk)))
            return tuple(out)

        zero = jnp.zeros((SR, 1), F32)
        init = tuple((zero, zero, jnp.zeros((SR, GW), F32)) for _ in range(NG))
        carry = lax.fori_loop(0, qi, lambda kb, c: block(kb, c, False), init)
        carry = block(qi, carry, True)
        for g in range(NG):
            dq_ref[:, g * GW:(g + 1) * GW] = _unstack_heads(carry[g][2], heads).astype(BF16)

        @pl.when(qi == grid[0] - 1)
        def _():
            dk_ref[...] = dk_s[...].astype(BF16)
            dv_ref[...] = dv_s[...].astype(BF16)

    col = lambda j: pl.BlockSpec((AQ, AD), lambda qi: (qi, j))
    whole = lambda j: pl.BlockSpec((T, AD), lambda qi: (0, j), pipeline_mode=pl.Buffered(1))
    res = pl.pallas_call(
        _host(body, grid, 5, 3, n_x, _scatter_copies), name="attn_bwd", grid=grid,
        in_specs=[col(0), whole(1), whole(2), col(0), pl.BlockSpec((NG, AQ, GW), lambda qi: (0, qi, 0))]
        + [HBM_SPEC] * n_x,
        out_specs=[col(0), whole(0), whole(0)] + [HBM_SPEC] * n_x,
        out_shape=[jax.ShapeDtypeStruct((T, AD), BF16)] * 3
        + [jax.ShapeDtypeStruct((NCHIP - 1,) + a.shape[1:], a.dtype) for a in partials],
        scratch_shapes=[pltpu.VMEM((T, AD), BF16), pltpu.VMEM((T, AD), BF16), pltpu.VMEM((T, AD), F32),
                        pltpu.VMEM((T, AD), F32)] + _hosted_sems(n_x),
        compiler_params=_cp(("arbitrary",), VMEM_BIG),
    )(u, u, u, do, tot, *partials)
    return res[0], res[1], res[2], list(res[3:])


def _inproj_dw(hn, du):
    tm = min(TM, T)

    def body(hn_ref, du_ref, dw_ref):
        @pl.when(pl.program_id(1) == 0)
        def _():
            dw_ref[...] = jnp.zeros_like(dw_ref)
        dw_ref[0] += _dot_tn(hn_ref[...], du_ref[...])

    return pl.pallas_call(
        body, name="inproj_dw", grid=(NCHIP, T // tm),
        in_specs=[pl.BlockSpec((tm, D), lambda k, i: (i, 0)), pl.BlockSpec((tm, SHW), lambda k, i: (i, k))],
        out_specs=pl.BlockSpec((1, D, SHW), lambda k, i: (k, 0, 0)),
        out_shape=jax.ShapeDtypeStruct((NCHIP, D, SHW), F32),
        compiler_params=_cp(("arbitrary", "arbitrary"), VMEM_BIG),
    )(hn, du)


def _inproj_dx(du, w, h, g, dres, partials=(), grads=(), after=None):
    tm = min(TM, T)
    sent = list(partials) + list(grads)
    n_x = len(sent)
    assert after is None or not sent
    grid = (T // tm, NCHIP)
    if grads:
        landing = [jax.ShapeDtypeStruct((NCHIP, a.shape[1] // 2, a.shape[2]), F32) for a in grads]
    else:
        landing = [jax.ShapeDtypeStruct((NCHIP - 1,) + a.shape[1:], a.dtype) for a in partials]

    def body(du_ref, w_ref, h_ref, g_ref, dres_ref, dh_ref, dg_ref, acc_s):
        i, k = pl.program_id(0), pl.program_id(1)

        @pl.when(jnp.logical_and(i == 0, k == 0))
        def _():
            dg_ref[...] = jnp.zeros_like(dg_ref)

        @pl.when(k == 0)
        def _():
            acc_s[...] = _dot_nt(du_ref[...], w_ref[0])

        @pl.when(k > 0)
        def _():
            acc_s[...] += _dot_nt(du_ref[...], w_ref[0])

        @pl.when(k == NCHIP - 1)
        def _():
            hh = h_ref[...]
            r = _rstd(hh)
            dhn = acc_s[...]
            dg_ref[...] += jnp.sum(dhn * hh * r, axis=0, keepdims=True)
            dh_ref[...] = dres_ref[...] + _rms_bwd(dhn, hh, r, g_ref[...])

    behind = [] if after is None else [after]
    hosting = _host(body, grid, 5, 2, n_x, _pair_copies if grads else _scatter_copies)
    res = pl.pallas_call(
        (lambda *refs: body(*refs[:5], *refs[6:])) if behind else hosting, name="inproj_dx", grid=grid,
        in_specs=[pl.BlockSpec((tm, SHW), lambda i, k: (i, k)),
                  pl.BlockSpec((1, D, SHW), lambda i, k: (k, 0, 0)),
                  pl.BlockSpec((tm, D), lambda i, k: (i, 0)),
                  pl.BlockSpec((1, D), lambda i, k: (0, 0)),
                  pl.BlockSpec((tm, D), lambda i, k: (i, 0))] + [HBM_SPEC] * n_x
        + [pl.BlockSpec(memory_space=pl.ANY)] * len(behind),
        out_specs=[pl.BlockSpec((tm, D), lambda i, k: (i, 0)), pl.BlockSpec((1, D), lambda i, k: (0, 0))]
        + [HBM_SPEC] * n_x,
        out_shape=[jax.ShapeDtypeStruct((T, D), F32), jax.ShapeDtypeStruct((1, D), F32)] + landing,
        scratch_shapes=[pltpu.VMEM((tm, D), F32)] + _hosted_sems(n_x),
        compiler_params=_cp(("arbitrary", "arbitrary"), VMEM_BIG),
    )(du, w, h, g, dres, *sent, *behind)
    return res[0], res[1], list(res[2:])


def _sum_pair(core, grads, gots):
    n = len(grads)

    def body(c_ref, *refs):
        for a in range(n):
            refs[2 * n + a][...] = (refs[a][...] + refs[n + a][...]).astype(BF16)

    mine = [pl.BlockSpec((1,) + s.shape[1:], lambda k, c: (k, c[0], 0)) for s in gots]
    same = [pl.BlockSpec((1,) + s.shape[1:], lambda k, c: (k, 0, 0)) for s in gots]
    return pl.pallas_call(
        body, name="sum_pair",
        grid_spec=pltpu.PrefetchScalarGridSpec(
            num_scalar_prefetch=1, grid=(NCHIP,), in_specs=mine + same, out_specs=same),
        out_shape=[jax.ShapeDtypeStruct(s.shape, BF16) for s in gots],
        compiler_params=_cp(("arbitrary",), VMEM_BIG),
    )(core, *grads, *gots)


def _sum_chips_share(chip, partials, gots):
    flat_p = [p for layer in partials for p in layer]
    flat_g = [g for layer in gots for g in layer]
    n, per_layer = len(flat_p), len(partials[0])

    def body(c_ref, *refs):
        full, sums = refs[2 * n:2 * n + per_layer], refs[2 * n + per_layer:3 * n + per_layer]
        send_sems, recv_sems, local_sems = refs[3 * n + per_layer:]
        x, y, c = _place()
        copies = []
        for i in range(n):
            acc = refs[i][0].astype(F32)
            for j in range(NCHIP - 1):
                acc = acc + refs[n + i][j].astype(F32)
            sums[i][...] = acc
            half = flat_p[i].shape[1]
            rows = full[i % per_layer].at[i // per_layer, pl.ds(c * half, half)]
            copies.append(pltpu.make_async_copy(sums[i], rows, local_sems.at[i]))
            copies.append(pltpu.make_async_remote_copy(
                src_ref=sums[i], dst_ref=rows, send_sem=send_sems.at[i], recv_sem=recv_sems.at[i],
                device_id=(x, y, 1 - c), device_id_type=MESH))
        for cp in copies:
            cp.start()
        for cp in copies:
            cp.wait()

    return pl.pallas_call(
        body, name="sum_chips_share",
        grid_spec=pltpu.PrefetchScalarGridSpec(
            num_scalar_prefetch=1, grid=(1,),
            in_specs=[pl.BlockSpec((1,) + s.shape[1:], lambda i, c: (c[0], 0, 0)) for s in flat_p]
            + [pl.BlockSpec(s.shape, lambda i, c: (0, 0, 0)) for s in flat_g],
            out_specs=[HBM_SPEC] * per_layer,
            scratch_shapes=[pltpu.VMEM(s.shape[1:], F32) for s in flat_p]
            + [pltpu.SemaphoreType.DMA((n,)), pltpu.SemaphoreType.DMA((n,)), pltpu.SemaphoreType.DMA((n,))]),
        out_shape=[jax.ShapeDtypeStruct((len(partials), 2 * s.shape[1], s.shape[2]), F32) for s in partials[0]],
        compiler_params=_cp(("arbitrary",), VMEM_BIG),
    )(chip, *flat_p, *flat_g)


def _adam_math(w, g, m, v):
    nm = ADAM_B1 * m + (1.0 - ADAM_B1) * g
    nv = ADAM_B2 * v + (1.0 - ADAM_B2) * (g * g)
    m_hat = nm / (1.0 - ADAM_B1 ** ADAM_STEP)
    v_hat = nv / (1.0 - ADAM_B2 ** ADAM_STEP)
    return -ADAM_LR * (m_hat / (jnp.sqrt(v_hat) + ADAM_EPS) + ADAM_WD * w), nm, nv


def _adamw(w, g, m, v, rows):
    R, C = w.shape

    def body(w_ref, g_ref, m_ref, v_ref, g_out, d_ref, nm_ref, nv_ref):
        g = g_ref[...]
        g_out[...] = g
        d_ref[...], nm_ref[...], nv_ref[...] = _adam_math(w_ref[...], g, m_ref[...], v_ref[...])

    spec = pl.BlockSpec((rows, C), lambda i: (i, 0))
    sh = jax.ShapeDtypeStruct((R, C), F32)
    return pl.pallas_call(
        body, name="adamw", grid=(R // rows,), in_specs=[spec] * 4, out_specs=[spec] * 4,
        out_shape=[sh, sh, sh, sh], compiler_params=_cp(("arbitrary",)),
    )(w, g, m, v)


def _adamw_taps(w, g, m, v):
    def body(w_ref, g_ref, m_ref, v_ref, g_out, d_ref, nm_ref, nv_ref):
        g = g_ref[:, pl.ds(0, CW), :]
        g_out[...] = g
        d_ref[...], nm_ref[...], nv_ref[...] = _adam_math(w_ref[...], g, m_ref[...], v_ref[...])

    whole = lambda a: pl.BlockSpec(a.shape, lambda i: (0, 0, 0))
    return pl.pallas_call(
        body, name="adamw_taps", grid=(1,), in_specs=[whole(w), whole(g), whole(w), whole(w)],
        out_specs=[whole(w)] * 4, out_shape=[jax.ShapeDtypeStruct(w.shape, F32)] * 4,
    )(w, g, m, v)


def _small_adamw(tot, ws, ms, vs):
    n = len(ws)

    def body(*refs):
        tot_ref = refs[0]
        w_refs, m_refs, v_refs = refs[1:1 + n], refs[1 + n:1 + 2 * n], refs[1 + 2 * n:1 + 3 * n]
        outs = refs[1 + 3 * n:]
        for i in range(n):
            rows, width = ws[i].shape
            g = tot_ref[pl.ds(SMALL_ROW[i], rows), pl.ds(0, width)]
            outs[4 * i][...] = g
            outs[4 * i + 1][...], outs[4 * i + 2][...], outs[4 * i + 3][...] = _adam_math(
                w_refs[i][...], g, m_refs[i][...], v_refs[i][...])
        outs[4 * n][...] = tot_ref[pl.ds(LOSS_ROW, 1), pl.ds(0, LANE)]

    vmem = pl.BlockSpec(memory_space=pltpu.VMEM)
    res = pl.pallas_call(
        body, name="small_adamw", in_specs=[vmem] * (1 + 3 * n), out_specs=[vmem] * (4 * n + 1),
        out_shape=[jax.ShapeDtypeStruct(w.shape, F32) for w in ws for _ in range(4)]
        + [jax.ShapeDtypeStruct((1, LANE), F32)],
    )(tot, *ws, *ms, *vs)
    return [res[4 * i:4 * i + 4] for i in range(n)], res[4 * n]


HBM_SPEC = pl.BlockSpec(memory_space=pltpu.HBM)


def _place():
    return lax.axis_index("x"), lax.axis_index("y"), lax.axis_index("c")


def _all_gather_split(shard):
    def body(in_ref, out_ref, send_sems, recv_sems):
        direct, relayed, passed = _gather_tree_copies([in_ref], [out_ref], send_sems, recv_sems)
        for cp in direct:
            cp.start()
        for i in range(2):
            direct[i].wait_recv()
            relayed[i].start()
            passed[i].start()
        for cp in relayed:
            cp.wait_recv()
        passed[2].start()
        for cp in passed + direct[2:]:
            cp.wait_recv()
        for cp in direct + relayed + passed:
            cp.wait_send()

    return pl.pallas_call(
        body, name="all_gather_split", in_specs=[HBM_SPEC], out_specs=HBM_SPEC,
        out_shape=jax.ShapeDtypeStruct((NCHIP,) + shard.shape, shard.dtype),
        scratch_shapes=_hosted_sems(1, GATHER_SEMS),
    )(shard)


SEM_SPEC = pl.BlockSpec(memory_space=pltpu.SEMAPHORE)
ORDERED_EFFECT = pltpu.CompilerParams(has_side_effects=pltpu.SideEffectType.DATAFLOW_SIDE_EFFECTING)


def _scatter_start(partial):
    land = pltpu.with_memory_space_constraint(
        lax.empty((NCHIP - 1,) + partial.shape[1:], partial.dtype), pltpu.HBM)

    def body(p_ref, land_ref, send_sems, recv_sems, p_thru, land_thru, token):
        for cp in _scatter_copies([p_ref], [land_ref], send_sems, recv_sems):
            cp.start()
        token[...] = jnp.zeros_like(token)

    return pl.pallas_call(
        body, name="scatter_start",
        out_shape=(pltpu.SemaphoreType.DMA((NCHIP - 1,)), pltpu.SemaphoreType.DMA((NCHIP - 1,)),
                   pltpu.HBM(partial.shape, partial.dtype), pltpu.HBM(land.shape, land.dtype),
                   jax.ShapeDtypeStruct((8, LANE), F32)),
        in_specs=(HBM_SPEC, HBM_SPEC),
        out_specs=(SEM_SPEC, SEM_SPEC, HBM_SPEC, HBM_SPEC, pl.BlockSpec(memory_space=pltpu.VMEM)),
        input_output_aliases={0: 2, 1: 3}, compiler_params=ORDERED_EFFECT,
    )(pltpu.with_memory_space_constraint(partial, pltpu.HBM), land)


def _scatter_wait(send_sems, recv_sems, p_thru, land_thru, after):
    def body(p_ref, land_ref, send_sems, recv_sems, after_ref, p_dead, got_ref):
        for cp in _scatter_copies([p_ref], [land_ref], send_sems, recv_sems):
            cp.wait_send()
            cp.wait_recv()

    return pl.pallas_call(
        body, name="scatter_wait",
        out_shape=(pltpu.HBM(p_thru.shape, p_thru.dtype), pltpu.HBM(land_thru.shape, land_thru.dtype)),
        in_specs=(HBM_SPEC, HBM_SPEC, SEM_SPEC, SEM_SPEC, pl.BlockSpec(memory_space=pl.ANY)),
        out_specs=(HBM_SPEC, HBM_SPEC), input_output_aliases={0: 0, 1: 1}, compiler_params=ORDERED_EFFECT,
    )(p_thru, land_thru, send_sems, recv_sems, after)


def _pair_exchange(grads):
    n = len(grads)

    def body(*refs):
        copies = _pair_copies(refs[:n], refs[n:2 * n], refs[2 * n], refs[2 * n + 1])
        for cp in copies:
            cp.start()
        for cp in copies:
            cp.wait()

    return pl.pallas_call(
        body, name="pair_exchange", in_specs=[HBM_SPEC] * n, out_specs=[HBM_SPEC] * n,
        out_shape=[jax.ShapeDtypeStruct((NCHIP, g.shape[1] // 2, g.shape[2]), F32) for g in grads],
        scratch_shapes=[pltpu.SemaphoreType.DMA((n,)), pltpu.SemaphoreType.DMA((n,))],
    )(*grads)


def _small_allreduce(rows, loss_blk, per_head=()):
    n = len(rows)

    def body(*refs):
        loss_ref, o_ref, pk, slots, send_sems, recv_sems = refs[n:]
        pk[...] = jnp.zeros_like(pk)
        for i in range(n):
            row = refs[i][...]
            if i in per_head:
                row = functools.reduce(jnp.add, [row[:, h * DH:(h + 1) * DH] for h in range(AD // DH)])
            pk[pl.ds(i, 1), pl.ds(0, row.shape[1])] = row
        pk[pl.ds(LOSS_ROW, 1), pl.ds(0, LANE)] = loss_ref[pl.ds(0, 1), :]
        x, y, c = _place()
        me = 4 * x + 2 * y + c
        slots[me] = pk[...]
        copies = []
        for r in range(1, 8):
            rx, ry, rc = (r >> 2) & 1, (r >> 1) & 1, r & 1
            peer = (x + rx - 2 * x * rx, y + ry - 2 * y * ry, c + rc - 2 * c * rc)
            cp = pltpu.make_async_remote_copy(
                src_ref=pk, dst_ref=slots.at[me], send_sem=send_sems.at[r - 1], recv_sem=recv_sems.at[r - 1],
                device_id=peer, device_id_type=MESH)
            cp.start()
            copies.append(cp)
        for cp in copies:
            cp.wait()
        acc = slots[0]
        for j in range(1, 8):
            acc = acc + slots[j]
        o_ref[...] = acc

    vmem = pl.BlockSpec(memory_space=pltpu.VMEM)
    return pl.pallas_call(
        body, name="small_allreduce", in_specs=[vmem] * (n + 1), out_specs=vmem,
        out_shape=jax.ShapeDtypeStruct((SMALL_PK, D), F32),
        scratch_shapes=[pltpu.VMEM((SMALL_PK, D), F32), pltpu.VMEM((8, SMALL_PK, D), F32),
                        pltpu.SemaphoreType.DMA((7,)), pltpu.SemaphoreType.DMA((7,))],
    )(*rows, loss_blk)


def _seg_matrix():
    i = lax.broadcasted_iota(jnp.int32, (AD, AD), 0) // DH
    j = lax.broadcasted_iota(jnp.int32, (AD, AD), 1) // DH
    return (i == j).astype(BF16)


TAIL = ("w_out", "w_ple_gate", "w_ple", "w_pw", "dw_w")


def _local_step(x, p, tgt, sm, shards, chip, ci):
    seg = _seg_matrix()
    core = jnp.reshape(ci, (1,)).astype(jnp.int32)
    chip_idx = jnp.reshape(chip, (1,)).astype(jnp.int32)
    w_in_next = _all_gather_split(shards[0]["w_in"])
    h = x
    saved = []
    for l in range(DEPTH):
        w_in = w_in_next
        row = lambda name: sm[name][l:l + 1]
        u, hn = _rms_inproj(h, row("norm_g"), w_in)
        todo = [shards[l][k] for k in TAIL] + ([shards[l + 1]["w_in"]] if l + 1 < DEPTH else [])
        o, ya, tot, got = _attn_fwd(u, jnp.tile(row("attn_out_g"), (1, AD // DH)), todo)
        w_out = got[0].reshape(D, D)
        w_gate = got[1].reshape(D, D)
        w_ple = got[2]
        w_pw = got[3].reshape(CD, CD)
        dw = got[4]
        if l + 1 < DEPTH:
            w_in_next = got[5]
        c1 = _glu_conv(u, dw, row("dw_b"))
        c3, yc, h1, gate, pe, h2, *at_end = _layer_tail(
            c1, u, ya, h, p[l], row("conv_ln_g"), row("conv_ln_b"), w_pw, row("conv_out_g"), w_out,
            row("ple_norm_g"), w_gate, w_ple, head=(tgt, sm["final_g"]) if l == DEPTH - 1 else None)
        saved.append(dict(h=h, u=u, hn=hn, o=o, ya=ya, tot=tot, c1=c1, c3=c3, yc=yc, h1=h1, gate=gate, pe=pe,
                          w_in=w_in, w_out=w_out, w_gate=w_gate, w_pw=w_pw, dw=dw))
        h = h2
    dh, (loss_blk, dfg) = h, at_end
    small = [None] * DEPTH
    pending, partials, arrived = [], {}, {}
    pair_sum = lambda grads: _sum_pair(core, grads, _pair_exchange(grads))
    for l in reversed(range(DEPTH)):
        s = saved[l]
        row = lambda name: sm[name][l:l + 1]
        dh1, dy, dwg, dwp, dwo, dpg = _ple_out_bwd(
            dh, s["h1"], s["gate"], s["pe"], p[l], s["ya"], s["yc"], row("ple_norm_g"), s["w_gate"], s["w_out"])
        ag_t = jnp.tile(row("attn_out_g"), (1, AD // DH))
        do, dga, dgc, dc1, dwpw, dag, dcg, dlg, dlb = _branch_bwd(
            dy, s["o"], s["u"], s["c1"], s["c3"], ag_t, row("conv_ln_g"), row("conv_ln_b"), s["w_pw"],
            row("conv_out_g"), seg)
        tail = [dwo.reshape(NCHIP, 256, D), dwg.reshape(NCHIP, 256, D), dwp, dwpw.reshape(NCHIP, 128, CD)]
        dcv, dcgate, ddw, ddb, halves = _conv_bwd(dc1, s["u"], s["dw"], tail if l == 0 else ())
        tail.append(ddw)
        if l == 0:
            partials[(l, "tail")] = _sum_pair(core, tail, halves + list(_pair_exchange([ddw])))
            pending.append((l, "tail"))
        send = [t for key in pending for t in partials[key]]
        dq, dk, dv, got = _attn_bwd(s["u"], do, s["tot"], send)
        for key in pending:
            arrived[key], got = got[:len(partials[key])], got[len(partials[key]):]
        pending = []
        du = jnp.concatenate([dq, dk, dv, dga, dcv, dcgate, dgc], axis=1)
        dwin = _inproj_dw(s["hn"], du)
        if l == 0:
            in_flight = _scatter_start(pair_sum([dwin])[0])
            dh, dng, _ = _inproj_dx(du, s["w_in"], s["h"], row("norm_g"), dh1, after=in_flight[4])
            small[l] = dict(norm_g=dng, attn_out_g=dag, dw_b=ddb, conv_ln_g=dlg, conv_ln_b=dlb,
                            conv_out_g=dcg, ple_norm_g=dpg)
            per_head = tuple(DEPTH * SMALL2.index("attn_out_g") + j for j in range(DEPTH))
            small_tot = _small_allreduce([small[j][k] for k in SMALL2 for j in range(DEPTH)] + [dfg], loss_blk,
                                         per_head)
            landed = _scatter_wait(*in_flight[:4], small_tot)
            partials[(l, "w_in")], arrived[(l, "w_in")] = [landed[0]], [landed[1]]
        else:
            dh, dng, halves = _inproj_dx(du, s["w_in"], s["h"], row("norm_g"), dh1, grads=tail + [dwin])
            tail_p = _sum_pair(core, tail + [dwin], halves)
            partials[(l, "tail")], partials[(l, "w_in")] = tail_p[:-1], tail_p[-1:]
            pending = [(l, "tail"), (l, "w_in")]
            small[l] = dict(norm_g=dng, attn_out_g=dag, dw_b=ddb, conv_ln_g=dlg, conv_ln_b=dlb,
                            conv_out_g=dcg, ple_norm_g=dpg)
    both = lambda d: [list(d[(l, "w_in")]) + list(d[(l, "tail")]) for l in range(DEPTH)]
    big = dict(zip(BIG, _sum_chips_share(chip_idx, both(partials), both(arrived))))
    return dh, big, small_tot


BIG = ("w_in", "w_out", "w_ple_gate", "w_ple", "w_pw", "dw_w")
SMALL2 = ("norm_g", "ple_norm_g", "dw_b", "conv_ln_g", "conv_ln_b", "conv_out_g", "attn_out_g")
SMALL_ROW = (0, 2, 4, 6, 8, 10, 12, 14)


def kernel(x, p, norm_g, w_in, attn_out_g, dw_w, dw_b, conv_ln_g, conv_ln_b, w_pw, conv_out_g, w_out, ple_norm_g, w_ple_gate, w_ple, final_g, loss_target, m_norm_g, m_w_in, m_attn_out_g, m_dw_w, m_dw_b, m_conv_ln_g, m_conv_ln_b, m_w_pw, m_conv_out_g, m_w_out, m_ple_norm_g, m_w_ple_gate, m_w_ple, m_final_g, v_norm_g, v_w_in, v_attn_out_g, v_dw_w, v_dw_b, v_conv_ln_g, v_conv_ln_b, v_w_pw, v_conv_out_g, v_w_out, v_ple_norm_g, v_w_ple_gate, v_w_ple, v_final_g):
    W = dict(norm_g=norm_g, w_in=w_in, attn_out_g=attn_out_g, dw_w=dw_w, dw_b=dw_b, conv_ln_g=conv_ln_g,
             conv_ln_b=conv_ln_b, w_pw=w_pw, conv_out_g=conv_out_g, w_out=w_out, ple_norm_g=ple_norm_g,
             w_ple_gate=w_ple_gate, w_ple=w_ple, final_g=final_g)
    M = dict(norm_g=m_norm_g, w_in=m_w_in, attn_out_g=m_attn_out_g, dw_w=m_dw_w, dw_b=m_dw_b,
             conv_ln_g=m_conv_ln_g, conv_ln_b=m_conv_ln_b, w_pw=m_w_pw, conv_out_g=m_conv_out_g, w_out=m_w_out,
             ple_norm_g=m_ple_norm_g, w_ple_gate=m_w_ple_gate, w_ple=m_w_ple, final_g=m_final_g)
    V = dict(norm_g=v_norm_g, w_in=v_w_in, attn_out_g=v_attn_out_g, dw_w=v_dw_w, dw_b=v_dw_b,
             conv_ln_g=v_conv_ln_g, conv_ln_b=v_conv_ln_b, w_pw=v_w_pw, conv_out_g=v_conv_out_g, w_out=v_w_out,
             ple_norm_g=v_ple_norm_g, w_ple_gate=v_w_ple_gate, w_ple=v_w_ple, final_g=v_final_g)
    order = ("norm_g", "w_in", "attn_out_g", "dw_w", "dw_b", "conv_ln_g", "conv_ln_b", "w_pw", "conv_out_g",
             "w_out", "ple_norm_g", "w_ple_gate", "w_ple", "final_g")

    pad_taps = lambda a: jnp.pad(a, ((0, 0), (0, CWP - CW), (0, 0)))
    cast = dict(w_in=w_in.astype(BF16), w_out=w_out.astype(BF16), w_ple_gate=w_ple_gate.astype(BF16),
                w_ple=w_ple.astype(BF16), w_pw=w_pw.astype(BF16), dw_w=pad_taps(dw_w))
    shards = [{k: v[l] for k, v in cast.items()} for l in range(DEPTH)]
    xi, yi, ci = lax.axis_index("x"), lax.axis_index("y"), lax.axis_index("c")
    chip = 2 * xi + yi

    sm = {k: W[k] for k in SMALL2}
    sm["final_g"] = final_g.reshape(1, D)
    grad_x, big, small_tot = _local_step(x[0], p[:, 0], loss_target[0], sm, shards, chip, ci)
    g_big = {name: big[name].reshape(cast[name].shape) for name in BIG}

    small_names = SMALL2 + ("final_g",)
    as_rows = lambda t: t.reshape(1, D) if t.ndim == 1 else t
    results, loss_row = _small_adamw(
        small_tot, [as_rows(W[k]) for k in small_names],
        [as_rows(M[k]) for k in small_names], [as_rows(V[k]) for k in small_names])
    loss = loss_row[0, 0]

    grads, deltas, new_m, new_v = {}, {}, {}, {}
    for name in BIG:
        if name == "dw_w":
            grads[name], deltas[name], new_m[name], new_v[name] = _adamw_taps(W[name], g_big[name], M[name], V[name])
            continue
        cols = W[name].shape[-1]
        rows_total = W[name].size // cols
        flat = lambda a: a.reshape(rows_total, cols)
        four = _adamw(flat(W[name]), flat(g_big[name]), flat(M[name]), flat(V[name]), min(rows_total, 256))
        grads[name], deltas[name], new_m[name], new_v[name] = (t.reshape(W[name].shape) for t in four)
    for k, four in zip(small_names, results):
        grads[k], deltas[k], new_m[k], new_v[k] = (t.reshape(W[k].shape) for t in four)

    return (loss, grad_x[None], *[grads[n] for n in order], *[deltas[n] for n in order],
            *[new_m[n] for n in order], *[new_v[n] for n in order])
```

```python
import functools

import jax
import jax.numpy as jnp
from jax import lax
from jax.experimental import pallas as pl
from jax.experimental.pallas import tpu as pltpu

F32 = jnp.float32
BF16 = jnp.bfloat16

T = 2048
D = 1024
DIN = 3584
NCHIP = 4
SHW = DIN // NCHIP
AD = 512
CD = 512
DH = 64
CW = 31
CWP = 32
PLE = 256
DEPTH = 2
EPS = 1e-6
AQ = 256
HG = 4
GW = HG * DH
SR = HG * AQ
NG = AD // GW
LANE = 128
TM = 1024
TR = 256

ADAM_LR = 0.001
ADAM_B1 = 0.9
ADAM_B2 = 0.999
ADAM_EPS = 1e-08
ADAM_WD = 0.01
ADAM_STEP = 10

SMALL_PK = 16
LOSS_ROW = 15

VMEM_BIG = 56 * 1024 * 1024
MESH = pl.DeviceIdType.MESH


def _cp(sem=None, vmem=None):
    kw = {}
    if sem is not None:
        kw["dimension_semantics"] = sem
    if vmem is not None:
        kw["vmem_limit_bytes"] = vmem
    return pltpu.CompilerParams(**kw)


def _dot(a, b):
    return jnp.dot(a, b, preferred_element_type=F32)


def _dot_nt(a, b):
    return lax.dot_general(a, b, (((1,), (1,)), ((), ())), preferred_element_type=F32)


def _dot_tn(a, b):
    return lax.dot_general(a, b, (((0,), (0,)), ((), ())), preferred_element_type=F32)


def _dot2(x, m):
    hi = x.astype(BF16)
    lo = (x - hi.astype(F32)).astype(BF16)
    return _dot(hi, m) + _dot(lo, m)


def _sig(x):
    return 1.0 / (1.0 + jnp.exp(-x))


def _softplus(z):
    return jnp.maximum(z, 0.0) + jnp.log(1.0 + jnp.exp(-jnp.abs(z)))


def _rstd(x):
    return lax.rsqrt(jnp.mean(x * x, axis=-1, keepdims=True) + EPS)


def _rms_bwd(dy, x, r, g):
    dn = dy * g
    return r * dn - x * (r * r * r) * jnp.mean(dn * x, axis=-1, keepdims=True)


def _rms_inproj(h, g, w):
    tm = min(TM, T)

    def body(h_ref, g_ref, w_ref, u_ref, hn_ref, hn_s):
        @pl.when(pl.program_id(1) == 0)
        def _():
            hh = h_ref[...]
            hn = (hh * _rstd(hh) * g_ref[...]).astype(BF16)
            hn_s[...] = hn
            hn_ref[...] = hn
        u_ref[...] = _dot(hn_s[...], w_ref[0])

    return pl.pallas_call(
        body, name="rms_inproj", grid=(T // tm, NCHIP),
        in_specs=[pl.BlockSpec((tm, D), lambda i, k: (i, 0)),
                  pl.BlockSpec((1, D), lambda i, k: (0, 0)),
                  pl.BlockSpec((1, D, SHW), lambda i, k: (k, 0, 0))],
        out_specs=[pl.BlockSpec((tm, SHW), lambda i, k: (i, k)),
                   pl.BlockSpec((tm, D), lambda i, k: (i, 0))],
        out_shape=[jax.ShapeDtypeStruct((T, DIN), F32), jax.ShapeDtypeStruct((T, D), BF16)],
        scratch_shapes=[pltpu.VMEM((tm, D), BF16)],
        compiler_params=_cp(("arbitrary", "arbitrary"), VMEM_BIG),
    )(h, g, w)


def _attn_tiles():
    row = lax.broadcasted_iota(jnp.int32, (SR, AQ), 0) & (AQ - 1)
    col = lax.broadcasted_iota(jnp.int32, (SR, AQ), 1)
    tr = lax.broadcasted_iota(jnp.int32, (AQ, AQ), 0)
    tc = lax.broadcasted_iota(jnp.int32, (AQ, AQ), 1)
    lane_head = lax.broadcasted_iota(jnp.int32, (1, GW), 1) // DH
    return col < row, tr, tc, [lane_head == h for h in range(HG)]


def _stack_heads(t, heads):
    return jnp.concatenate([jnp.where(m, t, 0.0) for m in heads], axis=0)


def _unstack_heads(t, heads):
    out = t[:AQ]
    for h in range(1, HG):
        out = jnp.where(heads[h], t[h * AQ:(h + 1) * AQ], out)
    return out


def _tri_sum(x, tri):
    hi = x.astype(BF16)
    lo = (x - hi.astype(F32)).astype(BF16)
    both = _dot(jnp.concatenate([hi, lo], axis=0), tri)
    return both[:SR] + both[SR:]


def _scatter_copies(ps, gots, send_sems, recv_sems):
    x, y, c = _place()
    peers = [(1 - x, y), (x, 1 - y), (1 - x, 1 - y)]
    return [pltpu.make_async_remote_copy(
        src_ref=ps[a].at[2 * px + py], dst_ref=gots[a].at[r], send_sem=send_sems.at[3 * a + r],
        recv_sem=recv_sems.at[3 * a + r], device_id=(px, py, c), device_id_type=MESH)
        for a in range(len(ps)) for r, (px, py) in enumerate(peers)]


GATHER_SEMS = 8


def _gather_tree_copies(ins, outs, send_sems, recv_sems):
    x, y, c = _place()
    me, xn, yn, dg = 2 * x + y, 2 * (1 - x) + y, 2 * x + (1 - y), 2 * (1 - x) + (1 - y)
    to_x, to_y, sibling = (1 - x, y, c), (x, 1 - y, c), (x, y, 1 - c)
    direct, relayed, passed = [], [], []
    for a in range(len(ins)):
        half = ins[a].shape[0] // 2
        mine = pl.ds(c * half, half)
        first, second = pl.ds(c * half, half // 2), pl.ds(c * half + half // 2, half // 2)

        def copy(i, src, dst, to, k=GATHER_SEMS * a):
            return pltpu.make_async_remote_copy(src_ref=src, dst_ref=dst, send_sem=send_sems.at[k + i],
                                                recv_sem=recv_sems.at[k + i], device_id=to, device_id_type=MESH)

        own, slot = ins[a].at[mine], outs[a].at[me, mine]
        direct += [copy(0, own, slot, to_x), copy(1, own, slot, to_y), copy(7, ins[a], outs[a].at[me], sibling)]
        relayed += [copy(2, outs[a].at[xn, first], outs[a].at[xn, first], to_y),
                    copy(3, outs[a].at[yn, second], outs[a].at[yn, second], to_x)]
        passed += [copy(4 + i, outs[a].at[j, mine], outs[a].at[j, mine], sibling) for i, j in enumerate((xn, yn, dg))]
    return direct, relayed, passed


def _pair_copies(ins, outs, send_sems, recv_sems):
    x, y, c = _place()
    copies = []
    for a in range(len(ins)):
        half = ins[a].shape[1] // 2
        copies.append(pltpu.make_async_remote_copy(
            src_ref=ins[a].at[:, pl.ds((1 - c) * half, half), :], dst_ref=outs[a], send_sem=send_sems.at[a],
            recv_sem=recv_sems.at[a], device_id=(x, y, 1 - c), device_id_type=MESH))
    return copies


def _host(body, grid, n_in, n_out, n_x, make_copies, mids=()):
    if not n_x:
        return body

    def hosting(*refs):
        a, b = n_in + n_x, n_in + 2 * n_x + n_out
        copies = make_copies(refs[n_in:a], refs[a + n_out:b], refs[-2], refs[-1])
        stages = copies if isinstance(copies, tuple) else (copies,)
        ids = [pl.program_id(d) for d in range(len(grid))]
        at = lambda step: functools.reduce(jnp.logical_and, [i == s for i, s in zip(ids, step)])

        @pl.when(at([0] * len(grid)))
        def _():
            for cp in stages[0]:
                cp.start()

        for before, after, step in zip(stages, stages[1:], mids):
            @pl.when(at(step))
            def _(before=before, after=after):
                for cp in before:
                    cp.wait_recv()
                for cp in after:
                    cp.start()

        body(*refs[:n_in], *refs[a:a + n_out], *refs[b:-2])

        @pl.when(at([g - 1 for g in grid]))
        def _():
            for cp in stages[-1]:
                cp.wait_recv()
            for stage in stages:
                for cp in stage:
                    cp.wait_send()

    return hosting


def _hosted_sems(n_x, per_array=3):
    n = per_array * n_x
    return [pltpu.SemaphoreType.DMA((n,)), pltpu.SemaphoreType.DMA((n,))] if n_x else []


RC = 256


def _chunk_causal(r):
    row = lax.broadcasted_iota(jnp.int32, (RC, AQ), 0) + (r * RC) % AQ
    return lax.broadcasted_iota(jnp.int32, (RC, AQ), 1) < row


def _attn_fwd(u, agw, shards=()):
    n = len(shards)
    grid = (T // AQ,)

    def body(q_ref, k_ref, v_ref, g_ref, ag_ref, o_ref, y_ref, tot_ref,
             kb_s, vb_s, qs_s, z_s, zs_s, lmb_s, suf_s, att_s, acc_s, run_s):
        qi = pl.program_id(0)

        @pl.when(qi == 0)
        def _():
            kb_s[...] = k_ref[...].astype(BF16)
            vb_s[...] = v_ref[...].astype(BF16)

        _, tr, tc, heads = _attn_tiles()
        upper = (tr > tc).astype(BF16)
        same_head = ((tr // DH) == (tc // DH)).astype(BF16)
        for g in range(NG):
            qs_s[g] = _stack_heads(q_ref[:, g * GW:(g + 1) * GW] * 0.125, heads).astype(BF16)
        acc_s[...] = jnp.zeros_like(acc_s)
        run_s[...] = jnp.zeros_like(run_s)

        def block(kb, masked):
            k0 = pl.multiple_of(kb * AQ, AQ)
            for g in range(NG):
                lanes = pl.ds(g * GW, GW)
                z_s[g] = _dot_nt(qs_s[g], kb_s[pl.ds(k0, AQ), lanes])
                for r in range(SR // RC):
                    rows = pl.ds(r * RC, RC)
                    z = z_s[g, rows, :]
                    zs = jnp.minimum(z, 0.0) - jnp.log(1.0 + jnp.exp(-jnp.abs(z)))
                    lm = zs - z
                    if masked:
                        lm = jnp.where(_chunk_causal(r), lm, 0.0)
                    run = run_s[g, rows, :]
                    zs_s[g, rows, :] = zs + run[:, 0:1]
                    hi = lm.astype(BF16)
                    lmb_s[g, rows, :] = hi
                    lmb_s[g, pl.ds(SR + r * RC, RC), :] = (lm - hi.astype(F32)).astype(BF16)
                    run_s[g, rows, :] = run + jnp.sum(lm, axis=1, keepdims=True)
                suf_s[g] = _dot(lmb_s[g], upper)
                for r in range(SR // RC):
                    rows = pl.ds(r * RC, RC)
                    att = jnp.exp(zs_s[g, rows, :] + suf_s[g, rows, :] + suf_s[g, pl.ds(SR + r * RC, RC), :])
                    if masked:
                        att = jnp.where(_chunk_causal(r), att, 0.0)
                    att_s[g, rows, :] = att.astype(BF16)
                acc_s[g] += _dot(att_s[g], vb_s[pl.ds(k0, AQ), lanes])

        block(qi, True)

        def step(i, c):
            block(qi - 1 - i, False)
            return c

        lax.fori_loop(0, qi, step, 0)
        gate = g_ref[...]
        agv = ag_ref[...]
        for g in range(NG):
            lanes = slice(g * GW, (g + 1) * GW)
            o = _unstack_heads(acc_s[g], heads)
            osq = o * o
            ms = _dot2(osq, same_head)
            gg = gate[:, lanes]
            o_ref[:, lanes] = o
            y_ref[:, lanes] = (o * lax.rsqrt(ms * (1.0 / DH) + EPS) * agv[:, lanes] * (gg * _sig(gg))).astype(BF16)
            tot_ref[g] = _unstack_heads(jnp.broadcast_to(run_s[g][:, 0:1], (SR, GW)), heads)

    tile = lambda dt, rows=SR: pltpu.VMEM((NG, rows, AQ), dt)
    scratch = [pltpu.VMEM((T, AD), BF16), pltpu.VMEM((T, AD), BF16), pltpu.VMEM((NG, SR, GW), BF16),
               tile(F32), tile(F32), tile(BF16, 2 * SR), tile(F32, 2 * SR), tile(BF16),
               pltpu.VMEM((NG, SR, GW), F32), pltpu.VMEM((NG, SR, LANE), F32)]
    col = lambda j: pl.BlockSpec((AQ, AD), lambda qi: (qi, j))
    res = pl.pallas_call(
        _host(body, grid, 5, 3, n, _gather_tree_copies, mids=((grid[0] * 5 // 8,), (grid[0] * 7 // 8,))), name="attn_fwd", grid=grid,
        in_specs=[col(0), pl.BlockSpec((T, AD), lambda qi: (0, 1)), pl.BlockSpec((T, AD), lambda qi: (0, 2)),
                  col(3), pl.BlockSpec((1, AD), lambda qi: (0, 0))] + [HBM_SPEC] * n,
        out_specs=[col(0), col(0), pl.BlockSpec((NG, AQ, GW), lambda qi: (0, qi, 0))] + [HBM_SPEC] * n,
        out_shape=[jax.ShapeDtypeStruct((T, AD), F32), jax.ShapeDtypeStruct((T, AD), BF16),
                   jax.ShapeDtypeStruct((NG, T, GW), F32)]
        + [jax.ShapeDtypeStruct((NCHIP,) + s.shape, s.dtype) for s in shards],
        scratch_shapes=scratch + _hosted_sems(n, GATHER_SEMS),
        compiler_params=_cp(("arbitrary",), VMEM_BIG),
    )(u, u, u, u, agw, *shards)
    return res[0], res[1], res[2], list(res[3:])


def _glu_conv(u, dw, db):
    tr = 256

    def body(cv_ref, cg_ref, w_ref, b_ref, c1_ref, pad_s):
        pad_s[pl.ds(0, CWP), :] = jnp.zeros((CWP, LANE), F32)
        pad_s[pl.ds(CWP, T), :] = cv_ref[...] * _sig(cg_ref[...])
        wv = w_ref[0]
        bias = b_ref[...]

        def tile(i, carry):
            r0 = pl.multiple_of(i * tr, tr)
            acc = jnp.zeros((tr, LANE), F32) + bias
            for w in range(CW):
                acc = acc + pad_s[pl.ds(r0 + (CWP - CW + 1) + w, tr), :] * wv[w:w + 1, :]
            c1_ref[pl.ds(r0, tr), :] = acc
            return carry

        lax.fori_loop(0, T // tr, tile, 0)

    return pl.pallas_call(
        body, name="glu_conv", grid=(CD // LANE,),
        in_specs=[pl.BlockSpec((T, LANE), lambda cb: (0, 16 + cb)),
                  pl.BlockSpec((T, LANE), lambda cb: (0, 20 + cb)),
                  pl.BlockSpec((1, CWP, LANE), lambda cb: (cb, 0, 0)),
                  pl.BlockSpec((1, LANE), lambda cb: (0, cb))],
        out_specs=pl.BlockSpec((T, LANE), lambda cb: (0, cb)),
        out_shape=jax.ShapeDtypeStruct((T, CD), F32),
        scratch_shapes=[pltpu.VMEM((T + CWP, LANE), F32)],
        compiler_params=_cp(("arbitrary",)),
    )(u, u, dw, db)


def _ln_silu(c1, lg, lb):
    mu = jnp.mean(c1, axis=-1, keepdims=True)
    xc = c1 - mu
    rs = lax.rsqrt(jnp.mean(xc * xc, axis=-1, keepdims=True) + EPS)
    xh = xc * rs
    ln = xh * lg + lb
    s = _sig(ln)
    return xh, rs, ln, s


def _layer_tail(c1, u, ya, h, p, lg, lb, wpw, cg, wout, pg, wgate, wple, head=None):
    tm = min(TR, T)

    def body(c1_ref, gc_ref, ya_ref, h_ref, p_ref, lg_ref, lb_ref, wpw_ref, cg_ref, wout_ref,
             pg_ref, wgate_ref, wple_ref, *rest):
        c3_ref, yc_ref, h1_ref, gate_ref, pe_ref, h2_ref = rest[-6 - 2 * bool(head):][:6]
        _, _, ln, s = _ln_silu(c1_ref[...], lg_ref[...], lb_ref[...])
        c2 = (ln * s).astype(BF16)
        c3 = _dot(c2, wpw_ref[...])
        gc = gc_ref[...]
        yc = (c3 * _rstd(c3) * cg_ref[...] * (gc * _sig(gc))).astype(BF16)
        c3_ref[...] = c3
        yc_ref[...] = yc
        y = _dot(ya_ref[...], wout_ref[pl.ds(0, AD), :]) + _dot(yc, wout_ref[pl.ds(AD, CD), :])
        h1 = h_ref[...] + y
        hn2 = (h1 * _rstd(h1) * pg_ref[...]).astype(BF16)
        gate = _sig(_dot(hn2, wgate_ref[...]))
        pb = p_ref[...].astype(BF16)
        pe = jnp.concatenate([_dot(pb, wple_ref[k]) for k in range(NCHIP)], axis=1)
        h1_ref[...] = h1
        gate_ref[...] = gate.astype(BF16)
        pe_ref[...] = pe.astype(BF16)
        h2 = h1 + pe * gate
        if not head:
            h2_ref[...] = h2
            return
        t_ref, fg_ref, loss_ref, dfg_ref = rest[0], rest[1], rest[-2], rest[-1]

        @pl.when(pl.program_id(0) == 0)
        def _():
            loss_ref[...] = jnp.zeros_like(loss_ref)
            dfg_ref[...] = jnp.zeros_like(dfg_ref)
        fg = fg_ref[...]
        r = _rstd(h2)
        e = h2 * r * fg - t_ref[...]
        loss_ref[...] += 0.5 * jnp.sum(jnp.mean(e * e, axis=-1, keepdims=True))
        dy = e * (1.0 / D)
        dfg_ref[...] += jnp.sum(dy * h2 * r, axis=0, keepdims=True)
        h2_ref[...] = _rms_bwd(dy, h2, r, fg)

    row = lambda w: pl.BlockSpec((tm, w), lambda i: (i, 0))
    full = lambda *s: pl.BlockSpec(s, lambda i: (0,) * len(s), pipeline_mode=pl.Buffered(1))
    extra = bool(head)
    return pl.pallas_call(
        body, name="layer_tail", grid=(T // tm,),
        in_specs=[row(CD), pl.BlockSpec((tm, CD), lambda i: (i, 6)), row(AD), row(D), row(PLE),
                  full(1, CD), full(1, CD), full(CD, CD), full(1, CD), full(D, D),
                  full(1, D), full(D, D), full(NCHIP, PLE, PLE)] + [row(D), full(1, D)] * extra,
        out_specs=[row(CD), row(CD), row(D), row(D), row(D), row(D)] + [full(8, LANE), full(1, D)] * extra,
        out_shape=[jax.ShapeDtypeStruct((T, CD), F32), jax.ShapeDtypeStruct((T, CD), BF16),
                   jax.ShapeDtypeStruct((T, D), F32), jax.ShapeDtypeStruct((T, D), BF16),
                   jax.ShapeDtypeStruct((T, D), BF16), jax.ShapeDtypeStruct((T, D), F32)]
        + [jax.ShapeDtypeStruct((8, LANE), F32), jax.ShapeDtypeStruct((1, D), F32)] * extra,
        compiler_params=_cp(("arbitrary",), VMEM_BIG),
    )(c1, u, ya, h, p, lg, lb, wpw, cg, wout, pg, wgate, wple, *(head or ()))


def _ple_out_bwd(dh2, h1, gate, pe, p, ya, yc, pg, wgate, wout):
    tm = min(TR, T)

    def body(dh2_ref, h1_ref, gate_ref, pe_ref, p_ref, ya_ref, yc_ref, pg_ref, wgate_ref, wout_ref,
             dh1_ref, dy_ref, dwg_ref, dwp_ref, dwo_ref, dpg_ref):
        @pl.when(pl.program_id(0) == 0)
        def _():
            dwg_ref[...] = jnp.zeros_like(dwg_ref)
            dwp_ref[...] = jnp.zeros_like(dwp_ref)
            dwo_ref[...] = jnp.zeros_like(dwo_ref)
            dpg_ref[...] = jnp.zeros_like(dpg_ref)
        dh2 = dh2_ref[...]
        h1 = h1_ref[...]
        gate = gate_ref[...].astype(F32)
        pg = pg_ref[...]
        dpe = (dh2 * gate).astype(BF16)
        dgp = (dh2 * pe_ref[...].astype(F32) * gate * (1.0 - gate)).astype(BF16)
        r = _rstd(h1)
        hn = h1 * r
        dwg_ref[...] += _dot_tn((hn * pg).astype(BF16), dgp)
        dhn2 = _dot_nt(dgp, wgate_ref[...])
        dpg_ref[...] += jnp.sum(dhn2 * hn, axis=0, keepdims=True)
        dh1 = dh2 + _rms_bwd(dhn2, h1, r, pg)
        pb = p_ref[...].astype(BF16)
        for k in range(NCHIP):
            dwp_ref[k] += _dot_tn(pb, dpe[:, k * PLE:(k + 1) * PLE])
        dh1b = dh1.astype(BF16)
        dy_ref[...] = _dot_nt(dh1b, wout_ref[...])
        dwo_ref[pl.ds(0, AD), :] += _dot_tn(ya_ref[...], dh1b)
        dwo_ref[pl.ds(AD, CD), :] += _dot_tn(yc_ref[...], dh1b)
        dh1_ref[...] = dh1

    row = lambda w: pl.BlockSpec((tm, w), lambda i: (i, 0))
    full = lambda *s: pl.BlockSpec(s, lambda i: (0,) * len(s), pipeline_mode=pl.Buffered(1))
    return pl.pallas_call(
        body, name="ple_out_bwd", grid=(T // tm,),
        in_specs=[row(D), row(D), row(D), row(D), row(PLE), row(AD), row(CD),
                  full(1, D), full(D, D), full(D, D)],
        out_specs=[row(D), row(D), full(D, D), full(NCHIP, PLE, PLE), full(D, D), full(1, D)],
        out_shape=[jax.ShapeDtypeStruct((T, D), F32), jax.ShapeDtypeStruct((T, D), F32),
                   jax.ShapeDtypeStruct((D, D), F32), jax.ShapeDtypeStruct((NCHIP, PLE, PLE), F32),
                   jax.ShapeDtypeStruct((D, D), F32), jax.ShapeDtypeStruct((1, D), F32)],
        compiler_params=_cp(("arbitrary",), VMEM_BIG),
    )(dh2, h1, gate, pe, p, ya, yc, pg, wgate, wout)


def _branch_bwd(dy, o, u, c1, c3, ag, lg, lb, wpw, cg, seg):
    tm = min(TR, T)

    def body(dya_ref, dyc_ref, o_ref, ga_ref, gc_ref, c1_ref, c3_ref, ag_ref, lg_ref, lb_ref, wpw_ref,
             cg_ref, seg_ref, do_ref, dga_ref, dgc_ref, dc1_ref, dwpw_ref, dag_ref, dcg_ref, dlg_ref, dlb_ref):
        @pl.when(pl.program_id(0) == 0)
        def _():
            for r_ in (dwpw_ref, dag_ref, dcg_ref, dlg_ref, dlb_ref):
                r_[...] = jnp.zeros_like(r_)
        dya = dya_ref[...]
        o = o_ref[...]
        ga = ga_ref[...]
        ag_v = ag_ref[...]
        seg_m = seg_ref[...]
        r = lax.rsqrt(_dot2(o * o, seg_m) * (1.0 / DH) + EPS)
        onr = o * r
        sg = _sig(ga)
        dga_ref[...] = (dya * (onr * ag_v) * (sg * (1.0 + ga * (1.0 - sg)))).astype(BF16)
        don = dya * (ga * sg)
        dag_ref[...] += jnp.sum(don * onr, axis=0, keepdims=True)
        dn = don * ag_v
        do_ref[...] = r * dn - o * (r * r * r) * (_dot2(dn * o, seg_m) * (1.0 / DH))
        dyc = dyc_ref[...]
        c3 = c3_ref[...]
        gc = gc_ref[...]
        cg_v = cg_ref[...]
        r3 = _rstd(c3)
        cn = c3 * r3
        sc = _sig(gc)
        dgc_ref[...] = (dyc * (cn * cg_v) * (sc * (1.0 + gc * (1.0 - sc)))).astype(BF16)
        dcn = dyc * (gc * sc)
        dcg_ref[...] += jnp.sum(dcn * cn, axis=0, keepdims=True)
        dc3 = _rms_bwd(dcn, c3, r3, cg_v).astype(BF16)
        lg_v = lg_ref[...]
        xh, rs, ln, s = _ln_silu(c1_ref[...], lg_v, lb_ref[...])
        c2 = (ln * s).astype(BF16)
        dwpw_ref[...] += _dot_tn(c2, dc3)
        dc2 = _dot_nt(dc3, wpw_ref[...])
        dln = dc2 * (s * (1.0 + ln * (1.0 - s)))
        dlb_ref[...] += jnp.sum(dln, axis=0, keepdims=True)
        dlg_ref[...] += jnp.sum(dln * xh, axis=0, keepdims=True)
        dxh = dln * lg_v
        dc1_ref[...] = rs * (dxh - jnp.mean(dxh, axis=-1, keepdims=True)
                             - xh * jnp.mean(dxh * xh, axis=-1, keepdims=True))

    half = lambda j: pl.BlockSpec((tm, 512), lambda i: (i, j))
    full = lambda *s: pl.BlockSpec(s, lambda i: (0,) * len(s), pipeline_mode=pl.Buffered(1))
    vec = jax.ShapeDtypeStruct((1, 512), F32)
    act = jax.ShapeDtypeStruct((T, 512), F32)
    return pl.pallas_call(
        body, name="branch_bwd", grid=(T // tm,),
        in_specs=[half(0), half(1), half(0), half(3), half(6), half(0), half(0),
                  full(1, AD), full(1, CD), full(1, CD), full(CD, CD), full(1, CD), full(AD, AD)],
        out_specs=[half(0), half(0), half(0), half(0), full(CD, CD), full(1, 512), full(1, 512),
                   full(1, 512), full(1, 512)],
        out_shape=[act, jax.ShapeDtypeStruct((T, 512), BF16), jax.ShapeDtypeStruct((T, 512), BF16), act,
                   jax.ShapeDtypeStruct((CD, CD), F32), vec, vec, vec, vec],
        compiler_params=_cp(("arbitrary",), VMEM_BIG),
    )(dy, dy, o, u, u, c1, c3, ag, lg, lb, wpw, cg, seg)


def _conv_bwd(dc1, u, dw, grads=()):
    tr = 64
    n_x = len(grads)
    grid = (CD // LANE,)
    off = CWP - CW + 1

    def body(d_ref, cv_ref, cg_ref, w_ref, dcv_ref, dcg_ref, ddw_ref, ddb_ref, padc_s, padd_s, acc_s):
        cv = cv_ref[...]
        sg = _sig(cg_ref[...])
        padc_s[pl.ds(0, CWP), :] = jnp.zeros((CWP, LANE), F32)
        padc_s[pl.ds(CWP, T), :] = cv * sg
        padd_s[pl.ds(0, T), :] = d_ref[...]
        padd_s[pl.ds(T, CWP), :] = jnp.zeros((CWP, LANE), F32)
        acc_s[...] = jnp.zeros_like(acc_s)
        wv = w_ref[0]

        def tile(i, carry):
            r0 = pl.multiple_of(i * tr, tr)
            dt = padd_s[pl.ds(r0, tr), :]
            dc0 = jnp.zeros((tr, LANE), F32)
            for w in range(CW):
                dc0 = dc0 + padd_s[pl.ds(r0 + (CW - 1) - w, tr), :] * wv[w:w + 1, :]
                prod = dt * padc_s[pl.ds(r0 + off + w, tr), :]
                acc_s[w] += jnp.sum(prod.reshape(tr // 8, 8, LANE), axis=0)
            cvt = cv_ref[pl.ds(r0, tr), :]
            sgt = _sig(cg_ref[pl.ds(r0, tr), :])
            dcv_ref[pl.ds(r0, tr), :] = (dc0 * sgt).astype(BF16)
            dcg_ref[pl.ds(r0, tr), :] = (dc0 * cvt * sgt * (1.0 - sgt)).astype(BF16)
            return carry

        lax.fori_loop(0, T // tr, tile, 0)
        ddw_ref[0] = jnp.sum(acc_s[...], axis=1)
        ddb_ref[...] = jnp.sum(d_ref[...], axis=0, keepdims=True)

    col = lambda j: pl.BlockSpec((T, LANE), lambda cb: (0, j + cb))
    res = pl.pallas_call(
        _host(body, grid, 4, 4, n_x, _pair_copies), name="conv_bwd", grid=grid,
        in_specs=[col(0), col(16), col(20), pl.BlockSpec((1, CWP, LANE), lambda cb: (cb, 0, 0))] + [HBM_SPEC] * n_x,
        out_specs=[col(0), col(0), pl.BlockSpec((1, CWP, LANE), lambda cb: (cb, 0, 0)),
                   pl.BlockSpec((1, LANE), lambda cb: (0, cb))] + [HBM_SPEC] * n_x,
        out_shape=[jax.ShapeDtypeStruct((T, CD), BF16), jax.ShapeDtypeStruct((T, CD), BF16),
                   jax.ShapeDtypeStruct((NCHIP, CWP, LANE), F32), jax.ShapeDtypeStruct((1, CD), F32)]
        + [jax.ShapeDtypeStruct((NCHIP, g.shape[1] // 2, g.shape[2]), F32) for g in grads],
        scratch_shapes=[pltpu.VMEM((T + CWP, LANE), F32), pltpu.VMEM((T + CWP, LANE), F32),
                        pltpu.VMEM((CWP, 8, LANE), F32)] + _hosted_sems(n_x),
        compiler_params=_cp(("arbitrary",)),
    )(dc1, u, u, dw, *grads)
    return res[0], res[1], res[2], res[3], list(res[4:])


def _attn_bwd(u, do, tot, partials=()):
    n_x = len(partials)
    grid = (T // AQ,)

    def body(q_ref, k_ref, v_ref, do_ref, tot_ref, dq_ref, dk_ref, dv_ref, kb_s, vb_s, dk_s, dv_s):
        qi = pl.program_id(0)

        @pl.when(qi == 0)
        def _():
            kb_s[...] = k_ref[...].astype(BF16)
            vb_s[...] = v_ref[...].astype(BF16)
            dk_s[...] = jnp.zeros_like(dk_s)
            dv_s[...] = jnp.zeros_like(dv_s)

        causal, tr, tc, heads = _attn_tiles()
        upper = (tr > tc).astype(BF16)
        lower = (tr < tc).astype(BF16)
        qs, qus, dos, tots = [], [], [], []
        for g in range(NG):
            lanes = slice(g * GW, (g + 1) * GW)
            q = q_ref[:, lanes]
            qs.append(_stack_heads(q * 0.125, heads).astype(BF16))
            qus.append(_stack_heads(q, heads).astype(BF16))
            dos.append(_stack_heads(do_ref[:, lanes], heads).astype(BF16))
            totv = tot_ref[g]
            tots.append(jnp.concatenate([totv[:, h * DH:h * DH + 1] for h in range(HG)], axis=0))

        def block(kb, carry, masked):
            k0 = pl.multiple_of(kb * AQ, AQ)
            out = []
            for g in range(NG):
                lanes = pl.ds(g * GW, GW)
                lm_left, dl_left, dq = carry[g]
                kk = kb_s[pl.ds(k0, AQ), lanes]
                vv = vb_s[pl.ds(k0, AQ), lanes]
                z = _dot_nt(qs[g], kk)
                sp = _softplus(z)
                lm = jnp.where(causal, -sp, 0.0) if masked else -sp
                lm_incl = lm_left + jnp.sum(lm, axis=1, keepdims=True)
                att = jnp.exp((z - sp) + _tri_sum(lm, upper) + (tots[g] - lm_incl))
                if masked:
                    att = jnp.where(causal, att, 0.0)
                dl = att * _dot_nt(dos[g], vv)
                dv_s[pl.ds(k0, AQ), lanes] += _dot_tn(att.astype(BF16), dos[g])
                prefix = dl_left + _tri_sum(dl, lower)
                beta = jnp.exp(z - sp)
                dz = (1.0 - beta) * dl - beta * prefix
                if masked:
                    dz = jnp.where(causal, dz, 0.0)
                dzs = (dz * 0.125).astype(BF16)
                dk_s[pl.ds(k0, AQ), lanes] += _dot_tn(dzs, qus[g])
                out.append((lm_incl, dl_left + jnp.sum(dl, axis=1, keepdims=True), dq + _dot(dzs, kk)))
            return tuple(out)

        zero = jnp.zeros((SR, 1), F32)
        init = tuple((zero, zero, jnp.zeros((SR, GW), F32)) for _ in range(NG))
        carry = lax.fori_loop(0, qi, lambda kb, c: block(kb, c, False), init)
        carry = block(qi, carry, True)
        for g in range(NG):
            dq_ref[:, g * GW:(g + 1) * GW] = _unstack_heads(carry[g][2], heads).astype(BF16)

        @pl.when(qi == grid[0] - 1)
        def _():
            dk_ref[...] = dk_s[...].astype(BF16)
            dv_ref[...] = dv_s[...].astype(BF16)

    col = lambda j: pl.BlockSpec((AQ, AD), lambda qi: (qi, j))
    whole = lambda j: pl.BlockSpec((T, AD), lambda qi: (0, j), pipeline_mode=pl.Buffered(1))
    res = pl.pallas_call(
        _host(body, grid, 5, 3, n_x, _scatter_copies), name="attn_bwd", grid=grid,
        in_specs=[col(0), whole(1), whole(2), col(0), pl.BlockSpec((NG, AQ, GW), lambda qi: (0, qi, 0))]
        + [HBM_SPEC] * n_x,
        out_specs=[col(0), whole(0), whole(0)] + [HBM_SPEC] * n_x,
        out_shape=[jax.ShapeDtypeStruct((T, AD), BF16)] * 3
        + [jax.ShapeDtypeStruct((NCHIP - 1,) + a.shape[1:], a.dtype) for a in partials],
        scratch_shapes=[pltpu.VMEM((T, AD), BF16), pltpu.VMEM((T, AD), BF16), pltpu.VMEM((T, AD), F32),
                        pltpu.VMEM((T, AD), F32)] + _hosted_sems(n_x),
        compiler_params=_cp(("arbitrary",), VMEM_BIG),
    )(u, u, u, do, tot, *partials)
    return res[0], res[1], res[2], list(res[3:])


def _inproj_dw(hn, du):
    tm = T

    def body(hn_ref, du_ref, dw_ref):
        @pl.when(pl.program_id(1) == 0)
        def _():
            dw_ref[...] = jnp.zeros_like(dw_ref)
        dw_ref[0] += _dot_tn(hn_ref[...], du_ref[...])

    return pl.pallas_call(
        body, name="inproj_dw", grid=(NCHIP, T // tm),
        in_specs=[pl.BlockSpec((tm, D), lambda k, i: (i, 0)), pl.BlockSpec((tm, SHW), lambda k, i: (i, k))],
        out_specs=pl.BlockSpec((1, D, SHW), lambda k, i: (k, 0, 0)),
        out_shape=jax.ShapeDtypeStruct((NCHIP, D, SHW), F32),
        compiler_params=_cp(("arbitrary", "arbitrary"), VMEM_BIG),
    )(hn, du)


def _inproj_dx(du, w, h, g, dres, partials=(), grads=(), after=None):
    tm = min(TM, T)
    sent = list(partials) + list(grads)
    n_x = len(sent)
    assert after is None or not sent
    grid = (T // tm, NCHIP)
    if grads:
        landing = [jax.ShapeDtypeStruct((NCHIP, a.shape[1] // 2, a.shape[2]), F32) for a in grads]
    else:
        landing = [jax.ShapeDtypeStruct((NCHIP - 1,) + a.shape[1:], a.dtype) for a in partials]

    def body(du_ref, w_ref, h_ref, g_ref, dres_ref, dh_ref, dg_ref, acc_s):
        i, k = pl.program_id(0), pl.program_id(1)

        @pl.when(jnp.logical_and(i == 0, k == 0))
        def _():
            dg_ref[...] = jnp.zeros_like(dg_ref)

        @pl.when(k == 0)
        def _():
            acc_s[...] = _dot_nt(du_ref[...], w_ref[0])

        @pl.when(k > 0)
        def _():
            acc_s[...] += _dot_nt(du_ref[...], w_ref[0])

        @pl.when(k == NCHIP - 1)
        def _():
            hh = h_ref[...]
            r = _rstd(hh)
            dhn = acc_s[...]
            dg_ref[...] += jnp.sum(dhn * hh * r, axis=0, keepdims=True)
            dh_ref[...] = dres_ref[...] + _rms_bwd(dhn, hh, r, g_ref[...])

    behind = [] if after is None else [after]
    hosting = _host(body, grid, 5, 2, n_x, _pair_copies if grads else _scatter_copies)
    res = pl.pallas_call(
        (lambda *refs: body(*refs[:5], *refs[6:])) if behind else hosting, name="inproj_dx", grid=grid,
        in_specs=[pl.BlockSpec((tm, SHW), lambda i, k: (i, k)),
                  pl.BlockSpec((1, D, SHW), lambda i, k: (k, 0, 0)),
                  pl.BlockSpec((tm, D), lambda i, k: (i, 0)),
                  pl.BlockSpec((1, D), lambda i, k: (0, 0)),
                  pl.BlockSpec((tm, D), lambda i, k: (i, 0))] + [HBM_SPEC] * n_x
        + [pl.BlockSpec(memory_space=pl.ANY)] * len(behind),
        out_specs=[pl.BlockSpec((tm, D), lambda i, k: (i, 0)), pl.BlockSpec((1, D), lambda i, k: (0, 0))]
        + [HBM_SPEC] * n_x,
        out_shape=[jax.ShapeDtypeStruct((T, D), F32), jax.ShapeDtypeStruct((1, D), F32)] + landing,
        scratch_shapes=[pltpu.VMEM((tm, D), F32)] + _hosted_sems(n_x),
        compiler_params=_cp(("arbitrary", "arbitrary"), VMEM_BIG),
    )(du, w, h, g, dres, *sent, *behind)
    return res[0], res[1], list(res[2:])


def _sum_pair(core, grads, gots):
    n = len(grads)

    def body(c_ref, *refs):
        for a in range(n):
            refs[2 * n + a][...] = (refs[a][...] + refs[n + a][...]).astype(BF16)

    mine = [pl.BlockSpec((1,) + s.shape[1:], lambda k, c: (k, c[0], 0)) for s in gots]
    same = [pl.BlockSpec((1,) + s.shape[1:], lambda k, c: (k, 0, 0)) for s in gots]
    return pl.pallas_call(
        body, name="sum_pair",
        grid_spec=pltpu.PrefetchScalarGridSpec(
            num_scalar_prefetch=1, grid=(NCHIP,), in_specs=mine + same, out_specs=same),
        out_shape=[jax.ShapeDtypeStruct(s.shape, BF16) for s in gots],
        compiler_params=_cp(("arbitrary",), VMEM_BIG),
    )(core, *grads, *gots)


def _sum_chips_share(chip, partials, gots):
    flat_p = [p for layer in partials for p in layer]
    flat_g = [g for layer in gots for g in layer]
    n, per_layer = len(flat_p), len(partials[0])

    def body(c_ref, *refs):
        full, sums = refs[2 * n:2 * n + per_layer], refs[2 * n + per_layer:3 * n + per_layer]
        send_sems, recv_sems, local_sems = refs[3 * n + per_layer:]
        x, y, c = _place()
        copies = []
        for i in range(n):
            acc = refs[i][0].astype(F32)
            for j in range(NCHIP - 1):
                acc = acc + refs[n + i][j].astype(F32)
            sums[i][...] = acc
            half = flat_p[i].shape[1]
            rows = full[i % per_layer].at[i // per_layer, pl.ds(c * half, half)]
            copies.append(pltpu.make_async_copy(sums[i], rows, local_sems.at[i]))
            copies.append(pltpu.make_async_remote_copy(
                src_ref=sums[i], dst_ref=rows, send_sem=send_sems.at[i], recv_sem=recv_sems.at[i],
                device_id=(x, y, 1 - c), device_id_type=MESH))
        for cp in copies:
            cp.start()
        for cp in copies:
            cp.wait()

    return pl.pallas_call(
        body, name="sum_chips_share",
        grid_spec=pltpu.PrefetchScalarGridSpec(
            num_scalar_prefetch=1, grid=(1,),
            in_specs=[pl.BlockSpec((1,) + s.shape[1:], lambda i, c: (c[0], 0, 0)) for s in flat_p]
            + [pl.BlockSpec(s.shape, lambda i, c: (0, 0, 0)) for s in flat_g],
            out_specs=[HBM_SPEC] * per_layer,
            scratch_shapes=[pltpu.VMEM(s.shape[1:], F32) for s in flat_p]
            + [pltpu.SemaphoreType.DMA((n,)), pltpu.SemaphoreType.DMA((n,)), pltpu.SemaphoreType.DMA((n,))]),
        out_shape=[jax.ShapeDtypeStruct((len(partials), 2 * s.shape[1], s.shape[2]), F32) for s in partials[0]],
        compiler_params=_cp(("arbitrary",), VMEM_BIG),
    )(chip, *flat_p, *flat_g)


def _adam_math(w, g, m, v):
    nm = ADAM_B1 * m + (1.0 - ADAM_B1) * g
    nv = ADAM_B2 * v + (1.0 - ADAM_B2) * (g * g)
    m_hat = nm / (1.0 - ADAM_B1 ** ADAM_STEP)
    v_hat = nv / (1.0 - ADAM_B2 ** ADAM_STEP)
    return -ADAM_LR * (m_hat / (jnp.sqrt(v_hat) + ADAM_EPS) + ADAM_WD * w), nm, nv


def _adamw(w, g, m, v, rows):
    R, C = w.shape

    def body(w_ref, g_ref, m_ref, v_ref, g_out, d_ref, nm_ref, nv_ref):
        g = g_ref[...]
        g_out[...] = g
        d_ref[...], nm_ref[...], nv_ref[...] = _adam_math(w_ref[...], g, m_ref[...], v_ref[...])

    spec = pl.BlockSpec((rows, C), lambda i: (i, 0))
    sh = jax.ShapeDtypeStruct((R, C), F32)
    return pl.pallas_call(
        body, name="adamw", grid=(R // rows,), in_specs=[spec] * 4, out_specs=[spec] * 4,
        out_shape=[sh, sh, sh, sh], compiler_params=_cp(("arbitrary",)),
    )(w, g, m, v)


def _adamw_taps(w, g, m, v):
    def body(w_ref, g_ref, m_ref, v_ref, g_out, d_ref, nm_ref, nv_ref):
        g = g_ref[:, pl.ds(0, CW), :]
        g_out[...] = g
        d_ref[...], nm_ref[...], nv_ref[...] = _adam_math(w_ref[...], g, m_ref[...], v_ref[...])

    whole = lambda a: pl.BlockSpec(a.shape, lambda i: (0, 0, 0))
    return pl.pallas_call(
        body, name="adamw_taps", grid=(1,), in_specs=[whole(w), whole(g), whole(w), whole(w)],
        out_specs=[whole(w)] * 4, out_shape=[jax.ShapeDtypeStruct(w.shape, F32)] * 4,
    )(w, g, m, v)


def _small_adamw(tot, ws, ms, vs):
    n = len(ws)

    def body(*refs):
        tot_ref = refs[0]
        w_refs, m_refs, v_refs = refs[1:1 + n], refs[1 + n:1 + 2 * n], refs[1 + 2 * n:1 + 3 * n]
        outs = refs[1 + 3 * n:]
        for i in range(n):
            rows, width = ws[i].shape
            g = tot_ref[pl.ds(SMALL_ROW[i], rows), pl.ds(0, width)]
            outs[4 * i][...] = g
            outs[4 * i + 1][...], outs[4 * i + 2][...], outs[4 * i + 3][...] = _adam_math(
                w_refs[i][...], g, m_refs[i][...], v_refs[i][...])
        outs[4 * n][...] = tot_ref[pl.ds(LOSS_ROW, 1), pl.ds(0, LANE)]

    vmem = pl.BlockSpec(memory_space=pltpu.VMEM)
    res = pl.pallas_call(
        body, name="small_adamw", in_specs=[vmem] * (1 + 3 * n), out_specs=[vmem] * (4 * n + 1),
        out_shape=[jax.ShapeDtypeStruct(w.shape, F32) for w in ws for _ in range(4)]
        + [jax.ShapeDtypeStruct((1, LANE), F32)],
    )(tot, *ws, *ms, *vs)
    return [res[4 * i:4 * i + 4] for i in range(n)], res[4 * n]


HBM_SPEC = pl.BlockSpec(memory_space=pltpu.HBM)


def _place():
    return lax.axis_index("x"), lax.axis_index("y"), lax.axis_index("c")


def _all_gather_split(shard):
    def body(in_ref, out_ref, send_sems, recv_sems):
        direct, relayed, passed = _gather_tree_copies([in_ref], [out_ref], send_sems, recv_sems)
        for cp in direct:
            cp.start()
        for i in range(2):
            direct[i].wait_recv()
            relayed[i].start()
            passed[i].start()
        for cp in relayed:
            cp.wait_recv()
        passed[2].start()
        for cp in passed + direct[2:]:
            cp.wait_recv()
        for cp in direct + relayed + passed:
            cp.wait_send()

    return pl.pallas_call(
        body, name="all_gather_split", in_specs=[HBM_SPEC], out_specs=HBM_SPEC,
        out_shape=jax.ShapeDtypeStruct((NCHIP,) + shard.shape, shard.dtype),
        scratch_shapes=_hosted_sems(1, GATHER_SEMS),
    )(shard)


SEM_SPEC = pl.BlockSpec(memory_space=pltpu.SEMAPHORE)
ORDERED_EFFECT = pltpu.CompilerParams(has_side_effects=pltpu.SideEffectType.DATAFLOW_SIDE_EFFECTING)


def _scatter_start(partial):
    land = pltpu.with_memory_space_constraint(
        lax.empty((NCHIP - 1,) + partial.shape[1:], partial.dtype), pltpu.HBM)

    def body(p_ref, land_ref, send_sems, recv_sems, p_thru, land_thru, token):
        for cp in _scatter_copies([p_ref], [land_ref], send_sems, recv_sems):
            cp.start()
        token[...] = jnp.zeros_like(token)

    return pl.pallas_call(
        body, name="scatter_start",
        out_shape=(pltpu.SemaphoreType.DMA((NCHIP - 1,)), pltpu.SemaphoreType.DMA((NCHIP - 1,)),
                   pltpu.HBM(partial.shape, partial.dtype), pltpu.HBM(land.shape, land.dtype),
                   jax.ShapeDtypeStruct((8, LANE), F32)),
        in_specs=(HBM_SPEC, HBM_SPEC),
        out_specs=(SEM_SPEC, SEM_SPEC, HBM_SPEC, HBM_SPEC, pl.BlockSpec(memory_space=pltpu.VMEM)),
        input_output_aliases={0: 2, 1: 3}, compiler_params=ORDERED_EFFECT,
    )(pltpu.with_memory_space_constraint(partial, pltpu.HBM), land)


def _scatter_wait(send_sems, recv_sems, p_thru, land_thru, after):
    def body(p_ref, land_ref, send_sems, recv_sems, after_ref, p_dead, got_ref):
        for cp in _scatter_copies([p_ref], [land_ref], send_sems, recv_sems):
            cp.wait_send()
            cp.wait_recv()

    return pl.pallas_call(
        body, name="scatter_wait",
        out_shape=(pltpu.HBM(p_thru.shape, p_thru.dtype), pltpu.HBM(land_thru.shape, land_thru.dtype)),
        in_specs=(HBM_SPEC, HBM_SPEC, SEM_SPEC, SEM_SPEC, pl.BlockSpec(memory_space=pl.ANY)),
        out_specs=(HBM_SPEC, HBM_SPEC), input_output_aliases={0: 0, 1: 1}, compiler_params=ORDERED_EFFECT,
    )(p_thru, land_thru, send_sems, recv_sems, after)


def _pair_exchange(grads):
    n = len(grads)

    def body(*refs):
        copies = _pair_copies(refs[:n], refs[n:2 * n], refs[2 * n], refs[2 * n + 1])
        for cp in copies:
            cp.start()
        for cp in copies:
            cp.wait()

    return pl.pallas_call(
        body, name="pair_exchange", in_specs=[HBM_SPEC] * n, out_specs=[HBM_SPEC] * n,
        out_shape=[jax.ShapeDtypeStruct((NCHIP, g.shape[1] // 2, g.shape[2]), F32) for g in grads],
        scratch_shapes=[pltpu.SemaphoreType.DMA((n,)), pltpu.SemaphoreType.DMA((n,))],
    )(*grads)


def _small_allreduce(rows, loss_blk, per_head=()):
    n = len(rows)

    def body(*refs):
        loss_ref, o_ref, pk, slots, send_sems, recv_sems = refs[n:]
        pk[...] = jnp.zeros_like(pk)
        for i in range(n):
            row = refs[i][...]
            if i in per_head:
                row = functools.reduce(jnp.add, [row[:, h * DH:(h + 1) * DH] for h in range(AD // DH)])
            pk[pl.ds(i, 1), pl.ds(0, row.shape[1])] = row
        pk[pl.ds(LOSS_ROW, 1), pl.ds(0, LANE)] = loss_ref[pl.ds(0, 1), :]
        x, y, c = _place()
        me = 4 * x + 2 * y + c
        slots[me] = pk[...]
        copies = []
        for r in range(1, 8):
            rx, ry, rc = (r >> 2) & 1, (r >> 1) & 1, r & 1
            peer = (x + rx - 2 * x * rx, y + ry - 2 * y * ry, c + rc - 2 * c * rc)
            cp = pltpu.make_async_remote_copy(
                src_ref=pk, dst_ref=slots.at[me], send_sem=send_sems.at[r - 1], recv_sem=recv_sems.at[r - 1],
                device_id=peer, device_id_type=MESH)
            cp.start()
            copies.append(cp)
        for cp in copies:
            cp.wait()
        acc = slots[0]
        for j in range(1, 8):
            acc = acc + slots[j]
        o_ref[...] = acc

    vmem = pl.BlockSpec(memory_space=pltpu.VMEM)
    return pl.pallas_call(
        body, name="small_allreduce", in_specs=[vmem] * (n + 1), out_specs=vmem,
        out_shape=jax.ShapeDtypeStruct((SMALL_PK, D), F32),
        scratch_shapes=[pltpu.VMEM((SMALL_PK, D), F32), pltpu.VMEM((8, SMALL_PK, D), F32),
                        pltpu.SemaphoreType.DMA((7,)), pltpu.SemaphoreType.DMA((7,))],
    )(*rows, loss_blk)


def _seg_matrix():
    i = lax.broadcasted_iota(jnp.int32, (AD, AD), 0) // DH
    j = lax.broadcasted_iota(jnp.int32, (AD, AD), 1) // DH
    return (i == j).astype(BF16)


TAIL = ("w_out", "w_ple_gate", "w_ple", "w_pw", "dw_w")


def _local_step(x, p, tgt, sm, shards, chip, ci):
    seg = _seg_matrix()
    core = jnp.reshape(ci, (1,)).astype(jnp.int32)
    chip_idx = jnp.reshape(chip, (1,)).astype(jnp.int32)
    w_in_next = _all_gather_split(shards[0]["w_in"])
    h = x
    saved = []
    for l in range(DEPTH):
        w_in = w_in_next
        row = lambda name: sm[name][l:l + 1]
        u, hn = _rms_inproj(h, row("norm_g"), w_in)
        todo = [shards[l][k] for k in TAIL] + ([shards[l + 1]["w_in"]] if l + 1 < DEPTH else [])
        o, ya, tot, got = _attn_fwd(u, jnp.tile(row("attn_out_g"), (1, AD // DH)), todo)
        w_out = got[0].reshape(D, D)
        w_gate = got[1].reshape(D, D)
        w_ple = got[2]
        w_pw = got[3].reshape(CD, CD)
        dw = got[4]
        if l + 1 < DEPTH:
            w_in_next = got[5]
        c1 = _glu_conv(u, dw, row("dw_b"))
        c3, yc, h1, gate, pe, h2, *at_end = _layer_tail(
            c1, u, ya, h, p[l], row("conv_ln_g"), row("conv_ln_b"), w_pw, row("conv_out_g"), w_out,
            row("ple_norm_g"), w_gate, w_ple, head=(tgt, sm["final_g"]) if l == DEPTH - 1 else None)
        saved.append(dict(h=h, u=u, hn=hn, o=o, ya=ya, tot=tot, c1=c1, c3=c3, yc=yc, h1=h1, gate=gate, pe=pe,
                          w_in=w_in, w_out=w_out, w_gate=w_gate, w_pw=w_pw, dw=dw))
        h = h2
    dh, (loss_blk, dfg) = h, at_end
    small = [None] * DEPTH
    pending, partials, arrived = [], {}, {}
    pair_sum = lambda grads: _sum_pair(core, grads, _pair_exchange(grads))
    for l in reversed(range(DEPTH)):
        s = saved[l]
        row = lambda name: sm[name][l:l + 1]
        dh1, dy, dwg, dwp, dwo, dpg = _ple_out_bwd(
            dh, s["h1"], s["gate"], s["pe"], p[l], s["ya"], s["yc"], row("ple_norm_g"), s["w_gate"], s["w_out"])
        ag_t = jnp.tile(row("attn_out_g"), (1, AD // DH))
        do, dga, dgc, dc1, dwpw, dag, dcg, dlg, dlb = _branch_bwd(
            dy, s["o"], s["u"], s["c1"], s["c3"], ag_t, row("conv_ln_g"), row("conv_ln_b"), s["w_pw"],
            row("conv_out_g"), seg)
        tail = [dwo.reshape(NCHIP, 256, D), dwg.reshape(NCHIP, 256, D), dwp, dwpw.reshape(NCHIP, 128, CD)]
        dcv, dcgate, ddw, ddb, halves = _conv_bwd(dc1, s["u"], s["dw"], tail if l == 0 else ())
        tail.append(ddw)
        if l == 0:
            partials[(l, "tail")] = _sum_pair(core, tail, halves + list(_pair_exchange([ddw])))
            pending.append((l, "tail"))
        send = [t for key in pending for t in partials[key]]
        dq, dk, dv, got = _attn_bwd(s["u"], do, s["tot"], send)
        for key in pending:
            arrived[key], got = got[:len(partials[key])], got[len(partials[key]):]
        pending = []
        du = jnp.concatenate([dq, dk, dv, dga, dcv, dcgate, dgc], axis=1)
        dwin = _inproj_dw(s["hn"], du)
        if l == 0:
            in_flight = _scatter_start(pair_sum([dwin])[0])
            dh, dng, _ = _inproj_dx(du, s["w_in"], s["h"], row("norm_g"), dh1, after=in_flight[4])
            small[l] = dict(norm_g=dng, attn_out_g=dag, dw_b=ddb, conv_ln_g=dlg, conv_ln_b=dlb,
                            conv_out_g=dcg, ple_norm_g=dpg)
            per_head = tuple(DEPTH * SMALL2.index("attn_out_g") + j for j in range(DEPTH))
            small_tot = _small_allreduce([small[j][k] for k in SMALL2 for j in range(DEPTH)] + [dfg], loss_blk,
                                         per_head)
            landed = _scatter_wait(*in_flight[:4], small_tot)
            partials[(l, "w_in")], arrived[(l, "w_in")] = [landed[0]], [landed[1]]
        else:
            dh, dng, halves = _inproj_dx(du, s["w_in"], s["h"], row("norm_g"), dh1, grads=tail + [dwin])
            tail_p = _sum_pair(core, tail + [dwin], halves)
            partials[(l, "tail")], partials[(l, "w_in")] = tail_p[:-1], tail_p[-1:]
            pending = [(l, "tail"), (l, "w_in")]
            small[l] = dict(norm_g=dng, attn_out_g=dag, dw_b=ddb, conv_ln_g=dlg, conv_ln_b=dlb,
                            conv_out_g=dcg, ple_norm_g=dpg)
    both = lambda d: [list(d[(l, "w_in")]) + list(d[(l, "tail")]) for l in range(DEPTH)]
    big = dict(zip(BIG, _sum_chips_share(chip_idx, both(partials), both(arrived))))
    return dh, big, small_tot


BIG = ("w_in", "w_out", "w_ple_gate", "w_ple", "w_pw", "dw_w")
SMALL2 = ("norm_g", "ple_norm_g", "dw_b", "conv_ln_g", "conv_ln_b", "conv_out_g", "attn_out_g")
SMALL_ROW = (0, 2, 4, 6, 8, 10, 12, 14)


def kernel(x, p, norm_g, w_in, attn_out_g, dw_w, dw_b, conv_ln_g, conv_ln_b, w_pw, conv_out_g, w_out, ple_norm_g, w_ple_gate, w_ple, final_g, loss_target, m_norm_g, m_w_in, m_attn_out_g, m_dw_w, m_dw_b, m_conv_ln_g, m_conv_ln_b, m_w_pw, m_conv_out_g, m_w_out, m_ple_norm_g, m_w_ple_gate, m_w_ple, m_final_g, v_norm_g, v_w_in, v_attn_out_g, v_dw_w, v_dw_b, v_conv_ln_g, v_conv_ln_b, v_w_pw, v_conv_out_g, v_w_out, v_ple_norm_g, v_w_ple_gate, v_w_ple, v_final_g):
    W = dict(norm_g=norm_g, w_in=w_in, attn_out_g=attn_out_g, dw_w=dw_w, dw_b=dw_b, conv_ln_g=conv_ln_g,
             conv_ln_b=conv_ln_b, w_pw=w_pw, conv_out_g=conv_out_g, w_out=w_out, ple_norm_g=ple_norm_g,
             w_ple_gate=w_ple_gate, w_ple=w_ple, final_g=final_g)
    M = dict(norm_g=m_norm_g, w_in=m_w_in, attn_out_g=m_attn_out_g, dw_w=m_dw_w, dw_b=m_dw_b,
             conv_ln_g=m_conv_ln_g, conv_ln_b=m_conv_ln_b, w_pw=m_w_pw, conv_out_g=m_conv_out_g, w_out=m_w_out,
             ple_norm_g=m_ple_norm_g, w_ple_gate=m_w_ple_gate, w_ple=m_w_ple, final_g=m_final_g)
    V = dict(norm_g=v_norm_g, w_in=v_w_in, attn_out_g=v_attn_out_g, dw_w=v_dw_w, dw_b=v_dw_b,
             conv_ln_g=v_conv_ln_g, conv_ln_b=v_conv_ln_b, w_pw=v_w_pw, conv_out_g=v_conv_out_g, w_out=v_w_out,
             ple_norm_g=v_ple_norm_g, w_ple_gate=v_w_ple_gate, w_ple=v_w_ple, final_g=v_final_g)
    order = ("norm_g", "w_in", "attn_out_g", "dw_w", "dw_b", "conv_ln_g", "conv_ln_b", "w_pw", "conv_out_g",
             "w_out", "ple_norm_g", "w_ple_gate", "w_ple", "final_g")

    pad_taps = lambda a: jnp.pad(a, ((0, 0), (0, CWP - CW), (0, 0)))
    cast = dict(w_in=w_in.astype(BF16), w_out=w_out.astype(BF16), w_ple_gate=w_ple_gate.astype(BF16),
                w_ple=w_ple.astype(BF16), w_pw=w_pw.astype(BF16), dw_w=pad_taps(dw_w))
    shards = [{k: v[l] for k, v in cast.items()} for l in range(DEPTH)]
    xi, yi, ci = lax.axis_index("x"), lax.axis_index("y"), lax.axis_index("c")
    chip = 2 * xi + yi

    sm = {k: W[k] for k in SMALL2}
    sm["final_g"] = final_g.reshape(1, D)
    grad_x, big, small_tot = _local_step(x[0], p[:, 0], loss_target[0], sm, shards, chip, ci)
    g_big = {name: big[name].reshape(cast[name].shape) for name in BIG}

    small_names = SMALL2 + ("final_g",)
    as_rows = lambda t: t.reshape(1, D) if t.ndim == 1 else t
    results, loss_row = _small_adamw(
        small_tot, [as_rows(W[k]) for k in small_names],
        [as_rows(M[k]) for k in small_names], [as_rows(V[k]) for k in small_names])
    loss = loss_row[0, 0]

    grads, deltas, new_m, new_v = {}, {}, {}, {}
    for name in BIG:
        if name == "dw_w":
            grads[name], deltas[name], new_m[name], new_v[name] = _adamw_taps(W[name], g_big[name], M[name], V[name])
            continue
        cols = W[name].shape[-1]
        rows_total = W[name].size // cols
        flat = lambda a: a.reshape(rows_total, cols)
        four = _adamw(flat(W[name]), flat(g_big[name]), flat(M[name]), flat(V[name]), min(rows_total, 256))
        grads[name], deltas[name], new_m[name], new_v[name] = (t.reshape(W[name].shape) for t in four)
    for k, four in zip(small_names, results):
        grads[k], deltas[k], new_m[k], new_v[k] = (t.reshape(W[k].shape) for t in four)

    return (loss, grad_x[None], *[grads[n] for n in order], *[deltas[n] for n in order],
            *[new_m[n] for n in order], *[new_v[n] for n in order])
```

```python
import functools

import jax
import jax.numpy as jnp
from jax import lax
from jax.experimental import pallas as pl
from jax.experimental.pallas import tpu as pltpu

F32 = jnp.float32
BF16 = jnp.bfloat16

T = 2048
D = 1024
DIN = 3584
NCHIP = 4
SHW = DIN // NCHIP
AD = 512
CD = 512
DH = 64
CW = 31
CWP = 32
PLE = 256
DEPTH = 2
EPS = 1e-6
AQ = 256
HG = 4
GW = HG * DH
SR = HG * AQ
NG = AD // GW
LANE = 128
TM = 1024
TR = 256

ADAM_LR = 0.001
ADAM_B1 = 0.9
ADAM_B2 = 0.999
ADAM_EPS = 1e-08
ADAM_WD = 0.01
ADAM_STEP = 10

SMALL_PK = 16
LOSS_ROW = 15

VMEM_BIG = 56 * 1024 * 1024
MESH = pl.DeviceIdType.MESH


def _cp(sem=None, vmem=None):
    kw = {}
    if sem is not None:
        kw["dimension_semantics"] = sem
    if vmem is not None:
        kw["vmem_limit_bytes"] = vmem
    return pltpu.CompilerParams(**kw)


def _dot(a, b):
    return jnp.dot(a, b, preferred_element_type=F32)


def _dot_nt(a, b):
    return lax.dot_general(a, b, (((1,), (1,)), ((), ())), preferred_element_type=F32)


def _dot_tn(a, b):
    return lax.dot_general(a, b, (((0,), (0,)), ((), ())), preferred_element_type=F32)


def _dot2(x, m):
    hi = x.astype(BF16)
    lo = (x - hi.astype(F32)).astype(BF16)
    return _dot(hi, m) + _dot(lo, m)


def _sig(x):
    return 1.0 / (1.0 + jnp.exp(-x))


def _softplus(z):
    return jnp.maximum(z, 0.0) + jnp.log(1.0 + jnp.exp(-jnp.abs(z)))


def _rstd(x):
    return lax.rsqrt(jnp.mean(x * x, axis=-1, keepdims=True) + EPS)


def _rms_bwd(dy, x, r, g):
    dn = dy * g
    return r * dn - x * (r * r * r) * jnp.mean(dn * x, axis=-1, keepdims=True)


def _rms_inproj(h, g, w):
    tm = T

    def body(h_ref, g_ref, w_ref, u_ref, hn_ref, hn_s):
        @pl.when(pl.program_id(1) == 0)
        def _():
            hh = h_ref[...]
            hn = (hh * _rstd(hh) * g_ref[...]).astype(BF16)
            hn_s[...] = hn
            hn_ref[...] = hn
        u_ref[...] = _dot(hn_s[...], w_ref[0])

    return pl.pallas_call(
        body, name="rms_inproj", grid=(T // tm, NCHIP),
        in_specs=[pl.BlockSpec((tm, D), lambda i, k: (i, 0)),
                  pl.BlockSpec((1, D), lambda i, k: (0, 0)),
                  pl.BlockSpec((1, D, SHW), lambda i, k: (k, 0, 0))],
        out_specs=[pl.BlockSpec((tm, SHW), lambda i, k: (i, k)),
                   pl.BlockSpec((tm, D), lambda i, k: (i, 0))],
        out_shape=[jax.ShapeDtypeStruct((T, DIN), F32), jax.ShapeDtypeStruct((T, D), BF16)],
        scratch_shapes=[pltpu.VMEM((tm, D), BF16)],
        compiler_params=_cp(("arbitrary", "arbitrary"), VMEM_BIG),
    )(h, g, w)


def _attn_tiles():
    row = lax.broadcasted_iota(jnp.int32, (SR, AQ), 0) & (AQ - 1)
    col = lax.broadcasted_iota(jnp.int32, (SR, AQ), 1)
    tr = lax.broadcasted_iota(jnp.int32, (AQ, AQ), 0)
    tc = lax.broadcasted_iota(jnp.int32, (AQ, AQ), 1)
    lane_head = lax.broadcasted_iota(jnp.int32, (1, GW), 1) // DH
    return col < row, tr, tc, [lane_head == h for h in range(HG)]


def _stack_heads(t, heads):
    return jnp.concatenate([jnp.where(m, t, 0.0) for m in heads], axis=0)


def _unstack_heads(t, heads):
    out = t[:AQ]
    for h in range(1, HG):
        out = jnp.where(heads[h], t[h * AQ:(h + 1) * AQ], out)
    return out


def _tri_sum(x, tri):
    hi = x.astype(BF16)
    lo = (x - hi.astype(F32)).astype(BF16)
    both = _dot(jnp.concatenate([hi, lo], axis=0), tri)
    return both[:SR] + both[SR:]


def _scatter_copies(ps, gots, send_sems, recv_sems):
    x, y, c = _place()
    peers = [(1 - x, y), (x, 1 - y), (1 - x, 1 - y)]
    return [pltpu.make_async_remote_copy(
        src_ref=ps[a].at[2 * px + py], dst_ref=gots[a].at[r], send_sem=send_sems.at[3 * a + r],
        recv_sem=recv_sems.at[3 * a + r], device_id=(px, py, c), device_id_type=MESH)
        for a in range(len(ps)) for r, (px, py) in enumerate(peers)]


GATHER_SEMS = 8


def _gather_tree_copies(ins, outs, send_sems, recv_sems):
    x, y, c = _place()
    me, xn, yn, dg = 2 * x + y, 2 * (1 - x) + y, 2 * x + (1 - y), 2 * (1 - x) + (1 - y)
    to_x, to_y, sibling = (1 - x, y, c), (x, 1 - y, c), (x, y, 1 - c)
    direct, relayed, passed = [], [], []
    for a in range(len(ins)):
        half = ins[a].shape[0] // 2
        mine = pl.ds(c * half, half)
        first, second = pl.ds(c * half, half // 2), pl.ds(c * half + half // 2, half // 2)

        def copy(i, src, dst, to, k=GATHER_SEMS * a):
            return pltpu.make_async_remote_copy(src_ref=src, dst_ref=dst, send_sem=send_sems.at[k + i],
                                                recv_sem=recv_sems.at[k + i], device_id=to, device_id_type=MESH)

        own, slot = ins[a].at[mine], outs[a].at[me, mine]
        direct += [copy(0, own, slot, to_x), copy(1, own, slot, to_y), copy(7, ins[a], outs[a].at[me], sibling)]
        relayed += [copy(2, outs[a].at[xn, first], outs[a].at[xn, first], to_y),
                    copy(3, outs[a].at[yn, second], outs[a].at[yn, second], to_x)]
        passed += [copy(4 + i, outs[a].at[j, mine], outs[a].at[j, mine], sibling) for i, j in enumerate((xn, yn, dg))]
    return direct, relayed, passed


def _pair_copies(ins, outs, send_sems, recv_sems):
    x, y, c = _place()
    copies = []
    for a in range(len(ins)):
        half = ins[a].shape[1] // 2
        copies.append(pltpu.make_async_remote_copy(
            src_ref=ins[a].at[:, pl.ds((1 - c) * half, half), :], dst_ref=outs[a], send_sem=send_sems.at[a],
            recv_sem=recv_sems.at[a], device_id=(x, y, 1 - c), device_id_type=MESH))
    return copies


def _host(body, grid, n_in, n_out, n_x, make_copies, mids=()):
    if not n_x:
        return body

    def hosting(*refs):
        a, b = n_in + n_x, n_in + 2 * n_x + n_out
        copies = make_copies(refs[n_in:a], refs[a + n_out:b], refs[-2], refs[-1])
        stages = copies if isinstance(copies, tuple) else (copies,)
        ids = [pl.program_id(d) for d in range(len(grid))]
        at = lambda step: functools.reduce(jnp.logical_and, [i == s for i, s in zip(ids, step)])

        @pl.when(at([0] * len(grid)))
        def _():
            for cp in stages[0]:
                cp.start()

        for before, after, step in zip(stages, stages[1:], mids):
            @pl.when(at(step))
            def _(before=before, after=after):
                for cp in before:
                    cp.wait_recv()
                for cp in after:
                    cp.start()

        body(*refs[:n_in], *refs[a:a + n_out], *refs[b:-2])

        @pl.when(at([g - 1 for g in grid]))
        def _():
            for cp in stages[-1]:
                cp.wait_recv()
            for stage in stages:
                for cp in stage:
                    cp.wait_send()

    return hosting


def _hosted_sems(n_x, per_array=3):
    n = per_array * n_x
    return [pltpu.SemaphoreType.DMA((n,)), pltpu.SemaphoreType.DMA((n,))] if n_x else []


RC = 256


def _chunk_causal(r):
    row = lax.broadcasted_iota(jnp.int32, (RC, AQ), 0) + (r * RC) % AQ
    return lax.broadcasted_iota(jnp.int32, (RC, AQ), 1) < row


def _attn_fwd(u, agw, shards=()):
    n = len(shards)
    grid = (T // AQ,)

    def body(q_ref, k_ref, v_ref, g_ref, ag_ref, o_ref, y_ref, tot_ref,
             kb_s, vb_s, qs_s, z_s, zs_s, lmb_s, suf_s, att_s, acc_s, run_s):
        qi = pl.program_id(0)

        @pl.when(qi == 0)
        def _():
            kb_s[...] = k_ref[...].astype(BF16)
            vb_s[...] = v_ref[...].astype(BF16)

        _, tr, tc, heads = _attn_tiles()
        upper = (tr > tc).astype(BF16)
        same_head = ((tr // DH) == (tc // DH)).astype(BF16)
        for g in range(NG):
            qs_s[g] = _stack_heads(q_ref[:, g * GW:(g + 1) * GW] * 0.125, heads).astype(BF16)
        acc_s[...] = jnp.zeros_like(acc_s)
        run_s[...] = jnp.zeros_like(run_s)

        def block(kb, masked):
            k0 = pl.multiple_of(kb * AQ, AQ)
            for g in range(NG):
                lanes = pl.ds(g * GW, GW)
                z_s[g] = _dot_nt(qs_s[g], kb_s[pl.ds(k0, AQ), lanes])
                for r in range(SR // RC):
                    rows = pl.ds(r * RC, RC)
                    z = z_s[g, rows, :]
                    zs = jnp.minimum(z, 0.0) - jnp.log(1.0 + jnp.exp(-jnp.abs(z)))
                    lm = zs - z
                    if masked:
                        lm = jnp.where(_chunk_causal(r), lm, 0.0)
                    run = run_s[g, rows, :]
                    zs_s[g, rows, :] = zs + run[:, 0:1]
                    hi = lm.astype(BF16)
                    lmb_s[g, rows, :] = hi
                    lmb_s[g, pl.ds(SR + r * RC, RC), :] = (lm - hi.astype(F32)).astype(BF16)
                    run_s[g, rows, :] = run + jnp.sum(lm, axis=1, keepdims=True)
                suf_s[g] = _dot(lmb_s[g], upper)
                for r in range(SR // RC):
                    rows = pl.ds(r * RC, RC)
                    att = jnp.exp(zs_s[g, rows, :] + suf_s[g, rows, :] + suf_s[g, pl.ds(SR + r * RC, RC), :])
                    if masked:
                        att = jnp.where(_chunk_causal(r), att, 0.0)
                    att_s[g, rows, :] = att.astype(BF16)
                acc_s[g] += _dot(att_s[g], vb_s[pl.ds(k0, AQ), lanes])

        block(qi, True)

        def step(i, c):
            block(qi - 1 - i, False)
            return c

        lax.fori_loop(0, qi, step, 0)
        gate = g_ref[...]
        agv = ag_ref[...]
        for g in range(NG):
            lanes = slice(g * GW, (g + 1) * GW)
            o = _unstack_heads(acc_s[g], heads)
            osq = o * o
            ms = _dot2(osq, same_head)
            gg = gate[:, lanes]
            o_ref[:, lanes] = o
            y_ref[:, lanes] = (o * lax.rsqrt(ms * (1.0 / DH) + EPS) * agv[:, lanes] * (gg * _sig(gg))).astype(BF16)
            tot_ref[g] = _unstack_heads(jnp.broadcast_to(run_s[g][:, 0:1], (SR, GW)), heads)

    tile = lambda dt, rows=SR: pltpu.VMEM((NG, rows, AQ), dt)
    scratch = [pltpu.VMEM((T, AD), BF16), pltpu.VMEM((T, AD), BF16), pltpu.VMEM((NG, SR, GW), BF16),
               tile(F32), tile(F32), tile(BF16, 2 * SR), tile(F32, 2 * SR), tile(BF16),
               pltpu.VMEM((NG, SR, GW), F32), pltpu.VMEM((NG, SR, LANE), F32)]
    col = lambda j: pl.BlockSpec((AQ, AD), lambda qi: (qi, j))
    res = pl.pallas_call(
        _host(body, grid, 5, 3, n, _gather_tree_copies, mids=((grid[0] * 5 // 8,), (grid[0] * 7 // 8,))), name="attn_fwd", grid=grid,
        in_specs=[col(0), pl.BlockSpec((T, AD), lambda qi: (0, 1)), pl.BlockSpec((T, AD), lambda qi: (0, 2)),
                  col(3), pl.BlockSpec((1, AD), lambda qi: (0, 0))] + [HBM_SPEC] * n,
        out_specs=[col(0), col(0), pl.BlockSpec((NG, AQ, GW), lambda qi: (0, qi, 0))] + [HBM_SPEC] * n,
        out_shape=[jax.ShapeDtypeStruct((T, AD), F32), jax.ShapeDtypeStruct((T, AD), BF16),
                   jax.ShapeDtypeStruct((NG, T, GW), F32)]
        + [jax.ShapeDtypeStruct((NCHIP,) + s.shape, s.dtype) for s in shards],
        scratch_shapes=scratch + _hosted_sems(n, GATHER_SEMS),
        compiler_params=_cp(("arbitrary",), VMEM_BIG),
    )(u, u, u, u, agw, *shards)
    return res[0], res[1], res[2], list(res[3:])


def _glu_conv(u, dw, db):
    tr = 256

    def body(cv_ref, cg_ref, w_ref, b_ref, c1_ref, pad_s):
        pad_s[pl.ds(0, CWP), :] = jnp.zeros((CWP, LANE), F32)
        pad_s[pl.ds(CWP, T), :] = cv_ref[...] * _sig(cg_ref[...])
        wv = w_ref[0]
        bias = b_ref[...]

        def tile(i, carry):
            r0 = pl.multiple_of(i * tr, tr)
            acc = jnp.zeros((tr, LANE), F32) + bias
            for w in range(CW):
                acc = acc + pad_s[pl.ds(r0 + (CWP - CW + 1) + w, tr), :] * wv[w:w + 1, :]
            c1_ref[pl.ds(r0, tr), :] = acc
            return carry

        lax.fori_loop(0, T // tr, tile, 0)

    return pl.pallas_call(
        body, name="glu_conv", grid=(CD // LANE,),
        in_specs=[pl.BlockSpec((T, LANE), lambda cb: (0, 16 + cb)),
                  pl.BlockSpec((T, LANE), lambda cb: (0, 20 + cb)),
                  pl.BlockSpec((1, CWP, LANE), lambda cb: (cb, 0, 0)),
                  pl.BlockSpec((1, LANE), lambda cb: (0, cb))],
        out_specs=pl.BlockSpec((T, LANE), lambda cb: (0, cb)),
        out_shape=jax.ShapeDtypeStruct((T, CD), F32),
        scratch_shapes=[pltpu.VMEM((T + CWP, LANE), F32)],
        compiler_params=_cp(("arbitrary",)),
    )(u, u, dw, db)


def _ln_silu(c1, lg, lb):
    mu = jnp.mean(c1, axis=-1, keepdims=True)
    xc = c1 - mu
    rs = lax.rsqrt(jnp.mean(xc * xc, axis=-1, keepdims=True) + EPS)
    xh = xc * rs
    ln = xh * lg + lb
    s = _sig(ln)
    return xh, rs, ln, s


def _layer_tail(c1, u, ya, h, p, lg, lb, wpw, cg, wout, pg, wgate, wple, head=None):
    tm = min(TR, T)

    def body(c1_ref, gc_ref, ya_ref, h_ref, p_ref, lg_ref, lb_ref, wpw_ref, cg_ref, wout_ref,
             pg_ref, wgate_ref, wple_ref, *rest):
        c3_ref, yc_ref, h1_ref, gate_ref, pe_ref, h2_ref = rest[-6 - 2 * bool(head):][:6]
        _, _, ln, s = _ln_silu(c1_ref[...], lg_ref[...], lb_ref[...])
        c2 = (ln * s).astype(BF16)
        c3 = _dot(c2, wpw_ref[...])
        gc = gc_ref[...]
        yc = (c3 * _rstd(c3) * cg_ref[...] * (gc * _sig(gc))).astype(BF16)
        c3_ref[...] = c3
        yc_ref[...] = yc
        y = _dot(ya_ref[...], wout_ref[pl.ds(0, AD), :]) + _dot(yc, wout_ref[pl.ds(AD, CD), :])
        h1 = h_ref[...] + y
        hn2 = (h1 * _rstd(h1) * pg_ref[...]).astype(BF16)
        gate = _sig(_dot(hn2, wgate_ref[...]))
        pb = p_ref[...].astype(BF16)
        pe = jnp.concatenate([_dot(pb, wple_ref[k]) for k in range(NCHIP)], axis=1)
        h1_ref[...] = h1
        gate_ref[...] = gate.astype(BF16)
        pe_ref[...] = pe.astype(BF16)
        h2 = h1 + pe * gate
        if not head:
            h2_ref[...] = h2
            return
        t_ref, fg_ref, loss_ref, dfg_ref = rest[0], rest[1], rest[-2], rest[-1]

        @pl.when(pl.program_id(0) == 0)
        def _():
            loss_ref[...] = jnp.zeros_like(loss_ref)
            dfg_ref[...] = jnp.zeros_like(dfg_ref)
        fg = fg_ref[...]
        r = _rstd(h2)
        e = h2 * r * fg - t_ref[...]
        loss_ref[...] += 0.5 * jnp.sum(jnp.mean(e * e, axis=-1, keepdims=True))
        dy = e * (1.0 / D)
        dfg_ref[...] += jnp.sum(dy * h2 * r, axis=0, keepdims=True)
        h2_ref[...] = _rms_bwd(dy, h2, r, fg)

    row = lambda w: pl.BlockSpec((tm, w), lambda i: (i, 0))
    full = lambda *s: pl.BlockSpec(s, lambda i: (0,) * len(s), pipeline_mode=pl.Buffered(1))
    extra = bool(head)
    return pl.pallas_call(
        body, name="layer_tail", grid=(T // tm,),
        in_specs=[row(CD), pl.BlockSpec((tm, CD), lambda i: (i, 6)), row(AD), row(D), row(PLE),
                  full(1, CD), full(1, CD), full(CD, CD), full(1, CD), full(D, D),
                  full(1, D), full(D, D), full(NCHIP, PLE, PLE)] + [row(D), full(1, D)] * extra,
        out_specs=[row(CD), row(CD), row(D), row(D), row(D), row(D)] + [full(8, LANE), full(1, D)] * extra,
        out_shape=[jax.ShapeDtypeStruct((T, CD), F32), jax.ShapeDtypeStruct((T, CD), BF16),
                   jax.ShapeDtypeStruct((T, D), F32), jax.ShapeDtypeStruct((T, D), BF16),
                   jax.ShapeDtypeStruct((T, D), BF16), jax.ShapeDtypeStruct((T, D), F32)]
        + [jax.ShapeDtypeStruct((8, LANE), F32), jax.ShapeDtypeStruct((1, D), F32)] * extra,
        compiler_params=_cp(("arbitrary",), VMEM_BIG),
    )(c1, u, ya, h, p, lg, lb, wpw, cg, wout, pg, wgate, wple, *(head or ()))


def _ple_out_bwd(dh2, h1, gate, pe, p, ya, yc, pg, wgate, wout):
    tm = min(TR, T)

    def body(dh2_ref, h1_ref, gate_ref, pe_ref, p_ref, ya_ref, yc_ref, pg_ref, wgate_ref, wout_ref,
             dh1_ref, dy_ref, dwg_ref, dwp_ref, dwo_ref, dpg_ref):
        @pl.when(pl.program_id(0) == 0)
        def _():
            dwg_ref[...] = jnp.zeros_like(dwg_ref)
            dwp_ref[...] = jnp.zeros_like(dwp_ref)
            dwo_ref[...] = jnp.zeros_like(dwo_ref)
            dpg_ref[...] = jnp.zeros_like(dpg_ref)
        dh2 = dh2_ref[...]
        h1 = h1_ref[...]
        gate = gate_ref[...].astype(F32)
        pg = pg_ref[...]
        dpe = (dh2 * gate).astype(BF16)
        dgp = (dh2 * pe_ref[...].astype(F32) * gate * (1.0 - gate)).astype(BF16)
        r = _rstd(h1)
        hn = h1 * r
        dwg_ref[...] += _dot_tn((hn * pg).astype(BF16), dgp)
        dhn2 = _dot_nt(dgp, wgate_ref[...])
        dpg_ref[...] += jnp.sum(dhn2 * hn, axis=0, keepdims=True)
        dh1 = dh2 + _rms_bwd(dhn2, h1, r, pg)
        pb = p_ref[...].astype(BF16)
        for k in range(NCHIP):
            dwp_ref[k] += _dot_tn(pb, dpe[:, k * PLE:(k + 1) * PLE])
        dh1b = dh1.astype(BF16)
        dy_ref[...] = _dot_nt(dh1b, wout_ref[...])
        dwo_ref[pl.ds(0, AD), :] += _dot_tn(ya_ref[...], dh1b)
        dwo_ref[pl.ds(AD, CD), :] += _dot_tn(yc_ref[...], dh1b)
        dh1_ref[...] = dh1

    row = lambda w: pl.BlockSpec((tm, w), lambda i: (i, 0))
    full = lambda *s: pl.BlockSpec(s, lambda i: (0,) * len(s), pipeline_mode=pl.Buffered(1))
    return pl.pallas_call(
        body, name="ple_out_bwd", grid=(T // tm,),
        in_specs=[row(D), row(D), row(D), row(D), row(PLE), row(AD), row(CD),
                  full(1, D), full(D, D), full(D, D)],
        out_specs=[row(D), row(D), full(D, D), full(NCHIP, PLE, PLE), full(D, D), full(1, D)],
        out_shape=[jax.ShapeDtypeStruct((T, D), F32), jax.ShapeDtypeStruct((T, D), F32),
                   jax.ShapeDtypeStruct((D, D), F32), jax.ShapeDtypeStruct((NCHIP, PLE, PLE), F32),
                   jax.ShapeDtypeStruct((D, D), F32), jax.ShapeDtypeStruct((1, D), F32)],
        compiler_params=_cp(("arbitrary",), VMEM_BIG),
    )(dh2, h1, gate, pe, p, ya, yc, pg, wgate, wout)


def _branch_bwd(dy, o, u, c1, c3, ag, lg, lb, wpw, cg, seg):
    tm = min(TR, T)

    def body(dya_ref, dyc_ref, o_ref, ga_ref, gc_ref, c1_ref, c3_ref, ag_ref, lg_ref, lb_ref, wpw_ref,
             cg_ref, seg_ref, do_ref, dga_ref, dgc_ref, dc1_ref, dwpw_ref, dag_ref, dcg_ref, dlg_ref, dlb_ref):
        @pl.when(pl.program_id(0) == 0)
        def _():
            for r_ in (dwpw_ref, dag_ref, dcg_ref, dlg_ref, dlb_ref):
                r_[...] = jnp.zeros_like(r_)
        dya = dya_ref[...]
        o = o_ref[...]
        ga = ga_ref[...]
        ag_v = ag_ref[...]
        seg_m = seg_ref[...]
        r = lax.rsqrt(_dot2(o * o, seg_m) * (1.0 / DH) + EPS)
        onr = o * r
        sg = _sig(ga)
        dga_ref[...] = (dya * (onr * ag_v) * (sg * (1.0 + ga * (1.0 - sg)))).astype(BF16)
        don = dya * (ga * sg)
        dag_ref[...] += jnp.sum(don * onr, axis=0, keepdims=True)
        dn = don * ag_v
        do_ref[...] = r * dn - o * (r * r * r) * (_dot2(dn * o, seg_m) * (1.0 / DH))
        dyc = dyc_ref[...]
        c3 = c3_ref[...]
        gc = gc_ref[...]
        cg_v = cg_ref[...]
        r3 = _rstd(c3)
        cn = c3 * r3
        sc = _sig(gc)
        dgc_ref[...] = (dyc * (cn * cg_v) * (sc * (1.0 + gc * (1.0 - sc)))).astype(BF16)
        dcn = dyc * (gc * sc)
        dcg_ref[...] += jnp.sum(dcn * cn, axis=0, keepdims=True)
        dc3 = _rms_bwd(dcn, c3, r3, cg_v).astype(BF16)
        lg_v = lg_ref[...]
        xh, rs, ln, s = _ln_silu(c1_ref[...], lg_v, lb_ref[...])
        c2 = (ln * s).astype(BF16)
        dwpw_ref[...] += _dot_tn(c2, dc3)
        dc2 = _dot_nt(dc3, wpw_ref[...])
        dln = dc2 * (s * (1.0 + ln * (1.0 - s)))
        dlb_ref[...] += jnp.sum(dln, axis=0, keepdims=True)
        dlg_ref[...] += jnp.sum(dln * xh, axis=0, keepdims=True)
        dxh = dln * lg_v
        dc1_ref[...] = rs * (dxh - jnp.mean(dxh, axis=-1, keepdims=True)
                             - xh * jnp.mean(dxh * xh, axis=-1, keepdims=True))

    half = lambda j: pl.BlockSpec((tm, 512), lambda i: (i, j))
    full = lambda *s: pl.BlockSpec(s, lambda i: (0,) * len(s), pipeline_mode=pl.Buffered(1))
    vec = jax.ShapeDtypeStruct((1, 512), F32)
    act = jax.ShapeDtypeStruct((T, 512), F32)
    return pl.pallas_call(
        body, name="branch_bwd", grid=(T // tm,),
        in_specs=[half(0), half(1), half(0), half(3), half(6), half(0), half(0),
                  full(1, AD), full(1, CD), full(1, CD), full(CD, CD), full(1, CD), full(AD, AD)],
        out_specs=[half(0), half(0), half(0), half(0), full(CD, CD), full(1, 512), full(1, 512),
                   full(1, 512), full(1, 512)],
        out_shape=[act, jax.ShapeDtypeStruct((T, 512), BF16), jax.ShapeDtypeStruct((T, 512), BF16), act,
                   jax.ShapeDtypeStruct((CD, CD), F32), vec, vec, vec, vec],
        compiler_params=_cp(("arbitrary",), VMEM_BIG),
    )(dy, dy, o, u, u, c1, c3, ag, lg, lb, wpw, cg, seg)


def _conv_bwd(dc1, u, dw, grads=()):
    tr = 64
    n_x = len(grads)
    grid = (CD // LANE,)
    off = CWP - CW + 1

    def body(d_ref, cv_ref, cg_ref, w_ref, dcv_ref, dcg_ref, ddw_ref, ddb_ref, padc_s, padd_s, acc_s):
        cv = cv_ref[...]
        sg = _sig(cg_ref[...])
        padc_s[pl.ds(0, CWP), :] = jnp.zeros((CWP, LANE), F32)
        padc_s[pl.ds(CWP, T), :] = cv * sg
        padd_s[pl.ds(0, T), :] = d_ref[...]
        padd_s[pl.ds(T, CWP), :] = jnp.zeros((CWP, LANE), F32)
        acc_s[...] = jnp.zeros_like(acc_s)
        wv = w_ref[0]

        def tile(i, carry):
            r0 = pl.multiple_of(i * tr, tr)
            dt = padd_s[pl.ds(r0, tr), :]
            dc0 = jnp.zeros((tr, LANE), F32)
            for w in range(CW):
                dc0 = dc0 + padd_s[pl.ds(r0 + (CW - 1) - w, tr), :] * wv[w:w + 1, :]
                prod = dt * padc_s[pl.ds(r0 + off + w, tr), :]
                acc_s[w] += jnp.sum(prod.reshape(tr // 8, 8, LANE), axis=0)
            cvt = cv_ref[pl.ds(r0, tr), :]
            sgt = _sig(cg_ref[pl.ds(r0, tr), :])
            dcv_ref[pl.ds(r0, tr), :] = (dc0 * sgt).astype(BF16)
            dcg_ref[pl.ds(r0, tr), :] = (dc0 * cvt * sgt * (1.0 - sgt)).astype(BF16)
            return carry

        lax.fori_loop(0, T // tr, tile, 0)
        ddw_ref[0] = jnp.sum(acc_s[...], axis=1)
        ddb_ref[...] = jnp.sum(d_ref[...], axis=0, keepdims=True)

    col = lambda j: pl.BlockSpec((T, LANE), lambda cb: (0, j + cb))
    res = pl.pallas_call(
        _host(body, grid, 4, 4, n_x, _pair_copies), name="conv_bwd", grid=grid,
        in_specs=[col(0), col(16), col(20), pl.BlockSpec((1, CWP, LANE), lambda cb: (cb, 0, 0))] + [HBM_SPEC] * n_x,
        out_specs=[col(0), col(0), pl.BlockSpec((1, CWP, LANE), lambda cb: (cb, 0, 0)),
                   pl.BlockSpec((1, LANE), lambda cb: (0, cb))] + [HBM_SPEC] * n_x,
        out_shape=[jax.ShapeDtypeStruct((T, CD), BF16), jax.ShapeDtypeStruct((T, CD), BF16),
                   jax.ShapeDtypeStruct((NCHIP, CWP, LANE), F32), jax.ShapeDtypeStruct((1, CD), F32)]
        + [jax.ShapeDtypeStruct((NCHIP, g.shape[1] // 2, g.shape[2]), F32) for g in grads],
        scratch_shapes=[pltpu.VMEM((T + CWP, LANE), F32), pltpu.VMEM((T + CWP, LANE), F32),
                        pltpu.VMEM((CWP, 8, LANE), F32)] + _hosted_sems(n_x),
        compiler_params=_cp(("arbitrary",)),
    )(dc1, u, u, dw, *grads)
    return res[0], res[1], res[2], res[3], list(res[4:])


def _attn_bwd(u, do, tot, partials=()):
    n_x = len(partials)
    grid = (T // AQ,)

    def body(q_ref, k_ref, v_ref, do_ref, tot_ref, dq_ref, dk_ref, dv_ref, kb_s, vb_s, dk_s, dv_s):
        qi = pl.program_id(0)

        @pl.when(qi == 0)
        def _():
            kb_s[...] = k_ref[...].astype(BF16)
            vb_s[...] = v_ref[...].astype(BF16)
            dk_s[...] = jnp.zeros_like(dk_s)
            dv_s[...] = jnp.zeros_like(dv_s)

        causal, tr, tc, heads = _attn_tiles()
        upper = (tr > tc).astype(BF16)
        lower = (tr < tc).astype(BF16)
        qs, qus, dos, tots = [], [], [], []
        for g in range(NG):
            lanes = slice(g * GW, (g + 1) * GW)
            q = q_ref[:, lanes]
            qs.append(_stack_heads(q * 0.125, heads).astype(BF16))
            qus.append(_stack_heads(q, heads).astype(BF16))
            dos.append(_stack_heads(do_ref[:, lanes], heads).astype(BF16))
            totv = tot_ref[g]
            tots.append(jnp.concatenate([totv[:, h * DH:h * DH + 1] for h in range(HG)], axis=0))

        def block(kb, carry, masked):
            k0 = pl.multiple_of(kb * AQ, AQ)
            out = []
            for g in range(NG):
                lanes = pl.ds(g * GW, GW)
                lm_left, dl_left, dq = carry[g]
                kk = kb_s[pl.ds(k0, AQ), lanes]
                vv = vb_s[pl.ds(k0, AQ), lanes]
                z = _dot_nt(qs[g], kk)
                sp = _softplus(z)
                lm = jnp.where(causal, -sp, 0.0) if masked else -sp
                lm_incl = lm_left + jnp.sum(lm, axis=1, keepdims=True)
                att = jnp.exp((z - sp) + _tri_sum(lm, upper) + (tots[g] - lm_incl))
                if masked:
                    att = jnp.where(causal, att, 0.0)
                dl = att * _dot_nt(dos[g], vv)
                dv_s[pl.ds(k0, AQ), lanes] += _dot_tn(att.astype(BF16), dos[g])
                prefix = dl_left + _tri_sum(dl, lower)
                beta = jnp.exp(z - sp)
                dz = (1.0 - beta) * dl - beta * prefix
                if masked:
                    dz = jnp.where(causal, dz, 0.0)
                dzs = (dz * 0.125).astype(BF16)
                dk_s[pl.ds(k0, AQ), lanes] += _dot_tn(dzs, qus[g])
                out.append((lm_incl, dl_left + jnp.sum(dl, axis=1, keepdims=True), dq + _dot(dzs, kk)))
            return tuple(out)

        zero = jnp.zeros((SR, 1), F32)
        init = tuple((zero, zero, jnp.zeros((SR, GW), F32)) for _ in range(NG))
        carry = lax.fori_loop(0, qi, lambda kb, c: block(kb, c, False), init)
        carry = block(qi, carry, True)
        for g in range(NG):
            dq_ref[:, g * GW:(g + 1) * GW] = _unstack_heads(carry[g][2], heads).astype(BF16)

        @pl.when(qi == grid[0] - 1)
        def _():
            dk_ref[...] = dk_s[...].astype(BF16)
            dv_ref[...] = dv_s[...].astype(BF16)

    col = lambda j: pl.BlockSpec((AQ, AD), lambda qi: (qi, j))
    whole = lambda j: pl.BlockSpec((T, AD), lambda qi: (0, j), pipeline_mode=pl.Buffered(1))
    res = pl.pallas_call(
        _host(body, grid, 5, 3, n_x, _scatter_copies), name="attn_bwd", grid=grid,
        in_specs=[col(0), whole(1), whole(2), col(0), pl.BlockSpec((NG, AQ, GW), lambda qi: (0, qi, 0))]
        + [HBM_SPEC] * n_x,
        out_specs=[col(0), whole(0), whole(0)] + [HBM_SPEC] * n_x,
        out_shape=[jax.ShapeDtypeStruct((T, AD), BF16)] * 3
        + [jax.ShapeDtypeStruct((NCHIP - 1,) + a.shape[1:], a.dtype) for a in partials],
        scratch_shapes=[pltpu.VMEM((T, AD), BF16), pltpu.VMEM((T, AD), BF16), pltpu.VMEM((T, AD), F32),
                        pltpu.VMEM((T, AD), F32)] + _hosted_sems(n_x),
        compiler_params=_cp(("arbitrary",), VMEM_BIG),
    )(u, u, u, do, tot, *partials)
    return res[0], res[1], res[2], list(res[3:])


def _inproj_dw(hn, du):
    tm = T

    def body(hn_ref, du_ref, dw_ref):
        @pl.when(pl.program_id(1) == 0)
        def _():
            dw_ref[...] = jnp.zeros_like(dw_ref)
        dw_ref[0] += _dot_tn(hn_ref[...], du_ref[...])

    return pl.pallas_call(
        body, name="inproj_dw", grid=(NCHIP, T // tm),
        in_specs=[pl.BlockSpec((tm, D), lambda k, i: (i, 0)), pl.BlockSpec((tm, SHW), lambda k, i: (i, k))],
        out_specs=pl.BlockSpec((1, D, SHW), lambda k, i: (k, 0, 0)),
        out_shape=jax.ShapeDtypeStruct((NCHIP, D, SHW), F32),
        compiler_params=_cp(("arbitrary", "arbitrary"), VMEM_BIG),
    )(hn, du)


def _inproj_dx(du, w, h, g, dres, partials=(), grads=(), after=None):
    tm = min(TM, T)
    sent = list(partials) + list(grads)
    n_x = len(sent)
    assert after is None or not sent
    grid = (T // tm, NCHIP)
    if grads:
        landing = [jax.ShapeDtypeStruct((NCHIP, a.shape[1] // 2, a.shape[2]), F32) for a in grads]
    else:
        landing = [jax.ShapeDtypeStruct((NCHIP - 1,) + a.shape[1:], a.dtype) for a in partials]

    def body(du_ref, w_ref, h_ref, g_ref, dres_ref, dh_ref, dg_ref, acc_s):
        i, k = pl.program_id(0), pl.program_id(1)

        @pl.when(jnp.logical_and(i == 0, k == 0))
        def _():
            dg_ref[...] = jnp.zeros_like(dg_ref)

        @pl.when(k == 0)
        def _():
            acc_s[...] = _dot_nt(du_ref[...], w_ref[0])

        @pl.when(k > 0)
        def _():
            acc_s[...] += _dot_nt(du_ref[...], w_ref[0])

        @pl.when(k == NCHIP - 1)
        def _():
            hh = h_ref[...]
            r = _rstd(hh)
            dhn = acc_s[...]
            dg_ref[...] += jnp.sum(dhn * hh * r, axis=0, keepdims=True)
            dh_ref[...] = dres_ref[...] + _rms_bwd(dhn, hh, r, g_ref[...])

    behind = [] if after is None else [after]
    hosting = _host(body, grid, 5, 2, n_x, _pair_copies if grads else _scatter_copies)
    res = pl.pallas_call(
        (lambda *refs: body(*refs[:5], *refs[6:])) if behind else hosting, name="inproj_dx", grid=grid,
        in_specs=[pl.BlockSpec((tm, SHW), lambda i, k: (i, k)),
                  pl.BlockSpec((1, D, SHW), lambda i, k: (k, 0, 0)),
                  pl.BlockSpec((tm, D), lambda i, k: (i, 0)),
                  pl.BlockSpec((1, D), lambda i, k: (0, 0)),
                  pl.BlockSpec((tm, D), lambda i, k: (i, 0))] + [HBM_SPEC] * n_x
        + [pl.BlockSpec(memory_space=pl.ANY)] * len(behind),
        out_specs=[pl.BlockSpec((tm, D), lambda i, k: (i, 0)), pl.BlockSpec((1, D), lambda i, k: (0, 0))]
        + [HBM_SPEC] * n_x,
        out_shape=[jax.ShapeDtypeStruct((T, D), F32), jax.ShapeDtypeStruct((1, D), F32)] + landing,
        scratch_shapes=[pltpu.VMEM((tm, D), F32)] + _hosted_sems(n_x),
        compiler_params=_cp(("arbitrary", "arbitrary"), VMEM_BIG),
    )(du, w, h, g, dres, *sent, *behind)
    return res[0], res[1], list(res[2:])


def _sum_pair(core, grads, gots):
    n = len(grads)

    def body(c_ref, *refs):
        for a in range(n):
            refs[2 * n + a][...] = (refs[a][...] + refs[n + a][...]).astype(BF16)

    mine = [pl.BlockSpec((1,) + s.shape[1:], lambda k, c: (k, c[0], 0)) for s in gots]
    same = [pl.BlockSpec((1,) + s.shape[1:], lambda k, c: (k, 0, 0)) for s in gots]
    return pl.pallas_call(
        body, name="sum_pair",
        grid_spec=pltpu.PrefetchScalarGridSpec(
            num_scalar_prefetch=1, grid=(NCHIP,), in_specs=mine + same, out_specs=same),
        out_shape=[jax.ShapeDtypeStruct(s.shape, BF16) for s in gots],
        compiler_params=_cp(("arbitrary",), VMEM_BIG),
    )(core, *grads, *gots)


def _sum_chips_share(chip, partials, gots):
    flat_p = [p for layer in partials for p in layer]
    flat_g = [g for layer in gots for g in layer]
    n, per_layer = len(flat_p), len(partials[0])

    def body(c_ref, *refs):
        full, sums = refs[2 * n:2 * n + per_layer], refs[2 * n + per_layer:3 * n + per_layer]
        send_sems, recv_sems, local_sems = refs[3 * n + per_layer:]
        x, y, c = _place()
        copies = []
        for i in range(n):
            acc = refs[i][0].astype(F32)
            for j in range(NCHIP - 1):
                acc = acc + refs[n + i][j].astype(F32)
            sums[i][...] = acc
            half = flat_p[i].shape[1]
            rows = full[i % per_layer].at[i // per_layer, pl.ds(c * half, half)]
            copies.append(pltpu.make_async_copy(sums[i], rows, local_sems.at[i]))
            copies.append(pltpu.make_async_remote_copy(
                src_ref=sums[i], dst_ref=rows, send_sem=send_sems.at[i], recv_sem=recv_sems.at[i],
                device_id=(x, y, 1 - c), device_id_type=MESH))
        for cp in copies:
            cp.start()
        for cp in copies:
            cp.wait()

    return pl.pallas_call(
        body, name="sum_chips_share",
        grid_spec=pltpu.PrefetchScalarGridSpec(
            num_scalar_prefetch=1, grid=(1,),
            in_specs=[pl.BlockSpec((1,) + s.shape[1:], lambda i, c: (c[0], 0, 0)) for s in flat_p]
            + [pl.BlockSpec(s.shape, lambda i, c: (0, 0, 0)) for s in flat_g],
            out_specs=[HBM_SPEC] * per_layer,
            scratch_shapes=[pltpu.VMEM(s.shape[1:], F32) for s in flat_p]
            + [pltpu.SemaphoreType.DMA((n,)), pltpu.SemaphoreType.DMA((n,)), pltpu.SemaphoreType.DMA((n,))]),
        out_shape=[jax.ShapeDtypeStruct((len(partials), 2 * s.shape[1], s.shape[2]), F32) for s in partials[0]],
        compiler_params=_cp(("arbitrary",), VMEM_BIG),
    )(chip, *flat_p, *flat_g)


def _adam_math(w, g, m, v):
    nm = ADAM_B1 * m + (1.0 - ADAM_B1) * g
    nv = ADAM_B2 * v + (1.0 - ADAM_B2) * (g * g)
    m_hat = nm / (1.0 - ADAM_B1 ** ADAM_STEP)
    v_hat = nv / (1.0 - ADAM_B2 ** ADAM_STEP)
    return -ADAM_LR * (m_hat / (jnp.sqrt(v_hat) + ADAM_EPS) + ADAM_WD * w), nm, nv


def _adamw(w, g, m, v, rows):
    R, C = w.shape

    def body(w_ref, g_ref, m_ref, v_ref, g_out, d_ref, nm_ref, nv_ref):
        g = g_ref[...]
        g_out[...] = g
        d_ref[...], nm_ref[...], nv_ref[...] = _adam_math(w_ref[...], g, m_ref[...], v_ref[...])

    spec = pl.BlockSpec((rows, C), lambda i: (i, 0))
    sh = jax.ShapeDtypeStruct((R, C), F32)
    return pl.pallas_call(
        body, name="adamw", grid=(R // rows,), in_specs=[spec] * 4, out_specs=[spec] * 4,
        out_shape=[sh, sh, sh, sh], compiler_params=_cp(("arbitrary",)),
    )(w, g, m, v)


def _adamw_taps(w, g, m, v):
    def body(w_ref, g_ref, m_ref, v_ref, g_out, d_ref, nm_ref, nv_ref):
        g = g_ref[:, pl.ds(0, CW), :]
        g_out[...] = g
        d_ref[...], nm_ref[...], nv_ref[...] = _adam_math(w_ref[...], g, m_ref[...], v_ref[...])

    whole = lambda a: pl.BlockSpec(a.shape, lambda i: (0, 0, 0))
    return pl.pallas_call(
        body, name="adamw_taps", grid=(1,), in_specs=[whole(w), whole(g), whole(w), whole(w)],
        out_specs=[whole(w)] * 4, out_shape=[jax.ShapeDtypeStruct(w.shape, F32)] * 4,
    )(w, g, m, v)


def _small_adamw(tot, ws, ms, vs):
    n = len(ws)

    def body(*refs):
        tot_ref = refs[0]
        w_refs, m_refs, v_refs = refs[1:1 + n], refs[1 + n:1 + 2 * n], refs[1 + 2 * n:1 + 3 * n]
        outs = refs[1 + 3 * n:]
        for i in range(n):
            rows, width = ws[i].shape
            g = tot_ref[pl.ds(SMALL_ROW[i], rows), pl.ds(0, width)]
            outs[4 * i][...] = g
            outs[4 * i + 1][...], outs[4 * i + 2][...], outs[4 * i + 3][...] = _adam_math(
                w_refs[i][...], g, m_refs[i][...], v_refs[i][...])
        outs[4 * n][...] = tot_ref[pl.ds(LOSS_ROW, 1), pl.ds(0, LANE)]

    vmem = pl.BlockSpec(memory_space=pltpu.VMEM)
    res = pl.pallas_call(
        body, name="small_adamw", in_specs=[vmem] * (1 + 3 * n), out_specs=[vmem] * (4 * n + 1),
        out_shape=[jax.ShapeDtypeStruct(w.shape, F32) for w in ws for _ in range(4)]
        + [jax.ShapeDtypeStruct((1, LANE), F32)],
    )(tot, *ws, *ms, *vs)
    return [res[4 * i:4 * i + 4] for i in range(n)], res[4 * n]


HBM_SPEC = pl.BlockSpec(memory_space=pltpu.HBM)


def _place():
    return lax.axis_index("x"), lax.axis_index("y"), lax.axis_index("c")


def _all_gather_split(shard):
    def body(in_ref, out_ref, send_sems, recv_sems):
        direct, relayed, passed = _gather_tree_copies([in_ref], [out_ref], send_sems, recv_sems)
        for cp in direct:
            cp.start()
        for i in range(2):
            direct[i].wait_recv()
            relayed[i].start()
            passed[i].start()
        for cp in relayed:
            cp.wait_recv()
        passed[2].start()
        for cp in passed + direct[2:]:
            cp.wait_recv()
        for cp in direct + relayed + passed:
            cp.wait_send()

    return pl.pallas_call(
        body, name="all_gather_split", in_specs=[HBM_SPEC], out_specs=HBM_SPEC,
        out_shape=jax.ShapeDtypeStruct((NCHIP,) + shard.shape, shard.dtype),
        scratch_shapes=_hosted_sems(1, GATHER_SEMS),
    )(shard)


SEM_SPEC = pl.BlockSpec(memory_space=pltpu.SEMAPHORE)
ORDERED_EFFECT = pltpu.CompilerParams(has_side_effects=pltpu.SideEffectType.DATAFLOW_SIDE_EFFECTING)


def _scatter_start(partial):
    land = pltpu.with_memory_space_constraint(
        lax.empty((NCHIP - 1,) + partial.shape[1:], partial.dtype), pltpu.HBM)

    def body(p_ref, land_ref, send_sems, recv_sems, p_thru, land_thru, token):
        for cp in _scatter_copies([p_ref], [land_ref], send_sems, recv_sems):
            cp.start()
        token[...] = jnp.zeros_like(token)

    return pl.pallas_call(
        body, name="scatter_start",
        out_shape=(pltpu.SemaphoreType.DMA((NCHIP - 1,)), pltpu.SemaphoreType.DMA((NCHIP - 1,)),
                   pltpu.HBM(partial.shape, partial.dtype), pltpu.HBM(land.shape, land.dtype),
                   jax.ShapeDtypeStruct((8, LANE), F32)),
        in_specs=(HBM_SPEC, HBM_SPEC),
        out_specs=(SEM_SPEC, SEM_SPEC, HBM_SPEC, HBM_SPEC, pl.BlockSpec(memory_space=pltpu.VMEM)),
        input_output_aliases={0: 2, 1: 3}, compiler_params=ORDERED_EFFECT,
    )(pltpu.with_memory_space_constraint(partial, pltpu.HBM), land)


def _scatter_wait(send_sems, recv_sems, p_thru, land_thru, after):
    def body(p_ref, land_ref, send_sems, recv_sems, after_ref, p_dead, got_ref):
        for cp in _scatter_copies([p_ref], [land_ref], send_sems, recv_sems):
            cp.wait_send()
            cp.wait_recv()

    return pl.pallas_call(
        body, name="scatter_wait",
        out_shape=(pltpu.HBM(p_thru.shape, p_thru.dtype), pltpu.HBM(land_thru.shape, land_thru.dtype)),
        in_specs=(HBM_SPEC, HBM_SPEC, SEM_SPEC, SEM_SPEC, pl.BlockSpec(memory_space=pl.ANY)),
        out_specs=(HBM_SPEC, HBM_SPEC), input_output_aliases={0: 0, 1: 1}, compiler_params=ORDERED_EFFECT,
    )(p_thru, land_thru, send_sems, recv_sems, after)


def _pair_exchange(grads):
    n = len(grads)

    def body(*refs):
        copies = _pair_copies(refs[:n], refs[n:2 * n], refs[2 * n], refs[2 * n + 1])
        for cp in copies:
            cp.start()
        for cp in copies:
            cp.wait()

    return pl.pallas_call(
        body, name="pair_exchange", in_specs=[HBM_SPEC] * n, out_specs=[HBM_SPEC] * n,
        out_shape=[jax.ShapeDtypeStruct((NCHIP, g.shape[1] // 2, g.shape[2]), F32) for g in grads],
        scratch_shapes=[pltpu.SemaphoreType.DMA((n,)), pltpu.SemaphoreType.DMA((n,))],
    )(*grads)


def _small_allreduce(rows, loss_blk, per_head=()):
    n = len(rows)

    def body(*refs):
        loss_ref, o_ref, pk, slots, send_sems, recv_sems = refs[n:]
        pk[...] = jnp.zeros_like(pk)
        for i in range(n):
            row = refs[i][...]
            if i in per_head:
                row = functools.reduce(jnp.add, [row[:, h * DH:(h + 1) * DH] for h in range(AD // DH)])
            pk[pl.ds(i, 1), pl.ds(0, row.shape[1])] = row
        pk[pl.ds(LOSS_ROW, 1), pl.ds(0, LANE)] = loss_ref[pl.ds(0, 1), :]
        x, y, c = _place()
        me = 4 * x + 2 * y + c
        slots[me] = pk[...]
        copies = []
        for r in range(1, 8):
            rx, ry, rc = (r >> 2) & 1, (r >> 1) & 1, r & 1
            peer = (x + rx - 2 * x * rx, y + ry - 2 * y * ry, c + rc - 2 * c * rc)
            cp = pltpu.make_async_remote_copy(
                src_ref=pk, dst_ref=slots.at[me], send_sem=send_sems.at[r - 1], recv_sem=recv_sems.at[r - 1],
                device_id=peer, device_id_type=MESH)
            cp.start()
            copies.append(cp)
        for cp in copies:
            cp.wait()
        acc = slots[0]
        for j in range(1, 8):
            acc = acc + slots[j]
        o_ref[...] = acc

    vmem = pl.BlockSpec(memory_space=pltpu.VMEM)
    return pl.pallas_call(
        body, name="small_allreduce", in_specs=[vmem] * (n + 1), out_specs=vmem,
        out_shape=jax.ShapeDtypeStruct((SMALL_PK, D), F32),
        scratch_shapes=[pltpu.VMEM((SMALL_PK, D), F32), pltpu.VMEM((8, SMALL_PK, D), F32),
                        pltpu.SemaphoreType.DMA((7,)), pltpu.SemaphoreType.DMA((7,))],
    )(*rows, loss_blk)


def _seg_matrix():
    i = lax.broadcasted_iota(jnp.int32, (AD, AD), 0) // DH
    j = lax.broadcasted_iota(jnp.int32, (AD, AD), 1) // DH
    return (i == j).astype(BF16)


TAIL = ("w_out", "w_ple_gate", "w_ple", "w_pw", "dw_w")


def _local_step(x, p, tgt, sm, shards, chip, ci):
    seg = _seg_matrix()
    core = jnp.reshape(ci, (1,)).astype(jnp.int32)
    chip_idx = jnp.reshape(chip, (1,)).astype(jnp.int32)
    w_in_next = _all_gather_split(shards[0]["w_in"])
    h = x
    saved = []
    for l in range(DEPTH):
        w_in = w_in_next
        row = lambda name: sm[name][l:l + 1]
        u, hn = _rms_inproj(h, row("norm_g"), w_in)
        todo = [shards[l][k] for k in TAIL] + ([shards[l + 1]["w_in"]] if l + 1 < DEPTH else [])
        o, ya, tot, got = _attn_fwd(u, jnp.tile(row("attn_out_g"), (1, AD // DH)), todo)
        w_out = got[0].reshape(D, D)
        w_gate = got[1].reshape(D, D)
        w_ple = got[2]
        w_pw = got[3].reshape(CD, CD)
        dw = got[4]
        if l + 1 < DEPTH:
            w_in_next = got[5]
        c1 = _glu_conv(u, dw, row("dw_b"))
        c3, yc, h1, gate, pe, h2, *at_end = _layer_tail(
            c1, u, ya, h, p[l], row("conv_ln_g"), row("conv_ln_b"), w_pw, row("conv_out_g"), w_out,
            row("ple_norm_g"), w_gate, w_ple, head=(tgt, sm["final_g"]) if l == DEPTH - 1 else None)
        saved.append(dict(h=h, u=u, hn=hn, o=o, ya=ya, tot=tot, c1=c1, c3=c3, yc=yc, h1=h1, gate=gate, pe=pe,
                          w_in=w_in, w_out=w_out, w_gate=w_gate, w_pw=w_pw, dw=dw))
        h = h2
    dh, (loss_blk, dfg) = h, at_end
    small = [None] * DEPTH
    pending, partials, arrived = [], {}, {}
    pair_sum = lambda grads: _sum_pair(core, grads, _pair_exchange(grads))
    for l in reversed(range(DEPTH)):
        s = saved[l]
        row = lambda name: sm[name][l:l + 1]
        dh1, dy, dwg, dwp, dwo, dpg = _ple_out_bwd(
            dh, s["h1"], s["gate"], s["pe"], p[l], s["ya"], s["yc"], row("ple_norm_g"), s["w_gate"], s["w_out"])
        ag_t = jnp.tile(row("attn_out_g"), (1, AD // DH))
        do, dga, dgc, dc1, dwpw, dag, dcg, dlg, dlb = _branch_bwd(
            dy, s["o"], s["u"], s["c1"], s["c3"], ag_t, row("conv_ln_g"), row("conv_ln_b"), s["w_pw"],
            row("conv_out_g"), seg)
        tail = [dwo.reshape(NCHIP, 256, D), dwg.reshape(NCHIP, 256, D), dwp, dwpw.reshape(NCHIP, 128, CD)]
        dcv, dcgate, ddw, ddb, halves = _conv_bwd(dc1, s["u"], s["dw"], tail if l == 0 else ())
        tail.append(ddw)
        if l == 0:
            partials[(l, "tail")] = _sum_pair(core, tail, halves + list(_pair_exchange([ddw])))
            pending.append((l, "tail"))
        send = [t for key in pending for t in partials[key]]
        dq, dk, dv, got = _attn_bwd(s["u"], do, s["tot"], send)
        for key in pending:
            arrived[key], got = got[:len(partials[key])], got[len(partials[key]):]
        pending = []
        du = jnp.concatenate([dq, dk, dv, dga, dcv, dcgate, dgc], axis=1)
        dwin = _inproj_dw(s["hn"], du)
        if l == 0:
            in_flight = _scatter_start(pair_sum([dwin])[0])
            dh, dng, _ = _inproj_dx(du, s["w_in"], s["h"], row("norm_g"), dh1, after=in_flight[4])
            small[l] = dict(norm_g=dng, attn_out_g=dag, dw_b=ddb, conv_ln_g=dlg, conv_ln_b=dlb,
                            conv_out_g=dcg, ple_norm_g=dpg)
            per_head = tuple(DEPTH * SMALL2.index("attn_out_g") + j for j in range(DEPTH))
            small_tot = _small_allreduce([small[j][k] for k in SMALL2 for j in range(DEPTH)] + [dfg], loss_blk,
                                         per_head)
            landed = _scatter_wait(*in_flight[:4], small_tot)
            partials[(l, "w_in")], arrived[(l, "w_in")] = [landed[0]], [landed[1]]
        else:
            dh, dng, halves = _inproj_dx(du, s["w_in"], s["h"], row("norm_g"), dh1, grads=tail + [dwin])
            tail_p = _sum_pair(core, tail + [dwin], halves)
            partials[(l, "tail")], partials[(l, "w_in")] = tail_p[:-1], tail_p[-1:]
            pending = [(l, "tail"), (l, "w_in")]
            small[l] = dict(norm_g=dng, attn_out_g=dag, dw_b=ddb, conv_ln_g=dlg, conv_ln_b=dlb,
                            conv_out_g=dcg, ple_norm_g=dpg)
    both = lambda d: [list(d[(l, "w_in")]) + list(d[(l, "tail")]) for l in range(DEPTH)]
    big = dict(zip(BIG, _sum_chips_share(chip_idx, both(partials), both(arrived))))
    return dh, big, small_tot


BIG = ("w_in", "w_out", "w_ple_gate", "w_ple", "w_pw", "dw_w")
SMALL2 = ("norm_g", "ple_norm_g", "dw_b", "conv_ln_g", "conv_ln_b", "conv_out_g", "attn_out_g")
SMALL_ROW = (0, 2, 4, 6, 8, 10, 12, 14)


def kernel(x, p, norm_g, w_in, attn_out_g, dw_w, dw_b, conv_ln_g, conv_ln_b, w_pw, conv_out_g, w_out, ple_norm_g, w_ple_gate, w_ple, final_g, loss_target, m_norm_g, m_w_in, m_attn_out_g, m_dw_w, m_dw_b, m_conv_ln_g, m_conv_ln_b, m_w_pw, m_conv_out_g, m_w_out, m_ple_norm_g, m_w_ple_gate, m_w_ple, m_final_g, v_norm_g, v_w_in, v_attn_out_g, v_dw_w, v_dw_b, v_conv_ln_g, v_conv_ln_b, v_w_pw, v_conv_out_g, v_w_out, v_ple_norm_g, v_w_ple_gate, v_w_ple, v_final_g):
    W = dict(norm_g=norm_g, w_in=w_in, attn_out_g=attn_out_g, dw_w=dw_w, dw_b=dw_b, conv_ln_g=conv_ln_g,
             conv_ln_b=conv_ln_b, w_pw=w_pw, conv_out_g=conv_out_g, w_out=w_out, ple_norm_g=ple_norm_g,
             w_ple_gate=w_ple_gate, w_ple=w_ple, final_g=final_g)
    M = dict(norm_g=m_norm_g, w_in=m_w_in, attn_out_g=m_attn_out_g, dw_w=m_dw_w, dw_b=m_dw_b,
             conv_ln_g=m_conv_ln_g, conv_ln_b=m_conv_ln_b, w_pw=m_w_pw, conv_out_g=m_conv_out_g, w_out=m_w_out,
             ple_norm_g=m_ple_norm_g, w_ple_gate=m_w_ple_gate, w_ple=m_w_ple, final_g=m_final_g)
    V = dict(norm_g=v_norm_g, w_in=v_w_in, attn_out_g=v_attn_out_g, dw_w=v_dw_w, dw_b=v_dw_b,
             conv_ln_g=v_conv_ln_g, conv_ln_b=v_conv_ln_b, w_pw=v_w_pw, conv_out_g=v_conv_out_g, w_out=v_w_out,
             ple_norm_g=v_ple_norm_g, w_ple_gate=v_w_ple_gate, w_ple=v_w_ple, final_g=v_final_g)
    order = ("norm_g", "w_in", "attn_out_g", "dw_w", "dw_b", "conv_ln_g", "conv_ln_b", "w_pw", "conv_out_g",
             "w_out", "ple_norm_g", "w_ple_gate", "w_ple", "final_g")

    pad_taps = lambda a: jnp.pad(a, ((0, 0), (0, CWP - CW), (0, 0)))
    cast = dict(w_in=w_in.astype(BF16), w_out=w_out.astype(BF16), w_ple_gate=w_ple_gate.astype(BF16),
                w_ple=w_ple.astype(BF16), w_pw=w_pw.astype(BF16), dw_w=pad_taps(dw_w))
    shards = [{k: v[l] for k, v in cast.items()} for l in range(DEPTH)]
    xi, yi, ci = lax.axis_index("x"), lax.axis_index("y"), lax.axis_index("c")
    chip = 2 * xi + yi

    sm = {k: W[k] for k in SMALL2}
    sm["final_g"] = final_g.reshape(1, D)
    grad_x, big, small_tot = _local_step(x[0], p[:, 0], loss_target[0], sm, shards, chip, ci)
    g_big = {name: big[name].reshape(cast[name].shape) for name in BIG}

    small_names = SMALL2 + ("final_g",)
    as_rows = lambda t: t.reshape(1, D) if t.ndim == 1 else t
    results, loss_row = _small_adamw(
        small_tot, [as_rows(W[k]) for k in small_names],
        [as_rows(M[k]) for k in small_names], [as_rows(V[k]) for k in small_names])
    loss = loss_row[0, 0]

    grads, deltas, new_m, new_v = {}, {}, {}, {}
    for name in BIG:
        if name == "dw_w":
            grads[name], deltas[name], new_m[name], new_v[name] = _adamw_taps(W[name], g_big[name], M[name], V[name])
            continue
        cols = W[name].shape[-1]
        rows_total = W[name].size // cols
        flat = lambda a: a.reshape(rows_total, cols)
        four = _adamw(flat(W[name]), flat(g_big[name]), flat(M[name]), flat(V[name]), min(rows_total, 256))
        grads[name], deltas[name], new_m[name], new_v[name] = (t.reshape(W[name].shape) for t in four)
    for k, four in zip(small_names, results):
        grads[k], deltas[k], new_m[k], new_v[k] = (t.reshape(W[k].shape) for t in four)

    return (loss, grad_x[None], *[grads[n] for n in order], *[deltas[n] for n in order],
            *[new_m[n] for n in order], *[new_v[n] for n in order])
```

```python
import functools

import jax
import jax.numpy as jnp
from jax import lax
from jax.experimental import pallas as pl
from jax.experimental.pallas import tpu as pltpu

F32 = jnp.float32
BF16 = jnp.bfloat16

T = 2048
D = 1024
DIN = 3584
NCHIP = 4
SHW = DIN // NCHIP
AD = 512
CD = 512
DH = 64
CW = 31
CWP = 32
PLE = 256
DEPTH = 2
EPS = 1e-6
AQ = 256
HG = 4
GW = HG * DH
SR = HG * AQ
NG = AD // GW
LANE = 128
TM = 1024
TR = 256

ADAM_LR = 0.001
ADAM_B1 = 0.9
ADAM_B2 = 0.999
ADAM_EPS = 1e-08
ADAM_WD = 0.01
ADAM_STEP = 10

SMALL_PK = 16
LOSS_ROW = 15

VMEM_BIG = 56 * 1024 * 1024
MESH = pl.DeviceIdType.MESH


def _cp(sem=None, vmem=None):
    kw = {}
    if sem is not None:
        kw["dimension_semantics"] = sem
    if vmem is not None:
        kw["vmem_limit_bytes"] = vmem
    return pltpu.CompilerParams(**kw)


def _dot(a, b):
    return jnp.dot(a, b, preferred_element_type=F32)


def _dot_nt(a, b):
    return lax.dot_general(a, b, (((1,), (1,)), ((), ())), preferred_element_type=F32)


def _dot_tn(a, b):
    return lax.dot_general(a, b, (((0,), (0,)), ((), ())), preferred_element_type=F32)


def _dot2(x, m):
    hi = x.astype(BF16)
    lo = (x - hi.astype(F32)).astype(BF16)
    return _dot(hi, m) + _dot(lo, m)


def _sig(x):
    return 1.0 / (1.0 + jnp.exp(-x))


def _softplus(z):
    return jnp.maximum(z, 0.0) + jnp.log(1.0 + jnp.exp(-jnp.abs(z)))


def _rstd(x):
    return lax.rsqrt(jnp.mean(x * x, axis=-1, keepdims=True) + EPS)


def _rms_bwd(dy, x, r, g):
    dn = dy * g
    return r * dn - x * (r * r * r) * jnp.mean(dn * x, axis=-1, keepdims=True)


def _rms_inproj(h, g, w):
    tm = T

    def body(h_ref, g_ref, w_ref, u_ref, hn_ref, hn_s):
        @pl.when(pl.program_id(1) == 0)
        def _():
            hh = h_ref[...]
            hn = (hh * _rstd(hh) * g_ref[...]).astype(BF16)
            hn_s[...] = hn
            hn_ref[...] = hn
        u_ref[...] = _dot(hn_s[...], w_ref[0])

    return pl.pallas_call(
        body, name="rms_inproj", grid=(T // tm, NCHIP),
        in_specs=[pl.BlockSpec((tm, D), lambda i, k: (i, 0)),
                  pl.BlockSpec((1, D), lambda i, k: (0, 0)),
                  pl.BlockSpec((1, D, SHW), lambda i, k: (k, 0, 0))],
        out_specs=[pl.BlockSpec((tm, SHW), lambda i, k: (i, k)),
                   pl.BlockSpec((tm, D), lambda i, k: (i, 0))],
        out_shape=[jax.ShapeDtypeStruct((T, DIN), F32), jax.ShapeDtypeStruct((T, D), BF16)],
        scratch_shapes=[pltpu.VMEM((tm, D), BF16)],
        compiler_params=_cp(("arbitrary", "arbitrary"), VMEM_BIG),
    )(h, g, w)


def _attn_tiles():
    row = lax.broadcasted_iota(jnp.int32, (SR, AQ), 0) & (AQ - 1)
    col = lax.broadcasted_iota(jnp.int32, (SR, AQ), 1)
    tr = lax.broadcasted_iota(jnp.int32, (AQ, AQ), 0)
    tc = lax.broadcasted_iota(jnp.int32, (AQ, AQ), 1)
    lane_head = lax.broadcasted_iota(jnp.int32, (1, GW), 1) // DH
    return col < row, tr, tc, [lane_head == h for h in range(HG)]


def _stack_heads(t, heads):
    return jnp.concatenate([jnp.where(m, t, 0.0) for m in heads], axis=0)


def _unstack_heads(t, heads):
    out = t[:AQ]
    for h in range(1, HG):
        out = jnp.where(heads[h], t[h * AQ:(h + 1) * AQ], out)
    return out


def _tri_sum(x, tri):
    hi = x.astype(BF16)
    lo = (x - hi.astype(F32)).astype(BF16)
    both = _dot(jnp.concatenate([hi, lo], axis=0), tri)
    return both[:SR] + both[SR:]


def _scatter_copies(ps, gots, send_sems, recv_sems):
    x, y, c = _place()
    peers = [(1 - x, y), (x, 1 - y), (1 - x, 1 - y)]
    return [pltpu.make_async_remote_copy(
        src_ref=ps[a].at[2 * px + py], dst_ref=gots[a].at[r], send_sem=send_sems.at[3 * a + r],
        recv_sem=recv_sems.at[3 * a + r], device_id=(px, py, c), device_id_type=MESH)
        for a in range(len(ps)) for r, (px, py) in enumerate(peers)]


GATHER_SEMS = 8


def _gather_tree_copies(ins, outs, send_sems, recv_sems):
    x, y, c = _place()
    me, xn, yn, dg = 2 * x + y, 2 * (1 - x) + y, 2 * x + (1 - y), 2 * (1 - x) + (1 - y)
    to_x, to_y, sibling = (1 - x, y, c), (x, 1 - y, c), (x, y, 1 - c)
    direct, relayed, passed = [], [], []
    for a in range(len(ins)):
        half = ins[a].shape[0] // 2
        mine = pl.ds(c * half, half)
        first, second = pl.ds(c * half, half // 2), pl.ds(c * half + half // 2, half // 2)

        def copy(i, src, dst, to, k=GATHER_SEMS * a):
            return pltpu.make_async_remote_copy(src_ref=src, dst_ref=dst, send_sem=send_sems.at[k + i],
                                                recv_sem=recv_sems.at[k + i], device_id=to, device_id_type=MESH)

        own, slot = ins[a].at[mine], outs[a].at[me, mine]
        direct += [copy(0, own, slot, to_x), copy(1, own, slot, to_y), copy(7, ins[a], outs[a].at[me], sibling)]
        relayed += [copy(2, outs[a].at[xn, first], outs[a].at[xn, first], to_y),
                    copy(3, outs[a].at[yn, second], outs[a].at[yn, second], to_x)]
        passed += [copy(4 + i, outs[a].at[j, mine], outs[a].at[j, mine], sibling) for i, j in enumerate((xn, yn, dg))]
    return direct, relayed, passed


def _pair_copies(ins, outs, send_sems, recv_sems):
    x, y, c = _place()
    copies = []
    for a in range(len(ins)):
        half = ins[a].shape[1] // 2
        copies.append(pltpu.make_async_remote_copy(
            src_ref=ins[a].at[:, pl.ds((1 - c) * half, half), :], dst_ref=outs[a], send_sem=send_sems.at[a],
            recv_sem=recv_sems.at[a], device_id=(x, y, 1 - c), device_id_type=MESH))
    return copies


def _host(body, grid, n_in, n_out, n_x, make_copies, mids=()):
    if not n_x:
        return body

    def hosting(*refs):
        a, b = n_in + n_x, n_in + 2 * n_x + n_out
        copies = make_copies(refs[n_in:a], refs[a + n_out:b], refs[-2], refs[-1])
        stages = copies if isinstance(copies, tuple) else (copies,)
        ids = [pl.program_id(d) for d in range(len(grid))]
        at = lambda step: functools.reduce(jnp.logical_and, [i == s for i, s in zip(ids, step)])

        @pl.when(at([0] * len(grid)))
        def _():
            for cp in stages[0]:
                cp.start()

        for before, after, step in zip(stages, stages[1:], mids):
            @pl.when(at(step))
            def _(before=before, after=after):
                for cp in before:
                    cp.wait_recv()
                for cp in after:
                    cp.start()

        body(*refs[:n_in], *refs[a:a + n_out], *refs[b:-2])

        @pl.when(at([g - 1 for g in grid]))
        def _():
            for cp in stages[-1]:
                cp.wait_recv()
            for stage in stages:
                for cp in stage:
                    cp.wait_send()

    return hosting


def _hosted_sems(n_x, per_array=3):
    n = per_array * n_x
    return [pltpu.SemaphoreType.DMA((n,)), pltpu.SemaphoreType.DMA((n,))] if n_x else []


RC = 256


def _chunk_causal(r):
    row = lax.broadcasted_iota(jnp.int32, (RC, AQ), 0) + (r * RC) % AQ
    return lax.broadcasted_iota(jnp.int32, (RC, AQ), 1) < row


def _attn_fwd(u, agw, shards=()):
    n = len(shards)
    grid = (T // AQ,)

    def body(q_ref, k_ref, v_ref, g_ref, ag_ref, o_ref, y_ref, tot_ref,
             kb_s, vb_s, qs_s, z_s, zs_s, lmb_s, suf_s, att_s, acc_s, run_s):
        qi = pl.program_id(0)

        @pl.when(qi == 0)
        def _():
            kb_s[...] = k_ref[...].astype(BF16)
            vb_s[...] = v_ref[...].astype(BF16)

        _, tr, tc, heads = _attn_tiles()
        upper = (tr > tc).astype(BF16)
        same_head = ((tr // DH) == (tc // DH)).astype(BF16)
        for g in range(NG):
            qs_s[g] = _stack_heads(q_ref[:, g * GW:(g + 1) * GW] * 0.125, heads).astype(BF16)
        acc_s[...] = jnp.zeros_like(acc_s)
        run_s[...] = jnp.zeros_like(run_s)

        def block(kb, masked):
            k0 = pl.multiple_of(kb * AQ, AQ)
            for g in range(NG):
                lanes = pl.ds(g * GW, GW)
                z_s[g] = _dot_nt(qs_s[g], kb_s[pl.ds(k0, AQ), lanes])
                for r in range(SR // RC):
                    rows = pl.ds(r * RC, RC)
                    z = z_s[g, rows, :]
                    zs = jnp.minimum(z, 0.0) - jnp.log(1.0 + jnp.exp(-jnp.abs(z)))
                    lm = zs - z
                    if masked:
                        lm = jnp.where(_chunk_causal(r), lm, 0.0)
                    run = run_s[g, rows, :]
                    zs_s[g, rows, :] = zs + run[:, 0:1]
                    hi = lm.astype(BF16)
                    lmb_s[g, rows, :] = hi
                    lmb_s[g, pl.ds(SR + r * RC, RC), :] = (lm - hi.astype(F32)).astype(BF16)
                    run_s[g, rows, :] = run + jnp.sum(lm, axis=1, keepdims=True)
                suf_s[g] = _dot(lmb_s[g], upper)
                for r in range(SR // RC):
                    rows = pl.ds(r * RC, RC)
                    att = jnp.exp(zs_s[g, rows, :] + suf_s[g, rows, :] + suf_s[g, pl.ds(SR + r * RC, RC), :])
                    if masked:
                        att = jnp.where(_chunk_causal(r), att, 0.0)
                    att_s[g, rows, :] = att.astype(BF16)
                acc_s[g] += _dot(att_s[g], vb_s[pl.ds(k0, AQ), lanes])

        block(qi, True)

        def step(i, c):
            block(qi - 1 - i, False)
            return c

        lax.fori_loop(0, qi, step, 0)
        gate = g_ref[...]
        agv = ag_ref[...]
        for g in range(NG):
            lanes = slice(g * GW, (g + 1) * GW)
            o = _unstack_heads(acc_s[g], heads)
            osq = o * o
            ms = _dot2(osq, same_head)
            gg = gate[:, lanes]
            o_ref[:, lanes] = o
            y_ref[:, lanes] = (o * lax.rsqrt(ms * (1.0 / DH) + EPS) * agv[:, lanes] * (gg * _sig(gg))).astype(BF16)
            tot_ref[g] = _unstack_heads(jnp.broadcast_to(run_s[g][:, 0:1], (SR, GW)), heads)

    tile = lambda dt, rows=SR: pltpu.VMEM((NG, rows, AQ), dt)
    scratch = [pltpu.VMEM((T, AD), BF16), pltpu.VMEM((T, AD), BF16), pltpu.VMEM((NG, SR, GW), BF16),
               tile(F32), tile(F32), tile(BF16, 2 * SR), tile(F32, 2 * SR), tile(BF16),
               pltpu.VMEM((NG, SR, GW), F32), pltpu.VMEM((NG, SR, LANE), F32)]
    col = lambda j: pl.BlockSpec((AQ, AD), lambda qi: (qi, j))
    res = pl.pallas_call(
        _host(body, grid, 5, 3, n, _gather_tree_copies, mids=((grid[0] * 5 // 8,), (grid[0] * 7 // 8,))), name="attn_fwd", grid=grid,
        in_specs=[col(0), pl.BlockSpec((T, AD), lambda qi: (0, 1)), pl.BlockSpec((T, AD), lambda qi: (0, 2)),
                  col(3), pl.BlockSpec((1, AD), lambda qi: (0, 0))] + [HBM_SPEC] * n,
        out_specs=[col(0), col(0), pl.BlockSpec((NG, AQ, GW), lambda qi: (0, qi, 0))] + [HBM_SPEC] * n,
        out_shape=[jax.ShapeDtypeStruct((T, AD), F32), jax.ShapeDtypeStruct((T, AD), BF16),
                   jax.ShapeDtypeStruct((NG, T, GW), F32)]
        + [jax.ShapeDtypeStruct((NCHIP,) + s.shape, s.dtype) for s in shards],
        scratch_shapes=scratch + _hosted_sems(n, GATHER_SEMS),
        compiler_params=_cp(("arbitrary",), VMEM_BIG),
    )(u, u, u, u, agw, *shards)
    return res[0], res[1], res[2], list(res[3:])


def _glu_conv(u, dw, db):
    tr = 256

    def body(cv_ref, cg_ref, w_ref, b_ref, c1_ref, pad_s):
        pad_s[pl.ds(0, CWP), :] = jnp.zeros((CWP, LANE), F32)
        pad_s[pl.ds(CWP, T), :] = cv_ref[...] * _sig(cg_ref[...])
        wv = w_ref[0]
        bias = b_ref[...]

        def tile(i, carry):
            r0 = pl.multiple_of(i * tr, tr)
            acc = jnp.zeros((tr, LANE), F32) + bias
            for w in range(CW):
                acc = acc + pad_s[pl.ds(r0 + (CWP - CW + 1) + w, tr), :] * wv[w:w + 1, :]
            c1_ref[pl.ds(r0, tr), :] = acc
            return carry

        lax.fori_loop(0, T // tr, tile, 0)

    return pl.pallas_call(
        body, name="glu_conv", grid=(CD // LANE,),
        in_specs=[pl.BlockSpec((T, LANE), lambda cb: (0, 16 + cb)),
                  pl.BlockSpec((T, LANE), lambda cb: (0, 20 + cb)),
                  pl.BlockSpec((1, CWP, LANE), lambda cb: (cb, 0, 0)),
                  pl.BlockSpec((1, LANE), lambda cb: (0, cb))],
        out_specs=pl.BlockSpec((T, LANE), lambda cb: (0, cb)),
        out_shape=jax.ShapeDtypeStruct((T, CD), F32),
        scratch_shapes=[pltpu.VMEM((T + CWP, LANE), F32)],
        compiler_params=_cp(("arbitrary",)),
    )(u, u, dw, db)


def _ln_silu(c1, lg, lb):
    mu = jnp.mean(c1, axis=-1, keepdims=True)
    xc = c1 - mu
    rs = lax.rsqrt(jnp.mean(xc * xc, axis=-1, keepdims=True) + EPS)
    xh = xc * rs
    ln = xh * lg + lb
    s = _sig(ln)
    return xh, rs, ln, s


def _layer_tail(c1, u, ya, h, p, lg, lb, wpw, cg, wout, pg, wgate, wple, head=None):
    tm = min(2 * TR, T)

    def body(c1_ref, gc_ref, ya_ref, h_ref, p_ref, lg_ref, lb_ref, wpw_ref, cg_ref, wout_ref,
             pg_ref, wgate_ref, wple_ref, *rest):
        c3_ref, yc_ref, h1_ref, gate_ref, pe_ref, h2_ref = rest[-6 - 2 * bool(head):][:6]
        _, _, ln, s = _ln_silu(c1_ref[...], lg_ref[...], lb_ref[...])
        c2 = (ln * s).astype(BF16)
        c3 = _dot(c2, wpw_ref[...])
        gc = gc_ref[...]
        yc = (c3 * _rstd(c3) * cg_ref[...] * (gc * _sig(gc))).astype(BF16)
        c3_ref[...] = c3
        yc_ref[...] = yc
        y = _dot(ya_ref[...], wout_ref[pl.ds(0, AD), :]) + _dot(yc, wout_ref[pl.ds(AD, CD), :])
        h1 = h_ref[...] + y
        hn2 = (h1 * _rstd(h1) * pg_ref[...]).astype(BF16)
        gate = _sig(_dot(hn2, wgate_ref[...]))
        pb = p_ref[...].astype(BF16)
        pe = jnp.concatenate([_dot(pb, wple_ref[k]) for k in range(NCHIP)], axis=1)
        h1_ref[...] = h1
        gate_ref[...] = gate.astype(BF16)
        pe_ref[...] = pe.astype(BF16)
        h2 = h1 + pe * gate
        if not head:
            h2_ref[...] = h2
            return
        t_ref, fg_ref, loss_ref, dfg_ref = rest[0], rest[1], rest[-2], rest[-1]

        @pl.when(pl.program_id(0) == 0)
        def _():
            loss_ref[...] = jnp.zeros_like(loss_ref)
            dfg_ref[...] = jnp.zeros_like(dfg_ref)
        fg = fg_ref[...]
        r = _rstd(h2)
        e = h2 * r * fg - t_ref[...]
        loss_ref[...] += 0.5 * jnp.sum(jnp.mean(e * e, axis=-1, keepdims=True))
        dy = e * (1.0 / D)
        dfg_ref[...] += jnp.sum(dy * h2 * r, axis=0, keepdims=True)
        h2_ref[...] = _rms_bwd(dy, h2, r, fg)

    row = lambda w: pl.BlockSpec((tm, w), lambda i: (i, 0))
    full = lambda *s: pl.BlockSpec(s, lambda i: (0,) * len(s), pipeline_mode=pl.Buffered(1))
    extra = bool(head)
    return pl.pallas_call(
        body, name="layer_tail", grid=(T // tm,),
        in_specs=[row(CD), pl.BlockSpec((tm, CD), lambda i: (i, 6)), row(AD), row(D), row(PLE),
                  full(1, CD), full(1, CD), full(CD, CD), full(1, CD), full(D, D),
                  full(1, D), full(D, D), full(NCHIP, PLE, PLE)] + [row(D), full(1, D)] * extra,
        out_specs=[row(CD), row(CD), row(D), row(D), row(D), row(D)] + [full(8, LANE), full(1, D)] * extra,
        out_shape=[jax.ShapeDtypeStruct((T, CD), F32), jax.ShapeDtypeStruct((T, CD), BF16),
                   jax.ShapeDtypeStruct((T, D), F32), jax.ShapeDtypeStruct((T, D), BF16),
                   jax.ShapeDtypeStruct((T, D), BF16), jax.ShapeDtypeStruct((T, D), F32)]
        + [jax.ShapeDtypeStruct((8, LANE), F32), jax.ShapeDtypeStruct((1, D), F32)] * extra,
        compiler_params=_cp(("arbitrary",), VMEM_BIG),
    )(c1, u, ya, h, p, lg, lb, wpw, cg, wout, pg, wgate, wple, *(head or ()))


def _ple_out_bwd(dh2, h1, gate, pe, p, ya, yc, pg, wgate, wout):
    tm = min(TR, T)

    def body(dh2_ref, h1_ref, gate_ref, pe_ref, p_ref, ya_ref, yc_ref, pg_ref, wgate_ref, wout_ref,
             dh1_ref, dy_ref, dwg_ref, dwp_ref, dwo_ref, dpg_ref):
        @pl.when(pl.program_id(0) == 0)
        def _():
            dwg_ref[...] = jnp.zeros_like(dwg_ref)
            dwp_ref[...] = jnp.zeros_like(dwp_ref)
            dwo_ref[...] = jnp.zeros_like(dwo_ref)
            dpg_ref[...] = jnp.zeros_like(dpg_ref)
        dh2 = dh2_ref[...]
        h1 = h1_ref[...]
        gate = gate_ref[...].astype(F32)
        pg = pg_ref[...]
        dpe = (dh2 * gate).astype(BF16)
        dgp = (dh2 * pe_ref[...].astype(F32) * gate * (1.0 - gate)).astype(BF16)
        r = _rstd(h1)
        hn = h1 * r
        dwg_ref[...] += _dot_tn((hn * pg).astype(BF16), dgp)
        dhn2 = _dot_nt(dgp, wgate_ref[...])
        dpg_ref[...] += jnp.sum(dhn2 * hn, axis=0, keepdims=True)
        dh1 = dh2 + _rms_bwd(dhn2, h1, r, pg)
        pb = p_ref[...].astype(BF16)
        for k in range(NCHIP):
            dwp_ref[k] += _dot_tn(pb, dpe[:, k * PLE:(k + 1) * PLE])
        dh1b = dh1.astype(BF16)
        dy_ref[...] = _dot_nt(dh1b, wout_ref[...])
        dwo_ref[pl.ds(0, AD), :] += _dot_tn(ya_ref[...], dh1b)
        dwo_ref[pl.ds(AD, CD), :] += _dot_tn(yc_ref[...], dh1b)
        dh1_ref[...] = dh1

    row = lambda w: pl.BlockSpec((tm, w), lambda i: (i, 0))
    full = lambda *s: pl.BlockSpec(s, lambda i: (0,) * len(s), pipeline_mode=pl.Buffered(1))
    return pl.pallas_call(
        body, name="ple_out_bwd", grid=(T // tm,),
        in_specs=[row(D), row(D), row(D), row(D), row(PLE), row(AD), row(CD),
                  full(1, D), full(D, D), full(D, D)],
        out_specs=[row(D), row(D), full(D, D), full(NCHIP, PLE, PLE), full(D, D), full(1, D)],
        out_shape=[jax.ShapeDtypeStruct((T, D), F32), jax.ShapeDtypeStruct((T, D), F32),
                   jax.ShapeDtypeStruct((D, D), F32), jax.ShapeDtypeStruct((NCHIP, PLE, PLE), F32),
                   jax.ShapeDtypeStruct((D, D), F32), jax.ShapeDtypeStruct((1, D), F32)],
        compiler_params=_cp(("arbitrary",), VMEM_BIG),
    )(dh2, h1, gate, pe, p, ya, yc, pg, wgate, wout)


def _branch_bwd(dy, o, u, c1, c3, ag, lg, lb, wpw, cg, seg):
    tm = min(TR, T)

    def body(dya_ref, dyc_ref, o_ref, ga_ref, gc_ref, c1_ref, c3_ref, ag_ref, lg_ref, lb_ref, wpw_ref,
             cg_ref, seg_ref, do_ref, dga_ref, dgc_ref, dc1_ref, dwpw_ref, dag_ref, dcg_ref, dlg_ref, dlb_ref):
        @pl.when(pl.program_id(0) == 0)
        def _():
            for r_ in (dwpw_ref, dag_ref, dcg_ref, dlg_ref, dlb_ref):
                r_[...] = jnp.zeros_like(r_)
        dya = dya_ref[...]
        o = o_ref[...]
        ga = ga_ref[...]
        ag_v = ag_ref[...]
        seg_m = seg_ref[...]
        r = lax.rsqrt(_dot2(o * o, seg_m) * (1.0 / DH) + EPS)
        onr = o * r
        sg = _sig(ga)
        dga_ref[...] = (dya * (onr * ag_v) * (sg * (1.0 + ga * (1.0 - sg)))).astype(BF16)
        don = dya * (ga * sg)
        dag_ref[...] += jnp.sum(don * onr, axis=0, keepdims=True)
        dn = don * ag_v
        do_ref[...] = r * dn - o * (r * r * r) * (_dot2(dn * o, seg_m) * (1.0 / DH))
        dyc = dyc_ref[...]
        c3 = c3_ref[...]
        gc = gc_ref[...]
        cg_v = cg_ref[...]
        r3 = _rstd(c3)
        cn = c3 * r3
        sc = _sig(gc)
        dgc_ref[...] = (dyc * (cn * cg_v) * (sc * (1.0 + gc * (1.0 - sc)))).astype(BF16)
        dcn = dyc * (gc * sc)
        dcg_ref[...] += jnp.sum(dcn * cn, axis=0, keepdims=True)
        dc3 = _rms_bwd(dcn, c3, r3, cg_v).astype(BF16)
        lg_v = lg_ref[...]
        xh, rs, ln, s = _ln_silu(c1_ref[...], lg_v, lb_ref[...])
        c2 = (ln * s).astype(BF16)
        dwpw_ref[...] += _dot_tn(c2, dc3)
        dc2 = _dot_nt(dc3, wpw_ref[...])
        dln = dc2 * (s * (1.0 + ln * (1.0 - s)))
        dlb_ref[...] += jnp.sum(dln, axis=0, keepdims=True)
        dlg_ref[...] += jnp.sum(dln * xh, axis=0, keepdims=True)
        dxh = dln * lg_v
        dc1_ref[...] = rs * (dxh - jnp.mean(dxh, axis=-1, keepdims=True)
                             - xh * jnp.mean(dxh * xh, axis=-1, keepdims=True))

    half = lambda j: pl.BlockSpec((tm, 512), lambda i: (i, j))
    full = lambda *s: pl.BlockSpec(s, lambda i: (0,) * len(s), pipeline_mode=pl.Buffered(1))
    vec = jax.ShapeDtypeStruct((1, 512), F32)
    act = jax.ShapeDtypeStruct((T, 512), F32)
    return pl.pallas_call(
        body, name="branch_bwd", grid=(T // tm,),
        in_specs=[half(0), half(1), half(0), half(3), half(6), half(0), half(0),
                  full(1, AD), full(1, CD), full(1, CD), full(CD, CD), full(1, CD), full(AD, AD)],
        out_specs=[half(0), half(0), half(0), half(0), full(CD, CD), full(1, 512), full(1, 512),
                   full(1, 512), full(1, 512)],
        out_shape=[act, jax.ShapeDtypeStruct((T, 512), BF16), jax.ShapeDtypeStruct((T, 512), BF16), act,
                   jax.ShapeDtypeStruct((CD, CD), F32), vec, vec, vec, vec],
        compiler_params=_cp(("arbitrary",), VMEM_BIG),
    )(dy, dy, o, u, u, c1, c3, ag, lg, lb, wpw, cg, seg)


def _conv_bwd(dc1, u, dw, grads=()):
    tr = 64
    n_x = len(grads)
    grid = (CD // LANE,)
    off = CWP - CW + 1

    def body(d_ref, cv_ref, cg_ref, w_ref, dcv_ref, dcg_ref, ddw_ref, ddb_ref, padc_s, padd_s, acc_s):
        cv = cv_ref[...]
        sg = _sig(cg_ref[...])
        padc_s[pl.ds(0, CWP), :] = jnp.zeros((CWP, LANE), F32)
        padc_s[pl.ds(CWP, T), :] = cv * sg
        padd_s[pl.ds(0, T), :] = d_ref[...]
        padd_s[pl.ds(T, CWP), :] = jnp.zeros((CWP, LANE), F32)
        acc_s[...] = jnp.zeros_like(acc_s)
        wv = w_ref[0]

        def tile(i, carry):
            r0 = pl.multiple_of(i * tr, tr)
            dt = padd_s[pl.ds(r0, tr), :]
            dc0 = jnp.zeros((tr, LANE), F32)
            for w in range(CW):
                dc0 = dc0 + padd_s[pl.ds(r0 + (CW - 1) - w, tr), :] * wv[w:w + 1, :]
                prod = dt * padc_s[pl.ds(r0 + off + w, tr), :]
                acc_s[w] += jnp.sum(prod.reshape(tr // 8, 8, LANE), axis=0)
            cvt = cv_ref[pl.ds(r0, tr), :]
            sgt = _sig(cg_ref[pl.ds(r0, tr), :])
            dcv_ref[pl.ds(r0, tr), :] = (dc0 * sgt).astype(BF16)
            dcg_ref[pl.ds(r0, tr), :] = (dc0 * cvt * sgt * (1.0 - sgt)).astype(BF16)
            return carry

        lax.fori_loop(0, T // tr, tile, 0)
        ddw_ref[0] = jnp.sum(acc_s[...], axis=1)
        ddb_ref[...] = jnp.sum(d_ref[...], axis=0, keepdims=True)

    col = lambda j: pl.BlockSpec((T, LANE), lambda cb: (0, j + cb))
    res = pl.pallas_call(
        _host(body, grid, 4, 4, n_x, _pair_copies), name="conv_bwd", grid=grid,
        in_specs=[col(0), col(16), col(20), pl.BlockSpec((1, CWP, LANE), lambda cb: (cb, 0, 0))] + [HBM_SPEC] * n_x,
        out_specs=[col(0), col(0), pl.BlockSpec((1, CWP, LANE), lambda cb: (cb, 0, 0)),
                   pl.BlockSpec((1, LANE), lambda cb: (0, cb))] + [HBM_SPEC] * n_x,
        out_shape=[jax.ShapeDtypeStruct((T, CD), BF16), jax.ShapeDtypeStruct((T, CD), BF16),
                   jax.ShapeDtypeStruct((NCHIP, CWP, LANE), F32), jax.ShapeDtypeStruct((1, CD), F32)]
        + [jax.ShapeDtypeStruct((NCHIP, g.shape[1] // 2, g.shape[2]), F32) for g in grads],
        scratch_shapes=[pltpu.VMEM((T + CWP, LANE), F32), pltpu.VMEM((T + CWP, LANE), F32),
                        pltpu.VMEM((CWP, 8, LANE), F32)] + _hosted_sems(n_x),
        compiler_params=_cp(("arbitrary",)),
    )(dc1, u, u, dw, *grads)
    return res[0], res[1], res[2], res[3], list(res[4:])


def _attn_bwd(u, do, tot, partials=()):
    n_x = len(partials)
    grid = (T // AQ,)

    def body(q_ref, k_ref, v_ref, do_ref, tot_ref, dq_ref, dk_ref, dv_ref, kb_s, vb_s, dk_s, dv_s):
        qi = pl.program_id(0)

        @pl.when(qi == 0)
        def _():
            kb_s[...] = k_ref[...].astype(BF16)
            vb_s[...] = v_ref[...].astype(BF16)
            dk_s[...] = jnp.zeros_like(dk_s)
            dv_s[...] = jnp.zeros_like(dv_s)

        causal, tr, tc, heads = _attn_tiles()
        upper = (tr > tc).astype(BF16)
        lower = (tr < tc).astype(BF16)
        qs, qus, dos, tots = [], [], [], []
        for g in range(NG):
            lanes = slice(g * GW, (g + 1) * GW)
            q = q_ref[:, lanes]
            qs.append(_stack_heads(q * 0.125, heads).astype(BF16))
            qus.append(_stack_heads(q, heads).astype(BF16))
            dos.append(_stack_heads(do_ref[:, lanes], heads).astype(BF16))
            totv = tot_ref[g]
            tots.append(jnp.concatenate([totv[:, h * DH:h * DH + 1] for h in range(HG)], axis=0))

        def block(kb, carry, masked):
            k0 = pl.multiple_of(kb * AQ, AQ)
            out = []
            for g in range(NG):
                lanes = pl.ds(g * GW, GW)
                lm_left, dl_left, dq = carry[g]
                kk = kb_s[pl.ds(k0, AQ), lanes]
                vv = vb_s[pl.ds(k0, AQ), lanes]
                z = _dot_nt(qs[g], kk)
                sp = _softplus(z)
                lm = jnp.where(causal, -sp, 0.0) if masked else -sp
                lm_incl = lm_left + jnp.sum(lm, axis=1, keepdims=True)
                att = jnp.exp((z - sp) + _tri_sum(lm, upper) + (tots[g] - lm_incl))
                if masked:
                    att = jnp.where(causal, att, 0.0)
                dl = att * _dot_nt(dos[g], vv)
                dv_s[pl.ds(k0, AQ), lanes] += _dot_tn(att.astype(BF16), dos[g])
                prefix = dl_left + _tri_sum(dl, lower)
                beta = jnp.exp(z - sp)
                dz = (1.0 - beta) * dl - beta * prefix
                if masked:
                    dz = jnp.where(causal, dz, 0.0)
                dzs = (dz * 0.125).astype(BF16)
                dk_s[pl.ds(k0, AQ), lanes] += _dot_tn(dzs, qus[g])
                out.append((lm_incl, dl_left + jnp.sum(dl, axis=1, keepdims=True), dq + _dot(dzs, kk)))
            return tuple(out)

        zero = jnp.zeros((SR, 1), F32)
        init = tuple((zero, zero, jnp.zeros((SR, GW), F32)) for _ in range(NG))
        carry = lax.fori_loop(0, qi, lambda kb, c: block(kb, c, False), init)
        carry = block(qi, carry, True)
        for g in range(NG):
            dq_ref[:, g * GW:(g + 1) * GW] = _unstack_heads(carry[g][2], heads).astype(BF16)

        @pl.when(qi == grid[0] - 1)
        def _():
            dk_ref[...] = dk_s[...].astype(BF16)
            dv_ref[...] = dv_s[...].astype(BF16)

    col = lambda j: pl.BlockSpec((AQ, AD), lambda qi: (qi, j))
    whole = lambda j: pl.BlockSpec((T, AD), lambda qi: (0, j), pipeline_mode=pl.Buffered(1))
    res = pl.pallas_call(
        _host(body, grid, 5, 3, n_x, _scatter_copies), name="attn_bwd", grid=grid,
        in_specs=[col(0), whole(1), whole(2), col(0), pl.BlockSpec((NG, AQ, GW), lambda qi: (0, qi, 0))]
        + [HBM_SPEC] * n_x,
        out_specs=[col(0), whole(0), whole(0)] + [HBM_SPEC] * n_x,
        out_shape=[jax.ShapeDtypeStruct((T, AD), BF16)] * 3
        + [jax.ShapeDtypeStruct((NCHIP - 1,) + a.shape[1:], a.dtype) for a in partials],
        scratch_shapes=[pltpu.VMEM((T, AD), BF16), pltpu.VMEM((T, AD), BF16), pltpu.VMEM((T, AD), F32),
                        pltpu.VMEM((T, AD), F32)] + _hosted_sems(n_x),
        compiler_params=_cp(("arbitrary",), VMEM_BIG),
    )(u, u, u, do, tot, *partials)
    return res[0], res[1], res[2], list(res[3:])


def _inproj_dw(hn, du):
    tm = T

    def body(hn_ref, du_ref, dw_ref):
        @pl.when(pl.program_id(1) == 0)
        def _():
            dw_ref[...] = jnp.zeros_like(dw_ref)
        dw_ref[0] += _dot_tn(hn_ref[...], du_ref[...])

    return pl.pallas_call(
        body, name="inproj_dw", grid=(NCHIP, T // tm),
        in_specs=[pl.BlockSpec((tm, D), lambda k, i: (i, 0)), pl.BlockSpec((tm, SHW), lambda k, i: (i, k))],
        out_specs=pl.BlockSpec((1, D, SHW), lambda k, i: (k, 0, 0)),
        out_shape=jax.ShapeDtypeStruct((NCHIP, D, SHW), F32),
        compiler_params=_cp(("arbitrary", "arbitrary"), VMEM_BIG),
    )(hn, du)


def _inproj_dx(du, w, h, g, dres, partials=(), grads=(), after=None):
    tm = min(TM, T)
    sent = list(partials) + list(grads)
    n_x = len(sent)
    assert after is None or not sent
    grid = (T // tm, NCHIP)
    if grads:
        landing = [jax.ShapeDtypeStruct((NCHIP, a.shape[1] // 2, a.shape[2]), F32) for a in grads]
    else:
        landing = [jax.ShapeDtypeStruct((NCHIP - 1,) + a.shape[1:], a.dtype) for a in partials]

    def body(du_ref, w_ref, h_ref, g_ref, dres_ref, dh_ref, dg_ref, acc_s):
        i, k = pl.program_id(0), pl.program_id(1)

        @pl.when(jnp.logical_and(i == 0, k == 0))
        def _():
            dg_ref[...] = jnp.zeros_like(dg_ref)

        @pl.when(k == 0)
        def _():
            acc_s[...] = _dot_nt(du_ref[...], w_ref[0])

        @pl.when(k > 0)
        def _():
            acc_s[...] += _dot_nt(du_ref[...], w_ref[0])

        @pl.when(k == NCHIP - 1)
        def _():
            hh = h_ref[...]
            r = _rstd(hh)
            dhn = acc_s[...]
            dg_ref[...] += jnp.sum(dhn * hh * r, axis=0, keepdims=True)
            dh_ref[...] = dres_ref[...] + _rms_bwd(dhn, hh, r, g_ref[...])

    behind = [] if after is None else [after]
    hosting = _host(body, grid, 5, 2, n_x, _pair_copies if grads else _scatter_copies)
    res = pl.pallas_call(
        (lambda *refs: body(*refs[:5], *refs[6:])) if behind else hosting, name="inproj_dx", grid=grid,
        in_specs=[pl.BlockSpec((tm, SHW), lambda i, k: (i, k)),
                  pl.BlockSpec((1, D, SHW), lambda i, k: (k, 0, 0)),
                  pl.BlockSpec((tm, D), lambda i, k: (i, 0)),
                  pl.BlockSpec((1, D), lambda i, k: (0, 0)),
                  pl.BlockSpec((tm, D), lambda i, k: (i, 0))] + [HBM_SPEC] * n_x
        + [pl.BlockSpec(memory_space=pl.ANY)] * len(behind),
        out_specs=[pl.BlockSpec((tm, D), lambda i, k: (i, 0)), pl.BlockSpec((1, D), lambda i, k: (0, 0))]
        + [HBM_SPEC] * n_x,
        out_shape=[jax.ShapeDtypeStruct((T, D), F32), jax.ShapeDtypeStruct((1, D), F32)] + landing,
        scratch_shapes=[pltpu.VMEM((tm, D), F32)] + _hosted_sems(n_x),
        compiler_params=_cp(("arbitrary", "arbitrary"), VMEM_BIG),
    )(du, w, h, g, dres, *sent, *behind)
    return res[0], res[1], list(res[2:])


def _sum_pair(core, grads, gots):
    n = len(grads)

    def body(c_ref, *refs):
        for a in range(n):
            refs[2 * n + a][...] = (refs[a][...] + refs[n + a][...]).astype(BF16)

    mine = [pl.BlockSpec((1,) + s.shape[1:], lambda k, c: (k, c[0], 0)) for s in gots]
    same = [pl.BlockSpec((1,) + s.shape[1:], lambda k, c: (k, 0, 0)) for s in gots]
    return pl.pallas_call(
        body, name="sum_pair",
        grid_spec=pltpu.PrefetchScalarGridSpec(
            num_scalar_prefetch=1, grid=(NCHIP,), in_specs=mine + same, out_specs=same),
        out_shape=[jax.ShapeDtypeStruct(s.shape, BF16) for s in gots],
        compiler_params=_cp(("arbitrary",), VMEM_BIG),
    )(core, *grads, *gots)


def _sum_chips_share(chip, partials, gots):
    flat_p = [p for layer in partials for p in layer]
    flat_g = [g for layer in gots for g in layer]
    n, per_layer = len(flat_p), len(partials[0])

    def body(c_ref, *refs):
        full, sums = refs[2 * n:2 * n + per_layer], refs[2 * n + per_layer:3 * n + per_layer]
        send_sems, recv_sems, local_sems = refs[3 * n + per_layer:]
        x, y, c = _place()
        copies = []
        for i in range(n):
            acc = refs[i][0].astype(F32)
            for j in range(NCHIP - 1):
                acc = acc + refs[n + i][j].astype(F32)
            sums[i][...] = acc
            half = flat_p[i].shape[1]
            rows = full[i % per_layer].at[i // per_layer, pl.ds(c * half, half)]
            copies.append(pltpu.make_async_copy(sums[i], rows, local_sems.at[i]))
            copies.append(pltpu.make_async_remote_copy(
                src_ref=sums[i], dst_ref=rows, send_sem=send_sems.at[i], recv_sem=recv_sems.at[i],
                device_id=(x, y, 1 - c), device_id_type=MESH))
        for cp in copies:
            cp.start()
        for cp in copies:
            cp.wait()

    return pl.pallas_call(
        body, name="sum_chips_share",
        grid_spec=pltpu.PrefetchScalarGridSpec(
            num_scalar_prefetch=1, grid=(1,),
            in_specs=[pl.BlockSpec((1,) + s.shape[1:], lambda i, c: (c[0], 0, 0)) for s in flat_p]
            + [pl.BlockSpec(s.shape, lambda i, c: (0, 0, 0)) for s in flat_g],
            out_specs=[HBM_SPEC] * per_layer,
            scratch_shapes=[pltpu.VMEM(s.shape[1:], F32) for s in flat_p]
            + [pltpu.SemaphoreType.DMA((n,)), pltpu.SemaphoreType.DMA((n,)), pltpu.SemaphoreType.DMA((n,))]),
        out_shape=[jax.ShapeDtypeStruct((len(partials), 2 * s.shape[1], s.shape[2]), F32) for s in partials[0]],
        compiler_params=_cp(("arbitrary",), VMEM_BIG),
    )(chip, *flat_p, *flat_g)


def _adam_math(w, g, m, v):
    nm = ADAM_B1 * m + (1.0 - ADAM_B1) * g
    nv = ADAM_B2 * v + (1.0 - ADAM_B2) * (g * g)
    m_hat = nm / (1.0 - ADAM_B1 ** ADAM_STEP)
    v_hat = nv / (1.0 - ADAM_B2 ** ADAM_STEP)
    return -ADAM_LR * (m_hat / (jnp.sqrt(v_hat) + ADAM_EPS) + ADAM_WD * w), nm, nv


def _adamw(w, g, m, v, rows):
    R, C = w.shape

    def body(w_ref, g_ref, m_ref, v_ref, g_out, d_ref, nm_ref, nv_ref):
        g = g_ref[...]
        g_out[...] = g
        d_ref[...], nm_ref[...], nv_ref[...] = _adam_math(w_ref[...], g, m_ref[...], v_ref[...])

    spec = pl.BlockSpec((rows, C), lambda i: (i, 0))
    sh = jax.ShapeDtypeStruct((R, C), F32)
    return pl.pallas_call(
        body, name="adamw", grid=(R // rows,), in_specs=[spec] * 4, out_specs=[spec] * 4,
        out_shape=[sh, sh, sh, sh], compiler_params=_cp(("arbitrary",)),
    )(w, g, m, v)


def _adamw_taps(w, g, m, v):
    def body(w_ref, g_ref, m_ref, v_ref, g_out, d_ref, nm_ref, nv_ref):
        g = g_ref[:, pl.ds(0, CW), :]
        g_out[...] = g
        d_ref[...], nm_ref[...], nv_ref[...] = _adam_math(w_ref[...], g, m_ref[...], v_ref[...])

    whole = lambda a: pl.BlockSpec(a.shape, lambda i: (0, 0, 0))
    return pl.pallas_call(
        body, name="adamw_taps", grid=(1,), in_specs=[whole(w), whole(g), whole(w), whole(w)],
        out_specs=[whole(w)] * 4, out_shape=[jax.ShapeDtypeStruct(w.shape, F32)] * 4,
    )(w, g, m, v)


def _small_adamw(tot, ws, ms, vs):
    n = len(ws)

    def body(*refs):
        tot_ref = refs[0]
        w_refs, m_refs, v_refs = refs[1:1 + n], refs[1 + n:1 + 2 * n], refs[1 + 2 * n:1 + 3 * n]
        outs = refs[1 + 3 * n:]
        for i in range(n):
            rows, width = ws[i].shape
            g = tot_ref[pl.ds(SMALL_ROW[i], rows), pl.ds(0, width)]
            outs[4 * i][...] = g
            outs[4 * i + 1][...], outs[4 * i + 2][...], outs[4 * i + 3][...] = _adam_math(
                w_refs[i][...], g, m_refs[i][...], v_refs[i][...])
        outs[4 * n][...] = tot_ref[pl.ds(LOSS_ROW, 1), pl.ds(0, LANE)]

    vmem = pl.BlockSpec(memory_space=pltpu.VMEM)
    res = pl.pallas_call(
        body, name="small_adamw", in_specs=[vmem] * (1 + 3 * n), out_specs=[vmem] * (4 * n + 1),
        out_shape=[jax.ShapeDtypeStruct(w.shape, F32) for w in ws for _ in range(4)]
        + [jax.ShapeDtypeStruct((1, LANE), F32)],
    )(tot, *ws, *ms, *vs)
    return [res[4 * i:4 * i + 4] for i in range(n)], res[4 * n]


HBM_SPEC = pl.BlockSpec(memory_space=pltpu.HBM)


def _place():
    return lax.axis_index("x"), lax.axis_index("y"), lax.axis_index("c")


def _all_gather_split(shard):
    def body(in_ref, out_ref, send_sems, recv_sems):
        direct, relayed, passed = _gather_tree_copies([in_ref], [out_ref], send_sems, recv_sems)
        for cp in direct:
            cp.start()
        for i in range(2):
            direct[i].wait_recv()
            relayed[i].start()
            passed[i].start()
        for cp in relayed:
            cp.wait_recv()
        passed[2].start()
        for cp in passed + direct[2:]:
            cp.wait_recv()
        for cp in direct + relayed + passed:
            cp.wait_send()

    return pl.pallas_call(
        body, name="all_gather_split", in_specs=[HBM_SPEC], out_specs=HBM_SPEC,
        out_shape=jax.ShapeDtypeStruct((NCHIP,) + shard.shape, shard.dtype),
        scratch_shapes=_hosted_sems(1, GATHER_SEMS),
    )(shard)


SEM_SPEC = pl.BlockSpec(memory_space=pltpu.SEMAPHORE)
ORDERED_EFFECT = pltpu.CompilerParams(has_side_effects=pltpu.SideEffectType.DATAFLOW_SIDE_EFFECTING)


def _scatter_start(partial):
    land = pltpu.with_memory_space_constraint(
        lax.empty((NCHIP - 1,) + partial.shape[1:], partial.dtype), pltpu.HBM)

    def body(p_ref, land_ref, send_sems, recv_sems, p_thru, land_thru, token):
        for cp in _scatter_copies([p_ref], [land_ref], send_sems, recv_sems):
            cp.start()
        token[...] = jnp.zeros_like(token)

    return pl.pallas_call(
        body, name="scatter_start",
        out_shape=(pltpu.SemaphoreType.DMA((NCHIP - 1,)), pltpu.SemaphoreType.DMA((NCHIP - 1,)),
                   pltpu.HBM(partial.shape, partial.dtype), pltpu.HBM(land.shape, land.dtype),
                   jax.ShapeDtypeStruct((8, LANE), F32)),
        in_specs=(HBM_SPEC, HBM_SPEC),
        out_specs=(SEM_SPEC, SEM_SPEC, HBM_SPEC, HBM_SPEC, pl.BlockSpec(memory_space=pltpu.VMEM)),
        input_output_aliases={0: 2, 1: 3}, compiler_params=ORDERED_EFFECT,
    )(pltpu.with_memory_space_constraint(partial, pltpu.HBM), land)


def _scatter_wait(send_sems, recv_sems, p_thru, land_thru, after):
    def body(p_ref, land_ref, send_sems, recv_sems, after_ref, p_dead, got_ref):
        for cp in _scatter_copies([p_ref], [land_ref], send_sems, recv_sems):
            cp.wait_send()
            cp.wait_recv()

    return pl.pallas_call(
        body, name="scatter_wait",
        out_shape=(pltpu.HBM(p_thru.shape, p_thru.dtype), pltpu.HBM(land_thru.shape, land_thru.dtype)),
        in_specs=(HBM_SPEC, HBM_SPEC, SEM_SPEC, SEM_SPEC, pl.BlockSpec(memory_space=pl.ANY)),
        out_specs=(HBM_SPEC, HBM_SPEC), input_output_aliases={0: 0, 1: 1}, compiler_params=ORDERED_EFFECT,
    )(p_thru, land_thru, send_sems, recv_sems, after)


def _pair_exchange(grads):
    n = len(grads)

    def body(*refs):
        copies = _pair_copies(refs[:n], refs[n:2 * n], refs[2 * n], refs[2 * n + 1])
        for cp in copies:
            cp.start()
        for cp in copies:
            cp.wait()

    return pl.pallas_call(
        body, name="pair_exchange", in_specs=[HBM_SPEC] * n, out_specs=[HBM_SPEC] * n,
        out_shape=[jax.ShapeDtypeStruct((NCHIP, g.shape[1] // 2, g.shape[2]), F32) for g in grads],
        scratch_shapes=[pltpu.SemaphoreType.DMA((n,)), pltpu.SemaphoreType.DMA((n,))],
    )(*grads)


def _small_allreduce(rows, loss_blk, per_head=()):
    n = len(rows)

    def body(*refs):
        loss_ref, o_ref, pk, slots, send_sems, recv_sems = refs[n:]
        pk[...] = jnp.zeros_like(pk)
        for i in range(n):
            row = refs[i][...]
            if i in per_head:
                row = functools.reduce(jnp.add, [row[:, h * DH:(h + 1) * DH] for h in range(AD // DH)])
            pk[pl.ds(i, 1), pl.ds(0, row.shape[1])] = row
        pk[pl.ds(LOSS_ROW, 1), pl.ds(0, LANE)] = loss_ref[pl.ds(0, 1), :]
        x, y, c = _place()
        me = 4 * x + 2 * y + c
        slots[me] = pk[...]
        copies = []
        for r in range(1, 8):
            rx, ry, rc = (r >> 2) & 1, (r >> 1) & 1, r & 1
            peer = (x + rx - 2 * x * rx, y + ry - 2 * y * ry, c + rc - 2 * c * rc)
            cp = pltpu.make_async_remote_copy(
                src_ref=pk, dst_ref=slots.at[me], send_sem=send_sems.at[r - 1], recv_sem=recv_sems.at[r - 1],
                device_id=peer, device_id_type=MESH)
            cp.start()
            copies.append(cp)
        for cp in copies:
            cp.wait()
        acc = slots[0]
        for j in range(1, 8):
            acc = acc + slots[j]
        o_ref[...] = acc

    vmem = pl.BlockSpec(memory_space=pltpu.VMEM)
    return pl.pallas_call(
        body, name="small_allreduce", in_specs=[vmem] * (n + 1), out_specs=vmem,
        out_shape=jax.ShapeDtypeStruct((SMALL_PK, D), F32),
        scratch_shapes=[pltpu.VMEM((SMALL_PK, D), F32), pltpu.VMEM((8, SMALL_PK, D), F32),
                        pltpu.SemaphoreType.DMA((7,)), pltpu.SemaphoreType.DMA((7,))],
    )(*rows, loss_blk)


def _seg_matrix():
    i = lax.broadcasted_iota(jnp.int32, (AD, AD), 0) // DH
    j = lax.broadcasted_iota(jnp.int32, (AD, AD), 1) // DH
    return (i == j).astype(BF16)


TAIL = ("w_out", "w_ple_gate", "w_ple", "w_pw", "dw_w")


def _local_step(x, p, tgt, sm, shards, chip, ci):
    seg = _seg_matrix()
    core = jnp.reshape(ci, (1,)).astype(jnp.int32)
    chip_idx = jnp.reshape(chip, (1,)).astype(jnp.int32)
    w_in_next = _all_gather_split(shards[0]["w_in"])
    h = x
    saved = []
    for l in range(DEPTH):
        w_in = w_in_next
        row = lambda name: sm[name][l:l + 1]
        u, hn = _rms_inproj(h, row("norm_g"), w_in)
        todo = [shards[l][k] for k in TAIL] + ([shards[l + 1]["w_in"]] if l + 1 < DEPTH else [])
        o, ya, tot, got = _attn_fwd(u, jnp.tile(row("attn_out_g"), (1, AD // DH)), todo)
        w_out = got[0].reshape(D, D)
        w_gate = got[1].reshape(D, D)
        w_ple = got[2]
        w_pw = got[3].reshape(CD, CD)
        dw = got[4]
        if l + 1 < DEPTH:
            w_in_next = got[5]
        c1 = _glu_conv(u, dw, row("dw_b"))
        c3, yc, h1, gate, pe, h2, *at_end = _layer_tail(
            c1, u, ya, h, p[l], row("conv_ln_g"), row("conv_ln_b"), w_pw, row("conv_out_g"), w_out,
            row("ple_norm_g"), w_gate, w_ple, head=(tgt, sm["final_g"]) if l == DEPTH - 1 else None)
        saved.append(dict(h=h, u=u, hn=hn, o=o, ya=ya, tot=tot, c1=c1, c3=c3, yc=yc, h1=h1, gate=gate, pe=pe,
                          w_in=w_in, w_out=w_out, w_gate=w_gate, w_pw=w_pw, dw=dw))
        h = h2
    dh, (loss_blk, dfg) = h, at_end
    small = [None] * DEPTH
    pending, partials, arrived = [], {}, {}
    pair_sum = lambda grads: _sum_pair(core, grads, _pair_exchange(grads))
    for l in reversed(range(DEPTH)):
        s = saved[l]
        row = lambda name: sm[name][l:l + 1]
        dh1, dy, dwg, dwp, dwo, dpg = _ple_out_bwd(
            dh, s["h1"], s["gate"], s["pe"], p[l], s["ya"], s["yc"], row("ple_norm_g"), s["w_gate"], s["w_out"])
        ag_t = jnp.tile(row("attn_out_g"), (1, AD // DH))
        do, dga, dgc, dc1, dwpw, dag, dcg, dlg, dlb = _branch_bwd(
            dy, s["o"], s["u"], s["c1"], s["c3"], ag_t, row("conv_ln_g"), row("conv_ln_b"), s["w_pw"],
            row("conv_out_g"), seg)
        tail = [dwo.reshape(NCHIP, 256, D), dwg.reshape(NCHIP, 256, D), dwp, dwpw.reshape(NCHIP, 128, CD)]
        dcv, dcgate, ddw, ddb, halves = _conv_bwd(dc1, s["u"], s["dw"], tail if l == 0 else ())
        tail.append(ddw)
        if l == 0:
            partials[(l, "tail")] = _sum_pair(core, tail, halves + list(_pair_exchange([ddw])))
            pending.append((l, "tail"))
        send = [t for key in pending for t in partials[key]]
        dq, dk, dv, got = _attn_bwd(s["u"], do, s["tot"], send)
        for key in pending:
            arrived[key], got = got[:len(partials[key])], got[len(partials[key]):]
        pending = []
        du = jnp.concatenate([dq, dk, dv, dga, dcv, dcgate, dgc], axis=1)
        dwin = _inproj_dw(s["hn"], du)
        if l == 0:
            in_flight = _scatter_start(pair_sum([dwin])[0])
            dh, dng, _ = _inproj_dx(du, s["w_in"], s["h"], row("norm_g"), dh1, after=in_flight[4])
            small[l] = dict(norm_g=dng, attn_out_g=dag, dw_b=ddb, conv_ln_g=dlg, conv_ln_b=dlb,
                            conv_out_g=dcg, ple_norm_g=dpg)
            per_head = tuple(DEPTH * SMALL2.index("attn_out_g") + j for j in range(DEPTH))
            small_tot = _small_allreduce([small[j][k] for k in SMALL2 for j in range(DEPTH)] + [dfg], loss_blk,
                                         per_head)
            landed = _scatter_wait(*in_flight[:4], small_tot)
            partials[(l, "w_in")], arrived[(l, "w_in")] = [landed[0]], [landed[1]]
        else:
            dh, dng, halves = _inproj_dx(du, s["w_in"], s["h"], row("norm_g"), dh1, grads=tail + [dwin])
            tail_p = _sum_pair(core, tail + [dwin], halves)
            partials[(l, "tail")], partials[(l, "w_in")] = tail_p[:-1], tail_p[-1:]
            pending = [(l, "tail"), (l, "w_in")]
            small[l] = dict(norm_g=dng, attn_out_g=dag, dw_b=ddb, conv_ln_g=dlg, conv_ln_b=dlb,
                            conv_out_g=dcg, ple_norm_g=dpg)
    both = lambda d: [list(d[(l, "w_in")]) + list(d[(l, "tail")]) for l in range(DEPTH)]
    big = dict(zip(BIG, _sum_chips_share(chip_idx, both(partials), both(arrived))))
    return dh, big, small_tot


BIG = ("w_in", "w_out", "w_ple_gate", "w_ple", "w_pw", "dw_w")
SMALL2 = ("norm_g", "ple_norm_g", "dw_b", "conv_ln_g", "conv_ln_b", "conv_out_g", "attn_out_g")
SMALL_ROW = (0, 2, 4, 6, 8, 10, 12, 14)


def kernel(x, p, norm_g, w_in, attn_out_g, dw_w, dw_b, conv_ln_g, conv_ln_b, w_pw, conv_out_g, w_out, ple_norm_g, w_ple_gate, w_ple, final_g, loss_target, m_norm_g, m_w_in, m_attn_out_g, m_dw_w, m_dw_b, m_conv_ln_g, m_conv_ln_b, m_w_pw, m_conv_out_g, m_w_out, m_ple_norm_g, m_w_ple_gate, m_w_ple, m_final_g, v_norm_g, v_w_in, v_attn_out_g, v_dw_w, v_dw_b, v_conv_ln_g, v_conv_ln_b, v_w_pw, v_conv_out_g, v_w_out, v_ple_norm_g, v_w_ple_gate, v_w_ple, v_final_g):
    W = dict(norm_g=norm_g, w_in=w_in, attn_out_g=attn_out_g, dw_w=dw_w, dw_b=dw_b, conv_ln_g=conv_ln_g,
             conv_ln_b=conv_ln_b, w_pw=w_pw, conv_out_g=conv_out_g, w_out=w_out, ple_norm_g=ple_norm_g,
             w_ple_gate=w_ple_gate, w_ple=w_ple, final_g=final_g)
    M = dict(norm_g=m_norm_g, w_in=m_w_in, attn_out_g=m_attn_out_g, dw_w=m_dw_w, dw_b=m_dw_b,
             conv_ln_g=m_conv_ln_g, conv_ln_b=m_conv_ln_b, w_pw=m_w_pw, conv_out_g=m_conv_out_g, w_out=m_w_out,
             ple_norm_g=m_ple_norm_g, w_ple_gate=m_w_ple_gate, w_ple=m_w_ple, final_g=m_final_g)
    V = dict(norm_g=v_norm_g, w_in=v_w_in, attn_out_g=v_attn_out_g, dw_w=v_dw_w, dw_b=v_dw_b,
             conv_ln_g=v_conv_ln_g, conv_ln_b=v_conv_ln_b, w_pw=v_w_pw, conv_out_g=v_conv_out_g, w_out=v_w_out,
             ple_norm_g=v_ple_norm_g, w_ple_gate=v_w_ple_gate, w_ple=v_w_ple, final_g=v_final_g)
    order = ("norm_g", "w_in", "attn_out_g", "dw_w", "dw_b", "conv_ln_g", "conv_ln_b", "w_pw", "conv_out_g",
             "w_out", "ple_norm_g", "w_ple_gate", "w_ple", "final_g")

    pad_taps = lambda a: jnp.pad(a, ((0, 0), (0, CWP - CW), (0, 0)))
    cast = dict(w_in=w_in.astype(BF16), w_out=w_out.astype(BF16), w_ple_gate=w_ple_gate.astype(BF16),
                w_ple=w_ple.astype(BF16), w_pw=w_pw.astype(BF16), dw_w=pad_taps(dw_w))
    shards = [{k: v[l] for k, v in cast.items()} for l in range(DEPTH)]
    xi, yi, ci = lax.axis_index("x"), lax.axis_index("y"), lax.axis_index("c")
    chip = 2 * xi + yi

    sm = {k: W[k] for k in SMALL2}
    sm["final_g"] = final_g.reshape(1, D)
    grad_x, big, small_tot = _local_step(x[0], p[:, 0], loss_target[0], sm, shards, chip, ci)
    g_big = {name: big[name].reshape(cast[name].shape) for name in BIG}

    small_names = SMALL2 + ("final_g",)
    as_rows = lambda t: t.reshape(1, D) if t.ndim == 1 else t
    results, loss_row = _small_adamw(
        small_tot, [as_rows(W[k]) for k in small_names],
        [as_rows(M[k]) for k in small_names], [as_rows(V[k]) for k in small_names])
    loss = loss_row[0, 0]

    grads, deltas, new_m, new_v = {}, {}, {}, {}
    for name in BIG:
        if name == "dw_w":
            grads[name], deltas[name], new_m[name], new_v[name] = _adamw_taps(W[name], g_big[name], M[name], V[name])
            continue
        cols = W[name].shape[-1]
        rows_total = W[name].size // cols
        flat = lambda a: a.reshape(rows_total, cols)
        four = _adamw(flat(W[name]), flat(g_big[name]), flat(M[name]), flat(V[name]), min(rows_total, 256))
        grads[name], deltas[name], new_m[name], new_v[name] = (t.reshape(W[name].shape) for t in four)
    for k, four in zip(small_names, results):
        grads[k], deltas[k], new_m[k], new_v[k] = (t.reshape(W[k].shape) for t in four)

    return (loss, grad_x[None], *[grads[n] for n in order], *[deltas[n] for n in order],
            *[new_m[n] for n in order], *[new_v[n] for n in order])
```
